```python
import jax, jax.numpy as jnp
from jax import lax
import numpy as np

D_MODEL = 1024
BATCH = 8
SEQ = 2048
DEPTH = 1

CHUNK = 64
D_CONV = 512
CONV_WIDTH = 31
N_HEADS = 8
HEAD_DIM = 64
D_ATTN = N_HEADS * HEAD_DIM
D_MIX = D_CONV + D_ATTN
Q_BLOCK = 128
D_FF = 2816
EPS = 1e-6
N_IN = 2 * D_CONV + 3 * D_ATTN + N_HEADS

kernel_name = "hybrid_conv_fox_macaron_block"


def rms_norm(x, g):
    xf = x.astype(jnp.float32)
    y = xf * lax.rsqrt(jnp.mean(xf * xf, axis=-1, keepdims=True) + EPS)
    return (y * g.astype(jnp.float32)).astype(x.dtype)


def layer_norm(x, g, b):
    xf = x.astype(jnp.float32)
    mu = jnp.mean(xf, axis=-1, keepdims=True)
    xc = xf - mu
    y = xc * lax.rsqrt(jnp.mean(xc * xc, axis=-1, keepdims=True) + EPS)
    return (y * g.astype(jnp.float32) + b.astype(jnp.float32)).astype(x.dtype)


def swiglu_ffn(h, w13, w2):
    gate, up = jnp.split(h @ w13, 2, axis=-1)
    return (jax.nn.silu(gate) * up) @ w2


def conv_module(a, g, conv_w, conv_b, ln_g, ln_b):
    u = a * jax.nn.sigmoid(g)
    u_pad = jnp.pad(u, ((0, 0), (CONV_WIDTH - 1, 0), (0, 0)))
    y = lax.conv_general_dilated(
        u_pad, conv_w[:, None, :].astype(u.dtype), window_strides=(1,), padding="VALID",
        dimension_numbers=("NWC", "WIO", "NWC"), feature_group_count=D_CONV)
    y = y + conv_b.astype(y.dtype)
    y = layer_norm(y, ln_g, ln_b)
    return jax.nn.silu(y)


def forgetting_attention(q, k, v, f_logit):
    seq = q.shape[1]
    log_f = jax.nn.log_sigmoid(f_logit.astype(jnp.float32))
    cum = jnp.cumsum(log_f, axis=1).transpose(0, 2, 1)
    scale = HEAD_DIM ** -0.5
    outs = []
    for i in range(seq // Q_BLOCK):
        q0, q1 = i * Q_BLOCK, (i + 1) * Q_BLOCK
        qb, kb, vb = q[:, q0:q1], k[:, :q1], v[:, :q1]
        s = jnp.einsum("bqhd,bkhd->bhqk", qb, kb, preferred_element_type=jnp.float32) * scale
        s = s + cum[:, :, q0:q1, None] - cum[:, :, None, :q1]
        qpos = jnp.arange(q0, q1)
        kpos = jnp.arange(q1)
        s = jnp.where(kpos[None, :] <= qpos[:, None], s, -jnp.inf)
        p = jax.nn.softmax(s, axis=-1)
        outs.append(jnp.einsum("bhqk,bkhd->bqhd", p.astype(vb.dtype), vb))
    return jnp.concatenate(outs, axis=1)


def _fwd_setup_inputs(seed: int = 0) -> dict:
    key = jax.random.key(seed)
    ks = jax.random.split(key, 24)
    f32 = jnp.float32

    def nrm(k, shape, scale):
        return jax.random.normal(k, shape, f32) * scale

    def gain(k, shape):
        return 1.0 + 0.05 * jax.random.normal(k, shape, f32)

    L = DEPTH
    return {
        "x": jax.random.normal(ks[0], (BATCH, SEQ, D_MODEL), f32),
        "ffn1_norm": gain(ks[1], (L, D_MODEL)),
        "ffn1_w13": nrm(ks[2], (L, D_MODEL, 2 * D_FF), D_MODEL ** -0.5),
        "ffn1_w2": nrm(ks[3], (L, D_FF, D_MODEL), D_FF ** -0.5),
        "mix_norm": gain(ks[4], (L, D_MODEL)),
        "w_in": nrm(ks[5], (L, D_MODEL, N_IN), D_MODEL ** -0.5),
        "conv_w": nrm(ks[6], (L, CONV_WIDTH, D_CONV), CONV_WIDTH ** -0.5),
        "conv_b": nrm(ks[7], (L, D_CONV), 0.02),
        "conv_ln_g": gain(ks[8], (L, D_CONV)),
        "conv_ln_b": nrm(ks[9], (L, D_CONV), 0.02),
        "forget_b": jax.random.uniform(ks[10], (L, N_HEADS), f32, minval=1.0, maxval=4.0),
        "out_norm_conv": gain(ks[11], (L, D_CONV)),
        "out_norm_attn": gain(ks[12], (L, D_ATTN)),
        "w_out": nrm(ks[13], (L, D_MIX, D_MODEL), D_MIX ** -0.5),
        "ffn2_norm": gain(ks[14], (L, D_MODEL)),
        "ffn2_w13": nrm(ks[15], (L, D_MODEL, 2 * D_FF), D_MODEL ** -0.5),
        "ffn2_w2": nrm(ks[16], (L, D_FF, D_MODEL), D_FF ** -0.5),
        "final_norm": gain(ks[17], (D_MODEL,)),
    }


def _fwd_reference(x, ffn1_norm, ffn1_w13, ffn1_w2, mix_norm, w_in, conv_w, conv_b, conv_ln_g,
              conv_ln_b, forget_b, out_norm_conv, out_norm_attn, w_out, ffn2_norm, ffn2_w13,
              ffn2_w2, final_norm):
    bsz, seq, _ = x.shape
    splits = [D_CONV, 2 * D_CONV, 2 * D_CONV + D_ATTN, 2 * D_CONV + 2 * D_ATTN,
              2 * D_CONV + 3 * D_ATTN]
    for l in range(DEPTH):
        x = x + 0.5 * swiglu_ffn(rms_norm(x, ffn1_norm[l]), ffn1_w13[l], ffn1_w2[l])

        h = rms_norm(x, mix_norm[l])
        proj = h @ w_in[l]
        a, g, q, k, v, fl = jnp.split(proj, splits, axis=-1)

        y_conv = conv_module(a, g, conv_w[l], conv_b[l], conv_ln_g[l], conv_ln_b[l])

        heads = (bsz, seq, N_HEADS, HEAD_DIM)
        y_attn = forgetting_attention(q.reshape(heads), k.reshape(heads), v.reshape(heads),
                                      fl + forget_b[l].astype(fl.dtype))
        y_attn = y_attn.reshape(bsz, seq, D_ATTN)

        y = jnp.concatenate([rms_norm(y_conv, out_norm_conv[l]),
                             rms_norm(y_attn, out_norm_attn[l])], axis=-1)
        x = x + y @ w_out[l]

        x = x + 0.5 * swiglu_ffn(rms_norm(x, ffn2_norm[l]), ffn2_w13[l], ffn2_w2[l])
    return rms_norm(x, final_norm)


import jax as _jax
import jax.numpy as _jnp

TWIN_FORMAT = 'train_step'
FWD_PARAMS = ['x', 'ffn1_norm', 'ffn1_w13', 'ffn1_w2', 'mix_norm', 'w_in', 'conv_w', 'conv_b', 'conv_ln_g', 'conv_ln_b', 'forget_b', 'out_norm_conv', 'out_norm_attn', 'w_out', 'ffn2_norm', 'ffn2_w13', 'ffn2_w2', 'final_norm']
TWIN_WEIGHTS = ['ffn1_norm', 'ffn1_w13', 'ffn1_w2', 'mix_norm', 'w_in', 'conv_w', 'conv_b', 'conv_ln_g', 'conv_ln_b', 'forget_b', 'out_norm_conv', 'out_norm_attn', 'w_out', 'ffn2_norm', 'ffn2_w13', 'ffn2_w2', 'final_norm']
TWIN_DIFF_INPUT = 'x'
TWIN_INPUTS = ['x', 'ffn1_norm', 'ffn1_w13', 'ffn1_w2', 'mix_norm', 'w_in', 'conv_w', 'conv_b', 'conv_ln_g', 'conv_ln_b', 'forget_b', 'out_norm_conv', 'out_norm_attn', 'w_out', 'ffn2_norm', 'ffn2_w13', 'ffn2_w2', 'final_norm', 'loss_target', 'm_ffn1_norm', 'm_ffn1_w13', 'm_ffn1_w2', 'm_mix_norm', 'm_w_in', 'm_conv_w', 'm_conv_b', 'm_conv_ln_g', 'm_conv_ln_b', 'm_forget_b', 'm_out_norm_conv', 'm_out_norm_attn', 'm_w_out', 'm_ffn2_norm', 'm_ffn2_w13', 'm_ffn2_w2', 'm_final_norm', 'v_ffn1_norm', 'v_ffn1_w13', 'v_ffn1_w2', 'v_mix_norm', 'v_w_in', 'v_conv_w', 'v_conv_b', 'v_conv_ln_g', 'v_conv_ln_b', 'v_forget_b', 'v_out_norm_conv', 'v_out_norm_attn', 'v_w_out', 'v_ffn2_norm', 'v_ffn2_w13', 'v_ffn2_w2', 'v_final_norm']
TWIN_OUTPUTS = ['loss', 'grad_x', 'grad_ffn1_norm', 'grad_ffn1_w13', 'grad_ffn1_w2', 'grad_mix_norm', 'grad_w_in', 'grad_conv_w', 'grad_conv_b', 'grad_conv_ln_g', 'grad_conv_ln_b', 'grad_forget_b', 'grad_out_norm_conv', 'grad_out_norm_attn', 'grad_w_out', 'grad_ffn2_norm', 'grad_ffn2_w13', 'grad_ffn2_w2', 'grad_final_norm', 'delta_ffn1_norm', 'delta_ffn1_w13', 'delta_ffn1_w2', 'delta_mix_norm', 'delta_w_in', 'delta_conv_w', 'delta_conv_b', 'delta_conv_ln_g', 'delta_conv_ln_b', 'delta_forget_b', 'delta_out_norm_conv', 'delta_out_norm_attn', 'delta_w_out', 'delta_ffn2_norm', 'delta_ffn2_w13', 'delta_ffn2_w2', 'delta_final_norm', 'new_m_ffn1_norm', 'new_m_ffn1_w13', 'new_m_ffn1_w2', 'new_m_mix_norm', 'new_m_w_in', 'new_m_conv_w', 'new_m_conv_b', 'new_m_conv_ln_g', 'new_m_conv_ln_b', 'new_m_forget_b', 'new_m_out_norm_conv', 'new_m_out_norm_attn', 'new_m_w_out', 'new_m_ffn2_norm', 'new_m_ffn2_w13', 'new_m_ffn2_w2', 'new_m_final_norm', 'new_v_ffn1_norm', 'new_v_ffn1_w13', 'new_v_ffn1_w2', 'new_v_mix_norm', 'new_v_w_in', 'new_v_conv_w', 'new_v_conv_b', 'new_v_conv_ln_g', 'new_v_conv_ln_b', 'new_v_forget_b', 'new_v_out_norm_conv', 'new_v_out_norm_attn', 'new_v_w_out', 'new_v_ffn2_norm', 'new_v_ffn2_w13', 'new_v_ffn2_w2', 'new_v_final_norm']
TWIN_LEAF_KINDS = {'loss': 'loss', 'grad_x': 'grad_x', 'grad_ffn1_norm': 'grad_w', 'grad_ffn1_w13': 'grad_w', 'grad_ffn1_w2': 'grad_w', 'grad_mix_norm': 'grad_w', 'grad_w_in': 'grad_w', 'grad_conv_w': 'grad_w', 'grad_conv_b': 'grad_w', 'grad_conv_ln_g': 'grad_w', 'grad_conv_ln_b': 'grad_w', 'grad_forget_b': 'grad_w', 'grad_out_norm_conv': 'grad_w', 'grad_out_norm_attn': 'grad_w', 'grad_w_out': 'grad_w', 'grad_ffn2_norm': 'grad_w', 'grad_ffn2_w13': 'grad_w', 'grad_ffn2_w2': 'grad_w', 'grad_final_norm': 'grad_w', 'delta_ffn1_norm': 'delta_w', 'delta_ffn1_w13': 'delta_w', 'delta_ffn1_w2': 'delta_w', 'delta_mix_norm': 'delta_w', 'delta_w_in': 'delta_w', 'delta_conv_w': 'delta_w', 'delta_conv_b': 'delta_w', 'delta_conv_ln_g': 'delta_w', 'delta_conv_ln_b': 'delta_w', 'delta_forget_b': 'delta_w', 'delta_out_norm_conv': 'delta_w', 'delta_out_norm_attn': 'delta_w', 'delta_w_out': 'delta_w', 'delta_ffn2_norm': 'delta_w', 'delta_ffn2_w13': 'delta_w', 'delta_ffn2_w2': 'delta_w', 'delta_final_norm': 'delta_w', 'new_m_ffn1_norm': 'new_m', 'new_m_ffn1_w13': 'new_m', 'new_m_ffn1_w2': 'new_m', 'new_m_mix_norm': 'new_m', 'new_m_w_in': 'new_m', 'new_m_conv_w': 'new_m', 'new_m_conv_b': 'new_m', 'new_m_conv_ln_g': 'new_m', 'new_m_conv_ln_b': 'new_m', 'new_m_forget_b': 'new_m', 'new_m_out_norm_conv': 'new_m', 'new_m_out_norm_attn': 'new_m', 'new_m_w_out': 'new_m', 'new_m_ffn2_norm': 'new_m', 'new_m_ffn2_w13': 'new_m', 'new_m_ffn2_w2': 'new_m', 'new_m_final_norm': 'new_m', 'new_v_ffn1_norm': 'new_v', 'new_v_ffn1_w13': 'new_v', 'new_v_ffn1_w2': 'new_v', 'new_v_mix_norm': 'new_v', 'new_v_w_in': 'new_v', 'new_v_conv_w': 'new_v', 'new_v_conv_b': 'new_v', 'new_v_conv_ln_g': 'new_v', 'new_v_conv_ln_b': 'new_v', 'new_v_forget_b': 'new_v', 'new_v_out_norm_conv': 'new_v', 'new_v_out_norm_attn': 'new_v', 'new_v_w_out': 'new_v', 'new_v_ffn2_norm': 'new_v', 'new_v_ffn2_w13': 'new_v', 'new_v_ffn2_w2': 'new_v', 'new_v_final_norm': 'new_v'}


def _forward(args):
    return _fwd_reference(*[args[k] for k in FWD_PARAMS])


def _output_shape():
    out = _jax.eval_shape(lambda: _forward(_fwd_setup_inputs(0)))
    return out.shape, out.dtype

N_MICROBATCH = 1
ADAM_LR = 0.001
ADAM_B1 = 0.9
ADAM_B2 = 0.999
ADAM_EPS = 1e-08
ADAM_WD = 0.01
ADAM_STEP = 10
PER_EXAMPLE_BATCH_AXIS = {'x': 0, 'loss_target': 0}
SHARED_INPUTS = []
_WEIGHT_DTYPES = {'ffn1_norm': _jnp.float32, 'ffn1_w13': _jnp.float32, 'ffn1_w2': _jnp.float32, 'mix_norm': _jnp.float32, 'w_in': _jnp.float32, 'conv_w': _jnp.float32, 'conv_b': _jnp.float32, 'conv_ln_g': _jnp.float32, 'conv_ln_b': _jnp.float32, 'forget_b': _jnp.float32, 'out_norm_conv': _jnp.float32, 'out_norm_attn': _jnp.float32, 'w_out': _jnp.float32, 'ffn2_norm': _jnp.float32, 'ffn2_w13': _jnp.float32, 'ffn2_w2': _jnp.float32, 'final_norm': _jnp.float32}
MOMENT_SCALE = {'ffn1_norm': 6.458663e-02, 'ffn1_w13': 2.648451e-02, 'ffn1_w2': 4.314522e-02, 'mix_norm': 1.147383e-01, 'w_in': 7.542637e-02, 'conv_w': 9.437324e-02, 'conv_b': 2.788769e-01, 'conv_ln_g': 1.452993e-01, 'conv_ln_b': 1.608622e-01, 'forget_b': 6.566169e-01, 'out_norm_conv': 1.074243e-01, 'out_norm_attn': 9.425836e-02, 'w_out': 9.627538e-02, 'ffn2_norm': 3.712195e-02, 'ffn2_w13': 1.576388e-02, 'ffn2_w2': 2.584905e-02, 'final_norm': 1.605448e+01}


def _to_microbatches(a, axis):
    t = _jnp.moveaxis(a, axis, 0)
    t = t.reshape((N_MICROBATCH, t.shape[0] // N_MICROBATCH) + t.shape[1:])
    return _jnp.moveaxis(t, 1, axis + 1)


def setup_inputs(seed: int = 0) -> dict:
    inp = _fwd_setup_inputs(seed)
    key = _jax.random.fold_in(_jax.random.key(seed), 7919)
    shape, _ = _output_shape()
    out = dict(inp)
    out["loss_target"] = _jax.random.normal(_jax.random.fold_in(key, 0), shape, _jnp.float32)
    for i, name in enumerate(TWIN_WEIGHTS):
        w = inp[name].astype(_jnp.float32)
        if MOMENT_SCALE is None:
            s = _jnp.sqrt(_jnp.mean(_jnp.square(w)) + 1e-30)
        else:
            s = MOMENT_SCALE[name]
        km, kv = _jax.random.split(_jax.random.fold_in(key, i + 1))
        out[name] = w
        out["m_" + name] = s * _jax.random.normal(km, w.shape, _jnp.float32)
        out["v_" + name] = (s * s) * _jax.random.uniform(kv, w.shape, _jnp.float32, 0.5, 1.5)
    if N_MICROBATCH > 1:
        for name, axis in PER_EXAMPLE_BATCH_AXIS.items():
            out[name] = _to_microbatches(out[name], axis)
    return {'x': out['x'], 'ffn1_norm': out['ffn1_norm'], 'ffn1_w13': out['ffn1_w13'], 'ffn1_w2': out['ffn1_w2'], 'mix_norm': out['mix_norm'], 'w_in': out['w_in'], 'conv_w': out['conv_w'], 'conv_b': out['conv_b'], 'conv_ln_g': out['conv_ln_g'], 'conv_ln_b': out['conv_ln_b'], 'forget_b': out['forget_b'], 'out_norm_conv': out['out_norm_conv'], 'out_norm_attn': out['out_norm_attn'], 'w_out': out['w_out'], 'ffn2_norm': out['ffn2_norm'], 'ffn2_w13': out['ffn2_w13'], 'ffn2_w2': out['ffn2_w2'], 'final_norm': out['final_norm'], 'loss_target': out['loss_target'], 'm_ffn1_norm': out['m_ffn1_norm'], 'm_ffn1_w13': out['m_ffn1_w13'], 'm_ffn1_w2': out['m_ffn1_w2'], 'm_mix_norm': out['m_mix_norm'], 'm_w_in': out['m_w_in'], 'm_conv_w': out['m_conv_w'], 'm_conv_b': out['m_conv_b'], 'm_conv_ln_g': out['m_conv_ln_g'], 'm_conv_ln_b': out['m_conv_ln_b'], 'm_forget_b': out['m_forget_b'], 'm_out_norm_conv': out['m_out_norm_conv'], 'm_out_norm_attn': out['m_out_norm_attn'], 'm_w_out': out['m_w_out'], 'm_ffn2_norm': out['m_ffn2_norm'], 'm_ffn2_w13': out['m_ffn2_w13'], 'm_ffn2_w2': out['m_ffn2_w2'], 'm_final_norm': out['m_final_norm'], 'v_ffn1_norm': out['v_ffn1_norm'], 'v_ffn1_w13': out['v_ffn1_w13'], 'v_ffn1_w2': out['v_ffn1_w2'], 'v_mix_norm': out['v_mix_norm'], 'v_w_in': out['v_w_in'], 'v_conv_w': out['v_conv_w'], 'v_conv_b': out['v_conv_b'], 'v_conv_ln_g': out['v_conv_ln_g'], 'v_conv_ln_b': out['v_conv_ln_b'], 'v_forget_b': out['v_forget_b'], 'v_out_norm_conv': out['v_out_norm_conv'], 'v_out_norm_attn': out['v_out_norm_attn'], 'v_w_out': out['v_w_out'], 'v_ffn2_norm': out['v_ffn2_norm'], 'v_ffn2_w13': out['v_ffn2_w13'], 'v_ffn2_w2': out['v_ffn2_w2'], 'v_final_norm': out['v_final_norm']}


def _loss(weights, diff, rest, loss_target):
    with _jax.named_scope("forward"):
        args = {**rest, TWIN_DIFF_INPUT: diff, **{k: w.astype(_WEIGHT_DTYPES[k]) for k, w in weights.items()}}
        y = _forward(args)
    with _jax.named_scope("loss_head"):
        err = _jnp.square(y.astype(_jnp.float32) - loss_target)
        return 0.5 * _jnp.sum(_jnp.mean(err, axis=-1)) if err.ndim else 0.5 * err


def _adamw(w, g, m, v):
    m = ADAM_B1 * m + (1.0 - ADAM_B1) * g
    v = ADAM_B2 * v + (1.0 - ADAM_B2) * _jnp.square(g)
    m_hat = m / (1.0 - ADAM_B1 ** ADAM_STEP)
    v_hat = v / (1.0 - ADAM_B2 ** ADAM_STEP)
    delta = -ADAM_LR * (m_hat / (_jnp.sqrt(v_hat) + ADAM_EPS) + ADAM_WD * w)
    return delta, m, v


def reference(x, ffn1_norm, ffn1_w13, ffn1_w2, mix_norm, w_in, conv_w, conv_b, conv_ln_g, conv_ln_b, forget_b, out_norm_conv, out_norm_attn, w_out, ffn2_norm, ffn2_w13, ffn2_w2, final_norm, loss_target, m_ffn1_norm, m_ffn1_w13, m_ffn1_w2, m_mix_norm, m_w_in, m_conv_w, m_conv_b, m_conv_ln_g, m_conv_ln_b, m_forget_b, m_out_norm_conv, m_out_norm_attn, m_w_out, m_ffn2_norm, m_ffn2_w13, m_ffn2_w2, m_final_norm, v_ffn1_norm, v_ffn1_w13, v_ffn1_w2, v_mix_norm, v_w_in, v_conv_w, v_conv_b, v_conv_ln_g, v_conv_ln_b, v_forget_b, v_out_norm_conv, v_out_norm_attn, v_w_out, v_ffn2_norm, v_ffn2_w13, v_ffn2_w2, v_final_norm):
    given = dict(x=x, ffn1_norm=ffn1_norm, ffn1_w13=ffn1_w13, ffn1_w2=ffn1_w2, mix_norm=mix_norm, w_in=w_in, conv_w=conv_w, conv_b=conv_b, conv_ln_g=conv_ln_g, conv_ln_b=conv_ln_b, forget_b=forget_b, out_norm_conv=out_norm_conv, out_norm_attn=out_norm_attn, w_out=w_out, ffn2_norm=ffn2_norm, ffn2_w13=ffn2_w13, ffn2_w2=ffn2_w2, final_norm=final_norm, loss_target=loss_target, m_ffn1_norm=m_ffn1_norm, m_ffn1_w13=m_ffn1_w13, m_ffn1_w2=m_ffn1_w2, m_mix_norm=m_mix_norm, m_w_in=m_w_in, m_conv_w=m_conv_w, m_conv_b=m_conv_b, m_conv_ln_g=m_conv_ln_g, m_conv_ln_b=m_conv_ln_b, m_forget_b=m_forget_b, m_out_norm_conv=m_out_norm_conv, m_out_norm_attn=m_out_norm_attn, m_w_out=m_w_out, m_ffn2_norm=m_ffn2_norm, m_ffn2_w13=m_ffn2_w13, m_ffn2_w2=m_ffn2_w2, m_final_norm=m_final_norm, v_ffn1_norm=v_ffn1_norm, v_ffn1_w13=v_ffn1_w13, v_ffn1_w2=v_ffn1_w2, v_mix_norm=v_mix_norm, v_w_in=v_w_in, v_conv_w=v_conv_w, v_conv_b=v_conv_b, v_conv_ln_g=v_conv_ln_g, v_conv_ln_b=v_conv_ln_b, v_forget_b=v_forget_b, v_out_norm_conv=v_out_norm_conv, v_out_norm_attn=v_out_norm_attn, v_w_out=v_w_out, v_ffn2_norm=v_ffn2_norm, v_ffn2_w13=v_ffn2_w13, v_ffn2_w2=v_ffn2_w2, v_final_norm=v_final_norm)
    weights = {n: given[n] for n in TWIN_WEIGHTS}
    shared = {n: given[n] for n in SHARED_INPUTS}
    per_example = {n: given[n] for n in ['x']}
    grad_fn = _jax.value_and_grad(_loss, argnums=(0, 1))

    def one_microbatch(ex, loss_target):
        ex = dict(ex)
        diff = ex.pop(TWIN_DIFF_INPUT)
        return grad_fn(weights, diff, {**shared, **ex}, loss_target)

    if N_MICROBATCH == 1:
        loss, (grad_w, grad_x) = one_microbatch(per_example, given["loss_target"])
    else:
        def body(carry, xs):
            loss_sum, grad_sum = carry
            l_k, (gw_k, gx_k) = one_microbatch(xs[0], xs[1])
            with _jax.named_scope("update"):
                return (loss_sum + l_k, _jax.tree.map(_jnp.add, grad_sum, gw_k)), gx_k

        init = (_jnp.zeros((), _jnp.float32), _jax.tree.map(_jnp.zeros_like, weights))
        (loss, grad_w), grad_x = _jax.lax.scan(body, init, (per_example, given["loss_target"]))
    with _jax.named_scope("update"):
        delta_w, new_m, new_v = {}, {}, {}
        for n in TWIN_WEIGHTS:
            delta_w[n], new_m[n], new_v[n] = _adamw(weights[n], grad_w[n], given["m_" + n], given["v_" + n])
    return (loss, grad_x, *[grad_w[n] for n in TWIN_WEIGHTS], *[delta_w[n] for n in TWIN_WEIGHTS],
            *[new_m[n] for n in TWIN_WEIGHTS], *[new_v[n] for n in TWIN_WEIGHTS])
```

```python
import functools

import jax
import jax.numpy as jnp
from jax import lax
from jax.experimental import pallas as pl
from jax.experimental.pallas import tpu as pltpu

F32 = jnp.float32
BF16 = jnp.bfloat16

D_MODEL = 1024
D_FF = 2816
FF_SHARD = D_FF // 2
D_CONV = 512
D_ATTN = 512
N_HEADS = 8
HEAD_DIM = 64
CONV_WIDTH = 31
CONV_PAD = 32
N_IN = 2 * D_CONV + 3 * D_ATTN + N_HEADS
IN_SHARD = N_IN // 4
EPS = 1e-6
N_CHIPS = 4
LANES = 128
HEAD_ROWS = 16

ADAM_LR = 0.001
ADAM_B1 = 0.9
ADAM_B2 = 0.999
ADAM_EPS = 1e-08
ADAM_WD = 0.01
ADAM_STEP = 10

VMEM_LIMIT = 56 * 1024 * 1024

_NT = (((1,), (1,)), ((), ()))
_TN = (((0,), (0,)), ((), ()))


def _dot(a, b):
    return jnp.dot(a, b, preferred_element_type=F32)


def _dot_nt(a, b):
    return lax.dot_general(a, b, _NT, preferred_element_type=F32)


def _dot_tn(a, b):
    return lax.dot_general(a, b, _TN, preferred_element_type=F32)


def _params(**kw):
    return pltpu.CompilerParams(vmem_limit_bytes=VMEM_LIMIT, **kw)


def _sigmoid(x):
    return 1.0 / (1.0 + jnp.exp(-x))


def _rms_stats(x):
    return lax.rsqrt(jnp.mean(x * x, axis=-1, keepdims=True) + EPS)


def _rms_bwd(x, r, g, dh):
    t = dh * g
    dx = r * t - x * (r * r * r) * jnp.mean(t * x, axis=-1, keepdims=True)
    return dx, dh * x * r


def _silu_grad(z, sg):
    return sg * (1.0 + z * (1.0 - sg))


def _row_spec(tm, n):
    return pl.BlockSpec((tm, n), lambda i: (i, 0))


def _full_spec(shape):
    nd = len(shape)
    return pl.BlockSpec(shape, lambda i: (0,) * nd)


_ANY = pl.BlockSpec(memory_space=pl.ANY)


def _ffn_fwd(x, g, w13s, w2, name):
    t = x.shape[0]
    tm = 256

    def body(x_ref, g_ref, w13_hbm, w2_hbm, xo_ref, h_ref, gu_ref, w13_ref, w2_ref):
        @pl.when(pl.program_id(0) == 0)
        def _():
            pltpu.sync_copy(w13_hbm, w13_ref)
            pltpu.sync_copy(w2_hbm, w2_ref)

        xv = x_ref[...]
        hb = (xv * _rms_stats(xv) * g_ref[...]).astype(BF16)
        h_ref[...] = hb
        acc = jnp.zeros((tm, D_MODEL), F32)
        for half in range(2):
            lo = half * FF_SHARD
            gate = _dot(hb, w13_ref[half])
            up = _dot(hb, w13_ref[2 + half])
            gu_ref[:, lo:lo + FF_SHARD] = gate.astype(BF16)
            gu_ref[:, D_FF + lo:D_FF + lo + FF_SHARD] = up.astype(BF16)
            a = (gate * _sigmoid(gate) * up).astype(BF16)
            acc = acc + _dot(a, w2_ref[lo:lo + FF_SHARD, :])
        xo_ref[...] = xv + 0.5 * acc

    return pl.pallas_call(
        body, name=name, grid=(t // tm,),
        in_specs=[_row_spec(tm, D_MODEL), _full_spec((1, D_MODEL)), _ANY, _ANY],
        out_specs=[_row_spec(tm, D_MODEL), _row_spec(tm, D_MODEL), _row_spec(tm, 2 * D_FF)],
        out_shape=[jax.ShapeDtypeStruct((t, D_MODEL), F32), jax.ShapeDtypeStruct((t, D_MODEL), BF16),
                   jax.ShapeDtypeStruct((t, 2 * D_FF), BF16)],
        scratch_shapes=[pltpu.VMEM(w13s.shape, BF16), pltpu.VMEM(w2.shape, BF16)],
        compiler_params=_params(dimension_semantics=("arbitrary",)),
    )(x, g, w13s, w2)


def _ffn_bwd(dy, x, gu, g, w13s, w2, name):
    t = x.shape[0]
    tm = 256

    def body(dy_ref, x_ref, gu_ref, g_ref, w13_hbm, w2_hbm, dx_ref, dgu_ref, a_ref, dg_ref, w13_ref, w2_ref):
        @pl.when(pl.program_id(0) == 0)
        def _():
            pltpu.sync_copy(w13_hbm, w13_ref)
            pltpu.sync_copy(w2_hbm, w2_ref)
            dg_ref[...] = jnp.zeros_like(dg_ref)

        dyv = dy_ref[...]
        dyh = (0.5 * dyv).astype(BF16)
        dh = jnp.zeros((tm, D_MODEL), F32)
        for half in range(2):
            lo = half * FF_SHARD
            da = _dot_nt(dyh, w2_ref[lo:lo + FF_SHARD, :])
            gate = gu_ref[:, lo:lo + FF_SHARD].astype(F32)
            up = gu_ref[:, D_FF + lo:D_FF + lo + FF_SHARD].astype(F32)
            sg = _sigmoid(gate)
            act = gate * sg
            a_ref[:, lo:lo + FF_SHARD] = (act * up).astype(BF16)
            dgate = (da * up * _silu_grad(gate, sg)).astype(BF16)
            dup = (da * act).astype(BF16)
            dgu_ref[:, lo:lo + FF_SHARD] = dgate
            dgu_ref[:, D_FF + lo:D_FF + lo + FF_SHARD] = dup
            dh = dh + _dot_nt(dgate, w13_ref[half]) + _dot_nt(dup, w13_ref[2 + half])
        xv = x_ref[...]
        dxn, dg_rows = _rms_bwd(xv, _rms_stats(xv), g_ref[...], dh)
        dx_ref[...] = dyv + dxn
        dg_ref[...] += jnp.sum(dg_rows, axis=0, keepdims=True)

    return pl.pallas_call(
        body, name=name, grid=(t // tm,),
        in_specs=[_row_spec(tm, D_MODEL), _row_spec(tm, D_MODEL), _row_spec(tm, 2 * D_FF),
                  _full_spec((1, D_MODEL)), _ANY, _ANY],
        out_specs=[_row_spec(tm, D_MODEL), _row_spec(tm, 2 * D_FF), _row_spec(tm, D_FF),
                   _full_spec((1, D_MODEL))],
        out_shape=[jax.ShapeDtypeStruct((t, D_MODEL), F32), jax.ShapeDtypeStruct((t, 2 * D_FF), BF16),
                   jax.ShapeDtypeStruct((t, D_FF), BF16), jax.ShapeDtypeStruct((1, D_MODEL), F32)],
        scratch_shapes=[pltpu.VMEM(w13s.shape, BF16), pltpu.VMEM(w2.shape, BF16)],
        compiler_params=_params(dimension_semantics=("arbitrary",)),
    )(dy, x, gu, g, w13s, w2)


def _wgrad(a, b, n_blocks, name, scale=1.0, tm=256):
    t, m = a.shape
    n = b.shape[1]
    bn = n // n_blocks

    def body(a_ref, b_ref, o_ref):
        bv = b_ref[...]
        if scale != 1.0:
            bv = bv * scale
        o_ref[0] = _dot_tn(a_ref[...].astype(BF16), bv.astype(BF16)).astype(BF16)

    return pl.pallas_call(
        body, name=name, grid=(n_blocks, m // tm),
        in_specs=[pl.BlockSpec((t, tm), lambda j, i: (0, i)), pl.BlockSpec((t, bn), lambda j, i: (0, j))],
        out_specs=pl.BlockSpec((1, tm, bn), lambda j, i: (j, i, 0)),
        out_shape=jax.ShapeDtypeStruct((n_blocks, m, bn), BF16),
        compiler_params=_params(dimension_semantics=("arbitrary", "arbitrary")),
    )(a, b)


def _mix_proj(x, g, w_ag, w_qkv, w_f):
    t = x.shape[0]
    tm = 256

    def body(x_ref, g_ref, wag_ref, wqkv_ref, wf_ref, h_ref, ag_ref, qkv_ref, fl_ref):
        xv = x_ref[...]
        hb = (xv * _rms_stats(xv) * g_ref[...]).astype(BF16)
        h_ref[...] = hb
        ag_ref[...] = _dot(hb, wag_ref[...])
        qkv_ref[...] = _dot(hb, wqkv_ref[...]).astype(BF16)
        fl_ref[...] = _dot(hb, wf_ref[...])

    return pl.pallas_call(
        body, name="mix_proj", grid=(t // tm,),
        in_specs=[_row_spec(tm, D_MODEL), _full_spec((1, D_MODEL)), _full_spec(w_ag.shape),
                  _full_spec(w_qkv.shape), _full_spec(w_f.shape)],
        out_specs=[_row_spec(tm, D_MODEL), _row_spec(tm, 2 * D_CONV), _row_spec(tm, 3 * D_ATTN),
                   _row_spec(tm, LANES)],
        out_shape=[jax.ShapeDtypeStruct((t, D_MODEL), BF16), jax.ShapeDtypeStruct((t, 2 * D_CONV), F32),
                   jax.ShapeDtypeStruct((t, 3 * D_ATTN), BF16), jax.ShapeDtypeStruct((t, LANES), F32)],
        compiler_params=_params(dimension_semantics=("arbitrary",)),
    )(x, g, w_ag, w_qkv, w_f)


def _mix_proj_bwd(dag, dqkv, dfl, dx2, x1, g, w_ag, w_qkv, w_f):
    t = x1.shape[0]
    tm = 256

    def body(dag_ref, dqkv_ref, dfl_ref, dx2_ref, x_ref, g_ref, wag_ref, wqkv_ref, wf_ref, dx_ref, dg_ref):
        @pl.when(pl.program_id(0) == 0)
        def _():
            dg_ref[...] = jnp.zeros_like(dg_ref)

        dh = (_dot_nt(dag_ref[...].astype(BF16), wag_ref[...]) + _dot_nt(dqkv_ref[...], wqkv_ref[...])
              + _dot_nt(dfl_ref[...].astype(BF16), wf_ref[...]))
        xv = x_ref[...]
        dxn, dg_rows = _rms_bwd(xv, _rms_stats(xv), g_ref[...], dh)
        dx_ref[...] = dx2_ref[...] + dxn
        dg_ref[...] += jnp.sum(dg_rows, axis=0, keepdims=True)

    return pl.pallas_call(
        body, name="mix_proj_bwd", grid=(t // tm,),
        in_specs=[_row_spec(tm, 2 * D_CONV), _row_spec(tm, 3 * D_ATTN), _row_spec(tm, LANES),
                  _row_spec(tm, D_MODEL), _row_spec(tm, D_MODEL), _full_spec((1, D_MODEL)),
                  _full_spec(w_ag.shape), _full_spec(w_qkv.shape), _full_spec(w_f.shape)],
        out_specs=[_row_spec(tm, D_MODEL), _full_spec((1, D_MODEL))],
        out_shape=[jax.ShapeDtypeStruct((t, D_MODEL), F32), jax.ShapeDtypeStruct((1, D_MODEL), F32)],
        compiler_params=_params(dimension_semantics=("arbitrary",)),
    )(dag, dqkv, dfl, dx2, x1, g, w_ag, w_qkv, w_f)


def _split3(x):
    hi = x.astype(BF16)
    r1 = x - hi.astype(F32)
    mid = r1.astype(BF16)
    lo = (r1 - mid.astype(F32)).astype(BF16)
    return hi, mid, lo


def _gates_fwd(flt, fb):
    t = flt.shape[1]

    def body(f_ref, b_ref, d_ref):
        z = f_ref[...] + b_ref[...]
        logf = jnp.minimum(z, 0.0) - jnp.log(1.0 + jnp.exp(-jnp.abs(z)))
        row = lax.broadcasted_iota(jnp.int32, (LANES, LANES), 0)
        col = lax.broadcasted_iota(jnp.int32, (LANES, LANES), 1)
        upper = (row <= col).astype(BF16)
        carry = jnp.zeros((HEAD_ROWS, 1), F32)
        for blk in range(t // LANES):
            hi, mid, lo = _split3(logf[:, blk * LANES:(blk + 1) * LANES])
            cs = _dot(hi, upper) + _dot(mid, upper) + _dot(lo, upper)
            d_ref[:, blk * LANES:(blk + 1) * LANES] = cs + carry
            carry = carry + cs[:, LANES - 1:LANES]

    return pl.pallas_call(
        body, name="gates_fwd", out_shape=jax.ShapeDtypeStruct((HEAD_ROWS, t), F32),
        compiler_params=_params(),
    )(flt, fb)


def _gates_bwd(dd, flt, fb):
    t = flt.shape[1]

    def body(dd_ref, f_ref, b_ref, df_ref, db_ref):
        z = f_ref[...] + b_ref[...]
        row = lax.broadcasted_iota(jnp.int32, (LANES, LANES), 0)
        col = lax.broadcasted_iota(jnp.int32, (LANES, LANES), 1)
        lower = (row >= col).astype(BF16)
        carry = jnp.zeros((HEAD_ROWS, 1), F32)
        db = jnp.zeros((HEAD_ROWS, 1), F32)
        for blk in reversed(range(t // LANES)):
            sl = slice(blk * LANES, (blk + 1) * LANES)
            hi, mid, lo = _split3(dd_ref[:, sl])
            cs = _dot(hi, lower) + _dot(mid, lower) + _dot(lo, lower)
            dz = (cs + carry) * _sigmoid(-z[:, sl])
            df_ref[:, sl] = dz
            db = db + jnp.sum(dz, axis=1, keepdims=True)
            carry = carry + cs[:, 0:1]
        db_ref[...] = db

    return pl.pallas_call(
        body, name="gates_bwd",
        out_shape=[jax.ShapeDtypeStruct((HEAD_ROWS, t), F32), jax.ShapeDtypeStruct((HEAD_ROWS, 1), F32)],
        compiler_params=_params(),
    )(dd, flt, fb)


CONV_CHUNK = 64
CONV_TAIL = 16
CONV_WINDOW = CONV_CHUNK + CONV_PAD + 8
CONV_ROWS_EXTRA = CONV_PAD + CONV_TAIL
SUBLANES = 8


def _conv_rows(ag_ref, u_ref, t):
    u_ref[0:CONV_PAD, :] = jnp.zeros((CONV_PAD, D_CONV), F32)
    u_ref[CONV_PAD + t:CONV_ROWS_EXTRA + t, :] = jnp.zeros((CONV_TAIL, D_CONV), F32)

    def fill(i, c):
        r0 = pl.multiple_of(i * CONV_CHUNK, CONV_CHUNK)
        a = ag_ref[pl.ds(r0, CONV_CHUNK), 0:D_CONV]
        gt = ag_ref[pl.ds(r0, CONV_CHUNK), D_CONV:2 * D_CONV]
        u_ref[pl.ds(CONV_PAD + r0, CONV_CHUNK), :] = a * _sigmoid(gt)
        return c

    lax.fori_loop(0, t // CONV_CHUNK, fill, 0)


def _for_shifted(ref, r0, offsets, fn):
    window = ref[pl.ds(r0, CONV_WINDOW), :]
    for rem in range(SUBLANES):
        mine = [o for o in offsets if o % SUBLANES == rem]
        if not mine:
            continue
        turned = window if rem == 0 else pltpu.roll(window, CONV_WINDOW - rem, 0)
        for o in mine:
            fn(o, turned[o - rem:o - rem + CONV_CHUNK])


def _conv_point(u_ref, r0, w_ref, cb, lg, lb):
    acc = [jnp.zeros((CONV_CHUNK, D_CONV), F32)]

    def tap(o, rows):
        j = o - (CONV_PAD - CONV_WIDTH + 1)
        acc[0] = acc[0] + w_ref[j:j + 1, :] * rows

    _for_shifted(u_ref, r0, [j + CONV_PAD - CONV_WIDTH + 1 for j in range(CONV_WIDTH)], tap)
    y = acc[0] + cb
    mu = jnp.mean(y, axis=-1, keepdims=True)
    yc = y - mu
    rstd = lax.rsqrt(jnp.mean(yc * yc, axis=-1, keepdims=True) + EPS)
    yhat = yc * rstd
    z = yhat * lg + lb
    sg = _sigmoid(z)
    s = z * sg
    rr = _rms_stats(s)
    return yhat, rstd, z, sg, s, rr


def _conv_fwd(ag, conv_w, conv_b, ln_g, ln_b, norm_g):
    t = ag.shape[0]

    def body(ag_ref, w_ref, cb_ref, lg_ref, lb_ref, ng_ref, o_ref, u_ref):
        _conv_rows(ag_ref, u_ref, t)
        cb, lg, lb, ng = cb_ref[...], lg_ref[...], lb_ref[...], ng_ref[...]

        def chunk(i, c):
            r0 = pl.multiple_of(i * CONV_CHUNK, CONV_CHUNK)
            _, _, _, _, s, rr = _conv_point(u_ref, r0, w_ref, cb, lg, lb)
            o_ref[pl.ds(r0, CONV_CHUNK), :] = (s * rr * ng).astype(BF16)
            return c

        lax.fori_loop(0, t // CONV_CHUNK, chunk, 0)

    return pl.pallas_call(
        body, name="conv_fwd", out_shape=jax.ShapeDtypeStruct((t, D_CONV), BF16),
        scratch_shapes=[pltpu.VMEM((t + CONV_ROWS_EXTRA, D_CONV), F32)],
        compiler_params=_params(),
    )(ag, conv_w, conv_b, ln_g, ln_b, norm_g)


def _conv_bwd(ag, dout, conv_w, conv_b, ln_g, ln_b, norm_g):
    t = ag.shape[0]

    def body(ag_ref, do_ref, w_ref, cb_ref, lg_ref, lb_ref, ng_ref,
             dag_ref, dw_ref, dcb_ref, dlg_ref, dlb_ref, dng_ref, u_ref, dy_ref):
        _conv_rows(ag_ref, u_ref, t)
        dy_ref[t:t + CONV_ROWS_EXTRA, :] = jnp.zeros((CONV_ROWS_EXTRA, D_CONV), F32)
        cb, lg, lb, ng = cb_ref[...], lg_ref[...], lb_ref[...], ng_ref[...]
        dw_ref[...] = jnp.zeros_like(dw_ref)
        zero = jnp.zeros((1, D_CONV), F32)

        def chunk(i, carry):
            dcb, dlg, dlb, dng = carry
            r0 = pl.multiple_of(i * CONV_CHUNK, CONV_CHUNK)
            yhat, rstd, z, sg, s, rr = _conv_point(u_ref, r0, w_ref, cb, lg, lb)
            do = do_ref[pl.ds(r0, CONV_CHUNK), :]
            ds, dng_rows = _rms_bwd(s, rr, ng, do)
            dz = ds * _silu_grad(z, sg)
            dyhat = dz * lg
            dy = rstd * (dyhat - jnp.mean(dyhat, axis=-1, keepdims=True)
                         - yhat * jnp.mean(dyhat * yhat, axis=-1, keepdims=True))
            dy_ref[pl.ds(r0, CONV_CHUNK), :] = dy
            def tap(o, rows):
                j = o - (CONV_PAD - CONV_WIDTH + 1)
                dw_ref[j:j + 1, :] += jnp.sum(dy * rows, axis=0, keepdims=True)

            _for_shifted(u_ref, r0, [j + CONV_PAD - CONV_WIDTH + 1 for j in range(CONV_WIDTH)], tap)
            return (dcb + jnp.sum(dy, axis=0, keepdims=True), dlg + jnp.sum(dz * yhat, axis=0, keepdims=True),
                    dlb + jnp.sum(dz, axis=0, keepdims=True), dng + jnp.sum(dng_rows, axis=0, keepdims=True))

        dcb, dlg, dlb, dng = lax.fori_loop(0, t // CONV_CHUNK, chunk, (zero, zero, zero, zero))
        dcb_ref[...] = dcb
        dlg_ref[...] = dlg
        dlb_ref[...] = dlb
        dng_ref[...] = dng

        def chunk2(i, c):
            r0 = pl.multiple_of(i * CONV_CHUNK, CONV_CHUNK)
            acc = [jnp.zeros((CONV_CHUNK, D_CONV), F32)]

            def tap(o, rows):
                j = CONV_WIDTH - 1 - o
                acc[0] = acc[0] + w_ref[j:j + 1, :] * rows

            _for_shifted(dy_ref, r0, list(range(CONV_WIDTH)), tap)
            du = acc[0]
            a = ag_ref[pl.ds(r0, CONV_CHUNK), 0:D_CONV]
            gt = ag_ref[pl.ds(r0, CONV_CHUNK), D_CONV:2 * D_CONV]
            sg = _sigmoid(gt)
            dag_ref[pl.ds(r0, CONV_CHUNK), 0:D_CONV] = du * sg
            dag_ref[pl.ds(r0, CONV_CHUNK), D_CONV:2 * D_CONV] = du * a * sg * (1.0 - sg)
            return c

        lax.fori_loop(0, t // CONV_CHUNK, chunk2, 0)

    vec = jax.ShapeDtypeStruct((1, D_CONV), F32)
    return pl.pallas_call(
        body, name="conv_bwd",
        out_shape=[jax.ShapeDtypeStruct((t, 2 * D_CONV), F32), jax.ShapeDtypeStruct((CONV_PAD, D_CONV), F32),
                   vec, vec, vec, vec],
        scratch_shapes=[pltpu.VMEM((t + CONV_ROWS_EXTRA, D_CONV), F32), pltpu.VMEM((t + CONV_ROWS_EXTRA, D_CONV), F32)],
        compiler_params=_params(),
    )(ag, dout, conv_w, conv_b, ln_g, ln_b, norm_g)


Q_ROWS = 256
ATTN_SCALE = HEAD_DIM ** -0.5


def _attn_specs(t):
    blk = lambda off: pl.BlockSpec((t, LANES), lambda p: (0, off + p))
    pairs = N_HEADS // 2
    return [blk(0), blk(pairs), blk(2 * pairs),
            pl.BlockSpec((2, t, 1), lambda p: (p, 0, 0)), pl.BlockSpec((2, 1, t), lambda p: (p, 0, 0))]


def _attn_scores(qm, k2, dcol, drow, r0, q1):
    s = _dot_nt(qm, k2) * ATTN_SCALE + dcol - drow
    rowi = lax.broadcasted_iota(jnp.int32, (q1 - r0, q1), 0) + r0
    coli = lax.broadcasted_iota(jnp.int32, (q1 - r0, q1), 1)
    return s, coli <= rowi


def _attn_fwd(qkv, dcol, drow):
    t = qkv.shape[0]

    def body(q_ref, k_ref, v_ref, dc_ref, dr_ref, o_ref, lse_ref):
        head_a = lax.broadcasted_iota(jnp.int32, (1, LANES), 1) < HEAD_DIM
        for qb in range(t // Q_ROWS):
            r0, q1 = qb * Q_ROWS, (qb + 1) * Q_ROWS
            q2 = q_ref[r0:q1, :]
            k2 = k_ref[0:q1, :]
            v2 = v_ref[0:q1, :]
            outs = []
            for hh in range(2):
                qm = jnp.where(head_a if hh == 0 else ~head_a, q2, jnp.zeros_like(q2))
                s, keep = _attn_scores(qm, k2, dc_ref[hh, r0:q1, :], dr_ref[hh, :, 0:q1], r0, q1)
                s = jnp.where(keep, s, -jnp.inf)
                mx = jnp.max(s, axis=1, keepdims=True)
                p = jnp.exp(s - mx)
                l = jnp.sum(p, axis=1, keepdims=True)
                lse_ref[hh, r0:q1, :] = mx + jnp.log(l)
                outs.append(_dot((p * (1.0 / l)).astype(BF16), v2))
            o_ref[r0:q1, :] = jnp.where(head_a, outs[0], outs[1])

    pairs = N_HEADS // 2
    return pl.pallas_call(
        body, name="attn_fwd", grid=(pairs,), in_specs=_attn_specs(t),
        out_specs=[pl.BlockSpec((t, LANES), lambda p: (0, p)), pl.BlockSpec((2, t, 1), lambda p: (p, 0, 0))],
        out_shape=[jax.ShapeDtypeStruct((t, D_ATTN), F32), jax.ShapeDtypeStruct((N_HEADS, t, 1), F32)],
        compiler_params=_params(dimension_semantics=("arbitrary",)),
    )(qkv, qkv, qkv, dcol, drow)


def _attn_bwd(qkv, dcol, drow, lse, do):
    t = qkv.shape[0]

    def body(q_ref, k_ref, v_ref, dc_ref, dr_ref, lse_ref, do_ref,
             dq_ref, dk_ref, dv_ref, dd_ref, dk_acc, dv_acc):
        head_a = lax.broadcasted_iota(jnp.int32, (1, LANES), 1) < HEAD_DIM
        dk_acc[...] = jnp.zeros_like(dk_acc)
        dv_acc[...] = jnp.zeros_like(dv_acc)
        dd_ref[...] = jnp.zeros_like(dd_ref)
        for qb in range(t // Q_ROWS):
            r0, q1 = qb * Q_ROWS, (qb + 1) * Q_ROWS
            q2 = q_ref[r0:q1, :]
            k2 = k_ref[0:q1, :]
            v2 = v_ref[0:q1, :]
            do2 = do_ref[r0:q1, :]
            dqs = []
            dk_sum = jnp.zeros((q1, LANES), F32)
            dv_sum = jnp.zeros((q1, LANES), F32)
            for hh in range(2):
                mask = head_a if hh == 0 else ~head_a
                qm = jnp.where(mask, q2, jnp.zeros_like(q2))
                dob = jnp.where(mask, do2, 0.0).astype(BF16)
                s, keep = _attn_scores(qm, k2, dc_ref[hh, r0:q1, :], dr_ref[hh, :, 0:q1], r0, q1)
                p = jnp.where(keep, jnp.exp(s - lse_ref[hh, r0:q1, :]), 0.0)
                dp = _dot_nt(dob, v2)
                ds = p * (dp - jnp.sum(p * dp, axis=1, keepdims=True))
                dsb = ds.astype(BF16)
                dqs.append(_dot(dsb, k2) * ATTN_SCALE)
                dk_sum = dk_sum + _dot_tn(dsb, qm)
                dv_sum = dv_sum + _dot_tn(p.astype(BF16), dob)
                dd_ref[hh, :, 0:q1] -= jnp.sum(ds, axis=0, keepdims=True)
            dq_ref[r0:q1, :] = jnp.where(head_a, dqs[0], dqs[1]).astype(BF16)
            dk_acc[0:q1, :] += dk_sum * ATTN_SCALE
            dv_acc[0:q1, :] += dv_sum
        dk_ref[...] = dk_acc[...].astype(BF16)
        dv_ref[...] = dv_acc[...].astype(BF16)

    pairs = N_HEADS // 2
    col = pl.BlockSpec((t, LANES), lambda p: (0, p))
    grad = jax.ShapeDtypeStruct((t, D_ATTN), BF16)
    return pl.pallas_call(
        body, name="attn_bwd", grid=(pairs,),
        in_specs=_attn_specs(t) + [pl.BlockSpec((2, t, 1), lambda p: (p, 0, 0)), col],
        out_specs=[col, col, col, pl.BlockSpec((2, 1, t), lambda p: (p, 0, 0))],
        out_shape=[grad, grad, grad, jax.ShapeDtypeStruct((N_HEADS, 1, t), F32)],
        scratch_shapes=[pltpu.VMEM((t, LANES), F32), pltpu.VMEM((t, LANES), F32)],
        compiler_params=_params(dimension_semantics=("arbitrary",)),
    )(qkv, qkv, qkv, dcol, drow, lse, do)


def _out_proj(ycn, o, g_attn, w_out, x1):
    t = x1.shape[0]
    tm = 256

    def body(yc_ref, o_ref, g_ref, w_ref, x_ref, xo_ref, ya_ref):
        ov = o_ref[...]
        ya = (ov * _rms_stats(ov) * g_ref[...]).astype(BF16)
        ya_ref[...] = ya
        xo_ref[...] = x_ref[...] + _dot(yc_ref[...], w_ref[0:D_CONV, :]) + _dot(ya, w_ref[D_CONV:, :])

    return pl.pallas_call(
        body, name="out_proj", grid=(t // tm,),
        in_specs=[_row_spec(tm, D_CONV), _row_spec(tm, D_ATTN), _full_spec((1, D_ATTN)),
                  _full_spec(w_out.shape), _row_spec(tm, D_MODEL)],
        out_specs=[_row_spec(tm, D_MODEL), _row_spec(tm, D_ATTN)],
        out_shape=[jax.ShapeDtypeStruct((t, D_MODEL), F32), jax.ShapeDtypeStruct((t, D_ATTN), BF16)],
        compiler_params=_params(dimension_semantics=("arbitrary",)),
    )(ycn, o, g_attn, w_out, x1)


def _out_proj_bwd(dx2, o, g_attn, w_out):
    t = dx2.shape[0]
    tm = 256

    def body(dx_ref, o_ref, g_ref, w_ref, dyc_ref, do_ref, dg_ref):
        @pl.when(pl.program_id(0) == 0)
        def _():
            dg_ref[...] = jnp.zeros_like(dg_ref)

        dxb = dx_ref[...].astype(BF16)
        dyc_ref[...] = _dot_nt(dxb, w_ref[0:D_CONV, :])
        dya = _dot_nt(dxb, w_ref[D_CONV:, :])
        ov = o_ref[...]
        do, dg_rows = _rms_bwd(ov, _rms_stats(ov), g_ref[...], dya)
        do_ref[...] = do
        dg_ref[...] += jnp.sum(dg_rows, axis=0, keepdims=True)

    return pl.pallas_call(
        body, name="out_proj_bwd", grid=(t // tm,),
        in_specs=[_row_spec(tm, D_MODEL), _row_spec(tm, D_ATTN), _full_spec((1, D_ATTN)), _full_spec(w_out.shape)],
        out_specs=[_row_spec(tm, D_CONV), _row_spec(tm, D_ATTN), _full_spec((1, D_ATTN))],
        out_shape=[jax.ShapeDtypeStruct((t, D_CONV), F32), jax.ShapeDtypeStruct((t, D_ATTN), F32),
                   jax.ShapeDtypeStruct((1, D_ATTN), F32)],
        compiler_params=_params(dimension_semantics=("arbitrary",)),
    )(dx2, o, g_attn, w_out)


def _loss_bwd(x3, target, g):
    t = x3.shape[0]
    tm = 256

    def body(x_ref, t_ref, g_ref, loss_ref, dx_ref, dg_ref):
        @pl.when(pl.program_id(0) == 0)
        def _():
            loss_ref[...] = jnp.zeros_like(loss_ref)
            dg_ref[...] = jnp.zeros_like(dg_ref)

        xv = x_ref[...]
        r = _rms_stats(xv)
        gv = g_ref[...]
        err = xv * r * gv - t_ref[...]
        row = jnp.sum(err * err, axis=1, keepdims=True) * (0.5 / D_MODEL)
        loss_ref[...] += jnp.sum(row, axis=0, keepdims=True)
        dx, dg_rows = _rms_bwd(xv, r, gv, err * (1.0 / D_MODEL))
        dx_ref[...] = dx
        dg_ref[...] += jnp.sum(dg_rows, axis=0, keepdims=True)

    return pl.pallas_call(
        body, name="loss_bwd", grid=(t // tm,),
        in_specs=[_row_spec(tm, D_MODEL), _row_spec(tm, D_MODEL), _full_spec((1, D_MODEL))],
        out_specs=[_full_spec((1, LANES)), _row_spec(tm, D_MODEL), _full_spec((1, D_MODEL))],
        out_shape=[jax.ShapeDtypeStruct((1, LANES), F32), jax.ShapeDtypeStruct((t, D_MODEL), F32),
                   jax.ShapeDtypeStruct((1, D_MODEL), F32)],
        compiler_params=_params(dimension_semantics=("arbitrary",)),
    )(x3, target, g)


def _split_w_in(w_in):
    w_ag = w_in[:, :2 * D_CONV]
    w_qkv = w_in[:, 2 * D_CONV:2 * D_CONV + 3 * D_ATTN]
    w_f = jnp.pad(w_in[:, 2 * D_CONV + 3 * D_ATTN:], ((0, 0), (0, LANES - N_HEADS)))
    return w_ag, w_qkv, w_f


def _head_rows(v):
    return jnp.pad(v, ((0, HEAD_ROWS - N_HEADS),) + ((0, 0),) * (v.ndim - 1))


def _local_step(x, target, w, p):
    t = x.shape[0]
    w_ag, w_qkv, w_f = _split_w_in(w["w_in"])
    conv_w = jnp.pad(p["conv_w"], ((0, CONV_PAD - CONV_WIDTH), (0, 0)))
    fb = _head_rows(p["forget_b"].reshape(N_HEADS, 1))

    x1, h1, gu1 = _ffn_fwd(x, p["ffn1_norm"], w["ffn1_w13"], w["ffn1_w2"], "ffn1_fwd")
    h2, ag, qkv, fl = _mix_proj(x1, p["mix_norm"], w_ag, w_qkv, w_f)
    flt = _head_rows(fl[:, :N_HEADS].T)
    dcum = _gates_fwd(flt, fb)[:N_HEADS]
    dcol, drow = dcum.reshape(N_HEADS, t, 1), dcum.reshape(N_HEADS, 1, t)
    ycn = _conv_fwd(ag, conv_w, p["conv_b"], p["conv_ln_g"], p["conv_ln_b"], p["out_norm_conv"])
    o, lse = _attn_fwd(qkv, dcol, drow)
    x2, yan = _out_proj(ycn, o, p["out_norm_attn"], w["w_out"], x1)
    x3, h3, gu2 = _ffn_fwd(x2, p["ffn2_norm"], w["ffn2_w13"], w["ffn2_w2"], "ffn2_fwd")
    loss, dx3, d_final = _loss_bwd(x3, target, p["final_norm"])

    g = {}
    dx2, dgu2, a2, g["ffn2_norm"] = _ffn_bwd(dx3, x2, gu2, p["ffn2_norm"], w["ffn2_w13"], w["ffn2_w2"], "ffn2_bwd")
    g["ffn2_w13"] = _wgrad(h3, dgu2, N_CHIPS, "ffn2_dw13")
    g["ffn2_w2"] = _wgrad(a2, dx3, 1, "ffn2_dw2", scale=0.5).reshape(D_FF, D_MODEL)
    dyc, do, g["out_norm_attn"] = _out_proj_bwd(dx2, o, p["out_norm_attn"], w["w_out"])
    g["w_out"] = _wgrad(jnp.concatenate([ycn, yan], axis=1), dx2, 1, "dw_out").reshape(D_MODEL, D_MODEL)
    dq, dk, dv, ddrow = _attn_bwd(qkv, dcol, drow, lse, do)
    dflt, dfb = _gates_bwd(_head_rows(ddrow.reshape(N_HEADS, t)), flt, fb)
    g["forget_b"] = dfb[:N_HEADS, 0].reshape(1, N_HEADS)
    dfl = jnp.pad(dflt[:N_HEADS].T, ((0, 0), (0, LANES - N_HEADS)))
    dag, dconv_w, g["conv_b"], g["conv_ln_g"], g["conv_ln_b"], g["out_norm_conv"] = _conv_bwd(
        ag, dyc, conv_w, p["conv_b"], p["conv_ln_g"], p["conv_ln_b"], p["out_norm_conv"])
    g["conv_w"] = dconv_w[:CONV_WIDTH]
    dqkv = jnp.concatenate([dq, dk, dv], axis=1)
    dx1, g["mix_norm"] = _mix_proj_bwd(dag, dqkv, dfl, dx2, x1, p["mix_norm"], w_ag, w_qkv, w_f)
    dproj = jnp.concatenate([dag.astype(BF16), dqkv, dfl.astype(BF16)], axis=1)
    g["w_in"] = _wgrad(h2, dproj, 1, "dw_in").reshape(D_MODEL, dproj.shape[1])[:, :N_IN]
    dx0, dgu1, a1, g["ffn1_norm"] = _ffn_bwd(dx1, x, gu1, p["ffn1_norm"], w["ffn1_w13"], w["ffn1_w2"], "ffn1_bwd")
    g["ffn1_w13"] = _wgrad(h1, dgu1, N_CHIPS, "ffn1_dw13")
    g["ffn1_w2"] = _wgrad(a1, dx1, 1, "ffn1_dw2", scale=0.5).reshape(D_FF, D_MODEL)
    g["final_norm"] = d_final
    return loss[0, 0], dx0, g


MESH = pl.DeviceIdType.MESH


def _place():
    x, y, c = lax.axis_index("x"), lax.axis_index("y"), lax.axis_index("c")
    chips = [(1 - x, y), (x, 1 - y), (1 - x, 1 - y)]
    return x, y, c, chips


def _hbm_out(shape, dtype):
    return jax.ShapeDtypeStruct(shape, dtype)


def _comm_call(body, name, ins, out_shapes, n_remote, n_local):
    return pl.pallas_call(
        body, name=name, in_specs=[_ANY] * len(ins), out_specs=[_ANY] * len(out_shapes), out_shape=out_shapes,
        scratch_shapes=[pltpu.SemaphoreType.DMA((n_remote,)), pltpu.SemaphoreType.DMA((n_remote,)),
                        pltpu.SemaphoreType.DMA((max(n_local, 1),))],
    )(*ins)


def _remote(src, dst, sems, n, to):
    send_sems, recv_sems = sems
    return pltpu.make_async_remote_copy(src_ref=src, dst_ref=dst, send_sem=send_sems.at[n], recv_sem=recv_sems.at[n],
                                        device_id=to, device_id_type=MESH)


def _gather_shards(shards):
    n = len(shards)

    def body(*refs):
        ins, outs = refs[:n], refs[n:2 * n]
        sems, local_sems = refs[2 * n:2 * n + 2], refs[2 * n + 2]
        x, y, c, chips = _place()
        me = 2 * x + y
        sibling = (x, y, 1 - c)

        def half(i, chip_index, core):
            hr = shards[i].shape[0] // 2
            return outs[i].at[chip_index, pl.ds(core * hr, hr), :]

        local = [pltpu.make_async_copy(ins[i], outs[i].at[me], local_sems.at[i]) for i in range(n)]
        for cp in local:
            cp.start()
        sends = []
        for i in range(n):
            hr = shards[i].shape[0] // 2
            for j, chip in enumerate(chips):
                cp = _remote(ins[i].at[pl.ds(c * hr, hr), :], half(i, me, c), sems, 6 * i + j, (*chip, c))
                cp.start()
                sends.append(cp)
        for i in range(n):
            for j, chip in enumerate(chips):
                src_chip = 2 * chip[0] + chip[1]
                landed = half(i, src_chip, c)
                _remote(landed, landed, sems, 6 * i + j, (*chip, c)).wait_recv()
                cp = _remote(landed, landed, sems, 6 * i + 3 + j, sibling)
                cp.start()
                sends.append(cp)
        for i in range(n):
            for j, chip in enumerate(chips):
                src_chip = 2 * chip[0] + chip[1]
                landed = half(i, src_chip, 1 - c)
                _remote(landed, landed, sems, 6 * i + 3 + j, sibling).wait_recv()
        for cp in sends:
            cp.wait_send()
        for cp in local:
            cp.wait()

    outs = [_hbm_out((N_CHIPS,) + s.shape, s.dtype) for s in shards]
    return _comm_call(body, "gather_shards", shards, outs, 6 * n, n)


def _pair_exchange(grads):
    n = len(grads)

    def body(*refs):
        ins, outs = refs[:n], refs[n:2 * n]
        sems = refs[2 * n:2 * n + 2]
        x, y, c, _ = _place()
        sibling = (x, y, 1 - c)
        sends = []
        for i in range(n):
            hr = grads[i].shape[1] // 2
            cp = _remote(ins[i].at[:, pl.ds((1 - c) * hr, hr), :], outs[i], sems, i, sibling)
            cp.start()
            sends.append(cp)
        for cp in sends:
            cp.wait()

    outs = [_hbm_out((N_CHIPS, g.shape[1] // 2, g.shape[2]), g.dtype) for g in grads]
    return _comm_call(body, "pair_exchange", grads, outs, n, 0)


def _chip_scatter(parts):
    n = len(parts)

    def body(*refs):
        ins, outs = refs[:n], refs[n:2 * n]
        sems = refs[2 * n:2 * n + 2]
        x, y, c, chips = _place()
        sends = []
        for i in range(n):
            for j, chip in enumerate(chips):
                cp = _remote(ins[i].at[2 * chip[0] + chip[1]], outs[i].at[j], sems, 3 * i + j, (*chip, c))
                cp.start()
                sends.append(cp)
        for cp in sends:
            cp.wait()

    outs = [_hbm_out((N_CHIPS - 1,) + p.shape[1:], p.dtype) for p in parts]
    return _comm_call(body, "chip_scatter", parts, outs, 3 * n, 0)


def _pair_share(halves):
    n = len(halves)

    def body(*refs):
        ins, outs = refs[:n], refs[n:2 * n]
        sems, local_sems = refs[2 * n:2 * n + 2], refs[2 * n + 2]
        x, y, c, _ = _place()
        sibling = (x, y, 1 - c)
        local = [pltpu.make_async_copy(ins[i], outs[i].at[c], local_sems.at[i]) for i in range(n)]
        sends = [_remote(ins[i], outs[i].at[c], sems, i, sibling) for i in range(n)]
        for cp in local + sends:
            cp.start()
        for cp in sends:
            cp.wait_send()
        for i in range(n):
            _remote(ins[i], outs[i].at[1 - c], sems, i, sibling).wait_recv()
        for cp in local:
            cp.wait()

    outs = [_hbm_out((2,) + h.shape, h.dtype) for h in halves]
    return _comm_call(body, "pair_share", halves, outs, n, n)


def _all_reduce_small(v):
    rows = v.shape[0]
    flips = [(fx, fy, fc) for fx in range(2) for fy in range(2) for fc in range(2)][1:]

    def body(v_ref, o_ref, slots, send_sems, recv_sems):
        x, y, c, _ = _place()
        me = 4 * x + 2 * y + c
        slots[me] = v_ref[...]
        sends = []
        for n, (fx, fy, fc) in enumerate(flips):
            to = (x ^ fx, y ^ fy, c ^ fc)
            cp = _remote(v_ref, slots.at[me], (send_sems, recv_sems), n, to)
            cp.start()
            sends.append(cp)
        for n, (fx, fy, fc) in enumerate(flips):
            src = 4 * (x ^ fx) + 2 * (y ^ fy) + (c ^ fc)
            _remote(v_ref, slots.at[src], (send_sems, recv_sems), n, (x ^ fx, y ^ fy, c ^ fc)).wait_recv()
        for cp in sends:
            cp.wait_send()
        acc = slots[0]
        for s in range(1, 8):
            acc = acc + slots[s]
        o_ref[...] = acc

    return pl.pallas_call(
        body, name="all_reduce_small", out_shape=jax.ShapeDtypeStruct(v.shape, F32),
        in_specs=[pl.BlockSpec(memory_space=pltpu.VMEM)], out_specs=pl.BlockSpec(memory_space=pltpu.VMEM),
        scratch_shapes=[pltpu.VMEM((8, rows, LANES), F32), pltpu.SemaphoreType.DMA((7,)), pltpu.SemaphoreType.DMA((7,))],
    )(v)


def _pair_add(g, sib, core, name):
    _, r, cols = g.shape
    hr = r // 2
    g4 = g.reshape(N_CHIPS, 2, hr, cols)

    def body(c_ref, g_ref, s_ref, o_ref):
        o_ref[0] = (g_ref[0, 0].astype(F32) + s_ref[0].astype(F32)).astype(BF16)

    return pl.pallas_call(
        body, name=name,
        grid_spec=pltpu.PrefetchScalarGridSpec(
            num_scalar_prefetch=1, grid=(N_CHIPS,),
            in_specs=[pl.BlockSpec((1, 1, hr, cols), lambda s, c_ref: (s, c_ref[0], 0, 0)),
                      pl.BlockSpec((1, hr, cols), lambda s, c_ref: (s, 0, 0))],
            out_specs=pl.BlockSpec((1, hr, cols), lambda s, c_ref: (s, 0, 0))),
        out_shape=jax.ShapeDtypeStruct((N_CHIPS, hr, cols), BF16),
        compiler_params=_params(dimension_semantics=("arbitrary",)),
    )(core, g4, sib)


def _chip_add(part, recv, chip, name):
    _, hr, cols = part.shape

    def body(k_ref, p_ref, r_ref, o_ref):
        acc = p_ref[0].astype(F32)
        for j in range(N_CHIPS - 1):
            acc = acc + r_ref[j].astype(F32)
        o_ref[...] = acc

    return pl.pallas_call(
        body, name=name,
        grid_spec=pltpu.PrefetchScalarGridSpec(
            num_scalar_prefetch=1, grid=(1,),
            in_specs=[pl.BlockSpec((1, hr, cols), lambda s, k_ref: (k_ref[0], 0, 0)),
                      pl.BlockSpec((N_CHIPS - 1, hr, cols), lambda s, k_ref: (0, 0, 0))],
            out_specs=pl.BlockSpec((hr, cols), lambda s, k_ref: (0, 0))),
        out_shape=jax.ShapeDtypeStruct((hr, cols), F32),
        compiler_params=_params(dimension_semantics=("arbitrary",)),
    )(chip, part, recv)


def _adamw_math(w, g, m, v):
    m = ADAM_B1 * m + (1.0 - ADAM_B1) * g
    v = ADAM_B2 * v + (1.0 - ADAM_B2) * (g * g)
    m_hat = m / (1.0 - ADAM_B1 ** ADAM_STEP)
    v_hat = v / (1.0 - ADAM_B2 ** ADAM_STEP)
    delta = -ADAM_LR * (m_hat / (jnp.sqrt(v_hat) + ADAM_EPS) + ADAM_WD * w)
    return delta, m, v


def _adamw_matrix(w, g, m, v, name, tr):
    rows, cols = w.shape

    def body(w_ref, g_ref, m_ref, v_ref, d_ref, mo_ref, vo_ref):
        d_ref[...], mo_ref[...], vo_ref[...] = _adamw_math(w_ref[...], g_ref[...], m_ref[...], v_ref[...])

    spec = _row_spec(tr, cols)
    shape = jax.ShapeDtypeStruct((rows, cols), F32)
    return pl.pallas_call(
        body, name=name, grid=(rows // tr,), in_specs=[spec] * 4, out_specs=[spec] * 3, out_shape=[shape] * 3,
        compiler_params=_params(dimension_semantics=("arbitrary",)),
    )(w, g, m, v)


def _adamw_small(ws, gs, ms, vs):
    n = len(ws)

    def body(*refs):
        for i in range(n):
            w_ref, g_ref, m_ref, v_ref = (refs[k * n + i] for k in range(4))
            d_ref, mo_ref, vo_ref = (refs[(4 + k) * n + i] for k in range(3))
            d_ref[...], mo_ref[...], vo_ref[...] = _adamw_math(w_ref[...], g_ref[...], m_ref[...], v_ref[...])

    shapes = [jax.ShapeDtypeStruct(w.shape, F32) for w in ws]
    out = pl.pallas_call(body, name="adamw_small", out_shape=shapes * 3, compiler_params=_params())(*ws, *gs, *ms, *vs)
    return out[:n], out[n:2 * n], out[2 * n:]


MATRICES = ["ffn1_w13", "ffn1_w2", "w_in", "w_out", "ffn2_w13", "ffn2_w2"]
VECTORS = ["ffn1_norm", "mix_norm", "conv_b", "conv_ln_g", "conv_ln_b", "forget_b", "out_norm_conv",
           "out_norm_attn", "ffn2_norm", "final_norm"]
WEIGHTS = ["ffn1_norm", "ffn1_w13", "ffn1_w2", "mix_norm", "w_in", "conv_w", "conv_b", "conv_ln_g", "conv_ln_b",
           "forget_b", "out_norm_conv", "out_norm_attn", "w_out", "ffn2_norm", "ffn2_w13", "ffn2_w2", "final_norm"]
ADAM_ROWS = {"ffn1_w13": 256, "ffn2_w13": 256, "ffn1_w2": 352, "ffn2_w2": 352, "w_in": 256, "w_out": 256}


def _pack_small(g):
    rows, layout = [], []
    for n in VECTORS + ["conv_w"]:
        flat = g[n].reshape(-1)
        pad = (-flat.shape[0]) % LANES
        rows.append(jnp.pad(flat, (0, pad)).reshape(-1, LANES))
        layout.append((n, g[n].shape, flat.shape[0], rows[-1].shape[0]))
    packed = jnp.concatenate(rows, axis=0)
    pad_rows = (-packed.shape[0]) % 8
    return jnp.pad(packed, ((0, pad_rows), (0, 0))), layout


def _unpack_small(packed, layout):
    out, r = {}, 0
    for n, shape, size, nrows in layout:
        out[n] = packed[r:r + nrows].reshape(-1)[:size].reshape(shape)
        r += nrows
    return out


def kernel(x, ffn1_norm, ffn1_w13, ffn1_w2, mix_norm, w_in, conv_w, conv_b, conv_ln_g, conv_ln_b, forget_b, out_norm_conv, out_norm_attn, w_out, ffn2_norm, ffn2_w13, ffn2_w2, final_norm, loss_target, m_ffn1_norm, m_ffn1_w13, m_ffn1_w2, m_mix_norm, m_w_in, m_conv_w, m_conv_b, m_conv_ln_g, m_conv_ln_b, m_forget_b, m_out_norm_conv, m_out_norm_attn, m_w_out, m_ffn2_norm, m_ffn2_w13, m_ffn2_w2, m_final_norm, v_ffn1_norm, v_ffn1_w13, v_ffn1_w2, v_mix_norm, v_w_in, v_conv_w, v_conv_b, v_conv_ln_g, v_conv_ln_b, v_forget_b, v_out_norm_conv, v_out_norm_attn, v_w_out, v_ffn2_norm, v_ffn2_w13, v_ffn2_w2, v_final_norm):
    args = dict(locals())
    weights = {n: args[n] for n in WEIGHTS}
    shard = {n: weights[n][0] for n in MATRICES}
    core = lax.axis_index("c").astype(jnp.int32).reshape(1)
    chip = (2 * lax.axis_index("x") + lax.axis_index("y")).astype(jnp.int32)

    conv_w_rows = jnp.pad(conv_w[0], ((0, CONV_PAD - CONV_WIDTH), (0, 0)))
    gathered = _gather_shards([shard[n].astype(BF16) for n in MATRICES] + [conv_w_rows])
    w = dict(zip(MATRICES, gathered[:-1]))
    w["ffn1_w2"] = w["ffn1_w2"].reshape(D_FF, D_MODEL)
    w["ffn2_w2"] = w["ffn2_w2"].reshape(D_FF, D_MODEL)
    w["w_out"] = w["w_out"].reshape(D_MODEL, D_MODEL)
    w["w_in"] = w["w_in"].transpose(1, 0, 2).reshape(D_MODEL, N_IN)
    p = {n: weights[n] for n in VECTORS}
    p["final_norm"] = final_norm.reshape(1, D_MODEL)
    p["conv_w"] = gathered[-1][:, :CONV_WIDTH].transpose(1, 0, 2).reshape(CONV_WIDTH, D_CONV)

    loss_part, dx, g = _local_step(x[0], loss_target[0], w, p)
    loss = lax.psum(loss_part, ("x", "y", "c"))

    packed, layout = _pack_small(g)
    small = _unpack_small(_all_reduce_small(packed), layout)
    grad = {n: small[n] for n in VECTORS}
    grad["final_norm"] = small["final_norm"].reshape(D_MODEL)
    grad["conv_w"] = lax.dynamic_slice_in_dim(small["conv_w"], chip * (D_CONV // N_CHIPS), D_CONV // N_CHIPS, axis=1)[None]

    local = [g["ffn1_w13"], g["ffn1_w2"].reshape(N_CHIPS, D_FF // N_CHIPS, D_MODEL),
             g["w_in"].reshape(D_MODEL, N_CHIPS, IN_SHARD).transpose(1, 0, 2),
             g["w_out"].reshape(N_CHIPS, D_MODEL // N_CHIPS, D_MODEL), g["ffn2_w13"],
             g["ffn2_w2"].reshape(N_CHIPS, D_FF // N_CHIPS, D_MODEL)]
    sib = _pair_exchange(local)
    parts = [_pair_add(a, b, core, "pair_add_" + n) for a, b, n in zip(local, sib, MATRICES)]
    recv = _chip_scatter(parts)
    halves = [_chip_add(a, b, chip.reshape(1), "chip_add_" + n) for a, b, n in zip(parts, recv, MATRICES)]
    full = _pair_share(halves)
    for n, f in zip(MATRICES, full):
        grad[n] = f.reshape(1, f.shape[0] * f.shape[1], f.shape[2])

    delta, new_m, new_v = {}, {}, {}
    for n in MATRICES:
        d, mo, vo = _adamw_matrix(weights[n][0], grad[n][0], args["m_" + n][0], args["v_" + n][0], "adamw_" + n, ADAM_ROWS[n])
        delta[n], new_m[n], new_v[n] = d[None], mo[None], vo[None]
    small_names = VECTORS + ["conv_w"]
    as2d = lambda a: a.reshape(-1, a.shape[-1])
    ds, mos, vos = _adamw_small([as2d(weights[n]) for n in small_names], [as2d(grad[n]) for n in small_names],
                                [as2d(args["m_" + n]) for n in small_names], [as2d(args["v_" + n]) for n in small_names])
    for n, d, mo, vo in zip(small_names, ds, mos, vos):
        shape = weights[n].shape
        delta[n], new_m[n], new_v[n] = d.reshape(shape), mo.reshape(shape), vo.reshape(shape)

    return (loss, dx[None], *[grad[n] for n in WEIGHTS], *[delta[n] for n in WEIGHTS],
            *[new_m[n] for n in WEIGHTS], *[new_v[n] for n in WEIGHTS])
```

```python
import functools

import jax
import jax.numpy as jnp
from jax import lax
from jax.experimental import pallas as pl
from jax.experimental.pallas import tpu as pltpu

F32 = jnp.float32
BF16 = jnp.bfloat16

D_MODEL = 1024
D_FF = 2816
FF_SHARD = D_FF // 2
D_CONV = 512
D_ATTN = 512
N_HEADS = 8
HEAD_DIM = 64
CONV_WIDTH = 31
CONV_PAD = 32
N_IN = 2 * D_CONV + 3 * D_ATTN + N_HEADS
IN_SHARD = N_IN // 4
EPS = 1e-6
N_CHIPS = 4
LANES = 128
HEAD_ROWS = 16

ADAM_LR = 0.001
ADAM_B1 = 0.9
ADAM_B2 = 0.999
ADAM_EPS = 1e-08
ADAM_WD = 0.01
ADAM_STEP = 10

VMEM_LIMIT = 56 * 1024 * 1024

_NT = (((1,), (1,)), ((), ()))
_TN = (((0,), (0,)), ((), ()))


def _dot(a, b):
    return jnp.dot(a, b, preferred_element_type=F32)


def _dot_nt(a, b):
    return lax.dot_general(a, b, _NT, preferred_element_type=F32)


def _dot_tn(a, b):
    return lax.dot_general(a, b, _TN, preferred_element_type=F32)


def _params(**kw):
    return pltpu.CompilerParams(vmem_limit_bytes=VMEM_LIMIT, **kw)


def _sigmoid(x):
    return 1.0 / (1.0 + jnp.exp(-x))


def _rms_stats(x):
    return lax.rsqrt(jnp.mean(x * x, axis=-1, keepdims=True) + EPS)


def _rms_bwd(x, r, g, dh):
    t = dh * g
    dx = r * t - x * (r * r * r) * jnp.mean(t * x, axis=-1, keepdims=True)
    return dx, dh * x * r


def _silu_grad(z, sg):
    return sg * (1.0 + z * (1.0 - sg))


def _row_spec(tm, n):
    return pl.BlockSpec((tm, n), lambda i: (i, 0))


def _full_spec(shape):
    nd = len(shape)
    return pl.BlockSpec(shape, lambda i: (0,) * nd)


_ANY = pl.BlockSpec(memory_space=pl.ANY)


def _ffn_fwd(x, g, w13s, w2, name):
    t = x.shape[0]
    tm = 256

    def body(x_ref, g_ref, w13_hbm, w2_hbm, xo_ref, h_ref, gu_ref, w13_ref, w2_ref):
        @pl.when(pl.program_id(0) == 0)
        def _():
            pltpu.sync_copy(w13_hbm, w13_ref)
            pltpu.sync_copy(w2_hbm, w2_ref)

        xv = x_ref[...]
        hb = (xv * _rms_stats(xv) * g_ref[...]).astype(BF16)
        h_ref[...] = hb
        acc = jnp.zeros((tm, D_MODEL), F32)
        for half in range(2):
            lo = half * FF_SHARD
            gate = _dot(hb, w13_ref[half])
            up = _dot(hb, w13_ref[2 + half])
            gu_ref[:, lo:lo + FF_SHARD] = gate.astype(BF16)
            gu_ref[:, D_FF + lo:D_FF + lo + FF_SHARD] = up.astype(BF16)
            a = (gate * _sigmoid(gate) * up).astype(BF16)
            acc = acc + _dot(a, w2_ref[lo:lo + FF_SHARD, :])
        xo_ref[...] = xv + 0.5 * acc

    return pl.pallas_call(
        body, name=name, grid=(t // tm,),
        in_specs=[_row_spec(tm, D_MODEL), _full_spec((1, D_MODEL)), _ANY, _ANY],
        out_specs=[_row_spec(tm, D_MODEL), _row_spec(tm, D_MODEL), _row_spec(tm, 2 * D_FF)],
        out_shape=[jax.ShapeDtypeStruct((t, D_MODEL), F32), jax.ShapeDtypeStruct((t, D_MODEL), BF16),
                   jax.ShapeDtypeStruct((t, 2 * D_FF), BF16)],
        scratch_shapes=[pltpu.VMEM(w13s.shape, BF16), pltpu.VMEM(w2.shape, BF16)],
        compiler_params=_params(dimension_semantics=("arbitrary",)),
    )(x, g, w13s, w2)


def _ffn_bwd(dy, x, gu, g, w13s, w2, name):
    t = x.shape[0]
    tm = 256

    def body(dy_ref, x_ref, gu_ref, g_ref, w13_hbm, w2_hbm, dx_ref, dgu_ref, a_ref, dg_ref, w13_ref, w2_ref):
        @pl.when(pl.program_id(0) == 0)
        def _():
            pltpu.sync_copy(w13_hbm, w13_ref)
            pltpu.sync_copy(w2_hbm, w2_ref)
            dg_ref[...] = jnp.zeros_like(dg_ref)

        dyv = dy_ref[...]
        dyh = (0.5 * dyv).astype(BF16)
        dh = jnp.zeros((tm, D_MODEL), F32)
        for half in range(2):
            lo = half * FF_SHARD
            da = _dot_nt(dyh, w2_ref[lo:lo + FF_SHARD, :])
            gate = gu_ref[:, lo:lo + FF_SHARD].astype(F32)
            up = gu_ref[:, D_FF + lo:D_FF + lo + FF_SHARD].astype(F32)
            sg = _sigmoid(gate)
            act = gate * sg
            a_ref[:, lo:lo + FF_SHARD] = (act * up).astype(BF16)
            dgate = (da * up * _silu_grad(gate, sg)).astype(BF16)
            dup = (da * act).astype(BF16)
            dgu_ref[:, lo:lo + FF_SHARD] = dgate
            dgu_ref[:, D_FF + lo:D_FF + lo + FF_SHARD] = dup
            dh = dh + _dot_nt(dgate, w13_ref[half]) + _dot_nt(dup, w13_ref[2 + half])
        xv = x_ref[...]
        dxn, dg_rows = _rms_bwd(xv, _rms_stats(xv), g_ref[...], dh)
        dx_ref[...] = dyv + dxn
        dg_ref[...] += jnp.sum(dg_rows, axis=0, keepdims=True)

    return pl.pallas_call(
        body, name=name, grid=(t // tm,),
        in_specs=[_row_spec(tm, D_MODEL), _row_spec(tm, D_MODEL), _row_spec(tm, 2 * D_FF),
                  _full_spec((1, D_MODEL)), _ANY, _ANY],
        out_specs=[_row_spec(tm, D_MODEL), _row_spec(tm, 2 * D_FF), _row_spec(tm, D_FF),
                   _full_spec((1, D_MODEL))],
        out_shape=[jax.ShapeDtypeStruct((t, D_MODEL), F32), jax.ShapeDtypeStruct((t, 2 * D_FF), BF16),
                   jax.ShapeDtypeStruct((t, D_FF), BF16), jax.ShapeDtypeStruct((1, D_MODEL), F32)],
        scratch_shapes=[pltpu.VMEM(w13s.shape, BF16), pltpu.VMEM(w2.shape, BF16)],
        compiler_params=_params(dimension_semantics=("arbitrary",)),
    )(dy, x, gu, g, w13s, w2)


def _wgrad(a, b, n_blocks, name, scale=1.0, tm=256):
    t, m = a.shape
    n = b.shape[1]
    bn = n // n_blocks

    def body(a_ref, b_ref, o_ref):
        bv = b_ref[...]
        if scale != 1.0:
            bv = bv * scale
        o_ref[0] = _dot_tn(a_ref[...].astype(BF16), bv.astype(BF16)).astype(BF16)

    return pl.pallas_call(
        body, name=name, grid=(n_blocks, m // tm),
        in_specs=[pl.BlockSpec((t, tm), lambda j, i: (0, i)), pl.BlockSpec((t, bn), lambda j, i: (0, j))],
        out_specs=pl.BlockSpec((1, tm, bn), lambda j, i: (j, i, 0)),
        out_shape=jax.ShapeDtypeStruct((n_blocks, m, bn), BF16),
        compiler_params=_params(dimension_semantics=("arbitrary", "arbitrary")),
    )(a, b)


def _mix_proj(x, g, w_ag, w_qkv, w_f):
    t = x.shape[0]
    tm = 256

    def body(x_ref, g_ref, wag_ref, wqkv_ref, wf_ref, h_ref, ag_ref, qkv_ref, fl_ref):
        xv = x_ref[...]
        hb = (xv * _rms_stats(xv) * g_ref[...]).astype(BF16)
        h_ref[...] = hb
        ag_ref[...] = _dot(hb, wag_ref[...])
        qkv_ref[...] = _dot(hb, wqkv_ref[...]).astype(BF16)
        fl_ref[...] = _dot(hb, wf_ref[...])

    return pl.pallas_call(
        body, name="mix_proj", grid=(t // tm,),
        in_specs=[_row_spec(tm, D_MODEL), _full_spec((1, D_MODEL)), _full_spec(w_ag.shape),
                  _full_spec(w_qkv.shape), _full_spec(w_f.shape)],
        out_specs=[_row_spec(tm, D_MODEL), _row_spec(tm, 2 * D_CONV), _row_spec(tm, 3 * D_ATTN),
                   _row_spec(tm, LANES)],
        out_shape=[jax.ShapeDtypeStruct((t, D_MODEL), BF16), jax.ShapeDtypeStruct((t, 2 * D_CONV), F32),
                   jax.ShapeDtypeStruct((t, 3 * D_ATTN), BF16), jax.ShapeDtypeStruct((t, LANES), F32)],
        compiler_params=_params(dimension_semantics=("arbitrary",)),
    )(x, g, w_ag, w_qkv, w_f)


def _mix_proj_bwd(dag, dqkv, dfl, dx2, x1, g, w_ag, w_qkv, w_f):
    t = x1.shape[0]
    tm = 256

    def body(dag_ref, dqkv_ref, dfl_ref, dx2_ref, x_ref, g_ref, wag_ref, wqkv_ref, wf_ref, dx_ref, dg_ref):
        @pl.when(pl.program_id(0) == 0)
        def _():
            dg_ref[...] = jnp.zeros_like(dg_ref)

        dh = (_dot_nt(dag_ref[...].astype(BF16), wag_ref[...]) + _dot_nt(dqkv_ref[...], wqkv_ref[...])
              + _dot_nt(dfl_ref[...].astype(BF16), wf_ref[...]))
        xv = x_ref[...]
        dxn, dg_rows = _rms_bwd(xv, _rms_stats(xv), g_ref[...], dh)
        dx_ref[...] = dx2_ref[...] + dxn
        dg_ref[...] += jnp.sum(dg_rows, axis=0, keepdims=True)

    return pl.pallas_call(
        body, name="mix_proj_bwd", grid=(t // tm,),
        in_specs=[_row_spec(tm, 2 * D_CONV), _row_spec(tm, 3 * D_ATTN), _row_spec(tm, LANES),
                  _row_spec(tm, D_MODEL), _row_spec(tm, D_MODEL), _full_spec((1, D_MODEL)),
                  _full_spec(w_ag.shape), _full_spec(w_qkv.shape), _full_spec(w_f.shape)],
        out_specs=[_row_spec(tm, D_MODEL), _full_spec((1, D_MODEL))],
        out_shape=[jax.ShapeDtypeStruct((t, D_MODEL), F32), jax.ShapeDtypeStruct((1, D_MODEL), F32)],
        compiler_params=_params(dimension_semantics=("arbitrary",)),
    )(dag, dqkv, dfl, dx2, x1, g, w_ag, w_qkv, w_f)


def _split3(x):
    hi = x.astype(BF16)
    r1 = x - hi.astype(F32)
    mid = r1.astype(BF16)
    lo = (r1 - mid.astype(F32)).astype(BF16)
    return hi, mid, lo


def _gates_fwd(flt, fb):
    t = flt.shape[1]

    def body(f_ref, b_ref, d_ref):
        z = f_ref[...] + b_ref[...]
        logf = jnp.minimum(z, 0.0) - jnp.log(1.0 + jnp.exp(-jnp.abs(z)))
        row = lax.broadcasted_iota(jnp.int32, (LANES, LANES), 0)
        col = lax.broadcasted_iota(jnp.int32, (LANES, LANES), 1)
        upper = (row <= col).astype(BF16)
        carry = jnp.zeros((HEAD_ROWS, 1), F32)
        for blk in range(t // LANES):
            hi, mid, lo = _split3(logf[:, blk * LANES:(blk + 1) * LANES])
            cs = _dot(hi, upper) + _dot(mid, upper) + _dot(lo, upper)
            d_ref[:, blk * LANES:(blk + 1) * LANES] = cs + carry
            carry = carry + cs[:, LANES - 1:LANES]

    return pl.pallas_call(
        body, name="gates_fwd", out_shape=jax.ShapeDtypeStruct((HEAD_ROWS, t), F32),
        compiler_params=_params(),
    )(flt, fb)


def _gates_bwd(dd, flt, fb):
    t = flt.shape[1]

    def body(dd_ref, f_ref, b_ref, df_ref, db_ref):
        z = f_ref[...] + b_ref[...]
        row = lax.broadcasted_iota(jnp.int32, (LANES, LANES), 0)
        col = lax.broadcasted_iota(jnp.int32, (LANES, LANES), 1)
        lower = (row >= col).astype(BF16)
        carry = jnp.zeros((HEAD_ROWS, 1), F32)
        db = jnp.zeros((HEAD_ROWS, 1), F32)
        for blk in reversed(range(t // LANES)):
            sl = slice(blk * LANES, (blk + 1) * LANES)
            hi, mid, lo = _split3(dd_ref[:, sl])
            cs = _dot(hi, lower) + _dot(mid, lower) + _dot(lo, lower)
            dz = (cs + carry) * _sigmoid(-z[:, sl])
            df_ref[:, sl] = dz
            db = db + jnp.sum(dz, axis=1, keepdims=True)
            carry = carry + cs[:, 0:1]
        db_ref[...] = db

    return pl.pallas_call(
        body, name="gates_bwd",
        out_shape=[jax.ShapeDtypeStruct((HEAD_ROWS, t), F32), jax.ShapeDtypeStruct((HEAD_ROWS, 1), F32)],
        compiler_params=_params(),
    )(dd, flt, fb)


CONV_CHUNK = 64
CONV_TAIL = 16
CONV_WINDOW = CONV_CHUNK + CONV_PAD + 8
CONV_ROWS_EXTRA = CONV_PAD + CONV_TAIL
SUBLANES = 8


def _conv_rows(ag_ref, u_ref, t):
    u_ref[0:CONV_PAD, :] = jnp.zeros((CONV_PAD, D_CONV), F32)
    u_ref[CONV_PAD + t:CONV_ROWS_EXTRA + t, :] = jnp.zeros((CONV_TAIL, D_CONV), F32)

    def fill(i, c):
        r0 = pl.multiple_of(i * CONV_CHUNK, CONV_CHUNK)
        a = ag_ref[pl.ds(r0, CONV_CHUNK), 0:D_CONV]
        gt = ag_ref[pl.ds(r0, CONV_CHUNK), D_CONV:2 * D_CONV]
        u_ref[pl.ds(CONV_PAD + r0, CONV_CHUNK), :] = a * _sigmoid(gt)
        return c

    lax.fori_loop(0, t // CONV_CHUNK, fill, 0)


def _for_shifted(ref, r0, offsets, fn):
    window = ref[pl.ds(r0, CONV_WINDOW), :]
    for rem in range(SUBLANES):
        mine = [o for o in offsets if o % SUBLANES == rem]
        if not mine:
            continue
        turned = window if rem == 0 else pltpu.roll(window, CONV_WINDOW - rem, 0)
        for o in mine:
            fn(o, turned[o - rem:o - rem + CONV_CHUNK])


def _conv_point(u_ref, r0, w_ref, cb, lg, lb):
    acc = [jnp.zeros((CONV_CHUNK, D_CONV), F32)]

    def tap(o, rows):
        j = o - (CONV_PAD - CONV_WIDTH + 1)
        acc[0] = acc[0] + w_ref[j:j + 1, :] * rows

    _for_shifted(u_ref, r0, [j + CONV_PAD - CONV_WIDTH + 1 for j in range(CONV_WIDTH)], tap)
    y = acc[0] + cb
    mu = jnp.mean(y, axis=-1, keepdims=True)
    yc = y - mu
    rstd = lax.rsqrt(jnp.mean(yc * yc, axis=-1, keepdims=True) + EPS)
    yhat = yc * rstd
    z = yhat * lg + lb
    sg = _sigmoid(z)
    s = z * sg
    rr = _rms_stats(s)
    return yhat, rstd, z, sg, s, rr


def _conv_fwd(ag, conv_w, conv_b, ln_g, ln_b, norm_g):
    t = ag.shape[0]

    def body(ag_ref, w_ref, cb_ref, lg_ref, lb_ref, ng_ref, o_ref, u_ref):
        _conv_rows(ag_ref, u_ref, t)
        cb, lg, lb, ng = cb_ref[...], lg_ref[...], lb_ref[...], ng_ref[...]

        def chunk(i, c):
            r0 = pl.multiple_of(i * CONV_CHUNK, CONV_CHUNK)
            _, _, _, _, s, rr = _conv_point(u_ref, r0, w_ref, cb, lg, lb)
            o_ref[pl.ds(r0, CONV_CHUNK), :] = (s * rr * ng).astype(BF16)
            return c

        lax.fori_loop(0, t // CONV_CHUNK, chunk, 0)

    return pl.pallas_call(
        body, name="conv_fwd", out_shape=jax.ShapeDtypeStruct((t, D_CONV), BF16),
        scratch_shapes=[pltpu.VMEM((t + CONV_ROWS_EXTRA, D_CONV), F32)],
        compiler_params=_params(),
    )(ag, conv_w, conv_b, ln_g, ln_b, norm_g)


def _conv_bwd(ag, dout, conv_w, conv_b, ln_g, ln_b, norm_g):
    t = ag.shape[0]

    def body(ag_ref, do_ref, w_ref, cb_ref, lg_ref, lb_ref, ng_ref,
             dag_ref, dw_ref, dcb_ref, dlg_ref, dlb_ref, dng_ref, u_ref, dy_ref):
        _conv_rows(ag_ref, u_ref, t)
        dy_ref[t:t + CONV_ROWS_EXTRA, :] = jnp.zeros((CONV_ROWS_EXTRA, D_CONV), F32)
        cb, lg, lb, ng = cb_ref[...], lg_ref[...], lb_ref[...], ng_ref[...]
        dw_ref[...] = jnp.zeros_like(dw_ref)
        zero = jnp.zeros((1, D_CONV), F32)

        def chunk(i, carry):
            dcb, dlg, dlb, dng = carry
            r0 = pl.multiple_of(i * CONV_CHUNK, CONV_CHUNK)
            yhat, rstd, z, sg, s, rr = _conv_point(u_ref, r0, w_ref, cb, lg, lb)
            do = do_ref[pl.ds(r0, CONV_CHUNK), :]
            ds, dng_rows = _rms_bwd(s, rr, ng, do)
            dz = ds * _silu_grad(z, sg)
            dyhat = dz * lg
            dy = rstd * (dyhat - jnp.mean(dyhat, axis=-1, keepdims=True)
                         - yhat * jnp.mean(dyhat * yhat, axis=-1, keepdims=True))
            dy_ref[pl.ds(r0, CONV_CHUNK), :] = dy
            def tap(o, rows):
                j = o - (CONV_PAD - CONV_WIDTH + 1)
                dw_ref[j:j + 1, :] += jnp.sum(dy * rows, axis=0, keepdims=True)

            _for_shifted(u_ref, r0, [j + CONV_PAD - CONV_WIDTH + 1 for j in range(CONV_WIDTH)], tap)
            return (dcb + jnp.sum(dy, axis=0, keepdims=True), dlg + jnp.sum(dz * yhat, axis=0, keepdims=True),
                    dlb + jnp.sum(dz, axis=0, keepdims=True), dng + jnp.sum(dng_rows, axis=0, keepdims=True))

        dcb, dlg, dlb, dng = lax.fori_loop(0, t // CONV_CHUNK, chunk, (zero, zero, zero, zero))
        dcb_ref[...] = dcb
        dlg_ref[...] = dlg
        dlb_ref[...] = dlb
        dng_ref[...] = dng

        def chunk2(i, c):
            r0 = pl.multiple_of(i * CONV_CHUNK, CONV_CHUNK)
            acc = [jnp.zeros((CONV_CHUNK, D_CONV), F32)]

            def tap(o, rows):
                j = CONV_WIDTH - 1 - o
                acc[0] = acc[0] + w_ref[j:j + 1, :] * rows

            _for_shifted(dy_ref, r0, list(range(CONV_WIDTH)), tap)
            du = acc[0]
            a = ag_ref[pl.ds(r0, CONV_CHUNK), 0:D_CONV]
            gt = ag_ref[pl.ds(r0, CONV_CHUNK), D_CONV:2 * D_CONV]
            sg = _sigmoid(gt)
            dag_ref[pl.ds(r0, CONV_CHUNK), 0:D_CONV] = du * sg
            dag_ref[pl.ds(r0, CONV_CHUNK), D_CONV:2 * D_CONV] = du * a * sg * (1.0 - sg)
            return c

        lax.fori_loop(0, t // CONV_CHUNK, chunk2, 0)

    vec = jax.ShapeDtypeStruct((1, D_CONV), F32)
    return pl.pallas_call(
        body, name="conv_bwd",
        out_shape=[jax.ShapeDtypeStruct((t, 2 * D_CONV), F32), jax.ShapeDtypeStruct((CONV_PAD, D_CONV), F32),
                   vec, vec, vec, vec],
        scratch_shapes=[pltpu.VMEM((t + CONV_ROWS_EXTRA, D_CONV), F32), pltpu.VMEM((t + CONV_ROWS_EXTRA, D_CONV), F32)],
        compiler_params=_params(),
    )(ag, dout, conv_w, conv_b, ln_g, ln_b, norm_g)


Q_ROWS = 256
ATTN_SCALE = HEAD_DIM ** -0.5


def _attn_specs(t):
    blk = lambda off: pl.BlockSpec((t, LANES), lambda p: (0, off + p))
    pairs = N_HEADS // 2
    return [blk(0), blk(pairs), blk(2 * pairs),
            pl.BlockSpec((2, t, 1), lambda p: (p, 0, 0)), pl.BlockSpec((2, 1, t), lambda p: (p, 0, 0))]


def _attn_scores(qm, k2, dcol, drow, r0, q1):
    s = _dot_nt(qm, k2) * ATTN_SCALE + dcol - drow
    rowi = lax.broadcasted_iota(jnp.int32, (q1 - r0, q1), 0) + r0
    coli = lax.broadcasted_iota(jnp.int32, (q1 - r0, q1), 1)
    return s, coli <= rowi


def _attn_fwd(qkv, dcol, drow):
    t = qkv.shape[0]

    def body(q_ref, k_ref, v_ref, dc_ref, dr_ref, o_ref, lse_ref):
        head_a = lax.broadcasted_iota(jnp.int32, (1, LANES), 1) < HEAD_DIM
        for qb in range(t // Q_ROWS):
            r0, q1 = qb * Q_ROWS, (qb + 1) * Q_ROWS
            q2 = q_ref[r0:q1, :]
            k2 = k_ref[0:q1, :]
            v2 = v_ref[0:q1, :]
            outs = []
            for hh in range(2):
                qm = jnp.where(head_a if hh == 0 else ~head_a, q2, jnp.zeros_like(q2))
                s, keep = _attn_scores(qm, k2, dc_ref[hh, r0:q1, :], dr_ref[hh, :, 0:q1], r0, q1)
                s = jnp.where(keep, s, -jnp.inf)
                mx = jnp.max(s, axis=1, keepdims=True)
                p = jnp.exp(s - mx)
                l = jnp.sum(p, axis=1, keepdims=True)
                lse_ref[hh, r0:q1, :] = mx + jnp.log(l)
                outs.append(_dot((p * (1.0 / l)).astype(BF16), v2))
            o_ref[r0:q1, :] = jnp.where(head_a, outs[0], outs[1])

    pairs = N_HEADS // 2
    return pl.pallas_call(
        body, name="attn_fwd", grid=(pairs,), in_specs=_attn_specs(t),
        out_specs=[pl.BlockSpec((t, LANES), lambda p: (0, p)), pl.BlockSpec((2, t, 1), lambda p: (p, 0, 0))],
        out_shape=[jax.ShapeDtypeStruct((t, D_ATTN), F32), jax.ShapeDtypeStruct((N_HEADS, t, 1), F32)],
        compiler_params=_params(dimension_semantics=("arbitrary",)),
    )(qkv, qkv, qkv, dcol, drow)


def _attn_bwd(qkv, dcol, drow, lse, do):
    t = qkv.shape[0]

    def body(q_ref, k_ref, v_ref, dc_ref, dr_ref, lse_ref, do_ref,
             dq_ref, dk_ref, dv_ref, dd_ref, dk_acc, dv_acc):
        head_a = lax.broadcasted_iota(jnp.int32, (1, LANES), 1) < HEAD_DIM
        dk_acc[...] = jnp.zeros_like(dk_acc)
        dv_acc[...] = jnp.zeros_like(dv_acc)
        dd_ref[...] = jnp.zeros_like(dd_ref)
        for qb in range(t // Q_ROWS):
            r0, q1 = qb * Q_ROWS, (qb + 1) * Q_ROWS
            q2 = q_ref[r0:q1, :]
            k2 = k_ref[0:q1, :]
            v2 = v_ref[0:q1, :]
            do2 = do_ref[r0:q1, :]
            dqs = []
            dk_sum = jnp.zeros((q1, LANES), F32)
            dv_sum = jnp.zeros((q1, LANES), F32)
            for hh in range(2):
                mask = head_a if hh == 0 else ~head_a
                qm = jnp.where(mask, q2, jnp.zeros_like(q2))
                dob = jnp.where(mask, do2, 0.0).astype(BF16)
                s, keep = _attn_scores(qm, k2, dc_ref[hh, r0:q1, :], dr_ref[hh, :, 0:q1], r0, q1)
                p = jnp.where(keep, jnp.exp(s - lse_ref[hh, r0:q1, :]), 0.0)
                dp = _dot_nt(dob, v2)
                ds = p * (dp - jnp.sum(p * dp, axis=1, keepdims=True))
                dsb = ds.astype(BF16)
                dqs.append(_dot(dsb, k2) * ATTN_SCALE)
                dk_sum = dk_sum + _dot_tn(dsb, qm)
                dv_sum = dv_sum + _dot_tn(p.astype(BF16), dob)
                dd_ref[hh, :, 0:q1] -= jnp.sum(ds, axis=0, keepdims=True)
            dq_ref[r0:q1, :] = jnp.where(head_a, dqs[0], dqs[1]).astype(BF16)
            dk_acc[0:q1, :] += dk_sum * ATTN_SCALE
            dv_acc[0:q1, :] += dv_sum
        dk_ref[...] = dk_acc[...].astype(BF16)
        dv_ref[...] = dv_acc[...].astype(BF16)

    pairs = N_HEADS // 2
    col = pl.BlockSpec((t, LANES), lambda p: (0, p))
    grad = jax.ShapeDtypeStruct((t, D_ATTN), BF16)
    return pl.pallas_call(
        body, name="attn_bwd", grid=(pairs,),
        in_specs=_attn_specs(t) + [pl.BlockSpec((2, t, 1), lambda p: (p, 0, 0)), col],
        out_specs=[col, col, col, pl.BlockSpec((2, 1, t), lambda p: (p, 0, 0))],
        out_shape=[grad, grad, grad, jax.ShapeDtypeStruct((N_HEADS, 1, t), F32)],
        scratch_shapes=[pltpu.VMEM((t, LANES), F32), pltpu.VMEM((t, LANES), F32)],
        compiler_params=_params(dimension_semantics=("arbitrary",)),
    )(qkv, qkv, qkv, dcol, drow, lse, do)


def _out_proj(ycn, o, g_attn, w_out, x1):
    t = x1.shape[0]
    tm = 256

    def body(yc_ref, o_ref, g_ref, w_ref, x_ref, xo_ref, ya_ref):
        ov = o_ref[...]
        ya = (ov * _rms_stats(ov) * g_ref[...]).astype(BF16)
        ya_ref[...] = ya
        xo_ref[...] = x_ref[...] + _dot(yc_ref[...], w_ref[0:D_CONV, :]) + _dot(ya, w_ref[D_CONV:, :])

    return pl.pallas_call(
        body, name="out_proj", grid=(t // tm,),
        in_specs=[_row_spec(tm, D_CONV), _row_spec(tm, D_ATTN), _full_spec((1, D_ATTN)),
                  _full_spec(w_out.shape), _row_spec(tm, D_MODEL)],
        out_specs=[_row_spec(tm, D_MODEL), _row_spec(tm, D_ATTN)],
        out_shape=[jax.ShapeDtypeStruct((t, D_MODEL), F32), jax.ShapeDtypeStruct((t, D_ATTN), BF16)],
        compiler_params=_params(dimension_semantics=("arbitrary",)),
    )(ycn, o, g_attn, w_out, x1)


def _out_proj_bwd(dx2, o, g_attn, w_out):
    t = dx2.shape[0]
    tm = 256

    def body(dx_ref, o_ref, g_ref, w_ref, dyc_ref, do_ref, dg_ref):
        @pl.when(pl.program_id(0) == 0)
        def _():
            dg_ref[...] = jnp.zeros_like(dg_ref)

        dxb = dx_ref[...].astype(BF16)
        dyc_ref[...] = _dot_nt(dxb, w_ref[0:D_CONV, :])
        dya = _dot_nt(dxb, w_ref[D_CONV:, :])
        ov = o_ref[...]
        do, dg_rows = _rms_bwd(ov, _rms_stats(ov), g_ref[...], dya)
        do_ref[...] = do
        dg_ref[...] += jnp.sum(dg_rows, axis=0, keepdims=True)

    return pl.pallas_call(
        body, name="out_proj_bwd", grid=(t // tm,),
        in_specs=[_row_spec(tm, D_MODEL), _row_spec(tm, D_ATTN), _full_spec((1, D_ATTN)), _full_spec(w_out.shape)],
        out_specs=[_row_spec(tm, D_CONV), _row_spec(tm, D_ATTN), _full_spec((1, D_ATTN))],
        out_shape=[jax.ShapeDtypeStruct((t, D_CONV), F32), jax.ShapeDtypeStruct((t, D_ATTN), F32),
                   jax.ShapeDtypeStruct((1, D_ATTN), F32)],
        compiler_params=_params(dimension_semantics=("arbitrary",)),
    )(dx2, o, g_attn, w_out)


def _loss_bwd(x3, target, g):
    t = x3.shape[0]
    tm = 256

    def body(x_ref, t_ref, g_ref, loss_ref, dx_ref, dg_ref):
        @pl.when(pl.program_id(0) == 0)
        def _():
            loss_ref[...] = jnp.zeros_like(loss_ref)
            dg_ref[...] = jnp.zeros_like(dg_ref)

        xv = x_ref[...]
        r = _rms_stats(xv)
        gv = g_ref[...]
        err = xv * r * gv - t_ref[...]
        row = jnp.sum(err * err, axis=1, keepdims=True) * (0.5 / D_MODEL)
        loss_ref[...] += jnp.sum(row, axis=0, keepdims=True)
        dx, dg_rows = _rms_bwd(xv, r, gv, err * (1.0 / D_MODEL))
        dx_ref[...] = dx
        dg_ref[...] += jnp.sum(dg_rows, axis=0, keepdims=True)

    return pl.pallas_call(
        body, name="loss_bwd", grid=(t // tm,),
        in_specs=[_row_spec(tm, D_MODEL), _row_spec(tm, D_MODEL), _full_spec((1, D_MODEL))],
        out_specs=[_full_spec((1, LANES)), _row_spec(tm, D_MODEL), _full_spec((1, D_MODEL))],
        out_shape=[jax.ShapeDtypeStruct((1, LANES), F32), jax.ShapeDtypeStruct((t, D_MODEL), F32),
                   jax.ShapeDtypeStruct((1, D_MODEL), F32)],
        compiler_params=_params(dimension_semantics=("arbitrary",)),
    )(x3, target, g)


def _split_w_in(w_in):
    w_ag = w_in[:, :2 * D_CONV]
    w_qkv = w_in[:, 2 * D_CONV:2 * D_CONV + 3 * D_ATTN]
    w_f = jnp.pad(w_in[:, 2 * D_CONV + 3 * D_ATTN:], ((0, 0), (0, LANES - N_HEADS)))
    return w_ag, w_qkv, w_f


def _head_rows(v):
    return jnp.pad(v, ((0, HEAD_ROWS - N_HEADS),) + ((0, 0),) * (v.ndim - 1))


def _local_step(x, target, w, p):
    t = x.shape[0]
    w_ag, w_qkv, w_f = _split_w_in(w["w_in"])
    conv_w = jnp.pad(p["conv_w"], ((0, CONV_PAD - CONV_WIDTH), (0, 0)))
    fb = _head_rows(p["forget_b"].reshape(N_HEADS, 1))

    x1, h1, gu1 = _ffn_fwd(x, p["ffn1_norm"], w["ffn1_w13"], w["ffn1_w2"], "ffn1_fwd")
    h2, ag, qkv, fl = _mix_proj(x1, p["mix_norm"], w_ag, w_qkv, w_f)
    flt = _head_rows(fl[:, :N_HEADS].T)
    dcum = _gates_fwd(flt, fb)[:N_HEADS]
    dcol, drow = dcum.reshape(N_HEADS, t, 1), dcum.reshape(N_HEADS, 1, t)
    ycn = _conv_fwd(ag, conv_w, p["conv_b"], p["conv_ln_g"], p["conv_ln_b"], p["out_norm_conv"])
    o, lse = _attn_fwd(qkv, dcol, drow)
    x2, yan = _out_proj(ycn, o, p["out_norm_attn"], w["w_out"], x1)
    x3, h3, gu2 = _ffn_fwd(x2, p["ffn2_norm"], w["ffn2_w13"], w["ffn2_w2"], "ffn2_fwd")
    loss, dx3, d_final = _loss_bwd(x3, target, p["final_norm"])

    g = {}
    dx2, dgu2, a2, g["ffn2_norm"] = _ffn_bwd(dx3, x2, gu2, p["ffn2_norm"], w["ffn2_w13"], w["ffn2_w2"], "ffn2_bwd")
    g["ffn2_w13"] = _wgrad(h3, dgu2, N_CHIPS, "ffn2_dw13")
    g["ffn2_w2"] = _wgrad(a2, dx3, 1, "ffn2_dw2", scale=0.5).reshape(D_FF, D_MODEL)
    dyc, do, g["out_norm_attn"] = _out_proj_bwd(dx2, o, p["out_norm_attn"], w["w_out"])
    g["w_out"] = _wgrad(jnp.concatenate([ycn, yan], axis=1), dx2, 1, "dw_out").reshape(D_MODEL, D_MODEL)
    dq, dk, dv, ddrow = _attn_bwd(qkv, dcol, drow, lse, do)
    dflt, dfb = _gates_bwd(_head_rows(ddrow.reshape(N_HEADS, t)), flt, fb)
    g["forget_b"] = dfb[:N_HEADS, 0].reshape(1, N_HEADS)
    dfl = jnp.pad(dflt[:N_HEADS].T, ((0, 0), (0, LANES - N_HEADS)))
    dag, dconv_w, g["conv_b"], g["conv_ln_g"], g["conv_ln_b"], g["out_norm_conv"] = _conv_bwd(
        ag, dyc, conv_w, p["conv_b"], p["conv_ln_g"], p["conv_ln_b"], p["out_norm_conv"])
    g["conv_w"] = dconv_w[:CONV_WIDTH]
    dqkv = jnp.concatenate([dq, dk, dv], axis=1)
    dx1, g["mix_norm"] = _mix_proj_bwd(dag, dqkv, dfl, dx2, x1, p["mix_norm"], w_ag, w_qkv, w_f)
    dproj = jnp.concatenate([dag.astype(BF16), dqkv, dfl.astype(BF16)], axis=1)
    g["w_in"] = _wgrad(h2, dproj, 1, "dw_in").reshape(D_MODEL, dproj.shape[1])[:, :N_IN]
    dx0, dgu1, a1, g["ffn1_norm"] = _ffn_bwd(dx1, x, gu1, p["ffn1_norm"], w["ffn1_w13"], w["ffn1_w2"], "ffn1_bwd")
    g["ffn1_w13"] = _wgrad(h1, dgu1, N_CHIPS, "ffn1_dw13")
    g["ffn1_w2"] = _wgrad(a1, dx1, 1, "ffn1_dw2", scale=0.5).reshape(D_FF, D_MODEL)
    g["final_norm"] = d_final
    return loss[0, 0], dx0, g


MESH = pl.DeviceIdType.MESH


def _place():
    x, y, c = lax.axis_index("x"), lax.axis_index("y"), lax.axis_index("c")
    chips = [(1 - x, y), (x, 1 - y), (1 - x, 1 - y)]
    return x, y, c, chips


def _hbm_out(shape, dtype):
    return jax.ShapeDtypeStruct(shape, dtype)


def _comm_call(body, name, ins, out_shapes, n_remote, in_place=False):
    return pl.pallas_call(
        body, name=name, in_specs=[_ANY] * len(ins), out_specs=[_ANY] * len(out_shapes), out_shape=out_shapes,
        scratch_shapes=[pltpu.SemaphoreType.DMA((n_remote,)), pltpu.SemaphoreType.DMA((n_remote,))],
        input_output_aliases={i: i for i in range(len(ins))} if in_place else {},
    )(*ins)


def _remote(src, dst, sems, n, to):
    send_sems, recv_sems = sems
    return pltpu.make_async_remote_copy(src_ref=src, dst_ref=dst, send_sem=send_sems.at[n], recv_sem=recv_sems.at[n],
                                        device_id=to, device_id_type=MESH)


def _into_slot(shard, chip, dtype, name):
    rows, cols = shard.shape
    tr = rows // 2

    def body(k_ref, s_ref, o_ref):
        o_ref[0] = s_ref[...].astype(dtype)

    return pl.pallas_call(
        body, name=name,
        grid_spec=pltpu.PrefetchScalarGridSpec(
            num_scalar_prefetch=1, grid=(rows // tr,),
            in_specs=[pl.BlockSpec((tr, cols), lambda i, k_ref: (i, 0))],
            out_specs=pl.BlockSpec((1, tr, cols), lambda i, k_ref: (k_ref[0], i, 0))),
        out_shape=jax.ShapeDtypeStruct((N_CHIPS, rows, cols), dtype),
        compiler_params=_params(dimension_semantics=("arbitrary",)),
    )(chip, shard)


def _gather_shards(slots):
    n = len(slots)

    def body(*refs):
        outs = refs[n:2 * n]
        sems = refs[2 * n:2 * n + 2]
        x, y, c, chips = _place()
        me = 2 * x + y
        sibling = (x, y, 1 - c)

        def half(i, chip_index, core):
            hr = slots[i].shape[1] // 2
            return outs[i].at[chip_index, pl.ds(core * hr, hr), :]

        sends = []
        for i in range(n):
            for j, chip in enumerate(chips):
                cp = _remote(half(i, me, c), half(i, me, c), sems, 6 * i + j, (*chip, c))
                cp.start()
                sends.append(cp)
        for i in range(n):
            for j, chip in enumerate(chips):
                src_chip = 2 * chip[0] + chip[1]
                landed = half(i, src_chip, c)
                _remote(landed, landed, sems, 6 * i + j, (*chip, c)).wait_recv()
                cp = _remote(landed, landed, sems, 6 * i + 3 + j, sibling)
                cp.start()
                sends.append(cp)
        for i in range(n):
            for j, chip in enumerate(chips):
                src_chip = 2 * chip[0] + chip[1]
                landed = half(i, src_chip, 1 - c)
                _remote(landed, landed, sems, 6 * i + 3 + j, sibling).wait_recv()
        for cp in sends:
            cp.wait_send()

    outs = [_hbm_out(s.shape, s.dtype) for s in slots]
    return _comm_call(body, "gather_shards", slots, outs, 6 * n, in_place=True)


def _pair_exchange(grads):
    n = len(grads)

    def body(*refs):
        ins, outs = refs[:n], refs[n:2 * n]
        sems = refs[2 * n:2 * n + 2]
        x, y, c, _ = _place()
        sibling = (x, y, 1 - c)
        sends = []
        for i in range(n):
            hr = grads[i].shape[1] // 2
            cp = _remote(ins[i].at[:, pl.ds((1 - c) * hr, hr), :], outs[i], sems, i, sibling)
            cp.start()
            sends.append(cp)
        for cp in sends:
            cp.wait()

    outs = [_hbm_out((N_CHIPS, g.shape[1] // 2, g.shape[2]), g.dtype) for g in grads]
    return _comm_call(body, "pair_exchange", grads, outs, n)


def _chip_scatter(parts):
    n = len(parts)

    def body(*refs):
        ins, outs = refs[:n], refs[n:2 * n]
        sems = refs[2 * n:2 * n + 2]
        x, y, c, chips = _place()
        sends = []
        for i in range(n):
            for j, chip in enumerate(chips):
                cp = _remote(ins[i].at[2 * chip[0] + chip[1]], outs[i].at[j], sems, 3 * i + j, (*chip, c))
                cp.start()
                sends.append(cp)
        for cp in sends:
            cp.wait()

    outs = [_hbm_out((N_CHIPS - 1,) + p.shape[1:], p.dtype) for p in parts]
    return _comm_call(body, "chip_scatter", parts, outs, 3 * n)


def _pair_share(halves):
    n = len(halves)

    def body(*refs):
        outs = refs[n:2 * n]
        sems = refs[2 * n:2 * n + 2]
        x, y, c, _ = _place()
        sibling = (x, y, 1 - c)
        sends = [_remote(outs[i].at[c], outs[i].at[c], sems, i, sibling) for i in range(n)]
        for cp in sends:
            cp.start()
        for cp in sends:
            cp.wait_send()
        for i in range(n):
            _remote(outs[i].at[1 - c], outs[i].at[1 - c], sems, i, sibling).wait_recv()

    outs = [_hbm_out(h.shape, h.dtype) for h in halves]
    return _comm_call(body, "pair_share", halves, outs, n, in_place=True)


def _all_reduce_small(v):
    rows = v.shape[0]
    flips = [(fx, fy, fc) for fx in range(2) for fy in range(2) for fc in range(2)][1:]

    def body(v_ref, o_ref, slots, send_sems, recv_sems):
        x, y, c, _ = _place()
        me = 4 * x + 2 * y + c
        slots[me] = v_ref[...]
        sends = []
        for n, (fx, fy, fc) in enumerate(flips):
            to = (x ^ fx, y ^ fy, c ^ fc)
            cp = _remote(v_ref, slots.at[me], (send_sems, recv_sems), n, to)
            cp.start()
            sends.append(cp)
        for n, (fx, fy, fc) in enumerate(flips):
            src = 4 * (x ^ fx) + 2 * (y ^ fy) + (c ^ fc)
            _remote(v_ref, slots.at[src], (send_sems, recv_sems), n, (x ^ fx, y ^ fy, c ^ fc)).wait_recv()
        for cp in sends:
            cp.wait_send()
        acc = slots[0]
        for s in range(1, 8):
            acc = acc + slots[s]
        o_ref[...] = acc

    return pl.pallas_call(
        body, name="all_reduce_small", out_shape=jax.ShapeDtypeStruct(v.shape, F32),
        in_specs=[pl.BlockSpec(memory_space=pltpu.VMEM)], out_specs=pl.BlockSpec(memory_space=pltpu.VMEM),
        scratch_shapes=[pltpu.VMEM((8, rows, LANES), F32), pltpu.SemaphoreType.DMA((7,)), pltpu.SemaphoreType.DMA((7,))],
    )(v)


def _pair_add(g, sib, core, name):
    _, r, cols = g.shape
    hr = r // 2
    g4 = g.reshape(N_CHIPS, 2, hr, cols)

    def body(c_ref, g_ref, s_ref, o_ref):
        o_ref[0] = (g_ref[0, 0].astype(F32) + s_ref[0].astype(F32)).astype(BF16)

    return pl.pallas_call(
        body, name=name,
        grid_spec=pltpu.PrefetchScalarGridSpec(
            num_scalar_prefetch=1, grid=(N_CHIPS,),
            in_specs=[pl.BlockSpec((1, 1, hr, cols), lambda s, c_ref: (s, c_ref[0], 0, 0)),
                      pl.BlockSpec((1, hr, cols), lambda s, c_ref: (s, 0, 0))],
            out_specs=pl.BlockSpec((1, hr, cols), lambda s, c_ref: (s, 0, 0))),
        out_shape=jax.ShapeDtypeStruct((N_CHIPS, hr, cols), BF16),
        compiler_params=_params(dimension_semantics=("arbitrary",)),
    )(core, g4, sib)


def _chip_add(part, recv, chip_core, name):
    _, hr, cols = part.shape

    def body(kc_ref, p_ref, r_ref, o_ref):
        acc = p_ref[0].astype(F32)
        for j in range(N_CHIPS - 1):
            acc = acc + r_ref[j].astype(F32)
        o_ref[0] = acc

    return pl.pallas_call(
        body, name=name,
        grid_spec=pltpu.PrefetchScalarGridSpec(
            num_scalar_prefetch=1, grid=(1,),
            in_specs=[pl.BlockSpec((1, hr, cols), lambda s, kc_ref: (kc_ref[0], 0, 0)),
                      pl.BlockSpec((N_CHIPS - 1, hr, cols), lambda s, kc_ref: (0, 0, 0))],
            out_specs=pl.BlockSpec((1, hr, cols), lambda s, kc_ref: (kc_ref[1], 0, 0))),
        out_shape=jax.ShapeDtypeStruct((2, hr, cols), F32),
        compiler_params=_params(dimension_semantics=("arbitrary",)),
    )(chip_core, part, recv)


def _adamw_math(w, g, m, v):
    m = ADAM_B1 * m + (1.0 - ADAM_B1) * g
    v = ADAM_B2 * v + (1.0 - ADAM_B2) * (g * g)
    m_hat = m / (1.0 - ADAM_B1 ** ADAM_STEP)
    v_hat = v / (1.0 - ADAM_B2 ** ADAM_STEP)
    delta = -ADAM_LR * (m_hat / (jnp.sqrt(v_hat) + ADAM_EPS) + ADAM_WD * w)
    return delta, m, v


def _adamw_matrix(w, g, m, v, name, tr):
    rows, cols = w.shape

    def body(w_ref, g_ref, m_ref, v_ref, d_ref, mo_ref, vo_ref):
        d_ref[...], mo_ref[...], vo_ref[...] = _adamw_math(w_ref[...], g_ref[...], m_ref[...], v_ref[...])

    spec = _row_spec(tr, cols)
    shape = jax.ShapeDtypeStruct((rows, cols), F32)
    return pl.pallas_call(
        body, name=name, grid=(rows // tr,), in_specs=[spec] * 4, out_specs=[spec] * 3, out_shape=[shape] * 3,
        compiler_params=_params(dimension_semantics=("arbitrary",)),
    )(w, g, m, v)


def _adamw_small(ws, gs, ms, vs):
    n = len(ws)

    def body(*refs):
        for i in range(n):
            w_ref, g_ref, m_ref, v_ref = (refs[k * n + i] for k in range(4))
            d_ref, mo_ref, vo_ref = (refs[(4 + k) * n + i] for k in range(3))
            d_ref[...], mo_ref[...], vo_ref[...] = _adamw_math(w_ref[...], g_ref[...], m_ref[...], v_ref[...])

    shapes = [jax.ShapeDtypeStruct(w.shape, F32) for w in ws]
    out = pl.pallas_call(body, name="adamw_small", out_shape=shapes * 3, compiler_params=_params())(*ws, *gs, *ms, *vs)
    return out[:n], out[n:2 * n], out[2 * n:]


MATRICES = ["ffn1_w13", "ffn1_w2", "w_in", "w_out", "ffn2_w13", "ffn2_w2"]
VECTORS = ["ffn1_norm", "mix_norm", "conv_b", "conv_ln_g", "conv_ln_b", "forget_b", "out_norm_conv",
           "out_norm_attn", "ffn2_norm", "final_norm"]
WEIGHTS = ["ffn1_norm", "ffn1_w13", "ffn1_w2", "mix_norm", "w_in", "conv_w", "conv_b", "conv_ln_g", "conv_ln_b",
           "forget_b", "out_norm_conv", "out_norm_attn", "w_out", "ffn2_norm", "ffn2_w13", "ffn2_w2", "final_norm"]
ADAM_ROWS = {"ffn1_w13": 256, "ffn2_w13": 256, "ffn1_w2": 352, "ffn2_w2": 352, "w_in": 256, "w_out": 256}


def _pack_small(g):
    rows, layout = [], []
    for n in VECTORS + ["conv_w"]:
        flat = g[n].reshape(-1)
        pad = (-flat.shape[0]) % LANES
        rows.append(jnp.pad(flat, (0, pad)).reshape(-1, LANES))
        layout.append((n, g[n].shape, flat.shape[0], rows[-1].shape[0]))
    packed = jnp.concatenate(rows, axis=0)
    pad_rows = (-packed.shape[0]) % 8
    return jnp.pad(packed, ((0, pad_rows), (0, 0))), layout


def _unpack_small(packed, layout):
    out, r = {}, 0
    for n, shape, size, nrows in layout:
        out[n] = packed[r:r + nrows].reshape(-1)[:size].reshape(shape)
        r += nrows
    return out


def kernel(x, ffn1_norm, ffn1_w13, ffn1_w2, mix_norm, w_in, conv_w, conv_b, conv_ln_g, conv_ln_b, forget_b, out_norm_conv, out_norm_attn, w_out, ffn2_norm, ffn2_w13, ffn2_w2, final_norm, loss_target, m_ffn1_norm, m_ffn1_w13, m_ffn1_w2, m_mix_norm, m_w_in, m_conv_w, m_conv_b, m_conv_ln_g, m_conv_ln_b, m_forget_b, m_out_norm_conv, m_out_norm_attn, m_w_out, m_ffn2_norm, m_ffn2_w13, m_ffn2_w2, m_final_norm, v_ffn1_norm, v_ffn1_w13, v_ffn1_w2, v_mix_norm, v_w_in, v_conv_w, v_conv_b, v_conv_ln_g, v_conv_ln_b, v_forget_b, v_out_norm_conv, v_out_norm_attn, v_w_out, v_ffn2_norm, v_ffn2_w13, v_ffn2_w2, v_final_norm):
    args = dict(locals())
    weights = {n: args[n] for n in WEIGHTS}
    shard = {n: weights[n][0] for n in MATRICES}
    core = lax.axis_index("c").astype(jnp.int32).reshape(1)
    chip = (2 * lax.axis_index("x") + lax.axis_index("y")).astype(jnp.int32)
    chip1 = chip.reshape(1)
    chip_core = jnp.concatenate([chip1, core])

    conv_w_rows = jnp.pad(conv_w[0], ((0, CONV_PAD - CONV_WIDTH), (0, 0)))
    slots = [_into_slot(shard[n], chip1, BF16, "slot_" + n) for n in MATRICES]
    gathered = _gather_shards(slots + [_into_slot(conv_w_rows, chip1, F32, "slot_conv_w")])
    w = dict(zip(MATRICES, gathered[:-1]))
    w["ffn1_w2"] = w["ffn1_w2"].reshape(D_FF, D_MODEL)
    w["ffn2_w2"] = w["ffn2_w2"].reshape(D_FF, D_MODEL)
    w["w_out"] = w["w_out"].reshape(D_MODEL, D_MODEL)
    w["w_in"] = jnp.concatenate([w["w_in"][k] for k in range(N_CHIPS)], axis=1)
    p = {n: weights[n] for n in VECTORS}
    p["final_norm"] = final_norm.reshape(1, D_MODEL)
    p["conv_w"] = gathered[-1][:, :CONV_WIDTH].transpose(1, 0, 2).reshape(CONV_WIDTH, D_CONV)

    loss_part, dx, g = _local_step(x[0], loss_target[0], w, p)
    loss = lax.psum(loss_part, ("x", "y", "c"))

    packed, layout = _pack_small(g)
    small = _unpack_small(_all_reduce_small(packed), layout)
    grad = {n: small[n] for n in VECTORS}
    grad["final_norm"] = small["final_norm"].reshape(D_MODEL)
    grad["conv_w"] = lax.dynamic_slice_in_dim(small["conv_w"], chip * (D_CONV // N_CHIPS), D_CONV // N_CHIPS, axis=1)[None]

    local = [g["ffn1_w13"], g["ffn1_w2"].reshape(N_CHIPS, D_FF // N_CHIPS, D_MODEL),
             jnp.stack([g["w_in"][:, k * IN_SHARD:(k + 1) * IN_SHARD] for k in range(N_CHIPS)]),
             g["w_out"].reshape(N_CHIPS, D_MODEL // N_CHIPS, D_MODEL), g["ffn2_w13"],
             g["ffn2_w2"].reshape(N_CHIPS, D_FF // N_CHIPS, D_MODEL)]
    sib = _pair_exchange(local)
    parts = [_pair_add(a, b, core, "pair_add_" + n) for a, b, n in zip(local, sib, MATRICES)]
    recv = _chip_scatter(parts)
    halves = [_chip_add(a, b, chip_core, "chip_add_" + n) for a, b, n in zip(parts, recv, MATRICES)]
    full = _pair_share(halves)
    for n, f in zip(MATRICES, full):
        grad[n] = f.reshape(1, f.shape[0] * f.shape[1], f.shape[2])

    delta, new_m, new_v = {}, {}, {}
    for n in MATRICES:
        d, mo, vo = _adamw_matrix(weights[n][0], grad[n][0], args["m_" + n][0], args["v_" + n][0], "adamw_" + n, ADAM_ROWS[n])
        delta[n], new_m[n], new_v[n] = d[None], mo[None], vo[None]
    small_names = VECTORS + ["conv_w"]
    as2d = lambda a: a.reshape(-1, a.shape[-1])
    ds, mos, vos = _adamw_small([as2d(weights[n]) for n in small_names], [as2d(grad[n]) for n in small_names],
                                [as2d(args["m_" + n]) for n in small_names], [as2d(args["v_" + n]) for n in small_names])
    for n, d, mo, vo in zip(small_names, ds, mos, vos):
        shape = weights[n].shape
        delta[n], new_m[n], new_v[n] = d.reshape(shape), mo.reshape(shape), vo.reshape(shape)

    return (loss, dx[None], *[grad[n] for n in WEIGHTS], *[delta[n] for n in WEIGHTS],
            *[new_m[n] for n in WEIGHTS], *[new_v[n] for n in WEIGHTS])
```

```python
import functools

import jax
import jax.numpy as jnp
from jax import lax
from jax.experimental import pallas as pl
from jax.experimental.pallas import tpu as pltpu

F32 = jnp.float32
BF16 = jnp.bfloat16

D_MODEL = 1024
D_FF = 2816
FF_SHARD = D_FF // 2
D_CONV = 512
D_ATTN = 512
N_HEADS = 8
HEAD_DIM = 64
CONV_WIDTH = 31
CONV_PAD = 32
N_IN = 2 * D_CONV + 3 * D_ATTN + N_HEADS
IN_SHARD = N_IN // 4
EPS = 1e-6
N_CHIPS = 4
LANES = 128
HEAD_ROWS = 16

ADAM_LR = 0.001
ADAM_B1 = 0.9
ADAM_B2 = 0.999
ADAM_EPS = 1e-08
ADAM_WD = 0.01
ADAM_STEP = 10

VMEM_LIMIT = 56 * 1024 * 1024

_NT = (((1,), (1,)), ((), ()))
_TN = (((0,), (0,)), ((), ()))


def _dot(a, b):
    return jnp.dot(a, b, preferred_element_type=F32)


def _dot_nt(a, b):
    return lax.dot_general(a, b, _NT, preferred_element_type=F32)


def _dot_tn(a, b):
    return lax.dot_general(a, b, _TN, preferred_element_type=F32)


def _params(**kw):
    return pltpu.CompilerParams(vmem_limit_bytes=VMEM_LIMIT, **kw)


def _sigmoid(x):
    return 1.0 / (1.0 + jnp.exp(-x))


def _rms_stats(x):
    return lax.rsqrt(jnp.mean(x * x, axis=-1, keepdims=True) + EPS)


def _rms_bwd(x, r, g, dh):
    t = dh * g
    dx = r * t - x * (r * r * r) * jnp.mean(t * x, axis=-1, keepdims=True)
    return dx, dh * x * r


def _silu_grad(z, sg):
    return sg * (1.0 + z * (1.0 - sg))


def _row_spec(tm, n):
    return pl.BlockSpec((tm, n), lambda i: (i, 0))


def _full_spec(shape):
    nd = len(shape)
    return pl.BlockSpec(shape, lambda i: (0,) * nd)


_ANY = pl.BlockSpec(memory_space=pl.ANY)


def _skip(n, body):
    return lambda *refs: body(*refs[n:])


def _ffn_fwd(x, g, w13s, w2, name, deps=()):
    t = x.shape[0]
    tm = 256
    deps = tuple(deps)

    def body(x_ref, g_ref, w13_hbm, w2_hbm, xo_ref, h_ref, gu_ref, w13_ref, w2_ref):
        @pl.when(pl.program_id(0) == 0)
        def _():
            pltpu.sync_copy(w13_hbm, w13_ref)
            pltpu.sync_copy(w2_hbm, w2_ref)

        xv = x_ref[...]
        hb = (xv * _rms_stats(xv) * g_ref[...]).astype(BF16)
        h_ref[...] = hb
        acc = jnp.zeros((tm, D_MODEL), F32)
        for half in range(2):
            lo = half * FF_SHARD
            gate = _dot(hb, w13_ref[half])
            up = _dot(hb, w13_ref[2 + half])
            gu_ref[:, lo:lo + FF_SHARD] = gate.astype(BF16)
            gu_ref[:, D_FF + lo:D_FF + lo + FF_SHARD] = up.astype(BF16)
            a = (gate * _sigmoid(gate) * up).astype(BF16)
            acc = acc + _dot(a, w2_ref[lo:lo + FF_SHARD, :])
        xo_ref[...] = xv + 0.5 * acc

    return pl.pallas_call(
        _skip(len(deps), body), name=name, grid=(t // tm,),
        in_specs=[_ANY] * len(deps) + [_row_spec(tm, D_MODEL), _full_spec((1, D_MODEL)), _ANY, _ANY],
        out_specs=[_row_spec(tm, D_MODEL), _row_spec(tm, D_MODEL), _row_spec(tm, 2 * D_FF)],
        out_shape=[jax.ShapeDtypeStruct((t, D_MODEL), F32), jax.ShapeDtypeStruct((t, D_MODEL), BF16),
                   jax.ShapeDtypeStruct((t, 2 * D_FF), BF16)],
        scratch_shapes=[pltpu.VMEM(w13s.shape, BF16), pltpu.VMEM(w2.shape, BF16)],
        compiler_params=_params(dimension_semantics=("arbitrary",)),
    )(*deps, x, g, w13s, w2)


def _ffn_bwd(dy, x, gu, g, w13s, w2, name, deps=()):
    t = x.shape[0]
    tm = 256
    deps = tuple(deps)

    def body(dy_ref, x_ref, gu_ref, g_ref, w13_hbm, w2_hbm, dx_ref, dgu_ref, a_ref, dg_ref, w13_ref, w2_ref):
        @pl.when(pl.program_id(0) == 0)
        def _():
            pltpu.sync_copy(w13_hbm, w13_ref)
            pltpu.sync_copy(w2_hbm, w2_ref)
            dg_ref[...] = jnp.zeros_like(dg_ref)

        dyv = dy_ref[...]
        dyh = (0.5 * dyv).astype(BF16)
        dh = jnp.zeros((tm, D_MODEL), F32)
        for half in range(2):
            lo = half * FF_SHARD
            da = _dot_nt(dyh, w2_ref[lo:lo + FF_SHARD, :])
            gate = gu_ref[:, lo:lo + FF_SHARD].astype(F32)
            up = gu_ref[:, D_FF + lo:D_FF + lo + FF_SHARD].astype(F32)
            sg = _sigmoid(gate)
            act = gate * sg
            a_ref[:, lo:lo + FF_SHARD] = (act * up).astype(BF16)
            dgate = (da * up * _silu_grad(gate, sg)).astype(BF16)
            dup = (da * act).astype(BF16)
            dgu_ref[:, lo:lo + FF_SHARD] = dgate
            dgu_ref[:, D_FF + lo:D_FF + lo + FF_SHARD] = dup
            dh = dh + _dot_nt(dgate, w13_ref[half]) + _dot_nt(dup, w13_ref[2 + half])
        xv = x_ref[...]
        dxn, dg_rows = _rms_bwd(xv, _rms_stats(xv), g_ref[...], dh)
        dx_ref[...] = dyv + dxn
        dg_ref[...] += jnp.sum(dg_rows, axis=0, keepdims=True)

    return pl.pallas_call(
        _skip(len(deps), body), name=name, grid=(t // tm,),
        in_specs=[_ANY] * len(deps) + [_row_spec(tm, D_MODEL), _row_spec(tm, D_MODEL), _row_spec(tm, 2 * D_FF),
                                       _full_spec((1, D_MODEL)), _ANY, _ANY],
        out_specs=[_row_spec(tm, D_MODEL), _row_spec(tm, 2 * D_FF), _row_spec(tm, D_FF),
                   _full_spec((1, D_MODEL))],
        out_shape=[jax.ShapeDtypeStruct((t, D_MODEL), F32), jax.ShapeDtypeStruct((t, 2 * D_FF), BF16),
                   jax.ShapeDtypeStruct((t, D_FF), BF16), jax.ShapeDtypeStruct((1, D_MODEL), F32)],
        scratch_shapes=[pltpu.VMEM(w13s.shape, BF16), pltpu.VMEM(w2.shape, BF16)],
        compiler_params=_params(dimension_semantics=("arbitrary",)),
    )(*deps, dy, x, gu, g, w13s, w2)


def _wgrad(a, b, n_blocks, name, scale=1.0, tm=256):
    t, m = a.shape
    n = b.shape[1]
    bn = n // n_blocks

    def body(a_ref, b_ref, o_ref):
        bv = b_ref[...]
        if scale != 1.0:
            bv = bv * scale
        o_ref[0] = _dot_tn(a_ref[...].astype(BF16), bv.astype(BF16)).astype(BF16)

    return pl.pallas_call(
        body, name=name, grid=(n_blocks, m // tm),
        in_specs=[pl.BlockSpec((t, tm), lambda j, i: (0, i)), pl.BlockSpec((t, bn), lambda j, i: (0, j))],
        out_specs=pl.BlockSpec((1, tm, bn), lambda j, i: (j, i, 0)),
        out_shape=jax.ShapeDtypeStruct((n_blocks, m, bn), BF16),
        compiler_params=_params(dimension_semantics=("arbitrary", "arbitrary")),
    )(a, b)


def _mix_proj(x, g, w_ag, w_qkv, w_f):
    t = x.shape[0]
    tm = 256

    def body(x_ref, g_ref, wag_ref, wqkv_ref, wf_ref, h_ref, ag_ref, qkv_ref, fl_ref):
        xv = x_ref[...]
        hb = (xv * _rms_stats(xv) * g_ref[...]).astype(BF16)
        h_ref[...] = hb
        ag_ref[...] = _dot(hb, wag_ref[...])
        qkv_ref[...] = _dot(hb, wqkv_ref[...]).astype(BF16)
        fl_ref[...] = _dot(hb, wf_ref[...])

    return pl.pallas_call(
        body, name="mix_proj", grid=(t // tm,),
        in_specs=[_row_spec(tm, D_MODEL), _full_spec((1, D_MODEL)), _full_spec(w_ag.shape),
                  _full_spec(w_qkv.shape), _full_spec(w_f.shape)],
        out_specs=[_row_spec(tm, D_MODEL), _row_spec(tm, 2 * D_CONV), _row_spec(tm, 3 * D_ATTN),
                   _row_spec(tm, LANES)],
        out_shape=[jax.ShapeDtypeStruct((t, D_MODEL), BF16), jax.ShapeDtypeStruct((t, 2 * D_CONV), F32),
                   jax.ShapeDtypeStruct((t, 3 * D_ATTN), BF16), jax.ShapeDtypeStruct((t, LANES), F32)],
        compiler_params=_params(dimension_semantics=("arbitrary",)),
    )(x, g, w_ag, w_qkv, w_f)


def _mix_proj_bwd(dag, dqkv, dfl, dx2, x1, g, w_ag, w_qkv, w_f):
    t = x1.shape[0]
    tm = 256

    def body(dag_ref, dqkv_ref, dfl_ref, dx2_ref, x_ref, g_ref, wag_ref, wqkv_ref, wf_ref, dx_ref, dg_ref):
        @pl.when(pl.program_id(0) == 0)
        def _():
            dg_ref[...] = jnp.zeros_like(dg_ref)

        dh = (_dot_nt(dag_ref[...].astype(BF16), wag_ref[...]) + _dot_nt(dqkv_ref[...], wqkv_ref[...])
              + _dot_nt(dfl_ref[...].astype(BF16), wf_ref[...]))
        xv = x_ref[...]
        dxn, dg_rows = _rms_bwd(xv, _rms_stats(xv), g_ref[...], dh)
        dx_ref[...] = dx2_ref[...] + dxn
        dg_ref[...] += jnp.sum(dg_rows, axis=0, keepdims=True)

    return pl.pallas_call(
        body, name="mix_proj_bwd", grid=(t // tm,),
        in_specs=[_row_spec(tm, 2 * D_CONV), _row_spec(tm, 3 * D_ATTN), _row_spec(tm, LANES),
                  _row_spec(tm, D_MODEL), _row_spec(tm, D_MODEL), _full_spec((1, D_MODEL)),
                  _full_spec(w_ag.shape), _full_spec(w_qkv.shape), _full_spec(w_f.shape)],
        out_specs=[_row_spec(tm, D_MODEL), _full_spec((1, D_MODEL))],
        out_shape=[jax.ShapeDtypeStruct((t, D_MODEL), F32), jax.ShapeDtypeStruct((1, D_MODEL), F32)],
        compiler_params=_params(dimension_semantics=("arbitrary",)),
    )(dag, dqkv, dfl, dx2, x1, g, w_ag, w_qkv, w_f)


def _split3(x):
    hi = x.astype(BF16)
    r1 = x - hi.astype(F32)
    mid = r1.astype(BF16)
    lo = (r1 - mid.astype(F32)).astype(BF16)
    return hi, mid, lo


def _gates_fwd(flt, fb):
    t = flt.shape[1]

    def body(f_ref, b_ref, d_ref):
        z = f_ref[...] + b_ref[...]
        logf = jnp.minimum(z, 0.0) - jnp.log(1.0 + jnp.exp(-jnp.abs(z)))
        row = lax.broadcasted_iota(jnp.int32, (LANES, LANES), 0)
        col = lax.broadcasted_iota(jnp.int32, (LANES, LANES), 1)
        upper = (row <= col).astype(BF16)
        carry = jnp.zeros((HEAD_ROWS, 1), F32)
        for blk in range(t // LANES):
            hi, mid, lo = _split3(logf[:, blk * LANES:(blk + 1) * LANES])
            cs = _dot(hi, upper) + _dot(mid, upper) + _dot(lo, upper)
            d_ref[:, blk * LANES:(blk + 1) * LANES] = cs + carry
            carry = carry + cs[:, LANES - 1:LANES]

    return pl.pallas_call(
        body, name="gates_fwd", out_shape=jax.ShapeDtypeStruct((HEAD_ROWS, t), F32),
        compiler_params=_params(),
    )(flt, fb)


def _gates_bwd(dd, flt, fb):
    t = flt.shape[1]

    def body(dd_ref, f_ref, b_ref, df_ref, db_ref):
        z = f_ref[...] + b_ref[...]
        row = lax.broadcasted_iota(jnp.int32, (LANES, LANES), 0)
        col = lax.broadcasted_iota(jnp.int32, (LANES, LANES), 1)
        lower = (row >= col).astype(BF16)
        carry = jnp.zeros((HEAD_ROWS, 1), F32)
        db = jnp.zeros((HEAD_ROWS, 1), F32)
        for blk in reversed(range(t // LANES)):
            sl = slice(blk * LANES, (blk + 1) * LANES)
            hi, mid, lo = _split3(dd_ref[:, sl])
            cs = _dot(hi, lower) + _dot(mid, lower) + _dot(lo, lower)
            dz = (cs + carry) * _sigmoid(-z[:, sl])
            df_ref[:, sl] = dz
            db = db + jnp.sum(dz, axis=1, keepdims=True)
            carry = carry + cs[:, 0:1]
        db_ref[...] = db

    return pl.pallas_call(
        body, name="gates_bwd",
        out_shape=[jax.ShapeDtypeStruct((HEAD_ROWS, t), F32), jax.ShapeDtypeStruct((HEAD_ROWS, 1), F32)],
        compiler_params=_params(),
    )(dd, flt, fb)


CONV_CHUNK = 64
CONV_TAIL = 16
CONV_WINDOW = CONV_CHUNK + CONV_PAD + 8
CONV_ROWS_EXTRA = CONV_PAD + CONV_TAIL
SUBLANES = 8


def _conv_rows(ag_ref, u_ref, t):
    u_ref[0:CONV_PAD, :] = jnp.zeros((CONV_PAD, D_CONV), F32)
    u_ref[CONV_PAD + t:CONV_ROWS_EXTRA + t, :] = jnp.zeros((CONV_TAIL, D_CONV), F32)

    def fill(i, c):
        r0 = pl.multiple_of(i * CONV_CHUNK, CONV_CHUNK)
        a = ag_ref[pl.ds(r0, CONV_CHUNK), 0:D_CONV]
        gt = ag_ref[pl.ds(r0, CONV_CHUNK), D_CONV:2 * D_CONV]
        u_ref[pl.ds(CONV_PAD + r0, CONV_CHUNK), :] = a * _sigmoid(gt)
        return c

    lax.fori_loop(0, t // CONV_CHUNK, fill, 0)


def _for_shifted(ref, r0, offsets, fn):
    window = ref[pl.ds(r0, CONV_WINDOW), :]
    for rem in range(SUBLANES):
        mine = [o for o in offsets if o % SUBLANES == rem]
        if not mine:
            continue
        turned = window if rem == 0 else pltpu.roll(window, CONV_WINDOW - rem, 0)
        for o in mine:
            fn(o, turned[o - rem:o - rem + CONV_CHUNK])


def _conv_point(u_ref, r0, w_ref, cb, lg, lb):
    acc = [jnp.zeros((CONV_CHUNK, D_CONV), F32)]

    def tap(o, rows):
        j = o - (CONV_PAD - CONV_WIDTH + 1)
        acc[0] = acc[0] + w_ref[j:j + 1, :] * rows

    _for_shifted(u_ref, r0, [j + CONV_PAD - CONV_WIDTH + 1 for j in range(CONV_WIDTH)], tap)
    y = acc[0] + cb
    mu = jnp.mean(y, axis=-1, keepdims=True)
    yc = y - mu
    rstd = lax.rsqrt(jnp.mean(yc * yc, axis=-1, keepdims=True) + EPS)
    yhat = yc * rstd
    z = yhat * lg + lb
    sg = _sigmoid(z)
    s = z * sg
    rr = _rms_stats(s)
    return yhat, rstd, z, sg, s, rr


def _conv_fwd(ag, conv_w, conv_b, ln_g, ln_b, norm_g):
    t = ag.shape[0]

    def body(ag_ref, w_ref, cb_ref, lg_ref, lb_ref, ng_ref, o_ref, u_ref):
        _conv_rows(ag_ref, u_ref, t)
        cb, lg, lb, ng = cb_ref[...], lg_ref[...], lb_ref[...], ng_ref[...]

        def chunk(i, c):
            r0 = pl.multiple_of(i * CONV_CHUNK, CONV_CHUNK)
            _, _, _, _, s, rr = _conv_point(u_ref, r0, w_ref, cb, lg, lb)
            o_ref[pl.ds(r0, CONV_CHUNK), :] = (s * rr * ng).astype(BF16)
            return c

        lax.fori_loop(0, t // CONV_CHUNK, chunk, 0)

    return pl.pallas_call(
        body, name="conv_fwd", out_shape=jax.ShapeDtypeStruct((t, D_CONV), BF16),
        scratch_shapes=[pltpu.VMEM((t + CONV_ROWS_EXTRA, D_CONV), F32)],
        compiler_params=_params(),
    )(ag, conv_w, conv_b, ln_g, ln_b, norm_g)


def _conv_bwd(ag, dout, conv_w, conv_b, ln_g, ln_b, norm_g):
    t = ag.shape[0]

    def body(ag_ref, do_ref, w_ref, cb_ref, lg_ref, lb_ref, ng_ref,
             dag_ref, dw_ref, dcb_ref, dlg_ref, dlb_ref, dng_ref, u_ref, dy_ref):
        _conv_rows(ag_ref, u_ref, t)
        dy_ref[t:t + CONV_ROWS_EXTRA, :] = jnp.zeros((CONV_ROWS_EXTRA, D_CONV), F32)
        cb, lg, lb, ng = cb_ref[...], lg_ref[...], lb_ref[...], ng_ref[...]
        dw_ref[...] = jnp.zeros_like(dw_ref)
        zero = jnp.zeros((1, D_CONV), F32)

        def chunk(i, carry):
            dcb, dlg, dlb, dng = carry
            r0 = pl.multiple_of(i * CONV_CHUNK, CONV_CHUNK)
            yhat, rstd, z, sg, s, rr = _conv_point(u_ref, r0, w_ref, cb, lg, lb)
            do = do_ref[pl.ds(r0, CONV_CHUNK), :]
            ds, dng_rows = _rms_bwd(s, rr, ng, do)
            dz = ds * _silu_grad(z, sg)
            dyhat = dz * lg
            dy = rstd * (dyhat - jnp.mean(dyhat, axis=-1, keepdims=True)
                         - yhat * jnp.mean(dyhat * yhat, axis=-1, keepdims=True))
            dy_ref[pl.ds(r0, CONV_CHUNK), :] = dy
            def tap(o, rows):
                j = o - (CONV_PAD - CONV_WIDTH + 1)
                dw_ref[j:j + 1, :] += jnp.sum(dy * rows, axis=0, keepdims=True)

            _for_shifted(u_ref, r0, [j + CONV_PAD - CONV_WIDTH + 1 for j in range(CONV_WIDTH)], tap)
            return (dcb + jnp.sum(dy, axis=0, keepdims=True), dlg + jnp.sum(dz * yhat, axis=0, keepdims=True),
                    dlb + jnp.sum(dz, axis=0, keepdims=True), dng + jnp.sum(dng_rows, axis=0, keepdims=True))

        dcb, dlg, dlb, dng = lax.fori_loop(0, t // CONV_CHUNK, chunk, (zero, zero, zero, zero))
        dcb_ref[...] = dcb
        dlg_ref[...] = dlg
        dlb_ref[...] = dlb
        dng_ref[...] = dng

        def chunk2(i, c):
            r0 = pl.multiple_of(i * CONV_CHUNK, CONV_CHUNK)
            acc = [jnp.zeros((CONV_CHUNK, D_CONV), F32)]

            def tap(o, rows):
                j = CONV_WIDTH - 1 - o
                acc[0] = acc[0] + w_ref[j:j + 1, :] * rows

            _for_shifted(dy_ref, r0, list(range(CONV_WIDTH)), tap)
            du = acc[0]
            a = ag_ref[pl.ds(r0, CONV_CHUNK), 0:D_CONV]
            gt = ag_ref[pl.ds(r0, CONV_CHUNK), D_CONV:2 * D_CONV]
            sg = _sigmoid(gt)
            dag_ref[pl.ds(r0, CONV_CHUNK), 0:D_CONV] = du * sg
            dag_ref[pl.ds(r0, CONV_CHUNK), D_CONV:2 * D_CONV] = du * a * sg * (1.0 - sg)
            return c

        lax.fori_loop(0, t // CONV_CHUNK, chunk2, 0)

    vec = jax.ShapeDtypeStruct((1, D_CONV), F32)
    return pl.pallas_call(
        body, name="conv_bwd",
        out_shape=[jax.ShapeDtypeStruct((t, 2 * D_CONV), F32), jax.ShapeDtypeStruct((CONV_PAD, D_CONV), F32),
                   vec, vec, vec, vec],
        scratch_shapes=[pltpu.VMEM((t + CONV_ROWS_EXTRA, D_CONV), F32), pltpu.VMEM((t + CONV_ROWS_EXTRA, D_CONV), F32)],
        compiler_params=_params(),
    )(ag, dout, conv_w, conv_b, ln_g, ln_b, norm_g)


Q_ROWS = 256
ATTN_SCALE = HEAD_DIM ** -0.5


def _attn_specs(t):
    blk = lambda off: pl.BlockSpec((t, LANES), lambda p: (0, off + p))
    pairs = N_HEADS // 2
    return [blk(0), blk(pairs), blk(2 * pairs),
            pl.BlockSpec((2, t, 1), lambda p: (p, 0, 0)), pl.BlockSpec((2, 1, t), lambda p: (p, 0, 0))]


def _attn_scores(qm, k2, dcol, drow, r0, q1):
    s = _dot_nt(qm, k2) * ATTN_SCALE + dcol - drow
    rowi = lax.broadcasted_iota(jnp.int32, (q1 - r0, q1), 0) + r0
    coli = lax.broadcasted_iota(jnp.int32, (q1 - r0, q1), 1)
    return s, coli <= rowi


def _attn_fwd(qkv, dcol, drow):
    t = qkv.shape[0]

    def body(q_ref, k_ref, v_ref, dc_ref, dr_ref, o_ref, lse_ref):
        head_a = lax.broadcasted_iota(jnp.int32, (1, LANES), 1) < HEAD_DIM
        for qb in range(t // Q_ROWS):
            r0, q1 = qb * Q_ROWS, (qb + 1) * Q_ROWS
            q2 = q_ref[r0:q1, :]
            k2 = k_ref[0:q1, :]
            v2 = v_ref[0:q1, :]
            outs = []
            for hh in range(2):
                qm = jnp.where(head_a if hh == 0 else ~head_a, q2, jnp.zeros_like(q2))
                s, keep = _attn_scores(qm, k2, dc_ref[hh, r0:q1, :], dr_ref[hh, :, 0:q1], r0, q1)
                s = jnp.where(keep, s, -jnp.inf)
                mx = jnp.max(s, axis=1, keepdims=True)
                p = jnp.exp(s - mx)
                l = jnp.sum(p, axis=1, keepdims=True)
                lse_ref[hh, r0:q1, :] = mx + jnp.log(l)
                outs.append(_dot((p * (1.0 / l)).astype(BF16), v2))
            o_ref[r0:q1, :] = jnp.where(head_a, outs[0], outs[1])

    pairs = N_HEADS // 2
    return pl.pallas_call(
        body, name="attn_fwd", grid=(pairs,), in_specs=_attn_specs(t),
        out_specs=[pl.BlockSpec((t, LANES), lambda p: (0, p)), pl.BlockSpec((2, t, 1), lambda p: (p, 0, 0))],
        out_shape=[jax.ShapeDtypeStruct((t, D_ATTN), F32), jax.ShapeDtypeStruct((N_HEADS, t, 1), F32)],
        compiler_params=_params(dimension_semantics=("arbitrary",)),
    )(qkv, qkv, qkv, dcol, drow)


def _attn_bwd(qkv, dcol, drow, lse, do):
    t = qkv.shape[0]

    def body(q_ref, k_ref, v_ref, dc_ref, dr_ref, lse_ref, do_ref,
             dq_ref, dk_ref, dv_ref, dd_ref, dk_acc, dv_acc):
        head_a = lax.broadcasted_iota(jnp.int32, (1, LANES), 1) < HEAD_DIM
        dk_acc[...] = jnp.zeros_like(dk_acc)
        dv_acc[...] = jnp.zeros_like(dv_acc)
        dd_ref[...] = jnp.zeros_like(dd_ref)
        for qb in range(t // Q_ROWS):
            r0, q1 = qb * Q_ROWS, (qb + 1) * Q_ROWS
            q2 = q_ref[r0:q1, :]
            k2 = k_ref[0:q1, :]
            v2 = v_ref[0:q1, :]
            do2 = do_ref[r0:q1, :]
            dqs = []
            dk_sum = jnp.zeros((q1, LANES), F32)
            dv_sum = jnp.zeros((q1, LANES), F32)
            for hh in range(2):
                mask = head_a if hh == 0 else ~head_a
                qm = jnp.where(mask, q2, jnp.zeros_like(q2))
                dob = jnp.where(mask, do2, 0.0).astype(BF16)
                s, keep = _attn_scores(qm, k2, dc_ref[hh, r0:q1, :], dr_ref[hh, :, 0:q1], r0, q1)
                p = jnp.where(keep, jnp.exp(s - lse_ref[hh, r0:q1, :]), 0.0)
                dp = _dot_nt(dob, v2)
                ds = p * (dp - jnp.sum(p * dp, axis=1, keepdims=True))
                dsb = ds.astype(BF16)
                dqs.append(_dot(dsb, k2) * ATTN_SCALE)
                dk_sum = dk_sum + _dot_tn(dsb, qm)
                dv_sum = dv_sum + _dot_tn(p.astype(BF16), dob)
                dd_ref[hh, :, 0:q1] -= jnp.sum(ds, axis=0, keepdims=True)
            dq_ref[r0:q1, :] = jnp.where(head_a, dqs[0], dqs[1]).astype(BF16)
            dk_acc[0:q1, :] += dk_sum * ATTN_SCALE
            dv_acc[0:q1, :] += dv_sum
        dk_ref[...] = dk_acc[...].astype(BF16)
        dv_ref[...] = dv_acc[...].astype(BF16)

    pairs = N_HEADS // 2
    col = pl.BlockSpec((t, LANES), lambda p: (0, p))
    grad = jax.ShapeDtypeStruct((t, D_ATTN), BF16)
    return pl.pallas_call(
        body, name="attn_bwd", grid=(pairs,),
        in_specs=_attn_specs(t) + [pl.BlockSpec((2, t, 1), lambda p: (p, 0, 0)), col],
        out_specs=[col, col, col, pl.BlockSpec((2, 1, t), lambda p: (p, 0, 0))],
        out_shape=[grad, grad, grad, jax.ShapeDtypeStruct((N_HEADS, 1, t), F32)],
        scratch_shapes=[pltpu.VMEM((t, LANES), F32), pltpu.VMEM((t, LANES), F32)],
        compiler_params=_params(dimension_semantics=("arbitrary",)),
    )(qkv, qkv, qkv, dcol, drow, lse, do)


def _out_proj(ycn, o, g_attn, w_out, x1):
    t = x1.shape[0]
    tm = 256

    def body(yc_ref, o_ref, g_ref, w_ref, x_ref, xo_ref, ya_ref):
        ov = o_ref[...]
        ya = (ov * _rms_stats(ov) * g_ref[...]).astype(BF16)
        ya_ref[...] = ya
        xo_ref[...] = x_ref[...] + _dot(yc_ref[...], w_ref[0:D_CONV, :]) + _dot(ya, w_ref[D_CONV:, :])

    return pl.pallas_call(
        body, name="out_proj", grid=(t // tm,),
        in_specs=[_row_spec(tm, D_CONV), _row_spec(tm, D_ATTN), _full_spec((1, D_ATTN)),
                  _full_spec(w_out.shape), _row_spec(tm, D_MODEL)],
        out_specs=[_row_spec(tm, D_MODEL), _row_spec(tm, D_ATTN)],
        out_shape=[jax.ShapeDtypeStruct((t, D_MODEL), F32), jax.ShapeDtypeStruct((t, D_ATTN), BF16)],
        compiler_params=_params(dimension_semantics=("arbitrary",)),
    )(ycn, o, g_attn, w_out, x1)


def _out_proj_bwd(dx2, o, g_attn, w_out, deps=()):
    t = dx2.shape[0]
    tm = 256
    deps = tuple(deps)

    def body(dx_ref, o_ref, g_ref, w_ref, dyc_ref, do_ref, dg_ref):
        @pl.when(pl.program_id(0) == 0)
        def _():
            dg_ref[...] = jnp.zeros_like(dg_ref)

        dxb = dx_ref[...].astype(BF16)
        dyc_ref[...] = _dot_nt(dxb, w_ref[0:D_CONV, :])
        dya = _dot_nt(dxb, w_ref[D_CONV:, :])
        ov = o_ref[...]
        do, dg_rows = _rms_bwd(ov, _rms_stats(ov), g_ref[...], dya)
        do_ref[...] = do
        dg_ref[...] += jnp.sum(dg_rows, axis=0, keepdims=True)

    return pl.pallas_call(
        _skip(len(deps), body), name="out_proj_bwd", grid=(t // tm,),
        in_specs=[_ANY] * len(deps) + [_row_spec(tm, D_MODEL), _row_spec(tm, D_ATTN), _full_spec((1, D_ATTN)),
                                       _full_spec(w_out.shape)],
        out_specs=[_row_spec(tm, D_CONV), _row_spec(tm, D_ATTN), _full_spec((1, D_ATTN))],
        out_shape=[jax.ShapeDtypeStruct((t, D_CONV), F32), jax.ShapeDtypeStruct((t, D_ATTN), F32),
                   jax.ShapeDtypeStruct((1, D_ATTN), F32)],
        compiler_params=_params(dimension_semantics=("arbitrary",)),
    )(*deps, dx2, o, g_attn, w_out)


def _loss_bwd(x3, target, g):
    t = x3.shape[0]
    tm = 256

    def body(x_ref, t_ref, g_ref, loss_ref, dx_ref, dg_ref):
        @pl.when(pl.program_id(0) == 0)
        def _():
            loss_ref[...] = jnp.zeros_like(loss_ref)
            dg_ref[...] = jnp.zeros_like(dg_ref)

        xv = x_ref[...]
        r = _rms_stats(xv)
        gv = g_ref[...]
        err = xv * r * gv - t_ref[...]
        row = jnp.sum(err * err, axis=1, keepdims=True) * (0.5 / D_MODEL)
        loss_ref[...] += jnp.sum(row, axis=0, keepdims=True)
        dx, dg_rows = _rms_bwd(xv, r, gv, err * (1.0 / D_MODEL))
        dx_ref[...] = dx
        dg_ref[...] += jnp.sum(dg_rows, axis=0, keepdims=True)

    return pl.pallas_call(
        body, name="loss_bwd", grid=(t // tm,),
        in_specs=[_row_spec(tm, D_MODEL), _row_spec(tm, D_MODEL), _full_spec((1, D_MODEL))],
        out_specs=[_full_spec((1, LANES)), _row_spec(tm, D_MODEL), _full_spec((1, D_MODEL))],
        out_shape=[jax.ShapeDtypeStruct((1, LANES), F32), jax.ShapeDtypeStruct((t, D_MODEL), F32),
                   jax.ShapeDtypeStruct((1, D_MODEL), F32)],
        compiler_params=_params(dimension_semantics=("arbitrary",)),
    )(x3, target, g)


def _split_w_in(w_in):
    w_ag = w_in[:, :2 * D_CONV]
    w_qkv = w_in[:, 2 * D_CONV:2 * D_CONV + 3 * D_ATTN]
    w_f = jnp.pad(w_in[:, 2 * D_CONV + 3 * D_ATTN:], ((0, 0), (0, LANES - N_HEADS)))
    return w_ag, w_qkv, w_f


def _head_rows(v):
    return jnp.pad(v, ((0, HEAD_ROWS - N_HEADS),) + ((0, 0),) * (v.ndim - 1))


def _local_step(x, target, p, get_weights, put_grads):
    t = x.shape[0]
    fb = _head_rows(p["forget_b"].reshape(N_HEADS, 1))

    w, deps = get_weights("ffn1", None)
    x1, h1, gu1 = _ffn_fwd(x, p["ffn1_norm"], w["ffn1_w13"], w["ffn1_w2"], "ffn1_fwd", deps)
    wm, _ = get_weights("mix", x1)
    w.update(wm)
    w_ag, w_qkv, w_f = _split_w_in(w["w_in"])
    conv_w = jnp.pad(w["conv_w"], ((0, CONV_PAD - CONV_WIDTH), (0, 0)))
    h2, ag, qkv, fl = _mix_proj(x1, p["mix_norm"], w_ag, w_qkv, w_f)
    flt = _head_rows(fl[:, :N_HEADS].T)
    dcum = _gates_fwd(flt, fb)[:N_HEADS]
    dcol, drow = dcum.reshape(N_HEADS, t, 1), dcum.reshape(N_HEADS, 1, t)
    ycn = _conv_fwd(ag, conv_w, p["conv_b"], p["conv_ln_g"], p["conv_ln_b"], p["out_norm_conv"])
    o, lse = _attn_fwd(qkv, dcol, drow)
    x2, yan = _out_proj(ycn, o, p["out_norm_attn"], w["w_out"], x1)
    w2, _ = get_weights("ffn2", x2)
    w.update(w2)
    x3, h3, gu2 = _ffn_fwd(x2, p["ffn2_norm"], w["ffn2_w13"], w["ffn2_w2"], "ffn2_fwd")
    loss, dx3, d_final = _loss_bwd(x3, target, p["final_norm"])

    g = {}
    dx2, dgu2, a2, g["ffn2_norm"] = _ffn_bwd(dx3, x2, gu2, p["ffn2_norm"], w["ffn2_w13"], w["ffn2_w2"], "ffn2_bwd")
    dw13 = _wgrad(h3, dgu2, N_CHIPS, "ffn2_dw13")
    dw2 = _wgrad(a2, dx3, 1, "ffn2_dw2", scale=0.5).reshape(D_FF, D_MODEL)
    deps = put_grads("ffn2", {"ffn2_w13": dw13, "ffn2_w2": dw2}, dx2)
    dyc, do, g["out_norm_attn"] = _out_proj_bwd(dx2, o, p["out_norm_attn"], w["w_out"], deps)
    dw_out = _wgrad(jnp.concatenate([ycn, yan], axis=1), dx2, 1, "dw_out").reshape(D_MODEL, D_MODEL)
    dq, dk, dv, ddrow = _attn_bwd(qkv, dcol, drow, lse, do)
    dflt, dfb = _gates_bwd(_head_rows(ddrow.reshape(N_HEADS, t)), flt, fb)
    g["forget_b"] = dfb[:N_HEADS, 0].reshape(1, N_HEADS)
    dfl = jnp.pad(dflt[:N_HEADS].T, ((0, 0), (0, LANES - N_HEADS)))
    dag, dconv_w, g["conv_b"], g["conv_ln_g"], g["conv_ln_b"], g["out_norm_conv"] = _conv_bwd(
        ag, dyc, conv_w, p["conv_b"], p["conv_ln_g"], p["conv_ln_b"], p["out_norm_conv"])
    g["conv_w"] = dconv_w[:CONV_WIDTH]
    dqkv = jnp.concatenate([dq, dk, dv], axis=1)
    dx1, g["mix_norm"] = _mix_proj_bwd(dag, dqkv, dfl, dx2, x1, p["mix_norm"], w_ag, w_qkv, w_f)
    dproj = jnp.concatenate([dag.astype(BF16), dqkv, dfl.astype(BF16)], axis=1)
    dw_in = _wgrad(h2, dproj, 1, "dw_in").reshape(D_MODEL, dproj.shape[1])[:, :N_IN]
    deps = put_grads("mix", {"w_in": dw_in, "w_out": dw_out}, dx1)
    dx0, dgu1, a1, g["ffn1_norm"] = _ffn_bwd(dx1, x, gu1, p["ffn1_norm"], w["ffn1_w13"], w["ffn1_w2"], "ffn1_bwd", deps)
    dw13 = _wgrad(h1, dgu1, N_CHIPS, "ffn1_dw13")
    dw2 = _wgrad(a1, dx1, 1, "ffn1_dw2", scale=0.5).reshape(D_FF, D_MODEL)
    put_grads("ffn1", {"ffn1_w13": dw13, "ffn1_w2": dw2}, dx0)
    g["final_norm"] = d_final
    return loss[0, 0], dx0, g


MESH = pl.DeviceIdType.MESH


def _place():
    x, y, c = lax.axis_index("x"), lax.axis_index("y"), lax.axis_index("c")
    chips = [(1 - x, y), (x, 1 - y), (1 - x, 1 - y)]
    return x, y, c, chips


def _hbm_out(shape, dtype):
    return jax.ShapeDtypeStruct(shape, dtype)


def _comm_call(body, name, ins, out_shapes, n_remote, in_place=False):
    return pl.pallas_call(
        body, name=name, in_specs=[_ANY] * len(ins), out_specs=[_ANY] * len(out_shapes), out_shape=out_shapes,
        scratch_shapes=[pltpu.SemaphoreType.DMA((n_remote,)), pltpu.SemaphoreType.DMA((n_remote,))],
        input_output_aliases={i: i for i in range(len(ins))} if in_place else {},
    )(*ins)


def _remote(src, dst, sems, n, to):
    send_sems, recv_sems = sems
    return pltpu.make_async_remote_copy(src_ref=src, dst_ref=dst, send_sem=send_sems.at[n], recv_sem=recv_sems.at[n],
                                        device_id=to, device_id_type=MESH)


def _into_slot(shard, chip, dtype, name):
    rows, cols = shard.shape
    tr = rows // 2

    def body(k_ref, s_ref, o_ref):
        o_ref[0] = s_ref[...].astype(dtype)

    return pl.pallas_call(
        body, name=name,
        grid_spec=pltpu.PrefetchScalarGridSpec(
            num_scalar_prefetch=1, grid=(rows // tr,),
            in_specs=[pl.BlockSpec((tr, cols), lambda i, k_ref: (i, 0))],
            out_specs=pl.BlockSpec((1, tr, cols), lambda i, k_ref: (k_ref[0], i, 0))),
        out_shape=jax.ShapeDtypeStruct((N_CHIPS, rows, cols), dtype),
        compiler_params=_params(dimension_semantics=("arbitrary",)),
    )(chip, shard)


def _gather_shards(slots, name, ici=True):
    n = len(slots)

    def body(*refs):
        outs = refs[n:2 * n]
        sems = refs[2 * n:2 * n + 2]
        x, y, c, chips = _place()
        me = 2 * x + y
        sibling = (x, y, 1 - c)

        def half(i, chip_index, core):
            hr = slots[i].shape[1] // 2
            return outs[i].at[chip_index, pl.ds(core * hr, hr), :]

        sends = []
        if ici:
            for i in range(n):
                for j, chip in enumerate(chips):
                    cp = _remote(half(i, me, c), half(i, me, c), sems, 6 * i + j, (*chip, c))
                    cp.start()
                    sends.append(cp)
        for i in range(n):
            for j, chip in enumerate(chips):
                src_chip = 2 * chip[0] + chip[1]
                landed = half(i, src_chip, c)
                if ici:
                    _remote(landed, landed, sems, 6 * i + j, (*chip, c)).wait_recv()
                cp = _remote(landed, landed, sems, 6 * i + 3 + j, sibling)
                cp.start()
                sends.append(cp)
        for i in range(n):
            for j, chip in enumerate(chips):
                src_chip = 2 * chip[0] + chip[1]
                landed = half(i, src_chip, 1 - c)
                _remote(landed, landed, sems, 6 * i + 3 + j, sibling).wait_recv()
        for cp in sends:
            cp.wait_send()

    outs = [_hbm_out(s.shape, s.dtype) for s in slots]
    return _comm_call(body, name, slots, outs, 6 * n, in_place=True)


_HBM = pl.BlockSpec(memory_space=pltpu.HBM)
_SEM = pl.BlockSpec(memory_space=pltpu.SEMAPHORE)
_DATAFLOW = pltpu.SideEffectType.DATAFLOW_SIDE_EFFECTING


def _split_copy_start(name, bufs, n_copies, plan):
    n = len(bufs)

    def body(*refs):
        for send, _ in plan(refs[:n], (refs[n], refs[n + 1])):
            send.start()
        token = refs[-1]
        token[...] = jnp.zeros_like(token)

    out = pl.pallas_call(
        body, name=name,
        out_shape=(pltpu.SemaphoreType.DMA((n_copies,)), pltpu.SemaphoreType.DMA((n_copies,)),
                   *[pltpu.HBM(b.shape, b.dtype) for b in bufs], jax.ShapeDtypeStruct((8, LANES), F32)),
        in_specs=[_HBM] * n, out_specs=(_SEM, _SEM, *[_HBM] * n, pl.BlockSpec(memory_space=pltpu.VMEM)),
        input_output_aliases={i: 2 + i for i in range(n)},
        compiler_params=pltpu.CompilerParams(has_side_effects=_DATAFLOW),
    )(*[pltpu.with_memory_space_constraint(b, pltpu.HBM) for b in bufs])
    return out[0], out[1], list(out[2:2 + n]), out[-1]


def _split_copy_wait(name, started, plan, after):
    send_sems, recv_sems, bufs, _ = started
    n = len(bufs)

    def body(*refs):
        for send, recv in plan(refs[:n], (refs[n], refs[n + 1])):
            send.wait_send()
            recv.wait_recv()

    out = pl.pallas_call(
        body, name=name, out_shape=tuple(pltpu.HBM(b.shape, b.dtype) for b in bufs),
        in_specs=[_HBM] * n + [_SEM, _SEM, _ANY], out_specs=tuple([_HBM] * n),
        input_output_aliases={i: i for i in range(n)},
        compiler_params=pltpu.CompilerParams(has_side_effects=_DATAFLOW),
    )(*bufs, send_sems, recv_sems, after)
    return list(out)


def _ici_gather_plan(slots):
    def plan(refs, sems):
        x, y, c, chips = _place()
        me = 2 * x + y
        copies = []
        for i, ref in enumerate(refs):
            hr = slots[i].shape[1] // 2
            for j, chip in enumerate(chips):
                mine = ref.at[me, pl.ds(c * hr, hr), :]
                theirs = ref.at[2 * chip[0] + chip[1], pl.ds(c * hr, hr), :]
                to = (*chip, c)
                copies.append((_remote(mine, mine, sems, 3 * i + j, to), _remote(theirs, theirs, sems, 3 * i + j, to)))
        return copies

    return plan


def _ici_scatter_plan(n):
    def plan(refs, sems):
        x, y, c, chips = _place()
        copies = []
        for i in range(n):
            for j, chip in enumerate(chips):
                cp = _remote(refs[i].at[2 * chip[0] + chip[1]], refs[n + i].at[j], sems, 3 * i + j, (*chip, c))
                copies.append((cp, cp))
        return copies

    return plan


def _pair_exchange(grads, name):
    n = len(grads)

    def body(*refs):
        ins, outs = refs[:n], refs[n:2 * n]
        sems = refs[2 * n:2 * n + 2]
        x, y, c, _ = _place()
        sibling = (x, y, 1 - c)
        sends = []
        for i in range(n):
            hr = grads[i].shape[1] // 2
            cp = _remote(ins[i].at[:, pl.ds((1 - c) * hr, hr), :], outs[i], sems, i, sibling)
            cp.start()
            sends.append(cp)
        for cp in sends:
            cp.wait()

    outs = [_hbm_out((N_CHIPS, g.shape[1] // 2, g.shape[2]), g.dtype) for g in grads]
    return _comm_call(body, name, grads, outs, n)


def _pair_share(halves, name):
    n = len(halves)

    def body(*refs):
        outs = refs[n:2 * n]
        sems = refs[2 * n:2 * n + 2]
        x, y, c, _ = _place()
        sibling = (x, y, 1 - c)
        sends = [_remote(outs[i].at[c], outs[i].at[c], sems, i, sibling) for i in range(n)]
        for cp in sends:
            cp.start()
        for cp in sends:
            cp.wait_send()
        for i in range(n):
            _remote(outs[i].at[1 - c], outs[i].at[1 - c], sems, i, sibling).wait_recv()

    outs = [_hbm_out(h.shape, h.dtype) for h in halves]
    return _comm_call(body, name, halves, outs, n, in_place=True)


def _all_reduce_small(v, deps=()):
    rows = v.shape[0]
    flips = [(fx, fy, fc) for fx in range(2) for fy in range(2) for fc in range(2)][1:]

    def body(v_ref, o_ref, slots, send_sems, recv_sems):
        x, y, c, _ = _place()
        me = 4 * x + 2 * y + c
        slots[me] = v_ref[...]
        sends = []
        for n, (fx, fy, fc) in enumerate(flips):
            to = (x ^ fx, y ^ fy, c ^ fc)
            cp = _remote(v_ref, slots.at[me], (send_sems, recv_sems), n, to)
            cp.start()
            sends.append(cp)
        for n, (fx, fy, fc) in enumerate(flips):
            src = 4 * (x ^ fx) + 2 * (y ^ fy) + (c ^ fc)
            _remote(v_ref, slots.at[src], (send_sems, recv_sems), n, (x ^ fx, y ^ fy, c ^ fc)).wait_recv()
        for cp in sends:
            cp.wait_send()
        acc = slots[0]
        for s in range(1, 8):
            acc = acc + slots[s]
        o_ref[...] = acc

    deps = tuple(deps)
    return pl.pallas_call(
        _skip(len(deps), body), name="all_reduce_small", out_shape=jax.ShapeDtypeStruct(v.shape, F32),
        in_specs=[_ANY] * len(deps) + [pl.BlockSpec(memory_space=pltpu.VMEM)],
        out_specs=pl.BlockSpec(memory_space=pltpu.VMEM),
        scratch_shapes=[pltpu.VMEM((8, rows, LANES), F32), pltpu.SemaphoreType.DMA((7,)), pltpu.SemaphoreType.DMA((7,))],
    )(*deps, v)


def _pair_add(g, sib, core, name):
    _, r, cols = g.shape
    hr = r // 2
    g4 = g.reshape(N_CHIPS, 2, hr, cols)

    def body(c_ref, g_ref, s_ref, o_ref):
        o_ref[0] = (g_ref[0, 0].astype(F32) + s_ref[0].astype(F32)).astype(BF16)

    return pl.pallas_call(
        body, name=name,
        grid_spec=pltpu.PrefetchScalarGridSpec(
            num_scalar_prefetch=1, grid=(N_CHIPS,),
            in_specs=[pl.BlockSpec((1, 1, hr, cols), lambda s, c_ref: (s, c_ref[0], 0, 0)),
                      pl.BlockSpec((1, hr, cols), lambda s, c_ref: (s, 0, 0))],
            out_specs=pl.BlockSpec((1, hr, cols), lambda s, c_ref: (s, 0, 0))),
        out_shape=jax.ShapeDtypeStruct((N_CHIPS, hr, cols), BF16),
        compiler_params=_params(dimension_semantics=("arbitrary",)),
    )(core, g4, sib)


def _chip_add(part, recv, chip_core, name, deps=()):
    _, hr, cols = part.shape
    deps = tuple(deps)

    def body(kc_ref, *refs):
        p_ref, r_ref, o_ref = refs[len(deps):]
        acc = p_ref[0].astype(F32)
        for j in range(N_CHIPS - 1):
            acc = acc + r_ref[j].astype(F32)
        o_ref[0] = acc

    return pl.pallas_call(
        body, name=name,
        grid_spec=pltpu.PrefetchScalarGridSpec(
            num_scalar_prefetch=1, grid=(1,),
            in_specs=[_ANY] * len(deps) + [pl.BlockSpec((1, hr, cols), lambda s, kc_ref: (kc_ref[0], 0, 0)),
                                           pl.BlockSpec((N_CHIPS - 1, hr, cols), lambda s, kc_ref: (0, 0, 0))],
            out_specs=pl.BlockSpec((1, hr, cols), lambda s, kc_ref: (kc_ref[1], 0, 0))),
        out_shape=jax.ShapeDtypeStruct((2, hr, cols), F32),
        compiler_params=_params(dimension_semantics=("arbitrary",)),
    )(chip_core, *deps, part, recv)


def _adamw_math(w, g, m, v):
    m = ADAM_B1 * m + (1.0 - ADAM_B1) * g
    v = ADAM_B2 * v + (1.0 - ADAM_B2) * (g * g)
    m_hat = m / (1.0 - ADAM_B1 ** ADAM_STEP)
    v_hat = v / (1.0 - ADAM_B2 ** ADAM_STEP)
    delta = -ADAM_LR * (m_hat / (jnp.sqrt(v_hat) + ADAM_EPS) + ADAM_WD * w)
    return delta, m, v


def _adamw_matrix(w, g, m, v, name, tr):
    rows, cols = w.shape

    def body(w_ref, g_ref, m_ref, v_ref, d_ref, mo_ref, vo_ref):
        d_ref[...], mo_ref[...], vo_ref[...] = _adamw_math(w_ref[...], g_ref[...], m_ref[...], v_ref[...])

    spec = _row_spec(tr, cols)
    shape = jax.ShapeDtypeStruct((rows, cols), F32)
    return pl.pallas_call(
        body, name=name, grid=(rows // tr,), in_specs=[spec] * 4, out_specs=[spec] * 3, out_shape=[shape] * 3,
        compiler_params=_params(dimension_semantics=("arbitrary",)),
    )(w, g, m, v)


def _adamw_small(ws, gs, ms, vs):
    n = len(ws)

    def body(*refs):
        for i in range(n):
            w_ref, g_ref, m_ref, v_ref = (refs[k * n + i] for k in range(4))
            d_ref, mo_ref, vo_ref = (refs[(4 + k) * n + i] for k in range(3))
            d_ref[...], mo_ref[...], vo_ref[...] = _adamw_math(w_ref[...], g_ref[...], m_ref[...], v_ref[...])

    shapes = [jax.ShapeDtypeStruct(w.shape, F32) for w in ws]
    out = pl.pallas_call(body, name="adamw_small", out_shape=shapes * 3, compiler_params=_params())(*ws, *gs, *ms, *vs)
    return out[:n], out[n:2 * n], out[2 * n:]


MATRICES = ["ffn1_w13", "ffn1_w2", "w_in", "w_out", "ffn2_w13", "ffn2_w2"]
VECTORS = ["ffn1_norm", "mix_norm", "conv_b", "conv_ln_g", "conv_ln_b", "forget_b", "out_norm_conv",
           "out_norm_attn", "ffn2_norm", "final_norm"]
WEIGHTS = ["ffn1_norm", "ffn1_w13", "ffn1_w2", "mix_norm", "w_in", "conv_w", "conv_b", "conv_ln_g", "conv_ln_b",
           "forget_b", "out_norm_conv", "out_norm_attn", "w_out", "ffn2_norm", "ffn2_w13", "ffn2_w2", "final_norm"]
ADAM_ROWS = {"ffn1_w13": 256, "ffn2_w13": 256, "ffn1_w2": 352, "ffn2_w2": 352, "w_in": 256, "w_out": 256}


def _pack_small(g):
    rows, layout = [], []
    for n in VECTORS + ["conv_w"]:
        flat = g[n].reshape(-1)
        pad = (-flat.shape[0]) % LANES
        rows.append(jnp.pad(flat, (0, pad)).reshape(-1, LANES))
        layout.append((n, g[n].shape, flat.shape[0], rows[-1].shape[0]))
    packed = jnp.concatenate(rows, axis=0)
    pad_rows = (-packed.shape[0]) % 8
    return jnp.pad(packed, ((0, pad_rows), (0, 0))), layout


def _unpack_small(packed, layout):
    out, r = {}, 0
    for n, shape, size, nrows in layout:
        out[n] = packed[r:r + nrows].reshape(-1)[:size].reshape(shape)
        r += nrows
    return out


def kernel(x, ffn1_norm, ffn1_w13, ffn1_w2, mix_norm, w_in, conv_w, conv_b, conv_ln_g, conv_ln_b, forget_b, out_norm_conv, out_norm_attn, w_out, ffn2_norm, ffn2_w13, ffn2_w2, final_norm, loss_target, m_ffn1_norm, m_ffn1_w13, m_ffn1_w2, m_mix_norm, m_w_in, m_conv_w, m_conv_b, m_conv_ln_g, m_conv_ln_b, m_forget_b, m_out_norm_conv, m_out_norm_attn, m_w_out, m_ffn2_norm, m_ffn2_w13, m_ffn2_w2, m_final_norm, v_ffn1_norm, v_ffn1_w13, v_ffn1_w2, v_mix_norm, v_w_in, v_conv_w, v_conv_b, v_conv_ln_g, v_conv_ln_b, v_forget_b, v_out_norm_conv, v_out_norm_attn, v_w_out, v_ffn2_norm, v_ffn2_w13, v_ffn2_w2, v_final_norm):
    args = dict(locals())
    weights = {n: args[n] for n in WEIGHTS}
    core = lax.axis_index("c").astype(jnp.int32).reshape(1)
    chip = (2 * lax.axis_index("x") + lax.axis_index("y")).astype(jnp.int32)
    chip1 = chip.reshape(1)
    chip_core = jnp.concatenate([chip1, core])

    slot = {n: _into_slot(weights[n][0], chip1, BF16, "slot_" + n) for n in MATRICES}
    slot["conv_w"] = _into_slot(jnp.pad(conv_w[0], ((0, CONV_PAD - CONV_WIDTH), (0, 0))), chip1, F32, "slot_conv_w")
    fetched = {"ffn1": ["ffn1_w13", "ffn1_w2"], "mix": ["w_in", "w_out", "conv_w"], "ffn2": ["ffn2_w13", "ffn2_w2"]}
    fetch = {}

    def as_weights(group, bufs):
        out = {}
        for n, b in zip(fetched[group], bufs):
            if n.endswith("w13"):
                out[n] = b
            elif n.endswith("w2"):
                out[n] = b.reshape(D_FF, D_MODEL)
            elif n == "w_out":
                out[n] = b.reshape(D_MODEL, D_MODEL)
            elif n == "w_in":
                out[n] = jnp.concatenate([b[k] for k in range(N_CHIPS)], axis=1)
            else:
                out[n] = b[:, :CONV_WIDTH].transpose(1, 0, 2).reshape(CONV_WIDTH, D_CONV)
        return out

    def get_weights(group, after):
        if group == "ffn1":
            bufs = _gather_shards([slot[n] for n in fetched[group]], "gather_ffn1")
            for later in ("mix", "ffn2"):
                bufs_later = [slot[n] for n in fetched[later]]
                plan = _ici_gather_plan(bufs_later)
                fetch[later] = plan, _split_copy_start("gather_%s_start" % later, bufs_later, 3 * len(bufs_later), plan)
            return as_weights(group, bufs), [fetch["mix"][1][3], fetch["ffn2"][1][3]]
        plan, started = fetch[group]
        landed = _split_copy_wait("gather_%s_wait" % group, started, plan, after)
        return as_weights(group, _gather_shards(landed, "forward_" + group, ici=False)), []

    def shard_major(n, g):
        if n.endswith("w13"):
            return g
        if n == "w_in":
            return jnp.stack([g[:, k * IN_SHARD:(k + 1) * IN_SHARD] for k in range(N_CHIPS)])
        return g.reshape(N_CHIPS, g.shape[0] // N_CHIPS, g.shape[1])

    scatter = {}

    def put_grads(group, grads, after):
        names = list(grads)
        local = [shard_major(n, grads[n]) for n in names]
        sib = _pair_exchange(local, "pair_exchange_" + group)
        parts = [_pair_add(a, b, core, "pair_add_" + n) for a, b, n in zip(local, sib, names)]
        landing = [lax.empty((N_CHIPS - 1,) + q.shape[1:], BF16) for q in parts]
        plan = _ici_scatter_plan(len(parts))
        scatter[group] = names, plan, _split_copy_start("scatter_%s_start" % group, parts + landing, 3 * len(parts), plan)
        return [scatter[group][2][3]]

    p = {n: weights[n] for n in VECTORS}
    p["final_norm"] = final_norm.reshape(1, D_MODEL)
    loss_part, dx, g = _local_step(x[0], loss_target[0], p, get_weights, put_grads)
    loss = lax.psum(loss_part, ("x", "y", "c"))
    last_start = scatter["ffn1"][2][3]

    packed, layout = _pack_small(g)
    small = _unpack_small(_all_reduce_small(packed, [last_start]), layout)
    grad = {n: small[n] for n in VECTORS}
    grad["final_norm"] = small["final_norm"].reshape(D_MODEL)
    grad["conv_w"] = lax.dynamic_slice_in_dim(small["conv_w"], chip * (D_CONV // N_CHIPS), D_CONV // N_CHIPS, axis=1)[None]

    delta, new_m, new_v = {}, {}, {}

    def finish(group, after):
        names, plan, started = scatter[group]
        done = _split_copy_wait("scatter_%s_wait" % group, started, plan, after)
        parts, landed = done[:len(names)], done[len(names):]
        halves = [_chip_add(a, b, chip_core, "chip_add_" + n) for a, b, n in zip(parts, landed, names)]
        for n, f in zip(names, _pair_share(halves, "pair_share_" + group)):
            grad[n] = f.reshape(1, f.shape[0] * f.shape[1], f.shape[2])
            d, mo, vo = _adamw_matrix(weights[n][0], grad[n][0], args["m_" + n][0], args["v_" + n][0], "adamw_" + n, ADAM_ROWS[n])
            delta[n], new_m[n], new_v[n] = d[None], mo[None], vo[None]
        return vo

    done_ffn2 = finish("ffn2", last_start)
    done_mix = finish("mix", done_ffn2)
    small_names = VECTORS + ["conv_w"]
    as2d = lambda a: a.reshape(-1, a.shape[-1])
    ds, mos, vos = _adamw_small([as2d(weights[n]) for n in small_names], [as2d(grad[n]) for n in small_names],
                                [as2d(args["m_" + n]) for n in small_names], [as2d(args["v_" + n]) for n in small_names])
    for n, d, mo, vo in zip(small_names, ds, mos, vos):
        shape = weights[n].shape
        delta[n], new_m[n], new_v[n] = d.reshape(shape), mo.reshape(shape), vo.reshape(shape)
    finish("ffn1", done_mix)

    return (loss, dx[None], *[grad[n] for n in WEIGHTS], *[delta[n] for n in WEIGHTS],
            *[new_m[n] for n in WEIGHTS], *[new_v[n] for n in WEIGHTS])
```

```python
import functools

import jax
import jax.numpy as jnp
from jax import lax
from jax.experimental import pallas as pl
from jax.experimental.pallas import tpu as pltpu

F32 = jnp.float32
BF16 = jnp.bfloat16

D_MODEL = 1024
D_FF = 2816
FF_SHARD = D_FF // 2
D_CONV = 512
D_ATTN = 512
N_HEADS = 8
HEAD_DIM = 64
CONV_WIDTH = 31
CONV_PAD = 32
N_IN = 2 * D_CONV + 3 * D_ATTN + N_HEADS
IN_SHARD = N_IN // 4
EPS = 1e-6
N_CHIPS = 4
LANES = 128
HEAD_ROWS = 16

ADAM_LR = 0.001
ADAM_B1 = 0.9
ADAM_B2 = 0.999
ADAM_EPS = 1e-08
ADAM_WD = 0.01
ADAM_STEP = 10

VMEM_LIMIT = 56 * 1024 * 1024

_NT = (((1,), (1,)), ((), ()))
_TN = (((0,), (0,)), ((), ()))


def _dot(a, b):
    return jnp.dot(a, b, preferred_element_type=F32)


def _dot_nt(a, b):
    return lax.dot_general(a, b, _NT, preferred_element_type=F32)


def _dot_tn(a, b):
    return lax.dot_general(a, b, _TN, preferred_element_type=F32)


def _params(**kw):
    return pltpu.CompilerParams(vmem_limit_bytes=VMEM_LIMIT, **kw)


def _sigmoid(x):
    return 1.0 / (1.0 + jnp.exp(-x))


def _rms_stats(x):
    return lax.rsqrt(jnp.mean(x * x, axis=-1, keepdims=True) + EPS)


def _rms_bwd(x, r, g, dh):
    t = dh * g
    dx = r * t - x * (r * r * r) * jnp.mean(t * x, axis=-1, keepdims=True)
    return dx, dh * x * r


def _silu_grad(z, sg):
    return sg * (1.0 + z * (1.0 - sg))


def _row_spec(tm, n):
    return pl.BlockSpec((tm, n), lambda i: (i, 0))


def _full_spec(shape):
    nd = len(shape)
    return pl.BlockSpec(shape, lambda i: (0,) * nd)


_ANY = pl.BlockSpec(memory_space=pl.ANY)


def _skip(n, body):
    return lambda *refs: body(*refs[n:])


def _ffn_fwd(x, g, w13s, w2, name, deps=()):
    t = x.shape[0]
    tm = 256
    deps = tuple(deps)

    def body(x_ref, g_ref, w13_hbm, w2_hbm, xo_ref, h_ref, gu_ref, w13_ref, w2_ref):
        @pl.when(pl.program_id(0) == 0)
        def _():
            pltpu.sync_copy(w13_hbm, w13_ref)
            pltpu.sync_copy(w2_hbm, w2_ref)

        xv = x_ref[...]
        hb = (xv * _rms_stats(xv) * g_ref[...]).astype(BF16)
        h_ref[...] = hb
        acc = jnp.zeros((tm, D_MODEL), F32)
        for half in range(2):
            lo = half * FF_SHARD
            gate = _dot(hb, w13_ref[half])
            up = _dot(hb, w13_ref[2 + half])
            gu_ref[:, lo:lo + FF_SHARD] = gate.astype(BF16)
            gu_ref[:, D_FF + lo:D_FF + lo + FF_SHARD] = up.astype(BF16)
            a = (gate * _sigmoid(gate) * up).astype(BF16)
            acc = acc + _dot(a, w2_ref[lo:lo + FF_SHARD, :])
        xo_ref[...] = xv + 0.5 * acc

    return pl.pallas_call(
        _skip(len(deps), body), name=name, grid=(t // tm,),
        in_specs=[_ANY] * len(deps) + [_row_spec(tm, D_MODEL), _full_spec((1, D_MODEL)), _ANY, _ANY],
        out_specs=[_row_spec(tm, D_MODEL), _row_spec(tm, D_MODEL), _row_spec(tm, 2 * D_FF)],
        out_shape=[jax.ShapeDtypeStruct((t, D_MODEL), F32), jax.ShapeDtypeStruct((t, D_MODEL), BF16),
                   jax.ShapeDtypeStruct((t, 2 * D_FF), BF16)],
        scratch_shapes=[pltpu.VMEM(w13s.shape, BF16), pltpu.VMEM(w2.shape, BF16)],
        compiler_params=_params(dimension_semantics=("arbitrary",)),
    )(*deps, x, g, w13s, w2)


def _ffn_bwd(dy, x, gu, g, w13s, w2, name, deps=()):
    t = x.shape[0]
    tm = 256
    deps = tuple(deps)

    def body(dy_ref, x_ref, gu_ref, g_ref, w13_hbm, w2_hbm, dx_ref, dgu_ref, a_ref, dg_ref, w13_ref, w2_ref):
        @pl.when(pl.program_id(0) == 0)
        def _():
            pltpu.sync_copy(w13_hbm, w13_ref)
            pltpu.sync_copy(w2_hbm, w2_ref)
            dg_ref[...] = jnp.zeros_like(dg_ref)

        dyv = dy_ref[...]
        dyh = (0.5 * dyv).astype(BF16)
        dh = jnp.zeros((tm, D_MODEL), F32)
        for half in range(2):
            lo = half * FF_SHARD
            da = _dot_nt(dyh, w2_ref[lo:lo + FF_SHARD, :])
            gate = gu_ref[:, lo:lo + FF_SHARD].astype(F32)
            up = gu_ref[:, D_FF + lo:D_FF + lo + FF_SHARD].astype(F32)
            sg = _sigmoid(gate)
            act = gate * sg
            a_ref[:, lo:lo + FF_SHARD] = (act * up).astype(BF16)
            dgate = (da * up * _silu_grad(gate, sg)).astype(BF16)
            dup = (da * act).astype(BF16)
            dgu_ref[:, lo:lo + FF_SHARD] = dgate
            dgu_ref[:, D_FF + lo:D_FF + lo + FF_SHARD] = dup
            dh = dh + _dot_nt(dgate, w13_ref[half]) + _dot_nt(dup, w13_ref[2 + half])
        xv = x_ref[...]
        dxn, dg_rows = _rms_bwd(xv, _rms_stats(xv), g_ref[...], dh)
        dx_ref[...] = dyv + dxn
        dg_ref[...] += jnp.sum(dg_rows, axis=0, keepdims=True)

    return pl.pallas_call(
        _skip(len(deps), body), name=name, grid=(t // tm,),
        in_specs=[_ANY] * len(deps) + [_row_spec(tm, D_MODEL), _row_spec(tm, D_MODEL), _row_spec(tm, 2 * D_FF),
                                       _full_spec((1, D_MODEL)), _ANY, _ANY],
        out_specs=[_row_spec(tm, D_MODEL), _row_spec(tm, 2 * D_FF), _row_spec(tm, D_FF),
                   _full_spec((1, D_MODEL))],
        out_shape=[jax.ShapeDtypeStruct((t, D_MODEL), F32), jax.ShapeDtypeStruct((t, 2 * D_FF), BF16),
                   jax.ShapeDtypeStruct((t, D_FF), BF16), jax.ShapeDtypeStruct((1, D_MODEL), F32)],
        scratch_shapes=[pltpu.VMEM(w13s.shape, BF16), pltpu.VMEM(w2.shape, BF16)],
        compiler_params=_params(dimension_semantics=("arbitrary",)),
    )(*deps, dy, x, gu, g, w13s, w2)


def _wgrad(a, b, n_blocks, name, scale=1.0, tm=256, deps=()):
    t, m = a.shape
    n = b.shape[1]
    bn = n // n_blocks
    deps = tuple(deps)

    def body(a_ref, b_ref, o_ref):
        bv = b_ref[...]
        if scale != 1.0:
            bv = bv * scale
        o_ref[0] = _dot_tn(a_ref[...].astype(BF16), bv.astype(BF16)).astype(BF16)

    return pl.pallas_call(
        _skip(len(deps), body), name=name, grid=(n_blocks, m // tm),
        in_specs=[_ANY] * len(deps) + [pl.BlockSpec((t, tm), lambda j, i: (0, i)),
                                       pl.BlockSpec((t, bn), lambda j, i: (0, j))],
        out_specs=pl.BlockSpec((1, tm, bn), lambda j, i: (j, i, 0)),
        out_shape=jax.ShapeDtypeStruct((n_blocks, m, bn), BF16),
        compiler_params=_params(dimension_semantics=("arbitrary", "arbitrary")),
    )(*deps, a, b)


def _mix_proj(x, g, w_ag, w_qkv, w_f):
    t = x.shape[0]
    tm = 256

    def body(x_ref, g_ref, wag_ref, wqkv_ref, wf_ref, h_ref, ag_ref, qkv_ref, fl_ref):
        xv = x_ref[...]
        hb = (xv * _rms_stats(xv) * g_ref[...]).astype(BF16)
        h_ref[...] = hb
        ag_ref[...] = _dot(hb, wag_ref[...])
        qkv_ref[...] = _dot(hb, wqkv_ref[...]).astype(BF16)
        fl_ref[...] = _dot(hb, wf_ref[...])

    return pl.pallas_call(
        body, name="mix_proj", grid=(t // tm,),
        in_specs=[_row_spec(tm, D_MODEL), _full_spec((1, D_MODEL)), _full_spec(w_ag.shape),
                  _full_spec(w_qkv.shape), _full_spec(w_f.shape)],
        out_specs=[_row_spec(tm, D_MODEL), _row_spec(tm, 2 * D_CONV), _row_spec(tm, 3 * D_ATTN),
                   _row_spec(tm, LANES)],
        out_shape=[jax.ShapeDtypeStruct((t, D_MODEL), BF16), jax.ShapeDtypeStruct((t, 2 * D_CONV), F32),
                   jax.ShapeDtypeStruct((t, 3 * D_ATTN), BF16), jax.ShapeDtypeStruct((t, LANES), F32)],
        compiler_params=_params(dimension_semantics=("arbitrary",)),
    )(x, g, w_ag, w_qkv, w_f)


def _mix_proj_bwd(dag, dqkv, dfl, dx2, x1, g, w_ag, w_qkv, w_f):
    t = x1.shape[0]
    tm = 256

    def body(dag_ref, dqkv_ref, dfl_ref, dx2_ref, x_ref, g_ref, wag_ref, wqkv_ref, wf_ref, dx_ref, dg_ref):
        @pl.when(pl.program_id(0) == 0)
        def _():
            dg_ref[...] = jnp.zeros_like(dg_ref)

        dh = (_dot_nt(dag_ref[...].astype(BF16), wag_ref[...]) + _dot_nt(dqkv_ref[...], wqkv_ref[...])
              + _dot_nt(dfl_ref[...].astype(BF16), wf_ref[...]))
        xv = x_ref[...]
        dxn, dg_rows = _rms_bwd(xv, _rms_stats(xv), g_ref[...], dh)
        dx_ref[...] = dx2_ref[...] + dxn
        dg_ref[...] += jnp.sum(dg_rows, axis=0, keepdims=True)

    return pl.pallas_call(
        body, name="mix_proj_bwd", grid=(t // tm,),
        in_specs=[_row_spec(tm, 2 * D_CONV), _row_spec(tm, 3 * D_ATTN), _row_spec(tm, LANES),
                  _row_spec(tm, D_MODEL), _row_spec(tm, D_MODEL), _full_spec((1, D_MODEL)),
                  _full_spec(w_ag.shape), _full_spec(w_qkv.shape), _full_spec(w_f.shape)],
        out_specs=[_row_spec(tm, D_MODEL), _full_spec((1, D_MODEL))],
        out_shape=[jax.ShapeDtypeStruct((t, D_MODEL), F32), jax.ShapeDtypeStruct((1, D_MODEL), F32)],
        compiler_params=_params(dimension_semantics=("arbitrary",)),
    )(dag, dqkv, dfl, dx2, x1, g, w_ag, w_qkv, w_f)


def _split3(x):
    hi = x.astype(BF16)
    r1 = x - hi.astype(F32)
    mid = r1.astype(BF16)
    lo = (r1 - mid.astype(F32)).astype(BF16)
    return hi, mid, lo


def _gates_fwd(flt, fb):
    t = flt.shape[1]

    def body(f_ref, b_ref, d_ref):
        z = f_ref[...] + b_ref[...]
        logf = jnp.minimum(z, 0.0) - jnp.log(1.0 + jnp.exp(-jnp.abs(z)))
        row = lax.broadcasted_iota(jnp.int32, (LANES, LANES), 0)
        col = lax.broadcasted_iota(jnp.int32, (LANES, LANES), 1)
        upper = (row <= col).astype(BF16)
        carry = jnp.zeros((HEAD_ROWS, 1), F32)
        for blk in range(t // LANES):
            hi, mid, lo = _split3(logf[:, blk * LANES:(blk + 1) * LANES])
            cs = _dot(hi, upper) + _dot(mid, upper) + _dot(lo, upper)
            d_ref[:, blk * LANES:(blk + 1) * LANES] = cs + carry
            carry = carry + cs[:, LANES - 1:LANES]

    return pl.pallas_call(
        body, name="gates_fwd", out_shape=jax.ShapeDtypeStruct((HEAD_ROWS, t), F32),
        compiler_params=_params(),
    )(flt, fb)


def _gates_bwd(dd, flt, fb):
    t = flt.shape[1]

    def body(dd_ref, f_ref, b_ref, df_ref, db_ref):
        z = f_ref[...] + b_ref[...]
        row = lax.broadcasted_iota(jnp.int32, (LANES, LANES), 0)
        col = lax.broadcasted_iota(jnp.int32, (LANES, LANES), 1)
        lower = (row >= col).astype(BF16)
        carry = jnp.zeros((HEAD_ROWS, 1), F32)
        db = jnp.zeros((HEAD_ROWS, 1), F32)
        for blk in reversed(range(t // LANES)):
            sl = slice(blk * LANES, (blk + 1) * LANES)
            hi, mid, lo = _split3(dd_ref[:, sl])
            cs = _dot(hi, lower) + _dot(mid, lower) + _dot(lo, lower)
            dz = (cs + carry) * _sigmoid(-z[:, sl])
            df_ref[:, sl] = dz
            db = db + jnp.sum(dz, axis=1, keepdims=True)
            carry = carry + cs[:, 0:1]
        db_ref[...] = db

    return pl.pallas_call(
        body, name="gates_bwd",
        out_shape=[jax.ShapeDtypeStruct((HEAD_ROWS, t), F32), jax.ShapeDtypeStruct((HEAD_ROWS, 1), F32)],
        compiler_params=_params(),
    )(dd, flt, fb)


CONV_CHUNK = 64
CONV_TAIL = 16
CONV_WINDOW = CONV_CHUNK + CONV_PAD + 8
CONV_ROWS_EXTRA = CONV_PAD + CONV_TAIL
SUBLANES = 8


def _conv_rows(ag_ref, u_ref, t):
    u_ref[0:CONV_PAD, :] = jnp.zeros((CONV_PAD, D_CONV), F32)
    u_ref[CONV_PAD + t:CONV_ROWS_EXTRA + t, :] = jnp.zeros((CONV_TAIL, D_CONV), F32)

    def fill(i, c):
        r0 = pl.multiple_of(i * CONV_CHUNK, CONV_CHUNK)
        a = ag_ref[pl.ds(r0, CONV_CHUNK), 0:D_CONV]
        gt = ag_ref[pl.ds(r0, CONV_CHUNK), D_CONV:2 * D_CONV]
        u_ref[pl.ds(CONV_PAD + r0, CONV_CHUNK), :] = a * _sigmoid(gt)
        return c

    lax.fori_loop(0, t // CONV_CHUNK, fill, 0)


def _for_shifted(ref, r0, offsets, fn):
    window = ref[pl.ds(r0, CONV_WINDOW), :]
    for rem in range(SUBLANES):
        mine = [o for o in offsets if o % SUBLANES == rem]
        if not mine:
            continue
        turned = window if rem == 0 else pltpu.roll(window, CONV_WINDOW - rem, 0)
        for o in mine:
            fn(o, turned[o - rem:o - rem + CONV_CHUNK])


def _conv_point(u_ref, r0, w_ref, cb, lg, lb):
    acc = [jnp.zeros((CONV_CHUNK, D_CONV), F32)]

    def tap(o, rows):
        j = o - (CONV_PAD - CONV_WIDTH + 1)
        acc[0] = acc[0] + w_ref[j:j + 1, :] * rows

    _for_shifted(u_ref, r0, [j + CONV_PAD - CONV_WIDTH + 1 for j in range(CONV_WIDTH)], tap)
    y = acc[0] + cb
    mu = jnp.mean(y, axis=-1, keepdims=True)
    yc = y - mu
    rstd = lax.rsqrt(jnp.mean(yc * yc, axis=-1, keepdims=True) + EPS)
    yhat = yc * rstd
    z = yhat * lg + lb
    sg = _sigmoid(z)
    s = z * sg
    rr = _rms_stats(s)
    return yhat, rstd, z, sg, s, rr


def _conv_fwd(ag, conv_w, conv_b, ln_g, ln_b, norm_g):
    t = ag.shape[0]

    def body(ag_ref, w_ref, cb_ref, lg_ref, lb_ref, ng_ref, o_ref, u_ref):
        _conv_rows(ag_ref, u_ref, t)
        cb, lg, lb, ng = cb_ref[...], lg_ref[...], lb_ref[...], ng_ref[...]

        def chunk(i, c):
            r0 = pl.multiple_of(i * CONV_CHUNK, CONV_CHUNK)
            _, _, _, _, s, rr = _conv_point(u_ref, r0, w_ref, cb, lg, lb)
            o_ref[pl.ds(r0, CONV_CHUNK), :] = (s * rr * ng).astype(BF16)
            return c

        lax.fori_loop(0, t // CONV_CHUNK, chunk, 0)

    return pl.pallas_call(
        body, name="conv_fwd", out_shape=jax.ShapeDtypeStruct((t, D_CONV), BF16),
        scratch_shapes=[pltpu.VMEM((t + CONV_ROWS_EXTRA, D_CONV), F32)],
        compiler_params=_params(),
    )(ag, conv_w, conv_b, ln_g, ln_b, norm_g)


def _conv_bwd(ag, dout, conv_w, conv_b, ln_g, ln_b, norm_g):
    t = ag.shape[0]

    def body(ag_ref, do_ref, w_ref, cb_ref, lg_ref, lb_ref, ng_ref,
             dag_ref, dw_ref, dcb_ref, dlg_ref, dlb_ref, dng_ref, u_ref, dy_ref):
        _conv_rows(ag_ref, u_ref, t)
        dy_ref[t:t + CONV_ROWS_EXTRA, :] = jnp.zeros((CONV_ROWS_EXTRA, D_CONV), F32)
        cb, lg, lb, ng = cb_ref[...], lg_ref[...], lb_ref[...], ng_ref[...]
        dw_ref[...] = jnp.zeros_like(dw_ref)
        zero = jnp.zeros((1, D_CONV), F32)

        def chunk(i, carry):
            dcb, dlg, dlb, dng = carry
            r0 = pl.multiple_of(i * CONV_CHUNK, CONV_CHUNK)
            yhat, rstd, z, sg, s, rr = _conv_point(u_ref, r0, w_ref, cb, lg, lb)
            do = do_ref[pl.ds(r0, CONV_CHUNK), :]
            ds, dng_rows = _rms_bwd(s, rr, ng, do)
            dz = ds * _silu_grad(z, sg)
            dyhat = dz * lg
            dy = rstd * (dyhat - jnp.mean(dyhat, axis=-1, keepdims=True)
                         - yhat * jnp.mean(dyhat * yhat, axis=-1, keepdims=True))
            dy_ref[pl.ds(r0, CONV_CHUNK), :] = dy
            def tap(o, rows):
                j = o - (CONV_PAD - CONV_WIDTH + 1)
                dw_ref[j:j + 1, :] += jnp.sum(dy * rows, axis=0, keepdims=True)

            _for_shifted(u_ref, r0, [j + CONV_PAD - CONV_WIDTH + 1 for j in range(CONV_WIDTH)], tap)
            return (dcb + jnp.sum(dy, axis=0, keepdims=True), dlg + jnp.sum(dz * yhat, axis=0, keepdims=True),
                    dlb + jnp.sum(dz, axis=0, keepdims=True), dng + jnp.sum(dng_rows, axis=0, keepdims=True))

        dcb, dlg, dlb, dng = lax.fori_loop(0, t // CONV_CHUNK, chunk, (zero, zero, zero, zero))
        dcb_ref[...] = dcb
        dlg_ref[...] = dlg
        dlb_ref[...] = dlb
        dng_ref[...] = dng

        def chunk2(i, c):
            r0 = pl.multiple_of(i * CONV_CHUNK, CONV_CHUNK)
            acc = [jnp.zeros((CONV_CHUNK, D_CONV), F32)]

            def tap(o, rows):
                j = CONV_WIDTH - 1 - o
                acc[0] = acc[0] + w_ref[j:j + 1, :] * rows

            _for_shifted(dy_ref, r0, list(range(CONV_WIDTH)), tap)
            du = acc[0]
            a = ag_ref[pl.ds(r0, CONV_CHUNK), 0:D_CONV]
            gt = ag_ref[pl.ds(r0, CONV_CHUNK), D_CONV:2 * D_CONV]
            sg = _sigmoid(gt)
            dag_ref[pl.ds(r0, CONV_CHUNK), 0:D_CONV] = du * sg
            dag_ref[pl.ds(r0, CONV_CHUNK), D_CONV:2 * D_CONV] = du * a * sg * (1.0 - sg)
            return c

        lax.fori_loop(0, t // CONV_CHUNK, chunk2, 0)

    vec = jax.ShapeDtypeStruct((1, D_CONV), F32)
    return pl.pallas_call(
        body, name="conv_bwd",
        out_shape=[jax.ShapeDtypeStruct((t, 2 * D_CONV), F32), jax.ShapeDtypeStruct((CONV_PAD, D_CONV), F32),
                   vec, vec, vec, vec],
        scratch_shapes=[pltpu.VMEM((t + CONV_ROWS_EXTRA, D_CONV), F32), pltpu.VMEM((t + CONV_ROWS_EXTRA, D_CONV), F32)],
        compiler_params=_params(),
    )(ag, dout, conv_w, conv_b, ln_g, ln_b, norm_g)


Q_ROWS = 256
ATTN_SCALE = HEAD_DIM ** -0.5


def _attn_specs(t):
    blk = lambda off: pl.BlockSpec((t, LANES), lambda p: (0, off + p))
    pairs = N_HEADS // 2
    return [blk(0), blk(pairs), blk(2 * pairs), pl.BlockSpec((2, 1, t), lambda p: (p, 0, 0))]


def _one_head(q2, mask):
    return jnp.where(mask, q2, jnp.zeros_like(q2)) * ATTN_SCALE


def _attn_scores(qs, k2, drow, r0, q1):
    s = _dot_nt(qs, k2) - drow
    rowi = lax.broadcasted_iota(jnp.int32, (q1 - r0, q1 - r0), 0)
    coli = lax.broadcasted_iota(jnp.int32, (q1 - r0, q1 - r0), 1)
    diag = jnp.where(coli <= rowi, s[:, r0:q1], -jnp.inf)
    return diag if r0 == 0 else jnp.concatenate([s[:, :r0], diag], axis=1)


def _attn_fwd(qkv, drow):
    t = qkv.shape[0]

    def body(q_ref, k_ref, v_ref, dr_ref, o_ref, lse_ref):
        head_a = lax.broadcasted_iota(jnp.int32, (1, LANES), 1) < HEAD_DIM
        for qb in range(t // Q_ROWS):
            r0, q1 = qb * Q_ROWS, (qb + 1) * Q_ROWS
            q2 = q_ref[r0:q1, :]
            k2 = k_ref[0:q1, :]
            v2 = v_ref[0:q1, :]
            outs = []
            for hh in range(2):
                qs = _one_head(q2, head_a if hh == 0 else ~head_a)
                s = _attn_scores(qs, k2, dr_ref[hh, :, 0:q1], r0, q1)
                mx = jnp.max(s, axis=1, keepdims=True)
                p = jnp.exp(s - mx)
                l = jnp.sum(p, axis=1, keepdims=True)
                lse_ref[hh, r0:q1, :] = mx + jnp.log(l)
                outs.append(_dot((p * (1.0 / l)).astype(BF16), v2))
            o_ref[r0:q1, :] = jnp.where(head_a, outs[0], outs[1])

    pairs = N_HEADS // 2
    return pl.pallas_call(
        body, name="attn_fwd", grid=(pairs,), in_specs=_attn_specs(t),
        out_specs=[pl.BlockSpec((t, LANES), lambda p: (0, p)), pl.BlockSpec((2, t, 1), lambda p: (p, 0, 0))],
        out_shape=[jax.ShapeDtypeStruct((t, D_ATTN), F32), jax.ShapeDtypeStruct((N_HEADS, t, 1), F32)],
        compiler_params=_params(dimension_semantics=("arbitrary",)),
    )(qkv, qkv, qkv, drow)


def _attn_bwd(qkv, drow, lse, do):
    t = qkv.shape[0]

    def body(q_ref, k_ref, v_ref, dr_ref, lse_ref, do_ref,
             dq_ref, dk_ref, dv_ref, dd_ref, dk_acc, dv_acc):
        head_a = lax.broadcasted_iota(jnp.int32, (1, LANES), 1) < HEAD_DIM
        dk_acc[...] = jnp.zeros_like(dk_acc)
        dv_acc[...] = jnp.zeros_like(dv_acc)
        dd_ref[...] = jnp.zeros_like(dd_ref)
        for qb in range(t // Q_ROWS):
            r0, q1 = qb * Q_ROWS, (qb + 1) * Q_ROWS
            q2 = q_ref[r0:q1, :]
            k2 = k_ref[0:q1, :]
            v2 = v_ref[0:q1, :]
            do2 = do_ref[r0:q1, :]
            dqs = []
            dk_sum = jnp.zeros((q1, LANES), F32)
            dv_sum = jnp.zeros((q1, LANES), F32)
            for hh in range(2):
                mask = head_a if hh == 0 else ~head_a
                qs = _one_head(q2, mask)
                dob = jnp.where(mask, do2, 0.0).astype(BF16)
                p = jnp.exp(_attn_scores(qs, k2, dr_ref[hh, :, 0:q1], r0, q1) - lse_ref[hh, r0:q1, :])
                dp = _dot_nt(dob, v2)
                ds = p * (dp - jnp.sum(p * dp, axis=1, keepdims=True))
                dsb = ds.astype(BF16)
                dqs.append(_dot(dsb, k2) * ATTN_SCALE)
                dk_sum = dk_sum + _dot_tn(dsb, qs)
                dv_sum = dv_sum + _dot_tn(p.astype(BF16), dob)
                dd_ref[hh, :, 0:q1] -= jnp.sum(ds, axis=0, keepdims=True)
            dq_ref[r0:q1, :] = jnp.where(head_a, dqs[0], dqs[1]).astype(BF16)
            dk_acc[0:q1, :] += dk_sum
            dv_acc[0:q1, :] += dv_sum
        dk_ref[...] = dk_acc[...].astype(BF16)
        dv_ref[...] = dv_acc[...].astype(BF16)

    pairs = N_HEADS // 2
    col = pl.BlockSpec((t, LANES), lambda p: (0, p))
    grad = jax.ShapeDtypeStruct((t, D_ATTN), BF16)
    return pl.pallas_call(
        body, name="attn_bwd", grid=(pairs,),
        in_specs=_attn_specs(t) + [pl.BlockSpec((2, t, 1), lambda p: (p, 0, 0)), col],
        out_specs=[col, col, col, pl.BlockSpec((2, 1, t), lambda p: (p, 0, 0))],
        out_shape=[grad, grad, grad, jax.ShapeDtypeStruct((N_HEADS, 1, t), F32)],
        scratch_shapes=[pltpu.VMEM((t, LANES), F32), pltpu.VMEM((t, LANES), F32)],
        compiler_params=_params(dimension_semantics=("arbitrary",)),
    )(qkv, qkv, qkv, drow, lse, do)


def _out_proj(ycn, o, g_attn, w_out, x1):
    t = x1.shape[0]
    tm = 256

    def body(yc_ref, o_ref, g_ref, w_ref, x_ref, xo_ref, ya_ref):
        ov = o_ref[...]
        ya = (ov * _rms_stats(ov) * g_ref[...]).astype(BF16)
        ya_ref[...] = ya
        xo_ref[...] = x_ref[...] + _dot(yc_ref[...], w_ref[0:D_CONV, :]) + _dot(ya, w_ref[D_CONV:, :])

    return pl.pallas_call(
        body, name="out_proj", grid=(t // tm,),
        in_specs=[_row_spec(tm, D_CONV), _row_spec(tm, D_ATTN), _full_spec((1, D_ATTN)),
                  _full_spec(w_out.shape), _row_spec(tm, D_MODEL)],
        out_specs=[_row_spec(tm, D_MODEL), _row_spec(tm, D_ATTN)],
        out_shape=[jax.ShapeDtypeStruct((t, D_MODEL), F32), jax.ShapeDtypeStruct((t, D_ATTN), BF16)],
        compiler_params=_params(dimension_semantics=("arbitrary",)),
    )(ycn, o, g_attn, w_out, x1)


def _out_proj_bwd(dx2, o, g_attn, w_out, deps=()):
    t = dx2.shape[0]
    tm = 256
    deps = tuple(deps)

    def body(dx_ref, o_ref, g_ref, w_ref, dyc_ref, do_ref, dg_ref):
        @pl.when(pl.program_id(0) == 0)
        def _():
            dg_ref[...] = jnp.zeros_like(dg_ref)

        dxb = dx_ref[...].astype(BF16)
        dyc_ref[...] = _dot_nt(dxb, w_ref[0:D_CONV, :])
        dya = _dot_nt(dxb, w_ref[D_CONV:, :])
        ov = o_ref[...]
        do, dg_rows = _rms_bwd(ov, _rms_stats(ov), g_ref[...], dya)
        do_ref[...] = do
        dg_ref[...] += jnp.sum(dg_rows, axis=0, keepdims=True)

    return pl.pallas_call(
        _skip(len(deps), body), name="out_proj_bwd", grid=(t // tm,),
        in_specs=[_ANY] * len(deps) + [_row_spec(tm, D_MODEL), _row_spec(tm, D_ATTN), _full_spec((1, D_ATTN)),
                                       _full_spec(w_out.shape)],
        out_specs=[_row_spec(tm, D_CONV), _row_spec(tm, D_ATTN), _full_spec((1, D_ATTN))],
        out_shape=[jax.ShapeDtypeStruct((t, D_CONV), F32), jax.ShapeDtypeStruct((t, D_ATTN), F32),
                   jax.ShapeDtypeStruct((1, D_ATTN), F32)],
        compiler_params=_params(dimension_semantics=("arbitrary",)),
    )(*deps, dx2, o, g_attn, w_out)


def _loss_bwd(x3, target, g):
    t = x3.shape[0]
    tm = 256

    def body(x_ref, t_ref, g_ref, loss_ref, dx_ref, dg_ref):
        @pl.when(pl.program_id(0) == 0)
        def _():
            loss_ref[...] = jnp.zeros_like(loss_ref)
            dg_ref[...] = jnp.zeros_like(dg_ref)

        xv = x_ref[...]
        r = _rms_stats(xv)
        gv = g_ref[...]
        err = xv * r * gv - t_ref[...]
        row = jnp.sum(err * err, axis=1, keepdims=True) * (0.5 / D_MODEL)
        loss_ref[...] += jnp.sum(row, axis=0, keepdims=True)
        dx, dg_rows = _rms_bwd(xv, r, gv, err * (1.0 / D_MODEL))
        dx_ref[...] = dx
        dg_ref[...] += jnp.sum(dg_rows, axis=0, keepdims=True)

    return pl.pallas_call(
        body, name="loss_bwd", grid=(t // tm,),
        in_specs=[_row_spec(tm, D_MODEL), _row_spec(tm, D_MODEL), _full_spec((1, D_MODEL))],
        out_specs=[_full_spec((1, LANES)), _row_spec(tm, D_MODEL), _full_spec((1, D_MODEL))],
        out_shape=[jax.ShapeDtypeStruct((1, LANES), F32), jax.ShapeDtypeStruct((t, D_MODEL), F32),
                   jax.ShapeDtypeStruct((1, D_MODEL), F32)],
        compiler_params=_params(dimension_semantics=("arbitrary",)),
    )(x3, target, g)


def _split_w_in(w_in):
    w_ag = w_in[:, :2 * D_CONV]
    w_qkv = w_in[:, 2 * D_CONV:2 * D_CONV + 3 * D_ATTN]
    w_f = jnp.pad(w_in[:, 2 * D_CONV + 3 * D_ATTN:], ((0, 0), (0, LANES - N_HEADS)))
    return w_ag, w_qkv, w_f


def _head_rows(v):
    return jnp.pad(v, ((0, HEAD_ROWS - N_HEADS),) + ((0, 0),) * (v.ndim - 1))


def _local_step(x, target, p, get_weights, put_grads):
    t = x.shape[0]
    fb = _head_rows(p["forget_b"].reshape(N_HEADS, 1))

    w, deps = get_weights("ffn1", None)
    x1, h1, gu1 = _ffn_fwd(x, p["ffn1_norm"], w["ffn1_w13"], w["ffn1_w2"], "ffn1_fwd", deps)
    wm, _ = get_weights("mix", x1)
    w.update(wm)
    w_ag, w_qkv, w_f = _split_w_in(w["w_in"])
    conv_w = jnp.pad(w["conv_w"], ((0, CONV_PAD - CONV_WIDTH), (0, 0)))
    h2, ag, qkv, fl = _mix_proj(x1, p["mix_norm"], w_ag, w_qkv, w_f)
    flt = _head_rows(fl[:, :N_HEADS].T)
    dcum = _gates_fwd(flt, fb)[:N_HEADS]
    drow = dcum.reshape(N_HEADS, 1, t)
    ycn = _conv_fwd(ag, conv_w, p["conv_b"], p["conv_ln_g"], p["conv_ln_b"], p["out_norm_conv"])
    o, lse = _attn_fwd(qkv, drow)
    x2, yan = _out_proj(ycn, o, p["out_norm_attn"], w["w_out"], x1)
    w2, _ = get_weights("ffn2", x2)
    w.update(w2)
    x3, h3, gu2 = _ffn_fwd(x2, p["ffn2_norm"], w["ffn2_w13"], w["ffn2_w2"], "ffn2_fwd")
    loss, dx3, d_final = _loss_bwd(x3, target, p["final_norm"])

    g = {}
    dx2, dgu2, a2, g["ffn2_norm"] = _ffn_bwd(dx3, x2, gu2, p["ffn2_norm"], w["ffn2_w13"], w["ffn2_w2"], "ffn2_bwd")
    dw13 = _wgrad(h3, dgu2, N_CHIPS, "ffn2_dw13")
    dw2 = _wgrad(a2, dx3, 1, "ffn2_dw2", scale=0.5).reshape(D_FF, D_MODEL)
    deps = put_grads("ffn2", {"ffn2_w13": dw13, "ffn2_w2": dw2}, dx2)
    dyc, do, g["out_norm_attn"] = _out_proj_bwd(dx2, o, p["out_norm_attn"], w["w_out"], deps)
    dw_out = _wgrad(jnp.concatenate([ycn, yan], axis=1), dx2, 1, "dw_out").reshape(D_MODEL, D_MODEL)
    dq, dk, dv, ddrow = _attn_bwd(qkv, drow, lse, do)
    dflt, dfb = _gates_bwd(_head_rows(ddrow.reshape(N_HEADS, t)), flt, fb)
    g["forget_b"] = dfb[:N_HEADS, 0].reshape(1, N_HEADS)
    dfl = jnp.pad(dflt[:N_HEADS].T, ((0, 0), (0, LANES - N_HEADS)))
    dag, dconv_w, g["conv_b"], g["conv_ln_g"], g["conv_ln_b"], g["out_norm_conv"] = _conv_bwd(
        ag, dyc, conv_w, p["conv_b"], p["conv_ln_g"], p["conv_ln_b"], p["out_norm_conv"])
    g["conv_w"] = dconv_w[:CONV_WIDTH]
    dqkv = jnp.concatenate([dq, dk, dv], axis=1)
    dx1, g["mix_norm"] = _mix_proj_bwd(dag, dqkv, dfl, dx2, x1, p["mix_norm"], w_ag, w_qkv, w_f)
    dproj = jnp.concatenate([dag.astype(BF16), dqkv, dfl.astype(BF16)], axis=1)
    dw_in = _wgrad(h2, dproj, 1, "dw_in").reshape(D_MODEL, dproj.shape[1])[:, :N_IN]
    deps = put_grads("mix", {"w_in": dw_in, "w_out": dw_out}, dx1)
    dx0, dgu1, a1, g["ffn1_norm"] = _ffn_bwd(dx1, x, gu1, p["ffn1_norm"], w["ffn1_w13"], w["ffn1_w2"], "ffn1_bwd", deps)
    g["final_norm"] = d_final
    g["loss"] = loss[:, :1]
    deps = put_grads("small", g, dx0)
    dw13 = _wgrad(h1, dgu1, N_CHIPS, "ffn1_dw13", deps=deps)
    dw2 = _wgrad(a1, dx1, 1, "ffn1_dw2", scale=0.5).reshape(D_FF, D_MODEL)
    put_grads("ffn1", {"ffn1_w13": dw13, "ffn1_w2": dw2}, dx0)
    return dx0


MESH = pl.DeviceIdType.MESH


def _place():
    x, y, c = lax.axis_index("x"), lax.axis_index("y"), lax.axis_index("c")
    chips = [(1 - x, y), (x, 1 - y), (1 - x, 1 - y)]
    return x, y, c, chips


def _hbm_out(shape, dtype):
    return jax.ShapeDtypeStruct(shape, dtype)


def _comm_call(body, name, ins, out_shapes, n_remote, in_place=False):
    return pl.pallas_call(
        body, name=name, in_specs=[_ANY] * len(ins), out_specs=[_ANY] * len(out_shapes), out_shape=out_shapes,
        scratch_shapes=[pltpu.SemaphoreType.DMA((n_remote,)), pltpu.SemaphoreType.DMA((n_remote,))],
        input_output_aliases={i: i for i in range(len(ins))} if in_place else {},
    )(*ins)


def _remote(src, dst, sems, n, to):
    send_sems, recv_sems = sems
    return pltpu.make_async_remote_copy(src_ref=src, dst_ref=dst, send_sem=send_sems.at[n], recv_sem=recv_sems.at[n],
                                        device_id=to, device_id_type=MESH)


def _into_slot(shard, chip, dtype, name):
    rows, cols = shard.shape
    tr = rows // 2

    def body(k_ref, s_ref, o_ref):
        o_ref[0] = s_ref[...].astype(dtype)

    return pl.pallas_call(
        body, name=name,
        grid_spec=pltpu.PrefetchScalarGridSpec(
            num_scalar_prefetch=1, grid=(rows // tr,),
            in_specs=[pl.BlockSpec((tr, cols), lambda i, k_ref: (i, 0))],
            out_specs=pl.BlockSpec((1, tr, cols), lambda i, k_ref: (k_ref[0], i, 0))),
        out_shape=jax.ShapeDtypeStruct((N_CHIPS, rows, cols), dtype),
        compiler_params=_params(dimension_semantics=("arbitrary",)),
    )(chip, shard)


def _gather_shards(slots, name, ici=True, passed=()):
    n = len(slots)
    slots = list(slots) + list(passed)
    total = len(slots)

    def body(*refs):
        outs = refs[total:total + n]
        sems = refs[2 * total:2 * total + 2]
        x, y, c, chips = _place()
        me = 2 * x + y
        sibling = (x, y, 1 - c)

        def half(i, chip_index, core):
            hr = slots[i].shape[1] // 2
            return outs[i].at[chip_index, pl.ds(core * hr, hr), :]

        sends = []
        if ici:
            for i in range(n):
                for j, chip in enumerate(chips):
                    cp = _remote(half(i, me, c), half(i, me, c), sems, 6 * i + j, (*chip, c))
                    cp.start()
                    sends.append(cp)
        for i in range(n):
            for j, chip in enumerate(chips):
                src_chip = 2 * chip[0] + chip[1]
                landed = half(i, src_chip, c)
                if ici:
                    _remote(landed, landed, sems, 6 * i + j, (*chip, c)).wait_recv()
                cp = _remote(landed, landed, sems, 6 * i + 3 + j, sibling)
                cp.start()
                sends.append(cp)
        for i in range(n):
            for j, chip in enumerate(chips):
                src_chip = 2 * chip[0] + chip[1]
                landed = half(i, src_chip, 1 - c)
                _remote(landed, landed, sems, 6 * i + 3 + j, sibling).wait_recv()
        for cp in sends:
            cp.wait_send()

    outs = [_hbm_out(s.shape, s.dtype) for s in slots]
    return _comm_call(body, name, slots, outs, 6 * n, in_place=True)


_HBM = pl.BlockSpec(memory_space=pltpu.HBM)
_SEM = pl.BlockSpec(memory_space=pltpu.SEMAPHORE)
_DATAFLOW = pltpu.SideEffectType.DATAFLOW_SIDE_EFFECTING


def _split_copy_start(name, bufs, n_copies, plan):
    n = len(bufs)

    def body(*refs):
        for send, _ in plan(refs[:n], (refs[n], refs[n + 1])):
            send.start()
        token = refs[-1]
        token[...] = jnp.zeros_like(token)

    out = pl.pallas_call(
        body, name=name,
        out_shape=(pltpu.SemaphoreType.DMA((n_copies,)), pltpu.SemaphoreType.DMA((n_copies,)),
                   *[pltpu.HBM(b.shape, b.dtype) for b in bufs], jax.ShapeDtypeStruct((8, LANES), F32)),
        in_specs=[_HBM] * n, out_specs=(_SEM, _SEM, *[_HBM] * n, pl.BlockSpec(memory_space=pltpu.VMEM)),
        input_output_aliases={i: 2 + i for i in range(n)},
        compiler_params=pltpu.CompilerParams(has_side_effects=_DATAFLOW),
    )(*[pltpu.with_memory_space_constraint(b, pltpu.HBM) for b in bufs])
    return out[0], out[1], list(out[2:2 + n]), out[-1]


def _split_copy_wait(name, started, plan, after):
    send_sems, recv_sems, bufs, _ = started
    n = len(bufs)
    after = tuple(after)

    def body(*refs):
        for send, recv in plan(refs[:n], (refs[n], refs[n + 1])):
            send.wait_send()
            recv.wait_recv()

    out = pl.pallas_call(
        body, name=name, out_shape=tuple(pltpu.HBM(b.shape, b.dtype) for b in bufs),
        in_specs=[_HBM] * n + [_SEM, _SEM] + [_ANY] * len(after), out_specs=tuple([_HBM] * n),
        input_output_aliases={i: i for i in range(n)},
        compiler_params=pltpu.CompilerParams(has_side_effects=_DATAFLOW),
    )(*bufs, send_sems, recv_sems, *after)
    return list(out)


def _ici_gather_plan(slots):
    def plan(refs, sems):
        x, y, c, chips = _place()
        me = 2 * x + y
        copies = []
        for i, ref in enumerate(refs):
            hr = slots[i].shape[1] // 2
            for j, chip in enumerate(chips):
                mine = ref.at[me, pl.ds(c * hr, hr), :]
                theirs = ref.at[2 * chip[0] + chip[1], pl.ds(c * hr, hr), :]
                to = (*chip, c)
                copies.append((_remote(mine, mine, sems, 3 * i + j, to), _remote(theirs, theirs, sems, 3 * i + j, to)))
        return copies

    return plan


def _ici_scatter_plan(n):
    def plan(refs, sems):
        x, y, c, chips = _place()
        copies = []
        for i in range(n):
            for j, chip in enumerate(chips):
                cp = _remote(refs[i].at[2 * chip[0] + chip[1]], refs[n + i].at[j], sems, 3 * i + j, (*chip, c))
                copies.append((cp, cp))
        return copies

    return plan


def _pair_exchange(grads, name):
    n = len(grads)

    def body(*refs):
        ins, outs = refs[:n], refs[n:2 * n]
        sems = refs[2 * n:2 * n + 2]
        x, y, c, _ = _place()
        sibling = (x, y, 1 - c)
        sends = []
        for i in range(n):
            hr = grads[i].shape[1] // 2
            cp = _remote(ins[i].at[:, pl.ds((1 - c) * hr, hr), :], outs[i], sems, i, sibling)
            cp.start()
            sends.append(cp)
        for cp in sends:
            cp.wait()

    outs = [_hbm_out((N_CHIPS, g.shape[1] // 2, g.shape[2]), g.dtype) for g in grads]
    return _comm_call(body, name, grads, outs, n)


def _pair_share(halves, name):
    n = len(halves)

    def body(*refs):
        outs = refs[n:2 * n]
        sems = refs[2 * n:2 * n + 2]
        x, y, c, _ = _place()
        sibling = (x, y, 1 - c)
        sends = [_remote(outs[i].at[c], outs[i].at[c], sems, i, sibling) for i in range(n)]
        for cp in sends:
            cp.start()
        for cp in sends:
            cp.wait_send()
        for i in range(n):
            _remote(outs[i].at[1 - c], outs[i].at[1 - c], sems, i, sibling).wait_recv()

    outs = [_hbm_out(h.shape, h.dtype) for h in halves]
    return _comm_call(body, name, halves, outs, n, in_place=True)


def _all_reduce_small(v, deps=()):
    rows = v.shape[0]
    flips = [(fx, fy, fc) for fx in range(2) for fy in range(2) for fc in range(2)][1:]

    def body(v_ref, o_ref, slots, send_sems, recv_sems):
        x, y, c, _ = _place()
        me = 4 * x + 2 * y + c
        slots[me] = v_ref[...]
        sends = []
        for n, (fx, fy, fc) in enumerate(flips):
            to = (x ^ fx, y ^ fy, c ^ fc)
            cp = _remote(v_ref, slots.at[me], (send_sems, recv_sems), n, to)
            cp.start()
            sends.append(cp)
        for n, (fx, fy, fc) in enumerate(flips):
            src = 4 * (x ^ fx) + 2 * (y ^ fy) + (c ^ fc)
            _remote(v_ref, slots.at[src], (send_sems, recv_sems), n, (x ^ fx, y ^ fy, c ^ fc)).wait_recv()
        for cp in sends:
            cp.wait_send()
        acc = slots[0]
        for s in range(1, 8):
            acc = acc + slots[s]
        o_ref[...] = acc

    deps = tuple(deps)
    return pl.pallas_call(
        _skip(len(deps), body), name="all_reduce_small", out_shape=jax.ShapeDtypeStruct(v.shape, F32),
        in_specs=[_ANY] * len(deps) + [pl.BlockSpec(memory_space=pltpu.VMEM)],
        out_specs=pl.BlockSpec(memory_space=pltpu.VMEM),
        scratch_shapes=[pltpu.VMEM((8, rows, LANES), F32), pltpu.SemaphoreType.DMA((7,)), pltpu.SemaphoreType.DMA((7,))],
    )(*deps, v)


def _pair_add(g, sib, core, name):
    _, r, cols = g.shape
    hr = r // 2
    g4 = g.reshape(N_CHIPS, 2, hr, cols)

    def body(c_ref, g_ref, s_ref, o_ref):
        o_ref[0] = (g_ref[0, 0].astype(F32) + s_ref[0].astype(F32)).astype(BF16)

    return pl.pallas_call(
        body, name=name,
        grid_spec=pltpu.PrefetchScalarGridSpec(
            num_scalar_prefetch=1, grid=(N_CHIPS,),
            in_specs=[pl.BlockSpec((1, 1, hr, cols), lambda s, c_ref: (s, c_ref[0], 0, 0)),
                      pl.BlockSpec((1, hr, cols), lambda s, c_ref: (s, 0, 0))],
            out_specs=pl.BlockSpec((1, hr, cols), lambda s, c_ref: (s, 0, 0))),
        out_shape=jax.ShapeDtypeStruct((N_CHIPS, hr, cols), BF16),
        compiler_params=_params(dimension_semantics=("arbitrary",)),
    )(core, g4, sib)


def _chip_add(part, recv, chip_core, name, deps=()):
    _, hr, cols = part.shape
    deps = tuple(deps)

    def body(kc_ref, *refs):
        p_ref, r_ref, o_ref = refs[len(deps):]
        acc = p_ref[0].astype(F32)
        for j in range(N_CHIPS - 1):
            acc = acc + r_ref[j].astype(F32)
        o_ref[0] = acc

    return pl.pallas_call(
        body, name=name,
        grid_spec=pltpu.PrefetchScalarGridSpec(
            num_scalar_prefetch=1, grid=(1,),
            in_specs=[_ANY] * len(deps) + [pl.BlockSpec((1, hr, cols), lambda s, kc_ref: (kc_ref[0], 0, 0)),
                                           pl.BlockSpec((N_CHIPS - 1, hr, cols), lambda s, kc_ref: (0, 0, 0))],
            out_specs=pl.BlockSpec((1, hr, cols), lambda s, kc_ref: (kc_ref[1], 0, 0))),
        out_shape=jax.ShapeDtypeStruct((2, hr, cols), F32),
        compiler_params=_params(dimension_semantics=("arbitrary",)),
    )(chip_core, *deps, part, recv)


def _adamw_math(w, g, m, v):
    m = ADAM_B1 * m + (1.0 - ADAM_B1) * g
    v = ADAM_B2 * v + (1.0 - ADAM_B2) * (g * g)
    m_hat = m / (1.0 - ADAM_B1 ** ADAM_STEP)
    v_hat = v / (1.0 - ADAM_B2 ** ADAM_STEP)
    delta = -ADAM_LR * (m_hat / (jnp.sqrt(v_hat) + ADAM_EPS) + ADAM_WD * w)
    return delta, m, v


def _adamw_matrix(w, g, m, v, name, tr):
    rows, cols = w.shape

    def body(w_ref, g_ref, m_ref, v_ref, d_ref, mo_ref, vo_ref):
        d_ref[...], mo_ref[...], vo_ref[...] = _adamw_math(w_ref[...], g_ref[...], m_ref[...], v_ref[...])

    spec = _row_spec(tr, cols)
    shape = jax.ShapeDtypeStruct((rows, cols), F32)
    return pl.pallas_call(
        body, name=name, grid=(rows // tr,), in_specs=[spec] * 4, out_specs=[spec] * 3, out_shape=[shape] * 3,
        compiler_params=_params(dimension_semantics=("arbitrary",)),
    )(w, g, m, v)


def _adamw_small(ws, gs, ms, vs):
    n = len(ws)

    def body(*refs):
        for i in range(n):
            w_ref, g_ref, m_ref, v_ref = (refs[k * n + i] for k in range(4))
            d_ref, mo_ref, vo_ref = (refs[(4 + k) * n + i] for k in range(3))
            d_ref[...], mo_ref[...], vo_ref[...] = _adamw_math(w_ref[...], g_ref[...], m_ref[...], v_ref[...])

    shapes = [jax.ShapeDtypeStruct(w.shape, F32) for w in ws]
    out = pl.pallas_call(body, name="adamw_small", out_shape=shapes * 3, compiler_params=_params())(*ws, *gs, *ms, *vs)
    return out[:n], out[n:2 * n], out[2 * n:]


MATRICES = ["ffn1_w13", "ffn1_w2", "w_in", "w_out", "ffn2_w13", "ffn2_w2"]
VECTORS = ["ffn1_norm", "mix_norm", "conv_b", "conv_ln_g", "conv_ln_b", "forget_b", "out_norm_conv",
           "out_norm_attn", "ffn2_norm", "final_norm"]
WEIGHTS = ["ffn1_norm", "ffn1_w13", "ffn1_w2", "mix_norm", "w_in", "conv_w", "conv_b", "conv_ln_g", "conv_ln_b",
           "forget_b", "out_norm_conv", "out_norm_attn", "w_out", "ffn2_norm", "ffn2_w13", "ffn2_w2", "final_norm"]
ADAM_ROWS = {"ffn1_w13": 256, "ffn2_w13": 256, "ffn1_w2": 352, "ffn2_w2": 352, "w_in": 256, "w_out": 256}


def _pack_small(g, names):
    rows, layout = [], []
    for n in names:
        flat = g[n].reshape(-1)
        pad = (-flat.shape[0]) % LANES
        rows.append(jnp.pad(flat, (0, pad)).reshape(-1, LANES))
        layout.append((n, g[n].shape, flat.shape[0], rows[-1].shape[0]))
    packed = jnp.concatenate(rows, axis=0)
    pad_rows = (-packed.shape[0]) % 8
    return jnp.pad(packed, ((0, pad_rows), (0, 0))), layout


def _unpack_small(packed, layout):
    out, r = {}, 0
    for n, shape, size, nrows in layout:
        out[n] = packed[r:r + nrows].reshape(-1)[:size].reshape(shape)
        r += nrows
    return out


def kernel(x, ffn1_norm, ffn1_w13, ffn1_w2, mix_norm, w_in, conv_w, conv_b, conv_ln_g, conv_ln_b, forget_b, out_norm_conv, out_norm_attn, w_out, ffn2_norm, ffn2_w13, ffn2_w2, final_norm, loss_target, m_ffn1_norm, m_ffn1_w13, m_ffn1_w2, m_mix_norm, m_w_in, m_conv_w, m_conv_b, m_conv_ln_g, m_conv_ln_b, m_forget_b, m_out_norm_conv, m_out_norm_attn, m_w_out, m_ffn2_norm, m_ffn2_w13, m_ffn2_w2, m_final_norm, v_ffn1_norm, v_ffn1_w13, v_ffn1_w2, v_mix_norm, v_w_in, v_conv_w, v_conv_b, v_conv_ln_g, v_conv_ln_b, v_forget_b, v_out_norm_conv, v_out_norm_attn, v_w_out, v_ffn2_norm, v_ffn2_w13, v_ffn2_w2, v_final_norm):
    args = dict(locals())
    weights = {n: args[n] for n in WEIGHTS}
    core = lax.axis_index("c").astype(jnp.int32).reshape(1)
    chip = (2 * lax.axis_index("x") + lax.axis_index("y")).astype(jnp.int32)
    chip1 = chip.reshape(1)
    chip_core = jnp.concatenate([chip1, core])

    slot = {n: _into_slot(weights[n][0], chip1, BF16, "slot_" + n) for n in MATRICES}
    slot["conv_w"] = _into_slot(jnp.pad(conv_w[0], ((0, CONV_PAD - CONV_WIDTH), (0, 0))), chip1, F32, "slot_conv_w")
    fetched = {"ffn1": ["ffn1_w13", "ffn1_w2"], "mix": ["w_in", "w_out", "conv_w"], "ffn2": ["ffn2_w13", "ffn2_w2"]}
    fetch = {}

    def as_weights(group, bufs):
        out = {}
        for n, b in zip(fetched[group], bufs):
            if n.endswith("w13"):
                out[n] = b
            elif n.endswith("w2"):
                out[n] = b.reshape(D_FF, D_MODEL)
            elif n == "w_out":
                out[n] = b.reshape(D_MODEL, D_MODEL)
            elif n == "w_in":
                out[n] = jnp.concatenate([b[k] for k in range(N_CHIPS)], axis=1)
            else:
                out[n] = b[:, :CONV_WIDTH].transpose(1, 0, 2).reshape(CONV_WIDTH, D_CONV)
        return out

    def get_weights(group, after):
        if group == "ffn1":
            later_names = fetched["mix"] + fetched["ffn2"]
            bufs = _gather_shards([slot[n] for n in fetched[group]], "gather_ffn1", passed=[slot[n] for n in later_names])
            behind = dict(zip(later_names, bufs[len(fetched[group]):]))
            for later in ("mix", "ffn2"):
                bufs_later = [behind[n] for n in fetched[later]]
                plan = _ici_gather_plan(bufs_later)
                fetch[later] = plan, _split_copy_start("gather_%s_start" % later, bufs_later, 3 * len(bufs_later), plan)
            return as_weights(group, bufs), [fetch["mix"][1][3], fetch["ffn2"][1][3]]
        plan, started = fetch[group]
        landed = _split_copy_wait("gather_%s_wait" % group, started, plan, [after])
        return as_weights(group, _gather_shards(landed, "forward_" + group, ici=False)), []

    def shard_major(n, g):
        if n.endswith("w13"):
            return g
        if n == "w_in":
            return jnp.stack([g[:, k * IN_SHARD:(k + 1) * IN_SHARD] for k in range(N_CHIPS)])
        return g.reshape(N_CHIPS, g.shape[0] // N_CHIPS, g.shape[1])

    scatter = {}
    small_names = VECTORS + ["conv_w"]
    small = {}

    def put_grads(group, grads, after):
        if group == "small":
            packed, layout = _pack_small(grads, small_names + ["loss"])
            total = _all_reduce_small(packed)
            small.update(_unpack_small(total, layout))
            return [total]
        names = list(grads)
        local = [shard_major(n, grads[n]) for n in names]
        sib = _pair_exchange(local, "pair_exchange_" + group)
        parts = [_pair_add(a, b, core, "pair_add_" + n) for a, b, n in zip(local, sib, names)]
        landing = [lax.empty((N_CHIPS - 1,) + q.shape[1:], BF16) for q in parts]
        plan = _ici_scatter_plan(len(parts))
        scatter[group] = names, plan, _split_copy_start("scatter_%s_start" % group, parts + landing, 3 * len(parts), plan)
        return [scatter[group][2][3]]

    p = {n: weights[n] for n in VECTORS}
    p["final_norm"] = final_norm.reshape(1, D_MODEL)
    dx = _local_step(x[0], loss_target[0], p, get_weights, put_grads)
    loss = small["loss"].reshape(())
    last_start = scatter["ffn1"][2][3]

    grad = {n: small[n] for n in VECTORS}
    grad["final_norm"] = small["final_norm"].reshape(D_MODEL)
    grad["conv_w"] = lax.dynamic_slice_in_dim(small["conv_w"], chip * (D_CONV // N_CHIPS), D_CONV // N_CHIPS, axis=1)[None]

    delta, new_m, new_v = {}, {}, {}

    def finish(group, after):
        names, plan, started = scatter[group]
        done = _split_copy_wait("scatter_%s_wait" % group, started, plan, after)
        parts, landed = done[:len(names)], done[len(names):]
        halves = [_chip_add(a, b, chip_core, "chip_add_" + n) for a, b, n in zip(parts, landed, names)]
        ends = []
        for n, f in zip(names, _pair_share(halves, "pair_share_" + group)):
            grad[n] = f.reshape(1, f.shape[0] * f.shape[1], f.shape[2])
            d, mo, vo = _adamw_matrix(weights[n][0], grad[n][0], args["m_" + n][0], args["v_" + n][0], "adamw_" + n, ADAM_ROWS[n])
            delta[n], new_m[n], new_v[n] = d[None], mo[None], vo[None]
            ends.append(vo)
        return ends

    done_ffn2 = finish("ffn2", [last_start])
    done_mix = finish("mix", done_ffn2)
    as2d = lambda a: a.reshape(-1, a.shape[-1])
    ds, mos, vos = _adamw_small([as2d(weights[n]) for n in small_names], [as2d(grad[n]) for n in small_names],
                                [as2d(args["m_" + n]) for n in small_names], [as2d(args["v_" + n]) for n in small_names])
    for n, d, mo, vo in zip(small_names, ds, mos, vos):
        shape = weights[n].shape
        delta[n], new_m[n], new_v[n] = d.reshape(shape), mo.reshape(shape), vo.reshape(shape)
    finish("ffn1", done_ffn2 + done_mix + [vos[0]])

    return (loss, dx[None], *[grad[n] for n in WEIGHTS], *[delta[n] for n in WEIGHTS],
            *[new_m[n] for n in WEIGHTS], *[new_v[n] for n in WEIGHTS])
```

```python
import functools

import jax
import jax.numpy as jnp
from jax import lax
from jax.experimental import pallas as pl
from jax.experimental.pallas import tpu as pltpu

F32 = jnp.float32
BF16 = jnp.bfloat16

D_MODEL = 1024
D_FF = 2816
FF_SHARD = D_FF // 2
D_CONV = 512
D_ATTN = 512
N_HEADS = 8
HEAD_DIM = 64
CONV_WIDTH = 31
CONV_PAD = 32
N_IN = 2 * D_CONV + 3 * D_ATTN + N_HEADS
IN_SHARD = N_IN // 4
EPS = 1e-6
N_CHIPS = 4
LANES = 128
HEAD_ROWS = 16

ADAM_LR = 0.001
ADAM_B1 = 0.9
ADAM_B2 = 0.999
ADAM_EPS = 1e-08
ADAM_WD = 0.01
ADAM_STEP = 10

VMEM_LIMIT = 56 * 1024 * 1024

_NT = (((1,), (1,)), ((), ()))
_TN = (((0,), (0,)), ((), ()))


def _dot(a, b):
    return jnp.dot(a, b, preferred_element_type=F32)


def _dot_nt(a, b):
    return lax.dot_general(a, b, _NT, preferred_element_type=F32)


def _dot_tn(a, b):
    return lax.dot_general(a, b, _TN, preferred_element_type=F32)


def _params(**kw):
    return pltpu.CompilerParams(vmem_limit_bytes=VMEM_LIMIT, **kw)


def _sigmoid(x):
    return 1.0 / (1.0 + jnp.exp(-x))


def _rms_stats(x):
    return lax.rsqrt(jnp.mean(x * x, axis=-1, keepdims=True) + EPS)


def _rms_bwd(x, r, g, dh):
    t = dh * g
    dx = r * t - x * (r * r * r) * jnp.mean(t * x, axis=-1, keepdims=True)
    return dx, dh * x * r


def _silu_grad(z, sg):
    return sg * (1.0 + z * (1.0 - sg))


def _row_spec(tm, n):
    return pl.BlockSpec((tm, n), lambda i: (i, 0))


def _full_spec(shape):
    nd = len(shape)
    return pl.BlockSpec(shape, lambda i: (0,) * nd)


_ANY = pl.BlockSpec(memory_space=pl.ANY)


def _skip(n, body):
    return lambda *refs: body(*refs[n:])


FFN_ROWS = 256
FFN_WEIGHT_PARTS = N_CHIPS + 2


def _with_ffn_weights(w13_hbm, w2_hbm, w13_ref, w2_ref, sems, order, tile):
    first = pl.program_id(0) == 0
    copies = {("w13", k): pltpu.make_async_copy(w13_hbm.at[k], w13_ref.at[k], sems.at[k]) for k in range(N_CHIPS)}
    for half in range(2):
        rows = pl.ds(half * FF_SHARD, FF_SHARD)
        copies["w2", half] = pltpu.make_async_copy(w2_hbm.at[rows, :], w2_ref.at[rows, :], sems.at[N_CHIPS + half])

    @pl.when(first)
    def _():
        for part in order:
            copies[part].start()

        def ready(*parts):
            for part in parts:
                copies[part].wait()

        tile(ready)

    @pl.when(jnp.logical_not(first))
    def _():
        tile(lambda *parts: None)


def _ffn_fwd(x, g, w13s, w2, name, deps=()):
    t = x.shape[0]
    tm = FFN_ROWS
    deps = tuple(deps)

    def body(x_ref, g_ref, w13_hbm, w2_hbm, xo_ref, h_ref, gu_ref, w13_ref, w2_ref, sems):
        def tile(ready):
            xv = x_ref[...]
            hb = (xv * _rms_stats(xv) * g_ref[...]).astype(BF16)
            h_ref[...] = hb
            acc = jnp.zeros((tm, D_MODEL), F32)
            for half in range(2):
                lo = half * FF_SHARD
                ready(("w13", half), ("w13", 2 + half))
                gate = _dot(hb, w13_ref[half])
                up = _dot(hb, w13_ref[2 + half])
                gu_ref[:, lo:lo + FF_SHARD] = gate.astype(BF16)
                gu_ref[:, D_FF + lo:D_FF + lo + FF_SHARD] = up.astype(BF16)
                a = (gate * _sigmoid(gate) * up).astype(BF16)
                ready(("w2", half))
                acc = acc + _dot(a, w2_ref[lo:lo + FF_SHARD, :])
            xo_ref[...] = xv + 0.5 * acc

        _with_ffn_weights(w13_hbm, w2_hbm, w13_ref, w2_ref, sems,
                          [("w13", 0), ("w13", 2), ("w2", 0), ("w13", 1), ("w13", 3), ("w2", 1)], tile)

    return pl.pallas_call(
        _skip(len(deps), body), name=name, grid=(t // tm,),
        in_specs=[_ANY] * len(deps) + [_row_spec(tm, D_MODEL), _full_spec((1, D_MODEL)), _ANY, _ANY],
        out_specs=[_row_spec(tm, D_MODEL), _row_spec(tm, D_MODEL), _row_spec(tm, 2 * D_FF)],
        out_shape=[jax.ShapeDtypeStruct((t, D_MODEL), F32), jax.ShapeDtypeStruct((t, D_MODEL), BF16),
                   jax.ShapeDtypeStruct((t, 2 * D_FF), BF16)],
        scratch_shapes=[pltpu.VMEM(w13s.shape, BF16), pltpu.VMEM(w2.shape, BF16),
                        pltpu.SemaphoreType.DMA((FFN_WEIGHT_PARTS,))],
        compiler_params=_params(dimension_semantics=("arbitrary",)),
    )(*deps, x, g, w13s, w2)


def _ffn_bwd(dy, x, gu, g, w13s, w2, name, deps=()):
    t = x.shape[0]
    tm = FFN_ROWS
    deps = tuple(deps)

    def body(dy_ref, x_ref, gu_ref, g_ref, w13_hbm, w2_hbm, dx_ref, dgu_ref, a_ref, dg_ref, w13_ref, w2_ref, sems):
        @pl.when(pl.program_id(0) == 0)
        def _():
            dg_ref[...] = jnp.zeros_like(dg_ref)

        def tile(ready):
            dyv = dy_ref[...]
            dyh = (0.5 * dyv).astype(BF16)
            dh = jnp.zeros((tm, D_MODEL), F32)
            for half in range(2):
                lo = half * FF_SHARD
                ready(("w2", half))
                da = _dot_nt(dyh, w2_ref[lo:lo + FF_SHARD, :])
                gate = gu_ref[:, lo:lo + FF_SHARD].astype(F32)
                up = gu_ref[:, D_FF + lo:D_FF + lo + FF_SHARD].astype(F32)
                sg = _sigmoid(gate)
                act = gate * sg
                a_ref[:, lo:lo + FF_SHARD] = (act * up).astype(BF16)
                dgate = (da * up * _silu_grad(gate, sg)).astype(BF16)
                dup = (da * act).astype(BF16)
                dgu_ref[:, lo:lo + FF_SHARD] = dgate
                dgu_ref[:, D_FF + lo:D_FF + lo + FF_SHARD] = dup
                ready(("w13", half), ("w13", 2 + half))
                dh = dh + _dot_nt(dgate, w13_ref[half]) + _dot_nt(dup, w13_ref[2 + half])
            xv = x_ref[...]
            dxn, dg_rows = _rms_bwd(xv, _rms_stats(xv), g_ref[...], dh)
            dx_ref[...] = dyv + dxn
            dg_ref[...] += jnp.sum(dg_rows, axis=0, keepdims=True)

        _with_ffn_weights(w13_hbm, w2_hbm, w13_ref, w2_ref, sems,
                          [("w2", 0), ("w13", 0), ("w13", 2), ("w2", 1), ("w13", 1), ("w13", 3)], tile)

    return pl.pallas_call(
        _skip(len(deps), body), name=name, grid=(t // tm,),
        in_specs=[_ANY] * len(deps) + [_row_spec(tm, D_MODEL), _row_spec(tm, D_MODEL), _row_spec(tm, 2 * D_FF),
                                       _full_spec((1, D_MODEL)), _ANY, _ANY],
        out_specs=[_row_spec(tm, D_MODEL), _row_spec(tm, 2 * D_FF), _row_spec(tm, D_FF),
                   _full_spec((1, D_MODEL))],
        out_shape=[jax.ShapeDtypeStruct((t, D_MODEL), F32), jax.ShapeDtypeStruct((t, 2 * D_FF), BF16),
                   jax.ShapeDtypeStruct((t, D_FF), BF16), jax.ShapeDtypeStruct((1, D_MODEL), F32)],
        scratch_shapes=[pltpu.VMEM(w13s.shape, BF16), pltpu.VMEM(w2.shape, BF16),
                        pltpu.SemaphoreType.DMA((FFN_WEIGHT_PARTS,))],
        compiler_params=_params(dimension_semantics=("arbitrary",)),
    )(*deps, dy, x, gu, g, w13s, w2)


def _wgrad(a, b, n_blocks, name, scale=1.0, tm=256, deps=()):
    t, m = a.shape
    n = b.shape[1]
    bn = n // n_blocks
    deps = tuple(deps)

    def body(a_ref, b_ref, o_ref):
        bv = b_ref[...]
        if scale != 1.0:
            bv = bv * scale
        o_ref[0] = _dot_tn(a_ref[...].astype(BF16), bv.astype(BF16)).astype(BF16)

    return pl.pallas_call(
        _skip(len(deps), body), name=name, grid=(n_blocks, m // tm),
        in_specs=[_ANY] * len(deps) + [pl.BlockSpec((t, tm), lambda j, i: (0, i)),
                                       pl.BlockSpec((t, bn), lambda j, i: (0, j))],
        out_specs=pl.BlockSpec((1, tm, bn), lambda j, i: (j, i, 0)),
        out_shape=jax.ShapeDtypeStruct((n_blocks, m, bn), BF16),
        compiler_params=_params(dimension_semantics=("arbitrary", "arbitrary")),
    )(*deps, a, b)


def _mix_proj(x, g, w_ag, w_qkv, w_f):
    t = x.shape[0]
    tm = 256

    def body(x_ref, g_ref, wag_ref, wqkv_ref, wf_ref, h_ref, ag_ref, qkv_ref, fl_ref):
        xv = x_ref[...]
        hb = (xv * _rms_stats(xv) * g_ref[...]).astype(BF16)
        h_ref[...] = hb
        ag_ref[...] = _dot(hb, wag_ref[...])
        qkv_ref[...] = _dot(hb, wqkv_ref[...]).astype(BF16)
        fl_ref[...] = _dot(hb, wf_ref[...])

    return pl.pallas_call(
        body, name="mix_proj", grid=(t // tm,),
        in_specs=[_row_spec(tm, D_MODEL), _full_spec((1, D_MODEL)), _full_spec(w_ag.shape),
                  _full_spec(w_qkv.shape), _full_spec(w_f.shape)],
        out_specs=[_row_spec(tm, D_MODEL), _row_spec(tm, 2 * D_CONV), _row_spec(tm, 3 * D_ATTN),
                   _row_spec(tm, LANES)],
        out_shape=[jax.ShapeDtypeStruct((t, D_MODEL), BF16), jax.ShapeDtypeStruct((t, 2 * D_CONV), F32),
                   jax.ShapeDtypeStruct((t, 3 * D_ATTN), BF16), jax.ShapeDtypeStruct((t, LANES), F32)],
        compiler_params=_params(dimension_semantics=("arbitrary",)),
    )(x, g, w_ag, w_qkv, w_f)


def _mix_proj_bwd(dag, dqkv, dfl, dx2, x1, g, w_ag, w_qkv, w_f):
    t = x1.shape[0]
    tm = 256

    def body(dag_ref, dqkv_ref, dfl_ref, dx2_ref, x_ref, g_ref, wag_ref, wqkv_ref, wf_ref, dx_ref, dg_ref):
        @pl.when(pl.program_id(0) == 0)
        def _():
            dg_ref[...] = jnp.zeros_like(dg_ref)

        dh = (_dot_nt(dag_ref[...].astype(BF16), wag_ref[...]) + _dot_nt(dqkv_ref[...], wqkv_ref[...])
              + _dot_nt(dfl_ref[...].astype(BF16), wf_ref[...]))
        xv = x_ref[...]
        dxn, dg_rows = _rms_bwd(xv, _rms_stats(xv), g_ref[...], dh)
        dx_ref[...] = dx2_ref[...] + dxn
        dg_ref[...] += jnp.sum(dg_rows, axis=0, keepdims=True)

    return pl.pallas_call(
        body, name="mix_proj_bwd", grid=(t // tm,),
        in_specs=[_row_spec(tm, 2 * D_CONV), _row_spec(tm, 3 * D_ATTN), _row_spec(tm, LANES),
                  _row_spec(tm, D_MODEL), _row_spec(tm, D_MODEL), _full_spec((1, D_MODEL)),
                  _full_spec(w_ag.shape), _full_spec(w_qkv.shape), _full_spec(w_f.shape)],
        out_specs=[_row_spec(tm, D_MODEL), _full_spec((1, D_MODEL))],
        out_shape=[jax.ShapeDtypeStruct((t, D_MODEL), F32), jax.ShapeDtypeStruct((1, D_MODEL), F32)],
        compiler_params=_params(dimension_semantics=("arbitrary",)),
    )(dag, dqkv, dfl, dx2, x1, g, w_ag, w_qkv, w_f)


def _split3(x):
    hi = x.astype(BF16)
    r1 = x - hi.astype(F32)
    mid = r1.astype(BF16)
    lo = (r1 - mid.astype(F32)).astype(BF16)
    return hi, mid, lo


def _gates_fwd(flt, fb):
    t = flt.shape[1]

    def body(f_ref, b_ref, d_ref):
        z = f_ref[...] + b_ref[...]
        logf = jnp.minimum(z, 0.0) - jnp.log(1.0 + jnp.exp(-jnp.abs(z)))
        row = lax.broadcasted_iota(jnp.int32, (LANES, LANES), 0)
        col = lax.broadcasted_iota(jnp.int32, (LANES, LANES), 1)
        upper = (row <= col).astype(BF16)
        carry = jnp.zeros((HEAD_ROWS, 1), F32)
        for blk in range(t // LANES):
            hi, mid, lo = _split3(logf[:, blk * LANES:(blk + 1) * LANES])
            cs = _dot(hi, upper) + _dot(mid, upper) + _dot(lo, upper)
            d_ref[:, blk * LANES:(blk + 1) * LANES] = cs + carry
            carry = carry + cs[:, LANES - 1:LANES]

    return pl.pallas_call(
        body, name="gates_fwd", out_shape=jax.ShapeDtypeStruct((HEAD_ROWS, t), F32),
        compiler_params=_params(),
    )(flt, fb)


def _gates_bwd(dd, flt, fb):
    t = flt.shape[1]

    def body(dd_ref, f_ref, b_ref, df_ref, db_ref):
        z = f_ref[...] + b_ref[...]
        row = lax.broadcasted_iota(jnp.int32, (LANES, LANES), 0)
        col = lax.broadcasted_iota(jnp.int32, (LANES, LANES), 1)
        lower = (row >= col).astype(BF16)
        carry = jnp.zeros((HEAD_ROWS, 1), F32)
        db = jnp.zeros((HEAD_ROWS, 1), F32)
        for blk in reversed(range(t // LANES)):
            sl = slice(blk * LANES, (blk + 1) * LANES)
            hi, mid, lo = _split3(dd_ref[:, sl])
            cs = _dot(hi, lower) + _dot(mid, lower) + _dot(lo, lower)
            dz = (cs + carry) * _sigmoid(-z[:, sl])
            df_ref[:, sl] = dz
            db = db + jnp.sum(dz, axis=1, keepdims=True)
            carry = carry + cs[:, 0:1]
        db_ref[...] = db

    return pl.pallas_call(
        body, name="gates_bwd",
        out_shape=[jax.ShapeDtypeStruct((HEAD_ROWS, t), F32), jax.ShapeDtypeStruct((HEAD_ROWS, 1), F32)],
        compiler_params=_params(),
    )(dd, flt, fb)


CONV_CHUNK = 128
CONV_TAIL = 16
CONV_WINDOW = CONV_CHUNK + CONV_PAD + 8
CONV_ROWS_EXTRA = CONV_PAD + CONV_TAIL
SUBLANES = 8


def _conv_rows(ag_ref, u_ref, t):
    u_ref[0:CONV_PAD, :] = jnp.zeros((CONV_PAD, D_CONV), F32)
    u_ref[CONV_PAD + t:CONV_ROWS_EXTRA + t, :] = jnp.zeros((CONV_TAIL, D_CONV), F32)

    def fill(i, c):
        r0 = pl.multiple_of(i * CONV_CHUNK, CONV_CHUNK)
        a = ag_ref[pl.ds(r0, CONV_CHUNK), 0:D_CONV]
        gt = ag_ref[pl.ds(r0, CONV_CHUNK), D_CONV:2 * D_CONV]
        u_ref[pl.ds(CONV_PAD + r0, CONV_CHUNK), :] = a * _sigmoid(gt)
        return c

    lax.fori_loop(0, t // CONV_CHUNK, fill, 0)


def _for_shifted(ref, r0, offsets, fn):
    window = ref[pl.ds(r0, CONV_WINDOW), :]
    for rem in range(SUBLANES):
        mine = [o for o in offsets if o % SUBLANES == rem]
        if not mine:
            continue
        turned = window if rem == 0 else pltpu.roll(window, CONV_WINDOW - rem, 0)
        for o in mine:
            fn(o, turned[o - rem:o - rem + CONV_CHUNK])


def _conv_point(u_ref, r0, w_ref, cb, lg, lb):
    acc = [jnp.zeros((CONV_CHUNK, D_CONV), F32)]

    def tap(o, rows):
        j = o - (CONV_PAD - CONV_WIDTH + 1)
        acc[0] = acc[0] + w_ref[j:j + 1, :] * rows

    _for_shifted(u_ref, r0, [j + CONV_PAD - CONV_WIDTH + 1 for j in range(CONV_WIDTH)], tap)
    y = acc[0] + cb
    mu = jnp.mean(y, axis=-1, keepdims=True)
    yc = y - mu
    rstd = lax.rsqrt(jnp.mean(yc * yc, axis=-1, keepdims=True) + EPS)
    yhat = yc * rstd
    z = yhat * lg + lb
    sg = _sigmoid(z)
    s = z * sg
    rr = _rms_stats(s)
    return yhat, rstd, z, sg, s, rr


def _conv_fwd(ag, conv_w, conv_b, ln_g, ln_b, norm_g):
    t = ag.shape[0]

    def body(ag_ref, w_ref, cb_ref, lg_ref, lb_ref, ng_ref, o_ref, u_ref):
        _conv_rows(ag_ref, u_ref, t)
        cb, lg, lb, ng = cb_ref[...], lg_ref[...], lb_ref[...], ng_ref[...]

        def chunk(i, c):
            r0 = pl.multiple_of(i * CONV_CHUNK, CONV_CHUNK)
            _, _, _, _, s, rr = _conv_point(u_ref, r0, w_ref, cb, lg, lb)
            o_ref[pl.ds(r0, CONV_CHUNK), :] = (s * rr * ng).astype(BF16)
            return c

        lax.fori_loop(0, t // CONV_CHUNK, chunk, 0)

    return pl.pallas_call(
        body, name="conv_fwd", out_shape=jax.ShapeDtypeStruct((t, D_CONV), BF16),
        scratch_shapes=[pltpu.VMEM((t + CONV_ROWS_EXTRA, D_CONV), F32)],
        compiler_params=_params(),
    )(ag, conv_w, conv_b, ln_g, ln_b, norm_g)


def _conv_bwd(ag, dout, conv_w, conv_b, ln_g, ln_b, norm_g):
    t = ag.shape[0]

    def body(ag_ref, do_ref, w_ref, cb_ref, lg_ref, lb_ref, ng_ref,
             dag_ref, dw_ref, dcb_ref, dlg_ref, dlb_ref, dng_ref, u_ref, dy_ref):
        _conv_rows(ag_ref, u_ref, t)
        dy_ref[t:t + CONV_ROWS_EXTRA, :] = jnp.zeros((CONV_ROWS_EXTRA, D_CONV), F32)
        cb, lg, lb, ng = cb_ref[...], lg_ref[...], lb_ref[...], ng_ref[...]
        dw_ref[...] = jnp.zeros_like(dw_ref)
        zero = jnp.zeros((1, D_CONV), F32)

        def chunk(i, carry):
            dcb, dlg, dlb, dng = carry
            r0 = pl.multiple_of(i * CONV_CHUNK, CONV_CHUNK)
            yhat, rstd, z, sg, s, rr = _conv_point(u_ref, r0, w_ref, cb, lg, lb)
            do = do_ref[pl.ds(r0, CONV_CHUNK), :]
            ds, dng_rows = _rms_bwd(s, rr, ng, do)
            dz = ds * _silu_grad(z, sg)
            dyhat = dz * lg
            dy = rstd * (dyhat - jnp.mean(dyhat, axis=-1, keepdims=True)
                         - yhat * jnp.mean(dyhat * yhat, axis=-1, keepdims=True))
            dy_ref[pl.ds(r0, CONV_CHUNK), :] = dy
            def tap(o, rows):
                j = o - (CONV_PAD - CONV_WIDTH + 1)
                dw_ref[j:j + 1, :] += jnp.sum(dy * rows, axis=0, keepdims=True)

            _for_shifted(u_ref, r0, [j + CONV_PAD - CONV_WIDTH + 1 for j in range(CONV_WIDTH)], tap)
            return (dcb + jnp.sum(dy, axis=0, keepdims=True), dlg + jnp.sum(dz * yhat, axis=0, keepdims=True),
                    dlb + jnp.sum(dz, axis=0, keepdims=True), dng + jnp.sum(dng_rows, axis=0, keepdims=True))

        dcb, dlg, dlb, dng = lax.fori_loop(0, t // CONV_CHUNK, chunk, (zero, zero, zero, zero))
        dcb_ref[...] = dcb
        dlg_ref[...] = dlg
        dlb_ref[...] = dlb
        dng_ref[...] = dng

        def chunk2(i, c):
            r0 = pl.multiple_of(i * CONV_CHUNK, CONV_CHUNK)
            acc = [jnp.zeros((CONV_CHUNK, D_CONV), F32)]

            def tap(o, rows):
                j = CONV_WIDTH - 1 - o
                acc[0] = acc[0] + w_ref[j:j + 1, :] * rows

            _for_shifted(dy_ref, r0, list(range(CONV_WIDTH)), tap)
            du = acc[0]
            a = ag_ref[pl.ds(r0, CONV_CHUNK), 0:D_CONV]
            gt = ag_ref[pl.ds(r0, CONV_CHUNK), D_CONV:2 * D_CONV]
            sg = _sigmoid(gt)
            dag_ref[pl.ds(r0, CONV_CHUNK), 0:D_CONV] = du * sg
            dag_ref[pl.ds(r0, CONV_CHUNK), D_CONV:2 * D_CONV] = du * a * sg * (1.0 - sg)
            return c

        lax.fori_loop(0, t // CONV_CHUNK, chunk2, 0)

    vec = jax.ShapeDtypeStruct((1, D_CONV), F32)
    return pl.pallas_call(
        body, name="conv_bwd",
        out_shape=[jax.ShapeDtypeStruct((t, 2 * D_CONV), F32), jax.ShapeDtypeStruct((CONV_PAD, D_CONV), F32),
                   vec, vec, vec, vec],
        scratch_shapes=[pltpu.VMEM((t + CONV_ROWS_EXTRA, D_CONV), F32), pltpu.VMEM((t + CONV_ROWS_EXTRA, D_CONV), F32)],
        compiler_params=_params(),
    )(ag, dout, conv_w, conv_b, ln_g, ln_b, norm_g)


Q_ROWS = 256
ATTN_SCALE = HEAD_DIM ** -0.5


def _attn_specs(t):
    blk = lambda off: pl.BlockSpec((t, LANES), lambda p: (0, off + p))
    pairs = N_HEADS // 2
    return [blk(0), blk(pairs), blk(2 * pairs), pl.BlockSpec((2, 1, t), lambda p: (p, 0, 0))]


def _one_head(q2, mask):
    return jnp.where(mask, q2, jnp.zeros_like(q2)) * ATTN_SCALE


def _attn_scores(qs, k2, drow, r0, q1):
    s = _dot_nt(qs, k2) - drow
    rowi = lax.broadcasted_iota(jnp.int32, (q1 - r0, q1 - r0), 0)
    coli = lax.broadcasted_iota(jnp.int32, (q1 - r0, q1 - r0), 1)
    diag = jnp.where(coli <= rowi, s[:, r0:q1], -jnp.inf)
    return diag if r0 == 0 else jnp.concatenate([s[:, :r0], diag], axis=1)


def _attn_fwd(qkv, drow):
    t = qkv.shape[0]

    def body(q_ref, k_ref, v_ref, dr_ref, o_ref, lse_ref):
        head_a = lax.broadcasted_iota(jnp.int32, (1, LANES), 1) < HEAD_DIM
        for qb in range(t // Q_ROWS):
            r0, q1 = qb * Q_ROWS, (qb + 1) * Q_ROWS
            q2 = q_ref[r0:q1, :]
            k2 = k_ref[0:q1, :]
            v2 = v_ref[0:q1, :]
            outs = []
            for hh in range(2):
                qs = _one_head(q2, head_a if hh == 0 else ~head_a)
                s = _attn_scores(qs, k2, dr_ref[hh, :, 0:q1], r0, q1)
                mx = jnp.max(s, axis=1, keepdims=True)
                p = jnp.exp(s - mx)
                l = jnp.sum(p, axis=1, keepdims=True)
                lse_ref[hh, r0:q1, :] = mx + jnp.log(l)
                outs.append(_dot((p * (1.0 / l)).astype(BF16), v2))
            o_ref[r0:q1, :] = jnp.where(head_a, outs[0], outs[1])

    pairs = N_HEADS // 2
    return pl.pallas_call(
        body, name="attn_fwd", grid=(pairs,), in_specs=_attn_specs(t),
        out_specs=[pl.BlockSpec((t, LANES), lambda p: (0, p)), pl.BlockSpec((2, t, 1), lambda p: (p, 0, 0))],
        out_shape=[jax.ShapeDtypeStruct((t, D_ATTN), F32), jax.ShapeDtypeStruct((N_HEADS, t, 1), F32)],
        compiler_params=_params(dimension_semantics=("arbitrary",)),
    )(qkv, qkv, qkv, drow)


def _attn_bwd(qkv, drow, lse, do):
    t = qkv.shape[0]

    def body(q_ref, k_ref, v_ref, dr_ref, lse_ref, do_ref,
             dq_ref, dk_ref, dv_ref, dd_ref, dk_acc, dv_acc):
        head_a = lax.broadcasted_iota(jnp.int32, (1, LANES), 1) < HEAD_DIM
        dk_acc[...] = jnp.zeros_like(dk_acc)
        dv_acc[...] = jnp.zeros_like(dv_acc)
        dd_ref[...] = jnp.zeros_like(dd_ref)
        for qb in range(t // Q_ROWS):
            r0, q1 = qb * Q_ROWS, (qb + 1) * Q_ROWS
            q2 = q_ref[r0:q1, :]
            k2 = k_ref[0:q1, :]
            v2 = v_ref[0:q1, :]
            do2 = do_ref[r0:q1, :]
            dqs = []
            dk_sum = jnp.zeros((q1, LANES), F32)
            dv_sum = jnp.zeros((q1, LANES), F32)
            for hh in range(2):
                mask = head_a if hh == 0 else ~head_a
                qs = _one_head(q2, mask)
                dob = jnp.where(mask, do2, 0.0).astype(BF16)
                p = jnp.exp(_attn_scores(qs, k2, dr_ref[hh, :, 0:q1], r0, q1) - lse_ref[hh, r0:q1, :])
                dp = _dot_nt(dob, v2)
                ds = p * (dp - jnp.sum(p * dp, axis=1, keepdims=True))
                dsb = ds.astype(BF16)
                dqs.append(_dot(dsb, k2) * ATTN_SCALE)
                dk_sum = dk_sum + _dot_tn(dsb, qs)
                dv_sum = dv_sum + _dot_tn(p.astype(BF16), dob)
                dd_ref[hh, :, 0:q1] -= jnp.sum(ds, axis=0, keepdims=True)
            dq_ref[r0:q1, :] = jnp.where(head_a, dqs[0], dqs[1]).astype(BF16)
            dk_acc[0:q1, :] += dk_sum
            dv_acc[0:q1, :] += dv_sum
        dk_ref[...] = dk_acc[...].astype(BF16)
        dv_ref[...] = dv_acc[...].astype(BF16)

    pairs = N_HEADS // 2
    col = pl.BlockSpec((t, LANES), lambda p: (0, p))
    grad = jax.ShapeDtypeStruct((t, D_ATTN), BF16)
    return pl.pallas_call(
        body, name="attn_bwd", grid=(pairs,),
        in_specs=_attn_specs(t) + [pl.BlockSpec((2, t, 1), lambda p: (p, 0, 0)), col],
        out_specs=[col, col, col, pl.BlockSpec((2, 1, t), lambda p: (p, 0, 0))],
        out_shape=[grad, grad, grad, jax.ShapeDtypeStruct((N_HEADS, 1, t), F32)],
        scratch_shapes=[pltpu.VMEM((t, LANES), F32), pltpu.VMEM((t, LANES), F32)],
        compiler_params=_params(dimension_semantics=("arbitrary",)),
    )(qkv, qkv, qkv, drow, lse, do)


def _out_proj(ycn, o, g_attn, w_out, x1):
    t = x1.shape[0]
    tm = 256

    def body(yc_ref, o_ref, g_ref, w_ref, x_ref, xo_ref, ya_ref):
        ov = o_ref[...]
        ya = (ov * _rms_stats(ov) * g_ref[...]).astype(BF16)
        ya_ref[...] = ya
        xo_ref[...] = x_ref[...] + _dot(yc_ref[...], w_ref[0:D_CONV, :]) + _dot(ya, w_ref[D_CONV:, :])

    return pl.pallas_call(
        body, name="out_proj", grid=(t // tm,),
        in_specs=[_row_spec(tm, D_CONV), _row_spec(tm, D_ATTN), _full_spec((1, D_ATTN)),
                  _full_spec(w_out.shape), _row_spec(tm, D_MODEL)],
        out_specs=[_row_spec(tm, D_MODEL), _row_spec(tm, D_ATTN)],
        out_shape=[jax.ShapeDtypeStruct((t, D_MODEL), F32), jax.ShapeDtypeStruct((t, D_ATTN), BF16)],
        compiler_params=_params(dimension_semantics=("arbitrary",)),
    )(ycn, o, g_attn, w_out, x1)


def _out_proj_bwd(dx2, o, g_attn, w_out, deps=()):
    t = dx2.shape[0]
    tm = 256
    deps = tuple(deps)

    def body(dx_ref, o_ref, g_ref, w_ref, dyc_ref, do_ref, dg_ref):
        @pl.when(pl.program_id(0) == 0)
        def _():
            dg_ref[...] = jnp.zeros_like(dg_ref)

        dxb = dx_ref[...].astype(BF16)
        dyc_ref[...] = _dot_nt(dxb, w_ref[0:D_CONV, :])
        dya = _dot_nt(dxb, w_ref[D_CONV:, :])
        ov = o_ref[...]
        do, dg_rows = _rms_bwd(ov, _rms_stats(ov), g_ref[...], dya)
        do_ref[...] = do
        dg_ref[...] += jnp.sum(dg_rows, axis=0, keepdims=True)

    return pl.pallas_call(
        _skip(len(deps), body), name="out_proj_bwd", grid=(t // tm,),
        in_specs=[_ANY] * len(deps) + [_row_spec(tm, D_MODEL), _row_spec(tm, D_ATTN), _full_spec((1, D_ATTN)),
                                       _full_spec(w_out.shape)],
        out_specs=[_row_spec(tm, D_CONV), _row_spec(tm, D_ATTN), _full_spec((1, D_ATTN))],
        out_shape=[jax.ShapeDtypeStruct((t, D_CONV), F32), jax.ShapeDtypeStruct((t, D_ATTN), F32),
                   jax.ShapeDtypeStruct((1, D_ATTN), F32)],
        compiler_params=_params(dimension_semantics=("arbitrary",)),
    )(*deps, dx2, o, g_attn, w_out)


def _loss_bwd(x3, target, g):
    t = x3.shape[0]
    tm = 256

    def body(x_ref, t_ref, g_ref, loss_ref, dx_ref, dg_ref):
        @pl.when(pl.program_id(0) == 0)
        def _():
            loss_ref[...] = jnp.zeros_like(loss_ref)
            dg_ref[...] = jnp.zeros_like(dg_ref)

        xv = x_ref[...]
        r = _rms_stats(xv)
        gv = g_ref[...]
        err = xv * r * gv - t_ref[...]
        row = jnp.sum(err * err, axis=1, keepdims=True) * (0.5 / D_MODEL)
        loss_ref[...] += jnp.sum(row, axis=0, keepdims=True)
        dx, dg_rows = _rms_bwd(xv, r, gv, err * (1.0 / D_MODEL))
        dx_ref[...] = dx
        dg_ref[...] += jnp.sum(dg_rows, axis=0, keepdims=True)

    return pl.pallas_call(
        body, name="loss_bwd", grid=(t // tm,),
        in_specs=[_row_spec(tm, D_MODEL), _row_spec(tm, D_MODEL), _full_spec((1, D_MODEL))],
        out_specs=[_full_spec((1, LANES)), _row_spec(tm, D_MODEL), _full_spec((1, D_MODEL))],
        out_shape=[jax.ShapeDtypeStruct((1, LANES), F32), jax.ShapeDtypeStruct((t, D_MODEL), F32),
                   jax.ShapeDtypeStruct((1, D_MODEL), F32)],
        compiler_params=_params(dimension_semantics=("arbitrary",)),
    )(x3, target, g)


def _split_w_in(w_in):
    w_ag = w_in[:, :2 * D_CONV]
    w_qkv = w_in[:, 2 * D_CONV:2 * D_CONV + 3 * D_ATTN]
    w_f = jnp.pad(w_in[:, 2 * D_CONV + 3 * D_ATTN:], ((0, 0), (0, LANES - N_HEADS)))
    return w_ag, w_qkv, w_f


def _head_rows(v):
    return jnp.pad(v, ((0, HEAD_ROWS - N_HEADS),) + ((0, 0),) * (v.ndim - 1))


def _local_step(x, target, p, get_weights, put_grads):
    t = x.shape[0]
    fb = _head_rows(p["forget_b"].reshape(N_HEADS, 1))

    w, deps = get_weights("ffn1", None)
    x1, h1, gu1 = _ffn_fwd(x, p["ffn1_norm"], w["ffn1_w13"], w["ffn1_w2"], "ffn1_fwd", deps)
    wm, _ = get_weights("mix", x1)
    w.update(wm)
    w_ag, w_qkv, w_f = _split_w_in(w["w_in"])
    conv_w = jnp.pad(w["conv_w"], ((0, CONV_PAD - CONV_WIDTH), (0, 0)))
    h2, ag, qkv, fl = _mix_proj(x1, p["mix_norm"], w_ag, w_qkv, w_f)
    flt = _head_rows(fl[:, :N_HEADS].T)
    dcum = _gates_fwd(flt, fb)[:N_HEADS]
    drow = dcum.reshape(N_HEADS, 1, t)
    ycn = _conv_fwd(ag, conv_w, p["conv_b"], p["conv_ln_g"], p["conv_ln_b"], p["out_norm_conv"])
    o, lse = _attn_fwd(qkv, drow)
    x2, yan = _out_proj(ycn, o, p["out_norm_attn"], w["w_out"], x1)
    w2, _ = get_weights("ffn2", x2)
    w.update(w2)
    x3, h3, gu2 = _ffn_fwd(x2, p["ffn2_norm"], w["ffn2_w13"], w["ffn2_w2"], "ffn2_fwd")
    loss, dx3, d_final = _loss_bwd(x3, target, p["final_norm"])

    g = {}
    dx2, dgu2, a2, g["ffn2_norm"] = _ffn_bwd(dx3, x2, gu2, p["ffn2_norm"], w["ffn2_w13"], w["ffn2_w2"], "ffn2_bwd")
    dw13 = _wgrad(h3, dgu2, N_CHIPS, "ffn2_dw13")
    dw2 = _wgrad(a2, dx3, 1, "ffn2_dw2", scale=0.5).reshape(D_FF, D_MODEL)
    deps = put_grads("ffn2", {"ffn2_w13": dw13, "ffn2_w2": dw2}, dx2)
    dyc, do, g["out_norm_attn"] = _out_proj_bwd(dx2, o, p["out_norm_attn"], w["w_out"], deps)
    dw_out = _wgrad(jnp.concatenate([ycn, yan], axis=1), dx2, 1, "dw_out").reshape(D_MODEL, D_MODEL)
    dq, dk, dv, ddrow = _attn_bwd(qkv, drow, lse, do)
    dflt, dfb = _gates_bwd(_head_rows(ddrow.reshape(N_HEADS, t)), flt, fb)
    g["forget_b"] = dfb[:N_HEADS, 0].reshape(1, N_HEADS)
    dfl = jnp.pad(dflt[:N_HEADS].T, ((0, 0), (0, LANES - N_HEADS)))
    dag, dconv_w, g["conv_b"], g["conv_ln_g"], g["conv_ln_b"], g["out_norm_conv"] = _conv_bwd(
        ag, dyc, conv_w, p["conv_b"], p["conv_ln_g"], p["conv_ln_b"], p["out_norm_conv"])
    g["conv_w"] = dconv_w[:CONV_WIDTH]
    dqkv = jnp.concatenate([dq, dk, dv], axis=1)
    dx1, g["mix_norm"] = _mix_proj_bwd(dag, dqkv, dfl, dx2, x1, p["mix_norm"], w_ag, w_qkv, w_f)
    dproj = jnp.concatenate([dag.astype(BF16), dqkv, dfl.astype(BF16)], axis=1)
    dw_in = _wgrad(h2, dproj, 1, "dw_in").reshape(D_MODEL, dproj.shape[1])[:, :N_IN]
    deps = put_grads("mix", {"w_in": dw_in, "w_out": dw_out}, dx1)
    dx0, dgu1, a1, g["ffn1_norm"] = _ffn_bwd(dx1, x, gu1, p["ffn1_norm"], w["ffn1_w13"], w["ffn1_w2"], "ffn1_bwd", deps)
    g["final_norm"] = d_final
    g["loss"] = loss[:, :1]
    deps = put_grads("small", g, dx0)
    dw13 = _wgrad(h1, dgu1, N_CHIPS, "ffn1_dw13", deps=deps)
    dw2 = _wgrad(a1, dx1, 1, "ffn1_dw2", scale=0.5).reshape(D_FF, D_MODEL)
    put_grads("ffn1", {"ffn1_w13": dw13, "ffn1_w2": dw2}, dx0)
    return dx0


MESH = pl.DeviceIdType.MESH


def _place():
    x, y, c = lax.axis_index("x"), lax.axis_index("y"), lax.axis_index("c")
    chips = [(1 - x, y), (x, 1 - y), (1 - x, 1 - y)]
    return x, y, c, chips


def _hbm_out(shape, dtype):
    return jax.ShapeDtypeStruct(shape, dtype)


def _comm_call(body, name, ins, out_shapes, n_remote, in_place=False):
    return pl.pallas_call(
        body, name=name, in_specs=[_ANY] * len(ins), out_specs=[_ANY] * len(out_shapes), out_shape=out_shapes,
        scratch_shapes=[pltpu.SemaphoreType.DMA((n_remote,)), pltpu.SemaphoreType.DMA((n_remote,))],
        input_output_aliases={i: i for i in range(len(ins))} if in_place else {},
    )(*ins)


def _remote(src, dst, sems, n, to):
    send_sems, recv_sems = sems
    return pltpu.make_async_remote_copy(src_ref=src, dst_ref=dst, send_sem=send_sems.at[n], recv_sem=recv_sems.at[n],
                                        device_id=to, device_id_type=MESH)


def _into_slot(shard, chip, dtype, name):
    rows, cols = shard.shape
    tr = rows // 2

    def body(k_ref, s_ref, o_ref):
        o_ref[0] = s_ref[...].astype(dtype)

    return pl.pallas_call(
        body, name=name,
        grid_spec=pltpu.PrefetchScalarGridSpec(
            num_scalar_prefetch=1, grid=(rows // tr,),
            in_specs=[pl.BlockSpec((tr, cols), lambda i, k_ref: (i, 0))],
            out_specs=pl.BlockSpec((1, tr, cols), lambda i, k_ref: (k_ref[0], i, 0))),
        out_shape=jax.ShapeDtypeStruct((N_CHIPS, rows, cols), dtype),
        compiler_params=_params(dimension_semantics=("arbitrary",)),
    )(chip, shard)


def _gather_shards(slots, name, ici=True, passed=()):
    n = len(slots)
    slots = list(slots) + list(passed)
    total = len(slots)

    def body(*refs):
        outs = refs[total:total + n]
        sems = refs[2 * total:2 * total + 2]
        x, y, c, chips = _place()
        me = 2 * x + y
        sibling = (x, y, 1 - c)

        def half(i, chip_index, core):
            hr = slots[i].shape[1] // 2
            return outs[i].at[chip_index, pl.ds(core * hr, hr), :]

        sends = []
        if ici:
            for i in range(n):
                for j, chip in enumerate(chips):
                    cp = _remote(half(i, me, c), half(i, me, c), sems, 6 * i + j, (*chip, c))
                    cp.start()
                    sends.append(cp)
        for i in range(n):
            for j, chip in enumerate(chips):
                src_chip = 2 * chip[0] + chip[1]
                landed = half(i, src_chip, c)
                if ici:
                    _remote(landed, landed, sems, 6 * i + j, (*chip, c)).wait_recv()
                cp = _remote(landed, landed, sems, 6 * i + 3 + j, sibling)
                cp.start()
                sends.append(cp)
        for i in range(n):
            for j, chip in enumerate(chips):
                src_chip = 2 * chip[0] + chip[1]
                landed = half(i, src_chip, 1 - c)
                _remote(landed, landed, sems, 6 * i + 3 + j, sibling).wait_recv()
        for cp in sends:
            cp.wait_send()

    outs = [_hbm_out(s.shape, s.dtype) for s in slots]
    return _comm_call(body, name, slots, outs, 6 * n, in_place=True)


_HBM = pl.BlockSpec(memory_space=pltpu.HBM)
_SEM = pl.BlockSpec(memory_space=pltpu.SEMAPHORE)
_DATAFLOW = pltpu.SideEffectType.DATAFLOW_SIDE_EFFECTING


def _split_copy_start(name, bufs, n_copies, plan):
    n = len(bufs)

    def body(*refs):
        for send, _ in plan(refs[:n], (refs[n], refs[n + 1])):
            send.start()
        token = refs[-1]
        token[...] = jnp.zeros_like(token)

    out = pl.pallas_call(
        body, name=name,
        out_shape=(pltpu.SemaphoreType.DMA((n_copies,)), pltpu.SemaphoreType.DMA((n_copies,)),
                   *[pltpu.HBM(b.shape, b.dtype) for b in bufs], jax.ShapeDtypeStruct((8, LANES), F32)),
        in_specs=[_HBM] * n, out_specs=(_SEM, _SEM, *[_HBM] * n, pl.BlockSpec(memory_space=pltpu.VMEM)),
        input_output_aliases={i: 2 + i for i in range(n)},
        compiler_params=pltpu.CompilerParams(has_side_effects=_DATAFLOW),
    )(*[pltpu.with_memory_space_constraint(b, pltpu.HBM) for b in bufs])
    return out[0], out[1], list(out[2:2 + n]), out[-1]


def _split_copy_wait(name, started, plan, after):
    send_sems, recv_sems, bufs, _ = started
    n = len(bufs)
    after = tuple(after)

    def body(*refs):
        for send, recv in plan(refs[:n], (refs[n], refs[n + 1])):
            send.wait_send()
            recv.wait_recv()

    out = pl.pallas_call(
        body, name=name, out_shape=tuple(pltpu.HBM(b.shape, b.dtype) for b in bufs),
        in_specs=[_HBM] * n + [_SEM, _SEM] + [_ANY] * len(after), out_specs=tuple([_HBM] * n),
        input_output_aliases={i: i for i in range(n)},
        compiler_params=pltpu.CompilerParams(has_side_effects=_DATAFLOW),
    )(*bufs, send_sems, recv_sems, *after)
    return list(out)


def _ici_gather_plan(slots):
    def plan(refs, sems):
        x, y, c, chips = _place()
        me = 2 * x + y
        copies = []
        for i, ref in enumerate(refs):
            hr = slots[i].shape[1] // 2
            for j, chip in enumerate(chips):
                mine = ref.at[me, pl.ds(c * hr, hr), :]
                theirs = ref.at[2 * chip[0] + chip[1], pl.ds(c * hr, hr), :]
                to = (*chip, c)
                copies.append((_remote(mine, mine, sems, 3 * i + j, to), _remote(theirs, theirs, sems, 3 * i + j, to)))
        return copies

    return plan


def _ici_scatter_plan(n):
    def plan(refs, sems):
        x, y, c, chips = _place()
        copies = []
        for i in range(n):
            for j, chip in enumerate(chips):
                cp = _remote(refs[i].at[2 * chip[0] + chip[1]], refs[n + i].at[j], sems, 3 * i + j, (*chip, c))
                copies.append((cp, cp))
        return copies

    return plan


def _pair_exchange(grads, name):
    n = len(grads)

    def body(*refs):
        ins, outs = refs[:n], refs[n:2 * n]
        sems = refs[2 * n:2 * n + 2]
        x, y, c, _ = _place()
        sibling = (x, y, 1 - c)
        sends = []
        for i in range(n):
            hr = grads[i].shape[1] // 2
            cp = _remote(ins[i].at[:, pl.ds((1 - c) * hr, hr), :], outs[i], sems, i, sibling)
            cp.start()
            sends.append(cp)
        for cp in sends:
            cp.wait()

    outs = [_hbm_out((N_CHIPS, g.shape[1] // 2, g.shape[2]), g.dtype) for g in grads]
    return _comm_call(body, name, grads, outs, n)


def _pair_share(halves, name):
    n = len(halves)

    def body(*refs):
        outs = refs[n:2 * n]
        sems = refs[2 * n:2 * n + 2]
        x, y, c, _ = _place()
        sibling = (x, y, 1 - c)
        sends = [_remote(outs[i].at[c], outs[i].at[c], sems, i, sibling) for i in range(n)]
        for cp in sends:
            cp.start()
        for cp in sends:
            cp.wait_send()
        for i in range(n):
            _remote(outs[i].at[1 - c], outs[i].at[1 - c], sems, i, sibling).wait_recv()

    outs = [_hbm_out(h.shape, h.dtype) for h in halves]
    return _comm_call(body, name, halves, outs, n, in_place=True)


def _all_reduce_small(v, deps=()):
    rows = v.shape[0]
    flips = [(fx, fy, fc) for fx in range(2) for fy in range(2) for fc in range(2)][1:]

    def body(v_ref, o_ref, slots, send_sems, recv_sems):
        x, y, c, _ = _place()
        me = 4 * x + 2 * y + c
        slots[me] = v_ref[...]
        sends = []
        for n, (fx, fy, fc) in enumerate(flips):
            to = (x ^ fx, y ^ fy, c ^ fc)
            cp = _remote(v_ref, slots.at[me], (send_sems, recv_sems), n, to)
            cp.start()
            sends.append(cp)
        for n, (fx, fy, fc) in enumerate(flips):
            src = 4 * (x ^ fx) + 2 * (y ^ fy) + (c ^ fc)
            _remote(v_ref, slots.at[src], (send_sems, recv_sems), n, (x ^ fx, y ^ fy, c ^ fc)).wait_recv()
        for cp in sends:
            cp.wait_send()
        acc = slots[0]
        for s in range(1, 8):
            acc = acc + slots[s]
        o_ref[...] = acc

    deps = tuple(deps)
    return pl.pallas_call(
        _skip(len(deps), body), name="all_reduce_small", out_shape=jax.ShapeDtypeStruct(v.shape, F32),
        in_specs=[_ANY] * len(deps) + [pl.BlockSpec(memory_space=pltpu.VMEM)],
        out_specs=pl.BlockSpec(memory_space=pltpu.VMEM),
        scratch_shapes=[pltpu.VMEM((8, rows, LANES), F32), pltpu.SemaphoreType.DMA((7,)), pltpu.SemaphoreType.DMA((7,))],
    )(*deps, v)


def _pair_add(g, sib, core, name):
    _, r, cols = g.shape
    hr = r // 2
    g4 = g.reshape(N_CHIPS, 2, hr, cols)

    def body(c_ref, g_ref, s_ref, o_ref):
        o_ref[0] = (g_ref[0, 0].astype(F32) + s_ref[0].astype(F32)).astype(BF16)

    return pl.pallas_call(
        body, name=name,
        grid_spec=pltpu.PrefetchScalarGridSpec(
            num_scalar_prefetch=1, grid=(N_CHIPS,),
            in_specs=[pl.BlockSpec((1, 1, hr, cols), lambda s, c_ref: (s, c_ref[0], 0, 0)),
                      pl.BlockSpec((1, hr, cols), lambda s, c_ref: (s, 0, 0))],
            out_specs=pl.BlockSpec((1, hr, cols), lambda s, c_ref: (s, 0, 0))),
        out_shape=jax.ShapeDtypeStruct((N_CHIPS, hr, cols), BF16),
        compiler_params=_params(dimension_semantics=("arbitrary",)),
    )(core, g4, sib)


def _chip_add(part, recv, chip_core, name, deps=()):
    _, hr, cols = part.shape
    deps = tuple(deps)

    def body(kc_ref, *refs):
        p_ref, r_ref, o_ref = refs[len(deps):]
        acc = p_ref[0].astype(F32)
        for j in range(N_CHIPS - 1):
            acc = acc + r_ref[j].astype(F32)
        o_ref[0] = acc

    return pl.pallas_call(
        body, name=name,
        grid_spec=pltpu.PrefetchScalarGridSpec(
            num_scalar_prefetch=1, grid=(1,),
            in_specs=[_ANY] * len(deps) + [pl.BlockSpec((1, hr, cols), lambda s, kc_ref: (kc_ref[0], 0, 0)),
                                           pl.BlockSpec((N_CHIPS - 1, hr, cols), lambda s, kc_ref: (0, 0, 0))],
            out_specs=pl.BlockSpec((1, hr, cols), lambda s, kc_ref: (kc_ref[1], 0, 0))),
        out_shape=jax.ShapeDtypeStruct((2, hr, cols), F32),
        compiler_params=_params(dimension_semantics=("arbitrary",)),
    )(chip_core, *deps, part, recv)


def _adamw_math(w, g, m, v):
    m = ADAM_B1 * m + (1.0 - ADAM_B1) * g
    v = ADAM_B2 * v + (1.0 - ADAM_B2) * (g * g)
    m_hat = m / (1.0 - ADAM_B1 ** ADAM_STEP)
    v_hat = v / (1.0 - ADAM_B2 ** ADAM_STEP)
    delta = -ADAM_LR * (m_hat / (jnp.sqrt(v_hat) + ADAM_EPS) + ADAM_WD * w)
    return delta, m, v


def _adamw_matrix(w, g, m, v, name, tr):
    rows, cols = w.shape

    def body(w_ref, g_ref, m_ref, v_ref, go_ref, d_ref, mo_ref, vo_ref):
        gv = g_ref[...]
        go_ref[...] = gv
        d_ref[...], mo_ref[...], vo_ref[...] = _adamw_math(w_ref[...], gv, m_ref[...], v_ref[...])

    spec = _row_spec(tr, cols)
    shape = jax.ShapeDtypeStruct((rows, cols), F32)
    return pl.pallas_call(
        body, name=name, grid=(rows // tr,), in_specs=[spec] * 4, out_specs=[spec] * 4, out_shape=[shape] * 4,
        compiler_params=_params(dimension_semantics=("arbitrary",)),
    )(w, g, m, v)


def _adamw_small(ws, gs, ms, vs):
    n = len(ws)

    def body(*refs):
        for i in range(n):
            w_ref, g_ref, m_ref, v_ref = (refs[k * n + i] for k in range(4))
            d_ref, mo_ref, vo_ref = (refs[(4 + k) * n + i] for k in range(3))
            d_ref[...], mo_ref[...], vo_ref[...] = _adamw_math(w_ref[...], g_ref[...], m_ref[...], v_ref[...])

    shapes = [jax.ShapeDtypeStruct(w.shape, F32) for w in ws]
    out = pl.pallas_call(body, name="adamw_small", out_shape=shapes * 3, compiler_params=_params())(*ws, *gs, *ms, *vs)
    return out[:n], out[n:2 * n], out[2 * n:]


MATRICES = ["ffn1_w13", "ffn1_w2", "w_in", "w_out", "ffn2_w13", "ffn2_w2"]
VECTORS = ["ffn1_norm", "mix_norm", "conv_b", "conv_ln_g", "conv_ln_b", "forget_b", "out_norm_conv",
           "out_norm_attn", "ffn2_norm", "final_norm"]
WEIGHTS = ["ffn1_norm", "ffn1_w13", "ffn1_w2", "mix_norm", "w_in", "conv_w", "conv_b", "conv_ln_g", "conv_ln_b",
           "forget_b", "out_norm_conv", "out_norm_attn", "w_out", "ffn2_norm", "ffn2_w13", "ffn2_w2", "final_norm"]
ADAM_ROWS = {"ffn1_w13": 256, "ffn2_w13": 256, "ffn1_w2": 352, "ffn2_w2": 352, "w_in": 256, "w_out": 256}


def _pack_small(g, names):
    rows, layout = [], []
    for n in names:
        flat = g[n].reshape(-1)
        pad = (-flat.shape[0]) % LANES
        rows.append(jnp.pad(flat, (0, pad)).reshape(-1, LANES))
        layout.append((n, g[n].shape, flat.shape[0], rows[-1].shape[0]))
    packed = jnp.concatenate(rows, axis=0)
    pad_rows = (-packed.shape[0]) % 8
    return jnp.pad(packed, ((0, pad_rows), (0, 0))), layout


def _unpack_small(packed, layout):
    out, r = {}, 0
    for n, shape, size, nrows in layout:
        out[n] = packed[r:r + nrows].reshape(-1)[:size].reshape(shape)
        r += nrows
    return out


def kernel(x, ffn1_norm, ffn1_w13, ffn1_w2, mix_norm, w_in, conv_w, conv_b, conv_ln_g, conv_ln_b, forget_b, out_norm_conv, out_norm_attn, w_out, ffn2_norm, ffn2_w13, ffn2_w2, final_norm, loss_target, m_ffn1_norm, m_ffn1_w13, m_ffn1_w2, m_mix_norm, m_w_in, m_conv_w, m_conv_b, m_conv_ln_g, m_conv_ln_b, m_forget_b, m_out_norm_conv, m_out_norm_attn, m_w_out, m_ffn2_norm, m_ffn2_w13, m_ffn2_w2, m_final_norm, v_ffn1_norm, v_ffn1_w13, v_ffn1_w2, v_mix_norm, v_w_in, v_conv_w, v_conv_b, v_conv_ln_g, v_conv_ln_b, v_forget_b, v_out_norm_conv, v_out_norm_attn, v_w_out, v_ffn2_norm, v_ffn2_w13, v_ffn2_w2, v_final_norm):
    args = dict(locals())
    weights = {n: args[n] for n in WEIGHTS}
    core = lax.axis_index("c").astype(jnp.int32).reshape(1)
    chip = (2 * lax.axis_index("x") + lax.axis_index("y")).astype(jnp.int32)
    chip1 = chip.reshape(1)
    chip_core = jnp.concatenate([chip1, core])

    slot = {n: _into_slot(weights[n][0], chip1, BF16, "slot_" + n) for n in MATRICES}
    slot["conv_w"] = _into_slot(jnp.pad(conv_w[0], ((0, CONV_PAD - CONV_WIDTH), (0, 0))), chip1, F32, "slot_conv_w")
    fetched = {"ffn1": ["ffn1_w13", "ffn1_w2"], "mix": ["w_in", "w_out", "conv_w"], "ffn2": ["ffn2_w13", "ffn2_w2"]}
    fetch = {}

    def as_weights(group, bufs):
        out = {}
        for n, b in zip(fetched[group], bufs):
            if n.endswith("w13"):
                out[n] = b
            elif n.endswith("w2"):
                out[n] = b.reshape(D_FF, D_MODEL)
            elif n == "w_out":
                out[n] = b.reshape(D_MODEL, D_MODEL)
            elif n == "w_in":
                out[n] = jnp.concatenate([b[k] for k in range(N_CHIPS)], axis=1)
            else:
                out[n] = b[:, :CONV_WIDTH].transpose(1, 0, 2).reshape(CONV_WIDTH, D_CONV)
        return out

    def get_weights(group, after):
        if group == "ffn1":
            later_names = fetched["mix"] + fetched["ffn2"]
            bufs = _gather_shards([slot[n] for n in fetched[group]], "gather_ffn1", passed=[slot[n] for n in later_names])
            behind = dict(zip(later_names, bufs[len(fetched[group]):]))
            for later in ("mix", "ffn2"):
                bufs_later = [behind[n] for n in fetched[later]]
                plan = _ici_gather_plan(bufs_later)
                fetch[later] = plan, _split_copy_start("gather_%s_start" % later, bufs_later, 3 * len(bufs_later), plan)
            return as_weights(group, bufs), [fetch["mix"][1][3], fetch["ffn2"][1][3]]
        plan, started = fetch[group]
        landed = _split_copy_wait("gather_%s_wait" % group, started, plan, [after])
        return as_weights(group, _gather_shards(landed, "forward_" + group, ici=False)), []

    def shard_major(n, g):
        if n.endswith("w13"):
            return g
        if n == "w_in":
            return jnp.stack([g[:, k * IN_SHARD:(k + 1) * IN_SHARD] for k in range(N_CHIPS)])
        return g.reshape(N_CHIPS, g.shape[0] // N_CHIPS, g.shape[1])

    scatter = {}
    small_names = VECTORS + ["conv_w"]
    small = {}

    def put_grads(group, grads, after):
        if group == "small":
            packed, layout = _pack_small(grads, small_names + ["loss"])
            total = _all_reduce_small(packed)
            small.update(_unpack_small(total, layout))
            return [total]
        names = list(grads)
        local = [shard_major(n, grads[n]) for n in names]
        sib = _pair_exchange(local, "pair_exchange_" + group)
        parts = [_pair_add(a, b, core, "pair_add_" + n) for a, b, n in zip(local, sib, names)]
        landing = [lax.empty((N_CHIPS - 1,) + q.shape[1:], BF16) for q in parts]
        plan = _ici_scatter_plan(len(parts))
        scatter[group] = names, plan, _split_copy_start("scatter_%s_start" % group, parts + landing, 3 * len(parts), plan)
        return [scatter[group][2][3]]

    p = {n: weights[n] for n in VECTORS}
    p["final_norm"] = final_norm.reshape(1, D_MODEL)
    dx = _local_step(x[0], loss_target[0], p, get_weights, put_grads)
    loss = small["loss"].reshape(())
    last_start = scatter["ffn1"][2][3]

    grad = {n: small[n] for n in VECTORS}
    grad["final_norm"] = small["final_norm"].reshape(D_MODEL)
    grad["conv_w"] = lax.dynamic_slice_in_dim(small["conv_w"], chip * (D_CONV // N_CHIPS), D_CONV // N_CHIPS, axis=1)[None]

    delta, new_m, new_v = {}, {}, {}

    def finish(group, after):
        names, plan, started = scatter[group]
        done = _split_copy_wait("scatter_%s_wait" % group, started, plan, after)
        parts, landed = done[:len(names)], done[len(names):]
        halves = [_chip_add(a, b, chip_core, "chip_add_" + n) for a, b, n in zip(parts, landed, names)]
        ends = []
        for n, f in zip(names, _pair_share(halves, "pair_share_" + group)):
            reduced = f.reshape(f.shape[0] * f.shape[1], f.shape[2])
            go, d, mo, vo = _adamw_matrix(weights[n][0], reduced, args["m_" + n][0], args["v_" + n][0], "adamw_" + n, ADAM_ROWS[n])
            grad[n], delta[n], new_m[n], new_v[n] = go[None], d[None], mo[None], vo[None]
            ends.append(vo)
        return ends

    done_ffn2 = finish("ffn2", [last_start])
    done_mix = finish("mix", done_ffn2)
    as2d = lambda a: a.reshape(-1, a.shape[-1])
    ds, mos, vos = _adamw_small([as2d(weights[n]) for n in small_names], [as2d(grad[n]) for n in small_names],
                                [as2d(args["m_" + n]) for n in small_names], [as2d(args["v_" + n]) for n in small_names])
    for n, d, mo, vo in zip(small_names, ds, mos, vos):
        shape = weights[n].shape
        delta[n], new_m[n], new_v[n] = d.reshape(shape), mo.reshape(shape), vo.reshape(shape)
    finish("ffn1", done_ffn2 + done_mix + [vos[0]])

    return (loss, dx[None], *[grad[n] for n in WEIGHTS], *[delta[n] for n in WEIGHTS],
            *[new_m[n] for n in WEIGHTS], *[new_v[n] for n in WEIGHTS])
```

```python
import functools

import jax
import jax.numpy as jnp
from jax import lax
from jax.experimental import pallas as pl
from jax.experimental.pallas import tpu as pltpu

F32 = jnp.float32
BF16 = jnp.bfloat16

D_MODEL = 1024
D_FF = 2816
FF_SHARD = D_FF // 2
D_CONV = 512
D_ATTN = 512
N_HEADS = 8
HEAD_DIM = 64
CONV_WIDTH = 31
CONV_PAD = 32
N_IN = 2 * D_CONV + 3 * D_ATTN + N_HEADS
IN_SHARD = N_IN // 4
EPS = 1e-6
N_CHIPS = 4
LANES = 128
HEAD_ROWS = 16

ADAM_LR = 0.001
ADAM_B1 = 0.9
ADAM_B2 = 0.999
ADAM_EPS = 1e-08
ADAM_WD = 0.01
ADAM_STEP = 10

VMEM_LIMIT = 56 * 1024 * 1024

_NT = (((1,), (1,)), ((), ()))
_TN = (((0,), (0,)), ((), ()))


def _dot(a, b):
    return jnp.dot(a, b, preferred_element_type=F32)


def _dot_nt(a, b):
    return lax.dot_general(a, b, _NT, preferred_element_type=F32)


def _dot_tn(a, b):
    return lax.dot_general(a, b, _TN, preferred_element_type=F32)


def _params(**kw):
    return pltpu.CompilerParams(vmem_limit_bytes=VMEM_LIMIT, **kw)


def _sigmoid(x):
    return 1.0 / (1.0 + jnp.exp(-x))


def _rms_stats(x):
    return lax.rsqrt(jnp.mean(x * x, axis=-1, keepdims=True) + EPS)


def _rms_bwd(x, r, g, dh):
    t = dh * g
    dx = r * t - x * (r * r * r) * jnp.mean(t * x, axis=-1, keepdims=True)
    return dx, dh * x * r


def _silu_grad(z, sg):
    return sg * (1.0 + z * (1.0 - sg))


def _row_spec(tm, n):
    return pl.BlockSpec((tm, n), lambda i: (i, 0))


def _full_spec(shape):
    nd = len(shape)
    return pl.BlockSpec(shape, lambda i: (0,) * nd)


_ANY = pl.BlockSpec(memory_space=pl.ANY)


def _skip(n, body):
    return lambda *refs: body(*refs[n:])


FFN_ROWS = 256
FFN_WEIGHT_PARTS = N_CHIPS + 2


def _with_ffn_weights(w13_hbm, w2_hbm, w13_ref, w2_ref, sems, order, tile):
    first = pl.program_id(0) == 0
    copies = {("w13", k): pltpu.make_async_copy(w13_hbm.at[k], w13_ref.at[k], sems.at[k]) for k in range(N_CHIPS)}
    for half in range(2):
        rows = pl.ds(half * FF_SHARD, FF_SHARD)
        copies["w2", half] = pltpu.make_async_copy(w2_hbm.at[rows, :], w2_ref.at[rows, :], sems.at[N_CHIPS + half])

    @pl.when(first)
    def _():
        for part in order:
            copies[part].start()

        def ready(*parts):
            for part in parts:
                copies[part].wait()

        tile(ready)

    @pl.when(jnp.logical_not(first))
    def _():
        tile(lambda *parts: None)


def _ffn_fwd(x, g, w13s, w2, name, deps=()):
    t = x.shape[0]
    tm = FFN_ROWS
    deps = tuple(deps)

    def body(x_ref, g_ref, w13_hbm, w2_hbm, xo_ref, h_ref, gu_ref, w13_ref, w2_ref, sems):
        def tile(ready):
            xv = x_ref[...]
            hb = (xv * _rms_stats(xv) * g_ref[...]).astype(BF16)
            h_ref[...] = hb
            acc = jnp.zeros((tm, D_MODEL), F32)
            for half in range(2):
                lo = half * FF_SHARD
                ready(("w13", half), ("w13", 2 + half))
                gate = _dot(hb, w13_ref[half])
                up = _dot(hb, w13_ref[2 + half])
                gu_ref[:, lo:lo + FF_SHARD] = gate.astype(BF16)
                gu_ref[:, D_FF + lo:D_FF + lo + FF_SHARD] = up.astype(BF16)
                a = (gate * _sigmoid(gate) * up).astype(BF16)
                ready(("w2", half))
                acc = acc + _dot(a, w2_ref[lo:lo + FF_SHARD, :])
            xo_ref[...] = xv + 0.5 * acc

        _with_ffn_weights(w13_hbm, w2_hbm, w13_ref, w2_ref, sems,
                          [("w13", 0), ("w13", 2), ("w2", 0), ("w13", 1), ("w13", 3), ("w2", 1)], tile)

    return pl.pallas_call(
        _skip(len(deps), body), name=name, grid=(t // tm,),
        in_specs=[_ANY] * len(deps) + [_row_spec(tm, D_MODEL), _full_spec((1, D_MODEL)), _ANY, _ANY],
        out_specs=[_row_spec(tm, D_MODEL), _row_spec(tm, D_MODEL), _row_spec(tm, 2 * D_FF)],
        out_shape=[jax.ShapeDtypeStruct((t, D_MODEL), F32), jax.ShapeDtypeStruct((t, D_MODEL), BF16),
                   jax.ShapeDtypeStruct((t, 2 * D_FF), BF16)],
        scratch_shapes=[pltpu.VMEM(w13s.shape, BF16), pltpu.VMEM(w2.shape, BF16),
                        pltpu.SemaphoreType.DMA((FFN_WEIGHT_PARTS,))],
        compiler_params=_params(dimension_semantics=("arbitrary",)),
    )(*deps, x, g, w13s, w2)


def _ffn_bwd(dy, x, gu, g, w13s, w2, name, deps=()):
    t = x.shape[0]
    tm = FFN_ROWS
    deps = tuple(deps)

    def body(dy_ref, x_ref, gu_ref, g_ref, w13_hbm, w2_hbm, dx_ref, dgu_ref, a_ref, dg_ref, dyh_ref, dxb_ref,
             w13_ref, w2_ref, sems):
        @pl.when(pl.program_id(0) == 0)
        def _():
            dg_ref[...] = jnp.zeros_like(dg_ref)

        def tile(ready):
            dyv = dy_ref[...]
            dyh = (0.5 * dyv).astype(BF16)
            dyh_ref[...] = dyh
            dh = jnp.zeros((tm, D_MODEL), F32)
            for half in range(2):
                lo = half * FF_SHARD
                ready(("w2", half))
                da = _dot_nt(dyh, w2_ref[lo:lo + FF_SHARD, :])
                gate = gu_ref[:, lo:lo + FF_SHARD].astype(F32)
                up = gu_ref[:, D_FF + lo:D_FF + lo + FF_SHARD].astype(F32)
                sg = _sigmoid(gate)
                act = gate * sg
                a_ref[:, lo:lo + FF_SHARD] = (act * up).astype(BF16)
                dgate = (da * up * _silu_grad(gate, sg)).astype(BF16)
                dup = (da * act).astype(BF16)
                dgu_ref[:, lo:lo + FF_SHARD] = dgate
                dgu_ref[:, D_FF + lo:D_FF + lo + FF_SHARD] = dup
                ready(("w13", half), ("w13", 2 + half))
                dh = dh + _dot_nt(dgate, w13_ref[half]) + _dot_nt(dup, w13_ref[2 + half])
            xv = x_ref[...]
            dxn, dg_rows = _rms_bwd(xv, _rms_stats(xv), g_ref[...], dh)
            dx = dyv + dxn
            dx_ref[...] = dx
            dxb_ref[...] = dx.astype(BF16)
            dg_ref[...] += jnp.sum(dg_rows, axis=0, keepdims=True)

        _with_ffn_weights(w13_hbm, w2_hbm, w13_ref, w2_ref, sems,
                          [("w2", 0), ("w13", 0), ("w13", 2), ("w2", 1), ("w13", 1), ("w13", 3)], tile)

    return pl.pallas_call(
        _skip(len(deps), body), name=name, grid=(t // tm,),
        in_specs=[_ANY] * len(deps) + [_row_spec(tm, D_MODEL), _row_spec(tm, D_MODEL), _row_spec(tm, 2 * D_FF),
                                       _full_spec((1, D_MODEL)), _ANY, _ANY],
        out_specs=[_row_spec(tm, D_MODEL), _row_spec(tm, 2 * D_FF), _row_spec(tm, D_FF),
                   _full_spec((1, D_MODEL)), _row_spec(tm, D_MODEL), _row_spec(tm, D_MODEL)],
        out_shape=[jax.ShapeDtypeStruct((t, D_MODEL), F32), jax.ShapeDtypeStruct((t, 2 * D_FF), BF16),
                   jax.ShapeDtypeStruct((t, D_FF), BF16), jax.ShapeDtypeStruct((1, D_MODEL), F32),
                   jax.ShapeDtypeStruct((t, D_MODEL), BF16), jax.ShapeDtypeStruct((t, D_MODEL), BF16)],
        scratch_shapes=[pltpu.VMEM(w13s.shape, BF16), pltpu.VMEM(w2.shape, BF16),
                        pltpu.SemaphoreType.DMA((FFN_WEIGHT_PARTS,))],
        compiler_params=_params(dimension_semantics=("arbitrary",)),
    )(*deps, dy, x, gu, g, w13s, w2)


WGRAD_ROWS = 512


def _wgrad(a, b, n_blocks, name, deps=()):
    t, m = a.shape
    tm = WGRAD_ROWS if m % WGRAD_ROWS == 0 else WGRAD_ROWS // 2
    n = b.shape[1]
    bn = n // n_blocks
    deps = tuple(deps)
    assert a.dtype == BF16 and b.dtype == BF16

    def body(a_ref, b_ref, o_ref):
        o_ref[0] = _dot_tn(a_ref[...], b_ref[...]).astype(BF16)

    return pl.pallas_call(
        _skip(len(deps), body), name=name, grid=(n_blocks, m // tm),
        in_specs=[_ANY] * len(deps) + [pl.BlockSpec((t, tm), lambda j, i: (0, i)),
                                       pl.BlockSpec((t, bn), lambda j, i: (0, j))],
        out_specs=pl.BlockSpec((1, tm, bn), lambda j, i: (j, i, 0)),
        out_shape=jax.ShapeDtypeStruct((n_blocks, m, bn), BF16),
        compiler_params=_params(dimension_semantics=("arbitrary", "arbitrary")),
    )(*deps, a, b)


def _mix_proj(x, g, w_ag, w_qkv, w_f):
    t = x.shape[0]
    tm = 256

    def body(x_ref, g_ref, wag_ref, wqkv_ref, wf_ref, h_ref, ag_ref, qkv_ref, fl_ref):
        xv = x_ref[...]
        hb = (xv * _rms_stats(xv) * g_ref[...]).astype(BF16)
        h_ref[...] = hb
        ag_ref[...] = _dot(hb, wag_ref[...])
        qkv_ref[...] = _dot(hb, wqkv_ref[...]).astype(BF16)
        fl_ref[...] = _dot(hb, wf_ref[...])

    return pl.pallas_call(
        body, name="mix_proj", grid=(t // tm,),
        in_specs=[_row_spec(tm, D_MODEL), _full_spec((1, D_MODEL)), _full_spec(w_ag.shape),
                  _full_spec(w_qkv.shape), _full_spec(w_f.shape)],
        out_specs=[_row_spec(tm, D_MODEL), _row_spec(tm, 2 * D_CONV), _row_spec(tm, 3 * D_ATTN),
                   _row_spec(tm, LANES)],
        out_shape=[jax.ShapeDtypeStruct((t, D_MODEL), BF16), jax.ShapeDtypeStruct((t, 2 * D_CONV), F32),
                   jax.ShapeDtypeStruct((t, 3 * D_ATTN), BF16), jax.ShapeDtypeStruct((t, LANES), F32)],
        compiler_params=_params(dimension_semantics=("arbitrary",)),
    )(x, g, w_ag, w_qkv, w_f)


def _mix_proj_bwd(dag, dqkv, dfl, dx2, x1, g, w_ag, w_qkv, w_f):
    t = x1.shape[0]
    tm = 256

    def body(dag_ref, dqkv_ref, dfl_ref, dx2_ref, x_ref, g_ref, wag_ref, wqkv_ref, wf_ref, dx_ref, dg_ref):
        @pl.when(pl.program_id(0) == 0)
        def _():
            dg_ref[...] = jnp.zeros_like(dg_ref)

        dh = (_dot_nt(dag_ref[...].astype(BF16), wag_ref[...]) + _dot_nt(dqkv_ref[...], wqkv_ref[...])
              + _dot_nt(dfl_ref[...].astype(BF16), wf_ref[...]))
        xv = x_ref[...]
        dxn, dg_rows = _rms_bwd(xv, _rms_stats(xv), g_ref[...], dh)
        dx_ref[...] = dx2_ref[...] + dxn
        dg_ref[...] += jnp.sum(dg_rows, axis=0, keepdims=True)

    return pl.pallas_call(
        body, name="mix_proj_bwd", grid=(t // tm,),
        in_specs=[_row_spec(tm, 2 * D_CONV), _row_spec(tm, 3 * D_ATTN), _row_spec(tm, LANES),
                  _row_spec(tm, D_MODEL), _row_spec(tm, D_MODEL), _full_spec((1, D_MODEL)),
                  _full_spec(w_ag.shape), _full_spec(w_qkv.shape), _full_spec(w_f.shape)],
        out_specs=[_row_spec(tm, D_MODEL), _full_spec((1, D_MODEL))],
        out_shape=[jax.ShapeDtypeStruct((t, D_MODEL), F32), jax.ShapeDtypeStruct((1, D_MODEL), F32)],
        compiler_params=_params(dimension_semantics=("arbitrary",)),
    )(dag, dqkv, dfl, dx2, x1, g, w_ag, w_qkv, w_f)


def _split3(x):
    hi = x.astype(BF16)
    r1 = x - hi.astype(F32)
    mid = r1.astype(BF16)
    lo = (r1 - mid.astype(F32)).astype(BF16)
    return hi, mid, lo


def _gates_fwd(flt, fb):
    t = flt.shape[1]

    def body(f_ref, b_ref, d_ref):
        z = f_ref[...] + b_ref[...]
        logf = jnp.minimum(z, 0.0) - jnp.log(1.0 + jnp.exp(-jnp.abs(z)))
        row = lax.broadcasted_iota(jnp.int32, (LANES, LANES), 0)
        col = lax.broadcasted_iota(jnp.int32, (LANES, LANES), 1)
        upper = (row <= col).astype(BF16)
        carry = jnp.zeros((HEAD_ROWS, 1), F32)
        for blk in range(t // LANES):
            hi, mid, lo = _split3(logf[:, blk * LANES:(blk + 1) * LANES])
            cs = _dot(hi, upper) + _dot(mid, upper) + _dot(lo, upper)
            d_ref[:, blk * LANES:(blk + 1) * LANES] = cs + carry
            carry = carry + cs[:, LANES - 1:LANES]

    return pl.pallas_call(
        body, name="gates_fwd", out_shape=jax.ShapeDtypeStruct((HEAD_ROWS, t), F32),
        compiler_params=_params(),
    )(flt, fb)


def _gates_bwd(dd, flt, fb):
    t = flt.shape[1]

    def body(dd_ref, f_ref, b_ref, df_ref, db_ref):
        z = f_ref[...] + b_ref[...]
        row = lax.broadcasted_iota(jnp.int32, (LANES, LANES), 0)
        col = lax.broadcasted_iota(jnp.int32, (LANES, LANES), 1)
        lower = (row >= col).astype(BF16)
        carry = jnp.zeros((HEAD_ROWS, 1), F32)
        db = jnp.zeros((HEAD_ROWS, 1), F32)
        for blk in reversed(range(t // LANES)):
            sl = slice(blk * LANES, (blk + 1) * LANES)
            hi, mid, lo = _split3(dd_ref[:, sl])
            cs = _dot(hi, lower) + _dot(mid, lower) + _dot(lo, lower)
            dz = (cs + carry) * _sigmoid(-z[:, sl])
            df_ref[:, sl] = dz
            db = db + jnp.sum(dz, axis=1, keepdims=True)
            carry = carry + cs[:, 0:1]
        db_ref[...] = db

    return pl.pallas_call(
        body, name="gates_bwd",
        out_shape=[jax.ShapeDtypeStruct((HEAD_ROWS, t), F32), jax.ShapeDtypeStruct((HEAD_ROWS, 1), F32)],
        compiler_params=_params(),
    )(dd, flt, fb)


CONV_CHUNK = 128
CONV_TAIL = 16
CONV_WINDOW = CONV_CHUNK + CONV_PAD + 8
CONV_ROWS_EXTRA = CONV_PAD + CONV_TAIL
SUBLANES = 8


def _conv_rows(ag_ref, u_ref, t):
    u_ref[0:CONV_PAD, :] = jnp.zeros((CONV_PAD, D_CONV), F32)
    u_ref[CONV_PAD + t:CONV_ROWS_EXTRA + t, :] = jnp.zeros((CONV_TAIL, D_CONV), F32)

    def fill(i, c):
        r0 = pl.multiple_of(i * CONV_CHUNK, CONV_CHUNK)
        a = ag_ref[pl.ds(r0, CONV_CHUNK), 0:D_CONV]
        gt = ag_ref[pl.ds(r0, CONV_CHUNK), D_CONV:2 * D_CONV]
        u_ref[pl.ds(CONV_PAD + r0, CONV_CHUNK), :] = a * _sigmoid(gt)
        return c

    lax.fori_loop(0, t // CONV_CHUNK, fill, 0)


def _for_shifted(ref, r0, offsets, fn):
    window = ref[pl.ds(r0, CONV_WINDOW), :]
    for rem in range(SUBLANES):
        mine = [o for o in offsets if o % SUBLANES == rem]
        if not mine:
            continue
        turned = window if rem == 0 else pltpu.roll(window, CONV_WINDOW - rem, 0)
        for o in mine:
            fn(o, turned[o - rem:o - rem + CONV_CHUNK])


def _conv_point(u_ref, r0, w_ref, cb, lg, lb):
    acc = [jnp.zeros((CONV_CHUNK, D_CONV), F32)]

    def tap(o, rows):
        j = o - (CONV_PAD - CONV_WIDTH + 1)
        acc[0] = acc[0] + w_ref[j:j + 1, :] * rows

    _for_shifted(u_ref, r0, [j + CONV_PAD - CONV_WIDTH + 1 for j in range(CONV_WIDTH)], tap)
    y = acc[0] + cb
    mu = jnp.mean(y, axis=-1, keepdims=True)
    yc = y - mu
    rstd = lax.rsqrt(jnp.mean(yc * yc, axis=-1, keepdims=True) + EPS)
    yhat = yc * rstd
    z = yhat * lg + lb
    sg = _sigmoid(z)
    s = z * sg
    rr = _rms_stats(s)
    return yhat, rstd, z, sg, s, rr


def _conv_fwd(ag, conv_w, conv_b, ln_g, ln_b, norm_g):
    t = ag.shape[0]

    def body(ag_ref, w_ref, cb_ref, lg_ref, lb_ref, ng_ref, o_ref, u_ref):
        _conv_rows(ag_ref, u_ref, t)
        cb, lg, lb, ng = cb_ref[...], lg_ref[...], lb_ref[...], ng_ref[...]

        def chunk(i, c):
            r0 = pl.multiple_of(i * CONV_CHUNK, CONV_CHUNK)
            _, _, _, _, s, rr = _conv_point(u_ref, r0, w_ref, cb, lg, lb)
            o_ref[pl.ds(r0, CONV_CHUNK), :] = (s * rr * ng).astype(BF16)
            return c

        lax.fori_loop(0, t // CONV_CHUNK, chunk, 0)

    return pl.pallas_call(
        body, name="conv_fwd", out_shape=jax.ShapeDtypeStruct((t, D_CONV), BF16),
        scratch_shapes=[pltpu.VMEM((t + CONV_ROWS_EXTRA, D_CONV), F32)],
        compiler_params=_params(),
    )(ag, conv_w, conv_b, ln_g, ln_b, norm_g)


def _conv_bwd(ag, dout, conv_w, conv_b, ln_g, ln_b, norm_g):
    t = ag.shape[0]

    def body(ag_ref, do_ref, w_ref, cb_ref, lg_ref, lb_ref, ng_ref,
             dag_ref, dw_ref, dcb_ref, dlg_ref, dlb_ref, dng_ref, u_ref, dy_ref):
        _conv_rows(ag_ref, u_ref, t)
        dy_ref[t:t + CONV_ROWS_EXTRA, :] = jnp.zeros((CONV_ROWS_EXTRA, D_CONV), F32)
        cb, lg, lb, ng = cb_ref[...], lg_ref[...], lb_ref[...], ng_ref[...]
        dw_ref[...] = jnp.zeros_like(dw_ref)
        zero = jnp.zeros((1, D_CONV), F32)

        def chunk(i, carry):
            dcb, dlg, dlb, dng = carry
            r0 = pl.multiple_of(i * CONV_CHUNK, CONV_CHUNK)
            yhat, rstd, z, sg, s, rr = _conv_point(u_ref, r0, w_ref, cb, lg, lb)
            do = do_ref[pl.ds(r0, CONV_CHUNK), :]
            ds, dng_rows = _rms_bwd(s, rr, ng, do)
            dz = ds * _silu_grad(z, sg)
            dyhat = dz * lg
            dy = rstd * (dyhat - jnp.mean(dyhat, axis=-1, keepdims=True)
                         - yhat * jnp.mean(dyhat * yhat, axis=-1, keepdims=True))
            dy_ref[pl.ds(r0, CONV_CHUNK), :] = dy
            def tap(o, rows):
                j = o - (CONV_PAD - CONV_WIDTH + 1)
                dw_ref[j:j + 1, :] += jnp.sum(dy * rows, axis=0, keepdims=True)

            _for_shifted(u_ref, r0, [j + CONV_PAD - CONV_WIDTH + 1 for j in range(CONV_WIDTH)], tap)
            return (dcb + jnp.sum(dy, axis=0, keepdims=True), dlg + jnp.sum(dz * yhat, axis=0, keepdims=True),
                    dlb + jnp.sum(dz, axis=0, keepdims=True), dng + jnp.sum(dng_rows, axis=0, keepdims=True))

        dcb, dlg, dlb, dng = lax.fori_loop(0, t // CONV_CHUNK, chunk, (zero, zero, zero, zero))
        dcb_ref[...] = dcb
        dlg_ref[...] = dlg
        dlb_ref[...] = dlb
        dng_ref[...] = dng

        def chunk2(i, c):
            r0 = pl.multiple_of(i * CONV_CHUNK, CONV_CHUNK)
            acc = [jnp.zeros((CONV_CHUNK, D_CONV), F32)]

            def tap(o, rows):
                j = CONV_WIDTH - 1 - o
                acc[0] = acc[0] + w_ref[j:j + 1, :] * rows

            _for_shifted(dy_ref, r0, list(range(CONV_WIDTH)), tap)
            du = acc[0]
            a = ag_ref[pl.ds(r0, CONV_CHUNK), 0:D_CONV]
            gt = ag_ref[pl.ds(r0, CONV_CHUNK), D_CONV:2 * D_CONV]
            sg = _sigmoid(gt)
            dag_ref[pl.ds(r0, CONV_CHUNK), 0:D_CONV] = du * sg
            dag_ref[pl.ds(r0, CONV_CHUNK), D_CONV:2 * D_CONV] = du * a * sg * (1.0 - sg)
            return c

        lax.fori_loop(0, t // CONV_CHUNK, chunk2, 0)

    vec = jax.ShapeDtypeStruct((1, D_CONV), F32)
    return pl.pallas_call(
        body, name="conv_bwd",
        out_shape=[jax.ShapeDtypeStruct((t, 2 * D_CONV), F32), jax.ShapeDtypeStruct((CONV_PAD, D_CONV), F32),
                   vec, vec, vec, vec],
        scratch_shapes=[pltpu.VMEM((t + CONV_ROWS_EXTRA, D_CONV), F32), pltpu.VMEM((t + CONV_ROWS_EXTRA, D_CONV), F32)],
        compiler_params=_params(),
    )(ag, dout, conv_w, conv_b, ln_g, ln_b, norm_g)


Q_ROWS = 256
ATTN_SCALE = HEAD_DIM ** -0.5


def _attn_specs(t):
    blk = lambda off: pl.BlockSpec((t, LANES), lambda p: (0, off + p))
    pairs = N_HEADS // 2
    return [blk(0), blk(pairs), blk(2 * pairs), pl.BlockSpec((2, 1, t), lambda p: (p, 0, 0))]


def _one_head(q2, mask):
    return jnp.where(mask, q2, jnp.zeros_like(q2)) * ATTN_SCALE


def _attn_scores(qs, k2, drow, r0, q1):
    s = _dot_nt(qs, k2) - drow
    rowi = lax.broadcasted_iota(jnp.int32, (q1 - r0, q1 - r0), 0)
    coli = lax.broadcasted_iota(jnp.int32, (q1 - r0, q1 - r0), 1)
    diag = jnp.where(coli <= rowi, s[:, r0:q1], -jnp.inf)
    return diag if r0 == 0 else jnp.concatenate([s[:, :r0], diag], axis=1)


def _attn_fwd(qkv, drow):
    t = qkv.shape[0]

    def body(q_ref, k_ref, v_ref, dr_ref, o_ref, lse_ref):
        head_a = lax.broadcasted_iota(jnp.int32, (1, LANES), 1) < HEAD_DIM
        for qb in range(t // Q_ROWS):
            r0, q1 = qb * Q_ROWS, (qb + 1) * Q_ROWS
            q2 = q_ref[r0:q1, :]
            k2 = k_ref[0:q1, :]
            v2 = v_ref[0:q1, :]
            outs = []
            for hh in range(2):
                qs = _one_head(q2, head_a if hh == 0 else ~head_a)
                s = _attn_scores(qs, k2, dr_ref[hh, :, 0:q1], r0, q1)
                mx = jnp.max(s, axis=1, keepdims=True)
                p = jnp.exp(s - mx)
                l = jnp.sum(p, axis=1, keepdims=True)
                lse_ref[hh, r0:q1, :] = mx + jnp.log(l)
                outs.append(_dot((p * (1.0 / l)).astype(BF16), v2))
            o_ref[r0:q1, :] = jnp.where(head_a, outs[0], outs[1])

    pairs = N_HEADS // 2
    return pl.pallas_call(
        body, name="attn_fwd", grid=(pairs,), in_specs=_attn_specs(t),
        out_specs=[pl.BlockSpec((t, LANES), lambda p: (0, p)), pl.BlockSpec((2, t, 1), lambda p: (p, 0, 0))],
        out_shape=[jax.ShapeDtypeStruct((t, D_ATTN), F32), jax.ShapeDtypeStruct((N_HEADS, t, 1), F32)],
        compiler_params=_params(dimension_semantics=("arbitrary",)),
    )(qkv, qkv, qkv, drow)


def _attn_bwd(qkv, drow, lse, do):
    t = qkv.shape[0]

    def body(q_ref, k_ref, v_ref, dr_ref, lse_ref, do_ref,
             dq_ref, dk_ref, dv_ref, dd_ref, dk_acc, dv_acc):
        head_a = lax.broadcasted_iota(jnp.int32, (1, LANES), 1) < HEAD_DIM
        dk_acc[...] = jnp.zeros_like(dk_acc)
        dv_acc[...] = jnp.zeros_like(dv_acc)
        dd_ref[...] = jnp.zeros_like(dd_ref)
        for qb in range(t // Q_ROWS):
            r0, q1 = qb * Q_ROWS, (qb + 1) * Q_ROWS
            q2 = q_ref[r0:q1, :]
            k2 = k_ref[0:q1, :]
            v2 = v_ref[0:q1, :]
            do2 = do_ref[r0:q1, :]
            dqs = []
            dk_sum = jnp.zeros((q1, LANES), F32)
            dv_sum = jnp.zeros((q1, LANES), F32)
            for hh in range(2):
                mask = head_a if hh == 0 else ~head_a
                qs = _one_head(q2, mask)
                dob = jnp.where(mask, do2, 0.0).astype(BF16)
                p = jnp.exp(_attn_scores(qs, k2, dr_ref[hh, :, 0:q1], r0, q1) - lse_ref[hh, r0:q1, :])
                dp = _dot_nt(dob, v2)
                ds = p * (dp - jnp.sum(p * dp, axis=1, keepdims=True))
                dsb = ds.astype(BF16)
                dqs.append(_dot(dsb, k2) * ATTN_SCALE)
                dk_sum = dk_sum + _dot_tn(dsb, qs)
                dv_sum = dv_sum + _dot_tn(p.astype(BF16), dob)
                dd_ref[hh, :, 0:q1] -= jnp.sum(ds, axis=0, keepdims=True)
            dq_ref[r0:q1, :] = jnp.where(head_a, dqs[0], dqs[1]).astype(BF16)
            dk_acc[0:q1, :] += dk_sum
            dv_acc[0:q1, :] += dv_sum
        dk_ref[...] = dk_acc[...].astype(BF16)
        dv_ref[...] = dv_acc[...].astype(BF16)

    pairs = N_HEADS // 2
    col = pl.BlockSpec((t, LANES), lambda p: (0, p))
    grad = jax.ShapeDtypeStruct((t, D_ATTN), BF16)
    return pl.pallas_call(
        body, name="attn_bwd", grid=(pairs,),
        in_specs=_attn_specs(t) + [pl.BlockSpec((2, t, 1), lambda p: (p, 0, 0)), col],
        out_specs=[col, col, col, pl.BlockSpec((2, 1, t), lambda p: (p, 0, 0))],
        out_shape=[grad, grad, grad, jax.ShapeDtypeStruct((N_HEADS, 1, t), F32)],
        scratch_shapes=[pltpu.VMEM((t, LANES), F32), pltpu.VMEM((t, LANES), F32)],
        compiler_params=_params(dimension_semantics=("arbitrary",)),
    )(qkv, qkv, qkv, drow, lse, do)


def _out_proj(ycn, o, g_attn, w_out, x1, deps=()):
    t = x1.shape[0]
    tm = 256
    deps = tuple(deps)

    def body(yc_ref, o_ref, g_ref, w_ref, x_ref, xo_ref, ya_ref):
        ov = o_ref[...]
        ya = (ov * _rms_stats(ov) * g_ref[...]).astype(BF16)
        ya_ref[...] = ya
        xo_ref[...] = x_ref[...] + _dot(yc_ref[...], w_ref[0:D_CONV, :]) + _dot(ya, w_ref[D_CONV:, :])

    return pl.pallas_call(
        _skip(len(deps), body), name="out_proj", grid=(t // tm,),
        in_specs=[_ANY] * len(deps) + [_row_spec(tm, D_CONV), _row_spec(tm, D_ATTN), _full_spec((1, D_ATTN)),
                                       _full_spec(w_out.shape), _row_spec(tm, D_MODEL)],
        out_specs=[_row_spec(tm, D_MODEL), _row_spec(tm, D_ATTN)],
        out_shape=[jax.ShapeDtypeStruct((t, D_MODEL), F32), jax.ShapeDtypeStruct((t, D_ATTN), BF16)],
        compiler_params=_params(dimension_semantics=("arbitrary",)),
    )(*deps, ycn, o, g_attn, w_out, x1)


def _out_proj_bwd(dx2, o, g_attn, w_out, deps=()):
    t = dx2.shape[0]
    tm = 256
    deps = tuple(deps)

    def body(dx_ref, o_ref, g_ref, w_ref, dyc_ref, do_ref, dg_ref):
        @pl.when(pl.program_id(0) == 0)
        def _():
            dg_ref[...] = jnp.zeros_like(dg_ref)

        dxb = dx_ref[...]
        dyc_ref[...] = _dot_nt(dxb, w_ref[0:D_CONV, :])
        dya = _dot_nt(dxb, w_ref[D_CONV:, :])
        ov = o_ref[...]
        do, dg_rows = _rms_bwd(ov, _rms_stats(ov), g_ref[...], dya)
        do_ref[...] = do
        dg_ref[...] += jnp.sum(dg_rows, axis=0, keepdims=True)

    return pl.pallas_call(
        _skip(len(deps), body), name="out_proj_bwd", grid=(t // tm,),
        in_specs=[_ANY] * len(deps) + [_row_spec(tm, D_MODEL), _row_spec(tm, D_ATTN), _full_spec((1, D_ATTN)),
                                       _full_spec(w_out.shape)],
        out_specs=[_row_spec(tm, D_CONV), _row_spec(tm, D_ATTN), _full_spec((1, D_ATTN))],
        out_shape=[jax.ShapeDtypeStruct((t, D_CONV), F32), jax.ShapeDtypeStruct((t, D_ATTN), F32),
                   jax.ShapeDtypeStruct((1, D_ATTN), F32)],
        compiler_params=_params(dimension_semantics=("arbitrary",)),
    )(*deps, dx2, o, g_attn, w_out)


def _loss_bwd(x3, target, g):
    t = x3.shape[0]
    tm = 256

    def body(x_ref, t_ref, g_ref, loss_ref, dx_ref, dg_ref):
        @pl.when(pl.program_id(0) == 0)
        def _():
            loss_ref[...] = jnp.zeros_like(loss_ref)
            dg_ref[...] = jnp.zeros_like(dg_ref)

        xv = x_ref[...]
        r = _rms_stats(xv)
        gv = g_ref[...]
        err = xv * r * gv - t_ref[...]
        row = jnp.sum(err * err, axis=1, keepdims=True) * (0.5 / D_MODEL)
        loss_ref[...] += jnp.sum(row, axis=0, keepdims=True)
        dx, dg_rows = _rms_bwd(xv, r, gv, err * (1.0 / D_MODEL))
        dx_ref[...] = dx
        dg_ref[...] += jnp.sum(dg_rows, axis=0, keepdims=True)

    return pl.pallas_call(
        body, name="loss_bwd", grid=(t // tm,),
        in_specs=[_row_spec(tm, D_MODEL), _row_spec(tm, D_MODEL), _full_spec((1, D_MODEL))],
        out_specs=[_full_spec((1, LANES)), _row_spec(tm, D_MODEL), _full_spec((1, D_MODEL))],
        out_shape=[jax.ShapeDtypeStruct((1, LANES), F32), jax.ShapeDtypeStruct((t, D_MODEL), F32),
                   jax.ShapeDtypeStruct((1, D_MODEL), F32)],
        compiler_params=_params(dimension_semantics=("arbitrary",)),
    )(x3, target, g)


def _split_w_in(w_in):
    w_ag = w_in[:, :2 * D_CONV]
    w_qkv = w_in[:, 2 * D_CONV:2 * D_CONV + 3 * D_ATTN]
    w_f = jnp.pad(w_in[:, 2 * D_CONV + 3 * D_ATTN:], ((0, 0), (0, LANES - N_HEADS)))
    return w_ag, w_qkv, w_f


def _head_rows(v):
    return jnp.pad(v, ((0, HEAD_ROWS - N_HEADS),) + ((0, 0),) * (v.ndim - 1))


def _local_step(x, target, p, get_weights, put_grads, flush_grads):
    t = x.shape[0]
    fb = _head_rows(p["forget_b"].reshape(N_HEADS, 1))

    w, deps = get_weights("ffn1", None)
    x1, h1, gu1 = _ffn_fwd(x, p["ffn1_norm"], w["ffn1_w13"], w["ffn1_w2"], "ffn1_fwd", deps)
    wm, _ = get_weights("mix", x1)
    w.update(wm)
    w_ag, w_qkv, w_f = _split_w_in(w["w_in"])
    conv_w = jnp.pad(w["conv_w"], ((0, CONV_PAD - CONV_WIDTH), (0, 0)))
    h2, ag, qkv, fl = _mix_proj(x1, p["mix_norm"], w_ag, w_qkv, w_f)
    flt = _head_rows(fl[:, :N_HEADS].T)
    dcum = _gates_fwd(flt, fb)[:N_HEADS]
    drow = dcum.reshape(N_HEADS, 1, t)
    ycn = _conv_fwd(ag, conv_w, p["conv_b"], p["conv_ln_g"], p["conv_ln_b"], p["out_norm_conv"])
    o, lse = _attn_fwd(qkv, drow)
    _, deps = get_weights("ffn2:landed", o)
    x2, yan = _out_proj(ycn, o, p["out_norm_attn"], w["w_out"], x1, deps)
    w2, _ = get_weights("ffn2", x2)
    w.update(w2)
    x3, h3, gu2 = _ffn_fwd(x2, p["ffn2_norm"], w["ffn2_w13"], w["ffn2_w2"], "ffn2_fwd")
    loss, dx3, d_final = _loss_bwd(x3, target, p["final_norm"])

    g = {}
    dx2, dgu2, a2, g["ffn2_norm"], dx3_half, dx2_bf16 = _ffn_bwd(
        dx3, x2, gu2, p["ffn2_norm"], w["ffn2_w13"], w["ffn2_w2"], "ffn2_bwd")
    dw13 = _wgrad(h3, dgu2, N_CHIPS, "ffn2_dw13")
    dw2 = _wgrad(a2, dx3_half, 1, "ffn2_dw2").reshape(D_FF, D_MODEL)
    deps = put_grads("ffn2", {"ffn2_w13": dw13, "ffn2_w2": dw2})
    dyc, do, g["out_norm_attn"] = _out_proj_bwd(dx2_bf16, o, p["out_norm_attn"], w["w_out"], deps)
    deps = flush_grads("ffn2", [dyc])
    dw_out = _wgrad(jnp.concatenate([ycn, yan], axis=1), dx2_bf16, 1, "dw_out", deps).reshape(D_MODEL, D_MODEL)
    dq, dk, dv, ddrow = _attn_bwd(qkv, drow, lse, do)
    dflt, dfb = _gates_bwd(_head_rows(ddrow.reshape(N_HEADS, t)), flt, fb)
    g["forget_b"] = dfb[:N_HEADS, 0].reshape(1, N_HEADS)
    dfl = jnp.pad(dflt[:N_HEADS].T, ((0, 0), (0, LANES - N_HEADS)))
    dag, dconv_w, g["conv_b"], g["conv_ln_g"], g["conv_ln_b"], g["out_norm_conv"] = _conv_bwd(
        ag, dyc, conv_w, p["conv_b"], p["conv_ln_g"], p["conv_ln_b"], p["out_norm_conv"])
    g["conv_w"] = dconv_w[:CONV_WIDTH]
    dqkv = jnp.concatenate([dq, dk, dv], axis=1)
    dx1, g["mix_norm"] = _mix_proj_bwd(dag, dqkv, dfl, dx2, x1, p["mix_norm"], w_ag, w_qkv, w_f)
    dproj = jnp.concatenate([dag.astype(BF16), dqkv, dfl.astype(BF16)], axis=1)
    dw_in = _wgrad(h2, dproj, 1, "dw_in").reshape(D_MODEL, dproj.shape[1])[:, :N_IN]
    deps = put_grads("mix", {"w_in": dw_in, "w_out": dw_out})
    dx0, dgu1, a1, g["ffn1_norm"], dx1_half, _ = _ffn_bwd(
        dx1, x, gu1, p["ffn1_norm"], w["ffn1_w13"], w["ffn1_w2"], "ffn1_bwd", deps)
    g["final_norm"] = d_final
    g["loss"] = loss[:, :1]
    deps = flush_grads("mix", put_grads("small", g))
    dw13 = _wgrad(h1, dgu1, N_CHIPS, "ffn1_dw13", deps)
    dw2 = _wgrad(a1, dx1_half, 1, "ffn1_dw2").reshape(D_FF, D_MODEL)
    put_grads("ffn1", {"ffn1_w13": dw13, "ffn1_w2": dw2})
    return dx0


MESH = pl.DeviceIdType.MESH


def _place():
    x, y, c = lax.axis_index("x"), lax.axis_index("y"), lax.axis_index("c")
    chips = [(1 - x, y), (x, 1 - y), (1 - x, 1 - y)]
    return x, y, c, chips


def _hbm_out(shape, dtype):
    return jax.ShapeDtypeStruct(shape, dtype)


def _comm_call(body, name, ins, out_shapes, n_remote, in_place=False):
    return pl.pallas_call(
        body, name=name, in_specs=[_ANY] * len(ins), out_specs=[_ANY] * len(out_shapes), out_shape=out_shapes,
        scratch_shapes=[pltpu.SemaphoreType.DMA((n_remote,)), pltpu.SemaphoreType.DMA((n_remote,))],
        input_output_aliases={i: i for i in range(len(ins))} if in_place else {},
    )(*ins)


def _remote(src, dst, sems, n, to):
    send_sems, recv_sems = sems
    return pltpu.make_async_remote_copy(src_ref=src, dst_ref=dst, send_sem=send_sems.at[n], recv_sem=recv_sems.at[n],
                                        device_id=to, device_id_type=MESH)


def _into_slot(shard, chip, dtype, name):
    rows, cols = shard.shape
    tr = rows // 2

    def body(k_ref, s_ref, o_ref):
        o_ref[0] = s_ref[...].astype(dtype)

    return pl.pallas_call(
        body, name=name,
        grid_spec=pltpu.PrefetchScalarGridSpec(
            num_scalar_prefetch=1, grid=(rows // tr,),
            in_specs=[pl.BlockSpec((tr, cols), lambda i, k_ref: (i, 0))],
            out_specs=pl.BlockSpec((1, tr, cols), lambda i, k_ref: (k_ref[0], i, 0))),
        out_shape=jax.ShapeDtypeStruct((N_CHIPS, rows, cols), dtype),
        compiler_params=_params(dimension_semantics=("arbitrary",)),
    )(chip, shard)


def _gather_shards(slots, name, ici=True, passed=()):
    n = len(slots)
    slots = list(slots) + list(passed)
    total = len(slots)

    def body(*refs):
        outs = refs[total:total + n]
        sems = refs[2 * total:2 * total + 2]
        x, y, c, chips = _place()
        me = 2 * x + y
        sibling = (x, y, 1 - c)

        def half(i, chip_index, core):
            hr = slots[i].shape[1] // 2
            return outs[i].at[chip_index, pl.ds(core * hr, hr), :]

        sends = []
        if ici:
            for i in range(n):
                for j, chip in enumerate(chips):
                    cp = _remote(half(i, me, c), half(i, me, c), sems, 6 * i + j, (*chip, c))
                    cp.start()
                    sends.append(cp)
        for i in range(n):
            for j, chip in enumerate(chips):
                src_chip = 2 * chip[0] + chip[1]
                landed = half(i, src_chip, c)
                if ici:
                    _remote(landed, landed, sems, 6 * i + j, (*chip, c)).wait_recv()
                cp = _remote(landed, landed, sems, 6 * i + 3 + j, sibling)
                cp.start()
                sends.append(cp)
        for i in range(n):
            for j, chip in enumerate(chips):
                src_chip = 2 * chip[0] + chip[1]
                landed = half(i, src_chip, 1 - c)
                _remote(landed, landed, sems, 6 * i + 3 + j, sibling).wait_recv()
        for cp in sends:
            cp.wait_send()

    outs = [_hbm_out(s.shape, s.dtype) for s in slots]
    return _comm_call(body, name, slots, outs, 6 * n, in_place=True)


_HBM = pl.BlockSpec(memory_space=pltpu.HBM)
_SEM = pl.BlockSpec(memory_space=pltpu.SEMAPHORE)
_DATAFLOW = pltpu.SideEffectType.DATAFLOW_SIDE_EFFECTING


def _split_copy_start(name, bufs, n_copies, plan):
    n = len(bufs)

    def body(*refs):
        for send, _ in plan(refs[:n], (refs[n], refs[n + 1])):
            send.start()
        token = refs[-1]
        token[...] = jnp.zeros_like(token)

    out = pl.pallas_call(
        body, name=name,
        out_shape=(pltpu.SemaphoreType.DMA((n_copies,)), pltpu.SemaphoreType.DMA((n_copies,)),
                   *[pltpu.HBM(b.shape, b.dtype) for b in bufs], jax.ShapeDtypeStruct((8, LANES), F32)),
        in_specs=[_HBM] * n, out_specs=(_SEM, _SEM, *[_HBM] * n, pl.BlockSpec(memory_space=pltpu.VMEM)),
        input_output_aliases={i: 2 + i for i in range(n)},
        compiler_params=pltpu.CompilerParams(has_side_effects=_DATAFLOW),
    )(*[pltpu.with_memory_space_constraint(b, pltpu.HBM) for b in bufs])
    return out[0], out[1], list(out[2:2 + n]), out[-1]


def _split_copy_wait(name, started, plan, after):
    send_sems, recv_sems, bufs, _ = started
    n = len(bufs)
    after = tuple(after)

    def body(*refs):
        for send, recv in plan(refs[:n], (refs[n], refs[n + 1])):
            send.wait_send()
            recv.wait_recv()

    out = pl.pallas_call(
        body, name=name, out_shape=tuple(pltpu.HBM(b.shape, b.dtype) for b in bufs),
        in_specs=[_HBM] * n + [_SEM, _SEM] + [_ANY] * len(after), out_specs=tuple([_HBM] * n),
        input_output_aliases={i: i for i in range(n)},
        compiler_params=pltpu.CompilerParams(has_side_effects=_DATAFLOW),
    )(*bufs, send_sems, recv_sems, *after)
    return list(out)


def _ici_gather_plan(slots):
    def plan(refs, sems):
        x, y, c, chips = _place()
        me = 2 * x + y
        copies = []
        for i, ref in enumerate(refs):
            hr = slots[i].shape[1] // 2
            for j, chip in enumerate(chips):
                mine = ref.at[me, pl.ds(c * hr, hr), :]
                theirs = ref.at[2 * chip[0] + chip[1], pl.ds(c * hr, hr), :]
                to = (*chip, c)
                copies.append((_remote(mine, mine, sems, 3 * i + j, to), _remote(theirs, theirs, sems, 3 * i + j, to)))
        return copies

    return plan


def _ici_scatter_plan(n):
    def plan(refs, sems):
        x, y, c, chips = _place()
        copies = []
        for i in range(n):
            for j, chip in enumerate(chips):
                cp = _remote(refs[i].at[2 * chip[0] + chip[1]], refs[n + i].at[j], sems, 3 * i + j, (*chip, c))
                copies.append((cp, cp))
        return copies

    return plan


def _d2d_forward_plan(slots):
    def plan(refs, sems):
        x, y, c, chips = _place()
        sibling = (x, y, 1 - c)
        copies = []
        for i, ref in enumerate(refs):
            hr = slots[i].shape[1] // 2
            for j, chip in enumerate(chips):
                src_chip = 2 * chip[0] + chip[1]
                mine = ref.at[src_chip, pl.ds(c * hr, hr), :]
                theirs = ref.at[src_chip, pl.ds((1 - c) * hr, hr), :]
                copies.append((_remote(mine, mine, sems, 3 * i + j, sibling),
                               _remote(theirs, theirs, sems, 3 * i + j, sibling)))
        return copies

    return plan


def _pair_exchange_plan(grads):
    n = len(grads)

    def plan(refs, sems):
        x, y, c, _ = _place()
        copies = []
        for i in range(n):
            hr = grads[i].shape[1] // 2
            cp = _remote(refs[i].at[:, pl.ds((1 - c) * hr, hr), :], refs[n + i], sems, i, (x, y, 1 - c))
            copies.append((cp, cp))
        return copies

    return plan


def _pair_share_plan():
    def plan(refs, sems):
        x, y, c, _ = _place()
        sibling = (x, y, 1 - c)
        return [(_remote(ref.at[c], ref.at[c], sems, i, sibling), _remote(ref.at[1 - c], ref.at[1 - c], sems, i, sibling))
                for i, ref in enumerate(refs)]

    return plan


def _pair_share(halves, name):
    n = len(halves)

    def body(*refs):
        outs = refs[n:2 * n]
        sems = refs[2 * n:2 * n + 2]
        x, y, c, _ = _place()
        sibling = (x, y, 1 - c)
        sends = [_remote(outs[i].at[c], outs[i].at[c], sems, i, sibling) for i in range(n)]
        for cp in sends:
            cp.start()
        for cp in sends:
            cp.wait_send()
        for i in range(n):
            _remote(outs[i].at[1 - c], outs[i].at[1 - c], sems, i, sibling).wait_recv()

    outs = [_hbm_out(h.shape, h.dtype) for h in halves]
    return _comm_call(body, name, halves, outs, n, in_place=True)


def _all_reduce_small(v, deps=()):
    rows = v.shape[0]
    flips = [(fx, fy, fc) for fx in range(2) for fy in range(2) for fc in range(2)][1:]

    def body(v_ref, o_ref, slots, send_sems, recv_sems):
        x, y, c, _ = _place()
        me = 4 * x + 2 * y + c
        slots[me] = v_ref[...]
        sends = []
        for n, (fx, fy, fc) in enumerate(flips):
            to = (x ^ fx, y ^ fy, c ^ fc)
            cp = _remote(v_ref, slots.at[me], (send_sems, recv_sems), n, to)
            cp.start()
            sends.append(cp)
        for n, (fx, fy, fc) in enumerate(flips):
            src = 4 * (x ^ fx) + 2 * (y ^ fy) + (c ^ fc)
            _remote(v_ref, slots.at[src], (send_sems, recv_sems), n, (x ^ fx, y ^ fy, c ^ fc)).wait_recv()
        for cp in sends:
            cp.wait_send()
        acc = slots[0]
        for s in range(1, 8):
            acc = acc + slots[s]
        o_ref[...] = acc

    deps = tuple(deps)
    return pl.pallas_call(
        _skip(len(deps), body), name="all_reduce_small", out_shape=jax.ShapeDtypeStruct(v.shape, F32),
        in_specs=[_ANY] * len(deps) + [pl.BlockSpec(memory_space=pltpu.VMEM)],
        out_specs=pl.BlockSpec(memory_space=pltpu.VMEM),
        scratch_shapes=[pltpu.VMEM((8, rows, LANES), F32), pltpu.SemaphoreType.DMA((7,)), pltpu.SemaphoreType.DMA((7,))],
    )(*deps, v)


def _pair_add(g, sib, core, name):
    _, r, cols = g.shape
    hr = r // 2
    g4 = g.reshape(N_CHIPS, 2, hr, cols)

    def body(c_ref, g_ref, s_ref, o_ref):
        o_ref[0] = (g_ref[0, 0].astype(F32) + s_ref[0].astype(F32)).astype(BF16)

    return pl.pallas_call(
        body, name=name,
        grid_spec=pltpu.PrefetchScalarGridSpec(
            num_scalar_prefetch=1, grid=(N_CHIPS,),
            in_specs=[pl.BlockSpec((1, 1, hr, cols), lambda s, c_ref: (s, c_ref[0], 0, 0)),
                      pl.BlockSpec((1, hr, cols), lambda s, c_ref: (s, 0, 0))],
            out_specs=pl.BlockSpec((1, hr, cols), lambda s, c_ref: (s, 0, 0))),
        out_shape=jax.ShapeDtypeStruct((N_CHIPS, hr, cols), BF16),
        compiler_params=_params(dimension_semantics=("arbitrary",)),
    )(core, g4, sib)


def _chip_add(part, recv, chip_core, name, deps=()):
    _, hr, cols = part.shape
    deps = tuple(deps)

    def body(kc_ref, *refs):
        p_ref, r_ref, o_ref = refs[len(deps):]
        acc = p_ref[0].astype(F32)
        for j in range(N_CHIPS - 1):
            acc = acc + r_ref[j].astype(F32)
        o_ref[0] = acc

    return pl.pallas_call(
        body, name=name,
        grid_spec=pltpu.PrefetchScalarGridSpec(
            num_scalar_prefetch=1, grid=(1,),
            in_specs=[_ANY] * len(deps) + [pl.BlockSpec((1, hr, cols), lambda s, kc_ref: (kc_ref[0], 0, 0)),
                                           pl.BlockSpec((N_CHIPS - 1, hr, cols), lambda s, kc_ref: (0, 0, 0))],
            out_specs=pl.BlockSpec((1, hr, cols), lambda s, kc_ref: (kc_ref[1], 0, 0))),
        out_shape=jax.ShapeDtypeStruct((2, hr, cols), F32),
        compiler_params=_params(dimension_semantics=("arbitrary",)),
    )(chip_core, *deps, part, recv)


def _adamw_math(w, g, m, v):
    m = ADAM_B1 * m + (1.0 - ADAM_B1) * g
    v = ADAM_B2 * v + (1.0 - ADAM_B2) * (g * g)
    m_hat = m / (1.0 - ADAM_B1 ** ADAM_STEP)
    v_hat = v / (1.0 - ADAM_B2 ** ADAM_STEP)
    delta = -ADAM_LR * (m_hat / (jnp.sqrt(v_hat) + ADAM_EPS) + ADAM_WD * w)
    return delta, m, v


def _adamw_matrix(w, g, m, v, name, tr):
    rows, cols = w.shape

    def body(w_ref, g_ref, m_ref, v_ref, go_ref, d_ref, mo_ref, vo_ref):
        gv = g_ref[...]
        go_ref[...] = gv
        d_ref[...], mo_ref[...], vo_ref[...] = _adamw_math(w_ref[...], gv, m_ref[...], v_ref[...])

    spec = _row_spec(tr, cols)
    shape = jax.ShapeDtypeStruct((rows, cols), F32)
    return pl.pallas_call(
        body, name=name, grid=(rows // tr,), in_specs=[spec] * 4, out_specs=[spec] * 4, out_shape=[shape] * 4,
        compiler_params=_params(dimension_semantics=("arbitrary",)),
    )(w, g, m, v)


def _adamw_small(ws, gs, ms, vs):
    n = len(ws)

    def body(*refs):
        for i in range(n):
            w_ref, g_ref, m_ref, v_ref = (refs[k * n + i] for k in range(4))
            d_ref, mo_ref, vo_ref = (refs[(4 + k) * n + i] for k in range(3))
            d_ref[...], mo_ref[...], vo_ref[...] = _adamw_math(w_ref[...], g_ref[...], m_ref[...], v_ref[...])

    shapes = [jax.ShapeDtypeStruct(w.shape, F32) for w in ws]
    out = pl.pallas_call(body, name="adamw_small", out_shape=shapes * 3, compiler_params=_params())(*ws, *gs, *ms, *vs)
    return out[:n], out[n:2 * n], out[2 * n:]


MATRICES = ["ffn1_w13", "ffn1_w2", "w_in", "w_out", "ffn2_w13", "ffn2_w2"]
VECTORS = ["ffn1_norm", "mix_norm", "conv_b", "conv_ln_g", "conv_ln_b", "forget_b", "out_norm_conv",
           "out_norm_attn", "ffn2_norm", "final_norm"]
WEIGHTS = ["ffn1_norm", "ffn1_w13", "ffn1_w2", "mix_norm", "w_in", "conv_w", "conv_b", "conv_ln_g", "conv_ln_b",
           "forget_b", "out_norm_conv", "out_norm_attn", "w_out", "ffn2_norm", "ffn2_w13", "ffn2_w2", "final_norm"]
ADAM_ROWS = {"ffn1_w13": 256, "ffn2_w13": 256, "ffn1_w2": 352, "ffn2_w2": 352, "w_in": 256, "w_out": 256}


def _pack_small(g, names):
    rows, layout = [], []
    for n in names:
        flat = g[n].reshape(-1)
        pad = (-flat.shape[0]) % LANES
        rows.append(jnp.pad(flat, (0, pad)).reshape(-1, LANES))
        layout.append((n, g[n].shape, flat.shape[0], rows[-1].shape[0]))
    packed = jnp.concatenate(rows, axis=0)
    pad_rows = (-packed.shape[0]) % 8
    return jnp.pad(packed, ((0, pad_rows), (0, 0))), layout


def _unpack_small(packed, layout):
    out, r = {}, 0
    for n, shape, size, nrows in layout:
        out[n] = packed[r:r + nrows].reshape(-1)[:size].reshape(shape)
        r += nrows
    return out


def kernel(x, ffn1_norm, ffn1_w13, ffn1_w2, mix_norm, w_in, conv_w, conv_b, conv_ln_g, conv_ln_b, forget_b, out_norm_conv, out_norm_attn, w_out, ffn2_norm, ffn2_w13, ffn2_w2, final_norm, loss_target, m_ffn1_norm, m_ffn1_w13, m_ffn1_w2, m_mix_norm, m_w_in, m_conv_w, m_conv_b, m_conv_ln_g, m_conv_ln_b, m_forget_b, m_out_norm_conv, m_out_norm_attn, m_w_out, m_ffn2_norm, m_ffn2_w13, m_ffn2_w2, m_final_norm, v_ffn1_norm, v_ffn1_w13, v_ffn1_w2, v_mix_norm, v_w_in, v_conv_w, v_conv_b, v_conv_ln_g, v_conv_ln_b, v_forget_b, v_out_norm_conv, v_out_norm_attn, v_w_out, v_ffn2_norm, v_ffn2_w13, v_ffn2_w2, v_final_norm):
    args = dict(locals())
    weights = {n: args[n] for n in WEIGHTS}
    core = lax.axis_index("c").astype(jnp.int32).reshape(1)
    chip = (2 * lax.axis_index("x") + lax.axis_index("y")).astype(jnp.int32)
    chip1 = chip.reshape(1)
    chip_core = jnp.concatenate([chip1, core])

    slot = {n: _into_slot(weights[n][0], chip1, BF16, "slot_" + n) for n in MATRICES}
    slot["conv_w"] = _into_slot(jnp.pad(conv_w[0], ((0, CONV_PAD - CONV_WIDTH), (0, 0))), chip1, F32, "slot_conv_w")
    fetched = {"ffn1": ["ffn1_w13", "ffn1_w2"], "mix": ["w_in", "w_out", "conv_w"], "ffn2": ["ffn2_w13", "ffn2_w2"]}
    fetch = {}

    def as_weights(group, bufs):
        out = {}
        for n, b in zip(fetched[group], bufs):
            if n.endswith("w13"):
                out[n] = b
            elif n.endswith("w2"):
                out[n] = b.reshape(D_FF, D_MODEL)
            elif n == "w_out":
                out[n] = b.reshape(D_MODEL, D_MODEL)
            elif n == "w_in":
                out[n] = jnp.concatenate([b[k] for k in range(N_CHIPS)], axis=1)
            else:
                out[n] = b[:, :CONV_WIDTH].transpose(1, 0, 2).reshape(CONV_WIDTH, D_CONV)
        return out

    def get_weights(group, after):
        if group == "ffn1":
            later_names = fetched["mix"] + fetched["ffn2"]
            bufs = _gather_shards([slot[n] for n in fetched[group]], "gather_ffn1", passed=[slot[n] for n in later_names])
            behind = dict(zip(later_names, bufs[len(fetched[group]):]))
            for later in ("mix", "ffn2"):
                bufs_later = [behind[n] for n in fetched[later]]
                plan = _ici_gather_plan(bufs_later)
                fetch[later] = plan, _split_copy_start("gather_%s_start" % later, bufs_later, 3 * len(bufs_later), plan)
            return as_weights(group, bufs), [fetch["mix"][1][3], fetch["ffn2"][1][3]]
        plan, started = fetch[group.split(":")[0]]
        if group == "ffn2:landed":
            landed = _split_copy_wait("gather_ffn2_wait", started, plan, [after])
            plan = _d2d_forward_plan(landed)
            fetch["ffn2"] = plan, _split_copy_start("forward_ffn2_start", landed, 3 * len(landed), plan)
            return {}, [fetch["ffn2"][1][3]]
        if group == "ffn2":
            return as_weights(group, _split_copy_wait("forward_ffn2_wait", started, plan, [after])), []
        landed = _split_copy_wait("gather_%s_wait" % group, started, plan, [after])
        return as_weights(group, _gather_shards(landed, "forward_" + group, ici=False)), []

    def shard_major(n, g):
        if n.endswith("w13"):
            return g
        if n == "w_in":
            return jnp.stack([g[:, k * IN_SHARD:(k + 1) * IN_SHARD] for k in range(N_CHIPS)])
        return g.reshape(N_CHIPS, g.shape[0] // N_CHIPS, g.shape[1])

    exchange, scatter = {}, {}
    small_names = VECTORS + ["conv_w"]
    small = {}

    def put_grads(group, grads):
        if group == "small":
            packed, layout = _pack_small(grads, small_names + ["loss"])
            total = _all_reduce_small(packed)
            small.update(_unpack_small(total, layout))
            return [total]
        names = list(grads)
        local = [shard_major(n, grads[n]) for n in names]
        landing = [lax.empty((N_CHIPS, a.shape[1] // 2, a.shape[2]), BF16) for a in local]
        plan = _pair_exchange_plan(local)
        exchange[group] = names, plan, _split_copy_start("exchange_%s_start" % group, local + landing, len(local), plan)
        return [exchange[group][2][3]]

    def flush_grads(group, after):
        names, plan, started = exchange[group]
        done = _split_copy_wait("exchange_%s_wait" % group, started, plan, after)
        local, sib = done[:len(names)], done[len(names):]
        parts = [_pair_add(a, b, core, "pair_add_" + n) for a, b, n in zip(local, sib, names)]
        landing = [lax.empty((N_CHIPS - 1,) + q.shape[1:], BF16) for q in parts]
        plan = _ici_scatter_plan(len(parts))
        scatter[group] = names, plan, _split_copy_start("scatter_%s_start" % group, parts + landing, 3 * len(parts), plan)
        return [scatter[group][2][3]]

    p = {n: weights[n] for n in VECTORS}
    p["final_norm"] = final_norm.reshape(1, D_MODEL)
    dx = _local_step(x[0], loss_target[0], p, get_weights, put_grads, flush_grads)
    loss = small["loss"].reshape(())

    grad = {n: small[n] for n in VECTORS}
    grad["final_norm"] = small["final_norm"].reshape(D_MODEL)
    grad["conv_w"] = lax.dynamic_slice_in_dim(small["conv_w"], chip * (D_CONV // N_CHIPS), D_CONV // N_CHIPS, axis=1)[None]

    delta, new_m, new_v = {}, {}, {}

    def reduce_chips(group, after):
        names, plan, started = scatter[group]
        done = _split_copy_wait("scatter_%s_wait" % group, started, plan, after)
        parts, landed = done[:len(names)], done[len(names):]
        return [_chip_add(a, b, chip_core, "chip_add_" + n) for a, b, n in zip(parts, landed, names)]

    def update(group, full):
        ends = []
        for n, f in zip(scatter[group][0], full):
            reduced = f.reshape(f.shape[0] * f.shape[1], f.shape[2])
            go, d, mo, vo = _adamw_matrix(weights[n][0], reduced, args["m_" + n][0], args["v_" + n][0], "adamw_" + n, ADAM_ROWS[n])
            grad[n], delta[n], new_m[n], new_v[n] = go[None], d[None], mo[None], vo[None]
            ends.append(vo)
        return ends

    share_plan = _pair_share_plan()
    halves_ffn2 = reduce_chips("ffn2", [exchange["ffn1"][2][3]])
    share_ffn2 = _split_copy_start("share_ffn2_start", halves_ffn2, len(halves_ffn2), share_plan)
    last_scatter = flush_grads("ffn1", [share_ffn2[3]])
    halves_mix = reduce_chips("mix", last_scatter)
    share_mix = _split_copy_start("share_mix_start", halves_mix, len(halves_mix), share_plan)
    done_ffn2 = update("ffn2", _split_copy_wait("share_ffn2_wait", share_ffn2, share_plan, [share_mix[3]]))
    done_mix = update("mix", _split_copy_wait("share_mix_wait", share_mix, share_plan, done_ffn2))
    as2d = lambda a: a.reshape(-1, a.shape[-1])
    ds, mos, vos = _adamw_small([as2d(weights[n]) for n in small_names], [as2d(grad[n]) for n in small_names],
                                [as2d(args["m_" + n]) for n in small_names], [as2d(args["v_" + n]) for n in small_names])
    for n, d, mo, vo in zip(small_names, ds, mos, vos):
        shape = weights[n].shape
        delta[n], new_m[n], new_v[n] = d.reshape(shape), mo.reshape(shape), vo.reshape(shape)
    halves_ffn1 = reduce_chips("ffn1", done_ffn2 + done_mix + [vos[0]])
    update("ffn1", _pair_share(halves_ffn1, "pair_share_ffn1"))

    return (loss, dx[None], *[grad[n] for n in WEIGHTS], *[delta[n] for n in WEIGHTS],
            *[new_m[n] for n in WEIGHTS], *[new_v[n] for n in WEIGHTS])
```

```python
import functools

import jax
import jax.numpy as jnp
from jax import lax
from jax.experimental import pallas as pl
from jax.experimental.pallas import tpu as pltpu

F32 = jnp.float32
BF16 = jnp.bfloat16

D_MODEL = 1024
D_FF = 2816
FF_SHARD = D_FF // 2
D_CONV = 512
D_ATTN = 512
N_HEADS = 8
HEAD_DIM = 64
CONV_WIDTH = 31
CONV_PAD = 32
N_IN = 2 * D_CONV + 3 * D_ATTN + N_HEADS
IN_SHARD = N_IN // 4
EPS = 1e-6
N_CHIPS = 4
LANES = 128
HEAD_ROWS = 16

ADAM_LR = 0.001
ADAM_B1 = 0.9
ADAM_B2 = 0.999
ADAM_EPS = 1e-08
ADAM_WD = 0.01
ADAM_STEP = 10

VMEM_LIMIT = 56 * 1024 * 1024

_NT = (((1,), (1,)), ((), ()))
_TN = (((0,), (0,)), ((), ()))


def _dot(a, b):
    return jnp.dot(a, b, preferred_element_type=F32)


def _dot_nt(a, b):
    return lax.dot_general(a, b, _NT, preferred_element_type=F32)


def _dot_tn(a, b):
    return lax.dot_general(a, b, _TN, preferred_element_type=F32)


def _params(**kw):
    return pltpu.CompilerParams(vmem_limit_bytes=VMEM_LIMIT, **kw)


def _sigmoid(x):
    return 1.0 / (1.0 + jnp.exp(-x))


def _rms_stats(x):
    return lax.rsqrt(jnp.mean(x * x, axis=-1, keepdims=True) + EPS)


def _rms_bwd(x, r, g, dh):
    t = dh * g
    dx = r * t - x * (r * r * r) * jnp.mean(t * x, axis=-1, keepdims=True)
    return dx, dh * x * r


def _silu_grad(z, sg):
    return sg * (1.0 + z * (1.0 - sg))


def _row_spec(tm, n):
    return pl.BlockSpec((tm, n), lambda i: (i, 0))


def _full_spec(shape):
    nd = len(shape)
    return pl.BlockSpec(shape, lambda i: (0,) * nd)


_ANY = pl.BlockSpec(memory_space=pl.ANY)


def _skip(n, body):
    return lambda *refs: body(*refs[n:])


FFN_ROWS = 256
FFN_WEIGHT_PARTS = N_CHIPS + 2


def _with_ffn_weights(w13_hbm, w2_hbm, w13_ref, w2_ref, sems, order, tile):
    first = pl.program_id(0) == 0
    copies = {("w13", k): pltpu.make_async_copy(w13_hbm.at[k], w13_ref.at[k], sems.at[k]) for k in range(N_CHIPS)}
    for half in range(2):
        rows = pl.ds(half * FF_SHARD, FF_SHARD)
        copies["w2", half] = pltpu.make_async_copy(w2_hbm.at[rows, :], w2_ref.at[rows, :], sems.at[N_CHIPS + half])

    @pl.when(first)
    def _():
        for part in order:
            copies[part].start()

        def ready(*parts):
            for part in parts:
                copies[part].wait()

        tile(ready)

    @pl.when(jnp.logical_not(first))
    def _():
        tile(lambda *parts: None)


def _ffn_fwd(x, g, w13s, w2, name, deps=()):
    t = x.shape[0]
    tm = FFN_ROWS
    deps = tuple(deps)

    def body(x_ref, g_ref, w13_hbm, w2_hbm, xo_ref, h_ref, gu_ref, w13_ref, w2_ref, sems):
        def tile(ready):
            xv = x_ref[...]
            hb = (xv * _rms_stats(xv) * g_ref[...]).astype(BF16)
            h_ref[...] = hb
            acc = jnp.zeros((tm, D_MODEL), F32)
            for half in range(2):
                lo = half * FF_SHARD
                ready(("w13", half), ("w13", 2 + half))
                gate = _dot(hb, w13_ref[half])
                up = _dot(hb, w13_ref[2 + half])
                gu_ref[:, lo:lo + FF_SHARD] = gate.astype(BF16)
                gu_ref[:, D_FF + lo:D_FF + lo + FF_SHARD] = up.astype(BF16)
                a = (gate * _sigmoid(gate) * up).astype(BF16)
                ready(("w2", half))
                acc = acc + _dot(a, w2_ref[lo:lo + FF_SHARD, :])
            xo_ref[...] = xv + 0.5 * acc

        _with_ffn_weights(w13_hbm, w2_hbm, w13_ref, w2_ref, sems,
                          [("w13", 0), ("w13", 2), ("w2", 0), ("w13", 1), ("w13", 3), ("w2", 1)], tile)

    return pl.pallas_call(
        _skip(len(deps), body), name=name, grid=(t // tm,),
        in_specs=[_ANY] * len(deps) + [_row_spec(tm, D_MODEL), _full_spec((1, D_MODEL)), _ANY, _ANY],
        out_specs=[_row_spec(tm, D_MODEL), _row_spec(tm, D_MODEL), _row_spec(tm, 2 * D_FF)],
        out_shape=[jax.ShapeDtypeStruct((t, D_MODEL), F32), jax.ShapeDtypeStruct((t, D_MODEL), BF16),
                   jax.ShapeDtypeStruct((t, 2 * D_FF), BF16)],
        scratch_shapes=[pltpu.VMEM(w13s.shape, BF16), pltpu.VMEM(w2.shape, BF16),
                        pltpu.SemaphoreType.DMA((FFN_WEIGHT_PARTS,))],
        compiler_params=_params(dimension_semantics=("arbitrary",)),
    )(*deps, x, g, w13s, w2)


def _ffn_bwd(dy, x, gu, g, w13s, w2, name, deps=()):
    t = x.shape[0]
    tm = FFN_ROWS
    deps = tuple(deps)

    def body(dy_ref, x_ref, gu_ref, g_ref, w13_hbm, w2_hbm, dx_ref, dgu_ref, a_ref, dg_ref, dyh_ref, dxb_ref,
             w13_ref, w2_ref, sems):
        @pl.when(pl.program_id(0) == 0)
        def _():
            dg_ref[...] = jnp.zeros_like(dg_ref)

        def tile(ready):
            dyv = dy_ref[...]
            dyh = (0.5 * dyv).astype(BF16)
            dyh_ref[...] = dyh
            dh = jnp.zeros((tm, D_MODEL), F32)
            for half in range(2):
                lo = half * FF_SHARD
                ready(("w2", half))
                da = _dot_nt(dyh, w2_ref[lo:lo + FF_SHARD, :])
                gate = gu_ref[:, lo:lo + FF_SHARD].astype(F32)
                up = gu_ref[:, D_FF + lo:D_FF + lo + FF_SHARD].astype(F32)
                sg = _sigmoid(gate)
                act = gate * sg
                a_ref[:, lo:lo + FF_SHARD] = (act * up).astype(BF16)
                dgate = (da * up * _silu_grad(gate, sg)).astype(BF16)
                dup = (da * act).astype(BF16)
                dgu_ref[:, lo:lo + FF_SHARD] = dgate
                dgu_ref[:, D_FF + lo:D_FF + lo + FF_SHARD] = dup
                ready(("w13", half), ("w13", 2 + half))
                dh = dh + _dot_nt(dgate, w13_ref[half]) + _dot_nt(dup, w13_ref[2 + half])
            xv = x_ref[...]
            dxn, dg_rows = _rms_bwd(xv, _rms_stats(xv), g_ref[...], dh)
            dx = dyv + dxn
            dx_ref[...] = dx
            dxb_ref[...] = dx.astype(BF16)
            dg_ref[...] += jnp.sum(dg_rows, axis=0, keepdims=True)

        _with_ffn_weights(w13_hbm, w2_hbm, w13_ref, w2_ref, sems,
                          [("w2", 0), ("w13", 0), ("w13", 2), ("w2", 1), ("w13", 1), ("w13", 3)], tile)

    return pl.pallas_call(
        _skip(len(deps), body), name=name, grid=(t // tm,),
        in_specs=[_ANY] * len(deps) + [_row_spec(tm, D_MODEL), _row_spec(tm, D_MODEL), _row_spec(tm, 2 * D_FF),
                                       _full_spec((1, D_MODEL)), _ANY, _ANY],
        out_specs=[_row_spec(tm, D_MODEL), _row_spec(tm, 2 * D_FF), _row_spec(tm, D_FF),
                   _full_spec((1, D_MODEL)), _row_spec(tm, D_MODEL), _row_spec(tm, D_MODEL)],
        out_shape=[jax.ShapeDtypeStruct((t, D_MODEL), F32), jax.ShapeDtypeStruct((t, 2 * D_FF), BF16),
                   jax.ShapeDtypeStruct((t, D_FF), BF16), jax.ShapeDtypeStruct((1, D_MODEL), F32),
                   jax.ShapeDtypeStruct((t, D_MODEL), BF16), jax.ShapeDtypeStruct((t, D_MODEL), BF16)],
        scratch_shapes=[pltpu.VMEM(w13s.shape, BF16), pltpu.VMEM(w2.shape, BF16),
                        pltpu.SemaphoreType.DMA((FFN_WEIGHT_PARTS,))],
        compiler_params=_params(dimension_semantics=("arbitrary",)),
    )(*deps, dy, x, gu, g, w13s, w2)


WGRAD_ROWS = 512


def _wgrad(a, b, n_blocks, name, deps=()):
    t, m = a.shape
    tm = WGRAD_ROWS if m % WGRAD_ROWS == 0 else WGRAD_ROWS // 2
    n = b.shape[1]
    bn = n // n_blocks
    deps = tuple(deps)
    assert a.dtype == BF16 and b.dtype == BF16

    def body(a_ref, b_ref, o_ref):
        o_ref[0] = _dot_tn(a_ref[...], b_ref[...]).astype(BF16)

    return pl.pallas_call(
        _skip(len(deps), body), name=name, grid=(n_blocks, m // tm),
        in_specs=[_ANY] * len(deps) + [pl.BlockSpec((t, tm), lambda j, i: (0, i)),
                                       pl.BlockSpec((t, bn), lambda j, i: (0, j))],
        out_specs=pl.BlockSpec((1, tm, bn), lambda j, i: (j, i, 0)),
        out_shape=jax.ShapeDtypeStruct((n_blocks, m, bn), BF16),
        compiler_params=_params(dimension_semantics=("arbitrary", "arbitrary")),
    )(*deps, a, b)


def _mix_proj(x, g, w_ag, w_qkv, w_f):
    t = x.shape[0]
    tm = 256

    def body(x_ref, g_ref, wag_ref, wqkv_ref, wf_ref, h_ref, ag_ref, qkv_ref, fl_ref):
        xv = x_ref[...]
        hb = (xv * _rms_stats(xv) * g_ref[...]).astype(BF16)
        h_ref[...] = hb
        ag_ref[...] = _dot(hb, wag_ref[...])
        qkv_ref[...] = _dot(hb, wqkv_ref[...]).astype(BF16)
        fl_ref[...] = _dot(hb, wf_ref[...])

    return pl.pallas_call(
        body, name="mix_proj", grid=(t // tm,),
        in_specs=[_row_spec(tm, D_MODEL), _full_spec((1, D_MODEL)), _full_spec(w_ag.shape),
                  _full_spec(w_qkv.shape), _full_spec(w_f.shape)],
        out_specs=[_row_spec(tm, D_MODEL), _row_spec(tm, 2 * D_CONV), _row_spec(tm, 3 * D_ATTN),
                   _row_spec(tm, LANES)],
        out_shape=[jax.ShapeDtypeStruct((t, D_MODEL), BF16), jax.ShapeDtypeStruct((t, 2 * D_CONV), F32),
                   jax.ShapeDtypeStruct((t, 3 * D_ATTN), BF16), jax.ShapeDtypeStruct((t, LANES), F32)],
        compiler_params=_params(dimension_semantics=("arbitrary",)),
    )(x, g, w_ag, w_qkv, w_f)


def _mix_proj_bwd(dag, dqkv, dfl, dx2, x1, g, w_ag, w_qkv, w_f):
    t = x1.shape[0]
    tm = 256

    def body(dag_ref, dqkv_ref, dfl_ref, dx2_ref, x_ref, g_ref, wag_ref, wqkv_ref, wf_ref, dx_ref, dg_ref):
        @pl.when(pl.program_id(0) == 0)
        def _():
            dg_ref[...] = jnp.zeros_like(dg_ref)

        dh = (_dot_nt(dag_ref[...].astype(BF16), wag_ref[...]) + _dot_nt(dqkv_ref[...], wqkv_ref[...])
              + _dot_nt(dfl_ref[...].astype(BF16), wf_ref[...]))
        xv = x_ref[...]
        dxn, dg_rows = _rms_bwd(xv, _rms_stats(xv), g_ref[...], dh)
        dx_ref[...] = dx2_ref[...] + dxn
        dg_ref[...] += jnp.sum(dg_rows, axis=0, keepdims=True)

    return pl.pallas_call(
        body, name="mix_proj_bwd", grid=(t // tm,),
        in_specs=[_row_spec(tm, 2 * D_CONV), _row_spec(tm, 3 * D_ATTN), _row_spec(tm, LANES),
                  _row_spec(tm, D_MODEL), _row_spec(tm, D_MODEL), _full_spec((1, D_MODEL)),
                  _full_spec(w_ag.shape), _full_spec(w_qkv.shape), _full_spec(w_f.shape)],
        out_specs=[_row_spec(tm, D_MODEL), _full_spec((1, D_MODEL))],
        out_shape=[jax.ShapeDtypeStruct((t, D_MODEL), F32), jax.ShapeDtypeStruct((1, D_MODEL), F32)],
        compiler_params=_params(dimension_semantics=("arbitrary",)),
    )(dag, dqkv, dfl, dx2, x1, g, w_ag, w_qkv, w_f)


def _split3(x):
    hi = x.astype(BF16)
    r1 = x - hi.astype(F32)
    mid = r1.astype(BF16)
    lo = (r1 - mid.astype(F32)).astype(BF16)
    return hi, mid, lo


def _gates_fwd(flt, fb):
    t = flt.shape[1]

    def body(f_ref, b_ref, d_ref):
        z = f_ref[...] + b_ref[...]
        logf = jnp.minimum(z, 0.0) - jnp.log(1.0 + jnp.exp(-jnp.abs(z)))
        row = lax.broadcasted_iota(jnp.int32, (LANES, LANES), 0)
        col = lax.broadcasted_iota(jnp.int32, (LANES, LANES), 1)
        upper = (row <= col).astype(BF16)
        carry = jnp.zeros((HEAD_ROWS, 1), F32)
        for blk in range(t // LANES):
            hi, mid, lo = _split3(logf[:, blk * LANES:(blk + 1) * LANES])
            cs = _dot(hi, upper) + _dot(mid, upper) + _dot(lo, upper)
            d_ref[:, blk * LANES:(blk + 1) * LANES] = cs + carry
            carry = carry + cs[:, LANES - 1:LANES]

    return pl.pallas_call(
        body, name="gates_fwd", out_shape=jax.ShapeDtypeStruct((HEAD_ROWS, t), F32),
        compiler_params=_params(),
    )(flt, fb)


def _gates_bwd(dd, flt, fb):
    t = flt.shape[1]

    def body(dd_ref, f_ref, b_ref, df_ref, db_ref):
        z = f_ref[...] + b_ref[...]
        row = lax.broadcasted_iota(jnp.int32, (LANES, LANES), 0)
        col = lax.broadcasted_iota(jnp.int32, (LANES, LANES), 1)
        lower = (row >= col).astype(BF16)
        carry = jnp.zeros((HEAD_ROWS, 1), F32)
        db = jnp.zeros((HEAD_ROWS, 1), F32)
        for blk in reversed(range(t // LANES)):
            sl = slice(blk * LANES, (blk + 1) * LANES)
            hi, mid, lo = _split3(dd_ref[:, sl])
            cs = _dot(hi, lower) + _dot(mid, lower) + _dot(lo, lower)
            dz = (cs + carry) * _sigmoid(-z[:, sl])
            df_ref[:, sl] = dz
            db = db + jnp.sum(dz, axis=1, keepdims=True)
            carry = carry + cs[:, 0:1]
        db_ref[...] = db

    return pl.pallas_call(
        body, name="gates_bwd",
        out_shape=[jax.ShapeDtypeStruct((HEAD_ROWS, t), F32), jax.ShapeDtypeStruct((HEAD_ROWS, 1), F32)],
        compiler_params=_params(),
    )(dd, flt, fb)


CONV_CHUNK = 128
CONV_TAIL = 16
CONV_WINDOW = CONV_CHUNK + CONV_PAD + 8
CONV_ROWS_EXTRA = CONV_PAD + CONV_TAIL
SUBLANES = 8


def _conv_rows(ag_ref, u_ref, t):
    u_ref[0:CONV_PAD, :] = jnp.zeros((CONV_PAD, D_CONV), F32)
    u_ref[CONV_PAD + t:CONV_ROWS_EXTRA + t, :] = jnp.zeros((CONV_TAIL, D_CONV), F32)

    def fill(i, c):
        r0 = pl.multiple_of(i * CONV_CHUNK, CONV_CHUNK)
        a = ag_ref[pl.ds(r0, CONV_CHUNK), 0:D_CONV]
        gt = ag_ref[pl.ds(r0, CONV_CHUNK), D_CONV:2 * D_CONV]
        u_ref[pl.ds(CONV_PAD + r0, CONV_CHUNK), :] = a * _sigmoid(gt)
        return c

    lax.fori_loop(0, t // CONV_CHUNK, fill, 0)


def _for_shifted(ref, r0, offsets, fn):
    window = ref[pl.ds(r0, CONV_WINDOW), :]
    for rem in range(SUBLANES):
        mine = [o for o in offsets if o % SUBLANES == rem]
        if not mine:
            continue
        turned = window if rem == 0 else pltpu.roll(window, CONV_WINDOW - rem, 0)
        for o in mine:
            fn(o, turned[o - rem:o - rem + CONV_CHUNK])


def _conv_point(u_ref, r0, w_ref, cb, lg, lb):
    acc = [jnp.zeros((CONV_CHUNK, D_CONV), F32)]

    def tap(o, rows):
        j = o - (CONV_PAD - CONV_WIDTH + 1)
        acc[0] = acc[0] + w_ref[j:j + 1, :] * rows

    _for_shifted(u_ref, r0, [j + CONV_PAD - CONV_WIDTH + 1 for j in range(CONV_WIDTH)], tap)
    y = acc[0] + cb
    mu = jnp.mean(y, axis=-1, keepdims=True)
    yc = y - mu
    rstd = lax.rsqrt(jnp.mean(yc * yc, axis=-1, keepdims=True) + EPS)
    yhat = yc * rstd
    z = yhat * lg + lb
    sg = _sigmoid(z)
    s = z * sg
    rr = _rms_stats(s)
    return yhat, rstd, z, sg, s, rr


def _conv_fwd(ag, conv_w, conv_b, ln_g, ln_b, norm_g):
    t = ag.shape[0]

    def body(ag_ref, w_ref, cb_ref, lg_ref, lb_ref, ng_ref, o_ref, u_ref):
        _conv_rows(ag_ref, u_ref, t)
        cb, lg, lb, ng = cb_ref[...], lg_ref[...], lb_ref[...], ng_ref[...]

        def chunk(i, c):
            r0 = pl.multiple_of(i * CONV_CHUNK, CONV_CHUNK)
            _, _, _, _, s, rr = _conv_point(u_ref, r0, w_ref, cb, lg, lb)
            o_ref[pl.ds(r0, CONV_CHUNK), :] = (s * rr * ng).astype(BF16)
            return c

        lax.fori_loop(0, t // CONV_CHUNK, chunk, 0)

    return pl.pallas_call(
        body, name="conv_fwd", out_shape=jax.ShapeDtypeStruct((t, D_CONV), BF16),
        scratch_shapes=[pltpu.VMEM((t + CONV_ROWS_EXTRA, D_CONV), F32)],
        compiler_params=_params(),
    )(ag, conv_w, conv_b, ln_g, ln_b, norm_g)


def _conv_bwd(ag, dout, conv_w, conv_b, ln_g, ln_b, norm_g):
    t = ag.shape[0]

    def body(ag_ref, do_ref, w_ref, cb_ref, lg_ref, lb_ref, ng_ref,
             dag_ref, dw_ref, dcb_ref, dlg_ref, dlb_ref, dng_ref, u_ref, dy_ref):
        _conv_rows(ag_ref, u_ref, t)
        dy_ref[t:t + CONV_ROWS_EXTRA, :] = jnp.zeros((CONV_ROWS_EXTRA, D_CONV), F32)
        cb, lg, lb, ng = cb_ref[...], lg_ref[...], lb_ref[...], ng_ref[...]
        dw_ref[...] = jnp.zeros_like(dw_ref)
        zero = jnp.zeros((1, D_CONV), F32)

        def chunk(i, carry):
            dcb, dlg, dlb, dng = carry
            r0 = pl.multiple_of(i * CONV_CHUNK, CONV_CHUNK)
            yhat, rstd, z, sg, s, rr = _conv_point(u_ref, r0, w_ref, cb, lg, lb)
            do = do_ref[pl.ds(r0, CONV_CHUNK), :]
            ds, dng_rows = _rms_bwd(s, rr, ng, do)
            dz = ds * _silu_grad(z, sg)
            dyhat = dz * lg
            dy = rstd * (dyhat - jnp.mean(dyhat, axis=-1, keepdims=True)
                         - yhat * jnp.mean(dyhat * yhat, axis=-1, keepdims=True))
            dy_ref[pl.ds(r0, CONV_CHUNK), :] = dy
            def tap(o, rows):
                j = o - (CONV_PAD - CONV_WIDTH + 1)
                dw_ref[j:j + 1, :] += jnp.sum(dy * rows, axis=0, keepdims=True)

            _for_shifted(u_ref, r0, [j + CONV_PAD - CONV_WIDTH + 1 for j in range(CONV_WIDTH)], tap)
            return (dcb + jnp.sum(dy, axis=0, keepdims=True), dlg + jnp.sum(dz * yhat, axis=0, keepdims=True),
                    dlb + jnp.sum(dz, axis=0, keepdims=True), dng + jnp.sum(dng_rows, axis=0, keepdims=True))

        dcb, dlg, dlb, dng = lax.fori_loop(0, t // CONV_CHUNK, chunk, (zero, zero, zero, zero))
        dcb_ref[...] = dcb
        dlg_ref[...] = dlg
        dlb_ref[...] = dlb
        dng_ref[...] = dng

        def chunk2(i, c):
            r0 = pl.multiple_of(i * CONV_CHUNK, CONV_CHUNK)
            acc = [jnp.zeros((CONV_CHUNK, D_CONV), F32)]

            def tap(o, rows):
                j = CONV_WIDTH - 1 - o
                acc[0] = acc[0] + w_ref[j:j + 1, :] * rows

            _for_shifted(dy_ref, r0, list(range(CONV_WIDTH)), tap)
            du = acc[0]
            a = ag_ref[pl.ds(r0, CONV_CHUNK), 0:D_CONV]
            gt = ag_ref[pl.ds(r0, CONV_CHUNK), D_CONV:2 * D_CONV]
            sg = _sigmoid(gt)
            dag_ref[pl.ds(r0, CONV_CHUNK), 0:D_CONV] = du * sg
            dag_ref[pl.ds(r0, CONV_CHUNK), D_CONV:2 * D_CONV] = du * a * sg * (1.0 - sg)
            return c

        lax.fori_loop(0, t // CONV_CHUNK, chunk2, 0)

    vec = jax.ShapeDtypeStruct((1, D_CONV), F32)
    return pl.pallas_call(
        body, name="conv_bwd",
        out_shape=[jax.ShapeDtypeStruct((t, 2 * D_CONV), F32), jax.ShapeDtypeStruct((CONV_PAD, D_CONV), F32),
                   vec, vec, vec, vec],
        scratch_shapes=[pltpu.VMEM((t + CONV_ROWS_EXTRA, D_CONV), F32), pltpu.VMEM((t + CONV_ROWS_EXTRA, D_CONV), F32)],
        compiler_params=_params(),
    )(ag, dout, conv_w, conv_b, ln_g, ln_b, norm_g)


Q_ROWS = 256
ATTN_SCALE = HEAD_DIM ** -0.5


def _attn_specs(t):
    blk = lambda off: pl.BlockSpec((t, LANES), lambda p: (0, off + p))
    pairs = N_HEADS // 2
    return [blk(0), blk(pairs), blk(2 * pairs), pl.BlockSpec((2, 1, t), lambda p: (p, 0, 0))]


def _one_head(q2, mask):
    return jnp.where(mask, q2, jnp.zeros_like(q2)) * ATTN_SCALE


def _attn_scores(qs, k2, drow, r0, q1):
    s = _dot_nt(qs, k2) - drow
    rowi = lax.broadcasted_iota(jnp.int32, (q1 - r0, q1 - r0), 0)
    coli = lax.broadcasted_iota(jnp.int32, (q1 - r0, q1 - r0), 1)
    diag = jnp.where(coli <= rowi, s[:, r0:q1], -jnp.inf)
    return diag if r0 == 0 else jnp.concatenate([s[:, :r0], diag], axis=1)


def _attn_fwd(qkv, drow, deps=()):
    t = qkv.shape[0]
    deps = tuple(deps)

    def body(q_ref, k_ref, v_ref, dr_ref, o_ref, lse_ref):
        head_a = lax.broadcasted_iota(jnp.int32, (1, LANES), 1) < HEAD_DIM
        for qb in range(t // Q_ROWS):
            r0, q1 = qb * Q_ROWS, (qb + 1) * Q_ROWS
            q2 = q_ref[r0:q1, :]
            k2 = k_ref[0:q1, :]
            v2 = v_ref[0:q1, :]
            outs = []
            for hh in range(2):
                qs = _one_head(q2, head_a if hh == 0 else ~head_a)
                s = _attn_scores(qs, k2, dr_ref[hh, :, 0:q1], r0, q1)
                mx = jnp.max(s, axis=1, keepdims=True)
                p = jnp.exp(s - mx)
                l = jnp.sum(p, axis=1, keepdims=True)
                lse_ref[hh, r0:q1, :] = mx + jnp.log(l)
                outs.append(_dot((p * (1.0 / l)).astype(BF16), v2))
            o_ref[r0:q1, :] = jnp.where(head_a, outs[0], outs[1])

    pairs = N_HEADS // 2
    return pl.pallas_call(
        _skip(len(deps), body), name="attn_fwd", grid=(pairs,), in_specs=[_ANY] * len(deps) + _attn_specs(t),
        out_specs=[pl.BlockSpec((t, LANES), lambda p: (0, p)), pl.BlockSpec((2, t, 1), lambda p: (p, 0, 0))],
        out_shape=[jax.ShapeDtypeStruct((t, D_ATTN), F32), jax.ShapeDtypeStruct((N_HEADS, t, 1), F32)],
        compiler_params=_params(dimension_semantics=("arbitrary",)),
    )(*deps, qkv, qkv, qkv, drow)


def _attn_bwd(qkv, drow, lse, do):
    t = qkv.shape[0]

    def body(q_ref, k_ref, v_ref, dr_ref, lse_ref, do_ref,
             dq_ref, dk_ref, dv_ref, dd_ref, dk_acc, dv_acc):
        head_a = lax.broadcasted_iota(jnp.int32, (1, LANES), 1) < HEAD_DIM
        dk_acc[...] = jnp.zeros_like(dk_acc)
        dv_acc[...] = jnp.zeros_like(dv_acc)
        dd_ref[...] = jnp.zeros_like(dd_ref)
        for qb in range(t // Q_ROWS):
            r0, q1 = qb * Q_ROWS, (qb + 1) * Q_ROWS
            q2 = q_ref[r0:q1, :]
            k2 = k_ref[0:q1, :]
            v2 = v_ref[0:q1, :]
            do2 = do_ref[r0:q1, :]
            dqs = []
            dk_sum = jnp.zeros((q1, LANES), F32)
            dv_sum = jnp.zeros((q1, LANES), F32)
            for hh in range(2):
                mask = head_a if hh == 0 else ~head_a
                qs = _one_head(q2, mask)
                dob = jnp.where(mask, do2, 0.0).astype(BF16)
                p = jnp.exp(_attn_scores(qs, k2, dr_ref[hh, :, 0:q1], r0, q1) - lse_ref[hh, r0:q1, :])
                dp = _dot_nt(dob, v2)
                ds = p * (dp - jnp.sum(p * dp, axis=1, keepdims=True))
                dsb = ds.astype(BF16)
                dqs.append(_dot(dsb, k2) * ATTN_SCALE)
                dk_sum = dk_sum + _dot_tn(dsb, qs)
                dv_sum = dv_sum + _dot_tn(p.astype(BF16), dob)
                dd_ref[hh, :, 0:q1] -= jnp.sum(ds, axis=0, keepdims=True)
            dq_ref[r0:q1, :] = jnp.where(head_a, dqs[0], dqs[1]).astype(BF16)
            dk_acc[0:q1, :] += dk_sum
            dv_acc[0:q1, :] += dv_sum
        dk_ref[...] = dk_acc[...].astype(BF16)
        dv_ref[...] = dv_acc[...].astype(BF16)

    pairs = N_HEADS // 2
    col = pl.BlockSpec((t, LANES), lambda p: (0, p))
    grad = jax.ShapeDtypeStruct((t, D_ATTN), BF16)
    return pl.pallas_call(
        body, name="attn_bwd", grid=(pairs,),
        in_specs=_attn_specs(t) + [pl.BlockSpec((2, t, 1), lambda p: (p, 0, 0)), col],
        out_specs=[col, col, col, pl.BlockSpec((2, 1, t), lambda p: (p, 0, 0))],
        out_shape=[grad, grad, grad, jax.ShapeDtypeStruct((N_HEADS, 1, t), F32)],
        scratch_shapes=[pltpu.VMEM((t, LANES), F32), pltpu.VMEM((t, LANES), F32)],
        compiler_params=_params(dimension_semantics=("arbitrary",)),
    )(qkv, qkv, qkv, drow, lse, do)


def _out_proj(ycn, o, g_attn, w_out, x1, deps=()):
    t = x1.shape[0]
    tm = 256
    deps = tuple(deps)

    def body(yc_ref, o_ref, g_ref, w_ref, x_ref, xo_ref, ya_ref):
        ov = o_ref[...]
        ya = (ov * _rms_stats(ov) * g_ref[...]).astype(BF16)
        ya_ref[...] = ya
        xo_ref[...] = x_ref[...] + _dot(yc_ref[...], w_ref[0:D_CONV, :]) + _dot(ya, w_ref[D_CONV:, :])

    return pl.pallas_call(
        _skip(len(deps), body), name="out_proj", grid=(t // tm,),
        in_specs=[_ANY] * len(deps) + [_row_spec(tm, D_CONV), _row_spec(tm, D_ATTN), _full_spec((1, D_ATTN)),
                                       _full_spec(w_out.shape), _row_spec(tm, D_MODEL)],
        out_specs=[_row_spec(tm, D_MODEL), _row_spec(tm, D_ATTN)],
        out_shape=[jax.ShapeDtypeStruct((t, D_MODEL), F32), jax.ShapeDtypeStruct((t, D_ATTN), BF16)],
        compiler_params=_params(dimension_semantics=("arbitrary",)),
    )(*deps, ycn, o, g_attn, w_out, x1)


def _out_proj_bwd(dx2, o, g_attn, w_out, deps=()):
    t = dx2.shape[0]
    tm = 256
    deps = tuple(deps)

    def body(dx_ref, o_ref, g_ref, w_ref, dyc_ref, do_ref, dg_ref):
        @pl.when(pl.program_id(0) == 0)
        def _():
            dg_ref[...] = jnp.zeros_like(dg_ref)

        dxb = dx_ref[...]
        dyc_ref[...] = _dot_nt(dxb, w_ref[0:D_CONV, :])
        dya = _dot_nt(dxb, w_ref[D_CONV:, :])
        ov = o_ref[...]
        do, dg_rows = _rms_bwd(ov, _rms_stats(ov), g_ref[...], dya)
        do_ref[...] = do
        dg_ref[...] += jnp.sum(dg_rows, axis=0, keepdims=True)

    return pl.pallas_call(
        _skip(len(deps), body), name="out_proj_bwd", grid=(t // tm,),
        in_specs=[_ANY] * len(deps) + [_row_spec(tm, D_MODEL), _row_spec(tm, D_ATTN), _full_spec((1, D_ATTN)),
                                       _full_spec(w_out.shape)],
        out_specs=[_row_spec(tm, D_CONV), _row_spec(tm, D_ATTN), _full_spec((1, D_ATTN))],
        out_shape=[jax.ShapeDtypeStruct((t, D_CONV), F32), jax.ShapeDtypeStruct((t, D_ATTN), F32),
                   jax.ShapeDtypeStruct((1, D_ATTN), F32)],
        compiler_params=_params(dimension_semantics=("arbitrary",)),
    )(*deps, dx2, o, g_attn, w_out)


def _loss_bwd(x3, target, g):
    t = x3.shape[0]
    tm = 256

    def body(x_ref, t_ref, g_ref, loss_ref, dx_ref, dg_ref):
        @pl.when(pl.program_id(0) == 0)
        def _():
            loss_ref[...] = jnp.zeros_like(loss_ref)
            dg_ref[...] = jnp.zeros_like(dg_ref)

        xv = x_ref[...]
        r = _rms_stats(xv)
        gv = g_ref[...]
        err = xv * r * gv - t_ref[...]
        row = jnp.sum(err * err, axis=1, keepdims=True) * (0.5 / D_MODEL)
        loss_ref[...] += jnp.sum(row, axis=0, keepdims=True)
        dx, dg_rows = _rms_bwd(xv, r, gv, err * (1.0 / D_MODEL))
        dx_ref[...] = dx
        dg_ref[...] += jnp.sum(dg_rows, axis=0, keepdims=True)

    return pl.pallas_call(
        body, name="loss_bwd", grid=(t // tm,),
        in_specs=[_row_spec(tm, D_MODEL), _row_spec(tm, D_MODEL), _full_spec((1, D_MODEL))],
        out_specs=[_full_spec((1, LANES)), _row_spec(tm, D_MODEL), _full_spec((1, D_MODEL))],
        out_shape=[jax.ShapeDtypeStruct((1, LANES), F32), jax.ShapeDtypeStruct((t, D_MODEL), F32),
                   jax.ShapeDtypeStruct((1, D_MODEL), F32)],
        compiler_params=_params(dimension_semantics=("arbitrary",)),
    )(x3, target, g)


def _split_w_in(w_in):
    w_ag = w_in[:, :2 * D_CONV]
    w_qkv = w_in[:, 2 * D_CONV:2 * D_CONV + 3 * D_ATTN]
    w_f = jnp.pad(w_in[:, 2 * D_CONV + 3 * D_ATTN:], ((0, 0), (0, LANES - N_HEADS)))
    return w_ag, w_qkv, w_f


def _head_rows(v):
    return jnp.pad(v, ((0, HEAD_ROWS - N_HEADS),) + ((0, 0),) * (v.ndim - 1))


def _local_step(x, target, p, get_weights, put_grads, flush_grads):
    t = x.shape[0]
    fb = _head_rows(p["forget_b"].reshape(N_HEADS, 1))

    w, deps = get_weights("ffn1", None)
    x1, h1, gu1 = _ffn_fwd(x, p["ffn1_norm"], w["ffn1_w13"], w["ffn1_w2"], "ffn1_fwd", deps)
    wm, _ = get_weights("mix", x1)
    w.update(wm)
    w_ag, w_qkv, w_f = _split_w_in(w["w_in"])
    conv_w = jnp.pad(w["conv_w"], ((0, CONV_PAD - CONV_WIDTH), (0, 0)))
    h2, ag, qkv, fl = _mix_proj(x1, p["mix_norm"], w_ag, w_qkv, w_f)
    flt = _head_rows(fl[:, :N_HEADS].T)
    dcum = _gates_fwd(flt, fb)[:N_HEADS]
    drow = dcum.reshape(N_HEADS, 1, t)
    ycn = _conv_fwd(ag, conv_w, p["conv_b"], p["conv_ln_g"], p["conv_ln_b"], p["out_norm_conv"])
    o, lse = _attn_fwd(qkv, drow, [ycn])
    _, deps = get_weights("ffn2:landed", o)
    x2, yan = _out_proj(ycn, o, p["out_norm_attn"], w["w_out"], x1, deps)
    w2, _ = get_weights("ffn2", x2)
    w.update(w2)
    x3, h3, gu2 = _ffn_fwd(x2, p["ffn2_norm"], w["ffn2_w13"], w["ffn2_w2"], "ffn2_fwd")
    loss, dx3, d_final = _loss_bwd(x3, target, p["final_norm"])

    g = {}
    dx2, dgu2, a2, g["ffn2_norm"], dx3_half, dx2_bf16 = _ffn_bwd(
        dx3, x2, gu2, p["ffn2_norm"], w["ffn2_w13"], w["ffn2_w2"], "ffn2_bwd")
    dw13 = _wgrad(h3, dgu2, N_CHIPS, "ffn2_dw13")
    dw2 = _wgrad(a2, dx3_half, 1, "ffn2_dw2").reshape(D_FF, D_MODEL)
    deps = put_grads("ffn2", {"ffn2_w13": dw13, "ffn2_w2": dw2})
    dyc, do, g["out_norm_attn"] = _out_proj_bwd(dx2_bf16, o, p["out_norm_attn"], w["w_out"], deps)
    deps = flush_grads("ffn2", [dyc])
    dw_out = _wgrad(jnp.concatenate([ycn, yan], axis=1), dx2_bf16, 1, "dw_out", deps).reshape(D_MODEL, D_MODEL)
    dq, dk, dv, ddrow = _attn_bwd(qkv, drow, lse, do)
    dflt, dfb = _gates_bwd(_head_rows(ddrow.reshape(N_HEADS, t)), flt, fb)
    g["forget_b"] = dfb[:N_HEADS, 0].reshape(1, N_HEADS)
    dfl = jnp.pad(dflt[:N_HEADS].T, ((0, 0), (0, LANES - N_HEADS)))
    dag, dconv_w, g["conv_b"], g["conv_ln_g"], g["conv_ln_b"], g["out_norm_conv"] = _conv_bwd(
        ag, dyc, conv_w, p["conv_b"], p["conv_ln_g"], p["conv_ln_b"], p["out_norm_conv"])
    g["conv_w"] = dconv_w[:CONV_WIDTH]
    dqkv = jnp.concatenate([dq, dk, dv], axis=1)
    dx1, g["mix_norm"] = _mix_proj_bwd(dag, dqkv, dfl, dx2, x1, p["mix_norm"], w_ag, w_qkv, w_f)
    dproj = jnp.concatenate([dag.astype(BF16), dqkv, dfl.astype(BF16)], axis=1)
    dw_in = _wgrad(h2, dproj, 1, "dw_in").reshape(D_MODEL, dproj.shape[1])[:, :N_IN]
    deps = put_grads("mix", {"w_in": dw_in, "w_out": dw_out})
    dx0, dgu1, a1, g["ffn1_norm"], dx1_half, _ = _ffn_bwd(
        dx1, x, gu1, p["ffn1_norm"], w["ffn1_w13"], w["ffn1_w2"], "ffn1_bwd", deps)
    g["final_norm"] = d_final
    g["loss"] = loss[:, :1]
    deps = flush_grads("mix", put_grads("small", g))
    dw2 = _wgrad(a1, dx1_half, 1, "ffn1_dw2", deps).reshape(D_FF, D_MODEL)
    deps = flush_grads("ffn1_w2", put_grads("ffn1_w2", {"ffn1_w2": dw2}))
    dw13 = _wgrad(h1, dgu1, N_CHIPS, "ffn1_dw13", deps)
    put_grads("ffn1_w13", {"ffn1_w13": dw13})
    return dx0


MESH = pl.DeviceIdType.MESH


def _place():
    x, y, c = lax.axis_index("x"), lax.axis_index("y"), lax.axis_index("c")
    chips = [(1 - x, y), (x, 1 - y), (1 - x, 1 - y)]
    return x, y, c, chips


def _hbm_out(shape, dtype):
    return jax.ShapeDtypeStruct(shape, dtype)


def _comm_call(body, name, ins, out_shapes, n_remote, in_place=False):
    return pl.pallas_call(
        body, name=name, in_specs=[_ANY] * len(ins), out_specs=[_ANY] * len(out_shapes), out_shape=out_shapes,
        scratch_shapes=[pltpu.SemaphoreType.DMA((n_remote,)), pltpu.SemaphoreType.DMA((n_remote,))],
        input_output_aliases={i: i for i in range(len(ins))} if in_place else {},
    )(*ins)


def _remote(src, dst, sems, n, to):
    send_sems, recv_sems = sems
    return pltpu.make_async_remote_copy(src_ref=src, dst_ref=dst, send_sem=send_sems.at[n], recv_sem=recv_sems.at[n],
                                        device_id=to, device_id_type=MESH)


def _into_slot(shard, chip, dtype, name):
    rows, cols = shard.shape
    tr = rows // 2

    def body(k_ref, s_ref, o_ref):
        o_ref[0] = s_ref[...].astype(dtype)

    return pl.pallas_call(
        body, name=name,
        grid_spec=pltpu.PrefetchScalarGridSpec(
            num_scalar_prefetch=1, grid=(rows // tr,),
            in_specs=[pl.BlockSpec((tr, cols), lambda i, k_ref: (i, 0))],
            out_specs=pl.BlockSpec((1, tr, cols), lambda i, k_ref: (k_ref[0], i, 0))),
        out_shape=jax.ShapeDtypeStruct((N_CHIPS, rows, cols), dtype),
        compiler_params=_params(dimension_semantics=("arbitrary",)),
    )(chip, shard)


def _gather_shards(slots, name, ici=True, passed=()):
    n = len(slots)
    slots = list(slots) + list(passed)
    total = len(slots)

    def body(*refs):
        outs = refs[total:total + n]
        sems = refs[2 * total:2 * total + 2]
        x, y, c, chips = _place()
        me = 2 * x + y
        sibling = (x, y, 1 - c)

        def half(i, chip_index, core):
            hr = slots[i].shape[1] // 2
            return outs[i].at[chip_index, pl.ds(core * hr, hr), :]

        sends = []
        if ici:
            for i in range(n):
                for j, chip in enumerate(chips):
                    cp = _remote(half(i, me, c), half(i, me, c), sems, 6 * i + j, (*chip, c))
                    cp.start()
                    sends.append(cp)
        for i in range(n):
            for j, chip in enumerate(chips):
                src_chip = 2 * chip[0] + chip[1]
                landed = half(i, src_chip, c)
                if ici:
                    _remote(landed, landed, sems, 6 * i + j, (*chip, c)).wait_recv()
                cp = _remote(landed, landed, sems, 6 * i + 3 + j, sibling)
                cp.start()
                sends.append(cp)
        for i in range(n):
            for j, chip in enumerate(chips):
                src_chip = 2 * chip[0] + chip[1]
                landed = half(i, src_chip, 1 - c)
                _remote(landed, landed, sems, 6 * i + 3 + j, sibling).wait_recv()
        for cp in sends:
            cp.wait_send()

    outs = [_hbm_out(s.shape, s.dtype) for s in slots]
    return _comm_call(body, name, slots, outs, 6 * n, in_place=True)


_HBM = pl.BlockSpec(memory_space=pltpu.HBM)
_SEM = pl.BlockSpec(memory_space=pltpu.SEMAPHORE)
_DATAFLOW = pltpu.SideEffectType.DATAFLOW_SIDE_EFFECTING


def _split_copy_start(name, bufs, n_copies, plan):
    n = len(bufs)

    def body(*refs):
        for send, _ in plan(refs[:n], (refs[n], refs[n + 1])):
            send.start()
        token = refs[-1]
        token[...] = jnp.zeros_like(token)

    out = pl.pallas_call(
        body, name=name,
        out_shape=(pltpu.SemaphoreType.DMA((n_copies,)), pltpu.SemaphoreType.DMA((n_copies,)),
                   *[pltpu.HBM(b.shape, b.dtype) for b in bufs], jax.ShapeDtypeStruct((8, LANES), F32)),
        in_specs=[_HBM] * n, out_specs=(_SEM, _SEM, *[_HBM] * n, pl.BlockSpec(memory_space=pltpu.VMEM)),
        input_output_aliases={i: 2 + i for i in range(n)},
        compiler_params=pltpu.CompilerParams(has_side_effects=_DATAFLOW),
    )(*[pltpu.with_memory_space_constraint(b, pltpu.HBM) for b in bufs])
    return out[0], out[1], list(out[2:2 + n]), out[-1]


def _split_copy_wait(name, started, plan, after):
    send_sems, recv_sems, bufs, _ = started
    n = len(bufs)
    after = tuple(after)

    def body(*refs):
        for send, recv in plan(refs[:n], (refs[n], refs[n + 1])):
            send.wait_send()
            recv.wait_recv()

    out = pl.pallas_call(
        body, name=name, out_shape=tuple(pltpu.HBM(b.shape, b.dtype) for b in bufs),
        in_specs=[_HBM] * n + [_SEM, _SEM] + [_ANY] * len(after), out_specs=tuple([_HBM] * n),
        input_output_aliases={i: i for i in range(n)},
        compiler_params=pltpu.CompilerParams(has_side_effects=_DATAFLOW),
    )(*bufs, send_sems, recv_sems, *after)
    return list(out)


def _ici_gather_plan(slots):
    def plan(refs, sems):
        x, y, c, chips = _place()
        me = 2 * x + y
        copies = []
        for i, ref in enumerate(refs):
            hr = slots[i].shape[1] // 2
            for j, chip in enumerate(chips):
                mine = ref.at[me, pl.ds(c * hr, hr), :]
                theirs = ref.at[2 * chip[0] + chip[1], pl.ds(c * hr, hr), :]
                to = (*chip, c)
                copies.append((_remote(mine, mine, sems, 3 * i + j, to), _remote(theirs, theirs, sems, 3 * i + j, to)))
        return copies

    return plan


def _ici_scatter_plan(n):
    def plan(refs, sems):
        x, y, c, chips = _place()
        copies = []
        for i in range(n):
            for j, chip in enumerate(chips):
                cp = _remote(refs[i].at[2 * chip[0] + chip[1]], refs[n + i].at[j], sems, 3 * i + j, (*chip, c))
                copies.append((cp, cp))
        return copies

    return plan


def _d2d_forward_plan(slots):
    def plan(refs, sems):
        x, y, c, chips = _place()
        sibling = (x, y, 1 - c)
        copies = []
        for i, ref in enumerate(refs):
            hr = slots[i].shape[1] // 2
            for j, chip in enumerate(chips):
                src_chip = 2 * chip[0] + chip[1]
                mine = ref.at[src_chip, pl.ds(c * hr, hr), :]
                theirs = ref.at[src_chip, pl.ds((1 - c) * hr, hr), :]
                copies.append((_remote(mine, mine, sems, 3 * i + j, sibling),
                               _remote(theirs, theirs, sems, 3 * i + j, sibling)))
        return copies

    return plan


def _pair_exchange_plan(grads):
    n = len(grads)

    def plan(refs, sems):
        x, y, c, _ = _place()
        copies = []
        for i in range(n):
            hr = grads[i].shape[1] // 2
            cp = _remote(refs[i].at[:, pl.ds((1 - c) * hr, hr), :], refs[n + i], sems, i, (x, y, 1 - c))
            copies.append((cp, cp))
        return copies

    return plan


def _pair_share_plan():
    def plan(refs, sems):
        x, y, c, _ = _place()
        sibling = (x, y, 1 - c)
        return [(_remote(ref.at[c], ref.at[c], sems, i, sibling), _remote(ref.at[1 - c], ref.at[1 - c], sems, i, sibling))
                for i, ref in enumerate(refs)]

    return plan


def _pair_share(halves, name):
    n = len(halves)

    def body(*refs):
        outs = refs[n:2 * n]
        sems = refs[2 * n:2 * n + 2]
        x, y, c, _ = _place()
        sibling = (x, y, 1 - c)
        sends = [_remote(outs[i].at[c], outs[i].at[c], sems, i, sibling) for i in range(n)]
        for cp in sends:
            cp.start()
        for cp in sends:
            cp.wait_send()
        for i in range(n):
            _remote(outs[i].at[1 - c], outs[i].at[1 - c], sems, i, sibling).wait_recv()

    outs = [_hbm_out(h.shape, h.dtype) for h in halves]
    return _comm_call(body, name, halves, outs, n, in_place=True)


def _all_reduce_small(v, deps=()):
    rows = v.shape[0]
    flips = [(fx, fy, fc) for fx in range(2) for fy in range(2) for fc in range(2)][1:]

    def body(v_ref, o_ref, slots, send_sems, recv_sems):
        x, y, c, _ = _place()
        me = 4 * x + 2 * y + c
        slots[me] = v_ref[...]
        sends = []
        for n, (fx, fy, fc) in enumerate(flips):
            to = (x ^ fx, y ^ fy, c ^ fc)
            cp = _remote(v_ref, slots.at[me], (send_sems, recv_sems), n, to)
            cp.start()
            sends.append(cp)
        for n, (fx, fy, fc) in enumerate(flips):
            src = 4 * (x ^ fx) + 2 * (y ^ fy) + (c ^ fc)
            _remote(v_ref, slots.at[src], (send_sems, recv_sems), n, (x ^ fx, y ^ fy, c ^ fc)).wait_recv()
        for cp in sends:
            cp.wait_send()
        acc = slots[0]
        for s in range(1, 8):
            acc = acc + slots[s]
        o_ref[...] = acc

    deps = tuple(deps)
    return pl.pallas_call(
        _skip(len(deps), body), name="all_reduce_small", out_shape=jax.ShapeDtypeStruct(v.shape, F32),
        in_specs=[_ANY] * len(deps) + [pl.BlockSpec(memory_space=pltpu.VMEM)],
        out_specs=pl.BlockSpec(memory_space=pltpu.VMEM),
        scratch_shapes=[pltpu.VMEM((8, rows, LANES), F32), pltpu.SemaphoreType.DMA((7,)), pltpu.SemaphoreType.DMA((7,))],
    )(*deps, v)


def _pair_add(g, sib, core, name):
    _, r, cols = g.shape
    hr = r // 2
    g4 = g.reshape(N_CHIPS, 2, hr, cols)

    def body(c_ref, g_ref, s_ref, o_ref):
        o_ref[0] = (g_ref[0, 0].astype(F32) + s_ref[0].astype(F32)).astype(BF16)

    return pl.pallas_call(
        body, name=name,
        grid_spec=pltpu.PrefetchScalarGridSpec(
            num_scalar_prefetch=1, grid=(N_CHIPS,),
            in_specs=[pl.BlockSpec((1, 1, hr, cols), lambda s, c_ref: (s, c_ref[0], 0, 0)),
                      pl.BlockSpec((1, hr, cols), lambda s, c_ref: (s, 0, 0))],
            out_specs=pl.BlockSpec((1, hr, cols), lambda s, c_ref: (s, 0, 0))),
        out_shape=jax.ShapeDtypeStruct((N_CHIPS, hr, cols), BF16),
        compiler_params=_params(dimension_semantics=("arbitrary",)),
    )(core, g4, sib)


def _chip_add(part, recv, chip_core, name, deps=()):
    _, hr, cols = part.shape
    deps = tuple(deps)

    def body(kc_ref, *refs):
        p_ref, r_ref, o_ref = refs[len(deps):]
        acc = p_ref[0].astype(F32)
        for j in range(N_CHIPS - 1):
            acc = acc + r_ref[j].astype(F32)
        o_ref[0] = acc

    return pl.pallas_call(
        body, name=name,
        grid_spec=pltpu.PrefetchScalarGridSpec(
            num_scalar_prefetch=1, grid=(1,),
            in_specs=[_ANY] * len(deps) + [pl.BlockSpec((1, hr, cols), lambda s, kc_ref: (kc_ref[0], 0, 0)),
                                           pl.BlockSpec((N_CHIPS - 1, hr, cols), lambda s, kc_ref: (0, 0, 0))],
            out_specs=pl.BlockSpec((1, hr, cols), lambda s, kc_ref: (kc_ref[1], 0, 0))),
        out_shape=jax.ShapeDtypeStruct((2, hr, cols), F32),
        compiler_params=_params(dimension_semantics=("arbitrary",)),
    )(chip_core, *deps, part, recv)


def _adamw_math(w, g, m, v):
    m = ADAM_B1 * m + (1.0 - ADAM_B1) * g
    v = ADAM_B2 * v + (1.0 - ADAM_B2) * (g * g)
    m_hat = m / (1.0 - ADAM_B1 ** ADAM_STEP)
    v_hat = v / (1.0 - ADAM_B2 ** ADAM_STEP)
    delta = -ADAM_LR * (m_hat / (jnp.sqrt(v_hat) + ADAM_EPS) + ADAM_WD * w)
    return delta, m, v


def _adamw_matrix(w, g, m, v, name, tr):
    rows, cols = w.shape

    def body(w_ref, g_ref, m_ref, v_ref, go_ref, d_ref, mo_ref, vo_ref):
        gv = g_ref[...]
        go_ref[...] = gv
        d_ref[...], mo_ref[...], vo_ref[...] = _adamw_math(w_ref[...], gv, m_ref[...], v_ref[...])

    spec = _row_spec(tr, cols)
    shape = jax.ShapeDtypeStruct((rows, cols), F32)
    return pl.pallas_call(
        body, name=name, grid=(rows // tr,), in_specs=[spec] * 4, out_specs=[spec] * 4, out_shape=[shape] * 4,
        compiler_params=_params(dimension_semantics=("arbitrary",)),
    )(w, g, m, v)


def _adamw_small(ws, gs, ms, vs):
    n = len(ws)

    def body(*refs):
        for i in range(n):
            w_ref, g_ref, m_ref, v_ref = (refs[k * n + i] for k in range(4))
            d_ref, mo_ref, vo_ref = (refs[(4 + k) * n + i] for k in range(3))
            d_ref[...], mo_ref[...], vo_ref[...] = _adamw_math(w_ref[...], g_ref[...], m_ref[...], v_ref[...])

    shapes = [jax.ShapeDtypeStruct(w.shape, F32) for w in ws]
    out = pl.pallas_call(body, name="adamw_small", out_shape=shapes * 3, compiler_params=_params())(*ws, *gs, *ms, *vs)
    return out[:n], out[n:2 * n], out[2 * n:]


MATRICES = ["ffn1_w13", "ffn1_w2", "w_in", "w_out", "ffn2_w13", "ffn2_w2"]
VECTORS = ["ffn1_norm", "mix_norm", "conv_b", "conv_ln_g", "conv_ln_b", "forget_b", "out_norm_conv",
           "out_norm_attn", "ffn2_norm", "final_norm"]
WEIGHTS = ["ffn1_norm", "ffn1_w13", "ffn1_w2", "mix_norm", "w_in", "conv_w", "conv_b", "conv_ln_g", "conv_ln_b",
           "forget_b", "out_norm_conv", "out_norm_attn", "w_out", "ffn2_norm", "ffn2_w13", "ffn2_w2", "final_norm"]
ADAM_ROWS = {"ffn1_w13": 256, "ffn2_w13": 256, "ffn1_w2": 352, "ffn2_w2": 352, "w_in": 256, "w_out": 256}


def _pack_small(g, names):
    rows, layout = [], []
    for n in names:
        flat = g[n].reshape(-1)
        pad = (-flat.shape[0]) % LANES
        rows.append(jnp.pad(flat, (0, pad)).reshape(-1, LANES))
        layout.append((n, g[n].shape, flat.shape[0], rows[-1].shape[0]))
    packed = jnp.concatenate(rows, axis=0)
    pad_rows = (-packed.shape[0]) % 8
    return jnp.pad(packed, ((0, pad_rows), (0, 0))), layout


def _unpack_small(packed, layout):
    out, r = {}, 0
    for n, shape, size, nrows in layout:
        out[n] = packed[r:r + nrows].reshape(-1)[:size].reshape(shape)
        r += nrows
    return out


def kernel(x, ffn1_norm, ffn1_w13, ffn1_w2, mix_norm, w_in, conv_w, conv_b, conv_ln_g, conv_ln_b, forget_b, out_norm_conv, out_norm_attn, w_out, ffn2_norm, ffn2_w13, ffn2_w2, final_norm, loss_target, m_ffn1_norm, m_ffn1_w13, m_ffn1_w2, m_mix_norm, m_w_in, m_conv_w, m_conv_b, m_conv_ln_g, m_conv_ln_b, m_forget_b, m_out_norm_conv, m_out_norm_attn, m_w_out, m_ffn2_norm, m_ffn2_w13, m_ffn2_w2, m_final_norm, v_ffn1_norm, v_ffn1_w13, v_ffn1_w2, v_mix_norm, v_w_in, v_conv_w, v_conv_b, v_conv_ln_g, v_conv_ln_b, v_forget_b, v_out_norm_conv, v_out_norm_attn, v_w_out, v_ffn2_norm, v_ffn2_w13, v_ffn2_w2, v_final_norm):
    args = dict(locals())
    weights = {n: args[n] for n in WEIGHTS}
    core = lax.axis_index("c").astype(jnp.int32).reshape(1)
    chip = (2 * lax.axis_index("x") + lax.axis_index("y")).astype(jnp.int32)
    chip1 = chip.reshape(1)
    chip_core = jnp.concatenate([chip1, core])

    slot = {n: _into_slot(weights[n][0], chip1, BF16, "slot_" + n) for n in MATRICES}
    slot["conv_w"] = _into_slot(jnp.pad(conv_w[0], ((0, CONV_PAD - CONV_WIDTH), (0, 0))), chip1, F32, "slot_conv_w")
    fetched = {"ffn1": ["ffn1_w13", "ffn1_w2"], "mix": ["w_in", "w_out", "conv_w"], "ffn2": ["ffn2_w13", "ffn2_w2"]}
    fetch = {}

    def as_weights(group, bufs):
        out = {}
        for n, b in zip(fetched[group], bufs):
            if n.endswith("w13"):
                out[n] = b
            elif n.endswith("w2"):
                out[n] = b.reshape(D_FF, D_MODEL)
            elif n == "w_out":
                out[n] = b.reshape(D_MODEL, D_MODEL)
            elif n == "w_in":
                out[n] = jnp.concatenate([b[k] for k in range(N_CHIPS)], axis=1)
            else:
                out[n] = b[:, :CONV_WIDTH].transpose(1, 0, 2).reshape(CONV_WIDTH, D_CONV)
        return out

    def get_weights(group, after):
        if group == "ffn1":
            later_names = fetched["mix"] + fetched["ffn2"]
            bufs = _gather_shards([slot[n] for n in fetched[group]], "gather_ffn1", passed=[slot[n] for n in later_names])
            behind = dict(zip(later_names, bufs[len(fetched[group]):]))
            for later in ("mix", "ffn2"):
                bufs_later = [behind[n] for n in fetched[later]]
                plan = _ici_gather_plan(bufs_later)
                fetch[later] = plan, _split_copy_start("gather_%s_start" % later, bufs_later, 3 * len(bufs_later), plan)
            return as_weights(group, bufs), [fetch["mix"][1][3], fetch["ffn2"][1][3]]
        plan, started = fetch[group.split(":")[0]]
        if group == "ffn2:landed":
            landed = _split_copy_wait("gather_ffn2_wait", started, plan, [after])
            plan = _d2d_forward_plan(landed)
            fetch["ffn2"] = plan, _split_copy_start("forward_ffn2_start", landed, 3 * len(landed), plan)
            return {}, [fetch["ffn2"][1][3]]
        if group == "ffn2":
            return as_weights(group, _split_copy_wait("forward_ffn2_wait", started, plan, [after])), []
        landed = _split_copy_wait("gather_%s_wait" % group, started, plan, [after])
        return as_weights(group, _gather_shards(landed, "forward_" + group, ici=False)), []

    def shard_major(n, g):
        if n.endswith("w13"):
            return g
        if n == "w_in":
            return jnp.stack([g[:, k * IN_SHARD:(k + 1) * IN_SHARD] for k in range(N_CHIPS)])
        return g.reshape(N_CHIPS, g.shape[0] // N_CHIPS, g.shape[1])

    exchange, scatter = {}, {}
    small_names = VECTORS + ["conv_w"]
    small = {}

    def put_grads(group, grads):
        if group == "small":
            packed, layout = _pack_small(grads, small_names + ["loss"])
            total = _all_reduce_small(packed)
            small.update(_unpack_small(total, layout))
            return [total]
        names = list(grads)
        local = [shard_major(n, grads[n]) for n in names]
        landing = [lax.empty((N_CHIPS, a.shape[1] // 2, a.shape[2]), BF16) for a in local]
        plan = _pair_exchange_plan(local)
        exchange[group] = names, plan, _split_copy_start("exchange_%s_start" % group, local + landing, len(local), plan)
        return [exchange[group][2][3]]

    def flush_grads(group, after):
        names, plan, started = exchange[group]
        done = _split_copy_wait("exchange_%s_wait" % group, started, plan, after)
        local, sib = done[:len(names)], done[len(names):]
        parts = [_pair_add(a, b, core, "pair_add_" + n) for a, b, n in zip(local, sib, names)]
        landing = [lax.empty((N_CHIPS - 1,) + q.shape[1:], BF16) for q in parts]
        plan = _ici_scatter_plan(len(parts))
        scatter[group] = names, plan, _split_copy_start("scatter_%s_start" % group, parts + landing, 3 * len(parts), plan)
        return [scatter[group][2][3]]

    p = {n: weights[n] for n in VECTORS}
    p["final_norm"] = final_norm.reshape(1, D_MODEL)
    dx = _local_step(x[0], loss_target[0], p, get_weights, put_grads, flush_grads)
    loss = small["loss"].reshape(())

    grad = {n: small[n] for n in VECTORS}
    grad["final_norm"] = small["final_norm"].reshape(D_MODEL)
    grad["conv_w"] = lax.dynamic_slice_in_dim(small["conv_w"], chip * (D_CONV // N_CHIPS), D_CONV // N_CHIPS, axis=1)[None]

    delta, new_m, new_v = {}, {}, {}

    def reduce_chips(group, after):
        names, plan, started = scatter[group]
        done = _split_copy_wait("scatter_%s_wait" % group, started, plan, after)
        parts, landed = done[:len(names)], done[len(names):]
        return [_chip_add(a, b, chip_core, "chip_add_" + n) for a, b, n in zip(parts, landed, names)]

    def update(group, full):
        ends = []
        for n, f in zip(scatter[group][0], full):
            reduced = f.reshape(f.shape[0] * f.shape[1], f.shape[2])
            go, d, mo, vo = _adamw_matrix(weights[n][0], reduced, args["m_" + n][0], args["v_" + n][0], "adamw_" + n, ADAM_ROWS[n])
            grad[n], delta[n], new_m[n], new_v[n] = go[None], d[None], mo[None], vo[None]
            ends.append(vo)
        return ends

    share_plan = _pair_share_plan()
    halves_ffn2 = reduce_chips("ffn2", [exchange["ffn1_w13"][2][3]])
    share_ffn2 = _split_copy_start("share_ffn2_start", halves_ffn2, len(halves_ffn2), share_plan)
    last_scatter = flush_grads("ffn1_w13", [share_ffn2[3]])
    halves_mix = reduce_chips("mix", last_scatter)
    share_mix = _split_copy_start("share_mix_start", halves_mix, len(halves_mix), share_plan)
    done_ffn2 = update("ffn2", _split_copy_wait("share_ffn2_wait", share_ffn2, share_plan, [share_mix[3]]))
    done_mix = update("mix", _split_copy_wait("share_mix_wait", share_mix, share_plan, done_ffn2))
    as2d = lambda a: a.reshape(-1, a.shape[-1])
    ds, mos, vos = _adamw_small([as2d(weights[n]) for n in small_names], [as2d(grad[n]) for n in small_names],
                                [as2d(args["m_" + n]) for n in small_names], [as2d(args["v_" + n]) for n in small_names])
    for n, d, mo, vo in zip(small_names, ds, mos, vos):
        shape = weights[n].shape
        delta[n], new_m[n], new_v[n] = d.reshape(shape), mo.reshape(shape), vo.reshape(shape)
    behind = done_ffn2 + done_mix + [vos[0]]
    halves_w2 = reduce_chips("ffn1_w2", behind)
    halves_w13 = reduce_chips("ffn1_w13", behind)
    full_w2, full_w13 = _pair_share(halves_w2 + halves_w13, "pair_share_ffn1")
    update("ffn1_w2", [full_w2])
    update("ffn1_w13", [full_w13])

    return (loss, dx[None], *[grad[n] for n in WEIGHTS], *[delta[n] for n in WEIGHTS],
            *[new_m[n] for n in WEIGHTS], *[new_v[n] for n in WEIGHTS])
```

```python
import functools

import jax
import jax.numpy as jnp
from jax import lax
from jax.experimental import pallas as pl
from jax.experimental.pallas import tpu as pltpu

F32 = jnp.float32
BF16 = jnp.bfloat16

D_MODEL = 1024
D_FF = 2816
FF_SHARD = D_FF // 2
D_CONV = 512
D_ATTN = 512
N_HEADS = 8
HEAD_DIM = 64
CONV_WIDTH = 31
CONV_PAD = 32
N_IN = 2 * D_CONV + 3 * D_ATTN + N_HEADS
EPS = 1e-6
N_CHIPS = 4
LANES = 128
HEAD_ROWS = 16

ADAM_LR = 0.001
ADAM_B1 = 0.9
ADAM_B2 = 0.999
ADAM_EPS = 1e-08
ADAM_WD = 0.01
ADAM_STEP = 10

VMEM_LIMIT = 56 * 1024 * 1024

_NT = (((1,), (1,)), ((), ()))
_TN = (((0,), (0,)), ((), ()))


def _dot(a, b):
    return jnp.dot(a, b, preferred_element_type=F32)


def _dot_nt(a, b):
    return lax.dot_general(a, b, _NT, preferred_element_type=F32)


def _dot_tn(a, b):
    return lax.dot_general(a, b, _TN, preferred_element_type=F32)


def _params(**kw):
    return pltpu.CompilerParams(vmem_limit_bytes=VMEM_LIMIT, **kw)


def _sigmoid(x):
    return 1.0 / (1.0 + jnp.exp(-x))


def _rms_stats(x):
    return lax.rsqrt(jnp.mean(x * x, axis=-1, keepdims=True) + EPS)


def _rms_bwd(x, r, g, dh):
    t = dh * g
    dx = r * t - x * (r * r * r) * jnp.mean(t * x, axis=-1, keepdims=True)
    return dx, dh * x * r


def _silu_grad(z, sg):
    return sg * (1.0 + z * (1.0 - sg))


def _row_spec(tm, n):
    return pl.BlockSpec((tm, n), lambda i: (i, 0))


def _full_spec(shape):
    nd = len(shape)
    return pl.BlockSpec(shape, lambda i: (0,) * nd)


_ANY = pl.BlockSpec(memory_space=pl.ANY)


def _skip(n, body):
    return lambda *refs: body(*refs[n:])


FFN_ROWS = 256
FFN_WEIGHT_PARTS = N_CHIPS + 2


def _with_ffn_weights(w13_hbm, w2_hbm, w13_ref, w2_ref, sems, order, tile):
    first = pl.program_id(0) == 0
    copies = {("w13", k): pltpu.make_async_copy(w13_hbm.at[k], w13_ref.at[k], sems.at[k]) for k in range(N_CHIPS)}
    for half in range(2):
        rows = pl.ds(half * FF_SHARD, FF_SHARD)
        copies["w2", half] = pltpu.make_async_copy(w2_hbm.at[rows, :], w2_ref.at[rows, :], sems.at[N_CHIPS + half])

    @pl.when(first)
    def _():
        for part in order:
            copies[part].start()

        def ready(*parts):
            for part in parts:
                copies[part].wait()

        tile(ready)

    @pl.when(jnp.logical_not(first))
    def _():
        tile(lambda *parts: None)


def _ffn_fwd(x, g, w13s, w2, name, deps=()):
    t = x.shape[0]
    tm = FFN_ROWS
    deps = tuple(deps)

    def body(x_ref, g_ref, w13_hbm, w2_hbm, xo_ref, h_ref, gu_ref, w13_ref, w2_ref, sems):
        def tile(ready):
            xv = x_ref[...]
            hb = (xv * _rms_stats(xv) * g_ref[...]).astype(BF16)
            h_ref[...] = hb
            acc = jnp.zeros((tm, D_MODEL), F32)
            for half in range(2):
                lo = half * FF_SHARD
                ready(("w13", half), ("w13", 2 + half))
                gate = _dot(hb, w13_ref[half])
                up = _dot(hb, w13_ref[2 + half])
                gu_ref[:, lo:lo + FF_SHARD] = gate.astype(BF16)
                gu_ref[:, D_FF + lo:D_FF + lo + FF_SHARD] = up.astype(BF16)
                a = (gate * _sigmoid(gate) * up).astype(BF16)
                ready(("w2", half))
                acc = acc + _dot(a, w2_ref[lo:lo + FF_SHARD, :])
            xo_ref[...] = xv + 0.5 * acc

        _with_ffn_weights(w13_hbm, w2_hbm, w13_ref, w2_ref, sems,
                          [("w13", 0), ("w13", 2), ("w2", 0), ("w13", 1), ("w13", 3), ("w2", 1)], tile)

    return pl.pallas_call(
        _skip(len(deps), body), name=name, grid=(t // tm,),
        in_specs=[_ANY] * len(deps) + [_row_spec(tm, D_MODEL), _full_spec((1, D_MODEL)), _ANY, _ANY],
        out_specs=[_row_spec(tm, D_MODEL), _row_spec(tm, D_MODEL), _row_spec(tm, 2 * D_FF)],
        out_shape=[jax.ShapeDtypeStruct((t, D_MODEL), F32), jax.ShapeDtypeStruct((t, D_MODEL), BF16),
                   jax.ShapeDtypeStruct((t, 2 * D_FF), BF16)],
        scratch_shapes=[pltpu.VMEM(w13s.shape, BF16), pltpu.VMEM(w2.shape, BF16),
                        pltpu.SemaphoreType.DMA((FFN_WEIGHT_PARTS,))],
        compiler_params=_params(dimension_semantics=("arbitrary",)),
    )(*deps, x, g, w13s, w2)


def _ffn_bwd(dy, x, gu, g, w13s, w2, name, deps=()):
    t = x.shape[0]
    tm = FFN_ROWS
    deps = tuple(deps)

    def body(dy_ref, x_ref, gu_ref, g_ref, w13_hbm, w2_hbm, dx_ref, dgu_ref, a_ref, dg_ref, dyh_ref, dxb_ref,
             w13_ref, w2_ref, sems):
        @pl.when(pl.program_id(0) == 0)
        def _():
            dg_ref[...] = jnp.zeros_like(dg_ref)

        def tile(ready):
            dyv = dy_ref[...]
            dyh = (0.5 * dyv).astype(BF16)
            dyh_ref[...] = dyh
            dh = jnp.zeros((tm, D_MODEL), F32)
            for half in range(2):
                lo = half * FF_SHARD
                ready(("w2", half))
                da = _dot_nt(dyh, w2_ref[lo:lo + FF_SHARD, :])
                gate = gu_ref[:, lo:lo + FF_SHARD].astype(F32)
                up = gu_ref[:, D_FF + lo:D_FF + lo + FF_SHARD].astype(F32)
                sg = _sigmoid(gate)
                act = gate * sg
                a_ref[:, lo:lo + FF_SHARD] = (act * up).astype(BF16)
                dgate = (da * up * _silu_grad(gate, sg)).astype(BF16)
                dup = (da * act).astype(BF16)
                dgu_ref[:, lo:lo + FF_SHARD] = dgate
                dgu_ref[:, D_FF + lo:D_FF + lo + FF_SHARD] = dup
                ready(("w13", half), ("w13", 2 + half))
                dh = dh + _dot_nt(dgate, w13_ref[half]) + _dot_nt(dup, w13_ref[2 + half])
            xv = x_ref[...]
            dxn, dg_rows = _rms_bwd(xv, _rms_stats(xv), g_ref[...], dh)
            dx = dyv + dxn
            dx_ref[...] = dx
            dxb_ref[...] = dx.astype(BF16)
            dg_ref[...] += jnp.sum(dg_rows, axis=0, keepdims=True)

        _with_ffn_weights(w13_hbm, w2_hbm, w13_ref, w2_ref, sems,
                          [("w2", 0), ("w13", 0), ("w13", 2), ("w2", 1), ("w13", 1), ("w13", 3)], tile)

    return pl.pallas_call(
        _skip(len(deps), body), name=name, grid=(t // tm,),
        in_specs=[_ANY] * len(deps) + [_row_spec(tm, D_MODEL), _row_spec(tm, D_MODEL), _row_spec(tm, 2 * D_FF),
                                       _full_spec((1, D_MODEL)), _ANY, _ANY],
        out_specs=[_row_spec(tm, D_MODEL), _row_spec(tm, 2 * D_FF), _row_spec(tm, D_FF),
                   _full_spec((1, D_MODEL)), _row_spec(tm, D_MODEL), _row_spec(tm, D_MODEL)],
        out_shape=[jax.ShapeDtypeStruct((t, D_MODEL), F32), jax.ShapeDtypeStruct((t, 2 * D_FF), BF16),
                   jax.ShapeDtypeStruct((t, D_FF), BF16), jax.ShapeDtypeStruct((1, D_MODEL), F32),
                   jax.ShapeDtypeStruct((t, D_MODEL), BF16), jax.ShapeDtypeStruct((t, D_MODEL), BF16)],
        scratch_shapes=[pltpu.VMEM(w13s.shape, BF16), pltpu.VMEM(w2.shape, BF16),
                        pltpu.SemaphoreType.DMA((FFN_WEIGHT_PARTS,))],
        compiler_params=_params(dimension_semantics=("arbitrary",)),
    )(*deps, dy, x, gu, g, w13s, w2)


WGRAD_ROWS = (512, 384, 256)


def _wgrad(a, b, n_blocks, name, deps=()):
    t, m = a.shape
    tm = next(rows for rows in WGRAD_ROWS if m % rows == 0)
    n = b.shape[1]
    bn = n // n_blocks
    deps = tuple(deps)
    assert a.dtype == BF16 and b.dtype == BF16

    def body(a_ref, b_ref, o_ref):
        o_ref[0] = _dot_tn(a_ref[...], b_ref[...]).astype(BF16)

    return pl.pallas_call(
        _skip(len(deps), body), name=name, grid=(n_blocks, m // tm),
        in_specs=[_ANY] * len(deps) + [pl.BlockSpec((t, tm), lambda j, i: (0, i)),
                                       pl.BlockSpec((t, bn), lambda j, i: (0, j))],
        out_specs=pl.BlockSpec((1, tm, bn), lambda j, i: (j, i, 0)),
        out_shape=jax.ShapeDtypeStruct((n_blocks, m, bn), BF16),
        compiler_params=_params(dimension_semantics=("arbitrary", "arbitrary")),
    )(*deps, a, b)


def _mix_proj(x, g, w_ag, w_qkv, w_f):
    t = x.shape[0]
    tm = 256

    def body(x_ref, g_ref, wag_ref, wqkv_ref, wf_ref, h_ref, ag_ref, qkv_ref, fl_ref):
        xv = x_ref[...]
        hb = (xv * _rms_stats(xv) * g_ref[...]).astype(BF16)
        h_ref[...] = hb
        ag_ref[...] = _dot_nt(hb, wag_ref[...])
        qkv_ref[...] = _dot_nt(hb, wqkv_ref[...]).astype(BF16)
        fl_ref[...] = _dot_nt(hb, wf_ref[...])

    return pl.pallas_call(
        body, name="mix_proj", grid=(t // tm,),
        in_specs=[_row_spec(tm, D_MODEL), _full_spec((1, D_MODEL)), _full_spec(w_ag.shape),
                  _full_spec(w_qkv.shape), _full_spec(w_f.shape)],
        out_specs=[_row_spec(tm, D_MODEL), _row_spec(tm, 2 * D_CONV), _row_spec(tm, 3 * D_ATTN),
                   _row_spec(tm, LANES)],
        out_shape=[jax.ShapeDtypeStruct((t, D_MODEL), BF16), jax.ShapeDtypeStruct((t, 2 * D_CONV), F32),
                   jax.ShapeDtypeStruct((t, 3 * D_ATTN), BF16), jax.ShapeDtypeStruct((t, LANES), F32)],
        compiler_params=_params(dimension_semantics=("arbitrary",)),
    )(x, g, w_ag, w_qkv, w_f)


def _mix_proj_bwd(dag, dqkv, dfl, dx2, x1, g, w_ag, w_qkv, w_f):
    t = x1.shape[0]
    tm = 256

    def body(dag_ref, dqkv_ref, dfl_ref, dx2_ref, x_ref, g_ref, wag_ref, wqkv_ref, wf_ref, dx_ref, dg_ref):
        @pl.when(pl.program_id(0) == 0)
        def _():
            dg_ref[...] = jnp.zeros_like(dg_ref)

        dh = (_dot(dag_ref[...].astype(BF16), wag_ref[...]) + _dot(dqkv_ref[...], wqkv_ref[...])
              + _dot(dfl_ref[...].astype(BF16), wf_ref[...]))
        xv = x_ref[...]
        dxn, dg_rows = _rms_bwd(xv, _rms_stats(xv), g_ref[...], dh)
        dx_ref[...] = dx2_ref[...] + dxn
        dg_ref[...] += jnp.sum(dg_rows, axis=0, keepdims=True)

    return pl.pallas_call(
        body, name="mix_proj_bwd", grid=(t // tm,),
        in_specs=[_row_spec(tm, 2 * D_CONV), _row_spec(tm, 3 * D_ATTN), _row_spec(tm, LANES),
                  _row_spec(tm, D_MODEL), _row_spec(tm, D_MODEL), _full_spec((1, D_MODEL)),
                  _full_spec(w_ag.shape), _full_spec(w_qkv.shape), _full_spec(w_f.shape)],
        out_specs=[_row_spec(tm, D_MODEL), _full_spec((1, D_MODEL))],
        out_shape=[jax.ShapeDtypeStruct((t, D_MODEL), F32), jax.ShapeDtypeStruct((1, D_MODEL), F32)],
        compiler_params=_params(dimension_semantics=("arbitrary",)),
    )(dag, dqkv, dfl, dx2, x1, g, w_ag, w_qkv, w_f)


def _split3(x):
    hi = x.astype(BF16)
    r1 = x - hi.astype(F32)
    mid = r1.astype(BF16)
    lo = (r1 - mid.astype(F32)).astype(BF16)
    return hi, mid, lo


def _gates_fwd(flt, fb):
    t = flt.shape[1]

    def body(f_ref, b_ref, d_ref):
        z = f_ref[...] + b_ref[...]
        logf = jnp.minimum(z, 0.0) - jnp.log(1.0 + jnp.exp(-jnp.abs(z)))
        row = lax.broadcasted_iota(jnp.int32, (LANES, LANES), 0)
        col = lax.broadcasted_iota(jnp.int32, (LANES, LANES), 1)
        upper = (row <= col).astype(BF16)
        carry = jnp.zeros((HEAD_ROWS, 1), F32)
        for blk in range(t // LANES):
            hi, mid, lo = _split3(logf[:, blk * LANES:(blk + 1) * LANES])
            cs = _dot(hi, upper) + _dot(mid, upper) + _dot(lo, upper)
            d_ref[:, blk * LANES:(blk + 1) * LANES] = cs + carry
            carry = carry + cs[:, LANES - 1:LANES]

    return pl.pallas_call(
        body, name="gates_fwd", out_shape=jax.ShapeDtypeStruct((HEAD_ROWS, t), F32),
        compiler_params=_params(),
    )(flt, fb)


def _gates_bwd(dd, flt, fb):
    t = flt.shape[1]

    def body(dd_ref, f_ref, b_ref, df_ref, db_ref):
        z = f_ref[...] + b_ref[...]
        row = lax.broadcasted_iota(jnp.int32, (LANES, LANES), 0)
        col = lax.broadcasted_iota(jnp.int32, (LANES, LANES), 1)
        lower = (row >= col).astype(BF16)
        carry = jnp.zeros((HEAD_ROWS, 1), F32)
        db = jnp.zeros((HEAD_ROWS, 1), F32)
        for blk in reversed(range(t // LANES)):
            sl = slice(blk * LANES, (blk + 1) * LANES)
            hi, mid, lo = _split3(dd_ref[:, sl])
            cs = _dot(hi, lower) + _dot(mid, lower) + _dot(lo, lower)
            dz = (cs + carry) * _sigmoid(-z[:, sl])
            df_ref[:, sl] = dz
            db = db + jnp.sum(dz, axis=1, keepdims=True)
            carry = carry + cs[:, 0:1]
        db_ref[...] = db

    return pl.pallas_call(
        body, name="gates_bwd",
        out_shape=[jax.ShapeDtypeStruct((HEAD_ROWS, t), F32), jax.ShapeDtypeStruct((HEAD_ROWS, 1), F32)],
        compiler_params=_params(),
    )(dd, flt, fb)


CONV_CHUNK = 128
CONV_TAIL = 16
CONV_WINDOW = CONV_CHUNK + CONV_PAD + 8
CONV_ROWS_EXTRA = CONV_PAD + CONV_TAIL
SUBLANES = 8


def _conv_rows(ag_ref, u_ref, t):
    u_ref[0:CONV_PAD, :] = jnp.zeros((CONV_PAD, D_CONV), F32)
    u_ref[CONV_PAD + t:CONV_ROWS_EXTRA + t, :] = jnp.zeros((CONV_TAIL, D_CONV), F32)

    def fill(i, c):
        r0 = pl.multiple_of(i * CONV_CHUNK, CONV_CHUNK)
        a = ag_ref[pl.ds(r0, CONV_CHUNK), 0:D_CONV]
        gt = ag_ref[pl.ds(r0, CONV_CHUNK), D_CONV:2 * D_CONV]
        u_ref[pl.ds(CONV_PAD + r0, CONV_CHUNK), :] = a * _sigmoid(gt)
        return c

    lax.fori_loop(0, t // CONV_CHUNK, fill, 0)


def _for_shifted(ref, r0, offsets, fn):
    window = ref[pl.ds(r0, CONV_WINDOW), :]
    for rem in range(SUBLANES):
        mine = [o for o in offsets if o % SUBLANES == rem]
        if not mine:
            continue
        turned = window if rem == 0 else pltpu.roll(window, CONV_WINDOW - rem, 0)
        for o in mine:
            fn(o, turned[o - rem:o - rem + CONV_CHUNK])


def _conv_point(u_ref, r0, w_ref, cb, lg, lb):
    acc = [jnp.zeros((CONV_CHUNK, D_CONV), F32)]

    def tap(o, rows):
        j = o - (CONV_PAD - CONV_WIDTH + 1)
        acc[0] = acc[0] + w_ref[j:j + 1, :] * rows

    _for_shifted(u_ref, r0, [j + CONV_PAD - CONV_WIDTH + 1 for j in range(CONV_WIDTH)], tap)
    y = acc[0] + cb
    mu = jnp.mean(y, axis=-1, keepdims=True)
    yc = y - mu
    rstd = lax.rsqrt(jnp.mean(yc * yc, axis=-1, keepdims=True) + EPS)
    yhat = yc * rstd
    z = yhat * lg + lb
    sg = _sigmoid(z)
    s = z * sg
    rr = _rms_stats(s)
    return yhat, rstd, z, sg, s, rr


def _conv_fwd(ag, conv_w, conv_b, ln_g, ln_b, norm_g):
    t = ag.shape[0]

    def body(ag_ref, w_ref, cb_ref, lg_ref, lb_ref, ng_ref, o_ref, u_ref):
        _conv_rows(ag_ref, u_ref, t)
        cb, lg, lb, ng = cb_ref[...], lg_ref[...], lb_ref[...], ng_ref[...]

        def chunk(i, c):
            r0 = pl.multiple_of(i * CONV_CHUNK, CONV_CHUNK)
            _, _, _, _, s, rr = _conv_point(u_ref, r0, w_ref, cb, lg, lb)
            o_ref[pl.ds(r0, CONV_CHUNK), :] = (s * rr * ng).astype(BF16)
            return c

        lax.fori_loop(0, t // CONV_CHUNK, chunk, 0)

    return pl.pallas_call(
        body, name="conv_fwd", out_shape=jax.ShapeDtypeStruct((t, D_CONV), BF16),
        scratch_shapes=[pltpu.VMEM((t + CONV_ROWS_EXTRA, D_CONV), F32)],
        compiler_params=_params(),
    )(ag, conv_w, conv_b, ln_g, ln_b, norm_g)


def _conv_bwd(ag, dout, conv_w, conv_b, ln_g, ln_b, norm_g):
    t = ag.shape[0]

    def body(ag_ref, do_ref, w_ref, cb_ref, lg_ref, lb_ref, ng_ref,
             dag_ref, dw_ref, dcb_ref, dlg_ref, dlb_ref, dng_ref, u_ref, dy_ref):
        _conv_rows(ag_ref, u_ref, t)
        dy_ref[t:t + CONV_ROWS_EXTRA, :] = jnp.zeros((CONV_ROWS_EXTRA, D_CONV), F32)
        cb, lg, lb, ng = cb_ref[...], lg_ref[...], lb_ref[...], ng_ref[...]
        dw_ref[...] = jnp.zeros_like(dw_ref)
        zero = jnp.zeros((1, D_CONV), F32)

        def chunk(i, carry):
            dcb, dlg, dlb, dng = carry
            r0 = pl.multiple_of(i * CONV_CHUNK, CONV_CHUNK)
            yhat, rstd, z, sg, s, rr = _conv_point(u_ref, r0, w_ref, cb, lg, lb)
            do = do_ref[pl.ds(r0, CONV_CHUNK), :]
            ds, dng_rows = _rms_bwd(s, rr, ng, do)
            dz = ds * _silu_grad(z, sg)
            dyhat = dz * lg
            dy = rstd * (dyhat - jnp.mean(dyhat, axis=-1, keepdims=True)
                         - yhat * jnp.mean(dyhat * yhat, axis=-1, keepdims=True))
            dy_ref[pl.ds(r0, CONV_CHUNK), :] = dy
            def tap(o, rows):
                j = o - (CONV_PAD - CONV_WIDTH + 1)
                dw_ref[j:j + 1, :] += jnp.sum(dy * rows, axis=0, keepdims=True)

            _for_shifted(u_ref, r0, [j + CONV_PAD - CONV_WIDTH + 1 for j in range(CONV_WIDTH)], tap)
            return (dcb + jnp.sum(dy, axis=0, keepdims=True), dlg + jnp.sum(dz * yhat, axis=0, keepdims=True),
                    dlb + jnp.sum(dz, axis=0, keepdims=True), dng + jnp.sum(dng_rows, axis=0, keepdims=True))

        dcb, dlg, dlb, dng = lax.fori_loop(0, t // CONV_CHUNK, chunk, (zero, zero, zero, zero))
        dcb_ref[...] = dcb
        dlg_ref[...] = dlg
        dlb_ref[...] = dlb
        dng_ref[...] = dng

        def chunk2(i, c):
            r0 = pl.multiple_of(i * CONV_CHUNK, CONV_CHUNK)
            acc = [jnp.zeros((CONV_CHUNK, D_CONV), F32)]

            def tap(o, rows):
                j = CONV_WIDTH - 1 - o
                acc[0] = acc[0] + w_ref[j:j + 1, :] * rows

            _for_shifted(dy_ref, r0, list(range(CONV_WIDTH)), tap)
            du = acc[0]
            a = ag_ref[pl.ds(r0, CONV_CHUNK), 0:D_CONV]
            gt = ag_ref[pl.ds(r0, CONV_CHUNK), D_CONV:2 * D_CONV]
            sg = _sigmoid(gt)
            dag_ref[pl.ds(r0, CONV_CHUNK), 0:D_CONV] = du * sg
            dag_ref[pl.ds(r0, CONV_CHUNK), D_CONV:2 * D_CONV] = du * a * sg * (1.0 - sg)
            return c

        lax.fori_loop(0, t // CONV_CHUNK, chunk2, 0)

    vec = jax.ShapeDtypeStruct((1, D_CONV), F32)
    return pl.pallas_call(
        body, name="conv_bwd",
        out_shape=[jax.ShapeDtypeStruct((t, 2 * D_CONV), F32), jax.ShapeDtypeStruct((CONV_PAD, D_CONV), F32),
                   vec, vec, vec, vec],
        scratch_shapes=[pltpu.VMEM((t + CONV_ROWS_EXTRA, D_CONV), F32), pltpu.VMEM((t + CONV_ROWS_EXTRA, D_CONV), F32)],
        compiler_params=_params(),
    )(ag, dout, conv_w, conv_b, ln_g, ln_b, norm_g)


Q_ROWS = 256
ATTN_SCALE = HEAD_DIM ** -0.5


def _attn_specs(t):
    blk = lambda off: pl.BlockSpec((t, LANES), lambda p: (0, off + p))
    pairs = N_HEADS // 2
    return [blk(0), blk(pairs), blk(2 * pairs), pl.BlockSpec((2, 1, t), lambda p: (p, 0, 0))]


def _one_head(q2, mask):
    return jnp.where(mask, q2, jnp.zeros_like(q2)) * ATTN_SCALE


def _attn_scores(qs, k2, drow, r0, q1):
    s = _dot_nt(qs, k2) - drow
    rowi = lax.broadcasted_iota(jnp.int32, (q1 - r0, q1 - r0), 0)
    coli = lax.broadcasted_iota(jnp.int32, (q1 - r0, q1 - r0), 1)
    diag = jnp.where(coli <= rowi, s[:, r0:q1], -jnp.inf)
    return diag if r0 == 0 else jnp.concatenate([s[:, :r0], diag], axis=1)


def _attn_fwd(qkv, drow, deps=()):
    t = qkv.shape[0]
    deps = tuple(deps)

    def body(q_ref, k_ref, v_ref, dr_ref, o_ref, lse_ref):
        head_a = lax.broadcasted_iota(jnp.int32, (1, LANES), 1) < HEAD_DIM
        for qb in range(t // Q_ROWS):
            r0, q1 = qb * Q_ROWS, (qb + 1) * Q_ROWS
            q2 = q_ref[r0:q1, :]
            k2 = k_ref[0:q1, :]
            v2 = v_ref[0:q1, :]
            outs = []
            for hh in range(2):
                qs = _one_head(q2, head_a if hh == 0 else ~head_a)
                s = _attn_scores(qs, k2, dr_ref[hh, :, 0:q1], r0, q1)
                mx = jnp.max(s, axis=1, keepdims=True)
                p = jnp.exp(s - mx)
                l = jnp.sum(p, axis=1, keepdims=True)
                lse_ref[hh, r0:q1, :] = mx + jnp.log(l)
                outs.append(_dot((p * (1.0 / l)).astype(BF16), v2))
            o_ref[r0:q1, :] = jnp.where(head_a, outs[0], outs[1])

    pairs = N_HEADS // 2
    return pl.pallas_call(
        _skip(len(deps), body), name="attn_fwd", grid=(pairs,), in_specs=[_ANY] * len(deps) + _attn_specs(t),
        out_specs=[pl.BlockSpec((t, LANES), lambda p: (0, p)), pl.BlockSpec((2, t, 1), lambda p: (p, 0, 0))],
        out_shape=[jax.ShapeDtypeStruct((t, D_ATTN), F32), jax.ShapeDtypeStruct((N_HEADS, t, 1), F32)],
        compiler_params=_params(dimension_semantics=("arbitrary",)),
    )(*deps, qkv, qkv, qkv, drow)


def _attn_bwd(qkv, drow, lse, do):
    t = qkv.shape[0]

    def body(q_ref, k_ref, v_ref, dr_ref, lse_ref, do_ref,
             dq_ref, dk_ref, dv_ref, dd_ref, dk_acc, dv_acc):
        head_a = lax.broadcasted_iota(jnp.int32, (1, LANES), 1) < HEAD_DIM
        dk_acc[...] = jnp.zeros_like(dk_acc)
        dv_acc[...] = jnp.zeros_like(dv_acc)
        dd_ref[...] = jnp.zeros_like(dd_ref)
        for qb in range(t // Q_ROWS):
            r0, q1 = qb * Q_ROWS, (qb + 1) * Q_ROWS
            q2 = q_ref[r0:q1, :]
            k2 = k_ref[0:q1, :]
            v2 = v_ref[0:q1, :]
            do2 = do_ref[r0:q1, :]
            dqs = []
            dk_sum = jnp.zeros((q1, LANES), F32)
            dv_sum = jnp.zeros((q1, LANES), F32)
            for hh in range(2):
                mask = head_a if hh == 0 else ~head_a
                qs = _one_head(q2, mask)
                dob = jnp.where(mask, do2, 0.0).astype(BF16)
                p = jnp.exp(_attn_scores(qs, k2, dr_ref[hh, :, 0:q1], r0, q1) - lse_ref[hh, r0:q1, :])
                dp = _dot_nt(dob, v2)
                ds = p * (dp - jnp.sum(p * dp, axis=1, keepdims=True))
                dsb = ds.astype(BF16)
                dqs.append(_dot(dsb, k2) * ATTN_SCALE)
                dk_sum = dk_sum + _dot_tn(dsb, qs)
                dv_sum = dv_sum + _dot_tn(p.astype(BF16), dob)
                dd_ref[hh, :, 0:q1] -= jnp.sum(ds, axis=0, keepdims=True)
            dq_ref[r0:q1, :] = jnp.where(head_a, dqs[0], dqs[1]).astype(BF16)
            dk_acc[0:q1, :] += dk_sum
            dv_acc[0:q1, :] += dv_sum
        dk_ref[...] = dk_acc[...].astype(BF16)
        dv_ref[...] = dv_acc[...].astype(BF16)

    pairs = N_HEADS // 2
    col = pl.BlockSpec((t, LANES), lambda p: (0, p))
    grad = jax.ShapeDtypeStruct((t, D_ATTN), BF16)
    return pl.pallas_call(
        body, name="attn_bwd", grid=(pairs,),
        in_specs=_attn_specs(t) + [pl.BlockSpec((2, t, 1), lambda p: (p, 0, 0)), col],
        out_specs=[col, col, col, pl.BlockSpec((2, 1, t), lambda p: (p, 0, 0))],
        out_shape=[grad, grad, grad, jax.ShapeDtypeStruct((N_HEADS, 1, t), F32)],
        scratch_shapes=[pltpu.VMEM((t, LANES), F32), pltpu.VMEM((t, LANES), F32)],
        compiler_params=_params(dimension_semantics=("arbitrary",)),
    )(qkv, qkv, qkv, drow, lse, do)


def _out_proj(ycn, o, g_attn, w_out, x1, deps=()):
    t = x1.shape[0]
    tm = 256
    deps = tuple(deps)

    def body(yc_ref, o_ref, g_ref, w_ref, x_ref, xo_ref, ya_ref):
        ov = o_ref[...]
        ya = (ov * _rms_stats(ov) * g_ref[...]).astype(BF16)
        ya_ref[...] = ya
        xo_ref[...] = x_ref[...] + _dot(yc_ref[...], w_ref[0:D_CONV, :]) + _dot(ya, w_ref[D_CONV:, :])

    return pl.pallas_call(
        _skip(len(deps), body), name="out_proj", grid=(t // tm,),
        in_specs=[_ANY] * len(deps) + [_row_spec(tm, D_CONV), _row_spec(tm, D_ATTN), _full_spec((1, D_ATTN)),
                                       _full_spec(w_out.shape), _row_spec(tm, D_MODEL)],
        out_specs=[_row_spec(tm, D_MODEL), _row_spec(tm, D_ATTN)],
        out_shape=[jax.ShapeDtypeStruct((t, D_MODEL), F32), jax.ShapeDtypeStruct((t, D_ATTN), BF16)],
        compiler_params=_params(dimension_semantics=("arbitrary",)),
    )(*deps, ycn, o, g_attn, w_out, x1)


def _out_proj_bwd(dx2, o, g_attn, w_out, deps=()):
    t = dx2.shape[0]
    tm = 256
    deps = tuple(deps)

    def body(dx_ref, o_ref, g_ref, w_ref, dyc_ref, do_ref, dg_ref):
        @pl.when(pl.program_id(0) == 0)
        def _():
            dg_ref[...] = jnp.zeros_like(dg_ref)

        dxb = dx_ref[...]
        dyc_ref[...] = _dot_nt(dxb, w_ref[0:D_CONV, :])
        dya = _dot_nt(dxb, w_ref[D_CONV:, :])
        ov = o_ref[...]
        do, dg_rows = _rms_bwd(ov, _rms_stats(ov), g_ref[...], dya)
        do_ref[...] = do
        dg_ref[...] += jnp.sum(dg_rows, axis=0, keepdims=True)

    return pl.pallas_call(
        _skip(len(deps), body), name="out_proj_bwd", grid=(t // tm,),
        in_specs=[_ANY] * len(deps) + [_row_spec(tm, D_MODEL), _row_spec(tm, D_ATTN), _full_spec((1, D_ATTN)),
                                       _full_spec(w_out.shape)],
        out_specs=[_row_spec(tm, D_CONV), _row_spec(tm, D_ATTN), _full_spec((1, D_ATTN))],
        out_shape=[jax.ShapeDtypeStruct((t, D_CONV), F32), jax.ShapeDtypeStruct((t, D_ATTN), F32),
                   jax.ShapeDtypeStruct((1, D_ATTN), F32)],
        compiler_params=_params(dimension_semantics=("arbitrary",)),
    )(*deps, dx2, o, g_attn, w_out)


def _loss_bwd(x3, target, g):
    t = x3.shape[0]
    tm = 256

    def body(x_ref, t_ref, g_ref, loss_ref, dx_ref, dg_ref):
        @pl.when(pl.program_id(0) == 0)
        def _():
            loss_ref[...] = jnp.zeros_like(loss_ref)
            dg_ref[...] = jnp.zeros_like(dg_ref)

        xv = x_ref[...]
        r = _rms_stats(xv)
        gv = g_ref[...]
        err = xv * r * gv - t_ref[...]
        row = jnp.sum(err * err, axis=1, keepdims=True) * (0.5 / D_MODEL)
        loss_ref[...] += jnp.sum(row, axis=0, keepdims=True)
        dx, dg_rows = _rms_bwd(xv, r, gv, err * (1.0 / D_MODEL))
        dx_ref[...] = dx
        dg_ref[...] += jnp.sum(dg_rows, axis=0, keepdims=True)

    return pl.pallas_call(
        body, name="loss_bwd", grid=(t // tm,),
        in_specs=[_row_spec(tm, D_MODEL), _row_spec(tm, D_MODEL), _full_spec((1, D_MODEL))],
        out_specs=[_full_spec((1, LANES)), _row_spec(tm, D_MODEL), _full_spec((1, D_MODEL))],
        out_shape=[jax.ShapeDtypeStruct((1, LANES), F32), jax.ShapeDtypeStruct((t, D_MODEL), F32),
                   jax.ShapeDtypeStruct((1, D_MODEL), F32)],
        compiler_params=_params(dimension_semantics=("arbitrary",)),
    )(x3, target, g)


def _split_w_in(w_in_t):
    w_ag = w_in_t[:2 * D_CONV]
    w_qkv = w_in_t[2 * D_CONV:2 * D_CONV + 3 * D_ATTN]
    w_f = jnp.pad(w_in_t[2 * D_CONV + 3 * D_ATTN:], ((0, LANES - N_HEADS), (0, 0)))
    return w_ag, w_qkv, w_f


def _head_rows(v):
    return jnp.pad(v, ((0, HEAD_ROWS - N_HEADS),) + ((0, 0),) * (v.ndim - 1))


def _local_step(x, target, p, get_weights, put_grads, flush_grads):
    t = x.shape[0]
    fb = _head_rows(p["forget_b"].reshape(N_HEADS, 1))

    w, deps = get_weights("ffn1", None)
    x1, h1, gu1 = _ffn_fwd(x, p["ffn1_norm"], w["ffn1_w13"], w["ffn1_w2"], "ffn1_fwd", deps)
    wm, _ = get_weights("mix", x1)
    w.update(wm)
    w_ag, w_qkv, w_f = _split_w_in(w["w_in"])
    conv_w = jnp.pad(w["conv_w"], ((0, CONV_PAD - CONV_WIDTH), (0, 0)))
    h2, ag, qkv, fl = _mix_proj(x1, p["mix_norm"], w_ag, w_qkv, w_f)
    flt = _head_rows(fl[:, :N_HEADS].T)
    dcum = _gates_fwd(flt, fb)[:N_HEADS]
    drow = dcum.reshape(N_HEADS, 1, t)
    ycn = _conv_fwd(ag, conv_w, p["conv_b"], p["conv_ln_g"], p["conv_ln_b"], p["out_norm_conv"])
    o, lse = _attn_fwd(qkv, drow, [ycn])
    _, deps = get_weights("ffn2:landed", o)
    x2, yan = _out_proj(ycn, o, p["out_norm_attn"], w["w_out"], x1, deps)
    w2, _ = get_weights("ffn2", x2)
    w.update(w2)
    x3, h3, gu2 = _ffn_fwd(x2, p["ffn2_norm"], w["ffn2_w13"], w["ffn2_w2"], "ffn2_fwd")
    loss, dx3, d_final = _loss_bwd(x3, target, p["final_norm"])

    g = {}
    dx2, dgu2, a2, g["ffn2_norm"], dx3_half, dx2_bf16 = _ffn_bwd(
        dx3, x2, gu2, p["ffn2_norm"], w["ffn2_w13"], w["ffn2_w2"], "ffn2_bwd")
    dw13 = _wgrad(h3, dgu2, N_CHIPS, "ffn2_dw13")
    dw2 = _wgrad(a2, dx3_half, 1, "ffn2_dw2").reshape(D_FF, D_MODEL)
    deps = put_grads("ffn2", {"ffn2_w13": dw13, "ffn2_w2": dw2})
    dyc, do, g["out_norm_attn"] = _out_proj_bwd(dx2_bf16, o, p["out_norm_attn"], w["w_out"], deps)
    deps = flush_grads("ffn2", [dyc])
    dw_out = _wgrad(jnp.concatenate([ycn, yan], axis=1), dx2_bf16, 1, "dw_out", deps).reshape(D_MODEL, D_MODEL)
    dq, dk, dv, ddrow = _attn_bwd(qkv, drow, lse, do)
    dflt, dfb = _gates_bwd(_head_rows(ddrow.reshape(N_HEADS, t)), flt, fb)
    g["forget_b"] = dfb[:N_HEADS, 0].reshape(1, N_HEADS)
    dfl = jnp.pad(dflt[:N_HEADS].T, ((0, 0), (0, LANES - N_HEADS)))
    dag, dconv_w, g["conv_b"], g["conv_ln_g"], g["conv_ln_b"], g["out_norm_conv"] = _conv_bwd(
        ag, dyc, conv_w, p["conv_b"], p["conv_ln_g"], p["conv_ln_b"], p["out_norm_conv"])
    g["conv_w"] = dconv_w[:CONV_WIDTH]
    dqkv = jnp.concatenate([dq, dk, dv], axis=1)
    dx1, g["mix_norm"] = _mix_proj_bwd(dag, dqkv, dfl, dx2, x1, p["mix_norm"], w_ag, w_qkv, w_f)
    dproj = jnp.concatenate([dag.astype(BF16), dqkv, dfl.astype(BF16)], axis=1)
    dw_in = _wgrad(dproj, h2, 1, "dw_in").reshape(dproj.shape[1], D_MODEL)[:N_IN]
    deps = put_grads("mix", {"w_in": dw_in, "w_out": dw_out})
    dx0, dgu1, a1, g["ffn1_norm"], dx1_half, _ = _ffn_bwd(
        dx1, x, gu1, p["ffn1_norm"], w["ffn1_w13"], w["ffn1_w2"], "ffn1_bwd", deps)
    g["final_norm"] = d_final
    g["loss"] = loss[:, :1]
    deps = flush_grads("mix", put_grads("small", g))
    dw2 = _wgrad(a1, dx1_half, 1, "ffn1_dw2", deps).reshape(D_FF, D_MODEL)
    deps = flush_grads("ffn1_w2", put_grads("ffn1_w2", {"ffn1_w2": dw2}))
    dw13 = _wgrad(h1, dgu1, N_CHIPS, "ffn1_dw13", deps)
    put_grads("ffn1_w13", {"ffn1_w13": dw13})
    return dx0


MESH = pl.DeviceIdType.MESH


def _place():
    x, y, c = lax.axis_index("x"), lax.axis_index("y"), lax.axis_index("c")
    chips = [(1 - x, y), (x, 1 - y), (1 - x, 1 - y)]
    return x, y, c, chips


def _hbm_out(shape, dtype):
    return jax.ShapeDtypeStruct(shape, dtype)


def _comm_call(body, name, ins, out_shapes, n_remote, in_place=False):
    return pl.pallas_call(
        body, name=name, in_specs=[_ANY] * len(ins), out_specs=[_ANY] * len(out_shapes), out_shape=out_shapes,
        scratch_shapes=[pltpu.SemaphoreType.DMA((n_remote,)), pltpu.SemaphoreType.DMA((n_remote,))],
        input_output_aliases={i: i for i in range(len(ins))} if in_place else {},
    )(*ins)


def _remote(src, dst, sems, n, to):
    send_sems, recv_sems = sems
    return pltpu.make_async_remote_copy(src_ref=src, dst_ref=dst, send_sem=send_sems.at[n], recv_sem=recv_sems.at[n],
                                        device_id=to, device_id_type=MESH)


HALF_ROWS_MULTIPLE = 32


def _halved_by_rows(rows):
    return rows % HALF_ROWS_MULTIPLE == 0


def _half_shape(rows, cols):
    return (rows // 2, cols) if _halved_by_rows(rows) else (rows, cols // 2)


def _half_index(rows, core):
    return (core, 0) if _halved_by_rows(rows) else (0, core)


def _half_of(ref, rows, cols, core, *lead):
    if _halved_by_rows(rows):
        return ref.at[(*lead, pl.ds(core * (rows // 2), rows // 2), slice(None))]
    return ref.at[(*lead, slice(None), pl.ds(core * (cols // 2), cols // 2))]


def _into_slot(shard, chip, dtype, name):
    rows, cols = shard.shape
    half = _half_shape(rows, cols)
    by_rows = _halved_by_rows(rows)

    def body(k_ref, s_ref, o_ref):
        o_ref[0] = s_ref[...].astype(dtype)

    return pl.pallas_call(
        body, name=name,
        grid_spec=pltpu.PrefetchScalarGridSpec(
            num_scalar_prefetch=1, grid=(2,),
            in_specs=[pl.BlockSpec(half, lambda i, k_ref: (i, 0) if by_rows else (0, i))],
            out_specs=pl.BlockSpec((1,) + half, lambda i, k_ref: (k_ref[0], i, 0) if by_rows else (k_ref[0], 0, i))),
        out_shape=jax.ShapeDtypeStruct((N_CHIPS, rows, cols), dtype),
        compiler_params=_params(dimension_semantics=("arbitrary",)),
    )(chip, shard)


def _gather_shards(slots, name, ici=True, passed=()):
    n = len(slots)
    slots = list(slots) + list(passed)
    total = len(slots)

    def body(*refs):
        outs = refs[total:total + n]
        sems = refs[2 * total:2 * total + 2]
        x, y, c, chips = _place()
        me = 2 * x + y
        sibling = (x, y, 1 - c)

        def half(i, chip_index, core):
            return _half_of(outs[i], *slots[i].shape[1:], core, chip_index)

        sends = []
        if ici:
            for i in range(n):
                for j, chip in enumerate(chips):
                    cp = _remote(half(i, me, c), half(i, me, c), sems, 6 * i + j, (*chip, c))
                    cp.start()
                    sends.append(cp)
        for i in range(n):
            for j, chip in enumerate(chips):
                src_chip = 2 * chip[0] + chip[1]
                landed = half(i, src_chip, c)
                if ici:
                    _remote(landed, landed, sems, 6 * i + j, (*chip, c)).wait_recv()
                cp = _remote(landed, landed, sems, 6 * i + 3 + j, sibling)
                cp.start()
                sends.append(cp)
        for i in range(n):
            for j, chip in enumerate(chips):
                src_chip = 2 * chip[0] + chip[1]
                landed = half(i, src_chip, 1 - c)
                _remote(landed, landed, sems, 6 * i + 3 + j, sibling).wait_recv()
        for cp in sends:
            cp.wait_send()

    outs = [_hbm_out(s.shape, s.dtype) for s in slots]
    return _comm_call(body, name, slots, outs, 6 * n, in_place=True)


_HBM = pl.BlockSpec(memory_space=pltpu.HBM)
_SEM = pl.BlockSpec(memory_space=pltpu.SEMAPHORE)
_DATAFLOW = pltpu.SideEffectType.DATAFLOW_SIDE_EFFECTING


def _split_copy_start(name, bufs, n_copies, plan):
    n = len(bufs)

    def body(*refs):
        for send, _ in plan(refs[:n], (refs[n], refs[n + 1])):
            send.start()
        token = refs[-1]
        token[...] = jnp.zeros_like(token)

    out = pl.pallas_call(
        body, name=name,
        out_shape=(pltpu.SemaphoreType.DMA((n_copies,)), pltpu.SemaphoreType.DMA((n_copies,)),
                   *[pltpu.HBM(b.shape, b.dtype) for b in bufs], jax.ShapeDtypeStruct((8, LANES), F32)),
        in_specs=[_HBM] * n, out_specs=(_SEM, _SEM, *[_HBM] * n, pl.BlockSpec(memory_space=pltpu.VMEM)),
        input_output_aliases={i: 2 + i for i in range(n)},
        compiler_params=pltpu.CompilerParams(has_side_effects=_DATAFLOW),
    )(*[pltpu.with_memory_space_constraint(b, pltpu.HBM) for b in bufs])
    return out[0], out[1], list(out[2:2 + n]), out[-1]


def _split_copy_wait(name, started, plan, after):
    send_sems, recv_sems, bufs, _ = started
    n = len(bufs)
    after = tuple(after)

    def body(*refs):
        for send, recv in plan(refs[:n], (refs[n], refs[n + 1])):
            send.wait_send()
            recv.wait_recv()

    out = pl.pallas_call(
        body, name=name, out_shape=tuple(pltpu.HBM(b.shape, b.dtype) for b in bufs),
        in_specs=[_HBM] * n + [_SEM, _SEM] + [_ANY] * len(after), out_specs=tuple([_HBM] * n),
        input_output_aliases={i: i for i in range(n)},
        compiler_params=pltpu.CompilerParams(has_side_effects=_DATAFLOW),
    )(*bufs, send_sems, recv_sems, *after)
    return list(out)


def _ici_gather_plan(slots):
    def plan(refs, sems):
        x, y, c, chips = _place()
        me = 2 * x + y
        copies = []
        for i, ref in enumerate(refs):
            for j, chip in enumerate(chips):
                mine = _half_of(ref, *slots[i].shape[1:], c, me)
                theirs = _half_of(ref, *slots[i].shape[1:], c, 2 * chip[0] + chip[1])
                to = (*chip, c)
                copies.append((_remote(mine, mine, sems, 3 * i + j, to), _remote(theirs, theirs, sems, 3 * i + j, to)))
        return copies

    return plan


def _ici_scatter_plan(n):
    def plan(refs, sems):
        x, y, c, chips = _place()
        copies = []
        for i in range(n):
            for j, chip in enumerate(chips):
                cp = _remote(refs[i].at[2 * chip[0] + chip[1]], refs[n + i].at[j], sems, 3 * i + j, (*chip, c))
                copies.append((cp, cp))
        return copies

    return plan


def _d2d_forward_plan(slots):
    def plan(refs, sems):
        x, y, c, chips = _place()
        sibling = (x, y, 1 - c)
        copies = []
        for i, ref in enumerate(refs):
            for j, chip in enumerate(chips):
                src_chip = 2 * chip[0] + chip[1]
                mine = _half_of(ref, *slots[i].shape[1:], c, src_chip)
                theirs = _half_of(ref, *slots[i].shape[1:], 1 - c, src_chip)
                copies.append((_remote(mine, mine, sems, 3 * i + j, sibling),
                               _remote(theirs, theirs, sems, 3 * i + j, sibling)))
        return copies

    return plan


def _pair_exchange_plan(grads):
    n = len(grads)

    def plan(refs, sems):
        x, y, c, _ = _place()
        copies = []
        for i in range(n):
            theirs = _half_of(refs[i], *grads[i].shape[1:], 1 - c, slice(None))
            cp = _remote(theirs, refs[n + i], sems, i, (x, y, 1 - c))
            copies.append((cp, cp))
        return copies

    return plan


def _pair_share_plan(shapes):
    def plan(refs, sems):
        x, y, c, _ = _place()
        sibling = (x, y, 1 - c)
        copies = []
        for i, ref in enumerate(refs):
            mine, theirs = _half_of(ref, *shapes[i], c), _half_of(ref, *shapes[i], 1 - c)
            copies.append((_remote(mine, mine, sems, i, sibling), _remote(theirs, theirs, sems, i, sibling)))
        return copies

    return plan


def _pair_share(halves, name):
    n = len(halves)
    plan = _pair_share_plan([h.shape for h in halves])

    def body(*refs):
        copies = plan(refs[n:2 * n], refs[2 * n:2 * n + 2])
        for send, _ in copies:
            send.start()
        for send, recv in copies:
            send.wait_send()
            recv.wait_recv()

    outs = [_hbm_out(h.shape, h.dtype) for h in halves]
    return _comm_call(body, name, halves, outs, n, in_place=True)


def _all_reduce_small(v, deps=()):
    rows = v.shape[0]
    flips = [(fx, fy, fc) for fx in range(2) for fy in range(2) for fc in range(2)][1:]

    def body(v_ref, o_ref, slots, send_sems, recv_sems):
        x, y, c, _ = _place()
        me = 4 * x + 2 * y + c
        slots[me] = v_ref[...]
        sends = []
        for n, (fx, fy, fc) in enumerate(flips):
            to = (x ^ fx, y ^ fy, c ^ fc)
            cp = _remote(v_ref, slots.at[me], (send_sems, recv_sems), n, to)
            cp.start()
            sends.append(cp)
        for n, (fx, fy, fc) in enumerate(flips):
            src = 4 * (x ^ fx) + 2 * (y ^ fy) + (c ^ fc)
            _remote(v_ref, slots.at[src], (send_sems, recv_sems), n, (x ^ fx, y ^ fy, c ^ fc)).wait_recv()
        for cp in sends:
            cp.wait_send()
        acc = slots[0]
        for s in range(1, 8):
            acc = acc + slots[s]
        o_ref[...] = acc

    deps = tuple(deps)
    return pl.pallas_call(
        _skip(len(deps), body), name="all_reduce_small", out_shape=jax.ShapeDtypeStruct(v.shape, F32),
        in_specs=[_ANY] * len(deps) + [pl.BlockSpec(memory_space=pltpu.VMEM)],
        out_specs=pl.BlockSpec(memory_space=pltpu.VMEM),
        scratch_shapes=[pltpu.VMEM((8, rows, LANES), F32), pltpu.SemaphoreType.DMA((7,)), pltpu.SemaphoreType.DMA((7,))],
    )(*deps, v)


def _pair_add(g, sib, core, name):
    _, rows, cols = g.shape
    half = _half_shape(rows, cols)

    def body(c_ref, g_ref, s_ref, o_ref):
        o_ref[0] = (g_ref[0].astype(F32) + s_ref[0].astype(F32)).astype(BF16)

    return pl.pallas_call(
        body, name=name,
        grid_spec=pltpu.PrefetchScalarGridSpec(
            num_scalar_prefetch=1, grid=(N_CHIPS,),
            in_specs=[pl.BlockSpec((1,) + half, lambda s, c_ref: (s, *_half_index(rows, c_ref[0]))),
                      pl.BlockSpec((1,) + half, lambda s, c_ref: (s, 0, 0))],
            out_specs=pl.BlockSpec((1,) + half, lambda s, c_ref: (s, 0, 0))),
        out_shape=jax.ShapeDtypeStruct((N_CHIPS,) + half, BF16),
        compiler_params=_params(dimension_semantics=("arbitrary",)),
    )(core, g, sib)


def _chip_add(part, recv, chip_core, shape, name):
    rows, cols = shape
    half = _half_shape(rows, cols)

    def body(kc_ref, p_ref, r_ref, o_ref):
        acc = p_ref[0].astype(F32)
        for j in range(N_CHIPS - 1):
            acc = acc + r_ref[j].astype(F32)
        o_ref[...] = acc

    return pl.pallas_call(
        body, name=name,
        grid_spec=pltpu.PrefetchScalarGridSpec(
            num_scalar_prefetch=1, grid=(1,),
            in_specs=[pl.BlockSpec((1,) + half, lambda s, kc_ref: (kc_ref[0], 0, 0)),
                      pl.BlockSpec((N_CHIPS - 1,) + half, lambda s, kc_ref: (0, 0, 0))],
            out_specs=pl.BlockSpec(half, lambda s, kc_ref: _half_index(rows, kc_ref[1]))),
        out_shape=jax.ShapeDtypeStruct((rows, cols), F32),
        compiler_params=_params(dimension_semantics=("arbitrary",)),
    )(chip_core, part, recv)


def _adamw_math(w, g, m, v):
    m = ADAM_B1 * m + (1.0 - ADAM_B1) * g
    v = ADAM_B2 * v + (1.0 - ADAM_B2) * (g * g)
    m_hat = m / (1.0 - ADAM_B1 ** ADAM_STEP)
    v_hat = v / (1.0 - ADAM_B2 ** ADAM_STEP)
    delta = -ADAM_LR * (m_hat / (jnp.sqrt(v_hat) + ADAM_EPS) + ADAM_WD * w)
    return delta, m, v


ADAM_PARTS = 4


def _adamw_matrix(w, g, m, v, name):
    rows, cols = w.shape
    by_rows = rows % (8 * ADAM_PARTS) == 0
    block = (rows // ADAM_PARTS, cols) if by_rows else (rows, cols // ADAM_PARTS)

    def body(w_ref, g_ref, m_ref, v_ref, go_ref, d_ref, mo_ref, vo_ref):
        gv = g_ref[...]
        go_ref[...] = gv
        d_ref[...], mo_ref[...], vo_ref[...] = _adamw_math(w_ref[...], gv, m_ref[...], v_ref[...])

    spec = pl.BlockSpec(block, lambda i: (i, 0) if by_rows else (0, i))
    shape = jax.ShapeDtypeStruct((rows, cols), F32)
    return pl.pallas_call(
        body, name=name, grid=(ADAM_PARTS,), in_specs=[spec] * 4, out_specs=[spec] * 4, out_shape=[shape] * 4,
        compiler_params=_params(dimension_semantics=("arbitrary",)),
    )(w, g, m, v)


def _adamw_small(ws, gs, ms, vs):
    n = len(ws)

    def body(*refs):
        for i in range(n):
            w_ref, g_ref, m_ref, v_ref = (refs[k * n + i] for k in range(4))
            d_ref, mo_ref, vo_ref = (refs[(4 + k) * n + i] for k in range(3))
            d_ref[...], mo_ref[...], vo_ref[...] = _adamw_math(w_ref[...], g_ref[...], m_ref[...], v_ref[...])

    shapes = [jax.ShapeDtypeStruct(w.shape, F32) for w in ws]
    out = pl.pallas_call(body, name="adamw_small", out_shape=shapes * 3, compiler_params=_params())(*ws, *gs, *ms, *vs)
    return out[:n], out[n:2 * n], out[2 * n:]


MATRICES = ["ffn1_w13", "ffn1_w2", "w_in", "w_out", "ffn2_w13", "ffn2_w2"]
VECTORS = ["ffn1_norm", "mix_norm", "conv_b", "conv_ln_g", "conv_ln_b", "forget_b", "out_norm_conv",
           "out_norm_attn", "ffn2_norm", "final_norm"]
WEIGHTS = ["ffn1_norm", "ffn1_w13", "ffn1_w2", "mix_norm", "w_in", "conv_w", "conv_b", "conv_ln_g", "conv_ln_b",
           "forget_b", "out_norm_conv", "out_norm_attn", "w_out", "ffn2_norm", "ffn2_w13", "ffn2_w2", "final_norm"]


def _pack_small(g, names):
    rows, layout = [], []
    for n in names:
        flat = g[n].reshape(-1)
        pad = (-flat.shape[0]) % LANES
        rows.append(jnp.pad(flat, (0, pad)).reshape(-1, LANES))
        layout.append((n, g[n].shape, flat.shape[0], rows[-1].shape[0]))
    packed = jnp.concatenate(rows, axis=0)
    pad_rows = (-packed.shape[0]) % 8
    return jnp.pad(packed, ((0, pad_rows), (0, 0))), layout


def _unpack_small(packed, layout):
    out, r = {}, 0
    for n, shape, size, nrows in layout:
        out[n] = packed[r:r + nrows].reshape(-1)[:size].reshape(shape)
        r += nrows
    return out


def kernel(x, ffn1_norm, ffn1_w13, ffn1_w2, mix_norm, w_in, conv_w, conv_b, conv_ln_g, conv_ln_b, forget_b, out_norm_conv, out_norm_attn, w_out, ffn2_norm, ffn2_w13, ffn2_w2, final_norm, loss_target, m_ffn1_norm, m_ffn1_w13, m_ffn1_w2, m_mix_norm, m_w_in, m_conv_w, m_conv_b, m_conv_ln_g, m_conv_ln_b, m_forget_b, m_out_norm_conv, m_out_norm_attn, m_w_out, m_ffn2_norm, m_ffn2_w13, m_ffn2_w2, m_final_norm, v_ffn1_norm, v_ffn1_w13, v_ffn1_w2, v_mix_norm, v_w_in, v_conv_w, v_conv_b, v_conv_ln_g, v_conv_ln_b, v_forget_b, v_out_norm_conv, v_out_norm_attn, v_w_out, v_ffn2_norm, v_ffn2_w13, v_ffn2_w2, v_final_norm):
    args = dict(locals())
    weights = {n: args[n] for n in WEIGHTS}
    core = lax.axis_index("c").astype(jnp.int32).reshape(1)
    chip = (2 * lax.axis_index("x") + lax.axis_index("y")).astype(jnp.int32)
    chip1 = chip.reshape(1)
    chip_core = jnp.concatenate([chip1, core])

    def held(n, a):
        return a[0].T if n == "w_in" else a[0]

    def given(n, a):
        return (a.T if n == "w_in" else a)[None]

    slot = {n: _into_slot(held(n, weights[n]), chip1, BF16, "slot_" + n) for n in MATRICES}
    slot["conv_w"] = _into_slot(jnp.pad(conv_w[0], ((0, CONV_PAD - CONV_WIDTH), (0, 0))), chip1, F32, "slot_conv_w")
    fetched = {"ffn1": ["ffn1_w13", "ffn1_w2"], "mix": ["w_in", "w_out", "conv_w"], "ffn2": ["ffn2_w13", "ffn2_w2"]}
    fetch = {}

    def as_weights(group, bufs):
        out = {}
        for n, b in zip(fetched[group], bufs):
            if n.endswith("w13"):
                out[n] = b
            elif n != "conv_w":
                out[n] = b.reshape(N_CHIPS * b.shape[1], b.shape[2])
            else:
                out[n] = b[:, :CONV_WIDTH].transpose(1, 0, 2).reshape(CONV_WIDTH, D_CONV)
        return out

    def get_weights(group, after):
        if group == "ffn1":
            later_names = fetched["mix"] + fetched["ffn2"]
            bufs = _gather_shards([slot[n] for n in fetched[group]], "gather_ffn1", passed=[slot[n] for n in later_names])
            behind = dict(zip(later_names, bufs[len(fetched[group]):]))
            for later in ("mix", "ffn2"):
                bufs_later = [behind[n] for n in fetched[later]]
                plan = _ici_gather_plan(bufs_later)
                fetch[later] = plan, _split_copy_start("gather_%s_start" % later, bufs_later, 3 * len(bufs_later), plan)
            return as_weights(group, bufs), [fetch["mix"][1][3], fetch["ffn2"][1][3]]
        plan, started = fetch[group.split(":")[0]]
        if group == "ffn2:landed":
            landed = _split_copy_wait("gather_ffn2_wait", started, plan, [after])
            plan = _d2d_forward_plan(landed)
            fetch["ffn2"] = plan, _split_copy_start("forward_ffn2_start", landed, 3 * len(landed), plan)
            return {}, [fetch["ffn2"][1][3]]
        if group == "ffn2":
            return as_weights(group, _split_copy_wait("forward_ffn2_wait", started, plan, [after])), []
        landed = _split_copy_wait("gather_%s_wait" % group, started, plan, [after])
        return as_weights(group, _gather_shards(landed, "forward_" + group, ici=False)), []

    def shard_major(n, g):
        return g if n.endswith("w13") else g.reshape(N_CHIPS, g.shape[0] // N_CHIPS, g.shape[1])

    exchange, scatter = {}, {}
    small_names = VECTORS + ["conv_w"]
    small = {}

    def put_grads(group, grads):
        if group == "small":
            packed, layout = _pack_small(grads, small_names + ["loss"])
            total = _all_reduce_small(packed)
            small.update(_unpack_small(total, layout))
            return [total]
        names = list(grads)
        local = [shard_major(n, grads[n]) for n in names]
        landing = [lax.empty((N_CHIPS,) + _half_shape(*a.shape[1:]), BF16) for a in local]
        plan = _pair_exchange_plan(local)
        exchange[group] = names, plan, _split_copy_start("exchange_%s_start" % group, local + landing, len(local), plan)
        return [exchange[group][2][3]]

    def flush_grads(group, after):
        names, plan, started = exchange[group]
        done = _split_copy_wait("exchange_%s_wait" % group, started, plan, after)
        local, sib = done[:len(names)], done[len(names):]
        parts = [_pair_add(a, b, core, "pair_add_" + n) for a, b, n in zip(local, sib, names)]
        landing = [lax.empty((N_CHIPS - 1,) + q.shape[1:], BF16) for q in parts]
        plan = _ici_scatter_plan(len(parts))
        shapes = [a.shape[1:] for a in local]
        scatter[group] = names, plan, _split_copy_start("scatter_%s_start" % group, parts + landing, 3 * len(parts), plan), shapes
        return [scatter[group][2][3]]

    p = {n: weights[n] for n in VECTORS}
    p["final_norm"] = final_norm.reshape(1, D_MODEL)
    dx = _local_step(x[0], loss_target[0], p, get_weights, put_grads, flush_grads)
    loss = small["loss"].reshape(())

    grad = {n: small[n] for n in VECTORS}
    grad["final_norm"] = small["final_norm"].reshape(D_MODEL)
    grad["conv_w"] = lax.dynamic_slice_in_dim(small["conv_w"], chip * (D_CONV // N_CHIPS), D_CONV // N_CHIPS, axis=1)[None]

    delta, new_m, new_v = {}, {}, {}

    def reduce_chips(group, after):
        names, plan, started, shapes = scatter[group]
        done = _split_copy_wait("scatter_%s_wait" % group, started, plan, after)
        parts, landed = done[:len(names)], done[len(names):]
        return [_chip_add(a, b, chip_core, s, "chip_add_" + n) for a, b, s, n in zip(parts, landed, shapes, names)]

    def update(group, full):
        ends = []
        for n, reduced in zip(scatter[group][0], full):
            go, d, mo, vo = _adamw_matrix(held(n, weights[n]), reduced, held(n, args["m_" + n]), held(n, args["v_" + n]),
                                          "adamw_" + n)
            grad[n], delta[n], new_m[n], new_v[n] = given(n, go), given(n, d), given(n, mo), given(n, vo)
            ends.append(vo)
        return ends

    def share_start(group, halves):
        plan = _pair_share_plan(scatter[group][3])
        return plan, _split_copy_start("share_%s_start" % group, halves, len(halves), plan)

    halves_ffn2 = reduce_chips("ffn2", [exchange["ffn1_w13"][2][3]])
    plan_ffn2, share_ffn2 = share_start("ffn2", halves_ffn2)
    last_scatter = flush_grads("ffn1_w13", [share_ffn2[3]])
    halves_mix = reduce_chips("mix", last_scatter)
    plan_mix, share_mix = share_start("mix", halves_mix)
    done_ffn2 = update("ffn2", _split_copy_wait("share_ffn2_wait", share_ffn2, plan_ffn2, [share_mix[3]]))
    done_mix = update("mix", _split_copy_wait("share_mix_wait", share_mix, plan_mix, done_ffn2))
    as2d = lambda a: a.reshape(-1, a.shape[-1])
    ds, mos, vos = _adamw_small([as2d(weights[n]) for n in small_names], [as2d(grad[n]) for n in small_names],
                                [as2d(args["m_" + n]) for n in small_names], [as2d(args["v_" + n]) for n in small_names])
    for n, d, mo, vo in zip(small_names, ds, mos, vos):
        shape = weights[n].shape
        delta[n], new_m[n], new_v[n] = d.reshape(shape), mo.reshape(shape), vo.reshape(shape)
    behind = done_ffn2 + done_mix + [vos[0]]
    halves_w2 = reduce_chips("ffn1_w2", behind)
    halves_w13 = reduce_chips("ffn1_w13", behind)
    full_w2, full_w13 = _pair_share(halves_w2 + halves_w13, "pair_share_ffn1")
    update("ffn1_w2", [full_w2])
    update("ffn1_w13", [full_w13])

    return (loss, dx[None], *[grad[n] for n in WEIGHTS], *[delta[n] for n in WEIGHTS],
            *[new_m[n] for n in WEIGHTS], *[new_v[n] for n in WEIGHTS])
```

```python
import functools

import jax
import jax.numpy as jnp
from jax import lax
from jax.experimental import pallas as pl
from jax.experimental.pallas import tpu as pltpu

F32 = jnp.float32
BF16 = jnp.bfloat16

D_MODEL = 1024
D_FF = 2816
FF_SHARD = D_FF // 2
D_CONV = 512
D_ATTN = 512
N_HEADS = 8
HEAD_DIM = 64
CONV_WIDTH = 31
CONV_PAD = 32
N_IN = 2 * D_CONV + 3 * D_ATTN + N_HEADS
EPS = 1e-6
N_CHIPS = 4
LANES = 128
HEAD_ROWS = 16

ADAM_LR = 0.001
ADAM_B1 = 0.9
ADAM_B2 = 0.999
ADAM_EPS = 1e-08
ADAM_WD = 0.01
ADAM_STEP = 10

VMEM_LIMIT = 56 * 1024 * 1024

_NT = (((1,), (1,)), ((), ()))
_TN = (((0,), (0,)), ((), ()))


def _dot(a, b):
    return jnp.dot(a, b, preferred_element_type=F32)


def _dot_nt(a, b):
    return lax.dot_general(a, b, _NT, preferred_element_type=F32)


def _dot_tn(a, b):
    return lax.dot_general(a, b, _TN, preferred_element_type=F32)


def _params(**kw):
    return pltpu.CompilerParams(vmem_limit_bytes=VMEM_LIMIT, **kw)


def _sigmoid(x):
    return 1.0 / (1.0 + jnp.exp(-x))


def _rms_stats(x):
    return lax.rsqrt(jnp.mean(x * x, axis=-1, keepdims=True) + EPS)


def _rms_bwd(x, r, g, dh):
    t = dh * g
    dx = r * t - x * (r * r * r) * jnp.mean(t * x, axis=-1, keepdims=True)
    return dx, dh * x * r


def _silu_grad(z, sg):
    return sg * (1.0 + z * (1.0 - sg))


def _row_spec(tm, n):
    return pl.BlockSpec((tm, n), lambda i: (i, 0))


def _full_spec(shape):
    nd = len(shape)
    return pl.BlockSpec(shape, lambda i: (0,) * nd)


_ANY = pl.BlockSpec(memory_space=pl.ANY)


def _skip(n, body):
    return lambda *refs: body(*refs[n:])


FFN_ROWS = 256
FFN_WEIGHT_PARTS = N_CHIPS + 2


def _with_ffn_weights(w13_hbm, w2_hbm, w13_ref, w2_ref, sems, order, tile):
    first = pl.program_id(0) == 0
    copies = {("w13", k): pltpu.make_async_copy(w13_hbm.at[k], w13_ref.at[k], sems.at[k]) for k in range(N_CHIPS)}
    for half in range(2):
        rows = pl.ds(half * FF_SHARD, FF_SHARD)
        copies["w2", half] = pltpu.make_async_copy(w2_hbm.at[rows, :], w2_ref.at[rows, :], sems.at[N_CHIPS + half])

    @pl.when(first)
    def _():
        for part in order:
            copies[part].start()

        def ready(*parts):
            for part in parts:
                copies[part].wait()

        tile(ready)

    @pl.when(jnp.logical_not(first))
    def _():
        tile(lambda *parts: None)


def _ffn_fwd(x, g, w13s, w2, name, deps=()):
    t = x.shape[0]
    tm = FFN_ROWS
    deps = tuple(deps)

    def body(x_ref, g_ref, w13_hbm, w2_hbm, xo_ref, h_ref, gu_ref, w13_ref, w2_ref, sems):
        def tile(ready):
            xv = x_ref[...]
            hb = (xv * _rms_stats(xv) * g_ref[...]).astype(BF16)
            h_ref[...] = hb
            acc = jnp.zeros((tm, D_MODEL), F32)
            for half in range(2):
                lo = half * FF_SHARD
                ready(("w13", half), ("w13", 2 + half))
                gate = _dot(hb, w13_ref[half])
                up = _dot(hb, w13_ref[2 + half])
                gu_ref[:, lo:lo + FF_SHARD] = gate.astype(BF16)
                gu_ref[:, D_FF + lo:D_FF + lo + FF_SHARD] = up.astype(BF16)
                a = (gate * _sigmoid(gate) * up).astype(BF16)
                ready(("w2", half))
                acc = acc + _dot(a, w2_ref[lo:lo + FF_SHARD, :])
            xo_ref[...] = xv + 0.5 * acc

        _with_ffn_weights(w13_hbm, w2_hbm, w13_ref, w2_ref, sems,
                          [("w13", 0), ("w13", 2), ("w2", 0), ("w13", 1), ("w13", 3), ("w2", 1)], tile)

    return pl.pallas_call(
        _skip(len(deps), body), name=name, grid=(t // tm,),
        in_specs=[_ANY] * len(deps) + [_row_spec(tm, D_MODEL), _full_spec((1, D_MODEL)), _ANY, _ANY],
        out_specs=[_row_spec(tm, D_MODEL), _row_spec(tm, D_MODEL), _row_spec(tm, 2 * D_FF)],
        out_shape=[jax.ShapeDtypeStruct((t, D_MODEL), F32), jax.ShapeDtypeStruct((t, D_MODEL), BF16),
                   jax.ShapeDtypeStruct((t, 2 * D_FF), BF16)],
        scratch_shapes=[pltpu.VMEM(w13s.shape, BF16), pltpu.VMEM(w2.shape, BF16),
                        pltpu.SemaphoreType.DMA((FFN_WEIGHT_PARTS,))],
        compiler_params=_params(dimension_semantics=("arbitrary",)),
    )(*deps, x, g, w13s, w2)


def _ffn_bwd(dy, x, gu, g, w13s, w2, name, deps=()):
    t = x.shape[0]
    tm = FFN_ROWS
    deps = tuple(deps)

    def body(dy_ref, x_ref, gu_ref, g_ref, w13_hbm, w2_hbm, dx_ref, dgu_ref, a_ref, dg_ref, dyh_ref, dxb_ref,
             w13_ref, w2_ref, sems):
        @pl.when(pl.program_id(0) == 0)
        def _():
            dg_ref[...] = jnp.zeros_like(dg_ref)

        def tile(ready):
            dyv = dy_ref[...]
            dyh = (0.5 * dyv).astype(BF16)
            dyh_ref[...] = dyh
            dh = jnp.zeros((tm, D_MODEL), F32)
            for half in range(2):
                lo = half * FF_SHARD
                ready(("w2", half))
                da = _dot_nt(dyh, w2_ref[lo:lo + FF_SHARD, :])
                gate = gu_ref[:, lo:lo + FF_SHARD].astype(F32)
                up = gu_ref[:, D_FF + lo:D_FF + lo + FF_SHARD].astype(F32)
                sg = _sigmoid(gate)
                act = gate * sg
                a_ref[:, lo:lo + FF_SHARD] = (act * up).astype(BF16)
                dgate = (da * up * _silu_grad(gate, sg)).astype(BF16)
                dup = (da * act).astype(BF16)
                dgu_ref[:, lo:lo + FF_SHARD] = dgate
                dgu_ref[:, D_FF + lo:D_FF + lo + FF_SHARD] = dup
                ready(("w13", half), ("w13", 2 + half))
                dh = dh + _dot_nt(dgate, w13_ref[half]) + _dot_nt(dup, w13_ref[2 + half])
            xv = x_ref[...]
            dxn, dg_rows = _rms_bwd(xv, _rms_stats(xv), g_ref[...], dh)
            dx = dyv + dxn
            dx_ref[...] = dx
            dxb_ref[...] = dx.astype(BF16)
            dg_ref[...] += jnp.sum(dg_rows, axis=0, keepdims=True)

        _with_ffn_weights(w13_hbm, w2_hbm, w13_ref, w2_ref, sems,
                          [("w2", 0), ("w13", 0), ("w13", 2), ("w2", 1), ("w13", 1), ("w13", 3)], tile)

    return pl.pallas_call(
        _skip(len(deps), body), name=name, grid=(t // tm,),
        in_specs=[_ANY] * len(deps) + [_row_spec(tm, D_MODEL), _row_spec(tm, D_MODEL), _row_spec(tm, 2 * D_FF),
                                       _full_spec((1, D_MODEL)), _ANY, _ANY],
        out_specs=[_row_spec(tm, D_MODEL), _row_spec(tm, 2 * D_FF), _row_spec(tm, D_FF),
                   _full_spec((1, D_MODEL)), _row_spec(tm, D_MODEL), _row_spec(tm, D_MODEL)],
        out_shape=[jax.ShapeDtypeStruct((t, D_MODEL), F32), jax.ShapeDtypeStruct((t, 2 * D_FF), BF16),
                   jax.ShapeDtypeStruct((t, D_FF), BF16), jax.ShapeDtypeStruct((1, D_MODEL), F32),
                   jax.ShapeDtypeStruct((t, D_MODEL), BF16), jax.ShapeDtypeStruct((t, D_MODEL), BF16)],
        scratch_shapes=[pltpu.VMEM(w13s.shape, BF16), pltpu.VMEM(w2.shape, BF16),
                        pltpu.SemaphoreType.DMA((FFN_WEIGHT_PARTS,))],
        compiler_params=_params(dimension_semantics=("arbitrary",)),
    )(*deps, dy, x, gu, g, w13s, w2)


WGRAD_ROWS = (512, 384, 256)


def _wgrad(a, b, n_blocks, name, deps=()):
    t, m = a.shape
    tm = next(rows for rows in WGRAD_ROWS if m % rows == 0)
    n = b.shape[1]
    bn = n // n_blocks
    deps = tuple(deps)
    assert a.dtype == BF16 and b.dtype == BF16

    def body(a_ref, b_ref, o_ref):
        o_ref[0] = _dot_tn(a_ref[...], b_ref[...]).astype(BF16)

    return pl.pallas_call(
        _skip(len(deps), body), name=name, grid=(n_blocks, m // tm),
        in_specs=[_ANY] * len(deps) + [pl.BlockSpec((t, tm), lambda j, i: (0, i)),
                                       pl.BlockSpec((t, bn), lambda j, i: (0, j))],
        out_specs=pl.BlockSpec((1, tm, bn), lambda j, i: (j, i, 0)),
        out_shape=jax.ShapeDtypeStruct((n_blocks, m, bn), BF16),
        compiler_params=_params(dimension_semantics=("arbitrary", "arbitrary")),
    )(*deps, a, b)


def _mix_proj(x, g, w_ag, w_qkv, w_f):
    t = x.shape[0]
    tm = 256

    def body(x_ref, g_ref, wag_ref, wqkv_ref, wf_ref, h_ref, ag_ref, qkv_ref, fl_ref):
        xv = x_ref[...]
        hb = (xv * _rms_stats(xv) * g_ref[...]).astype(BF16)
        h_ref[...] = hb
        ag_ref[...] = _dot_nt(hb, wag_ref[...])
        qkv_ref[...] = _dot_nt(hb, wqkv_ref[...]).astype(BF16)
        fl_ref[...] = _dot_nt(hb, wf_ref[...])

    return pl.pallas_call(
        body, name="mix_proj", grid=(t // tm,),
        in_specs=[_row_spec(tm, D_MODEL), _full_spec((1, D_MODEL)), _full_spec(w_ag.shape),
                  _full_spec(w_qkv.shape), _full_spec(w_f.shape)],
        out_specs=[_row_spec(tm, D_MODEL), _row_spec(tm, 2 * D_CONV), _row_spec(tm, 3 * D_ATTN),
                   _row_spec(tm, LANES)],
        out_shape=[jax.ShapeDtypeStruct((t, D_MODEL), BF16), jax.ShapeDtypeStruct((t, 2 * D_CONV), F32),
                   jax.ShapeDtypeStruct((t, 3 * D_ATTN), BF16), jax.ShapeDtypeStruct((t, LANES), F32)],
        compiler_params=_params(dimension_semantics=("arbitrary",)),
    )(x, g, w_ag, w_qkv, w_f)


def _mix_proj_bwd(dproj, dx2, x1, g, w_ag, w_qkv, w_f):
    t = x1.shape[0]
    tm = 256
    n_ag, n_qkv = 2 * D_CONV, 3 * D_ATTN

    def body(dp_ref, dx2_ref, x_ref, g_ref, wag_ref, wqkv_ref, wf_ref, dx_ref, dg_ref):
        @pl.when(pl.program_id(0) == 0)
        def _():
            dg_ref[...] = jnp.zeros_like(dg_ref)

        dh = (_dot(dp_ref[:, 0:n_ag], wag_ref[...]) + _dot(dp_ref[:, n_ag:n_ag + n_qkv], wqkv_ref[...])
              + _dot(dp_ref[:, n_ag + n_qkv:], wf_ref[...]))
        xv = x_ref[...]
        dxn, dg_rows = _rms_bwd(xv, _rms_stats(xv), g_ref[...], dh)
        dx_ref[...] = dx2_ref[...] + dxn
        dg_ref[...] += jnp.sum(dg_rows, axis=0, keepdims=True)

    return pl.pallas_call(
        body, name="mix_proj_bwd", grid=(t // tm,),
        in_specs=[_row_spec(tm, dproj.shape[1]),
                  _row_spec(tm, D_MODEL), _row_spec(tm, D_MODEL), _full_spec((1, D_MODEL)),
                  _full_spec(w_ag.shape), _full_spec(w_qkv.shape), _full_spec(w_f.shape)],
        out_specs=[_row_spec(tm, D_MODEL), _full_spec((1, D_MODEL))],
        out_shape=[jax.ShapeDtypeStruct((t, D_MODEL), F32), jax.ShapeDtypeStruct((1, D_MODEL), F32)],
        compiler_params=_params(dimension_semantics=("arbitrary",)),
    )(dproj, dx2, x1, g, w_ag, w_qkv, w_f)


def _split3(x):
    hi = x.astype(BF16)
    r1 = x - hi.astype(F32)
    mid = r1.astype(BF16)
    lo = (r1 - mid.astype(F32)).astype(BF16)
    return hi, mid, lo


def _gates_fwd(flt, fb):
    t = flt.shape[1]

    def body(f_ref, b_ref, d_ref):
        z = f_ref[...] + b_ref[...]
        logf = jnp.minimum(z, 0.0) - jnp.log(1.0 + jnp.exp(-jnp.abs(z)))
        row = lax.broadcasted_iota(jnp.int32, (LANES, LANES), 0)
        col = lax.broadcasted_iota(jnp.int32, (LANES, LANES), 1)
        upper = (row <= col).astype(BF16)
        carry = jnp.zeros((HEAD_ROWS, 1), F32)
        for blk in range(t // LANES):
            hi, mid, lo = _split3(logf[:, blk * LANES:(blk + 1) * LANES])
            cs = _dot(hi, upper) + _dot(mid, upper) + _dot(lo, upper)
            d_ref[:, blk * LANES:(blk + 1) * LANES] = cs + carry
            carry = carry + cs[:, LANES - 1:LANES]

    return pl.pallas_call(
        body, name="gates_fwd", out_shape=jax.ShapeDtypeStruct((HEAD_ROWS, t), F32),
        compiler_params=_params(),
    )(flt, fb)


def _gates_bwd(dd, flt, fb):
    t = flt.shape[1]

    def body(dd_ref, f_ref, b_ref, df_ref, db_ref):
        z = f_ref[...] + b_ref[...]
        row = lax.broadcasted_iota(jnp.int32, (LANES, LANES), 0)
        col = lax.broadcasted_iota(jnp.int32, (LANES, LANES), 1)
        lower = (row >= col).astype(BF16)
        carry = jnp.zeros((HEAD_ROWS, 1), F32)
        db = jnp.zeros((HEAD_ROWS, 1), F32)
        for blk in reversed(range(t // LANES)):
            sl = slice(blk * LANES, (blk + 1) * LANES)
            hi, mid, lo = _split3(dd_ref[:, sl])
            cs = _dot(hi, lower) + _dot(mid, lower) + _dot(lo, lower)
            dz = (cs + carry) * _sigmoid(-z[:, sl])
            df_ref[:, sl] = dz
            db = db + jnp.sum(dz, axis=1, keepdims=True)
            carry = carry + cs[:, 0:1]
        db_ref[...] = db

    return pl.pallas_call(
        body, name="gates_bwd",
        out_shape=[jax.ShapeDtypeStruct((HEAD_ROWS, t), F32), jax.ShapeDtypeStruct((HEAD_ROWS, 1), F32)],
        compiler_params=_params(),
    )(dd, flt, fb)


CONV_CHUNK = 128
CONV_TAIL = 16
CONV_WINDOW = CONV_CHUNK + CONV_PAD + 8
CONV_ROWS_EXTRA = CONV_PAD + CONV_TAIL
SUBLANES = 8


def _conv_rows(ag_ref, u_ref, t):
    u_ref[0:CONV_PAD, :] = jnp.zeros((CONV_PAD, D_CONV), F32)
    u_ref[CONV_PAD + t:CONV_ROWS_EXTRA + t, :] = jnp.zeros((CONV_TAIL, D_CONV), F32)

    def fill(i, c):
        r0 = pl.multiple_of(i * CONV_CHUNK, CONV_CHUNK)
        a = ag_ref[pl.ds(r0, CONV_CHUNK), 0:D_CONV]
        gt = ag_ref[pl.ds(r0, CONV_CHUNK), D_CONV:2 * D_CONV]
        u_ref[pl.ds(CONV_PAD + r0, CONV_CHUNK), :] = a * _sigmoid(gt)
        return c

    lax.fori_loop(0, t // CONV_CHUNK, fill, 0)


def _for_shifted(ref, r0, offsets, fn):
    window = ref[pl.ds(r0, CONV_WINDOW), :]
    for rem in range(SUBLANES):
        mine = [o for o in offsets if o % SUBLANES == rem]
        if not mine:
            continue
        turned = window if rem == 0 else pltpu.roll(window, CONV_WINDOW - rem, 0)
        for o in mine:
            fn(o, turned[o - rem:o - rem + CONV_CHUNK])


def _conv_point(u_ref, r0, w_ref, cb, lg, lb):
    acc = [jnp.zeros((CONV_CHUNK, D_CONV), F32)]

    def tap(o, rows):
        j = o - (CONV_PAD - CONV_WIDTH + 1)
        acc[0] = acc[0] + w_ref[j:j + 1, :] * rows

    _for_shifted(u_ref, r0, [j + CONV_PAD - CONV_WIDTH + 1 for j in range(CONV_WIDTH)], tap)
    y = acc[0] + cb
    mu = jnp.mean(y, axis=-1, keepdims=True)
    yc = y - mu
    rstd = lax.rsqrt(jnp.mean(yc * yc, axis=-1, keepdims=True) + EPS)
    yhat = yc * rstd
    z = yhat * lg + lb
    sg = _sigmoid(z)
    s = z * sg
    rr = _rms_stats(s)
    return yhat, rstd, z, sg, s, rr


def _conv_fwd(ag, conv_w, conv_b, ln_g, ln_b, norm_g):
    t = ag.shape[0]

    def body(ag_ref, w_ref, cb_ref, lg_ref, lb_ref, ng_ref, o_ref, u_ref):
        _conv_rows(ag_ref, u_ref, t)
        cb, lg, lb, ng = cb_ref[...], lg_ref[...], lb_ref[...], ng_ref[...]

        def chunk(i, c):
            r0 = pl.multiple_of(i * CONV_CHUNK, CONV_CHUNK)
            _, _, _, _, s, rr = _conv_point(u_ref, r0, w_ref, cb, lg, lb)
            o_ref[pl.ds(r0, CONV_CHUNK), :] = (s * rr * ng).astype(BF16)
            return c

        lax.fori_loop(0, t // CONV_CHUNK, chunk, 0)

    return pl.pallas_call(
        body, name="conv_fwd", out_shape=jax.ShapeDtypeStruct((t, D_CONV), BF16),
        scratch_shapes=[pltpu.VMEM((t + CONV_ROWS_EXTRA, D_CONV), F32)],
        compiler_params=_params(),
    )(ag, conv_w, conv_b, ln_g, ln_b, norm_g)


def _conv_bwd(ag, dout, conv_w, conv_b, ln_g, ln_b, norm_g):
    t = ag.shape[0]

    def body(ag_ref, do_ref, w_ref, cb_ref, lg_ref, lb_ref, ng_ref,
             dag_ref, dw_ref, dcb_ref, dlg_ref, dlb_ref, dng_ref, u_ref, dy_ref):
        _conv_rows(ag_ref, u_ref, t)
        dy_ref[t:t + CONV_ROWS_EXTRA, :] = jnp.zeros((CONV_ROWS_EXTRA, D_CONV), F32)
        cb, lg, lb, ng = cb_ref[...], lg_ref[...], lb_ref[...], ng_ref[...]
        dw_ref[...] = jnp.zeros_like(dw_ref)
        zero = jnp.zeros((1, D_CONV), F32)

        def chunk(i, carry):
            dcb, dlg, dlb, dng = carry
            r0 = pl.multiple_of(i * CONV_CHUNK, CONV_CHUNK)
            yhat, rstd, z, sg, s, rr = _conv_point(u_ref, r0, w_ref, cb, lg, lb)
            do = do_ref[pl.ds(r0, CONV_CHUNK), :]
            ds, dng_rows = _rms_bwd(s, rr, ng, do)
            dz = ds * _silu_grad(z, sg)
            dyhat = dz * lg
            dy = rstd * (dyhat - jnp.mean(dyhat, axis=-1, keepdims=True)
                         - yhat * jnp.mean(dyhat * yhat, axis=-1, keepdims=True))
            dy_ref[pl.ds(r0, CONV_CHUNK), :] = dy
            def tap(o, rows):
                j = o - (CONV_PAD - CONV_WIDTH + 1)
                dw_ref[j:j + 1, :] += jnp.sum(dy * rows, axis=0, keepdims=True)

            _for_shifted(u_ref, r0, [j + CONV_PAD - CONV_WIDTH + 1 for j in range(CONV_WIDTH)], tap)
            return (dcb + jnp.sum(dy, axis=0, keepdims=True), dlg + jnp.sum(dz * yhat, axis=0, keepdims=True),
                    dlb + jnp.sum(dz, axis=0, keepdims=True), dng + jnp.sum(dng_rows, axis=0, keepdims=True))

        dcb, dlg, dlb, dng = lax.fori_loop(0, t // CONV_CHUNK, chunk, (zero, zero, zero, zero))
        dcb_ref[...] = dcb
        dlg_ref[...] = dlg
        dlb_ref[...] = dlb
        dng_ref[...] = dng

        def chunk2(i, c):
            r0 = pl.multiple_of(i * CONV_CHUNK, CONV_CHUNK)
            acc = [jnp.zeros((CONV_CHUNK, D_CONV), F32)]

            def tap(o, rows):
                j = CONV_WIDTH - 1 - o
                acc[0] = acc[0] + w_ref[j:j + 1, :] * rows

            _for_shifted(dy_ref, r0, list(range(CONV_WIDTH)), tap)
            du = acc[0]
            a = ag_ref[pl.ds(r0, CONV_CHUNK), 0:D_CONV]
            gt = ag_ref[pl.ds(r0, CONV_CHUNK), D_CONV:2 * D_CONV]
            sg = _sigmoid(gt)
            dag_ref[pl.ds(r0, CONV_CHUNK), 0:D_CONV] = (du * sg).astype(BF16)
            dag_ref[pl.ds(r0, CONV_CHUNK), D_CONV:2 * D_CONV] = (du * a * sg * (1.0 - sg)).astype(BF16)
            return c

        lax.fori_loop(0, t // CONV_CHUNK, chunk2, 0)

    vec = jax.ShapeDtypeStruct((1, D_CONV), F32)
    return pl.pallas_call(
        body, name="conv_bwd",
        out_shape=[jax.ShapeDtypeStruct((t, 2 * D_CONV), BF16), jax.ShapeDtypeStruct((CONV_PAD, D_CONV), F32),
                   vec, vec, vec, vec],
        scratch_shapes=[pltpu.VMEM((t + CONV_ROWS_EXTRA, D_CONV), F32), pltpu.VMEM((t + CONV_ROWS_EXTRA, D_CONV), F32)],
        compiler_params=_params(),
    )(ag, dout, conv_w, conv_b, ln_g, ln_b, norm_g)


Q_ROWS = 256
ATTN_SCALE = HEAD_DIM ** -0.5
ATTN_AHEAD = 1


def _attn_specs(t):
    blk = lambda off: pl.BlockSpec((t, LANES), lambda p: (0, off + p))
    pairs = N_HEADS // 2
    return [blk(0), blk(pairs), blk(2 * pairs), pl.BlockSpec((2, 1, t), lambda p: (p, 0, 0))]


def _one_head(q2, mask):
    return jnp.where(mask, q2, jnp.zeros_like(q2)) * ATTN_SCALE


def _attn_scores(qs, k2, drow, r0, q1):
    s = _dot_nt(qs, k2) - drow
    rowi = lax.broadcasted_iota(jnp.int32, (q1 - r0, q1 - r0), 0)
    coli = lax.broadcasted_iota(jnp.int32, (q1 - r0, q1 - r0), 1)
    diag = jnp.where(coli <= rowi, s[:, r0:q1], -jnp.inf)
    return diag if r0 == 0 else jnp.concatenate([s[:, :r0], diag], axis=1)


def _attn_fwd(qkv, drow, deps=()):
    t = qkv.shape[0]
    deps = tuple(deps)

    def body(q_ref, k_ref, v_ref, dr_ref, o_ref, lse_ref):
        head_a = lax.broadcasted_iota(jnp.int32, (1, LANES), 1) < HEAD_DIM
        items = [(qb, hh) for qb in range(t // Q_ROWS) for hh in range(2)]

        def scores(item):
            qb, hh = item
            r0, q1 = qb * Q_ROWS, (qb + 1) * Q_ROWS
            qs = _one_head(q_ref[r0:q1, :], head_a if hh == 0 else ~head_a)
            return _attn_scores(qs, k_ref[0:q1, :], dr_ref[hh, :, 0:q1], r0, q1)

        ahead = [scores(item) for item in items[:ATTN_AHEAD]]
        outs = []
        for n, (qb, hh) in enumerate(items):
            r0, q1 = qb * Q_ROWS, (qb + 1) * Q_ROWS
            s = ahead.pop(0)
            if n + ATTN_AHEAD < len(items):
                ahead.append(scores(items[n + ATTN_AHEAD]))
            mx = jnp.max(s, axis=1, keepdims=True)
            p = jnp.exp(s - mx)
            l = jnp.sum(p, axis=1, keepdims=True)
            lse_ref[hh, r0:q1, :] = mx + jnp.log(l)
            outs.append(_dot(p.astype(BF16), v_ref[0:q1, :]) * (1.0 / l))
            if hh == 1:
                o_ref[r0:q1, :] = jnp.where(head_a, outs[0], outs[1])
                outs = []

    pairs = N_HEADS // 2
    return pl.pallas_call(
        _skip(len(deps), body), name="attn_fwd", grid=(pairs,), in_specs=[_ANY] * len(deps) + _attn_specs(t),
        out_specs=[pl.BlockSpec((t, LANES), lambda p: (0, p)), pl.BlockSpec((2, t, 1), lambda p: (p, 0, 0))],
        out_shape=[jax.ShapeDtypeStruct((t, D_ATTN), F32), jax.ShapeDtypeStruct((N_HEADS, t, 1), F32)],
        compiler_params=_params(dimension_semantics=("arbitrary",)),
    )(*deps, qkv, qkv, qkv, drow)


def _attn_bwd(qkv, drow, lse, do):
    t = qkv.shape[0]

    def body(q_ref, k_ref, v_ref, dr_ref, lse_ref, do_ref,
             dq_ref, dk_ref, dv_ref, dd_ref, dk_acc, dv_acc):
        head_a = lax.broadcasted_iota(jnp.int32, (1, LANES), 1) < HEAD_DIM
        dk_acc[...] = jnp.zeros_like(dk_acc)
        dv_acc[...] = jnp.zeros_like(dv_acc)
        dd_ref[...] = jnp.zeros_like(dd_ref)
        items = [(qb, hh) for qb in range(t // Q_ROWS) for hh in range(2)]

        def products(item):
            qb, hh = item
            r0, q1 = qb * Q_ROWS, (qb + 1) * Q_ROWS
            mask = head_a if hh == 0 else ~head_a
            qs = _one_head(q_ref[r0:q1, :], mask)
            dob = jnp.where(mask, do_ref[r0:q1, :], 0.0).astype(BF16)
            s = _attn_scores(qs, k_ref[0:q1, :], dr_ref[hh, :, 0:q1], r0, q1)
            return qs, dob, s, _dot_nt(dob, v_ref[0:q1, :])

        ahead = products(items[0])
        dqs = []
        for n, (qb, hh) in enumerate(items):
            r0, q1 = qb * Q_ROWS, (qb + 1) * Q_ROWS
            qs, dob, s, dp = ahead
            if n + 1 < len(items):
                ahead = products(items[n + 1])
            p = jnp.exp(s - lse_ref[hh, r0:q1, :])
            ds = p * (dp - jnp.sum(p * dp, axis=1, keepdims=True))
            dsb = ds.astype(BF16)
            dqs.append(_dot(dsb, k_ref[0:q1, :]) * ATTN_SCALE)
            dk_acc[0:q1, :] += _dot_tn(dsb, qs)
            dv_acc[0:q1, :] += _dot_tn(p.astype(BF16), dob)
            dd_ref[hh, :, 0:q1] -= jnp.sum(ds, axis=0, keepdims=True)
            if hh == 1:
                dq_ref[r0:q1, :] = jnp.where(head_a, dqs[0], dqs[1]).astype(BF16)
                dqs = []
        dk_ref[...] = dk_acc[...].astype(BF16)
        dv_ref[...] = dv_acc[...].astype(BF16)

    pairs = N_HEADS // 2
    col = pl.BlockSpec((t, LANES), lambda p: (0, p))
    grad = jax.ShapeDtypeStruct((t, D_ATTN), BF16)
    return pl.pallas_call(
        body, name="attn_bwd", grid=(pairs,),
        in_specs=_attn_specs(t) + [pl.BlockSpec((2, t, 1), lambda p: (p, 0, 0)), col],
        out_specs=[col, col, col, pl.BlockSpec((2, 1, t), lambda p: (p, 0, 0))],
        out_shape=[grad, grad, grad, jax.ShapeDtypeStruct((N_HEADS, 1, t), F32)],
        scratch_shapes=[pltpu.VMEM((t, LANES), F32), pltpu.VMEM((t, LANES), F32)],
        compiler_params=_params(dimension_semantics=("arbitrary",)),
    )(qkv, qkv, qkv, drow, lse, do)


def _out_proj(ycn, o, g_attn, w_out, x1, deps=()):
    t = x1.shape[0]
    tm = 256
    deps = tuple(deps)

    def body(yc_ref, o_ref, g_ref, w_ref, x_ref, xo_ref, ya_ref):
        ov = o_ref[...]
        ya = (ov * _rms_stats(ov) * g_ref[...]).astype(BF16)
        ya_ref[...] = ya
        xo_ref[...] = x_ref[...] + _dot(yc_ref[...], w_ref[0:D_CONV, :]) + _dot(ya, w_ref[D_CONV:, :])

    return pl.pallas_call(
        _skip(len(deps), body), name="out_proj", grid=(t // tm,),
        in_specs=[_ANY] * len(deps) + [_row_spec(tm, D_CONV), _row_spec(tm, D_ATTN), _full_spec((1, D_ATTN)),
                                       _full_spec(w_out.shape), _row_spec(tm, D_MODEL)],
        out_specs=[_row_spec(tm, D_MODEL), _row_spec(tm, D_ATTN)],
        out_shape=[jax.ShapeDtypeStruct((t, D_MODEL), F32), jax.ShapeDtypeStruct((t, D_ATTN), BF16)],
        compiler_params=_params(dimension_semantics=("arbitrary",)),
    )(*deps, ycn, o, g_attn, w_out, x1)


def _out_proj_bwd(dx2, o, g_attn, w_out, deps=()):
    t = dx2.shape[0]
    tm = 256
    deps = tuple(deps)

    def body(dx_ref, o_ref, g_ref, w_ref, dyc_ref, do_ref, dg_ref):
        @pl.when(pl.program_id(0) == 0)
        def _():
            dg_ref[...] = jnp.zeros_like(dg_ref)

        dxb = dx_ref[...]
        dyc_ref[...] = _dot_nt(dxb, w_ref[0:D_CONV, :])
        dya = _dot_nt(dxb, w_ref[D_CONV:, :])
        ov = o_ref[...]
        do, dg_rows = _rms_bwd(ov, _rms_stats(ov), g_ref[...], dya)
        do_ref[...] = do
        dg_ref[...] += jnp.sum(dg_rows, axis=0, keepdims=True)

    return pl.pallas_call(
        _skip(len(deps), body), name="out_proj_bwd", grid=(t // tm,),
        in_specs=[_ANY] * len(deps) + [_row_spec(tm, D_MODEL), _row_spec(tm, D_ATTN), _full_spec((1, D_ATTN)),
                                       _full_spec(w_out.shape)],
        out_specs=[_row_spec(tm, D_CONV), _row_spec(tm, D_ATTN), _full_spec((1, D_ATTN))],
        out_shape=[jax.ShapeDtypeStruct((t, D_CONV), F32), jax.ShapeDtypeStruct((t, D_ATTN), F32),
                   jax.ShapeDtypeStruct((1, D_ATTN), F32)],
        compiler_params=_params(dimension_semantics=("arbitrary",)),
    )(*deps, dx2, o, g_attn, w_out)


def _loss_bwd(x3, target, g):
    t = x3.shape[0]
    tm = 256

    def body(x_ref, t_ref, g_ref, loss_ref, dx_ref, dg_ref):
        @pl.when(pl.program_id(0) == 0)
        def _():
            loss_ref[...] = jnp.zeros_like(loss_ref)
            dg_ref[...] = jnp.zeros_like(dg_ref)

        xv = x_ref[...]
        r = _rms_stats(xv)
        gv = g_ref[...]
        err = xv * r * gv - t_ref[...]
        row = jnp.sum(err * err, axis=1, keepdims=True) * (0.5 / D_MODEL)
        loss_ref[...] += jnp.sum(row, axis=0, keepdims=True)
        dx, dg_rows = _rms_bwd(xv, r, gv, err * (1.0 / D_MODEL))
        dx_ref[...] = dx
        dg_ref[...] += jnp.sum(dg_rows, axis=0, keepdims=True)

    return pl.pallas_call(
        body, name="loss_bwd", grid=(t // tm,),
        in_specs=[_row_spec(tm, D_MODEL), _row_spec(tm, D_MODEL), _full_spec((1, D_MODEL))],
        out_specs=[_full_spec((1, LANES)), _row_spec(tm, D_MODEL), _full_spec((1, D_MODEL))],
        out_shape=[jax.ShapeDtypeStruct((1, LANES), F32), jax.ShapeDtypeStruct((t, D_MODEL), F32),
                   jax.ShapeDtypeStruct((1, D_MODEL), F32)],
        compiler_params=_params(dimension_semantics=("arbitrary",)),
    )(x3, target, g)


def _split_w_in(w_in_t):
    w_ag = w_in_t[:2 * D_CONV]
    w_qkv = w_in_t[2 * D_CONV:2 * D_CONV + 3 * D_ATTN]
    w_f = jnp.pad(w_in_t[2 * D_CONV + 3 * D_ATTN:], ((0, LANES - N_HEADS), (0, 0)))
    return w_ag, w_qkv, w_f


def _head_rows(v):
    return jnp.pad(v, ((0, HEAD_ROWS - N_HEADS),) + ((0, 0),) * (v.ndim - 1))


def _local_step(x, target, p, get_weights, put_grads, flush_grads):
    t = x.shape[0]
    fb = _head_rows(p["forget_b"].reshape(N_HEADS, 1))

    w, deps = get_weights("ffn1", None)
    x1, h1, gu1 = _ffn_fwd(x, p["ffn1_norm"], w["ffn1_w13"], w["ffn1_w2"], "ffn1_fwd", deps)
    wm, _ = get_weights("mix", x1)
    w.update(wm)
    w_ag, w_qkv, w_f = _split_w_in(w["w_in"])
    conv_w = jnp.pad(w["conv_w"], ((0, CONV_PAD - CONV_WIDTH), (0, 0)))
    h2, ag, qkv, fl = _mix_proj(x1, p["mix_norm"], w_ag, w_qkv, w_f)
    flt = _head_rows(fl[:, :N_HEADS].T)
    dcum = _gates_fwd(flt, fb)[:N_HEADS]
    drow = dcum.reshape(N_HEADS, 1, t)
    ycn = _conv_fwd(ag, conv_w, p["conv_b"], p["conv_ln_g"], p["conv_ln_b"], p["out_norm_conv"])
    o, lse = _attn_fwd(qkv, drow, [ycn])
    _, deps = get_weights("ffn2:landed", o)
    x2, yan = _out_proj(ycn, o, p["out_norm_attn"], w["w_out"], x1, deps)
    w2, _ = get_weights("ffn2", x2)
    w.update(w2)
    x3, h3, gu2 = _ffn_fwd(x2, p["ffn2_norm"], w["ffn2_w13"], w["ffn2_w2"], "ffn2_fwd")
    loss, dx3, d_final = _loss_bwd(x3, target, p["final_norm"])

    g = {}
    dx2, dgu2, a2, g["ffn2_norm"], dx3_half, dx2_bf16 = _ffn_bwd(
        dx3, x2, gu2, p["ffn2_norm"], w["ffn2_w13"], w["ffn2_w2"], "ffn2_bwd")
    dw13 = _wgrad(h3, dgu2, N_CHIPS, "ffn2_dw13")
    dw2 = _wgrad(a2, dx3_half, 1, "ffn2_dw2").reshape(D_FF, D_MODEL)
    deps = put_grads("ffn2", {"ffn2_w13": dw13, "ffn2_w2": dw2})
    dyc, do, g["out_norm_attn"] = _out_proj_bwd(dx2_bf16, o, p["out_norm_attn"], w["w_out"], deps)
    deps = flush_grads("ffn2", [dyc])
    dw_out = _wgrad(jnp.concatenate([ycn, yan], axis=1), dx2_bf16, 1, "dw_out", deps).reshape(D_MODEL, D_MODEL)
    dq, dk, dv, ddrow = _attn_bwd(qkv, drow, lse, do)
    dflt, dfb = _gates_bwd(_head_rows(ddrow.reshape(N_HEADS, t)), flt, fb)
    g["forget_b"] = dfb[:N_HEADS, 0].reshape(1, N_HEADS)
    dfl = jnp.pad(dflt[:N_HEADS].T, ((0, 0), (0, LANES - N_HEADS)))
    dag, dconv_w, g["conv_b"], g["conv_ln_g"], g["conv_ln_b"], g["out_norm_conv"] = _conv_bwd(
        ag, dyc, conv_w, p["conv_b"], p["conv_ln_g"], p["conv_ln_b"], p["out_norm_conv"])
    g["conv_w"] = dconv_w[:CONV_WIDTH]
    dproj = jnp.concatenate([dag, dq, dk, dv, dfl.astype(BF16)], axis=1)
    dx1, g["mix_norm"] = _mix_proj_bwd(dproj, dx2, x1, p["mix_norm"], w_ag, w_qkv, w_f)
    dw_in = _wgrad(dproj, h2, 1, "dw_in").reshape(dproj.shape[1], D_MODEL)[:N_IN]
    deps = put_grads("mix", {"w_in": dw_in, "w_out": dw_out})
    dx0, dgu1, a1, g["ffn1_norm"], dx1_half, _ = _ffn_bwd(
        dx1, x, gu1, p["ffn1_norm"], w["ffn1_w13"], w["ffn1_w2"], "ffn1_bwd", deps)
    g["final_norm"] = d_final
    g["loss"] = loss[:, :1]
    deps = flush_grads("mix", put_grads("small", g))
    dw2 = _wgrad(a1, dx1_half, 1, "ffn1_dw2", deps).reshape(D_FF, D_MODEL)
    deps = flush_grads("ffn1_w2", put_grads("ffn1_w2", {"ffn1_w2": dw2}))
    dw13 = _wgrad(h1, dgu1, N_CHIPS, "ffn1_dw13", deps)
    put_grads("ffn1_w13", {"ffn1_w13": dw13})
    return dx0


MESH = pl.DeviceIdType.MESH


def _place():
    x, y, c = lax.axis_index("x"), lax.axis_index("y"), lax.axis_index("c")
    chips = [(1 - x, y), (x, 1 - y), (1 - x, 1 - y)]
    return x, y, c, chips


def _hbm_out(shape, dtype):
    return jax.ShapeDtypeStruct(shape, dtype)


def _comm_call(body, name, ins, out_shapes, n_remote, in_place=False):
    return pl.pallas_call(
        body, name=name, in_specs=[_ANY] * len(ins), out_specs=[_ANY] * len(out_shapes), out_shape=out_shapes,
        scratch_shapes=[pltpu.SemaphoreType.DMA((n_remote,)), pltpu.SemaphoreType.DMA((n_remote,))],
        input_output_aliases={i: i for i in range(len(ins))} if in_place else {},
    )(*ins)


def _remote(src, dst, sems, n, to):
    send_sems, recv_sems = sems
    return pltpu.make_async_remote_copy(src_ref=src, dst_ref=dst, send_sem=send_sems.at[n], recv_sem=recv_sems.at[n],
                                        device_id=to, device_id_type=MESH)


HALF_ROWS_MULTIPLE = 32


def _halved_by_rows(rows):
    return rows % HALF_ROWS_MULTIPLE == 0


def _half_shape(rows, cols):
    return (rows // 2, cols) if _halved_by_rows(rows) else (rows, cols // 2)


def _half_index(rows, core):
    return (core, 0) if _halved_by_rows(rows) else (0, core)


def _half_of(ref, rows, cols, core, *lead):
    if _halved_by_rows(rows):
        return ref.at[(*lead, pl.ds(core * (rows // 2), rows // 2), slice(None))]
    return ref.at[(*lead, slice(None), pl.ds(core * (cols // 2), cols // 2))]


def _into_slot(shard, chip, dtype, name):
    rows, cols = shard.shape
    half = _half_shape(rows, cols)
    by_rows = _halved_by_rows(rows)

    def body(k_ref, s_ref, o_ref):
        o_ref[0] = s_ref[...].astype(dtype)

    return pl.pallas_call(
        body, name=name,
        grid_spec=pltpu.PrefetchScalarGridSpec(
            num_scalar_prefetch=1, grid=(2,),
            in_specs=[pl.BlockSpec(half, lambda i, k_ref: (i, 0) if by_rows else (0, i))],
            out_specs=pl.BlockSpec((1,) + half, lambda i, k_ref: (k_ref[0], i, 0) if by_rows else (k_ref[0], 0, i))),
        out_shape=jax.ShapeDtypeStruct((N_CHIPS, rows, cols), dtype),
        compiler_params=_params(dimension_semantics=("arbitrary",)),
    )(chip, shard)


def _gather_shards(slots, name, ici=True, passed=()):
    n = len(slots)
    slots = list(slots) + list(passed)
    total = len(slots)

    def body(*refs):
        outs = refs[total:total + n]
        sems = refs[2 * total:2 * total + 2]
        x, y, c, chips = _place()
        me = 2 * x + y
        sibling = (x, y, 1 - c)

        def half(i, chip_index, core):
            return _half_of(outs[i], *slots[i].shape[1:], core, chip_index)

        sends = []
        if ici:
            for i in range(n):
                for j, chip in enumerate(chips):
                    cp = _remote(half(i, me, c), half(i, me, c), sems, 6 * i + j, (*chip, c))
                    cp.start()
                    sends.append(cp)
        for i in range(n):
            for j, chip in enumerate(chips):
                src_chip = 2 * chip[0] + chip[1]
                landed = half(i, src_chip, c)
                if ici:
                    _remote(landed, landed, sems, 6 * i + j, (*chip, c)).wait_recv()
                cp = _remote(landed, landed, sems, 6 * i + 3 + j, sibling)
                cp.start()
                sends.append(cp)
        for i in range(n):
            for j, chip in enumerate(chips):
                src_chip = 2 * chip[0] + chip[1]
                landed = half(i, src_chip, 1 - c)
                _remote(landed, landed, sems, 6 * i + 3 + j, sibling).wait_recv()
        for cp in sends:
            cp.wait_send()

    outs = [_hbm_out(s.shape, s.dtype) for s in slots]
    return _comm_call(body, name, slots, outs, 6 * n, in_place=True)


_HBM = pl.BlockSpec(memory_space=pltpu.HBM)
_SEM = pl.BlockSpec(memory_space=pltpu.SEMAPHORE)
_DATAFLOW = pltpu.SideEffectType.DATAFLOW_SIDE_EFFECTING


def _split_copy_start(name, bufs, n_copies, plan):
    n = len(bufs)

    def body(*refs):
        for send, _ in plan(refs[:n], (refs[n], refs[n + 1])):
            send.start()
        token = refs[-1]
        token[...] = jnp.zeros_like(token)

    out = pl.pallas_call(
        body, name=name,
        out_shape=(pltpu.SemaphoreType.DMA((n_copies,)), pltpu.SemaphoreType.DMA((n_copies,)),
                   *[pltpu.HBM(b.shape, b.dtype) for b in bufs], jax.ShapeDtypeStruct((8, LANES), F32)),
        in_specs=[_HBM] * n, out_specs=(_SEM, _SEM, *[_HBM] * n, pl.BlockSpec(memory_space=pltpu.VMEM)),
        input_output_aliases={i: 2 + i for i in range(n)},
        compiler_params=pltpu.CompilerParams(has_side_effects=_DATAFLOW),
    )(*[pltpu.with_memory_space_constraint(b, pltpu.HBM) for b in bufs])
    return out[0], out[1], list(out[2:2 + n]), out[-1]


def _split_copy_wait(name, started, plan, after):
    send_sems, recv_sems, bufs, _ = started
    n = len(bufs)
    after = tuple(after)

    def body(*refs):
        for send, recv in plan(refs[:n], (refs[n], refs[n + 1])):
            send.wait_send()
            recv.wait_recv()

    out = pl.pallas_call(
        body, name=name, out_shape=tuple(pltpu.HBM(b.shape, b.dtype) for b in bufs),
        in_specs=[_HBM] * n + [_SEM, _SEM] + [_ANY] * len(after), out_specs=tuple([_HBM] * n),
        input_output_aliases={i: i for i in range(n)},
        compiler_params=pltpu.CompilerParams(has_side_effects=_DATAFLOW),
    )(*bufs, send_sems, recv_sems, *after)
    return list(out)


def _ici_gather_plan(slots):
    def plan(refs, sems):
        x, y, c, chips = _place()
        me = 2 * x + y
        copies = []
        for i, ref in enumerate(refs):
            for j, chip in enumerate(chips):
                mine = _half_of(ref, *slots[i].shape[1:], c, me)
                theirs = _half_of(ref, *slots[i].shape[1:], c, 2 * chip[0] + chip[1])
                to = (*chip, c)
                copies.append((_remote(mine, mine, sems, 3 * i + j, to), _remote(theirs, theirs, sems, 3 * i + j, to)))
        return copies

    return plan


def _ici_scatter_plan(n):
    def plan(refs, sems):
        x, y, c, chips = _place()
        copies = []
        for i in range(n):
            for j, chip in enumerate(chips):
                cp = _remote(refs[i].at[2 * chip[0] + chip[1]], refs[n + i].at[j], sems, 3 * i + j, (*chip, c))
                copies.append((cp, cp))
        return copies

    return plan


def _d2d_forward_plan(slots):
    def plan(refs, sems):
        x, y, c, chips = _place()
        sibling = (x, y, 1 - c)
        copies = []
        for i, ref in enumerate(refs):
            for j, chip in enumerate(chips):
                src_chip = 2 * chip[0] + chip[1]
                mine = _half_of(ref, *slots[i].shape[1:], c, src_chip)
                theirs = _half_of(ref, *slots[i].shape[1:], 1 - c, src_chip)
                copies.append((_remote(mine, mine, sems, 3 * i + j, sibling),
                               _remote(theirs, theirs, sems, 3 * i + j, sibling)))
        return copies

    return plan


def _pair_exchange_plan(grads):
    n = len(grads)

    def plan(refs, sems):
        x, y, c, _ = _place()
        copies = []
        for i in range(n):
            theirs = _half_of(refs[i], *grads[i].shape[1:], 1 - c, slice(None))
            cp = _remote(theirs, refs[n + i], sems, i, (x, y, 1 - c))
            copies.append((cp, cp))
        return copies

    return plan


def _pair_share_plan(shapes):
    def plan(refs, sems):
        x, y, c, _ = _place()
        sibling = (x, y, 1 - c)
        copies = []
        for i, ref in enumerate(refs):
            mine, theirs = _half_of(ref, *shapes[i], c), _half_of(ref, *shapes[i], 1 - c)
            copies.append((_remote(mine, mine, sems, i, sibling), _remote(theirs, theirs, sems, i, sibling)))
        return copies

    return plan


def _pair_share(halves, name):
    n = len(halves)
    plan = _pair_share_plan([h.shape for h in halves])

    def body(*refs):
        copies = plan(refs[n:2 * n], refs[2 * n:2 * n + 2])
        for send, _ in copies:
            send.start()
        for send, recv in copies:
            send.wait_send()
            recv.wait_recv()

    outs = [_hbm_out(h.shape, h.dtype) for h in halves]
    return _comm_call(body, name, halves, outs, n, in_place=True)


def _all_reduce_small(v, deps=()):
    rows = v.shape[0]
    flips = [(fx, fy, fc) for fx in range(2) for fy in range(2) for fc in range(2)][1:]

    def body(v_ref, o_ref, slots, send_sems, recv_sems):
        x, y, c, _ = _place()
        me = 4 * x + 2 * y + c
        slots[me] = v_ref[...]
        sends = []
        for n, (fx, fy, fc) in enumerate(flips):
            to = (x ^ fx, y ^ fy, c ^ fc)
            cp = _remote(v_ref, slots.at[me], (send_sems, recv_sems), n, to)
            cp.start()
            sends.append(cp)
        for n, (fx, fy, fc) in enumerate(flips):
            src = 4 * (x ^ fx) + 2 * (y ^ fy) + (c ^ fc)
            _remote(v_ref, slots.at[src], (send_sems, recv_sems), n, (x ^ fx, y ^ fy, c ^ fc)).wait_recv()
        for cp in sends:
            cp.wait_send()
        acc = slots[0]
        for s in range(1, 8):
            acc = acc + slots[s]
        o_ref[...] = acc

    deps = tuple(deps)
    return pl.pallas_call(
        _skip(len(deps), body), name="all_reduce_small", out_shape=jax.ShapeDtypeStruct(v.shape, F32),
        in_specs=[_ANY] * len(deps) + [pl.BlockSpec(memory_space=pltpu.VMEM)],
        out_specs=pl.BlockSpec(memory_space=pltpu.VMEM),
        scratch_shapes=[pltpu.VMEM((8, rows, LANES), F32), pltpu.SemaphoreType.DMA((7,)), pltpu.SemaphoreType.DMA((7,))],
    )(*deps, v)


def _pair_add(gs, sibs, core, name):
    n = len(gs)
    halves = [_half_shape(*g.shape[1:]) for g in gs]

    def body(c_ref, *refs):
        for g_ref, s_ref, o_ref in zip(refs[:n], refs[n:2 * n], refs[2 * n:]):
            o_ref[0] = (g_ref[0].astype(F32) + s_ref[0].astype(F32)).astype(BF16)

    def mine(g, half):
        return pl.BlockSpec((1,) + half, lambda s, c_ref: (s, *_half_index(g.shape[1], c_ref[0])))

    whole = [pl.BlockSpec((1,) + half, lambda s, c_ref: (s, 0, 0)) for half in halves]
    return pl.pallas_call(
        body, name=name,
        grid_spec=pltpu.PrefetchScalarGridSpec(
            num_scalar_prefetch=1, grid=(N_CHIPS,),
            in_specs=[mine(g, half) for g, half in zip(gs, halves)] + whole, out_specs=whole),
        out_shape=[jax.ShapeDtypeStruct((N_CHIPS,) + half, BF16) for half in halves],
        compiler_params=_params(dimension_semantics=("arbitrary",)),
    )(core, *gs, *sibs)


def _chip_add(parts, recvs, chip_core, shapes, name):
    n = len(parts)
    halves = [_half_shape(*shape) for shape in shapes]

    def body(kc_ref, *refs):
        for p_ref, r_ref, o_ref in zip(refs[:n], refs[n:2 * n], refs[2 * n:]):
            acc = p_ref[0].astype(F32)
            for j in range(N_CHIPS - 1):
                acc = acc + r_ref[j].astype(F32)
            o_ref[...] = acc

    def out_spec(shape, half):
        return pl.BlockSpec(half, lambda s, kc_ref: _half_index(shape[0], kc_ref[1]))

    return pl.pallas_call(
        body, name=name,
        grid_spec=pltpu.PrefetchScalarGridSpec(
            num_scalar_prefetch=1, grid=(1,),
            in_specs=[pl.BlockSpec((1,) + half, lambda s, kc_ref: (kc_ref[0], 0, 0)) for half in halves]
            + [pl.BlockSpec((N_CHIPS - 1,) + half, lambda s, kc_ref: (0, 0, 0)) for half in halves],
            out_specs=[out_spec(shape, half) for shape, half in zip(shapes, halves)]),
        out_shape=[jax.ShapeDtypeStruct(tuple(shape), F32) for shape in shapes],
        compiler_params=_params(dimension_semantics=("arbitrary",)),
    )(chip_core, *parts, *recvs)


def _adamw_math(w, g, m, v):
    m = ADAM_B1 * m + (1.0 - ADAM_B1) * g
    v = ADAM_B2 * v + (1.0 - ADAM_B2) * (g * g)
    m_hat = m / (1.0 - ADAM_B1 ** ADAM_STEP)
    v_hat = v / (1.0 - ADAM_B2 ** ADAM_STEP)
    delta = -ADAM_LR * (m_hat / (jnp.sqrt(v_hat) + ADAM_EPS) + ADAM_WD * w)
    return delta, m, v


ADAM_PARTS = 4


def _adamw_matrix(w, g, m, v, name):
    rows, cols = w.shape
    by_rows = rows % (8 * ADAM_PARTS) == 0
    block = (rows // ADAM_PARTS, cols) if by_rows else (rows, cols // ADAM_PARTS)

    def body(w_ref, g_ref, m_ref, v_ref, go_ref, d_ref, mo_ref, vo_ref):
        gv = g_ref[...]
        go_ref[...] = gv
        d_ref[...], mo_ref[...], vo_ref[...] = _adamw_math(w_ref[...], gv, m_ref[...], v_ref[...])

    spec = pl.BlockSpec(block, lambda i: (i, 0) if by_rows else (0, i))
    shape = jax.ShapeDtypeStruct((rows, cols), F32)
    return pl.pallas_call(
        body, name=name, grid=(ADAM_PARTS,), in_specs=[spec] * 4, out_specs=[spec] * 4, out_shape=[shape] * 4,
        compiler_params=_params(dimension_semantics=("arbitrary",)),
    )(w, g, m, v)


def _adamw_small(ws, gs, ms, vs):
    n = len(ws)

    def body(*refs):
        for i in range(n):
            w_ref, g_ref, m_ref, v_ref = (refs[k * n + i] for k in range(4))
            d_ref, mo_ref, vo_ref = (refs[(4 + k) * n + i] for k in range(3))
            d_ref[...], mo_ref[...], vo_ref[...] = _adamw_math(w_ref[...], g_ref[...], m_ref[...], v_ref[...])

    shapes = [jax.ShapeDtypeStruct(w.shape, F32) for w in ws]
    out = pl.pallas_call(body, name="adamw_small", out_shape=shapes * 3, compiler_params=_params())(*ws, *gs, *ms, *vs)
    return out[:n], out[n:2 * n], out[2 * n:]


MATRICES = ["ffn1_w13", "ffn1_w2", "w_in", "w_out", "ffn2_w13", "ffn2_w2"]
VECTORS = ["ffn1_norm", "mix_norm", "conv_b", "conv_ln_g", "conv_ln_b", "forget_b", "out_norm_conv",
           "out_norm_attn", "ffn2_norm", "final_norm"]
WEIGHTS = ["ffn1_norm", "ffn1_w13", "ffn1_w2", "mix_norm", "w_in", "conv_w", "conv_b", "conv_ln_g", "conv_ln_b",
           "forget_b", "out_norm_conv", "out_norm_attn", "w_out", "ffn2_norm", "ffn2_w13", "ffn2_w2", "final_norm"]


def _pack_small(g, names):
    rows, layout = [], []
    for n in names:
        flat = g[n].reshape(-1)
        pad = (-flat.shape[0]) % LANES
        rows.append(jnp.pad(flat, (0, pad)).reshape(-1, LANES))
        layout.append((n, g[n].shape, flat.shape[0], rows[-1].shape[0]))
    packed = jnp.concatenate(rows, axis=0)
    pad_rows = (-packed.shape[0]) % 8
    return jnp.pad(packed, ((0, pad_rows), (0, 0))), layout


def _unpack_small(packed, layout):
    out, r = {}, 0
    for n, shape, size, nrows in layout:
        out[n] = packed[r:r + nrows].reshape(-1)[:size].reshape(shape)
        r += nrows
    return out


def kernel(x, ffn1_norm, ffn1_w13, ffn1_w2, mix_norm, w_in, conv_w, conv_b, conv_ln_g, conv_ln_b, forget_b, out_norm_conv, out_norm_attn, w_out, ffn2_norm, ffn2_w13, ffn2_w2, final_norm, loss_target, m_ffn1_norm, m_ffn1_w13, m_ffn1_w2, m_mix_norm, m_w_in, m_conv_w, m_conv_b, m_conv_ln_g, m_conv_ln_b, m_forget_b, m_out_norm_conv, m_out_norm_attn, m_w_out, m_ffn2_norm, m_ffn2_w13, m_ffn2_w2, m_final_norm, v_ffn1_norm, v_ffn1_w13, v_ffn1_w2, v_mix_norm, v_w_in, v_conv_w, v_conv_b, v_conv_ln_g, v_conv_ln_b, v_forget_b, v_out_norm_conv, v_out_norm_attn, v_w_out, v_ffn2_norm, v_ffn2_w13, v_ffn2_w2, v_final_norm):
    args = dict(locals())
    weights = {n: args[n] for n in WEIGHTS}
    core = lax.axis_index("c").astype(jnp.int32).reshape(1)
    chip = (2 * lax.axis_index("x") + lax.axis_index("y")).astype(jnp.int32)
    chip1 = chip.reshape(1)
    chip_core = jnp.concatenate([chip1, core])

    def held(n, a):
        return a[0].T if n == "w_in" else a[0]

    def given(n, a):
        return (a.T if n == "w_in" else a)[None]

    slot = {n: _into_slot(held(n, weights[n]), chip1, BF16, "slot_" + n) for n in MATRICES}
    slot["conv_w"] = _into_slot(jnp.pad(conv_w[0], ((0, CONV_PAD - CONV_WIDTH), (0, 0))), chip1, F32, "slot_conv_w")
    fetched = {"ffn1": ["ffn1_w13", "ffn1_w2"], "mix": ["w_in", "w_out", "conv_w"], "ffn2": ["ffn2_w13", "ffn2_w2"]}
    fetch = {}

    def as_weights(group, bufs):
        out = {}
        for n, b in zip(fetched[group], bufs):
            if n.endswith("w13"):
                out[n] = b
            elif n != "conv_w":
                out[n] = b.reshape(N_CHIPS * b.shape[1], b.shape[2])
            else:
                out[n] = b[:, :CONV_WIDTH].transpose(1, 0, 2).reshape(CONV_WIDTH, D_CONV)
        return out

    def get_weights(group, after):
        if group == "ffn1":
            later_names = fetched["mix"] + fetched["ffn2"]
            bufs = _gather_shards([slot[n] for n in fetched[group]], "gather_ffn1", passed=[slot[n] for n in later_names])
            behind = dict(zip(later_names, bufs[len(fetched[group]):]))
            for later in ("mix", "ffn2"):
                bufs_later = [behind[n] for n in fetched[later]]
                plan = _ici_gather_plan(bufs_later)
                fetch[later] = plan, _split_copy_start("gather_%s_start" % later, bufs_later, 3 * len(bufs_later), plan)
            return as_weights(group, bufs), [fetch["mix"][1][3], fetch["ffn2"][1][3]]
        plan, started = fetch[group.split(":")[0]]
        if group == "ffn2:landed":
            landed = _split_copy_wait("gather_ffn2_wait", started, plan, [after])
            plan = _d2d_forward_plan(landed)
            fetch["ffn2"] = plan, _split_copy_start("forward_ffn2_start", landed, 3 * len(landed), plan)
            return {}, [fetch["ffn2"][1][3]]
        if group == "ffn2":
            return as_weights(group, _split_copy_wait("forward_ffn2_wait", started, plan, [after])), []
        landed = _split_copy_wait("gather_%s_wait" % group, started, plan, [after])
        return as_weights(group, _gather_shards(landed, "forward_" + group, ici=False)), []

    def shard_major(n, g):
        return g if n.endswith("w13") else g.reshape(N_CHIPS, g.shape[0] // N_CHIPS, g.shape[1])

    exchange, scatter = {}, {}
    small_names = VECTORS + ["conv_w"]
    small = {}

    def put_grads(group, grads):
        if group == "small":
            packed, layout = _pack_small(grads, small_names + ["loss"])
            total = _all_reduce_small(packed)
            small.update(_unpack_small(total, layout))
            return [total]
        names = list(grads)
        local = [shard_major(n, grads[n]) for n in names]
        landing = [lax.empty((N_CHIPS,) + _half_shape(*a.shape[1:]), BF16) for a in local]
        plan = _pair_exchange_plan(local)
        exchange[group] = names, plan, _split_copy_start("exchange_%s_start" % group, local + landing, len(local), plan)
        return [exchange[group][2][3]]

    def flush_grads(group, after):
        names, plan, started = exchange[group]
        done = _split_copy_wait("exchange_%s_wait" % group, started, plan, after)
        local, sib = done[:len(names)], done[len(names):]
        parts = list(_pair_add(local, sib, core, "pair_add_" + group))
        landing = [lax.empty((N_CHIPS - 1,) + q.shape[1:], BF16) for q in parts]
        plan = _ici_scatter_plan(len(parts))
        shapes = [a.shape[1:] for a in local]
        scatter[group] = names, plan, _split_copy_start("scatter_%s_start" % group, parts + landing, 3 * len(parts), plan), shapes
        return [scatter[group][2][3]]

    p = {n: weights[n] for n in VECTORS}
    p["final_norm"] = final_norm.reshape(1, D_MODEL)
    dx = _local_step(x[0], loss_target[0], p, get_weights, put_grads, flush_grads)
    loss = small["loss"].reshape(())

    grad = {n: small[n] for n in VECTORS}
    grad["final_norm"] = small["final_norm"].reshape(D_MODEL)
    grad["conv_w"] = lax.dynamic_slice_in_dim(small["conv_w"], chip * (D_CONV // N_CHIPS), D_CONV // N_CHIPS, axis=1)[None]

    delta, new_m, new_v = {}, {}, {}

    def reduce_chips(group, after):
        names, plan, started, shapes = scatter[group]
        done = _split_copy_wait("scatter_%s_wait" % group, started, plan, after)
        parts, landed = done[:len(names)], done[len(names):]
        return list(_chip_add(parts, landed, chip_core, shapes, "chip_add_" + group))

    def update(group, full):
        ends = []
        for n, reduced in zip(scatter[group][0], full):
            go, d, mo, vo = _adamw_matrix(held(n, weights[n]), reduced, held(n, args["m_" + n]), held(n, args["v_" + n]),
                                          "adamw_" + n)
            grad[n], delta[n], new_m[n], new_v[n] = given(n, go), given(n, d), given(n, mo), given(n, vo)
            ends.append(vo)
        return ends

    def share_start(group, halves):
        plan = _pair_share_plan(scatter[group][3])
        return plan, _split_copy_start("share_%s_start" % group, halves, len(halves), plan)

    halves_ffn2 = reduce_chips("ffn2", [exchange["ffn1_w13"][2][3]])
    plan_ffn2, share_ffn2 = share_start("ffn2", halves_ffn2)
    last_scatter = flush_grads("ffn1_w13", [share_ffn2[3]])
    halves_mix = reduce_chips("mix", last_scatter)
    plan_mix, share_mix = share_start("mix", halves_mix)
    done_ffn2 = update("ffn2", _split_copy_wait("share_ffn2_wait", share_ffn2, plan_ffn2, [share_mix[3]]))
    done_mix = update("mix", _split_copy_wait("share_mix_wait", share_mix, plan_mix, done_ffn2))
    as2d = lambda a: a.reshape(-1, a.shape[-1])
    ds, mos, vos = _adamw_small([as2d(weights[n]) for n in small_names], [as2d(grad[n]) for n in small_names],
                                [as2d(args["m_" + n]) for n in small_names], [as2d(args["v_" + n]) for n in small_names])
    for n, d, mo, vo in zip(small_names, ds, mos, vos):
        shape = weights[n].shape
        delta[n], new_m[n], new_v[n] = d.reshape(shape), mo.reshape(shape), vo.reshape(shape)
    behind = done_ffn2 + done_mix + [vos[0]]
    halves_w2 = reduce_chips("ffn1_w2", behind)
    halves_w13 = reduce_chips("ffn1_w13", behind)
    full_w2, full_w13 = _pair_share(halves_w2 + halves_w13, "pair_share_ffn1")
    update("ffn1_w2", [full_w2])
    update("ffn1_w13", [full_w13])

    return (loss, dx[None], *[grad[n] for n in WEIGHTS], *[delta[n] for n in WEIGHTS],
            *[new_m[n] for n in WEIGHTS], *[new_v[n] for n in WEIGHTS])
```

```python
import functools

import jax
import jax.numpy as jnp
from jax import lax
from jax.experimental import pallas as pl
from jax.experimental.pallas import tpu as pltpu

F32 = jnp.float32
BF16 = jnp.bfloat16

D_MODEL = 1024
D_FF = 2816
FF_SHARD = D_FF // 2
D_CONV = 512
D_ATTN = 512
N_HEADS = 8
HEAD_DIM = 64
CONV_WIDTH = 31
CONV_PAD = 32
N_IN = 2 * D_CONV + 3 * D_ATTN + N_HEADS
EPS = 1e-6
N_CHIPS = 4
LANES = 128
TOKEN_ROWS = 512
HEAD_ROWS = 16

ADAM_LR = 0.001
ADAM_B1 = 0.9
ADAM_B2 = 0.999
ADAM_EPS = 1e-08
ADAM_WD = 0.01
ADAM_STEP = 10

VMEM_LIMIT = 56 * 1024 * 1024

_NT = (((1,), (1,)), ((), ()))
_TN = (((0,), (0,)), ((), ()))


def _dot(a, b):
    return jnp.dot(a, b, preferred_element_type=F32)


def _dot_nt(a, b):
    return lax.dot_general(a, b, _NT, preferred_element_type=F32)


def _dot_tn(a, b):
    return lax.dot_general(a, b, _TN, preferred_element_type=F32)


def _params(**kw):
    return pltpu.CompilerParams(vmem_limit_bytes=VMEM_LIMIT, **kw)


def _sigmoid(x):
    return 1.0 / (1.0 + jnp.exp(-x))


def _rms_stats(x):
    return lax.rsqrt(jnp.mean(x * x, axis=-1, keepdims=True) + EPS)


def _rms_bwd(x, r, g, dh):
    t = dh * g
    dx = r * t - x * (r * r * r) * jnp.mean(t * x, axis=-1, keepdims=True)
    return dx, dh * x * r


def _silu_grad(z, sg):
    return sg * (1.0 + z * (1.0 - sg))


def _row_spec(tm, n):
    return pl.BlockSpec((tm, n), lambda i: (i, 0))


def _full_spec(shape):
    nd = len(shape)
    return pl.BlockSpec(shape, lambda i: (0,) * nd)


_ANY = pl.BlockSpec(memory_space=pl.ANY)


def _skip(n, body):
    return lambda *refs: body(*refs[n:])


FFN_ROWS = 256
FFN_WEIGHT_PARTS = N_CHIPS + 2


def _with_ffn_weights(w13_hbm, w2_hbm, w13_ref, w2_ref, sems, order, tile):
    first = pl.program_id(0) == 0
    copies = {("w13", k): pltpu.make_async_copy(w13_hbm.at[k], w13_ref.at[k], sems.at[k]) for k in range(N_CHIPS)}
    for half in range(2):
        rows = pl.ds(half * FF_SHARD, FF_SHARD)
        copies["w2", half] = pltpu.make_async_copy(w2_hbm.at[rows, :], w2_ref.at[rows, :], sems.at[N_CHIPS + half])

    @pl.when(first)
    def _():
        for part in order:
            copies[part].start()

        def ready(*parts):
            for part in parts:
                copies[part].wait()

        tile(ready)

    @pl.when(jnp.logical_not(first))
    def _():
        tile(lambda *parts: None)


def _ffn_fwd(x, g, w13s, w2, name, deps=()):
    t = x.shape[0]
    tm = FFN_ROWS
    deps = tuple(deps)

    def body(x_ref, g_ref, w13_hbm, w2_hbm, xo_ref, h_ref, gu_ref, w13_ref, w2_ref, sems):
        def tile(ready):
            xv = x_ref[...]
            hb = (xv * _rms_stats(xv) * g_ref[...]).astype(BF16)
            h_ref[...] = hb
            acc = jnp.zeros((tm, D_MODEL), F32)
            for half in range(2):
                lo = half * FF_SHARD
                ready(("w13", half), ("w13", 2 + half))
                gate = _dot(hb, w13_ref[half])
                up = _dot(hb, w13_ref[2 + half])
                gu_ref[:, lo:lo + FF_SHARD] = gate.astype(BF16)
                gu_ref[:, D_FF + lo:D_FF + lo + FF_SHARD] = up.astype(BF16)
                a = (gate * _sigmoid(gate) * up).astype(BF16)
                ready(("w2", half))
                acc = acc + _dot(a, w2_ref[lo:lo + FF_SHARD, :])
            xo_ref[...] = xv + 0.5 * acc

        _with_ffn_weights(w13_hbm, w2_hbm, w13_ref, w2_ref, sems,
                          [("w13", 0), ("w13", 2), ("w2", 0), ("w13", 1), ("w13", 3), ("w2", 1)], tile)

    return pl.pallas_call(
        _skip(len(deps), body), name=name, grid=(t // tm,),
        in_specs=[_ANY] * len(deps) + [_row_spec(tm, D_MODEL), _full_spec((1, D_MODEL)), _ANY, _ANY],
        out_specs=[_row_spec(tm, D_MODEL), _row_spec(tm, D_MODEL), _row_spec(tm, 2 * D_FF)],
        out_shape=[jax.ShapeDtypeStruct((t, D_MODEL), F32), jax.ShapeDtypeStruct((t, D_MODEL), BF16),
                   jax.ShapeDtypeStruct((t, 2 * D_FF), BF16)],
        scratch_shapes=[pltpu.VMEM(w13s.shape, BF16), pltpu.VMEM(w2.shape, BF16),
                        pltpu.SemaphoreType.DMA((FFN_WEIGHT_PARTS,))],
        compiler_params=_params(dimension_semantics=("arbitrary",)),
    )(*deps, x, g, w13s, w2)


def _ffn_bwd(dy, x, gu, g, w13s, w2, name, deps=()):
    t = x.shape[0]
    tm = FFN_ROWS
    deps = tuple(deps)

    def body(dy_ref, x_ref, gu_ref, g_ref, w13_hbm, w2_hbm, dx_ref, dgu_ref, a_ref, dg_ref, dyh_ref, dxb_ref,
             w13_ref, w2_ref, sems):
        @pl.when(pl.program_id(0) == 0)
        def _():
            dg_ref[...] = jnp.zeros_like(dg_ref)

        def tile(ready):
            dyv = dy_ref[...]
            dyh = (0.5 * dyv).astype(BF16)
            dyh_ref[...] = dyh
            dh = jnp.zeros((tm, D_MODEL), F32)
            for half in range(2):
                lo = half * FF_SHARD
                ready(("w2", half))
                da = _dot_nt(dyh, w2_ref[lo:lo + FF_SHARD, :])
                gate = gu_ref[:, lo:lo + FF_SHARD].astype(F32)
                up = gu_ref[:, D_FF + lo:D_FF + lo + FF_SHARD].astype(F32)
                sg = _sigmoid(gate)
                act = gate * sg
                a_ref[:, lo:lo + FF_SHARD] = (act * up).astype(BF16)
                dgate = (da * up * _silu_grad(gate, sg)).astype(BF16)
                dup = (da * act).astype(BF16)
                dgu_ref[:, lo:lo + FF_SHARD] = dgate
                dgu_ref[:, D_FF + lo:D_FF + lo + FF_SHARD] = dup
                ready(("w13", half), ("w13", 2 + half))
                dh = dh + _dot_nt(dgate, w13_ref[half]) + _dot_nt(dup, w13_ref[2 + half])
            xv = x_ref[...]
            dxn, dg_rows = _rms_bwd(xv, _rms_stats(xv), g_ref[...], dh)
            dx = dyv + dxn
            dx_ref[...] = dx
            dxb_ref[...] = dx.astype(BF16)
            dg_ref[...] += jnp.sum(dg_rows, axis=0, keepdims=True)

        _with_ffn_weights(w13_hbm, w2_hbm, w13_ref, w2_ref, sems,
                          [("w2", 0), ("w13", 0), ("w13", 2), ("w2", 1), ("w13", 1), ("w13", 3)], tile)

    return pl.pallas_call(
        _skip(len(deps), body), name=name, grid=(t // tm,),
        in_specs=[_ANY] * len(deps) + [_row_spec(tm, D_MODEL), _row_spec(tm, D_MODEL), _row_spec(tm, 2 * D_FF),
                                       _full_spec((1, D_MODEL)), _ANY, _ANY],
        out_specs=[_row_spec(tm, D_MODEL), _row_spec(tm, 2 * D_FF), _row_spec(tm, D_FF),
                   _full_spec((1, D_MODEL)), _row_spec(tm, D_MODEL), _row_spec(tm, D_MODEL)],
        out_shape=[jax.ShapeDtypeStruct((t, D_MODEL), F32), jax.ShapeDtypeStruct((t, 2 * D_FF), BF16),
                   jax.ShapeDtypeStruct((t, D_FF), BF16), jax.ShapeDtypeStruct((1, D_MODEL), F32),
                   jax.ShapeDtypeStruct((t, D_MODEL), BF16), jax.ShapeDtypeStruct((t, D_MODEL), BF16)],
        scratch_shapes=[pltpu.VMEM(w13s.shape, BF16), pltpu.VMEM(w2.shape, BF16),
                        pltpu.SemaphoreType.DMA((FFN_WEIGHT_PARTS,))],
        compiler_params=_params(dimension_semantics=("arbitrary",)),
    )(*deps, dy, x, gu, g, w13s, w2)


WGRAD_ROWS = (512, 384, 256)


def _wgrad(a, b, n_blocks, name, deps=()):
    t, m = a.shape
    tm = next(rows for rows in WGRAD_ROWS if m % rows == 0)
    n = b.shape[1]
    bn = n // n_blocks
    deps = tuple(deps)
    assert a.dtype == BF16 and b.dtype == BF16

    def body(a_ref, b_ref, o_ref):
        o_ref[0] = _dot_tn(a_ref[...], b_ref[...]).astype(BF16)

    return pl.pallas_call(
        _skip(len(deps), body), name=name, grid=(n_blocks, m // tm),
        in_specs=[_ANY] * len(deps) + [pl.BlockSpec((t, tm), lambda j, i: (0, i)),
                                       pl.BlockSpec((t, bn), lambda j, i: (0, j))],
        out_specs=pl.BlockSpec((1, tm, bn), lambda j, i: (j, i, 0)),
        out_shape=jax.ShapeDtypeStruct((n_blocks, m, bn), BF16),
        compiler_params=_params(dimension_semantics=("arbitrary", "arbitrary")),
    )(*deps, a, b)


def _mix_proj(x, g, w_ag, w_qkv, w_f):
    t = x.shape[0]
    tm = TOKEN_ROWS

    def body(x_ref, g_ref, wag_ref, wqkv_ref, wf_ref, h_ref, ag_ref, qkv_ref, fl_ref):
        xv = x_ref[...]
        hb = (xv * _rms_stats(xv) * g_ref[...]).astype(BF16)
        h_ref[...] = hb
        ag_ref[...] = _dot_nt(hb, wag_ref[...])
        qkv_ref[...] = _dot_nt(hb, wqkv_ref[...]).astype(BF16)
        fl_ref[...] = _dot_nt(hb, wf_ref[...])

    return pl.pallas_call(
        body, name="mix_proj", grid=(t // tm,),
        in_specs=[_row_spec(tm, D_MODEL), _full_spec((1, D_MODEL)), _full_spec(w_ag.shape),
                  _full_spec(w_qkv.shape), _full_spec(w_f.shape)],
        out_specs=[_row_spec(tm, D_MODEL), _row_spec(tm, 2 * D_CONV), _row_spec(tm, 3 * D_ATTN),
                   _row_spec(tm, LANES)],
        out_shape=[jax.ShapeDtypeStruct((t, D_MODEL), BF16), jax.ShapeDtypeStruct((t, 2 * D_CONV), F32),
                   jax.ShapeDtypeStruct((t, 3 * D_ATTN), BF16), jax.ShapeDtypeStruct((t, LANES), F32)],
        compiler_params=_params(dimension_semantics=("arbitrary",)),
    )(x, g, w_ag, w_qkv, w_f)


def _mix_proj_bwd(dproj, dx2, x1, g, w_ag, w_qkv, w_f):
    t = x1.shape[0]
    tm = TOKEN_ROWS
    n_ag, n_qkv = 2 * D_CONV, 3 * D_ATTN

    def body(dp_ref, dx2_ref, x_ref, g_ref, wag_ref, wqkv_ref, wf_ref, dx_ref, dg_ref):
        @pl.when(pl.program_id(0) == 0)
        def _():
            dg_ref[...] = jnp.zeros_like(dg_ref)

        dh = (_dot(dp_ref[:, 0:n_ag], wag_ref[...]) + _dot(dp_ref[:, n_ag:n_ag + n_qkv], wqkv_ref[...])
              + _dot(dp_ref[:, n_ag + n_qkv:], wf_ref[...]))
        xv = x_ref[...]
        dxn, dg_rows = _rms_bwd(xv, _rms_stats(xv), g_ref[...], dh)
        dx_ref[...] = dx2_ref[...] + dxn
        dg_ref[...] += jnp.sum(dg_rows, axis=0, keepdims=True)

    return pl.pallas_call(
        body, name="mix_proj_bwd", grid=(t // tm,),
        in_specs=[_row_spec(tm, dproj.shape[1]),
                  _row_spec(tm, D_MODEL), _row_spec(tm, D_MODEL), _full_spec((1, D_MODEL)),
                  _full_spec(w_ag.shape), _full_spec(w_qkv.shape), _full_spec(w_f.shape)],
        out_specs=[_row_spec(tm, D_MODEL), _full_spec((1, D_MODEL))],
        out_shape=[jax.ShapeDtypeStruct((t, D_MODEL), F32), jax.ShapeDtypeStruct((1, D_MODEL), F32)],
        compiler_params=_params(dimension_semantics=("arbitrary",)),
    )(dproj, dx2, x1, g, w_ag, w_qkv, w_f)


def _split3(x):
    hi = x.astype(BF16)
    r1 = x - hi.astype(F32)
    mid = r1.astype(BF16)
    lo = (r1 - mid.astype(F32)).astype(BF16)
    return hi, mid, lo


def _gates_fwd(flt, fb):
    t = flt.shape[1]

    def body(f_ref, b_ref, d_ref):
        z = f_ref[...] + b_ref[...]
        logf = jnp.minimum(z, 0.0) - jnp.log(1.0 + jnp.exp(-jnp.abs(z)))
        row = lax.broadcasted_iota(jnp.int32, (LANES, LANES), 0)
        col = lax.broadcasted_iota(jnp.int32, (LANES, LANES), 1)
        upper = (row <= col).astype(BF16)
        carry = jnp.zeros((HEAD_ROWS, 1), F32)
        for blk in range(t // LANES):
            hi, mid, lo = _split3(logf[:, blk * LANES:(blk + 1) * LANES])
            cs = _dot(hi, upper) + _dot(mid, upper) + _dot(lo, upper)
            d_ref[:, blk * LANES:(blk + 1) * LANES] = cs + carry
            carry = carry + cs[:, LANES - 1:LANES]

    return pl.pallas_call(
        body, name="gates_fwd", out_shape=jax.ShapeDtypeStruct((HEAD_ROWS, t), F32),
        compiler_params=_params(),
    )(flt, fb)


def _gates_bwd(dd, flt, fb):
    t = flt.shape[1]

    def body(dd_ref, f_ref, b_ref, df_ref, db_ref):
        z = f_ref[...] + b_ref[...]
        row = lax.broadcasted_iota(jnp.int32, (LANES, LANES), 0)
        col = lax.broadcasted_iota(jnp.int32, (LANES, LANES), 1)
        lower = (row >= col).astype(BF16)
        carry = jnp.zeros((HEAD_ROWS, 1), F32)
        db = jnp.zeros((HEAD_ROWS, 1), F32)
        for blk in reversed(range(t // LANES)):
            sl = slice(blk * LANES, (blk + 1) * LANES)
            hi, mid, lo = _split3(dd_ref[:, sl])
            cs = _dot(hi, lower) + _dot(mid, lower) + _dot(lo, lower)
            dz = (cs + carry) * _sigmoid(-z[:, sl])
            df_ref[:, sl] = dz
            db = db + jnp.sum(dz, axis=1, keepdims=True)
            carry = carry + cs[:, 0:1]
        db_ref[...] = db

    return pl.pallas_call(
        body, name="gates_bwd",
        out_shape=[jax.ShapeDtypeStruct((HEAD_ROWS, t), F32), jax.ShapeDtypeStruct((HEAD_ROWS, 1), F32)],
        compiler_params=_params(),
    )(dd, flt, fb)


CONV_CHUNK = 128
CONV_TAIL = 16
CONV_WINDOW = CONV_CHUNK + CONV_PAD + 8
CONV_ROWS_EXTRA = CONV_PAD + CONV_TAIL
SUBLANES = 8


def _conv_rows(ag_ref, u_ref, t):
    u_ref[0:CONV_PAD, :] = jnp.zeros((CONV_PAD, D_CONV), F32)
    u_ref[CONV_PAD + t:CONV_ROWS_EXTRA + t, :] = jnp.zeros((CONV_TAIL, D_CONV), F32)

    def fill(i, c):
        r0 = pl.multiple_of(i * CONV_CHUNK, CONV_CHUNK)
        a = ag_ref[pl.ds(r0, CONV_CHUNK), 0:D_CONV]
        gt = ag_ref[pl.ds(r0, CONV_CHUNK), D_CONV:2 * D_CONV]
        u_ref[pl.ds(CONV_PAD + r0, CONV_CHUNK), :] = a * _sigmoid(gt)
        return c

    lax.fori_loop(0, t // CONV_CHUNK, fill, 0)


def _for_shifted(ref, r0, offsets, fn):
    window = ref[pl.ds(r0, CONV_WINDOW), :]
    for rem in range(SUBLANES):
        mine = [o for o in offsets if o % SUBLANES == rem]
        if not mine:
            continue
        turned = window if rem == 0 else pltpu.roll(window, CONV_WINDOW - rem, 0)
        for o in mine:
            fn(o, turned[o - rem:o - rem + CONV_CHUNK])


def _conv_point(u_ref, r0, w_ref, cb, lg, lb):
    acc = [jnp.zeros((CONV_CHUNK, D_CONV), F32)]

    def tap(o, rows):
        j = o - (CONV_PAD - CONV_WIDTH + 1)
        acc[0] = acc[0] + w_ref[j:j + 1, :] * rows

    _for_shifted(u_ref, r0, [j + CONV_PAD - CONV_WIDTH + 1 for j in range(CONV_WIDTH)], tap)
    y = acc[0] + cb
    mu = jnp.mean(y, axis=-1, keepdims=True)
    yc = y - mu
    rstd = lax.rsqrt(jnp.mean(yc * yc, axis=-1, keepdims=True) + EPS)
    yhat = yc * rstd
    z = yhat * lg + lb
    sg = _sigmoid(z)
    s = z * sg
    rr = _rms_stats(s)
    return yhat, rstd, z, sg, s, rr


def _conv_fwd(ag, conv_w, conv_b, ln_g, ln_b, norm_g):
    t = ag.shape[0]

    def body(ag_ref, w_ref, cb_ref, lg_ref, lb_ref, ng_ref, o_ref, u_ref):
        _conv_rows(ag_ref, u_ref, t)
        cb, lg, lb, ng = cb_ref[...], lg_ref[...], lb_ref[...], ng_ref[...]

        def chunk(i, c):
            r0 = pl.multiple_of(i * CONV_CHUNK, CONV_CHUNK)
            _, _, _, _, s, rr = _conv_point(u_ref, r0, w_ref, cb, lg, lb)
            o_ref[pl.ds(r0, CONV_CHUNK), :] = (s * rr * ng).astype(BF16)
            return c

        lax.fori_loop(0, t // CONV_CHUNK, chunk, 0)

    return pl.pallas_call(
        body, name="conv_fwd", out_shape=jax.ShapeDtypeStruct((t, D_CONV), BF16),
        scratch_shapes=[pltpu.VMEM((t + CONV_ROWS_EXTRA, D_CONV), F32)],
        compiler_params=_params(),
    )(ag, conv_w, conv_b, ln_g, ln_b, norm_g)


def _conv_bwd(ag, dout, conv_w, conv_b, ln_g, ln_b, norm_g):
    t = ag.shape[0]

    def body(ag_ref, do_ref, w_ref, cb_ref, lg_ref, lb_ref, ng_ref,
             dag_ref, dw_ref, dcb_ref, dlg_ref, dlb_ref, dng_ref, u_ref, dy_ref):
        _conv_rows(ag_ref, u_ref, t)
        dy_ref[t:t + CONV_ROWS_EXTRA, :] = jnp.zeros((CONV_ROWS_EXTRA, D_CONV), F32)
        cb, lg, lb, ng = cb_ref[...], lg_ref[...], lb_ref[...], ng_ref[...]
        dw_ref[...] = jnp.zeros_like(dw_ref)
        zero = jnp.zeros((1, D_CONV), F32)

        def chunk(i, carry):
            dcb, dlg, dlb, dng = carry
            r0 = pl.multiple_of(i * CONV_CHUNK, CONV_CHUNK)
            yhat, rstd, z, sg, s, rr = _conv_point(u_ref, r0, w_ref, cb, lg, lb)
            do = do_ref[pl.ds(r0, CONV_CHUNK), :]
            ds, dng_rows = _rms_bwd(s, rr, ng, do)
            dz = ds * _silu_grad(z, sg)
            dyhat = dz * lg
            dy = rstd * (dyhat - jnp.mean(dyhat, axis=-1, keepdims=True)
                         - yhat * jnp.mean(dyhat * yhat, axis=-1, keepdims=True))
            dy_ref[pl.ds(r0, CONV_CHUNK), :] = dy
            def tap(o, rows):
                j = o - (CONV_PAD - CONV_WIDTH + 1)
                dw_ref[j:j + 1, :] += jnp.sum(dy * rows, axis=0, keepdims=True)

            _for_shifted(u_ref, r0, [j + CONV_PAD - CONV_WIDTH + 1 for j in range(CONV_WIDTH)], tap)
            return (dcb + jnp.sum(dy, axis=0, keepdims=True), dlg + jnp.sum(dz * yhat, axis=0, keepdims=True),
                    dlb + jnp.sum(dz, axis=0, keepdims=True), dng + jnp.sum(dng_rows, axis=0, keepdims=True))

        dcb, dlg, dlb, dng = lax.fori_loop(0, t // CONV_CHUNK, chunk, (zero, zero, zero, zero))
        dcb_ref[...] = dcb
        dlg_ref[...] = dlg
        dlb_ref[...] = dlb
        dng_ref[...] = dng

        def chunk2(i, c):
            r0 = pl.multiple_of(i * CONV_CHUNK, CONV_CHUNK)
            acc = [jnp.zeros((CONV_CHUNK, D_CONV), F32)]

            def tap(o, rows):
                j = CONV_WIDTH - 1 - o
                acc[0] = acc[0] + w_ref[j:j + 1, :] * rows

            _for_shifted(dy_ref, r0, list(range(CONV_WIDTH)), tap)
            du = acc[0]
            a = ag_ref[pl.ds(r0, CONV_CHUNK), 0:D_CONV]
            gt = ag_ref[pl.ds(r0, CONV_CHUNK), D_CONV:2 * D_CONV]
            sg = _sigmoid(gt)
            dag_ref[pl.ds(r0, CONV_CHUNK), 0:D_CONV] = (du * sg).astype(BF16)
            dag_ref[pl.ds(r0, CONV_CHUNK), D_CONV:2 * D_CONV] = (du * a * sg * (1.0 - sg)).astype(BF16)
            return c

        lax.fori_loop(0, t // CONV_CHUNK, chunk2, 0)

    vec = jax.ShapeDtypeStruct((1, D_CONV), F32)
    return pl.pallas_call(
        body, name="conv_bwd",
        out_shape=[jax.ShapeDtypeStruct((t, 2 * D_CONV), BF16), jax.ShapeDtypeStruct((CONV_PAD, D_CONV), F32),
                   vec, vec, vec, vec],
        scratch_shapes=[pltpu.VMEM((t + CONV_ROWS_EXTRA, D_CONV), F32), pltpu.VMEM((t + CONV_ROWS_EXTRA, D_CONV), F32)],
        compiler_params=_params(),
    )(ag, dout, conv_w, conv_b, ln_g, ln_b, norm_g)


Q_ROWS = 256
ATTN_SCALE = HEAD_DIM ** -0.5
ATTN_AHEAD = 1


def _attn_specs(t):
    blk = lambda off: pl.BlockSpec((t, LANES), lambda p: (0, off + p))
    pairs = N_HEADS // 2
    return [blk(0), blk(pairs), blk(2 * pairs), pl.BlockSpec((2, 1, t), lambda p: (p, 0, 0))]


def _one_head(q2, mask):
    return jnp.where(mask, q2, jnp.zeros_like(q2)) * ATTN_SCALE


def _attn_scores(qs, k2, drow, r0, q1):
    s = _dot_nt(qs, k2) - drow
    rowi = lax.broadcasted_iota(jnp.int32, (q1 - r0, q1 - r0), 0)
    coli = lax.broadcasted_iota(jnp.int32, (q1 - r0, q1 - r0), 1)
    diag = jnp.where(coli <= rowi, s[:, r0:q1], -jnp.inf)
    return diag if r0 == 0 else jnp.concatenate([s[:, :r0], diag], axis=1)


def _attn_fwd(qkv, drow, deps=()):
    t = qkv.shape[0]
    deps = tuple(deps)

    def body(q_ref, k_ref, v_ref, dr_ref, o_ref, lse_ref):
        head_a = lax.broadcasted_iota(jnp.int32, (1, LANES), 1) < HEAD_DIM
        items = [(qb, hh) for qb in range(t // Q_ROWS) for hh in range(2)]

        def scores(item):
            qb, hh = item
            r0, q1 = qb * Q_ROWS, (qb + 1) * Q_ROWS
            qs = _one_head(q_ref[r0:q1, :], head_a if hh == 0 else ~head_a)
            return _attn_scores(qs, k_ref[0:q1, :], dr_ref[hh, :, 0:q1], r0, q1)

        ahead = [scores(item) for item in items[:ATTN_AHEAD]]
        outs = []
        for n, (qb, hh) in enumerate(items):
            r0, q1 = qb * Q_ROWS, (qb + 1) * Q_ROWS
            s = ahead.pop(0)
            if n + ATTN_AHEAD < len(items):
                ahead.append(scores(items[n + ATTN_AHEAD]))
            mx = jnp.max(s, axis=1, keepdims=True)
            p = jnp.exp(s - mx)
            l = jnp.sum(p, axis=1, keepdims=True)
            lse_ref[hh, r0:q1, :] = mx + jnp.log(l)
            outs.append(_dot(p.astype(BF16), v_ref[0:q1, :]) * (1.0 / l))
            if hh == 1:
                o_ref[r0:q1, :] = jnp.where(head_a, outs[0], outs[1])
                outs = []

    pairs = N_HEADS // 2
    return pl.pallas_call(
        _skip(len(deps), body), name="attn_fwd", grid=(pairs,), in_specs=[_ANY] * len(deps) + _attn_specs(t),
        out_specs=[pl.BlockSpec((t, LANES), lambda p: (0, p)), pl.BlockSpec((2, t, 1), lambda p: (p, 0, 0))],
        out_shape=[jax.ShapeDtypeStruct((t, D_ATTN), F32), jax.ShapeDtypeStruct((N_HEADS, t, 1), F32)],
        compiler_params=_params(dimension_semantics=("arbitrary",)),
    )(*deps, qkv, qkv, qkv, drow)


def _attn_bwd(qkv, drow, lse, do):
    t = qkv.shape[0]

    def body(q_ref, k_ref, v_ref, dr_ref, lse_ref, do_ref,
             dq_ref, dk_ref, dv_ref, dd_ref, dk_acc, dv_acc):
        head_a = lax.broadcasted_iota(jnp.int32, (1, LANES), 1) < HEAD_DIM
        dk_acc[...] = jnp.zeros_like(dk_acc)
        dv_acc[...] = jnp.zeros_like(dv_acc)
        dd_ref[...] = jnp.zeros_like(dd_ref)
        items = [(qb, hh) for qb in range(t // Q_ROWS) for hh in range(2)]

        def products(item):
            qb, hh = item
            r0, q1 = qb * Q_ROWS, (qb + 1) * Q_ROWS
            mask = head_a if hh == 0 else ~head_a
            qs = _one_head(q_ref[r0:q1, :], mask)
            dob = jnp.where(mask, do_ref[r0:q1, :], 0.0).astype(BF16)
            s = _attn_scores(qs, k_ref[0:q1, :], dr_ref[hh, :, 0:q1], r0, q1)
            return qs, dob, s, _dot_nt(dob, v_ref[0:q1, :])

        ahead = products(items[0])
        dqs = []
        for n, (qb, hh) in enumerate(items):
            r0, q1 = qb * Q_ROWS, (qb + 1) * Q_ROWS
            qs, dob, s, dp = ahead
            if n + 1 < len(items):
                ahead = products(items[n + 1])
            p = jnp.exp(s - lse_ref[hh, r0:q1, :])
            ds = p * (dp - jnp.sum(p * dp, axis=1, keepdims=True))
            dsb = ds.astype(BF16)
            dqs.append(_dot(dsb, k_ref[0:q1, :]) * ATTN_SCALE)
            dk_acc[0:q1, :] += _dot_tn(dsb, qs)
            dv_acc[0:q1, :] += _dot_tn(p.astype(BF16), dob)
            dd_ref[hh, :, 0:q1] -= jnp.sum(ds, axis=0, keepdims=True)
            if hh == 1:
                dq_ref[r0:q1, :] = jnp.where(head_a, dqs[0], dqs[1]).astype(BF16)
                dqs = []
        dk_ref[...] = dk_acc[...].astype(BF16)
        dv_ref[...] = dv_acc[...].astype(BF16)

    pairs = N_HEADS // 2
    col = pl.BlockSpec((t, LANES), lambda p: (0, p))
    grad = jax.ShapeDtypeStruct((t, D_ATTN), BF16)
    return pl.pallas_call(
        body, name="attn_bwd", grid=(pairs,),
        in_specs=_attn_specs(t) + [pl.BlockSpec((2, t, 1), lambda p: (p, 0, 0)), col],
        out_specs=[col, col, col, pl.BlockSpec((2, 1, t), lambda p: (p, 0, 0))],
        out_shape=[grad, grad, grad, jax.ShapeDtypeStruct((N_HEADS, 1, t), F32)],
        scratch_shapes=[pltpu.VMEM((t, LANES), F32), pltpu.VMEM((t, LANES), F32)],
        compiler_params=_params(dimension_semantics=("arbitrary",)),
    )(qkv, qkv, qkv, drow, lse, do)


def _out_proj(ycn, o, g_attn, w_out, x1, deps=()):
    t = x1.shape[0]
    tm = TOKEN_ROWS
    deps = tuple(deps)

    def body(yc_ref, o_ref, g_ref, w_ref, x_ref, xo_ref, ya_ref):
        ov = o_ref[...]
        ya = (ov * _rms_stats(ov) * g_ref[...]).astype(BF16)
        ya_ref[...] = ya
        xo_ref[...] = x_ref[...] + _dot(yc_ref[...], w_ref[0:D_CONV, :]) + _dot(ya, w_ref[D_CONV:, :])

    return pl.pallas_call(
        _skip(len(deps), body), name="out_proj", grid=(t // tm,),
        in_specs=[_ANY] * len(deps) + [_row_spec(tm, D_CONV), _row_spec(tm, D_ATTN), _full_spec((1, D_ATTN)),
                                       _full_spec(w_out.shape), _row_spec(tm, D_MODEL)],
        out_specs=[_row_spec(tm, D_MODEL), _row_spec(tm, D_ATTN)],
        out_shape=[jax.ShapeDtypeStruct((t, D_MODEL), F32), jax.ShapeDtypeStruct((t, D_ATTN), BF16)],
        compiler_params=_params(dimension_semantics=("arbitrary",)),
    )(*deps, ycn, o, g_attn, w_out, x1)


def _out_proj_bwd(dx2, o, g_attn, w_out, deps=()):
    t = dx2.shape[0]
    tm = TOKEN_ROWS
    deps = tuple(deps)

    def body(dx_ref, o_ref, g_ref, w_ref, dyc_ref, do_ref, dg_ref):
        @pl.when(pl.program_id(0) == 0)
        def _():
            dg_ref[...] = jnp.zeros_like(dg_ref)

        dxb = dx_ref[...]
        dyc_ref[...] = _dot_nt(dxb, w_ref[0:D_CONV, :])
        dya = _dot_nt(dxb, w_ref[D_CONV:, :])
        ov = o_ref[...]
        do, dg_rows = _rms_bwd(ov, _rms_stats(ov), g_ref[...], dya)
        do_ref[...] = do
        dg_ref[...] += jnp.sum(dg_rows, axis=0, keepdims=True)

    return pl.pallas_call(
        _skip(len(deps), body), name="out_proj_bwd", grid=(t // tm,),
        in_specs=[_ANY] * len(deps) + [_row_spec(tm, D_MODEL), _row_spec(tm, D_ATTN), _full_spec((1, D_ATTN)),
                                       _full_spec(w_out.shape)],
        out_specs=[_row_spec(tm, D_CONV), _row_spec(tm, D_ATTN), _full_spec((1, D_ATTN))],
        out_shape=[jax.ShapeDtypeStruct((t, D_CONV), F32), jax.ShapeDtypeStruct((t, D_ATTN), F32),
                   jax.ShapeDtypeStruct((1, D_ATTN), F32)],
        compiler_params=_params(dimension_semantics=("arbitrary",)),
    )(*deps, dx2, o, g_attn, w_out)


def _loss_bwd(x3, target, g):
    t = x3.shape[0]
    tm = TOKEN_ROWS

    def body(x_ref, t_ref, g_ref, loss_ref, dx_ref, dg_ref):
        @pl.when(pl.program_id(0) == 0)
        def _():
            loss_ref[...] = jnp.zeros_like(loss_ref)
            dg_ref[...] = jnp.zeros_like(dg_ref)

        xv = x_ref[...]
        r = _rms_stats(xv)
        gv = g_ref[...]
        err = xv * r * gv - t_ref[...]
        row = jnp.sum(err * err, axis=1, keepdims=True) * (0.5 / D_MODEL)
        loss_ref[...] += jnp.sum(row, axis=0, keepdims=True)
        dx, dg_rows = _rms_bwd(xv, r, gv, err * (1.0 / D_MODEL))
        dx_ref[...] = dx
        dg_ref[...] += jnp.sum(dg_rows, axis=0, keepdims=True)

    return pl.pallas_call(
        body, name="loss_bwd", grid=(t // tm,),
        in_specs=[_row_spec(tm, D_MODEL), _row_spec(tm, D_MODEL), _full_spec((1, D_MODEL))],
        out_specs=[_full_spec((1, LANES)), _row_spec(tm, D_MODEL), _full_spec((1, D_MODEL))],
        out_shape=[jax.ShapeDtypeStruct((1, LANES), F32), jax.ShapeDtypeStruct((t, D_MODEL), F32),
                   jax.ShapeDtypeStruct((1, D_MODEL), F32)],
        compiler_params=_params(dimension_semantics=("arbitrary",)),
    )(x3, target, g)


def _split_w_in(w_in_t):
    w_ag = w_in_t[:2 * D_CONV]
    w_qkv = w_in_t[2 * D_CONV:2 * D_CONV + 3 * D_ATTN]
    w_f = jnp.pad(w_in_t[2 * D_CONV + 3 * D_ATTN:], ((0, LANES - N_HEADS), (0, 0)))
    return w_ag, w_qkv, w_f


def _head_rows(v):
    return jnp.pad(v, ((0, HEAD_ROWS - N_HEADS),) + ((0, 0),) * (v.ndim - 1))


def _local_step(x, target, p, get_weights, put_grads, flush_grads):
    t = x.shape[0]
    fb = _head_rows(p["forget_b"].reshape(N_HEADS, 1))

    w, deps = get_weights("ffn1", None)
    x1, h1, gu1 = _ffn_fwd(x, p["ffn1_norm"], w["ffn1_w13"], w["ffn1_w2"], "ffn1_fwd", deps)
    wm, _ = get_weights("mix", x1)
    w.update(wm)
    w_ag, w_qkv, w_f = _split_w_in(w["w_in"])
    conv_w = jnp.pad(w["conv_w"], ((0, CONV_PAD - CONV_WIDTH), (0, 0)))
    h2, ag, qkv, fl = _mix_proj(x1, p["mix_norm"], w_ag, w_qkv, w_f)
    flt = _head_rows(fl[:, :N_HEADS].T)
    dcum = _gates_fwd(flt, fb)[:N_HEADS]
    drow = dcum.reshape(N_HEADS, 1, t)
    ycn = _conv_fwd(ag, conv_w, p["conv_b"], p["conv_ln_g"], p["conv_ln_b"], p["out_norm_conv"])
    o, lse = _attn_fwd(qkv, drow, [ycn])
    _, deps = get_weights("ffn2:landed", o)
    x2, yan = _out_proj(ycn, o, p["out_norm_attn"], w["w_out"], x1, deps)
    w2, _ = get_weights("ffn2", x2)
    w.update(w2)
    x3, h3, gu2 = _ffn_fwd(x2, p["ffn2_norm"], w["ffn2_w13"], w["ffn2_w2"], "ffn2_fwd")
    loss, dx3, d_final = _loss_bwd(x3, target, p["final_norm"])

    g = {}
    dx2, dgu2, a2, g["ffn2_norm"], dx3_half, dx2_bf16 = _ffn_bwd(
        dx3, x2, gu2, p["ffn2_norm"], w["ffn2_w13"], w["ffn2_w2"], "ffn2_bwd")
    dw13 = _wgrad(h3, dgu2, N_CHIPS, "ffn2_dw13")
    dw2 = _wgrad(a2, dx3_half, 1, "ffn2_dw2").reshape(D_FF, D_MODEL)
    deps = put_grads("ffn2", {"ffn2_w13": dw13, "ffn2_w2": dw2})
    dyc, do, g["out_norm_attn"] = _out_proj_bwd(dx2_bf16, o, p["out_norm_attn"], w["w_out"], deps)
    deps = flush_grads("ffn2", [dyc])
    dw_out = _wgrad(jnp.concatenate([ycn, yan], axis=1), dx2_bf16, 1, "dw_out", deps).reshape(D_MODEL, D_MODEL)
    dq, dk, dv, ddrow = _attn_bwd(qkv, drow, lse, do)
    dflt, dfb = _gates_bwd(_head_rows(ddrow.reshape(N_HEADS, t)), flt, fb)
    g["forget_b"] = dfb[:N_HEADS, 0].reshape(1, N_HEADS)
    dfl = jnp.pad(dflt[:N_HEADS].T, ((0, 0), (0, LANES - N_HEADS)))
    dag, dconv_w, g["conv_b"], g["conv_ln_g"], g["conv_ln_b"], g["out_norm_conv"] = _conv_bwd(
        ag, dyc, conv_w, p["conv_b"], p["conv_ln_g"], p["conv_ln_b"], p["out_norm_conv"])
    g["conv_w"] = dconv_w[:CONV_WIDTH]
    dproj = jnp.concatenate([dag, dq, dk, dv, dfl.astype(BF16)], axis=1)
    dx1, g["mix_norm"] = _mix_proj_bwd(dproj, dx2, x1, p["mix_norm"], w_ag, w_qkv, w_f)
    dw_in = _wgrad(dproj, h2, 1, "dw_in").reshape(dproj.shape[1], D_MODEL)[:N_IN]
    deps = put_grads("mix", {"w_in": dw_in, "w_out": dw_out})
    dx0, dgu1, a1, g["ffn1_norm"], dx1_half, _ = _ffn_bwd(
        dx1, x, gu1, p["ffn1_norm"], w["ffn1_w13"], w["ffn1_w2"], "ffn1_bwd", deps)
    g["final_norm"] = d_final
    g["loss"] = loss[:, :1]
    deps = flush_grads("mix", put_grads("small", g))
    dw2 = _wgrad(a1, dx1_half, 1, "ffn1_dw2", deps).reshape(D_FF, D_MODEL)
    deps = flush_grads("ffn1_w2", put_grads("ffn1_w2", {"ffn1_w2": dw2}))
    dw13 = _wgrad(h1, dgu1, N_CHIPS, "ffn1_dw13", deps)
    put_grads("ffn1_w13", {"ffn1_w13": dw13})
    return dx0


MESH = pl.DeviceIdType.MESH


def _place():
    x, y, c = lax.axis_index("x"), lax.axis_index("y"), lax.axis_index("c")
    chips = [(1 - x, y), (x, 1 - y), (1 - x, 1 - y)]
    return x, y, c, chips


def _hbm_out(shape, dtype):
    return jax.ShapeDtypeStruct(shape, dtype)


def _comm_call(body, name, ins, out_shapes, n_remote, in_place=False):
    return pl.pallas_call(
        body, name=name, in_specs=[_ANY] * len(ins), out_specs=[_ANY] * len(out_shapes), out_shape=out_shapes,
        scratch_shapes=[pltpu.SemaphoreType.DMA((n_remote,)), pltpu.SemaphoreType.DMA((n_remote,))],
        input_output_aliases={i: i for i in range(len(ins))} if in_place else {},
    )(*ins)


def _remote(src, dst, sems, n, to):
    send_sems, recv_sems = sems
    return pltpu.make_async_remote_copy(src_ref=src, dst_ref=dst, send_sem=send_sems.at[n], recv_sem=recv_sems.at[n],
                                        device_id=to, device_id_type=MESH)


HALF_ROWS_MULTIPLE = 32


def _halved_by_rows(rows):
    return rows % HALF_ROWS_MULTIPLE == 0


def _half_shape(rows, cols):
    return (rows // 2, cols) if _halved_by_rows(rows) else (rows, cols // 2)


def _half_index(rows, core):
    return (core, 0) if _halved_by_rows(rows) else (0, core)


def _half_of(ref, rows, cols, core, *lead):
    if _halved_by_rows(rows):
        return ref.at[(*lead, pl.ds(core * (rows // 2), rows // 2), slice(None))]
    return ref.at[(*lead, slice(None), pl.ds(core * (cols // 2), cols // 2))]


def _into_slot(shard, chip, dtype, name, deps=()):
    rows, cols = shard.shape
    half = _half_shape(rows, cols)
    by_rows = _halved_by_rows(rows)
    deps = tuple(deps)

    def body(k_ref, *refs):
        s_ref, o_ref = refs[len(deps):]
        o_ref[0] = s_ref[...].astype(dtype)

    return pl.pallas_call(
        body, name=name,
        grid_spec=pltpu.PrefetchScalarGridSpec(
            num_scalar_prefetch=1, grid=(2,),
            in_specs=[_ANY] * len(deps) + [pl.BlockSpec(half, lambda i, k_ref: (i, 0) if by_rows else (0, i))],
            out_specs=pl.BlockSpec((1,) + half, lambda i, k_ref: (k_ref[0], i, 0) if by_rows else (k_ref[0], 0, i))),
        out_shape=jax.ShapeDtypeStruct((N_CHIPS, rows, cols), dtype),
        compiler_params=_params(dimension_semantics=("arbitrary",)),
    )(chip, *deps, shard)


def _gather_shards(slots, name, ici=True, passed=()):
    n = len(slots)
    slots = list(slots) + list(passed)
    total = len(slots)

    def body(*refs):
        outs = refs[total:total + n]
        sems = refs[2 * total:2 * total + 2]
        x, y, c, chips = _place()
        me = 2 * x + y
        sibling = (x, y, 1 - c)

        def half(i, chip_index, core):
            return _half_of(outs[i], *slots[i].shape[1:], core, chip_index)

        sends = []
        if ici:
            for i in range(n):
                for j, chip in enumerate(chips):
                    cp = _remote(half(i, me, c), half(i, me, c), sems, 6 * i + j, (*chip, c))
                    cp.start()
                    sends.append(cp)
        for i in range(n):
            for j, chip in enumerate(chips):
                src_chip = 2 * chip[0] + chip[1]
                landed = half(i, src_chip, c)
                if ici:
                    _remote(landed, landed, sems, 6 * i + j, (*chip, c)).wait_recv()
                cp = _remote(landed, landed, sems, 6 * i + 3 + j, sibling)
                cp.start()
                sends.append(cp)
        for i in range(n):
            for j, chip in enumerate(chips):
                src_chip = 2 * chip[0] + chip[1]
                landed = half(i, src_chip, 1 - c)
                _remote(landed, landed, sems, 6 * i + 3 + j, sibling).wait_recv()
        for cp in sends:
            cp.wait_send()

    outs = [_hbm_out(s.shape, s.dtype) for s in slots]
    return _comm_call(body, name, slots, outs, 6 * n, in_place=True)


_HBM = pl.BlockSpec(memory_space=pltpu.HBM)
_SEM = pl.BlockSpec(memory_space=pltpu.SEMAPHORE)
_DATAFLOW = pltpu.SideEffectType.DATAFLOW_SIDE_EFFECTING


def _split_copy_start(name, bufs, n_copies, plan):
    n = len(bufs)

    def body(*refs):
        for send, _ in plan(refs[:n], (refs[n], refs[n + 1])):
            send.start()
        token = refs[-1]
        token[...] = jnp.zeros_like(token)

    out = pl.pallas_call(
        body, name=name,
        out_shape=(pltpu.SemaphoreType.DMA((n_copies,)), pltpu.SemaphoreType.DMA((n_copies,)),
                   *[pltpu.HBM(b.shape, b.dtype) for b in bufs], jax.ShapeDtypeStruct((8, LANES), F32)),
        in_specs=[_HBM] * n, out_specs=(_SEM, _SEM, *[_HBM] * n, pl.BlockSpec(memory_space=pltpu.VMEM)),
        input_output_aliases={i: 2 + i for i in range(n)},
        compiler_params=pltpu.CompilerParams(has_side_effects=_DATAFLOW),
    )(*[pltpu.with_memory_space_constraint(b, pltpu.HBM) for b in bufs])
    return out[0], out[1], list(out[2:2 + n]), out[-1]


def _split_copy_wait(name, started, plan, after, passed=()):
    send_sems, recv_sems, bufs, _ = started
    n = len(bufs)
    after = tuple(after)
    bufs = list(bufs) + list(passed)
    total = len(bufs)

    def body(*refs):
        for send, recv in plan(refs[:n], (refs[total], refs[total + 1])):
            send.wait_send()
            recv.wait_recv()

    out = pl.pallas_call(
        body, name=name, out_shape=tuple(pltpu.HBM(b.shape, b.dtype) for b in bufs),
        in_specs=[_HBM] * total + [_SEM, _SEM] + [_ANY] * len(after), out_specs=tuple([_HBM] * total),
        input_output_aliases={i: i for i in range(total)},
        compiler_params=pltpu.CompilerParams(has_side_effects=_DATAFLOW),
    )(*bufs, send_sems, recv_sems, *after)
    return list(out)


def _ici_gather_plan(slots):
    def plan(refs, sems):
        x, y, c, chips = _place()
        me = 2 * x + y
        copies = []
        for i, ref in enumerate(refs):
            for j, chip in enumerate(chips):
                mine = _half_of(ref, *slots[i].shape[1:], c, me)
                theirs = _half_of(ref, *slots[i].shape[1:], c, 2 * chip[0] + chip[1])
                to = (*chip, c)
                copies.append((_remote(mine, mine, sems, 3 * i + j, to), _remote(theirs, theirs, sems, 3 * i + j, to)))
        return copies

    return plan


def _ici_scatter_plan(n):
    def plan(refs, sems):
        x, y, c, chips = _place()
        copies = []
        for i in range(n):
            for j, chip in enumerate(chips):
                cp = _remote(refs[i].at[2 * chip[0] + chip[1]], refs[n + i].at[j], sems, 3 * i + j, (*chip, c))
                copies.append((cp, cp))
        return copies

    return plan


def _d2d_forward_plan(slots):
    def plan(refs, sems):
        x, y, c, chips = _place()
        sibling = (x, y, 1 - c)
        copies = []
        for i, ref in enumerate(refs):
            for j, chip in enumerate(chips):
                src_chip = 2 * chip[0] + chip[1]
                mine = _half_of(ref, *slots[i].shape[1:], c, src_chip)
                theirs = _half_of(ref, *slots[i].shape[1:], 1 - c, src_chip)
                copies.append((_remote(mine, mine, sems, 3 * i + j, sibling),
                               _remote(theirs, theirs, sems, 3 * i + j, sibling)))
        return copies

    return plan


def _pair_exchange_plan(grads):
    n = len(grads)

    def plan(refs, sems):
        x, y, c, _ = _place()
        copies = []
        for i in range(n):
            theirs = _half_of(refs[i], *grads[i].shape[1:], 1 - c, slice(None))
            cp = _remote(theirs, refs[n + i], sems, i, (x, y, 1 - c))
            copies.append((cp, cp))
        return copies

    return plan


def _pair_share_plan(shapes):
    def plan(refs, sems):
        x, y, c, _ = _place()
        sibling = (x, y, 1 - c)
        copies = []
        for i, ref in enumerate(refs):
            mine, theirs = _half_of(ref, *shapes[i], c), _half_of(ref, *shapes[i], 1 - c)
            copies.append((_remote(mine, mine, sems, i, sibling), _remote(theirs, theirs, sems, i, sibling)))
        return copies

    return plan


def _pair_share(halves, name):
    n = len(halves)
    plan = _pair_share_plan([h.shape for h in halves])

    def body(*refs):
        copies = plan(refs[n:2 * n], refs[2 * n:2 * n + 2])
        for send, _ in copies:
            send.start()
        for send, recv in copies:
            send.wait_send()
            recv.wait_recv()

    outs = [_hbm_out(h.shape, h.dtype) for h in halves]
    return _comm_call(body, name, halves, outs, n, in_place=True)


def _all_reduce_small(v, deps=()):
    rows = v.shape[0]
    flips = [(fx, fy, fc) for fx in range(2) for fy in range(2) for fc in range(2)][1:]

    def body(v_ref, o_ref, slots, send_sems, recv_sems):
        x, y, c, _ = _place()
        me = 4 * x + 2 * y + c
        slots[me] = v_ref[...]
        sends = []
        for n, (fx, fy, fc) in enumerate(flips):
            to = (x ^ fx, y ^ fy, c ^ fc)
            cp = _remote(v_ref, slots.at[me], (send_sems, recv_sems), n, to)
            cp.start()
            sends.append(cp)
        for n, (fx, fy, fc) in enumerate(flips):
            src = 4 * (x ^ fx) + 2 * (y ^ fy) + (c ^ fc)
            _remote(v_ref, slots.at[src], (send_sems, recv_sems), n, (x ^ fx, y ^ fy, c ^ fc)).wait_recv()
        for cp in sends:
            cp.wait_send()
        acc = slots[0]
        for s in range(1, 8):
            acc = acc + slots[s]
        o_ref[...] = acc

    deps = tuple(deps)
    return pl.pallas_call(
        _skip(len(deps), body), name="all_reduce_small", out_shape=jax.ShapeDtypeStruct(v.shape, F32),
        in_specs=[_ANY] * len(deps) + [pl.BlockSpec(memory_space=pltpu.VMEM)],
        out_specs=pl.BlockSpec(memory_space=pltpu.VMEM),
        scratch_shapes=[pltpu.VMEM((8, rows, LANES), F32), pltpu.SemaphoreType.DMA((7,)), pltpu.SemaphoreType.DMA((7,))],
    )(*deps, v)


def _pair_add(gs, sibs, core, name):
    n = len(gs)
    halves = [_half_shape(*g.shape[1:]) for g in gs]

    def body(c_ref, *refs):
        for g_ref, s_ref, o_ref in zip(refs[:n], refs[n:2 * n], refs[2 * n:]):
            o_ref[0] = (g_ref[0].astype(F32) + s_ref[0].astype(F32)).astype(BF16)

    def mine(g, half):
        return pl.BlockSpec((1,) + half, lambda s, c_ref: (s, *_half_index(g.shape[1], c_ref[0])))

    whole = [pl.BlockSpec((1,) + half, lambda s, c_ref: (s, 0, 0)) for half in halves]
    return pl.pallas_call(
        body, name=name,
        grid_spec=pltpu.PrefetchScalarGridSpec(
            num_scalar_prefetch=1, grid=(N_CHIPS,),
            in_specs=[mine(g, half) for g, half in zip(gs, halves)] + whole, out_specs=whole),
        out_shape=[jax.ShapeDtypeStruct((N_CHIPS,) + half, BF16) for half in halves],
        compiler_params=_params(dimension_semantics=("arbitrary",)),
    )(core, *gs, *sibs)


def _chip_add(parts, recvs, chip_core, shapes, name):
    n = len(parts)
    halves = [_half_shape(*shape) for shape in shapes]

    def body(kc_ref, *refs):
        for p_ref, r_ref, o_ref in zip(refs[:n], refs[n:2 * n], refs[2 * n:]):
            acc = p_ref[0].astype(F32)
            for j in range(N_CHIPS - 1):
                acc = acc + r_ref[j].astype(F32)
            o_ref[...] = acc

    def out_spec(shape, half):
        return pl.BlockSpec(half, lambda s, kc_ref: _half_index(shape[0], kc_ref[1]))

    return pl.pallas_call(
        body, name=name,
        grid_spec=pltpu.PrefetchScalarGridSpec(
            num_scalar_prefetch=1, grid=(1,),
            in_specs=[pl.BlockSpec((1,) + half, lambda s, kc_ref: (kc_ref[0], 0, 0)) for half in halves]
            + [pl.BlockSpec((N_CHIPS - 1,) + half, lambda s, kc_ref: (0, 0, 0)) for half in halves],
            out_specs=[out_spec(shape, half) for shape, half in zip(shapes, halves)]),
        out_shape=[jax.ShapeDtypeStruct(tuple(shape), F32) for shape in shapes],
        compiler_params=_params(dimension_semantics=("arbitrary",)),
    )(chip_core, *parts, *recvs)


def _adamw_math(w, g, m, v):
    m = ADAM_B1 * m + (1.0 - ADAM_B1) * g
    v = ADAM_B2 * v + (1.0 - ADAM_B2) * (g * g)
    m_hat = m / (1.0 - ADAM_B1 ** ADAM_STEP)
    v_hat = v / (1.0 - ADAM_B2 ** ADAM_STEP)
    delta = -ADAM_LR * (m_hat / (jnp.sqrt(v_hat) + ADAM_EPS) + ADAM_WD * w)
    return delta, m, v


ADAM_PARTS = 4


def _adamw_matrix(w, g, m, v, name):
    rows, cols = w.shape
    by_rows = rows % (8 * ADAM_PARTS) == 0
    block = (rows // ADAM_PARTS, cols) if by_rows else (rows, cols // ADAM_PARTS)

    def body(w_ref, g_ref, m_ref, v_ref, go_ref, d_ref, mo_ref, vo_ref):
        gv = g_ref[...]
        go_ref[...] = gv
        d_ref[...], mo_ref[...], vo_ref[...] = _adamw_math(w_ref[...], gv, m_ref[...], v_ref[...])

    spec = pl.BlockSpec(block, lambda i: (i, 0) if by_rows else (0, i))
    shape = jax.ShapeDtypeStruct((rows, cols), F32)
    return pl.pallas_call(
        body, name=name, grid=(ADAM_PARTS,), in_specs=[spec] * 4, out_specs=[spec] * 4, out_shape=[shape] * 4,
        compiler_params=_params(dimension_semantics=("arbitrary",)),
    )(w, g, m, v)


def _adamw_small(ws, gs, ms, vs):
    n = len(ws)

    def body(*refs):
        for i in range(n):
            w_ref, g_ref, m_ref, v_ref = (refs[k * n + i] for k in range(4))
            d_ref, mo_ref, vo_ref = (refs[(4 + k) * n + i] for k in range(3))
            d_ref[...], mo_ref[...], vo_ref[...] = _adamw_math(w_ref[...], g_ref[...], m_ref[...], v_ref[...])

    shapes = [jax.ShapeDtypeStruct(w.shape, F32) for w in ws]
    out = pl.pallas_call(body, name="adamw_small", out_shape=shapes * 3, compiler_params=_params())(*ws, *gs, *ms, *vs)
    return out[:n], out[n:2 * n], out[2 * n:]


MATRICES = ["ffn1_w13", "ffn1_w2", "w_in", "w_out", "ffn2_w13", "ffn2_w2"]
VECTORS = ["ffn1_norm", "mix_norm", "conv_b", "conv_ln_g", "conv_ln_b", "forget_b", "out_norm_conv",
           "out_norm_attn", "ffn2_norm", "final_norm"]
WEIGHTS = ["ffn1_norm", "ffn1_w13", "ffn1_w2", "mix_norm", "w_in", "conv_w", "conv_b", "conv_ln_g", "conv_ln_b",
           "forget_b", "out_norm_conv", "out_norm_attn", "w_out", "ffn2_norm", "ffn2_w13", "ffn2_w2", "final_norm"]


def _pack_small(g, names):
    rows, layout = [], []
    for n in names:
        flat = g[n].reshape(-1)
        pad = (-flat.shape[0]) % LANES
        rows.append(jnp.pad(flat, (0, pad)).reshape(-1, LANES))
        layout.append((n, g[n].shape, flat.shape[0], rows[-1].shape[0]))
    packed = jnp.concatenate(rows, axis=0)
    pad_rows = (-packed.shape[0]) % 8
    return jnp.pad(packed, ((0, pad_rows), (0, 0))), layout


def _unpack_small(packed, layout):
    out, r = {}, 0
    for n, shape, size, nrows in layout:
        out[n] = packed[r:r + nrows].reshape(-1)[:size].reshape(shape)
        r += nrows
    return out


def kernel(x, ffn1_norm, ffn1_w13, ffn1_w2, mix_norm, w_in, conv_w, conv_b, conv_ln_g, conv_ln_b, forget_b, out_norm_conv, out_norm_attn, w_out, ffn2_norm, ffn2_w13, ffn2_w2, final_norm, loss_target, m_ffn1_norm, m_ffn1_w13, m_ffn1_w2, m_mix_norm, m_w_in, m_conv_w, m_conv_b, m_conv_ln_g, m_conv_ln_b, m_forget_b, m_out_norm_conv, m_out_norm_attn, m_w_out, m_ffn2_norm, m_ffn2_w13, m_ffn2_w2, m_final_norm, v_ffn1_norm, v_ffn1_w13, v_ffn1_w2, v_mix_norm, v_w_in, v_conv_w, v_conv_b, v_conv_ln_g, v_conv_ln_b, v_forget_b, v_out_norm_conv, v_out_norm_attn, v_w_out, v_ffn2_norm, v_ffn2_w13, v_ffn2_w2, v_final_norm):
    args = dict(locals())
    weights = {n: args[n] for n in WEIGHTS}
    core = lax.axis_index("c").astype(jnp.int32).reshape(1)
    chip = (2 * lax.axis_index("x") + lax.axis_index("y")).astype(jnp.int32)
    chip1 = chip.reshape(1)
    chip_core = jnp.concatenate([chip1, core])

    def held(n, a):
        return a[0].T if n == "w_in" else a[0]

    def given(n, a):
        return (a.T if n == "w_in" else a)[None]

    def slot(n, deps=()):
        if n == "conv_w":
            rows = jnp.pad(conv_w[0], ((0, CONV_PAD - CONV_WIDTH), (0, 0)))
            return _into_slot(rows, chip1, F32, "slot_conv_w", deps)
        return _into_slot(held(n, weights[n]), chip1, BF16, "slot_" + n, deps)

    fetched = {"ffn1": ["ffn1_w13", "ffn1_w2"], "mix": ["w_in", "w_out", "conv_w"], "ffn2": ["ffn2_w13", "ffn2_w2"]}
    fetch = {}

    def as_weights(group, bufs):
        out = {}
        for n, b in zip(fetched[group], bufs):
            if n.endswith("w13"):
                out[n] = b
            elif n != "conv_w":
                out[n] = b.reshape(N_CHIPS * b.shape[1], b.shape[2])
            else:
                out[n] = b[:, :CONV_WIDTH].transpose(1, 0, 2).reshape(CONV_WIDTH, D_CONV)
        return out

    def get_weights(group, after):
        if group == "ffn1":
            first = [slot(n) for n in fetched[group]]
            plan = _ici_gather_plan(first)
            started = _split_copy_start("gather_ffn1_start", first, 3 * len(first), plan)
            later_names = fetched["mix"] + fetched["ffn2"]
            later = [slot(n, [started[3]]) for n in later_names]
            landed = _split_copy_wait("gather_ffn1_wait", started, plan, [], passed=later)
            bufs = _gather_shards(landed[:len(first)], "forward_ffn1", ici=False)
            behind = dict(zip(later_names, landed[len(first):]))
            for later in ("mix", "ffn2"):
                bufs_later = [behind[n] for n in fetched[later]]
                plan = _ici_gather_plan(bufs_later)
                fetch[later] = plan, _split_copy_start("gather_%s_start" % later, bufs_later, 3 * len(bufs_later), plan)
            return as_weights(group, bufs), [fetch["mix"][1][3], fetch["ffn2"][1][3]]
        plan, started = fetch[group.split(":")[0]]
        if group == "ffn2:landed":
            landed = _split_copy_wait("gather_ffn2_wait", started, plan, [after])
            plan = _d2d_forward_plan(landed)
            fetch["ffn2"] = plan, _split_copy_start("forward_ffn2_start", landed, 3 * len(landed), plan)
            return {}, [fetch["ffn2"][1][3]]
        if group == "ffn2":
            return as_weights(group, _split_copy_wait("forward_ffn2_wait", started, plan, [after])), []
        landed = _split_copy_wait("gather_%s_wait" % group, started, plan, [after])
        return as_weights(group, _gather_shards(landed, "forward_" + group, ici=False)), []

    def shard_major(n, g):
        return g if n.endswith("w13") else g.reshape(N_CHIPS, g.shape[0] // N_CHIPS, g.shape[1])

    exchange, scatter = {}, {}
    small_names = VECTORS + ["conv_w"]
    small = {}

    def put_grads(group, grads):
        if group == "small":
            packed, layout = _pack_small(grads, small_names + ["loss"])
            total = _all_reduce_small(packed)
            small.update(_unpack_small(total, layout))
            return [total]
        names = list(grads)
        local = [shard_major(n, grads[n]) for n in names]
        landing = [lax.empty((N_CHIPS,) + _half_shape(*a.shape[1:]), BF16) for a in local]
        plan = _pair_exchange_plan(local)
        exchange[group] = names, plan, _split_copy_start("exchange_%s_start" % group, local + landing, len(local), plan)
        return [exchange[group][2][3]]

    def flush_grads(group, after):
        names, plan, started = exchange[group]
        done = _split_copy_wait("exchange_%s_wait" % group, started, plan, after)
        local, sib = done[:len(names)], done[len(names):]
        parts = list(_pair_add(local, sib, core, "pair_add_" + group))
        landing = [lax.empty((N_CHIPS - 1,) + q.shape[1:], BF16) for q in parts]
        plan = _ici_scatter_plan(len(parts))
        shapes = [a.shape[1:] for a in local]
        scatter[group] = names, plan, _split_copy_start("scatter_%s_start" % group, parts + landing, 3 * len(parts), plan), shapes
        return [scatter[group][2][3]]

    p = {n: weights[n] for n in VECTORS}
    p["final_norm"] = final_norm.reshape(1, D_MODEL)
    dx = _local_step(x[0], loss_target[0], p, get_weights, put_grads, flush_grads)
    loss = small["loss"].reshape(())

    grad = {n: small[n] for n in VECTORS}
    grad["final_norm"] = small["final_norm"].reshape(D_MODEL)
    grad["conv_w"] = lax.dynamic_slice_in_dim(small["conv_w"], chip * (D_CONV // N_CHIPS), D_CONV // N_CHIPS, axis=1)[None]

    delta, new_m, new_v = {}, {}, {}

    def reduce_chips(group, after):
        names, plan, started, shapes = scatter[group]
        done = _split_copy_wait("scatter_%s_wait" % group, started, plan, after)
        parts, landed = done[:len(names)], done[len(names):]
        return list(_chip_add(parts, landed, chip_core, shapes, "chip_add_" + group))

    def update(group, full):
        ends = []
        for n, reduced in zip(scatter[group][0], full):
            go, d, mo, vo = _adamw_matrix(held(n, weights[n]), reduced, held(n, args["m_" + n]), held(n, args["v_" + n]),
                                          "adamw_" + n)
            grad[n], delta[n], new_m[n], new_v[n] = given(n, go), given(n, d), given(n, mo), given(n, vo)
            ends.append(vo)
        return ends

    def share_start(group, halves):
        plan = _pair_share_plan(scatter[group][3])
        return plan, _split_copy_start("share_%s_start" % group, halves, len(halves), plan)

    halves_ffn2 = reduce_chips("ffn2", [exchange["ffn1_w13"][2][3]])
    plan_ffn2, share_ffn2 = share_start("ffn2", halves_ffn2)
    last_scatter = flush_grads("ffn1_w13", [share_ffn2[3]])
    halves_mix = reduce_chips("mix", last_scatter)
    plan_mix, share_mix = share_start("mix", halves_mix)
    done_ffn2 = update("ffn2", _split_copy_wait("share_ffn2_wait", share_ffn2, plan_ffn2, [share_mix[3]]))
    done_mix = update("mix", _split_copy_wait("share_mix_wait", share_mix, plan_mix, done_ffn2))
    as2d = lambda a: a.reshape(-1, a.shape[-1])
    ds, mos, vos = _adamw_small([as2d(weights[n]) for n in small_names], [as2d(grad[n]) for n in small_names],
                                [as2d(args["m_" + n]) for n in small_names], [as2d(args["v_" + n]) for n in small_names])
    for n, d, mo, vo in zip(small_names, ds, mos, vos):
        shape = weights[n].shape
        delta[n], new_m[n], new_v[n] = d.reshape(shape), mo.reshape(shape), vo.reshape(shape)
    behind = done_ffn2 + done_mix + [vos[0]]
    halves_w2 = reduce_chips("ffn1_w2", behind)
    halves_w13 = reduce_chips("ffn1_w13", behind)
    full_w2, full_w13 = _pair_share(halves_w2 + halves_w13, "pair_share_ffn1")
    update("ffn1_w2", [full_w2])
    update("ffn1_w13", [full_w13])

    return (loss, dx[None], *[grad[n] for n in WEIGHTS], *[delta[n] for n in WEIGHTS],
            *[new_m[n] for n in WEIGHTS], *[new_v[n] for n in WEIGHTS])
```

```python
import functools

import jax
import jax.numpy as jnp
from jax import lax
from jax.experimental import pallas as pl
from jax.experimental.pallas import tpu as pltpu

F32 = jnp.float32
BF16 = jnp.bfloat16

D_MODEL = 1024
D_FF = 2816
FF_SHARD = D_FF // 2
D_CONV = 512
D_ATTN = 512
N_HEADS = 8
HEAD_DIM = 64
CONV_WIDTH = 31
CONV_PAD = 32
N_IN = 2 * D_CONV + 3 * D_ATTN + N_HEADS
EPS = 1e-6
N_CHIPS = 4
LANES = 128
TOKEN_ROWS = 512
HEAD_ROWS = 16

ADAM_LR = 0.001
ADAM_B1 = 0.9
ADAM_B2 = 0.999
ADAM_EPS = 1e-08
ADAM_WD = 0.01
ADAM_STEP = 10

VMEM_LIMIT = 56 * 1024 * 1024

_NT = (((1,), (1,)), ((), ()))
_TN = (((0,), (0,)), ((), ()))


def _dot(a, b):
    return jnp.dot(a, b, preferred_element_type=F32)


def _dot_nt(a, b):
    return lax.dot_general(a, b, _NT, preferred_element_type=F32)


def _dot_tn(a, b):
    return lax.dot_general(a, b, _TN, preferred_element_type=F32)


def _params(**kw):
    return pltpu.CompilerParams(vmem_limit_bytes=VMEM_LIMIT, **kw)


def _sigmoid(x):
    return 1.0 / (1.0 + jnp.exp(-x))


def _rms_stats(x):
    return lax.rsqrt(jnp.mean(x * x, axis=-1, keepdims=True) + EPS)


def _rms_bwd(x, r, g, dh):
    t = dh * g
    dx = r * t - x * (r * r * r) * jnp.mean(t * x, axis=-1, keepdims=True)
    return dx, dh * x * r


def _silu_grad(z, sg):
    return sg * (1.0 + z * (1.0 - sg))


def _row_spec(tm, n):
    return pl.BlockSpec((tm, n), lambda i: (i, 0))


def _full_spec(shape):
    nd = len(shape)
    return pl.BlockSpec(shape, lambda i: (0,) * nd)


_ANY = pl.BlockSpec(memory_space=pl.ANY)


def _skip(n, body):
    return lambda *refs: body(*refs[n:])


FFN_ROWS = 256
FFN_WEIGHT_PARTS = N_CHIPS + 2


def _with_ffn_weights(w13_hbm, w2_hbm, w13_ref, w2_ref, sems, order, tile):
    first = pl.program_id(0) == 0
    copies = {}
    if w13_hbm is not None:
        for k in range(N_CHIPS):
            copies["w13", k] = pltpu.make_async_copy(w13_hbm.at[k], w13_ref.at[k], sems.at[k])
    if w2_hbm is not None:
        for half in range(2):
            rows = pl.ds(half * FF_SHARD, FF_SHARD)
            copies["w2", half] = pltpu.make_async_copy(w2_hbm.at[rows, :], w2_ref.at[rows, :], sems.at[N_CHIPS + half])

    @pl.when(first)
    def _():
        for part in order:
            copies[part].start()

        def ready(*parts):
            for part in parts:
                copies[part].wait()

        tile(ready)

    @pl.when(jnp.logical_not(first))
    def _():
        tile(lambda *parts: None)


def _ffn_fwd(x, g, w13s, w2, name, deps=()):
    t = x.shape[0]
    tm = FFN_ROWS
    deps = tuple(deps)

    def body(x_ref, g_ref, w13_hbm, w2_hbm, xo_ref, h_ref, gu_ref, w13_ref, w2_ref, sems):
        def tile(ready):
            xv = x_ref[...]
            hb = (xv * _rms_stats(xv) * g_ref[...]).astype(BF16)
            h_ref[...] = hb
            acc = jnp.zeros((tm, D_MODEL), F32)
            for half in range(2):
                lo = half * FF_SHARD
                ready(("w13", half), ("w13", 2 + half))
                gate = _dot(hb, w13_ref[half])
                up = _dot(hb, w13_ref[2 + half])
                gu_ref[:, lo:lo + FF_SHARD] = gate.astype(BF16)
                gu_ref[:, D_FF + lo:D_FF + lo + FF_SHARD] = up.astype(BF16)
                a = (gate * _sigmoid(gate) * up).astype(BF16)
                ready(("w2", half))
                acc = acc + _dot(a, w2_ref[lo:lo + FF_SHARD, :])
            xo_ref[...] = xv + 0.5 * acc

        _with_ffn_weights(w13_hbm, w2_hbm, w13_ref, w2_ref, sems,
                          [("w13", 0), ("w13", 2), ("w2", 0), ("w13", 1), ("w13", 3), ("w2", 1)], tile)

    return pl.pallas_call(
        _skip(len(deps), body), name=name, grid=(t // tm,),
        in_specs=[_ANY] * len(deps) + [_row_spec(tm, D_MODEL), _full_spec((1, D_MODEL)), _ANY, _ANY],
        out_specs=[_row_spec(tm, D_MODEL), _row_spec(tm, D_MODEL), _row_spec(tm, 2 * D_FF)],
        out_shape=[jax.ShapeDtypeStruct((t, D_MODEL), F32), jax.ShapeDtypeStruct((t, D_MODEL), BF16),
                   jax.ShapeDtypeStruct((t, 2 * D_FF), BF16)],
        scratch_shapes=[pltpu.VMEM(w13s.shape, BF16), pltpu.VMEM(w2.shape, BF16),
                        pltpu.SemaphoreType.DMA((FFN_WEIGHT_PARTS,))],
        compiler_params=_params(dimension_semantics=("arbitrary",)),
    )(*deps, x, g, w13s, w2)


def _ffn_up(x, g, w13s, name, deps=()):
    t = x.shape[0]
    tm = FFN_ROWS
    deps = tuple(deps)

    def body(x_ref, g_ref, w13_hbm, h_ref, gu_ref, a_ref, w13_ref, sems):
        def tile(ready):
            xv = x_ref[...]
            hb = (xv * _rms_stats(xv) * g_ref[...]).astype(BF16)
            h_ref[...] = hb
            for half in range(2):
                lo = half * FF_SHARD
                ready(("w13", half), ("w13", 2 + half))
                gate = _dot(hb, w13_ref[half])
                up = _dot(hb, w13_ref[2 + half])
                gu_ref[:, lo:lo + FF_SHARD] = gate.astype(BF16)
                gu_ref[:, D_FF + lo:D_FF + lo + FF_SHARD] = up.astype(BF16)
                a_ref[:, lo:lo + FF_SHARD] = (gate * _sigmoid(gate) * up).astype(BF16)

        _with_ffn_weights(w13_hbm, None, w13_ref, None, sems, [("w13", 0), ("w13", 2), ("w13", 1), ("w13", 3)], tile)

    return pl.pallas_call(
        _skip(len(deps), body), name=name, grid=(t // tm,),
        in_specs=[_ANY] * len(deps) + [_row_spec(tm, D_MODEL), _full_spec((1, D_MODEL)), _ANY],
        out_specs=[_row_spec(tm, D_MODEL), _row_spec(tm, 2 * D_FF), _row_spec(tm, D_FF)],
        out_shape=[jax.ShapeDtypeStruct((t, D_MODEL), BF16), jax.ShapeDtypeStruct((t, 2 * D_FF), BF16),
                   jax.ShapeDtypeStruct((t, D_FF), BF16)],
        scratch_shapes=[pltpu.VMEM(w13s.shape, BF16), pltpu.SemaphoreType.DMA((FFN_WEIGHT_PARTS,))],
        compiler_params=_params(dimension_semantics=("arbitrary",)),
    )(*deps, x, g, w13s)


def _ffn_down(x, a, w2, name):
    t = x.shape[0]
    tm = FFN_ROWS

    def body(x_ref, a_ref, w2_hbm, xo_ref, w2_ref, sems):
        def tile(ready):
            ready(("w2", 0))
            acc = _dot(a_ref[:, 0:FF_SHARD], w2_ref[0:FF_SHARD, :])
            ready(("w2", 1))
            acc = acc + _dot(a_ref[:, FF_SHARD:], w2_ref[FF_SHARD:, :])
            xo_ref[...] = x_ref[...] + 0.5 * acc

        _with_ffn_weights(None, w2_hbm, None, w2_ref, sems, [("w2", 0), ("w2", 1)], tile)

    return pl.pallas_call(
        body, name=name, grid=(t // tm,),
        in_specs=[_row_spec(tm, D_MODEL), _row_spec(tm, D_FF), _ANY],
        out_specs=_row_spec(tm, D_MODEL), out_shape=jax.ShapeDtypeStruct((t, D_MODEL), F32),
        scratch_shapes=[pltpu.VMEM(w2.shape, BF16), pltpu.SemaphoreType.DMA((FFN_WEIGHT_PARTS,))],
        compiler_params=_params(dimension_semantics=("arbitrary",)),
    )(x, a, w2)


def _ffn_bwd(dy, x, gu, g, w13s, w2, name, deps=()):
    t = x.shape[0]
    tm = FFN_ROWS
    deps = tuple(deps)

    def body(dy_ref, x_ref, gu_ref, g_ref, w13_hbm, w2_hbm, dx_ref, dgu_ref, a_ref, dg_ref, dyh_ref, dxb_ref,
             w13_ref, w2_ref, sems):
        @pl.when(pl.program_id(0) == 0)
        def _():
            dg_ref[...] = jnp.zeros_like(dg_ref)

        def tile(ready):
            dyv = dy_ref[...]
            dyh = (0.5 * dyv).astype(BF16)
            dyh_ref[...] = dyh
            dh = jnp.zeros((tm, D_MODEL), F32)
            for half in range(2):
                lo = half * FF_SHARD
                ready(("w2", half))
                da = _dot_nt(dyh, w2_ref[lo:lo + FF_SHARD, :])
                gate = gu_ref[:, lo:lo + FF_SHARD].astype(F32)
                up = gu_ref[:, D_FF + lo:D_FF + lo + FF_SHARD].astype(F32)
                sg = _sigmoid(gate)
                act = gate * sg
                a_ref[:, lo:lo + FF_SHARD] = (act * up).astype(BF16)
                dgate = (da * up * _silu_grad(gate, sg)).astype(BF16)
                dup = (da * act).astype(BF16)
                dgu_ref[:, lo:lo + FF_SHARD] = dgate
                dgu_ref[:, D_FF + lo:D_FF + lo + FF_SHARD] = dup
                ready(("w13", half), ("w13", 2 + half))
                dh = dh + _dot_nt(dgate, w13_ref[half]) + _dot_nt(dup, w13_ref[2 + half])
            xv = x_ref[...]
            dxn, dg_rows = _rms_bwd(xv, _rms_stats(xv), g_ref[...], dh)
            dx = dyv + dxn
            dx_ref[...] = dx
            dxb_ref[...] = dx.astype(BF16)
            dg_ref[...] += jnp.sum(dg_rows, axis=0, keepdims=True)

        _with_ffn_weights(w13_hbm, w2_hbm, w13_ref, w2_ref, sems,
                          [("w2", 0), ("w13", 0), ("w13", 2), ("w2", 1), ("w13", 1), ("w13", 3)], tile)

    return pl.pallas_call(
        _skip(len(deps), body), name=name, grid=(t // tm,),
        in_specs=[_ANY] * len(deps) + [_row_spec(tm, D_MODEL), _row_spec(tm, D_MODEL), _row_spec(tm, 2 * D_FF),
                                       _full_spec((1, D_MODEL)), _ANY, _ANY],
        out_specs=[_row_spec(tm, D_MODEL), _row_spec(tm, 2 * D_FF), _row_spec(tm, D_FF),
                   _full_spec((1, D_MODEL)), _row_spec(tm, D_MODEL), _row_spec(tm, D_MODEL)],
        out_shape=[jax.ShapeDtypeStruct((t, D_MODEL), F32), jax.ShapeDtypeStruct((t, 2 * D_FF), BF16),
                   jax.ShapeDtypeStruct((t, D_FF), BF16), jax.ShapeDtypeStruct((1, D_MODEL), F32),
                   jax.ShapeDtypeStruct((t, D_MODEL), BF16), jax.ShapeDtypeStruct((t, D_MODEL), BF16)],
        scratch_shapes=[pltpu.VMEM(w13s.shape, BF16), pltpu.VMEM(w2.shape, BF16),
                        pltpu.SemaphoreType.DMA((FFN_WEIGHT_PARTS,))],
        compiler_params=_params(dimension_semantics=("arbitrary",)),
    )(*deps, dy, x, gu, g, w13s, w2)


WGRAD_ROWS = (512, 384, 256)


def _wgrad(a, b, n_blocks, name, deps=()):
    t, m = a.shape
    tm = next(rows for rows in WGRAD_ROWS if m % rows == 0)
    n = b.shape[1]
    bn = n // n_blocks
    deps = tuple(deps)
    assert a.dtype == BF16 and b.dtype == BF16

    def body(a_ref, b_ref, o_ref):
        o_ref[0] = _dot_tn(a_ref[...], b_ref[...]).astype(BF16)

    return pl.pallas_call(
        _skip(len(deps), body), name=name, grid=(n_blocks, m // tm),
        in_specs=[_ANY] * len(deps) + [pl.BlockSpec((t, tm), lambda j, i: (0, i)),
                                       pl.BlockSpec((t, bn), lambda j, i: (0, j))],
        out_specs=pl.BlockSpec((1, tm, bn), lambda j, i: (j, i, 0)),
        out_shape=jax.ShapeDtypeStruct((n_blocks, m, bn), BF16),
        compiler_params=_params(dimension_semantics=("arbitrary", "arbitrary")),
    )(*deps, a, b)


def _mix_proj(x, g, w_ag, w_qkv, w_f):
    t = x.shape[0]
    tm = TOKEN_ROWS

    def body(x_ref, g_ref, wag_ref, wqkv_ref, wf_ref, h_ref, ag_ref, qkv_ref, fl_ref):
        xv = x_ref[...]
        hb = (xv * _rms_stats(xv) * g_ref[...]).astype(BF16)
        h_ref[...] = hb
        ag_ref[...] = _dot_nt(hb, wag_ref[...])
        qkv_ref[...] = _dot_nt(hb, wqkv_ref[...]).astype(BF16)
        fl_ref[...] = _dot_nt(hb, wf_ref[...])

    return pl.pallas_call(
        body, name="mix_proj", grid=(t // tm,),
        in_specs=[_row_spec(tm, D_MODEL), _full_spec((1, D_MODEL)), _full_spec(w_ag.shape),
                  _full_spec(w_qkv.shape), _full_spec(w_f.shape)],
        out_specs=[_row_spec(tm, D_MODEL), _row_spec(tm, 2 * D_CONV), _row_spec(tm, 3 * D_ATTN),
                   _row_spec(tm, LANES)],
        out_shape=[jax.ShapeDtypeStruct((t, D_MODEL), BF16), jax.ShapeDtypeStruct((t, 2 * D_CONV), F32),
                   jax.ShapeDtypeStruct((t, 3 * D_ATTN), BF16), jax.ShapeDtypeStruct((t, LANES), F32)],
        compiler_params=_params(dimension_semantics=("arbitrary",)),
    )(x, g, w_ag, w_qkv, w_f)


def _mix_proj_bwd(dproj, dx2, x1, g, w_ag, w_qkv, w_f):
    t = x1.shape[0]
    tm = TOKEN_ROWS
    n_ag, n_qkv = 2 * D_CONV, 3 * D_ATTN

    def body(dp_ref, dx2_ref, x_ref, g_ref, wag_ref, wqkv_ref, wf_ref, dx_ref, dg_ref):
        @pl.when(pl.program_id(0) == 0)
        def _():
            dg_ref[...] = jnp.zeros_like(dg_ref)

        dh = (_dot(dp_ref[:, 0:n_ag], wag_ref[...]) + _dot(dp_ref[:, n_ag:n_ag + n_qkv], wqkv_ref[...])
              + _dot(dp_ref[:, n_ag + n_qkv:], wf_ref[...]))
        xv = x_ref[...]
        dxn, dg_rows = _rms_bwd(xv, _rms_stats(xv), g_ref[...], dh)
        dx_ref[...] = dx2_ref[...] + dxn
        dg_ref[...] += jnp.sum(dg_rows, axis=0, keepdims=True)

    return pl.pallas_call(
        body, name="mix_proj_bwd", grid=(t // tm,),
        in_specs=[_row_spec(tm, dproj.shape[1]),
                  _row_spec(tm, D_MODEL), _row_spec(tm, D_MODEL), _full_spec((1, D_MODEL)),
                  _full_spec(w_ag.shape), _full_spec(w_qkv.shape), _full_spec(w_f.shape)],
        out_specs=[_row_spec(tm, D_MODEL), _full_spec((1, D_MODEL))],
        out_shape=[jax.ShapeDtypeStruct((t, D_MODEL), F32), jax.ShapeDtypeStruct((1, D_MODEL), F32)],
        compiler_params=_params(dimension_semantics=("arbitrary",)),
    )(dproj, dx2, x1, g, w_ag, w_qkv, w_f)


def _split3(x):
    hi = x.astype(BF16)
    r1 = x - hi.astype(F32)
    mid = r1.astype(BF16)
    lo = (r1 - mid.astype(F32)).astype(BF16)
    return hi, mid, lo


def _gates_fwd(flt, fb):
    t = flt.shape[1]

    def body(f_ref, b_ref, d_ref):
        z = f_ref[...] + b_ref[...]
        logf = jnp.minimum(z, 0.0) - jnp.log(1.0 + jnp.exp(-jnp.abs(z)))
        row = lax.broadcasted_iota(jnp.int32, (LANES, LANES), 0)
        col = lax.broadcasted_iota(jnp.int32, (LANES, LANES), 1)
        upper = (row <= col).astype(BF16)
        carry = jnp.zeros((HEAD_ROWS, 1), F32)
        for blk in range(t // LANES):
            hi, mid, lo = _split3(logf[:, blk * LANES:(blk + 1) * LANES])
            cs = _dot(hi, upper) + _dot(mid, upper) + _dot(lo, upper)
            d_ref[:, blk * LANES:(blk + 1) * LANES] = cs + carry
            carry = carry + cs[:, LANES - 1:LANES]

    return pl.pallas_call(
        body, name="gates_fwd", out_shape=jax.ShapeDtypeStruct((HEAD_ROWS, t), F32),
        compiler_params=_params(),
    )(flt, fb)


def _gates_bwd(dd, flt, fb):
    t = flt.shape[1]

    def body(dd_ref, f_ref, b_ref, df_ref, db_ref):
        z = f_ref[...] + b_ref[...]
        row = lax.broadcasted_iota(jnp.int32, (LANES, LANES), 0)
        col = lax.broadcasted_iota(jnp.int32, (LANES, LANES), 1)
        lower = (row >= col).astype(BF16)
        carry = jnp.zeros((HEAD_ROWS, 1), F32)
        db = jnp.zeros((HEAD_ROWS, 1), F32)
        for blk in reversed(range(t // LANES)):
            sl = slice(blk * LANES, (blk + 1) * LANES)
            hi, mid, lo = _split3(dd_ref[:, sl])
            cs = _dot(hi, lower) + _dot(mid, lower) + _dot(lo, lower)
            dz = (cs + carry) * _sigmoid(-z[:, sl])
            df_ref[:, sl] = dz
            db = db + jnp.sum(dz, axis=1, keepdims=True)
            carry = carry + cs[:, 0:1]
        db_ref[...] = db

    return pl.pallas_call(
        body, name="gates_bwd",
        out_shape=[jax.ShapeDtypeStruct((HEAD_ROWS, t), F32), jax.ShapeDtypeStruct((HEAD_ROWS, 1), F32)],
        compiler_params=_params(),
    )(dd, flt, fb)


CONV_CHUNK = 128
CONV_TAIL = 16
CONV_WINDOW = CONV_CHUNK + CONV_PAD + 8
CONV_ROWS_EXTRA = CONV_PAD + CONV_TAIL
SUBLANES = 8


def _conv_rows(ag_ref, u_ref, t):
    u_ref[0:CONV_PAD, :] = jnp.zeros((CONV_PAD, D_CONV), F32)
    u_ref[CONV_PAD + t:CONV_ROWS_EXTRA + t, :] = jnp.zeros((CONV_TAIL, D_CONV), F32)

    def fill(i, c):
        r0 = pl.multiple_of(i * CONV_CHUNK, CONV_CHUNK)
        a = ag_ref[pl.ds(r0, CONV_CHUNK), 0:D_CONV]
        gt = ag_ref[pl.ds(r0, CONV_CHUNK), D_CONV:2 * D_CONV]
        u_ref[pl.ds(CONV_PAD + r0, CONV_CHUNK), :] = a * _sigmoid(gt)
        return c

    lax.fori_loop(0, t // CONV_CHUNK, fill, 0)


def _for_shifted(ref, r0, offsets, fn):
    window = ref[pl.ds(r0, CONV_WINDOW), :]
    for rem in range(SUBLANES):
        mine = [o for o in offsets if o % SUBLANES == rem]
        if not mine:
            continue
        turned = window if rem == 0 else pltpu.roll(window, CONV_WINDOW - rem, 0)
        for o in mine:
            fn(o, turned[o - rem:o - rem + CONV_CHUNK])


def _conv_point(u_ref, r0, w_ref, cb, lg, lb):
    acc = [jnp.zeros((CONV_CHUNK, D_CONV), F32)]

    def tap(o, rows):
        j = o - (CONV_PAD - CONV_WIDTH + 1)
        acc[0] = acc[0] + w_ref[j:j + 1, :] * rows

    _for_shifted(u_ref, r0, [j + CONV_PAD - CONV_WIDTH + 1 for j in range(CONV_WIDTH)], tap)
    y = acc[0] + cb
    mu = jnp.mean(y, axis=-1, keepdims=True)
    yc = y - mu
    rstd = lax.rsqrt(jnp.mean(yc * yc, axis=-1, keepdims=True) + EPS)
    yhat = yc * rstd
    z = yhat * lg + lb
    sg = _sigmoid(z)
    s = z * sg
    rr = _rms_stats(s)
    return yhat, rstd, z, sg, s, rr


def _conv_fwd(ag, conv_w, conv_b, ln_g, ln_b, norm_g):
    t = ag.shape[0]

    def body(ag_ref, w_ref, cb_ref, lg_ref, lb_ref, ng_ref, o_ref, u_ref):
        _conv_rows(ag_ref, u_ref, t)
        cb, lg, lb, ng = cb_ref[...], lg_ref[...], lb_ref[...], ng_ref[...]

        def chunk(i, c):
            r0 = pl.multiple_of(i * CONV_CHUNK, CONV_CHUNK)
            _, _, _, _, s, rr = _conv_point(u_ref, r0, w_ref, cb, lg, lb)
            o_ref[pl.ds(r0, CONV_CHUNK), :] = (s * rr * ng).astype(BF16)
            return c

        lax.fori_loop(0, t // CONV_CHUNK, chunk, 0)

    return pl.pallas_call(
        body, name="conv_fwd", out_shape=jax.ShapeDtypeStruct((t, D_CONV), BF16),
        scratch_shapes=[pltpu.VMEM((t + CONV_ROWS_EXTRA, D_CONV), F32)],
        compiler_params=_params(),
    )(ag, conv_w, conv_b, ln_g, ln_b, norm_g)


def _conv_bwd(ag, dout, conv_w, conv_b, ln_g, ln_b, norm_g):
    t = ag.shape[0]

    def body(ag_ref, do_ref, w_ref, cb_ref, lg_ref, lb_ref, ng_ref,
             dag_ref, dw_ref, dcb_ref, dlg_ref, dlb_ref, dng_ref, u_ref, dy_ref):
        _conv_rows(ag_ref, u_ref, t)
        dy_ref[t:t + CONV_ROWS_EXTRA, :] = jnp.zeros((CONV_ROWS_EXTRA, D_CONV), F32)
        cb, lg, lb, ng = cb_ref[...], lg_ref[...], lb_ref[...], ng_ref[...]
        dw_ref[...] = jnp.zeros_like(dw_ref)
        zero = jnp.zeros((1, D_CONV), F32)

        def chunk(i, carry):
            dcb, dlg, dlb, dng = carry
            r0 = pl.multiple_of(i * CONV_CHUNK, CONV_CHUNK)
            yhat, rstd, z, sg, s, rr = _conv_point(u_ref, r0, w_ref, cb, lg, lb)
            do = do_ref[pl.ds(r0, CONV_CHUNK), :]
            ds, dng_rows = _rms_bwd(s, rr, ng, do)
            dz = ds * _silu_grad(z, sg)
            dyhat = dz * lg
            dy = rstd * (dyhat - jnp.mean(dyhat, axis=-1, keepdims=True)
                         - yhat * jnp.mean(dyhat * yhat, axis=-1, keepdims=True))
            dy_ref[pl.ds(r0, CONV_CHUNK), :] = dy
            def tap(o, rows):
                j = o - (CONV_PAD - CONV_WIDTH + 1)
                dw_ref[j:j + 1, :] += jnp.sum(dy * rows, axis=0, keepdims=True)

            _for_shifted(u_ref, r0, [j + CONV_PAD - CONV_WIDTH + 1 for j in range(CONV_WIDTH)], tap)
            return (dcb + jnp.sum(dy, axis=0, keepdims=True), dlg + jnp.sum(dz * yhat, axis=0, keepdims=True),
                    dlb + jnp.sum(dz, axis=0, keepdims=True), dng + jnp.sum(dng_rows, axis=0, keepdims=True))

        dcb, dlg, dlb, dng = lax.fori_loop(0, t // CONV_CHUNK, chunk, (zero, zero, zero, zero))
        dcb_ref[...] = dcb
        dlg_ref[...] = dlg
        dlb_ref[...] = dlb
        dng_ref[...] = dng

        def chunk2(i, c):
            r0 = pl.multiple_of(i * CONV_CHUNK, CONV_CHUNK)
            acc = [jnp.zeros((CONV_CHUNK, D_CONV), F32)]

            def tap(o, rows):
                j = CONV_WIDTH - 1 - o
                acc[0] = acc[0] + w_ref[j:j + 1, :] * rows

            _for_shifted(dy_ref, r0, list(range(CONV_WIDTH)), tap)
            du = acc[0]
            a = ag_ref[pl.ds(r0, CONV_CHUNK), 0:D_CONV]
            gt = ag_ref[pl.ds(r0, CONV_CHUNK), D_CONV:2 * D_CONV]
            sg = _sigmoid(gt)
            dag_ref[pl.ds(r0, CONV_CHUNK), 0:D_CONV] = (du * sg).astype(BF16)
            dag_ref[pl.ds(r0, CONV_CHUNK), D_CONV:2 * D_CONV] = (du * a * sg * (1.0 - sg)).astype(BF16)
            return c

        lax.fori_loop(0, t // CONV_CHUNK, chunk2, 0)

    vec = jax.ShapeDtypeStruct((1, D_CONV), F32)
    return pl.pallas_call(
        body, name="conv_bwd",
        out_shape=[jax.ShapeDtypeStruct((t, 2 * D_CONV), BF16), jax.ShapeDtypeStruct((CONV_PAD, D_CONV), F32),
                   vec, vec, vec, vec],
        scratch_shapes=[pltpu.VMEM((t + CONV_ROWS_EXTRA, D_CONV), F32), pltpu.VMEM((t + CONV_ROWS_EXTRA, D_CONV), F32)],
        compiler_params=_params(),
    )(ag, dout, conv_w, conv_b, ln_g, ln_b, norm_g)


Q_ROWS = 256
ATTN_SCALE = HEAD_DIM ** -0.5
ATTN_AHEAD = 1


def _attn_specs(t):
    blk = lambda off: pl.BlockSpec((t, LANES), lambda p: (0, off + p))
    pairs = N_HEADS // 2
    return [blk(0), blk(pairs), blk(2 * pairs), pl.BlockSpec((2, 1, t), lambda p: (p, 0, 0))]


def _one_head(q2, mask):
    return jnp.where(mask, q2, jnp.zeros_like(q2)) * ATTN_SCALE


def _attn_scores(qs, k2, drow, r0, q1):
    s = _dot_nt(qs, k2) - drow
    rowi = lax.broadcasted_iota(jnp.int32, (q1 - r0, q1 - r0), 0)
    coli = lax.broadcasted_iota(jnp.int32, (q1 - r0, q1 - r0), 1)
    diag = jnp.where(coli <= rowi, s[:, r0:q1], -jnp.inf)
    return diag if r0 == 0 else jnp.concatenate([s[:, :r0], diag], axis=1)


def _attn_fwd(qkv, drow, deps=()):
    t = qkv.shape[0]
    deps = tuple(deps)

    def body(q_ref, k_ref, v_ref, dr_ref, o_ref, lse_ref):
        head_a = lax.broadcasted_iota(jnp.int32, (1, LANES), 1) < HEAD_DIM
        items = [(qb, hh) for qb in range(t // Q_ROWS) for hh in range(2)]

        def scores(item):
            qb, hh = item
            r0, q1 = qb * Q_ROWS, (qb + 1) * Q_ROWS
            qs = _one_head(q_ref[r0:q1, :], head_a if hh == 0 else ~head_a)
            return _attn_scores(qs, k_ref[0:q1, :], dr_ref[hh, :, 0:q1], r0, q1)

        ahead = [scores(item) for item in items[:ATTN_AHEAD]]
        outs = []
        for n, (qb, hh) in enumerate(items):
            r0, q1 = qb * Q_ROWS, (qb + 1) * Q_ROWS
            s = ahead.pop(0)
            if n + ATTN_AHEAD < len(items):
                ahead.append(scores(items[n + ATTN_AHEAD]))
            mx = jnp.max(s, axis=1, keepdims=True)
            p = jnp.exp(s - mx)
            l = jnp.sum(p, axis=1, keepdims=True)
            lse_ref[hh, r0:q1, :] = mx + jnp.log(l)
            outs.append(_dot(p.astype(BF16), v_ref[0:q1, :]) * (1.0 / l))
            if hh == 1:
                o_ref[r0:q1, :] = jnp.where(head_a, outs[0], outs[1])
                outs = []

    pairs = N_HEADS // 2
    return pl.pallas_call(
        _skip(len(deps), body), name="attn_fwd", grid=(pairs,), in_specs=[_ANY] * len(deps) + _attn_specs(t),
        out_specs=[pl.BlockSpec((t, LANES), lambda p: (0, p)), pl.BlockSpec((2, t, 1), lambda p: (p, 0, 0))],
        out_shape=[jax.ShapeDtypeStruct((t, D_ATTN), F32), jax.ShapeDtypeStruct((N_HEADS, t, 1), F32)],
        compiler_params=_params(dimension_semantics=("arbitrary",)),
    )(*deps, qkv, qkv, qkv, drow)


def _attn_bwd(qkv, drow, lse, do):
    t = qkv.shape[0]

    def body(q_ref, k_ref, v_ref, dr_ref, lse_ref, do_ref,
             dq_ref, dk_ref, dv_ref, dd_ref, dk_acc, dv_acc):
        head_a = lax.broadcasted_iota(jnp.int32, (1, LANES), 1) < HEAD_DIM
        dk_acc[...] = jnp.zeros_like(dk_acc)
        dv_acc[...] = jnp.zeros_like(dv_acc)
        dd_ref[...] = jnp.zeros_like(dd_ref)
        items = [(qb, hh) for qb in range(t // Q_ROWS) for hh in range(2)]

        def products(item):
            qb, hh = item
            r0, q1 = qb * Q_ROWS, (qb + 1) * Q_ROWS
            mask = head_a if hh == 0 else ~head_a
            qs = _one_head(q_ref[r0:q1, :], mask)
            dob = jnp.where(mask, do_ref[r0:q1, :], 0.0).astype(BF16)
            s = _attn_scores(qs, k_ref[0:q1, :], dr_ref[hh, :, 0:q1], r0, q1)
            return qs, dob, s, _dot_nt(dob, v_ref[0:q1, :])

        ahead = products(items[0])
        dqs = []
        for n, (qb, hh) in enumerate(items):
            r0, q1 = qb * Q_ROWS, (qb + 1) * Q_ROWS
            qs, dob, s, dp = ahead
            if n + 1 < len(items):
                ahead = products(items[n + 1])
            p = jnp.exp(s - lse_ref[hh, r0:q1, :])
            ds = p * (dp - jnp.sum(p * dp, axis=1, keepdims=True))
            dsb = ds.astype(BF16)
            dqs.append(_dot(dsb, k_ref[0:q1, :]) * ATTN_SCALE)
            dk_acc[0:q1, :] += _dot_tn(dsb, qs)
            dv_acc[0:q1, :] += _dot_tn(p.astype(BF16), dob)
            dd_ref[hh, :, 0:q1] -= jnp.sum(ds, axis=0, keepdims=True)
            if hh == 1:
                dq_ref[r0:q1, :] = jnp.where(head_a, dqs[0], dqs[1]).astype(BF16)
                dqs = []
        dk_ref[...] = dk_acc[...].astype(BF16)
        dv_ref[...] = dv_acc[...].astype(BF16)

    pairs = N_HEADS // 2
    col = pl.BlockSpec((t, LANES), lambda p: (0, p))
    grad = jax.ShapeDtypeStruct((t, D_ATTN), BF16)
    return pl.pallas_call(
        body, name="attn_bwd", grid=(pairs,),
        in_specs=_attn_specs(t) + [pl.BlockSpec((2, t, 1), lambda p: (p, 0, 0)), col],
        out_specs=[col, col, col, pl.BlockSpec((2, 1, t), lambda p: (p, 0, 0))],
        out_shape=[grad, grad, grad, jax.ShapeDtypeStruct((N_HEADS, 1, t), F32)],
        scratch_shapes=[pltpu.VMEM((t, LANES), F32), pltpu.VMEM((t, LANES), F32)],
        compiler_params=_params(dimension_semantics=("arbitrary",)),
    )(qkv, qkv, qkv, drow, lse, do)


def _out_proj(ycn, o, g_attn, w_out, x1, deps=()):
    t = x1.shape[0]
    tm = TOKEN_ROWS
    deps = tuple(deps)

    def body(yc_ref, o_ref, g_ref, w_ref, x_ref, xo_ref, ya_ref):
        ov = o_ref[...]
        ya = (ov * _rms_stats(ov) * g_ref[...]).astype(BF16)
        ya_ref[...] = ya
        xo_ref[...] = x_ref[...] + _dot(yc_ref[...], w_ref[0:D_CONV, :]) + _dot(ya, w_ref[D_CONV:, :])

    return pl.pallas_call(
        _skip(len(deps), body), name="out_proj", grid=(t // tm,),
        in_specs=[_ANY] * len(deps) + [_row_spec(tm, D_CONV), _row_spec(tm, D_ATTN), _full_spec((1, D_ATTN)),
                                       _full_spec(w_out.shape), _row_spec(tm, D_MODEL)],
        out_specs=[_row_spec(tm, D_MODEL), _row_spec(tm, D_ATTN)],
        out_shape=[jax.ShapeDtypeStruct((t, D_MODEL), F32), jax.ShapeDtypeStruct((t, D_ATTN), BF16)],
        compiler_params=_params(dimension_semantics=("arbitrary",)),
    )(*deps, ycn, o, g_attn, w_out, x1)


def _out_proj_bwd(dx2, o, g_attn, w_out, deps=()):
    t = dx2.shape[0]
    tm = TOKEN_ROWS
    deps = tuple(deps)

    def body(dx_ref, o_ref, g_ref, w_ref, dyc_ref, do_ref, dg_ref):
        @pl.when(pl.program_id(0) == 0)
        def _():
            dg_ref[...] = jnp.zeros_like(dg_ref)

        dxb = dx_ref[...]
        dyc_ref[...] = _dot_nt(dxb, w_ref[0:D_CONV, :])
        dya = _dot_nt(dxb, w_ref[D_CONV:, :])
        ov = o_ref[...]
        do, dg_rows = _rms_bwd(ov, _rms_stats(ov), g_ref[...], dya)
        do_ref[...] = do
        dg_ref[...] += jnp.sum(dg_rows, axis=0, keepdims=True)

    return pl.pallas_call(
        _skip(len(deps), body), name="out_proj_bwd", grid=(t // tm,),
        in_specs=[_ANY] * len(deps) + [_row_spec(tm, D_MODEL), _row_spec(tm, D_ATTN), _full_spec((1, D_ATTN)),
                                       _full_spec(w_out.shape)],
        out_specs=[_row_spec(tm, D_CONV), _row_spec(tm, D_ATTN), _full_spec((1, D_ATTN))],
        out_shape=[jax.ShapeDtypeStruct((t, D_CONV), F32), jax.ShapeDtypeStruct((t, D_ATTN), F32),
                   jax.ShapeDtypeStruct((1, D_ATTN), F32)],
        compiler_params=_params(dimension_semantics=("arbitrary",)),
    )(*deps, dx2, o, g_attn, w_out)


def _loss_bwd(x3, target, g):
    t = x3.shape[0]
    tm = TOKEN_ROWS

    def body(x_ref, t_ref, g_ref, loss_ref, dx_ref, dg_ref):
        @pl.when(pl.program_id(0) == 0)
        def _():
            loss_ref[...] = jnp.zeros_like(loss_ref)
            dg_ref[...] = jnp.zeros_like(dg_ref)

        xv = x_ref[...]
        r = _rms_stats(xv)
        gv = g_ref[...]
        err = xv * r * gv - t_ref[...]
        row = jnp.sum(err * err, axis=1, keepdims=True) * (0.5 / D_MODEL)
        loss_ref[...] += jnp.sum(row, axis=0, keepdims=True)
        dx, dg_rows = _rms_bwd(xv, r, gv, err * (1.0 / D_MODEL))
        dx_ref[...] = dx
        dg_ref[...] += jnp.sum(dg_rows, axis=0, keepdims=True)

    return pl.pallas_call(
        body, name="loss_bwd", grid=(t // tm,),
        in_specs=[_row_spec(tm, D_MODEL), _row_spec(tm, D_MODEL), _full_spec((1, D_MODEL))],
        out_specs=[_full_spec((1, LANES)), _row_spec(tm, D_MODEL), _full_spec((1, D_MODEL))],
        out_shape=[jax.ShapeDtypeStruct((1, LANES), F32), jax.ShapeDtypeStruct((t, D_MODEL), F32),
                   jax.ShapeDtypeStruct((1, D_MODEL), F32)],
        compiler_params=_params(dimension_semantics=("arbitrary",)),
    )(x3, target, g)


def _split_w_in(w_in_t):
    w_ag = w_in_t[:2 * D_CONV]
    w_qkv = w_in_t[2 * D_CONV:2 * D_CONV + 3 * D_ATTN]
    w_f = jnp.pad(w_in_t[2 * D_CONV + 3 * D_ATTN:], ((0, LANES - N_HEADS), (0, 0)))
    return w_ag, w_qkv, w_f


def _head_rows(v):
    return jnp.pad(v, ((0, HEAD_ROWS - N_HEADS),) + ((0, 0),) * (v.ndim - 1))


def _local_step(x, target, p, get_weights, put_grads, flush_grads):
    t = x.shape[0]
    fb = _head_rows(p["forget_b"].reshape(N_HEADS, 1))

    w, deps = get_weights("ffn1_w13", None)
    h1, gu1, act1 = _ffn_up(x, p["ffn1_norm"], w["ffn1_w13"], "ffn1_up", deps)
    w2, _ = get_weights("ffn1_w2", act1)
    w.update(w2)
    x1 = _ffn_down(x, act1, w["ffn1_w2"], "ffn1_down")
    wm, _ = get_weights("mix", x1)
    w.update(wm)
    w_ag, w_qkv, w_f = _split_w_in(w["w_in"])
    conv_w = jnp.pad(w["conv_w"], ((0, CONV_PAD - CONV_WIDTH), (0, 0)))
    h2, ag, qkv, fl = _mix_proj(x1, p["mix_norm"], w_ag, w_qkv, w_f)
    flt = _head_rows(fl[:, :N_HEADS].T)
    dcum = _gates_fwd(flt, fb)[:N_HEADS]
    drow = dcum.reshape(N_HEADS, 1, t)
    ycn = _conv_fwd(ag, conv_w, p["conv_b"], p["conv_ln_g"], p["conv_ln_b"], p["out_norm_conv"])
    o, lse = _attn_fwd(qkv, drow, [ycn])
    _, deps = get_weights("ffn2:landed", o)
    x2, yan = _out_proj(ycn, o, p["out_norm_attn"], w["w_out"], x1, deps)
    w2, _ = get_weights("ffn2", x2)
    w.update(w2)
    x3, h3, gu2 = _ffn_fwd(x2, p["ffn2_norm"], w["ffn2_w13"], w["ffn2_w2"], "ffn2_fwd")
    loss, dx3, d_final = _loss_bwd(x3, target, p["final_norm"])

    g = {}
    dx2, dgu2, a2, g["ffn2_norm"], dx3_half, dx2_bf16 = _ffn_bwd(
        dx3, x2, gu2, p["ffn2_norm"], w["ffn2_w13"], w["ffn2_w2"], "ffn2_bwd")
    dw13 = _wgrad(h3, dgu2, N_CHIPS, "ffn2_dw13")
    dw2 = _wgrad(a2, dx3_half, 1, "ffn2_dw2").reshape(D_FF, D_MODEL)
    deps = put_grads("ffn2", {"ffn2_w13": dw13, "ffn2_w2": dw2})
    dyc, do, g["out_norm_attn"] = _out_proj_bwd(dx2_bf16, o, p["out_norm_attn"], w["w_out"], deps)
    deps = flush_grads("ffn2", [dyc])
    dw_out = _wgrad(jnp.concatenate([ycn, yan], axis=1), dx2_bf16, 1, "dw_out", deps).reshape(D_MODEL, D_MODEL)
    dq, dk, dv, ddrow = _attn_bwd(qkv, drow, lse, do)
    dflt, dfb = _gates_bwd(_head_rows(ddrow.reshape(N_HEADS, t)), flt, fb)
    g["forget_b"] = dfb[:N_HEADS, 0].reshape(1, N_HEADS)
    dfl = jnp.pad(dflt[:N_HEADS].T, ((0, 0), (0, LANES - N_HEADS)))
    dag, dconv_w, g["conv_b"], g["conv_ln_g"], g["conv_ln_b"], g["out_norm_conv"] = _conv_bwd(
        ag, dyc, conv_w, p["conv_b"], p["conv_ln_g"], p["conv_ln_b"], p["out_norm_conv"])
    g["conv_w"] = dconv_w[:CONV_WIDTH]
    dproj = jnp.concatenate([dag, dq, dk, dv, dfl.astype(BF16)], axis=1)
    dx1, g["mix_norm"] = _mix_proj_bwd(dproj, dx2, x1, p["mix_norm"], w_ag, w_qkv, w_f)
    dw_in = _wgrad(dproj, h2, 1, "dw_in").reshape(dproj.shape[1], D_MODEL)[:N_IN]
    deps = put_grads("mix", {"w_in": dw_in, "w_out": dw_out})
    dx0, dgu1, a1, g["ffn1_norm"], dx1_half, _ = _ffn_bwd(
        dx1, x, gu1, p["ffn1_norm"], w["ffn1_w13"], w["ffn1_w2"], "ffn1_bwd", deps)
    g["final_norm"] = d_final
    g["loss"] = loss[:, :1]
    deps = flush_grads("mix", put_grads("small", g))
    dw2 = _wgrad(a1, dx1_half, 1, "ffn1_dw2", deps).reshape(D_FF, D_MODEL)
    deps = flush_grads("ffn1_w2", put_grads("ffn1_w2", {"ffn1_w2": dw2}))
    dw13 = _wgrad(h1, dgu1, N_CHIPS, "ffn1_dw13", deps)
    put_grads("ffn1_w13", {"ffn1_w13": dw13})
    return dx0


MESH = pl.DeviceIdType.MESH


def _place():
    x, y, c = lax.axis_index("x"), lax.axis_index("y"), lax.axis_index("c")
    chips = [(1 - x, y), (x, 1 - y), (1 - x, 1 - y)]
    return x, y, c, chips


def _hbm_out(shape, dtype):
    return jax.ShapeDtypeStruct(shape, dtype)


def _comm_call(body, name, ins, out_shapes, n_remote, in_place=False):
    return pl.pallas_call(
        body, name=name, in_specs=[_ANY] * len(ins), out_specs=[_ANY] * len(out_shapes), out_shape=out_shapes,
        scratch_shapes=[pltpu.SemaphoreType.DMA((n_remote,)), pltpu.SemaphoreType.DMA((n_remote,))],
        input_output_aliases={i: i for i in range(len(ins))} if in_place else {},
    )(*ins)


def _remote(src, dst, sems, n, to):
    send_sems, recv_sems = sems
    return pltpu.make_async_remote_copy(src_ref=src, dst_ref=dst, send_sem=send_sems.at[n], recv_sem=recv_sems.at[n],
                                        device_id=to, device_id_type=MESH)


HALF_ROWS_MULTIPLE = 32


def _halved_by_rows(rows):
    return rows % HALF_ROWS_MULTIPLE == 0


def _half_shape(rows, cols):
    return (rows // 2, cols) if _halved_by_rows(rows) else (rows, cols // 2)


def _half_index(rows, core):
    return (core, 0) if _halved_by_rows(rows) else (0, core)


def _half_of(ref, rows, cols, core, *lead):
    if _halved_by_rows(rows):
        return ref.at[(*lead, pl.ds(core * (rows // 2), rows // 2), slice(None))]
    return ref.at[(*lead, slice(None), pl.ds(core * (cols // 2), cols // 2))]


def _into_slot(shard, chip, dtype, name, deps=()):
    rows, cols = shard.shape
    half = _half_shape(rows, cols)
    by_rows = _halved_by_rows(rows)
    deps = tuple(deps)

    def body(k_ref, *refs):
        s_ref, o_ref = refs[len(deps):]
        o_ref[0] = s_ref[...].astype(dtype)

    return pl.pallas_call(
        body, name=name,
        grid_spec=pltpu.PrefetchScalarGridSpec(
            num_scalar_prefetch=1, grid=(2,),
            in_specs=[_ANY] * len(deps) + [pl.BlockSpec(half, lambda i, k_ref: (i, 0) if by_rows else (0, i))],
            out_specs=pl.BlockSpec((1,) + half, lambda i, k_ref: (k_ref[0], i, 0) if by_rows else (k_ref[0], 0, i))),
        out_shape=jax.ShapeDtypeStruct((N_CHIPS, rows, cols), dtype),
        compiler_params=_params(dimension_semantics=("arbitrary",)),
    )(chip, *deps, shard)


def _gather_shards(slots, name, ici=True, passed=()):
    n = len(slots)
    slots = list(slots) + list(passed)
    total = len(slots)

    def body(*refs):
        outs = refs[total:total + n]
        sems = refs[2 * total:2 * total + 2]
        x, y, c, chips = _place()
        me = 2 * x + y
        sibling = (x, y, 1 - c)

        def half(i, chip_index, core):
            return _half_of(outs[i], *slots[i].shape[1:], core, chip_index)

        sends = []
        if ici:
            for i in range(n):
                for j, chip in enumerate(chips):
                    cp = _remote(half(i, me, c), half(i, me, c), sems, 6 * i + j, (*chip, c))
                    cp.start()
                    sends.append(cp)
        for i in range(n):
            for j, chip in enumerate(chips):
                src_chip = 2 * chip[0] + chip[1]
                landed = half(i, src_chip, c)
                if ici:
                    _remote(landed, landed, sems, 6 * i + j, (*chip, c)).wait_recv()
                cp = _remote(landed, landed, sems, 6 * i + 3 + j, sibling)
                cp.start()
                sends.append(cp)
        for i in range(n):
            for j, chip in enumerate(chips):
                src_chip = 2 * chip[0] + chip[1]
                landed = half(i, src_chip, 1 - c)
                _remote(landed, landed, sems, 6 * i + 3 + j, sibling).wait_recv()
        for cp in sends:
            cp.wait_send()

    outs = [_hbm_out(s.shape, s.dtype) for s in slots]
    return _comm_call(body, name, slots, outs, 6 * n, in_place=True)


_HBM = pl.BlockSpec(memory_space=pltpu.HBM)
_SEM = pl.BlockSpec(memory_space=pltpu.SEMAPHORE)
_DATAFLOW = pltpu.SideEffectType.DATAFLOW_SIDE_EFFECTING


def _split_copy_start(name, bufs, n_copies, plan):
    n = len(bufs)

    def body(*refs):
        for send, _ in plan(refs[:n], (refs[n], refs[n + 1])):
            send.start()
        token = refs[-1]
        token[...] = jnp.zeros_like(token)

    out = pl.pallas_call(
        body, name=name,
        out_shape=(pltpu.SemaphoreType.DMA((n_copies,)), pltpu.SemaphoreType.DMA((n_copies,)),
                   *[pltpu.HBM(b.shape, b.dtype) for b in bufs], jax.ShapeDtypeStruct((8, LANES), F32)),
        in_specs=[_HBM] * n, out_specs=(_SEM, _SEM, *[_HBM] * n, pl.BlockSpec(memory_space=pltpu.VMEM)),
        input_output_aliases={i: 2 + i for i in range(n)},
        compiler_params=pltpu.CompilerParams(has_side_effects=_DATAFLOW),
    )(*[pltpu.with_memory_space_constraint(b, pltpu.HBM) for b in bufs])
    return out[0], out[1], list(out[2:2 + n]), out[-1]


def _split_copy_wait(name, started, plan, after, passed=()):
    send_sems, recv_sems, bufs, _ = started
    n = len(bufs)
    after = tuple(after)
    bufs = list(bufs) + list(passed)
    total = len(bufs)

    def body(*refs):
        for send, recv in plan(refs[:n], (refs[total], refs[total + 1])):
            send.wait_send()
            recv.wait_recv()

    out = pl.pallas_call(
        body, name=name, out_shape=tuple(pltpu.HBM(b.shape, b.dtype) for b in bufs),
        in_specs=[_HBM] * total + [_SEM, _SEM] + [_ANY] * len(after), out_specs=tuple([_HBM] * total),
        input_output_aliases={i: i for i in range(total)},
        compiler_params=pltpu.CompilerParams(has_side_effects=_DATAFLOW),
    )(*bufs, send_sems, recv_sems, *after)
    return list(out)


def _ici_gather_plan(slots):
    def plan(refs, sems):
        x, y, c, chips = _place()
        me = 2 * x + y
        copies = []
        for i, ref in enumerate(refs):
            for j, chip in enumerate(chips):
                mine = _half_of(ref, *slots[i].shape[1:], c, me)
                theirs = _half_of(ref, *slots[i].shape[1:], c, 2 * chip[0] + chip[1])
                to = (*chip, c)
                copies.append((_remote(mine, mine, sems, 3 * i + j, to), _remote(theirs, theirs, sems, 3 * i + j, to)))
        return copies

    return plan


def _ici_scatter_plan(n):
    def plan(refs, sems):
        x, y, c, chips = _place()
        copies = []
        for i in range(n):
            for j, chip in enumerate(chips):
                cp = _remote(refs[i].at[2 * chip[0] + chip[1]], refs[n + i].at[j], sems, 3 * i + j, (*chip, c))
                copies.append((cp, cp))
        return copies

    return plan


def _d2d_forward_plan(slots):
    def plan(refs, sems):
        x, y, c, chips = _place()
        sibling = (x, y, 1 - c)
        copies = []
        for i, ref in enumerate(refs):
            for j, chip in enumerate(chips):
                src_chip = 2 * chip[0] + chip[1]
                mine = _half_of(ref, *slots[i].shape[1:], c, src_chip)
                theirs = _half_of(ref, *slots[i].shape[1:], 1 - c, src_chip)
                copies.append((_remote(mine, mine, sems, 3 * i + j, sibling),
                               _remote(theirs, theirs, sems, 3 * i + j, sibling)))
        return copies

    return plan


def _pair_exchange_plan(grads):
    n = len(grads)

    def plan(refs, sems):
        x, y, c, _ = _place()
        copies = []
        for i in range(n):
            theirs = _half_of(refs[i], *grads[i].shape[1:], 1 - c, slice(None))
            cp = _remote(theirs, refs[n + i], sems, i, (x, y, 1 - c))
            copies.append((cp, cp))
        return copies

    return plan


def _pair_share_plan(shapes):
    def plan(refs, sems):
        x, y, c, _ = _place()
        sibling = (x, y, 1 - c)
        copies = []
        for i, ref in enumerate(refs):
            mine, theirs = _half_of(ref, *shapes[i], c), _half_of(ref, *shapes[i], 1 - c)
            copies.append((_remote(mine, mine, sems, i, sibling), _remote(theirs, theirs, sems, i, sibling)))
        return copies

    return plan


def _pair_share(halves, name):
    n = len(halves)
    plan = _pair_share_plan([h.shape for h in halves])

    def body(*refs):
        copies = plan(refs[n:2 * n], refs[2 * n:2 * n + 2])
        for send, _ in copies:
            send.start()
        for send, recv in copies:
            send.wait_send()
            recv.wait_recv()

    outs = [_hbm_out(h.shape, h.dtype) for h in halves]
    return _comm_call(body, name, halves, outs, n, in_place=True)


def _all_reduce_small(v, deps=()):
    rows = v.shape[0]
    flips = [(fx, fy, fc) for fx in range(2) for fy in range(2) for fc in range(2)][1:]

    def body(v_ref, o_ref, slots, send_sems, recv_sems):
        x, y, c, _ = _place()
        me = 4 * x + 2 * y + c
        slots[me] = v_ref[...]
        sends = []
        for n, (fx, fy, fc) in enumerate(flips):
            to = (x ^ fx, y ^ fy, c ^ fc)
            cp = _remote(v_ref, slots.at[me], (send_sems, recv_sems), n, to)
            cp.start()
            sends.append(cp)
        for n, (fx, fy, fc) in enumerate(flips):
            src = 4 * (x ^ fx) + 2 * (y ^ fy) + (c ^ fc)
            _remote(v_ref, slots.at[src], (send_sems, recv_sems), n, (x ^ fx, y ^ fy, c ^ fc)).wait_recv()
        for cp in sends:
            cp.wait_send()
        acc = slots[0]
        for s in range(1, 8):
            acc = acc + slots[s]
        o_ref[...] = acc

    deps = tuple(deps)
    return pl.pallas_call(
        _skip(len(deps), body), name="all_reduce_small", out_shape=jax.ShapeDtypeStruct(v.shape, F32),
        in_specs=[_ANY] * len(deps) + [pl.BlockSpec(memory_space=pltpu.VMEM)],
        out_specs=pl.BlockSpec(memory_space=pltpu.VMEM),
        scratch_shapes=[pltpu.VMEM((8, rows, LANES), F32), pltpu.SemaphoreType.DMA((7,)), pltpu.SemaphoreType.DMA((7,))],
    )(*deps, v)


def _pair_add(gs, sibs, core, name):
    n = len(gs)
    halves = [_half_shape(*g.shape[1:]) for g in gs]

    def body(c_ref, *refs):
        for g_ref, s_ref, o_ref in zip(refs[:n], refs[n:2 * n], refs[2 * n:]):
            o_ref[0] = (g_ref[0].astype(F32) + s_ref[0].astype(F32)).astype(BF16)

    def mine(g, half):
        return pl.BlockSpec((1,) + half, lambda s, c_ref: (s, *_half_index(g.shape[1], c_ref[0])))

    whole = [pl.BlockSpec((1,) + half, lambda s, c_ref: (s, 0, 0)) for half in halves]
    return pl.pallas_call(
        body, name=name,
        grid_spec=pltpu.PrefetchScalarGridSpec(
            num_scalar_prefetch=1, grid=(N_CHIPS,),
            in_specs=[mine(g, half) for g, half in zip(gs, halves)] + whole, out_specs=whole),
        out_shape=[jax.ShapeDtypeStruct((N_CHIPS,) + half, BF16) for half in halves],
        compiler_params=_params(dimension_semantics=("arbitrary",)),
    )(core, *gs, *sibs)


def _chip_add(parts, recvs, chip_core, shapes, name):
    n = len(parts)
    halves = [_half_shape(*shape) for shape in shapes]

    def body(kc_ref, *refs):
        for p_ref, r_ref, o_ref in zip(refs[:n], refs[n:2 * n], refs[2 * n:]):
            acc = p_ref[0].astype(F32)
            for j in range(N_CHIPS - 1):
                acc = acc + r_ref[j].astype(F32)
            o_ref[...] = acc

    def out_spec(shape, half):
        return pl.BlockSpec(half, lambda s, kc_ref: _half_index(shape[0], kc_ref[1]))

    return pl.pallas_call(
        body, name=name,
        grid_spec=pltpu.PrefetchScalarGridSpec(
            num_scalar_prefetch=1, grid=(1,),
            in_specs=[pl.BlockSpec((1,) + half, lambda s, kc_ref: (kc_ref[0], 0, 0)) for half in halves]
            + [pl.BlockSpec((N_CHIPS - 1,) + half, lambda s, kc_ref: (0, 0, 0)) for half in halves],
            out_specs=[out_spec(shape, half) for shape, half in zip(shapes, halves)]),
        out_shape=[jax.ShapeDtypeStruct(tuple(shape), F32) for shape in shapes],
        compiler_params=_params(dimension_semantics=("arbitrary",)),
    )(chip_core, *parts, *recvs)


def _adamw_math(w, g, m, v):
    m = ADAM_B1 * m + (1.0 - ADAM_B1) * g
    v = ADAM_B2 * v + (1.0 - ADAM_B2) * (g * g)
    m_hat = m / (1.0 - ADAM_B1 ** ADAM_STEP)
    v_hat = v / (1.0 - ADAM_B2 ** ADAM_STEP)
    delta = -ADAM_LR * (m_hat / (jnp.sqrt(v_hat) + ADAM_EPS) + ADAM_WD * w)
    return delta, m, v


ADAM_PARTS = 4


def _adamw_matrix(w, g, m, v, name):
    rows, cols = w.shape
    by_rows = rows % (8 * ADAM_PARTS) == 0
    block = (rows // ADAM_PARTS, cols) if by_rows else (rows, cols // ADAM_PARTS)

    def body(w_ref, g_ref, m_ref, v_ref, go_ref, d_ref, mo_ref, vo_ref):
        gv = g_ref[...]
        go_ref[...] = gv
        d_ref[...], mo_ref[...], vo_ref[...] = _adamw_math(w_ref[...], gv, m_ref[...], v_ref[...])

    spec = pl.BlockSpec(block, lambda i: (i, 0) if by_rows else (0, i))
    shape = jax.ShapeDtypeStruct((rows, cols), F32)
    return pl.pallas_call(
        body, name=name, grid=(ADAM_PARTS,), in_specs=[spec] * 4, out_specs=[spec] * 4, out_shape=[shape] * 4,
        compiler_params=_params(dimension_semantics=("arbitrary",)),
    )(w, g, m, v)


def _adamw_small(ws, gs, ms, vs):
    n = len(ws)

    def body(*refs):
        for i in range(n):
            w_ref, g_ref, m_ref, v_ref = (refs[k * n + i] for k in range(4))
            d_ref, mo_ref, vo_ref = (refs[(4 + k) * n + i] for k in range(3))
            d_ref[...], mo_ref[...], vo_ref[...] = _adamw_math(w_ref[...], g_ref[...], m_ref[...], v_ref[...])

    shapes = [jax.ShapeDtypeStruct(w.shape, F32) for w in ws]
    out = pl.pallas_call(body, name="adamw_small", out_shape=shapes * 3, compiler_params=_params())(*ws, *gs, *ms, *vs)
    return out[:n], out[n:2 * n], out[2 * n:]


MATRICES = ["ffn1_w13", "ffn1_w2", "w_in", "w_out", "ffn2_w13", "ffn2_w2"]
VECTORS = ["ffn1_norm", "mix_norm", "conv_b", "conv_ln_g", "conv_ln_b", "forget_b", "out_norm_conv",
           "out_norm_attn", "ffn2_norm", "final_norm"]
WEIGHTS = ["ffn1_norm", "ffn1_w13", "ffn1_w2", "mix_norm", "w_in", "conv_w", "conv_b", "conv_ln_g", "conv_ln_b",
           "forget_b", "out_norm_conv", "out_norm_attn", "w_out", "ffn2_norm", "ffn2_w13", "ffn2_w2", "final_norm"]


def _pack_small(g, names):
    rows, layout = [], []
    for n in names:
        flat = g[n].reshape(-1)
        pad = (-flat.shape[0]) % LANES
        rows.append(jnp.pad(flat, (0, pad)).reshape(-1, LANES))
        layout.append((n, g[n].shape, flat.shape[0], rows[-1].shape[0]))
    packed = jnp.concatenate(rows, axis=0)
    pad_rows = (-packed.shape[0]) % 8
    return jnp.pad(packed, ((0, pad_rows), (0, 0))), layout


def _unpack_small(packed, layout):
    out, r = {}, 0
    for n, shape, size, nrows in layout:
        out[n] = packed[r:r + nrows].reshape(-1)[:size].reshape(shape)
        r += nrows
    return out


def kernel(x, ffn1_norm, ffn1_w13, ffn1_w2, mix_norm, w_in, conv_w, conv_b, conv_ln_g, conv_ln_b, forget_b, out_norm_conv, out_norm_attn, w_out, ffn2_norm, ffn2_w13, ffn2_w2, final_norm, loss_target, m_ffn1_norm, m_ffn1_w13, m_ffn1_w2, m_mix_norm, m_w_in, m_conv_w, m_conv_b, m_conv_ln_g, m_conv_ln_b, m_forget_b, m_out_norm_conv, m_out_norm_attn, m_w_out, m_ffn2_norm, m_ffn2_w13, m_ffn2_w2, m_final_norm, v_ffn1_norm, v_ffn1_w13, v_ffn1_w2, v_mix_norm, v_w_in, v_conv_w, v_conv_b, v_conv_ln_g, v_conv_ln_b, v_forget_b, v_out_norm_conv, v_out_norm_attn, v_w_out, v_ffn2_norm, v_ffn2_w13, v_ffn2_w2, v_final_norm):
    args = dict(locals())
    weights = {n: args[n] for n in WEIGHTS}
    core = lax.axis_index("c").astype(jnp.int32).reshape(1)
    chip = (2 * lax.axis_index("x") + lax.axis_index("y")).astype(jnp.int32)
    chip1 = chip.reshape(1)
    chip_core = jnp.concatenate([chip1, core])

    def held(n, a):
        return a[0].T if n == "w_in" else a[0]

    def given(n, a):
        return (a.T if n == "w_in" else a)[None]

    def slot(n, deps=()):
        if n == "conv_w":
            rows = jnp.pad(conv_w[0], ((0, CONV_PAD - CONV_WIDTH), (0, 0)))
            return _into_slot(rows, chip1, F32, "slot_conv_w", deps)
        return _into_slot(held(n, weights[n]), chip1, BF16, "slot_" + n, deps)

    fetched = {"ffn1_w13": ["ffn1_w13"], "ffn1_w2": ["ffn1_w2"], "mix": ["w_in", "w_out", "conv_w"],
               "ffn2": ["ffn2_w13", "ffn2_w2"]}
    fetch = {}

    def as_weights(group, bufs):
        out = {}
        for n, b in zip(fetched[group], bufs):
            if n.endswith("w13"):
                out[n] = b
            elif n != "conv_w":
                out[n] = b.reshape(N_CHIPS * b.shape[1], b.shape[2])
            else:
                out[n] = b[:, :CONV_WIDTH].transpose(1, 0, 2).reshape(CONV_WIDTH, D_CONV)
        return out

    def get_weights(group, after):
        if group == "ffn1_w13":
            first = [slot("ffn1_w13")]
            plan = _ici_gather_plan(first)
            started = _split_copy_start("gather_ffn1_w13_start", first, 3, plan)
            second = [slot("ffn1_w2", [started[3]])]
            plan2 = _ici_gather_plan(second)
            fetch["ffn1_w2"] = plan2, _split_copy_start("gather_ffn1_w2_start", second, 3, plan2)
            later_names = fetched["mix"] + fetched["ffn2"]
            later = [slot(n, [fetch["ffn1_w2"][1][3]]) for n in later_names]
            landed = _split_copy_wait("gather_ffn1_w13_wait", started, plan, [], passed=later)
            bufs = _gather_shards(landed[:1], "forward_ffn1_w13", ici=False)
            behind = dict(zip(later_names, landed[1:]))
            for later in ("mix", "ffn2"):
                bufs_later = [behind[n] for n in fetched[later]]
                plan = _ici_gather_plan(bufs_later)
                fetch[later] = plan, _split_copy_start("gather_%s_start" % later, bufs_later, 3 * len(bufs_later), plan)
            return as_weights(group, bufs), [fetch["mix"][1][3], fetch["ffn2"][1][3]]
        plan, started = fetch[group.split(":")[0]]
        if group == "ffn2:landed":
            landed = _split_copy_wait("gather_ffn2_wait", started, plan, [after])
            plan = _d2d_forward_plan(landed)
            fetch["ffn2"] = plan, _split_copy_start("forward_ffn2_start", landed, 3 * len(landed), plan)
            return {}, [fetch["ffn2"][1][3]]
        if group == "ffn2":
            return as_weights(group, _split_copy_wait("forward_ffn2_wait", started, plan, [after])), []
        landed = _split_copy_wait("gather_%s_wait" % group, started, plan, [after])
        return as_weights(group, _gather_shards(landed, "forward_" + group, ici=False)), []

    def shard_major(n, g):
        return g if n.endswith("w13") else g.reshape(N_CHIPS, g.shape[0] // N_CHIPS, g.shape[1])

    exchange, scatter = {}, {}
    small_names = VECTORS + ["conv_w"]
    small = {}

    def put_grads(group, grads):
        if group == "small":
            packed, layout = _pack_small(grads, small_names + ["loss"])
            total = _all_reduce_small(packed)
            small.update(_unpack_small(total, layout))
            return [total]
        names = list(grads)
        local = [shard_major(n, grads[n]) for n in names]
        landing = [lax.empty((N_CHIPS,) + _half_shape(*a.shape[1:]), BF16) for a in local]
        plan = _pair_exchange_plan(local)
        exchange[group] = names, plan, _split_copy_start("exchange_%s_start" % group, local + landing, len(local), plan)
        return [exchange[group][2][3]]

    def flush_grads(group, after):
        names, plan, started = exchange[group]
        done = _split_copy_wait("exchange_%s_wait" % group, started, plan, after)
        local, sib = done[:len(names)], done[len(names):]
        parts = list(_pair_add(local, sib, core, "pair_add_" + group))
        landing = [lax.empty((N_CHIPS - 1,) + q.shape[1:], BF16) for q in parts]
        plan = _ici_scatter_plan(len(parts))
        shapes = [a.shape[1:] for a in local]
        scatter[group] = names, plan, _split_copy_start("scatter_%s_start" % group, parts + landing, 3 * len(parts), plan), shapes
        return [scatter[group][2][3]]

    p = {n: weights[n] for n in VECTORS}
    p["final_norm"] = final_norm.reshape(1, D_MODEL)
    dx = _local_step(x[0], loss_target[0], p, get_weights, put_grads, flush_grads)
    loss = small["loss"].reshape(())

    grad = {n: small[n] for n in VECTORS}
    grad["final_norm"] = small["final_norm"].reshape(D_MODEL)
    grad["conv_w"] = lax.dynamic_slice_in_dim(small["conv_w"], chip * (D_CONV // N_CHIPS), D_CONV // N_CHIPS, axis=1)[None]

    delta, new_m, new_v = {}, {}, {}

    def reduce_chips(group, after):
        names, plan, started, shapes = scatter[group]
        done = _split_copy_wait("scatter_%s_wait" % group, started, plan, after)
        parts, landed = done[:len(names)], done[len(names):]
        return list(_chip_add(parts, landed, chip_core, shapes, "chip_add_" + group))

    def update(group, full):
        ends = []
        for n, reduced in zip(scatter[group][0], full):
            go, d, mo, vo = _adamw_matrix(held(n, weights[n]), reduced, held(n, args["m_" + n]), held(n, args["v_" + n]),
                                          "adamw_" + n)
            grad[n], delta[n], new_m[n], new_v[n] = given(n, go), given(n, d), given(n, mo), given(n, vo)
            ends.append(vo)
        return ends

    def share_start(group, halves):
        plan = _pair_share_plan(scatter[group][3])
        return plan, _split_copy_start("share_%s_start" % group, halves, len(halves), plan)

    halves_ffn2 = reduce_chips("ffn2", [exchange["ffn1_w13"][2][3]])
    plan_ffn2, share_ffn2 = share_start("ffn2", halves_ffn2)
    last_scatter = flush_grads("ffn1_w13", [share_ffn2[3]])
    halves_mix = reduce_chips("mix", last_scatter)
    plan_mix, share_mix = share_start("mix", halves_mix)
    done_ffn2 = update("ffn2", _split_copy_wait("share_ffn2_wait", share_ffn2, plan_ffn2, [share_mix[3]]))
    done_mix = update("mix", _split_copy_wait("share_mix_wait", share_mix, plan_mix, done_ffn2))
    as2d = lambda a: a.reshape(-1, a.shape[-1])
    ds, mos, vos = _adamw_small([as2d(weights[n]) for n in small_names], [as2d(grad[n]) for n in small_names],
                                [as2d(args["m_" + n]) for n in small_names], [as2d(args["v_" + n]) for n in small_names])
    for n, d, mo, vo in zip(small_names, ds, mos, vos):
        shape = weights[n].shape
        delta[n], new_m[n], new_v[n] = d.reshape(shape), mo.reshape(shape), vo.reshape(shape)
    behind = done_ffn2 + done_mix + [vos[0]]
    halves_w2 = reduce_chips("ffn1_w2", behind)
    halves_w13 = reduce_chips("ffn1_w13", behind)
    full_w2, full_w13 = _pair_share(halves_w2 + halves_w13, "pair_share_ffn1")
    update("ffn1_w2", [full_w2])
    update("ffn1_w13", [full_w13])

    return (loss, dx[None], *[grad[n] for n in WEIGHTS], *[delta[n] for n in WEIGHTS],
            *[new_m[n] for n in WEIGHTS], *[new_v[n] for n in WEIGHTS])
```

```python
import functools

import jax
import jax.numpy as jnp
from jax import lax
from jax.experimental import pallas as pl
from jax.experimental.pallas import tpu as pltpu

F32 = jnp.float32
BF16 = jnp.bfloat16

D_MODEL = 1024
D_FF = 2816
FF_SHARD = D_FF // 2
D_CONV = 512
D_ATTN = 512
N_HEADS = 8
HEAD_DIM = 64
CONV_WIDTH = 31
CONV_PAD = 32
N_IN = 2 * D_CONV + 3 * D_ATTN + N_HEADS
EPS = 1e-6
N_CHIPS = 4
LANES = 128
TOKEN_ROWS = 512
HEAD_ROWS = 16

ADAM_LR = 0.001
ADAM_B1 = 0.9
ADAM_B2 = 0.999
ADAM_EPS = 1e-08
ADAM_WD = 0.01
ADAM_STEP = 10

VMEM_LIMIT = 56 * 1024 * 1024

_NT = (((1,), (1,)), ((), ()))
_TN = (((0,), (0,)), ((), ()))


def _dot(a, b):
    return jnp.dot(a, b, preferred_element_type=F32)


def _dot_nt(a, b):
    return lax.dot_general(a, b, _NT, preferred_element_type=F32)


def _dot_tn(a, b):
    return lax.dot_general(a, b, _TN, preferred_element_type=F32)


def _params(**kw):
    return pltpu.CompilerParams(vmem_limit_bytes=VMEM_LIMIT, **kw)


def _sigmoid(x):
    return 1.0 / (1.0 + jnp.exp(-x))


def _rms_stats(x):
    return lax.rsqrt(jnp.mean(x * x, axis=-1, keepdims=True) + EPS)


def _rms_bwd(x, r, g, dh):
    t = dh * g
    dx = r * t - x * (r * r * r) * jnp.mean(t * x, axis=-1, keepdims=True)
    return dx, dh * x * r


def _silu_grad(z, sg):
    return sg * (1.0 + z * (1.0 - sg))


def _row_spec(tm, n):
    return pl.BlockSpec((tm, n), lambda i: (i, 0))


def _full_spec(shape):
    nd = len(shape)
    return pl.BlockSpec(shape, lambda i: (0,) * nd)


_ANY = pl.BlockSpec(memory_space=pl.ANY)


def _skip(n, body):
    return lambda *refs: body(*refs[n:])


FFN_ROWS = 256
FFN_WEIGHT_PARTS = N_CHIPS + 2


def _with_ffn_weights(w13_hbm, w2_hbm, w13_ref, w2_ref, sems, order, tile):
    first = pl.program_id(0) == 0
    copies = {}
    if w13_hbm is not None:
        for k in range(N_CHIPS):
            copies["w13", k] = pltpu.make_async_copy(w13_hbm.at[k], w13_ref.at[k], sems.at[k])
    if w2_hbm is not None:
        for half in range(2):
            rows = pl.ds(half * FF_SHARD, FF_SHARD)
            copies["w2", half] = pltpu.make_async_copy(w2_hbm.at[rows, :], w2_ref.at[rows, :], sems.at[N_CHIPS + half])

    @pl.when(first)
    def _():
        for part in order:
            copies[part].start()

        def ready(*parts):
            for part in parts:
                copies[part].wait()

        tile(ready)

    @pl.when(jnp.logical_not(first))
    def _():
        tile(lambda *parts: None)


def _ffn_fwd(x, g, w13s, w2, name, deps=()):
    t = x.shape[0]
    tm = FFN_ROWS
    deps = tuple(deps)

    def body(x_ref, g_ref, w13_hbm, w2_hbm, xo_ref, h_ref, gu_ref, w13_ref, w2_ref, sems):
        def tile(ready):
            xv = x_ref[...]
            hb = (xv * _rms_stats(xv) * g_ref[...]).astype(BF16)
            h_ref[...] = hb
            acc = jnp.zeros((tm, D_MODEL), F32)
            for half in range(2):
                lo = half * FF_SHARD
                ready(("w13", half), ("w13", 2 + half))
                gate = _dot(hb, w13_ref[half])
                up = _dot(hb, w13_ref[2 + half])
                gu_ref[:, lo:lo + FF_SHARD] = gate.astype(BF16)
                gu_ref[:, D_FF + lo:D_FF + lo + FF_SHARD] = up.astype(BF16)
                a = (gate * _sigmoid(gate) * up).astype(BF16)
                ready(("w2", half))
                acc = acc + _dot(a, w2_ref[lo:lo + FF_SHARD, :])
            xo_ref[...] = xv + 0.5 * acc

        _with_ffn_weights(w13_hbm, w2_hbm, w13_ref, w2_ref, sems,
                          [("w13", 0), ("w13", 2), ("w2", 0), ("w13", 1), ("w13", 3), ("w2", 1)], tile)

    return pl.pallas_call(
        _skip(len(deps), body), name=name, grid=(t // tm,),
        in_specs=[_ANY] * len(deps) + [_row_spec(tm, D_MODEL), _full_spec((1, D_MODEL)), _ANY, _ANY],
        out_specs=[_row_spec(tm, D_MODEL), _row_spec(tm, D_MODEL), _row_spec(tm, 2 * D_FF)],
        out_shape=[jax.ShapeDtypeStruct((t, D_MODEL), F32), jax.ShapeDtypeStruct((t, D_MODEL), BF16),
                   jax.ShapeDtypeStruct((t, 2 * D_FF), BF16)],
        scratch_shapes=[pltpu.VMEM(w13s.shape, BF16), pltpu.VMEM(w2.shape, BF16),
                        pltpu.SemaphoreType.DMA((FFN_WEIGHT_PARTS,))],
        compiler_params=_params(dimension_semantics=("arbitrary",)),
    )(*deps, x, g, w13s, w2)


def _ffn_up(x, g, w13s, name, deps=()):
    t = x.shape[0]
    tm = FFN_ROWS
    deps = tuple(deps)

    def body(x_ref, g_ref, w13_hbm, h_ref, gu_ref, a_ref, w13_ref, sems):
        def tile(ready):
            xv = x_ref[...]
            hb = (xv * _rms_stats(xv) * g_ref[...]).astype(BF16)
            h_ref[...] = hb
            for half in range(2):
                lo = half * FF_SHARD
                ready(("w13", half), ("w13", 2 + half))
                gate = _dot(hb, w13_ref[half])
                up = _dot(hb, w13_ref[2 + half])
                gu_ref[:, lo:lo + FF_SHARD] = gate.astype(BF16)
                gu_ref[:, D_FF + lo:D_FF + lo + FF_SHARD] = up.astype(BF16)
                a_ref[:, lo:lo + FF_SHARD] = (gate * _sigmoid(gate) * up).astype(BF16)

        _with_ffn_weights(w13_hbm, None, w13_ref, None, sems, [("w13", 0), ("w13", 2), ("w13", 1), ("w13", 3)], tile)

    return pl.pallas_call(
        _skip(len(deps), body), name=name, grid=(t // tm,),
        in_specs=[_ANY] * len(deps) + [_row_spec(tm, D_MODEL), _full_spec((1, D_MODEL)), _ANY],
        out_specs=[_row_spec(tm, D_MODEL), _row_spec(tm, 2 * D_FF), _row_spec(tm, D_FF)],
        out_shape=[jax.ShapeDtypeStruct((t, D_MODEL), BF16), jax.ShapeDtypeStruct((t, 2 * D_FF), BF16),
                   jax.ShapeDtypeStruct((t, D_FF), BF16)],
        scratch_shapes=[pltpu.VMEM(w13s.shape, BF16), pltpu.SemaphoreType.DMA((FFN_WEIGHT_PARTS,))],
        compiler_params=_params(dimension_semantics=("arbitrary",)),
    )(*deps, x, g, w13s)


def _ffn_down(x, a, w2, name):
    t = x.shape[0]
    tm = FFN_ROWS

    def body(x_ref, a_ref, w2_hbm, xo_ref, w2_ref, sems):
        def tile(ready):
            ready(("w2", 0))
            acc = _dot(a_ref[:, 0:FF_SHARD], w2_ref[0:FF_SHARD, :])
            ready(("w2", 1))
            acc = acc + _dot(a_ref[:, FF_SHARD:], w2_ref[FF_SHARD:, :])
            xo_ref[...] = x_ref[...] + 0.5 * acc

        _with_ffn_weights(None, w2_hbm, None, w2_ref, sems, [("w2", 0), ("w2", 1)], tile)

    return pl.pallas_call(
        body, name=name, grid=(t // tm,),
        in_specs=[_row_spec(tm, D_MODEL), _row_spec(tm, D_FF), _ANY],
        out_specs=_row_spec(tm, D_MODEL), out_shape=jax.ShapeDtypeStruct((t, D_MODEL), F32),
        scratch_shapes=[pltpu.VMEM(w2.shape, BF16), pltpu.SemaphoreType.DMA((FFN_WEIGHT_PARTS,))],
        compiler_params=_params(dimension_semantics=("arbitrary",)),
    )(x, a, w2)


def _ffn_bwd(dy, x, gu, g, w13s, w2, name, deps=()):
    t = x.shape[0]
    tm = FFN_ROWS
    deps = tuple(deps)

    def body(dy_ref, x_ref, gu_ref, g_ref, w13_hbm, w2_hbm, dx_ref, dgu_ref, a_ref, dg_ref, dyh_ref, dxb_ref,
             w13_ref, w2_ref, sems):
        @pl.when(pl.program_id(0) == 0)
        def _():
            dg_ref[...] = jnp.zeros_like(dg_ref)

        def tile(ready):
            dyv = dy_ref[...]
            dyh = (0.5 * dyv).astype(BF16)
            dyh_ref[...] = dyh
            dh = jnp.zeros((tm, D_MODEL), F32)
            for half in range(2):
                lo = half * FF_SHARD
                ready(("w2", half))
                da = _dot_nt(dyh, w2_ref[lo:lo + FF_SHARD, :])
                gate = gu_ref[:, lo:lo + FF_SHARD].astype(F32)
                up = gu_ref[:, D_FF + lo:D_FF + lo + FF_SHARD].astype(F32)
                sg = _sigmoid(gate)
                act = gate * sg
                a_ref[:, lo:lo + FF_SHARD] = (act * up).astype(BF16)
                dgate = (da * up * _silu_grad(gate, sg)).astype(BF16)
                dup = (da * act).astype(BF16)
                dgu_ref[:, lo:lo + FF_SHARD] = dgate
                dgu_ref[:, D_FF + lo:D_FF + lo + FF_SHARD] = dup
                ready(("w13", half), ("w13", 2 + half))
                dh = dh + _dot_nt(dgate, w13_ref[half]) + _dot_nt(dup, w13_ref[2 + half])
            xv = x_ref[...]
            dxn, dg_rows = _rms_bwd(xv, _rms_stats(xv), g_ref[...], dh)
            dx = dyv + dxn
            dx_ref[...] = dx
            dxb_ref[...] = dx.astype(BF16)
            dg_ref[...] += jnp.sum(dg_rows, axis=0, keepdims=True)

        _with_ffn_weights(w13_hbm, w2_hbm, w13_ref, w2_ref, sems,
                          [("w2", 0), ("w13", 0), ("w13", 2), ("w2", 1), ("w13", 1), ("w13", 3)], tile)

    return pl.pallas_call(
        _skip(len(deps), body), name=name, grid=(t // tm,),
        in_specs=[_ANY] * len(deps) + [_row_spec(tm, D_MODEL), _row_spec(tm, D_MODEL), _row_spec(tm, 2 * D_FF),
                                       _full_spec((1, D_MODEL)), _ANY, _ANY],
        out_specs=[_row_spec(tm, D_MODEL), _row_spec(tm, 2 * D_FF), _row_spec(tm, D_FF),
                   _full_spec((1, D_MODEL)), _row_spec(tm, D_MODEL), _row_spec(tm, D_MODEL)],
        out_shape=[jax.ShapeDtypeStruct((t, D_MODEL), F32), jax.ShapeDtypeStruct((t, 2 * D_FF), BF16),
                   jax.ShapeDtypeStruct((t, D_FF), BF16), jax.ShapeDtypeStruct((1, D_MODEL), F32),
                   jax.ShapeDtypeStruct((t, D_MODEL), BF16), jax.ShapeDtypeStruct((t, D_MODEL), BF16)],
        scratch_shapes=[pltpu.VMEM(w13s.shape, BF16), pltpu.VMEM(w2.shape, BF16),
                        pltpu.SemaphoreType.DMA((FFN_WEIGHT_PARTS,))],
        compiler_params=_params(dimension_semantics=("arbitrary",)),
    )(*deps, dy, x, gu, g, w13s, w2)


WGRAD_ROWS = (512, 384, 256)


def _wgrad(a, b, n_blocks, name, deps=()):
    t, m = a.shape
    tm = next(rows for rows in WGRAD_ROWS if m % rows == 0)
    n = b.shape[1]
    bn = n // n_blocks
    deps = tuple(deps)
    assert a.dtype == BF16 and b.dtype == BF16

    def body(a_ref, b_ref, o_ref):
        o_ref[0] = _dot_tn(a_ref[...], b_ref[...]).astype(BF16)

    return pl.pallas_call(
        _skip(len(deps), body), name=name, grid=(n_blocks, m // tm),
        in_specs=[_ANY] * len(deps) + [pl.BlockSpec((t, tm), lambda j, i: (0, i)),
                                       pl.BlockSpec((t, bn), lambda j, i: (0, j))],
        out_specs=pl.BlockSpec((1, tm, bn), lambda j, i: (j, i, 0)),
        out_shape=jax.ShapeDtypeStruct((n_blocks, m, bn), BF16),
        compiler_params=_params(dimension_semantics=("arbitrary", "arbitrary")),
    )(*deps, a, b)


def _mix_proj(x, g, w_ag, w_qkv, w_f):
    t = x.shape[0]
    tm = TOKEN_ROWS

    def body(x_ref, g_ref, wag_ref, wqkv_ref, wf_ref, h_ref, ag_ref, qkv_ref, fl_ref):
        xv = x_ref[...]
        hb = (xv * _rms_stats(xv) * g_ref[...]).astype(BF16)
        h_ref[...] = hb
        ag_ref[...] = _dot_nt(hb, wag_ref[...])
        qkv_ref[...] = _dot_nt(hb, wqkv_ref[...]).astype(BF16)
        fl_ref[...] = _dot_nt(hb, wf_ref[...])

    return pl.pallas_call(
        body, name="mix_proj", grid=(t // tm,),
        in_specs=[_row_spec(tm, D_MODEL), _full_spec((1, D_MODEL)), _full_spec(w_ag.shape),
                  _full_spec(w_qkv.shape), _full_spec(w_f.shape)],
        out_specs=[_row_spec(tm, D_MODEL), _row_spec(tm, 2 * D_CONV), _row_spec(tm, 3 * D_ATTN),
                   _row_spec(tm, LANES)],
        out_shape=[jax.ShapeDtypeStruct((t, D_MODEL), BF16), jax.ShapeDtypeStruct((t, 2 * D_CONV), F32),
                   jax.ShapeDtypeStruct((t, 3 * D_ATTN), BF16), jax.ShapeDtypeStruct((t, LANES), F32)],
        compiler_params=_params(dimension_semantics=("arbitrary",)),
    )(x, g, w_ag, w_qkv, w_f)


def _mix_proj_bwd(dproj, dx2, x1, g, w_ag, w_qkv, w_f):
    t = x1.shape[0]
    tm = TOKEN_ROWS
    n_ag, n_qkv = 2 * D_CONV, 3 * D_ATTN

    def body(dp_ref, dx2_ref, x_ref, g_ref, wag_ref, wqkv_ref, wf_ref, dx_ref, dg_ref):
        @pl.when(pl.program_id(0) == 0)
        def _():
            dg_ref[...] = jnp.zeros_like(dg_ref)

        dh = (_dot(dp_ref[:, 0:n_ag], wag_ref[...]) + _dot(dp_ref[:, n_ag:n_ag + n_qkv], wqkv_ref[...])
              + _dot(dp_ref[:, n_ag + n_qkv:], wf_ref[...]))
        xv = x_ref[...]
        dxn, dg_rows = _rms_bwd(xv, _rms_stats(xv), g_ref[...], dh)
        dx_ref[...] = dx2_ref[...] + dxn
        dg_ref[...] += jnp.sum(dg_rows, axis=0, keepdims=True)

    return pl.pallas_call(
        body, name="mix_proj_bwd", grid=(t // tm,),
        in_specs=[_row_spec(tm, dproj.shape[1]),
                  _row_spec(tm, D_MODEL), _row_spec(tm, D_MODEL), _full_spec((1, D_MODEL)),
                  _full_spec(w_ag.shape), _full_spec(w_qkv.shape), _full_spec(w_f.shape)],
        out_specs=[_row_spec(tm, D_MODEL), _full_spec((1, D_MODEL))],
        out_shape=[jax.ShapeDtypeStruct((t, D_MODEL), F32), jax.ShapeDtypeStruct((1, D_MODEL), F32)],
        compiler_params=_params(dimension_semantics=("arbitrary",)),
    )(dproj, dx2, x1, g, w_ag, w_qkv, w_f)


def _split3(x):
    hi = x.astype(BF16)
    r1 = x - hi.astype(F32)
    mid = r1.astype(BF16)
    lo = (r1 - mid.astype(F32)).astype(BF16)
    return hi, mid, lo


def _gates_fwd(flt, fb):
    t = flt.shape[1]

    def body(f_ref, b_ref, d_ref):
        z = f_ref[...] + b_ref[...]
        logf = jnp.minimum(z, 0.0) - jnp.log(1.0 + jnp.exp(-jnp.abs(z)))
        row = lax.broadcasted_iota(jnp.int32, (LANES, LANES), 0)
        col = lax.broadcasted_iota(jnp.int32, (LANES, LANES), 1)
        upper = (row <= col).astype(BF16)
        carry = jnp.zeros((HEAD_ROWS, 1), F32)
        for blk in range(t // LANES):
            hi, mid, lo = _split3(logf[:, blk * LANES:(blk + 1) * LANES])
            cs = _dot(hi, upper) + _dot(mid, upper) + _dot(lo, upper)
            d_ref[:, blk * LANES:(blk + 1) * LANES] = cs + carry
            carry = carry + cs[:, LANES - 1:LANES]

    return pl.pallas_call(
        body, name="gates_fwd", out_shape=jax.ShapeDtypeStruct((HEAD_ROWS, t), F32),
        compiler_params=_params(),
    )(flt, fb)


def _gates_bwd(dd, flt, fb):
    t = flt.shape[1]

    def body(dd_ref, f_ref, b_ref, df_ref, db_ref):
        z = f_ref[...] + b_ref[...]
        row = lax.broadcasted_iota(jnp.int32, (LANES, LANES), 0)
        col = lax.broadcasted_iota(jnp.int32, (LANES, LANES), 1)
        lower = (row >= col).astype(BF16)
        carry = jnp.zeros((HEAD_ROWS, 1), F32)
        db = jnp.zeros((HEAD_ROWS, 1), F32)
        for blk in reversed(range(t // LANES)):
            sl = slice(blk * LANES, (blk + 1) * LANES)
            hi, mid, lo = _split3(dd_ref[:, sl])
            cs = _dot(hi, lower) + _dot(mid, lower) + _dot(lo, lower)
            dz = (cs + carry) * _sigmoid(-z[:, sl])
            df_ref[:, sl] = dz
            db = db + jnp.sum(dz, axis=1, keepdims=True)
            carry = carry + cs[:, 0:1]
        db_ref[...] = db

    return pl.pallas_call(
        body, name="gates_bwd",
        out_shape=[jax.ShapeDtypeStruct((HEAD_ROWS, t), F32), jax.ShapeDtypeStruct((HEAD_ROWS, 1), F32)],
        compiler_params=_params(),
    )(dd, flt, fb)


CONV_CHUNK = 128
CONV_TAIL = 16
CONV_WINDOW = CONV_CHUNK + CONV_PAD + 8
CONV_ROWS_EXTRA = CONV_PAD + CONV_TAIL
SUBLANES = 8


def _conv_rows(ag_ref, u_ref, t):
    u_ref[0:CONV_PAD, :] = jnp.zeros((CONV_PAD, D_CONV), F32)
    u_ref[CONV_PAD + t:CONV_ROWS_EXTRA + t, :] = jnp.zeros((CONV_TAIL, D_CONV), F32)

    def fill(i, c):
        r0 = pl.multiple_of(i * CONV_CHUNK, CONV_CHUNK)
        a = ag_ref[pl.ds(r0, CONV_CHUNK), 0:D_CONV]
        gt = ag_ref[pl.ds(r0, CONV_CHUNK), D_CONV:2 * D_CONV]
        u_ref[pl.ds(CONV_PAD + r0, CONV_CHUNK), :] = a * _sigmoid(gt)
        return c

    lax.fori_loop(0, t // CONV_CHUNK, fill, 0)


def _for_shifted(ref, r0, offsets, fn):
    window = ref[pl.ds(r0, CONV_WINDOW), :]
    for rem in range(SUBLANES):
        mine = [o for o in offsets if o % SUBLANES == rem]
        if not mine:
            continue
        turned = window if rem == 0 else pltpu.roll(window, CONV_WINDOW - rem, 0)
        for o in mine:
            fn(o, turned[o - rem:o - rem + CONV_CHUNK])


def _conv_taps(u_ref, r0, w_ref, cb):
    acc = [jnp.zeros((CONV_CHUNK, D_CONV), F32)]

    def tap(o, rows):
        j = o - (CONV_PAD - CONV_WIDTH + 1)
        acc[0] = acc[0] + w_ref[j:j + 1, :] * rows

    _for_shifted(u_ref, r0, [j + CONV_PAD - CONV_WIDTH + 1 for j in range(CONV_WIDTH)], tap)
    return acc[0] + cb


def _conv_point(y, lg, lb):
    mu = jnp.mean(y, axis=-1, keepdims=True)
    yc = y - mu
    rstd = lax.rsqrt(jnp.mean(yc * yc, axis=-1, keepdims=True) + EPS)
    yhat = yc * rstd
    z = yhat * lg + lb
    sg = _sigmoid(z)
    s = z * sg
    rr = _rms_stats(s)
    return yhat, rstd, z, sg, s, rr


def _conv_fwd(ag, conv_w, conv_b, ln_g, ln_b, norm_g):
    t = ag.shape[0]

    def body(ag_ref, w_ref, cb_ref, lg_ref, lb_ref, ng_ref, o_ref, y_ref, u_ref):
        _conv_rows(ag_ref, u_ref, t)
        cb, lg, lb, ng = cb_ref[...], lg_ref[...], lb_ref[...], ng_ref[...]

        def chunk(i, c):
            r0 = pl.multiple_of(i * CONV_CHUNK, CONV_CHUNK)
            y = _conv_taps(u_ref, r0, w_ref, cb)
            y_ref[pl.ds(r0, CONV_CHUNK), :] = y
            _, _, _, _, s, rr = _conv_point(y, lg, lb)
            o_ref[pl.ds(r0, CONV_CHUNK), :] = (s * rr * ng).astype(BF16)
            return c

        lax.fori_loop(0, t // CONV_CHUNK, chunk, 0)

    return pl.pallas_call(
        body, name="conv_fwd",
        out_shape=[jax.ShapeDtypeStruct((t, D_CONV), BF16), jax.ShapeDtypeStruct((t, D_CONV), F32)],
        scratch_shapes=[pltpu.VMEM((t + CONV_ROWS_EXTRA, D_CONV), F32)],
        compiler_params=_params(),
    )(ag, conv_w, conv_b, ln_g, ln_b, norm_g)


def _conv_bwd(ag, y, dout, conv_w, ln_g, ln_b, norm_g):
    t = ag.shape[0]

    def body(ag_ref, y_ref, do_ref, w_ref, lg_ref, lb_ref, ng_ref,
             dag_ref, dw_ref, dcb_ref, dlg_ref, dlb_ref, dng_ref, u_ref, dy_ref):
        _conv_rows(ag_ref, u_ref, t)
        dy_ref[t:t + CONV_ROWS_EXTRA, :] = jnp.zeros((CONV_ROWS_EXTRA, D_CONV), F32)
        lg, lb, ng = lg_ref[...], lb_ref[...], ng_ref[...]
        dw_ref[...] = jnp.zeros_like(dw_ref)
        zero = jnp.zeros((1, D_CONV), F32)

        def chunk(i, carry):
            dcb, dlg, dlb, dng = carry
            r0 = pl.multiple_of(i * CONV_CHUNK, CONV_CHUNK)
            yhat, rstd, z, sg, s, rr = _conv_point(y_ref[pl.ds(r0, CONV_CHUNK), :], lg, lb)
            do = do_ref[pl.ds(r0, CONV_CHUNK), :]
            ds, dng_rows = _rms_bwd(s, rr, ng, do)
            dz = ds * _silu_grad(z, sg)
            dyhat = dz * lg
            dy = rstd * (dyhat - jnp.mean(dyhat, axis=-1, keepdims=True)
                         - yhat * jnp.mean(dyhat * yhat, axis=-1, keepdims=True))
            dy_ref[pl.ds(r0, CONV_CHUNK), :] = dy
            def tap(o, rows):
                j = o - (CONV_PAD - CONV_WIDTH + 1)
                dw_ref[j:j + 1, :] += jnp.sum(dy * rows, axis=0, keepdims=True)

            _for_shifted(u_ref, r0, [j + CONV_PAD - CONV_WIDTH + 1 for j in range(CONV_WIDTH)], tap)
            return (dcb + jnp.sum(dy, axis=0, keepdims=True), dlg + jnp.sum(dz * yhat, axis=0, keepdims=True),
                    dlb + jnp.sum(dz, axis=0, keepdims=True), dng + jnp.sum(dng_rows, axis=0, keepdims=True))

        dcb, dlg, dlb, dng = lax.fori_loop(0, t // CONV_CHUNK, chunk, (zero, zero, zero, zero))
        dcb_ref[...] = dcb
        dlg_ref[...] = dlg
        dlb_ref[...] = dlb
        dng_ref[...] = dng

        def chunk2(i, c):
            r0 = pl.multiple_of(i * CONV_CHUNK, CONV_CHUNK)
            acc = [jnp.zeros((CONV_CHUNK, D_CONV), F32)]

            def tap(o, rows):
                j = CONV_WIDTH - 1 - o
                acc[0] = acc[0] + w_ref[j:j + 1, :] * rows

            _for_shifted(dy_ref, r0, list(range(CONV_WIDTH)), tap)
            du = acc[0]
            a = ag_ref[pl.ds(r0, CONV_CHUNK), 0:D_CONV]
            gt = ag_ref[pl.ds(r0, CONV_CHUNK), D_CONV:2 * D_CONV]
            sg = _sigmoid(gt)
            dag_ref[pl.ds(r0, CONV_CHUNK), 0:D_CONV] = (du * sg).astype(BF16)
            dag_ref[pl.ds(r0, CONV_CHUNK), D_CONV:2 * D_CONV] = (du * a * sg * (1.0 - sg)).astype(BF16)
            return c

        lax.fori_loop(0, t // CONV_CHUNK, chunk2, 0)

    vec = jax.ShapeDtypeStruct((1, D_CONV), F32)
    return pl.pallas_call(
        body, name="conv_bwd",
        out_shape=[jax.ShapeDtypeStruct((t, 2 * D_CONV), BF16), jax.ShapeDtypeStruct((CONV_PAD, D_CONV), F32),
                   vec, vec, vec, vec],
        scratch_shapes=[pltpu.VMEM((t + CONV_ROWS_EXTRA, D_CONV), F32), pltpu.VMEM((t + CONV_ROWS_EXTRA, D_CONV), F32)],
        compiler_params=_params(),
    )(ag, y, dout, conv_w, ln_g, ln_b, norm_g)


Q_ROWS = 256
ATTN_SCALE = HEAD_DIM ** -0.5
ATTN_AHEAD = 1


def _attn_specs(t):
    blk = lambda off: pl.BlockSpec((t, LANES), lambda p: (0, off + p))
    pairs = N_HEADS // 2
    return [blk(0), blk(pairs), blk(2 * pairs), pl.BlockSpec((2, 1, t), lambda p: (p, 0, 0))]


def _one_head(q2, mask):
    return jnp.where(mask, q2, jnp.zeros_like(q2)) * ATTN_SCALE


def _attn_scores(qs, k2, drow, r0, q1):
    s = _dot_nt(qs, k2) - drow
    rowi = lax.broadcasted_iota(jnp.int32, (q1 - r0, q1 - r0), 0)
    coli = lax.broadcasted_iota(jnp.int32, (q1 - r0, q1 - r0), 1)
    diag = jnp.where(coli <= rowi, s[:, r0:q1], -jnp.inf)
    return diag if r0 == 0 else jnp.concatenate([s[:, :r0], diag], axis=1)


def _attn_fwd(qkv, drow, deps=()):
    t = qkv.shape[0]
    deps = tuple(deps)

    def body(q_ref, k_ref, v_ref, dr_ref, o_ref, lse_ref):
        head_a = lax.broadcasted_iota(jnp.int32, (1, LANES), 1) < HEAD_DIM
        items = [(qb, hh) for qb in range(t // Q_ROWS) for hh in range(2)]

        def scores(item):
            qb, hh = item
            r0, q1 = qb * Q_ROWS, (qb + 1) * Q_ROWS
            qs = _one_head(q_ref[r0:q1, :], head_a if hh == 0 else ~head_a)
            return _attn_scores(qs, k_ref[0:q1, :], dr_ref[hh, :, 0:q1], r0, q1)

        ahead = [scores(item) for item in items[:ATTN_AHEAD]]
        outs = []
        for n, (qb, hh) in enumerate(items):
            r0, q1 = qb * Q_ROWS, (qb + 1) * Q_ROWS
            s = ahead.pop(0)
            if n + ATTN_AHEAD < len(items):
                ahead.append(scores(items[n + ATTN_AHEAD]))
            mx = jnp.max(s, axis=1, keepdims=True)
            p = jnp.exp(s - mx)
            l = jnp.sum(p, axis=1, keepdims=True)
            lse_ref[hh, r0:q1, :] = mx + jnp.log(l)
            outs.append(_dot(p.astype(BF16), v_ref[0:q1, :]) * (1.0 / l))
            if hh == 1:
                o_ref[r0:q1, :] = jnp.where(head_a, outs[0], outs[1])
                outs = []

    pairs = N_HEADS // 2
    return pl.pallas_call(
        _skip(len(deps), body), name="attn_fwd", grid=(pairs,), in_specs=[_ANY] * len(deps) + _attn_specs(t),
        out_specs=[pl.BlockSpec((t, LANES), lambda p: (0, p)), pl.BlockSpec((2, t, 1), lambda p: (p, 0, 0))],
        out_shape=[jax.ShapeDtypeStruct((t, D_ATTN), F32), jax.ShapeDtypeStruct((N_HEADS, t, 1), F32)],
        compiler_params=_params(dimension_semantics=("arbitrary",)),
    )(*deps, qkv, qkv, qkv, drow)


def _attn_bwd(qkv, drow, lse, do):
    t = qkv.shape[0]

    def body(q_ref, k_ref, v_ref, dr_ref, lse_ref, do_ref,
             dq_ref, dk_ref, dv_ref, dd_ref, dk_acc, dv_acc):
        head_a = lax.broadcasted_iota(jnp.int32, (1, LANES), 1) < HEAD_DIM
        dk_acc[...] = jnp.zeros_like(dk_acc)
        dv_acc[...] = jnp.zeros_like(dv_acc)
        dd_ref[...] = jnp.zeros_like(dd_ref)
        items = [(qb, hh) for qb in range(t // Q_ROWS) for hh in range(2)]

        def products(item):
            qb, hh = item
            r0, q1 = qb * Q_ROWS, (qb + 1) * Q_ROWS
            mask = head_a if hh == 0 else ~head_a
            qs = _one_head(q_ref[r0:q1, :], mask)
            dob = jnp.where(mask, do_ref[r0:q1, :], 0.0).astype(BF16)
            s = _attn_scores(qs, k_ref[0:q1, :], dr_ref[hh, :, 0:q1], r0, q1)
            return qs, dob, s, _dot_nt(dob, v_ref[0:q1, :])

        ahead = products(items[0])
        dqs = []
        for n, (qb, hh) in enumerate(items):
            r0, q1 = qb * Q_ROWS, (qb + 1) * Q_ROWS
            qs, dob, s, dp = ahead
            if n + 1 < len(items):
                ahead = products(items[n + 1])
            p = jnp.exp(s - lse_ref[hh, r0:q1, :])
            ds = p * (dp - jnp.sum(p * dp, axis=1, keepdims=True))
            dsb = ds.astype(BF16)
            dqs.append(_dot(dsb, k_ref[0:q1, :]) * ATTN_SCALE)
            dk_acc[0:q1, :] += _dot_tn(dsb, qs)
            dv_acc[0:q1, :] += _dot_tn(p.astype(BF16), dob)
            dd_ref[hh, :, 0:q1] -= jnp.sum(ds, axis=0, keepdims=True)
            if hh == 1:
                dq_ref[r0:q1, :] = jnp.where(head_a, dqs[0], dqs[1]).astype(BF16)
                dqs = []
        dk_ref[...] = dk_acc[...].astype(BF16)
        dv_ref[...] = dv_acc[...].astype(BF16)

    pairs = N_HEADS // 2
    col = pl.BlockSpec((t, LANES), lambda p: (0, p))
    grad = jax.ShapeDtypeStruct((t, D_ATTN), BF16)
    return pl.pallas_call(
        body, name="attn_bwd", grid=(pairs,),
        in_specs=_attn_specs(t) + [pl.BlockSpec((2, t, 1), lambda p: (p, 0, 0)), col],
        out_specs=[col, col, col, pl.BlockSpec((2, 1, t), lambda p: (p, 0, 0))],
        out_shape=[grad, grad, grad, jax.ShapeDtypeStruct((N_HEADS, 1, t), F32)],
        scratch_shapes=[pltpu.VMEM((t, LANES), F32), pltpu.VMEM((t, LANES), F32)],
        compiler_params=_params(dimension_semantics=("arbitrary",)),
    )(qkv, qkv, qkv, drow, lse, do)


def _out_proj(ycn, o, g_attn, w_out, x1, deps=()):
    t = x1.shape[0]
    tm = TOKEN_ROWS
    deps = tuple(deps)

    def body(yc_ref, o_ref, g_ref, w_ref, x_ref, xo_ref, ya_ref):
        ov = o_ref[...]
        ya = (ov * _rms_stats(ov) * g_ref[...]).astype(BF16)
        ya_ref[...] = ya
        xo_ref[...] = x_ref[...] + _dot(yc_ref[...], w_ref[0:D_CONV, :]) + _dot(ya, w_ref[D_CONV:, :])

    return pl.pallas_call(
        _skip(len(deps), body), name="out_proj", grid=(t // tm,),
        in_specs=[_ANY] * len(deps) + [_row_spec(tm, D_CONV), _row_spec(tm, D_ATTN), _full_spec((1, D_ATTN)),
                                       _full_spec(w_out.shape), _row_spec(tm, D_MODEL)],
        out_specs=[_row_spec(tm, D_MODEL), _row_spec(tm, D_ATTN)],
        out_shape=[jax.ShapeDtypeStruct((t, D_MODEL), F32), jax.ShapeDtypeStruct((t, D_ATTN), BF16)],
        compiler_params=_params(dimension_semantics=("arbitrary",)),
    )(*deps, ycn, o, g_attn, w_out, x1)


def _out_proj_bwd(dx2, o, g_attn, w_out, deps=()):
    t = dx2.shape[0]
    tm = TOKEN_ROWS
    deps = tuple(deps)

    def body(dx_ref, o_ref, g_ref, w_ref, dyc_ref, do_ref, dg_ref):
        @pl.when(pl.program_id(0) == 0)
        def _():
            dg_ref[...] = jnp.zeros_like(dg_ref)

        dxb = dx_ref[...]
        dyc_ref[...] = _dot_nt(dxb, w_ref[0:D_CONV, :])
        dya = _dot_nt(dxb, w_ref[D_CONV:, :])
        ov = o_ref[...]
        do, dg_rows = _rms_bwd(ov, _rms_stats(ov), g_ref[...], dya)
        do_ref[...] = do
        dg_ref[...] += jnp.sum(dg_rows, axis=0, keepdims=True)

    return pl.pallas_call(
        _skip(len(deps), body), name="out_proj_bwd", grid=(t // tm,),
        in_specs=[_ANY] * len(deps) + [_row_spec(tm, D_MODEL), _row_spec(tm, D_ATTN), _full_spec((1, D_ATTN)),
                                       _full_spec(w_out.shape)],
        out_specs=[_row_spec(tm, D_CONV), _row_spec(tm, D_ATTN), _full_spec((1, D_ATTN))],
        out_shape=[jax.ShapeDtypeStruct((t, D_CONV), F32), jax.ShapeDtypeStruct((t, D_ATTN), F32),
                   jax.ShapeDtypeStruct((1, D_ATTN), F32)],
        compiler_params=_params(dimension_semantics=("arbitrary",)),
    )(*deps, dx2, o, g_attn, w_out)


def _loss_bwd(x3, target, g):
    t = x3.shape[0]
    tm = TOKEN_ROWS

    def body(x_ref, t_ref, g_ref, loss_ref, dx_ref, dg_ref):
        @pl.when(pl.program_id(0) == 0)
        def _():
            loss_ref[...] = jnp.zeros_like(loss_ref)
            dg_ref[...] = jnp.zeros_like(dg_ref)

        xv = x_ref[...]
        r = _rms_stats(xv)
        gv = g_ref[...]
        err = xv * r * gv - t_ref[...]
        row = jnp.sum(err * err, axis=1, keepdims=True) * (0.5 / D_MODEL)
        loss_ref[...] += jnp.sum(row, axis=0, keepdims=True)
        dx, dg_rows = _rms_bwd(xv, r, gv, err * (1.0 / D_MODEL))
        dx_ref[...] = dx
        dg_ref[...] += jnp.sum(dg_rows, axis=0, keepdims=True)

    return pl.pallas_call(
        body, name="loss_bwd", grid=(t // tm,),
        in_specs=[_row_spec(tm, D_MODEL), _row_spec(tm, D_MODEL), _full_spec((1, D_MODEL))],
        out_specs=[_full_spec((1, LANES)), _row_spec(tm, D_MODEL), _full_spec((1, D_MODEL))],
        out_shape=[jax.ShapeDtypeStruct((1, LANES), F32), jax.ShapeDtypeStruct((t, D_MODEL), F32),
                   jax.ShapeDtypeStruct((1, D_MODEL), F32)],
        compiler_params=_params(dimension_semantics=("arbitrary",)),
    )(x3, target, g)


def _split_w_in(w_in_t):
    w_ag = w_in_t[:2 * D_CONV]
    w_qkv = w_in_t[2 * D_CONV:2 * D_CONV + 3 * D_ATTN]
    w_f = jnp.pad(w_in_t[2 * D_CONV + 3 * D_ATTN:], ((0, LANES - N_HEADS), (0, 0)))
    return w_ag, w_qkv, w_f


def _head_rows(v):
    return jnp.pad(v, ((0, HEAD_ROWS - N_HEADS),) + ((0, 0),) * (v.ndim - 1))


def _local_step(x, target, p, get_weights, put_grads, flush_grads):
    t = x.shape[0]
    fb = _head_rows(p["forget_b"].reshape(N_HEADS, 1))

    w, deps = get_weights("ffn1_w13", None)
    h1, gu1, act1 = _ffn_up(x, p["ffn1_norm"], w["ffn1_w13"], "ffn1_up", deps)
    w2, _ = get_weights("ffn1_w2", act1)
    w.update(w2)
    x1 = _ffn_down(x, act1, w["ffn1_w2"], "ffn1_down")
    wm, _ = get_weights("mix", x1)
    w.update(wm)
    w_ag, w_qkv, w_f = _split_w_in(w["w_in"])
    conv_w = jnp.pad(w["conv_w"], ((0, CONV_PAD - CONV_WIDTH), (0, 0)))
    h2, ag, qkv, fl = _mix_proj(x1, p["mix_norm"], w_ag, w_qkv, w_f)
    flt = _head_rows(fl[:, :N_HEADS].T)
    dcum = _gates_fwd(flt, fb)[:N_HEADS]
    drow = dcum.reshape(N_HEADS, 1, t)
    ycn, y_conv = _conv_fwd(ag, conv_w, p["conv_b"], p["conv_ln_g"], p["conv_ln_b"], p["out_norm_conv"])
    o, lse = _attn_fwd(qkv, drow, [ycn])
    _, deps = get_weights("ffn2:landed", o)
    x2, yan = _out_proj(ycn, o, p["out_norm_attn"], w["w_out"], x1, deps)
    w2, _ = get_weights("ffn2", x2)
    w.update(w2)
    x3, h3, gu2 = _ffn_fwd(x2, p["ffn2_norm"], w["ffn2_w13"], w["ffn2_w2"], "ffn2_fwd")
    loss, dx3, d_final = _loss_bwd(x3, target, p["final_norm"])

    g = {}
    dx2, dgu2, a2, g["ffn2_norm"], dx3_half, dx2_bf16 = _ffn_bwd(
        dx3, x2, gu2, p["ffn2_norm"], w["ffn2_w13"], w["ffn2_w2"], "ffn2_bwd")
    dw13 = _wgrad(h3, dgu2, N_CHIPS, "ffn2_dw13")
    dw2 = _wgrad(a2, dx3_half, 1, "ffn2_dw2").reshape(D_FF, D_MODEL)
    deps = put_grads("ffn2", {"ffn2_w13": dw13, "ffn2_w2": dw2})
    dyc, do, g["out_norm_attn"] = _out_proj_bwd(dx2_bf16, o, p["out_norm_attn"], w["w_out"], deps)
    deps = flush_grads("ffn2", [dyc])
    dw_out = _wgrad(jnp.concatenate([ycn, yan], axis=1), dx2_bf16, 1, "dw_out", deps).reshape(D_MODEL, D_MODEL)
    dq, dk, dv, ddrow = _attn_bwd(qkv, drow, lse, do)
    dflt, dfb = _gates_bwd(_head_rows(ddrow.reshape(N_HEADS, t)), flt, fb)
    g["forget_b"] = dfb[:N_HEADS, 0].reshape(1, N_HEADS)
    dfl = jnp.pad(dflt[:N_HEADS].T, ((0, 0), (0, LANES - N_HEADS)))
    dag, dconv_w, g["conv_b"], g["conv_ln_g"], g["conv_ln_b"], g["out_norm_conv"] = _conv_bwd(
        ag, y_conv, dyc, conv_w, p["conv_ln_g"], p["conv_ln_b"], p["out_norm_conv"])
    g["conv_w"] = dconv_w[:CONV_WIDTH]
    dproj = jnp.concatenate([dag, dq, dk, dv, dfl.astype(BF16)], axis=1)
    dx1, g["mix_norm"] = _mix_proj_bwd(dproj, dx2, x1, p["mix_norm"], w_ag, w_qkv, w_f)
    dw_in = _wgrad(dproj, h2, 1, "dw_in").reshape(dproj.shape[1], D_MODEL)[:N_IN]
    deps = put_grads("mix", {"w_in": dw_in, "w_out": dw_out})
    dx0, dgu1, a1, g["ffn1_norm"], dx1_half, _ = _ffn_bwd(
        dx1, x, gu1, p["ffn1_norm"], w["ffn1_w13"], w["ffn1_w2"], "ffn1_bwd", deps)
    g["final_norm"] = d_final
    g["loss"] = loss[:, :1]
    deps = flush_grads("mix", put_grads("small", g))
    dw2 = _wgrad(a1, dx1_half, 1, "ffn1_dw2", deps).reshape(D_FF, D_MODEL)
    deps = flush_grads("ffn1_w2", put_grads("ffn1_w2", {"ffn1_w2": dw2}))
    dw13 = _wgrad(h1, dgu1, N_CHIPS, "ffn1_dw13", deps)
    put_grads("ffn1_w13", {"ffn1_w13": dw13})
    return dx0


MESH = pl.DeviceIdType.MESH


def _place():
    x, y, c = lax.axis_index("x"), lax.axis_index("y"), lax.axis_index("c")
    chips = [(1 - x, y), (x, 1 - y), (1 - x, 1 - y)]
    return x, y, c, chips


def _hbm_out(shape, dtype):
    return jax.ShapeDtypeStruct(shape, dtype)


def _comm_call(body, name, ins, out_shapes, n_remote, in_place=False):
    return pl.pallas_call(
        body, name=name, in_specs=[_ANY] * len(ins), out_specs=[_ANY] * len(out_shapes), out_shape=out_shapes,
        scratch_shapes=[pltpu.SemaphoreType.DMA((n_remote,)), pltpu.SemaphoreType.DMA((n_remote,))],
        input_output_aliases={i: i for i in range(len(ins))} if in_place else {},
    )(*ins)


def _remote(src, dst, sems, n, to):
    send_sems, recv_sems = sems
    return pltpu.make_async_remote_copy(src_ref=src, dst_ref=dst, send_sem=send_sems.at[n], recv_sem=recv_sems.at[n],
                                        device_id=to, device_id_type=MESH)


HALF_ROWS_MULTIPLE = 32


def _halved_by_rows(rows):
    return rows % HALF_ROWS_MULTIPLE == 0


def _half_shape(rows, cols):
    return (rows // 2, cols) if _halved_by_rows(rows) else (rows, cols // 2)


def _half_index(rows, core):
    return (core, 0) if _halved_by_rows(rows) else (0, core)


def _half_of(ref, rows, cols, core, *lead):
    if _halved_by_rows(rows):
        return ref.at[(*lead, pl.ds(core * (rows // 2), rows // 2), slice(None))]
    return ref.at[(*lead, slice(None), pl.ds(core * (cols // 2), cols // 2))]


def _into_slot(shard, chip, dtype, name, deps=()):
    rows, cols = shard.shape
    half = _half_shape(rows, cols)
    by_rows = _halved_by_rows(rows)
    deps = tuple(deps)

    def body(k_ref, *refs):
        s_ref, o_ref = refs[len(deps):]
        o_ref[0] = s_ref[...].astype(dtype)

    return pl.pallas_call(
        body, name=name,
        grid_spec=pltpu.PrefetchScalarGridSpec(
            num_scalar_prefetch=1, grid=(2,),
            in_specs=[_ANY] * len(deps) + [pl.BlockSpec(half, lambda i, k_ref: (i, 0) if by_rows else (0, i))],
            out_specs=pl.BlockSpec((1,) + half, lambda i, k_ref: (k_ref[0], i, 0) if by_rows else (k_ref[0], 0, i))),
        out_shape=jax.ShapeDtypeStruct((N_CHIPS, rows, cols), dtype),
        compiler_params=_params(dimension_semantics=("arbitrary",)),
    )(chip, *deps, shard)


def _gather_shards(slots, name, ici=True, passed=()):
    n = len(slots)
    slots = list(slots) + list(passed)
    total = len(slots)

    def body(*refs):
        outs = refs[total:total + n]
        sems = refs[2 * total:2 * total + 2]
        x, y, c, chips = _place()
        me = 2 * x + y
        sibling = (x, y, 1 - c)

        def half(i, chip_index, core):
            return _half_of(outs[i], *slots[i].shape[1:], core, chip_index)

        sends = []
        if ici:
            for i in range(n):
                for j, chip in enumerate(chips):
                    cp = _remote(half(i, me, c), half(i, me, c), sems, 6 * i + j, (*chip, c))
                    cp.start()
                    sends.append(cp)
        for i in range(n):
            for j, chip in enumerate(chips):
                src_chip = 2 * chip[0] + chip[1]
                landed = half(i, src_chip, c)
                if ici:
                    _remote(landed, landed, sems, 6 * i + j, (*chip, c)).wait_recv()
                cp = _remote(landed, landed, sems, 6 * i + 3 + j, sibling)
                cp.start()
                sends.append(cp)
        for i in range(n):
            for j, chip in enumerate(chips):
                src_chip = 2 * chip[0] + chip[1]
                landed = half(i, src_chip, 1 - c)
                _remote(landed, landed, sems, 6 * i + 3 + j, sibling).wait_recv()
        for cp in sends:
            cp.wait_send()

    outs = [_hbm_out(s.shape, s.dtype) for s in slots]
    return _comm_call(body, name, slots, outs, 6 * n, in_place=True)


_HBM = pl.BlockSpec(memory_space=pltpu.HBM)
_SEM = pl.BlockSpec(memory_space=pltpu.SEMAPHORE)
_DATAFLOW = pltpu.SideEffectType.DATAFLOW_SIDE_EFFECTING


def _split_copy_start(name, bufs, n_copies, plan):
    n = len(bufs)

    def body(*refs):
        for send, _ in plan(refs[:n], (refs[n], refs[n + 1])):
            send.start()
        token = refs[-1]
        token[...] = jnp.zeros_like(token)

    out = pl.pallas_call(
        body, name=name,
        out_shape=(pltpu.SemaphoreType.DMA((n_copies,)), pltpu.SemaphoreType.DMA((n_copies,)),
                   *[pltpu.HBM(b.shape, b.dtype) for b in bufs], jax.ShapeDtypeStruct((8, LANES), F32)),
        in_specs=[_HBM] * n, out_specs=(_SEM, _SEM, *[_HBM] * n, pl.BlockSpec(memory_space=pltpu.VMEM)),
        input_output_aliases={i: 2 + i for i in range(n)},
        compiler_params=pltpu.CompilerParams(has_side_effects=_DATAFLOW),
    )(*[pltpu.with_memory_space_constraint(b, pltpu.HBM) for b in bufs])
    return out[0], out[1], list(out[2:2 + n]), out[-1]


def _split_copy_wait(name, started, plan, after, passed=()):
    send_sems, recv_sems, bufs, _ = started
    n = len(bufs)
    after = tuple(after)
    bufs = list(bufs) + list(passed)
    total = len(bufs)

    def body(*refs):
        for send, recv in plan(refs[:n], (refs[total], refs[total + 1])):
            send.wait_send()
            recv.wait_recv()

    out = pl.pallas_call(
        body, name=name, out_shape=tuple(pltpu.HBM(b.shape, b.dtype) for b in bufs),
        in_specs=[_HBM] * total + [_SEM, _SEM] + [_ANY] * len(after), out_specs=tuple([_HBM] * total),
        input_output_aliases={i: i for i in range(total)},
        compiler_params=pltpu.CompilerParams(has_side_effects=_DATAFLOW),
    )(*bufs, send_sems, recv_sems, *after)
    return list(out)


def _ici_gather_plan(slots):
    def plan(refs, sems):
        x, y, c, chips = _place()
        me = 2 * x + y
        copies = []
        for i, ref in enumerate(refs):
            for j, chip in enumerate(chips):
                mine = _half_of(ref, *slots[i].shape[1:], c, me)
                theirs = _half_of(ref, *slots[i].shape[1:], c, 2 * chip[0] + chip[1])
                to = (*chip, c)
                copies.append((_remote(mine, mine, sems, 3 * i + j, to), _remote(theirs, theirs, sems, 3 * i + j, to)))
        return copies

    return plan


def _ici_scatter_plan(n):
    def plan(refs, sems):
        x, y, c, chips = _place()
        copies = []
        for i in range(n):
            for j, chip in enumerate(chips):
                cp = _remote(refs[i].at[2 * chip[0] + chip[1]], refs[n + i].at[j], sems, 3 * i + j, (*chip, c))
                copies.append((cp, cp))
        return copies

    return plan


def _d2d_forward_plan(slots):
    def plan(refs, sems):
        x, y, c, chips = _place()
        sibling = (x, y, 1 - c)
        copies = []
        for i, ref in enumerate(refs):
            for j, chip in enumerate(chips):
                src_chip = 2 * chip[0] + chip[1]
                mine = _half_of(ref, *slots[i].shape[1:], c, src_chip)
                theirs = _half_of(ref, *slots[i].shape[1:], 1 - c, src_chip)
                copies.append((_remote(mine, mine, sems, 3 * i + j, sibling),
                               _remote(theirs, theirs, sems, 3 * i + j, sibling)))
        return copies

    return plan


def _pair_exchange_plan(grads):
    n = len(grads)

    def plan(refs, sems):
        x, y, c, _ = _place()
        copies = []
        for i in range(n):
            theirs = _half_of(refs[i], *grads[i].shape[1:], 1 - c, slice(None))
            cp = _remote(theirs, refs[n + i], sems, i, (x, y, 1 - c))
            copies.append((cp, cp))
        return copies

    return plan


def _pair_share_plan(shapes):
    def plan(refs, sems):
        x, y, c, _ = _place()
        sibling = (x, y, 1 - c)
        copies = []
        for i, ref in enumerate(refs):
            mine, theirs = _half_of(ref, *shapes[i], c), _half_of(ref, *shapes[i], 1 - c)
            copies.append((_remote(mine, mine, sems, i, sibling), _remote(theirs, theirs, sems, i, sibling)))
        return copies

    return plan


def _pair_share(halves, name):
    n = len(halves)
    plan = _pair_share_plan([h.shape for h in halves])

    def body(*refs):
        copies = plan(refs[n:2 * n], refs[2 * n:2 * n + 2])
        for send, _ in copies:
            send.start()
        for send, recv in copies:
            send.wait_send()
            recv.wait_recv()

    outs = [_hbm_out(h.shape, h.dtype) for h in halves]
    return _comm_call(body, name, halves, outs, n, in_place=True)


N_DEVICES = 8
FLIPS = [(fx, fy, fc) for fx in range(2) for fy in range(2) for fc in range(2)][1:]


def _small_slots(v, me):
    rows = v.shape[0]

    def body(k_ref, v_ref, o_ref):
        o_ref[0] = v_ref[...]

    return pl.pallas_call(
        body, name="small_slot",
        grid_spec=pltpu.PrefetchScalarGridSpec(
            num_scalar_prefetch=1, grid=(1,),
            in_specs=[pl.BlockSpec((rows, LANES), lambda i, k_ref: (0, 0))],
            out_specs=pl.BlockSpec((1, rows, LANES), lambda i, k_ref: (k_ref[0], 0, 0))),
        out_shape=jax.ShapeDtypeStruct((N_DEVICES, rows, LANES), F32),
        compiler_params=_params(dimension_semantics=("arbitrary",)),
    )(me, v)


def _small_plan():
    def plan(refs, sems):
        x, y, c, _ = _place()
        slots = refs[0]
        me = 4 * x + 2 * y + c
        copies = []
        for n, (fx, fy, fc) in enumerate(FLIPS):
            to = (x ^ fx, y ^ fy, c ^ fc)
            src = 4 * to[0] + 2 * to[1] + to[2]
            copies.append((_remote(slots.at[me], slots.at[me], sems, n, to), _remote(slots.at[src], slots.at[src], sems, n, to)))
        return copies

    return plan


def _small_sum(slots):
    def body(s_ref, o_ref):
        acc = s_ref[0]
        for s in range(1, N_DEVICES):
            acc = acc + s_ref[s]
        o_ref[...] = acc

    return pl.pallas_call(body, name="small_sum", out_shape=jax.ShapeDtypeStruct(slots.shape[1:], F32),
                          compiler_params=_params())(slots)


def _pair_add(gs, sibs, core, name):
    n = len(gs)
    halves = [_half_shape(*g.shape[1:]) for g in gs]

    def body(c_ref, *refs):
        for g_ref, s_ref, o_ref in zip(refs[:n], refs[n:2 * n], refs[2 * n:]):
            o_ref[0] = (g_ref[0].astype(F32) + s_ref[0].astype(F32)).astype(BF16)

    def mine(g, half):
        return pl.BlockSpec((1,) + half, lambda s, c_ref: (s, *_half_index(g.shape[1], c_ref[0])))

    whole = [pl.BlockSpec((1,) + half, lambda s, c_ref: (s, 0, 0)) for half in halves]
    return pl.pallas_call(
        body, name=name,
        grid_spec=pltpu.PrefetchScalarGridSpec(
            num_scalar_prefetch=1, grid=(N_CHIPS,),
            in_specs=[mine(g, half) for g, half in zip(gs, halves)] + whole, out_specs=whole),
        out_shape=[jax.ShapeDtypeStruct((N_CHIPS,) + half, BF16) for half in halves],
        compiler_params=_params(dimension_semantics=("arbitrary",)),
    )(core, *gs, *sibs)


def _chip_add(parts, recvs, chip_core, shapes, name):
    n = len(parts)
    halves = [_half_shape(*shape) for shape in shapes]

    def body(kc_ref, *refs):
        for p_ref, r_ref, o_ref in zip(refs[:n], refs[n:2 * n], refs[2 * n:]):
            acc = p_ref[0].astype(F32)
            for j in range(N_CHIPS - 1):
                acc = acc + r_ref[j].astype(F32)
            o_ref[...] = acc

    def out_spec(shape, half):
        return pl.BlockSpec(half, lambda s, kc_ref: _half_index(shape[0], kc_ref[1]))

    return pl.pallas_call(
        body, name=name,
        grid_spec=pltpu.PrefetchScalarGridSpec(
            num_scalar_prefetch=1, grid=(1,),
            in_specs=[pl.BlockSpec((1,) + half, lambda s, kc_ref: (kc_ref[0], 0, 0)) for half in halves]
            + [pl.BlockSpec((N_CHIPS - 1,) + half, lambda s, kc_ref: (0, 0, 0)) for half in halves],
            out_specs=[out_spec(shape, half) for shape, half in zip(shapes, halves)]),
        out_shape=[jax.ShapeDtypeStruct(tuple(shape), F32) for shape in shapes],
        compiler_params=_params(dimension_semantics=("arbitrary",)),
    )(chip_core, *parts, *recvs)


def _adamw_math(w, g, m, v):
    m = ADAM_B1 * m + (1.0 - ADAM_B1) * g
    v = ADAM_B2 * v + (1.0 - ADAM_B2) * (g * g)
    m_hat = m / (1.0 - ADAM_B1 ** ADAM_STEP)
    v_hat = v / (1.0 - ADAM_B2 ** ADAM_STEP)
    delta = -ADAM_LR * (m_hat / (jnp.sqrt(v_hat) + ADAM_EPS) + ADAM_WD * w)
    return delta, m, v


ADAM_PARTS = 4


def _adamw_matrix(w, g, m, v, name):
    rows, cols = w.shape
    by_rows = rows % (8 * ADAM_PARTS) == 0
    block = (rows // ADAM_PARTS, cols) if by_rows else (rows, cols // ADAM_PARTS)

    def body(w_ref, g_ref, m_ref, v_ref, go_ref, d_ref, mo_ref, vo_ref):
        gv = g_ref[...]
        go_ref[...] = gv
        d_ref[...], mo_ref[...], vo_ref[...] = _adamw_math(w_ref[...], gv, m_ref[...], v_ref[...])

    spec = pl.BlockSpec(block, lambda i: (i, 0) if by_rows else (0, i))
    shape = jax.ShapeDtypeStruct((rows, cols), F32)
    return pl.pallas_call(
        body, name=name, grid=(ADAM_PARTS,), in_specs=[spec] * 4, out_specs=[spec] * 4, out_shape=[shape] * 4,
        compiler_params=_params(dimension_semantics=("arbitrary",)),
    )(w, g, m, v)


def _adamw_small(ws, gs, ms, vs):
    n = len(ws)

    def body(*refs):
        for i in range(n):
            w_ref, g_ref, m_ref, v_ref = (refs[k * n + i] for k in range(4))
            d_ref, mo_ref, vo_ref = (refs[(4 + k) * n + i] for k in range(3))
            d_ref[...], mo_ref[...], vo_ref[...] = _adamw_math(w_ref[...], g_ref[...], m_ref[...], v_ref[...])

    shapes = [jax.ShapeDtypeStruct(w.shape, F32) for w in ws]
    out = pl.pallas_call(body, name="adamw_small", out_shape=shapes * 3, compiler_params=_params())(*ws, *gs, *ms, *vs)
    return out[:n], out[n:2 * n], out[2 * n:]


MATRICES = ["ffn1_w13", "ffn1_w2", "w_in", "w_out", "ffn2_w13", "ffn2_w2"]
VECTORS = ["ffn1_norm", "mix_norm", "conv_b", "conv_ln_g", "conv_ln_b", "forget_b", "out_norm_conv",
           "out_norm_attn", "ffn2_norm", "final_norm"]
WEIGHTS = ["ffn1_norm", "ffn1_w13", "ffn1_w2", "mix_norm", "w_in", "conv_w", "conv_b", "conv_ln_g", "conv_ln_b",
           "forget_b", "out_norm_conv", "out_norm_attn", "w_out", "ffn2_norm", "ffn2_w13", "ffn2_w2", "final_norm"]


def _pack_small(g, names):
    rows, layout = [], []
    for n in names:
        flat = g[n].reshape(-1)
        pad = (-flat.shape[0]) % LANES
        rows.append(jnp.pad(flat, (0, pad)).reshape(-1, LANES))
        layout.append((n, g[n].shape, flat.shape[0], rows[-1].shape[0]))
    packed = jnp.concatenate(rows, axis=0)
    pad_rows = (-packed.shape[0]) % 8
    return jnp.pad(packed, ((0, pad_rows), (0, 0))), layout


def _unpack_small(packed, layout):
    out, r = {}, 0
    for n, shape, size, nrows in layout:
        out[n] = packed[r:r + nrows].reshape(-1)[:size].reshape(shape)
        r += nrows
    return out


def kernel(x, ffn1_norm, ffn1_w13, ffn1_w2, mix_norm, w_in, conv_w, conv_b, conv_ln_g, conv_ln_b, forget_b, out_norm_conv, out_norm_attn, w_out, ffn2_norm, ffn2_w13, ffn2_w2, final_norm, loss_target, m_ffn1_norm, m_ffn1_w13, m_ffn1_w2, m_mix_norm, m_w_in, m_conv_w, m_conv_b, m_conv_ln_g, m_conv_ln_b, m_forget_b, m_out_norm_conv, m_out_norm_attn, m_w_out, m_ffn2_norm, m_ffn2_w13, m_ffn2_w2, m_final_norm, v_ffn1_norm, v_ffn1_w13, v_ffn1_w2, v_mix_norm, v_w_in, v_conv_w, v_conv_b, v_conv_ln_g, v_conv_ln_b, v_forget_b, v_out_norm_conv, v_out_norm_attn, v_w_out, v_ffn2_norm, v_ffn2_w13, v_ffn2_w2, v_final_norm):
    args = dict(locals())
    weights = {n: args[n] for n in WEIGHTS}
    core = lax.axis_index("c").astype(jnp.int32).reshape(1)
    chip = (2 * lax.axis_index("x") + lax.axis_index("y")).astype(jnp.int32)
    chip1 = chip.reshape(1)
    chip_core = jnp.concatenate([chip1, core])

    def held(n, a):
        return a[0].T if n == "w_in" else a[0]

    def given(n, a):
        return (a.T if n == "w_in" else a)[None]

    def slot(n, deps=()):
        if n == "conv_w":
            rows = jnp.pad(conv_w[0], ((0, CONV_PAD - CONV_WIDTH), (0, 0)))
            return _into_slot(rows, chip1, F32, "slot_conv_w", deps)
        return _into_slot(held(n, weights[n]), chip1, BF16, "slot_" + n, deps)

    fetched = {"ffn1_w13": ["ffn1_w13"], "ffn1_w2": ["ffn1_w2"], "mix": ["w_in", "w_out", "conv_w"],
               "ffn2": ["ffn2_w13", "ffn2_w2"]}
    fetch = {}

    def as_weights(group, bufs):
        out = {}
        for n, b in zip(fetched[group], bufs):
            if n.endswith("w13"):
                out[n] = b
            elif n != "conv_w":
                out[n] = b.reshape(N_CHIPS * b.shape[1], b.shape[2])
            else:
                out[n] = b[:, :CONV_WIDTH].transpose(1, 0, 2).reshape(CONV_WIDTH, D_CONV)
        return out

    def get_weights(group, after):
        if group == "ffn1_w13":
            first = [slot("ffn1_w13")]
            plan = _ici_gather_plan(first)
            started = _split_copy_start("gather_ffn1_w13_start", first, 3, plan)
            second = [slot("ffn1_w2", [started[3]])]
            plan2 = _ici_gather_plan(second)
            fetch["ffn1_w2"] = plan2, _split_copy_start("gather_ffn1_w2_start", second, 3, plan2)
            later_names = fetched["mix"] + fetched["ffn2"]
            later = [slot(n, [fetch["ffn1_w2"][1][3]]) for n in later_names]
            landed = _split_copy_wait("gather_ffn1_w13_wait", started, plan, [], passed=later)
            bufs = _gather_shards(landed[:1], "forward_ffn1_w13", ici=False)
            behind = dict(zip(later_names, landed[1:]))
            for later in ("mix", "ffn2"):
                bufs_later = [behind[n] for n in fetched[later]]
                plan = _ici_gather_plan(bufs_later)
                fetch[later] = plan, _split_copy_start("gather_%s_start" % later, bufs_later, 3 * len(bufs_later), plan)
            return as_weights(group, bufs), [fetch["mix"][1][3], fetch["ffn2"][1][3]]
        plan, started = fetch[group.split(":")[0]]
        if group == "ffn2:landed":
            landed = _split_copy_wait("gather_ffn2_wait", started, plan, [after])
            plan = _d2d_forward_plan(landed)
            fetch["ffn2"] = plan, _split_copy_start("forward_ffn2_start", landed, 3 * len(landed), plan)
            return {}, [fetch["ffn2"][1][3]]
        if group == "ffn2":
            return as_weights(group, _split_copy_wait("forward_ffn2_wait", started, plan, [after])), []
        landed = _split_copy_wait("gather_%s_wait" % group, started, plan, [after])
        return as_weights(group, _gather_shards(landed, "forward_" + group, ici=False)), []

    def shard_major(n, g):
        return g if n.endswith("w13") else g.reshape(N_CHIPS, g.shape[0] // N_CHIPS, g.shape[1])

    exchange, scatter = {}, {}
    small_names = VECTORS + ["conv_w"]
    small = {}

    def put_grads(group, grads):
        if group == "small":
            packed, layout = _pack_small(grads, small_names + ["loss"])
            me = (4 * lax.axis_index("x") + 2 * lax.axis_index("y") + lax.axis_index("c")).astype(jnp.int32).reshape(1)
            plan = _small_plan()
            exchange[group] = layout, plan, _split_copy_start("small_start", [_small_slots(packed, me)], len(FLIPS), plan)
            return [exchange[group][2][3]]
        names = list(grads)
        local = [shard_major(n, grads[n]) for n in names]
        landing = [lax.empty((N_CHIPS,) + _half_shape(*a.shape[1:]), BF16) for a in local]
        plan = _pair_exchange_plan(local)
        exchange[group] = names, plan, _split_copy_start("exchange_%s_start" % group, local + landing, len(local), plan)
        return [exchange[group][2][3]]

    def flush_grads(group, after):
        names, plan, started = exchange[group]
        done = _split_copy_wait("exchange_%s_wait" % group, started, plan, after)
        local, sib = done[:len(names)], done[len(names):]
        parts = list(_pair_add(local, sib, core, "pair_add_" + group))
        landing = [lax.empty((N_CHIPS - 1,) + q.shape[1:], BF16) for q in parts]
        plan = _ici_scatter_plan(len(parts))
        shapes = [a.shape[1:] for a in local]
        scatter[group] = names, plan, _split_copy_start("scatter_%s_start" % group, parts + landing, 3 * len(parts), plan), shapes
        return [scatter[group][2][3]]

    p = {n: weights[n] for n in VECTORS}
    p["final_norm"] = final_norm.reshape(1, D_MODEL)
    dx = _local_step(x[0], loss_target[0], p, get_weights, put_grads, flush_grads)
    layout, plan, started = exchange["small"]
    slots, = _split_copy_wait("small_wait", started, plan, [exchange["ffn1_w13"][2][3]])
    small.update(_unpack_small(_small_sum(slots), layout))
    loss = small["loss"].reshape(())

    grad = {n: small[n] for n in VECTORS}
    grad["final_norm"] = small["final_norm"].reshape(D_MODEL)
    grad["conv_w"] = lax.dynamic_slice_in_dim(small["conv_w"], chip * (D_CONV // N_CHIPS), D_CONV // N_CHIPS, axis=1)[None]

    delta, new_m, new_v = {}, {}, {}

    def reduce_chips(group, after):
        names, plan, started, shapes = scatter[group]
        done = _split_copy_wait("scatter_%s_wait" % group, started, plan, after)
        parts, landed = done[:len(names)], done[len(names):]
        return list(_chip_add(parts, landed, chip_core, shapes, "chip_add_" + group))

    def update(group, full):
        ends = []
        for n, reduced in zip(scatter[group][0], full):
            go, d, mo, vo = _adamw_matrix(held(n, weights[n]), reduced, held(n, args["m_" + n]), held(n, args["v_" + n]),
                                          "adamw_" + n)
            grad[n], delta[n], new_m[n], new_v[n] = given(n, go), given(n, d), given(n, mo), given(n, vo)
            ends.append(vo)
        return ends

    def share_start(group, halves):
        plan = _pair_share_plan(scatter[group][3])
        return plan, _split_copy_start("share_%s_start" % group, halves, len(halves), plan)

    halves_ffn2 = reduce_chips("ffn2", [exchange["ffn1_w13"][2][3]])
    plan_ffn2, share_ffn2 = share_start("ffn2", halves_ffn2)
    last_scatter = flush_grads("ffn1_w13", [share_ffn2[3]])
    halves_mix = reduce_chips("mix", last_scatter)
    plan_mix, share_mix = share_start("mix", halves_mix)
    done_ffn2 = update("ffn2", _split_copy_wait("share_ffn2_wait", share_ffn2, plan_ffn2, [share_mix[3]]))
    done_mix = update("mix", _split_copy_wait("share_mix_wait", share_mix, plan_mix, done_ffn2))
    as2d = lambda a: a.reshape(-1, a.shape[-1])
    ds, mos, vos = _adamw_small([as2d(weights[n]) for n in small_names], [as2d(grad[n]) for n in small_names],
                                [as2d(args["m_" + n]) for n in small_names], [as2d(args["v_" + n]) for n in small_names])
    for n, d, mo, vo in zip(small_names, ds, mos, vos):
        shape = weights[n].shape
        delta[n], new_m[n], new_v[n] = d.reshape(shape), mo.reshape(shape), vo.reshape(shape)
    behind = done_ffn2 + done_mix + [vos[0]]
    halves_w2 = reduce_chips("ffn1_w2", behind)
    halves_w13 = reduce_chips("ffn1_w13", behind)
    full_w2, full_w13 = _pair_share(halves_w2 + halves_w13, "pair_share_ffn1")
    update("ffn1_w2", [full_w2])
    update("ffn1_w13", [full_w13])

    return (loss, dx[None], *[grad[n] for n in WEIGHTS], *[delta[n] for n in WEIGHTS],
            *[new_m[n] for n in WEIGHTS], *[new_v[n] for n in WEIGHTS])
```

```python
import functools

import jax
import jax.numpy as jnp
from jax import lax
from jax.experimental import pallas as pl
from jax.experimental.pallas import tpu as pltpu

F32 = jnp.float32
BF16 = jnp.bfloat16

D_MODEL = 1024
D_FF = 2816
FF_SHARD = D_FF // 2
D_CONV = 512
D_ATTN = 512
N_HEADS = 8
HEAD_DIM = 64
CONV_WIDTH = 31
CONV_PAD = 32
N_IN = 2 * D_CONV + 3 * D_ATTN + N_HEADS
EPS = 1e-6
N_CHIPS = 4
LANES = 128
TOKEN_ROWS = 512
HEAD_ROWS = 16

ADAM_LR = 0.001
ADAM_B1 = 0.9
ADAM_B2 = 0.999
ADAM_EPS = 1e-08
ADAM_WD = 0.01
ADAM_STEP = 10

VMEM_LIMIT = 56 * 1024 * 1024

_NT = (((1,), (1,)), ((), ()))
_TN = (((0,), (0,)), ((), ()))


def _dot(a, b):
    return jnp.dot(a, b, preferred_element_type=F32)


def _dot_nt(a, b):
    return lax.dot_general(a, b, _NT, preferred_element_type=F32)


def _dot_tn(a, b):
    return lax.dot_general(a, b, _TN, preferred_element_type=F32)


def _params(**kw):
    return pltpu.CompilerParams(vmem_limit_bytes=VMEM_LIMIT, **kw)


def _sigmoid(x):
    return 1.0 / (1.0 + jnp.exp(-x))


def _rms_stats(x):
    return lax.rsqrt(jnp.mean(x * x, axis=-1, keepdims=True) + EPS)


def _rms_bwd(x, r, g, dh):
    t = dh * g
    dx = r * t - x * (r * r * r) * jnp.mean(t * x, axis=-1, keepdims=True)
    return dx, dh * x * r


def _silu_grad(z, sg):
    return sg * (1.0 + z * (1.0 - sg))


def _row_spec(tm, n):
    return pl.BlockSpec((tm, n), lambda i: (i, 0))


def _full_spec(shape):
    nd = len(shape)
    return pl.BlockSpec(shape, lambda i: (0,) * nd)


_ANY = pl.BlockSpec(memory_space=pl.ANY)


def _skip(n, body):
    return lambda *refs: body(*refs[n:])


FFN_ROWS = 256
FFN_WEIGHT_PARTS = N_CHIPS + 2


def _with_ffn_weights(w13_hbm, w2_hbm, w13_ref, w2_ref, sems, order, tile):
    first = pl.program_id(0) == 0
    copies = {}
    if w13_hbm is not None:
        for k in range(N_CHIPS):
            copies["w13", k] = pltpu.make_async_copy(w13_hbm.at[k], w13_ref.at[k], sems.at[k])
    if w2_hbm is not None:
        for half in range(2):
            rows = pl.ds(half * FF_SHARD, FF_SHARD)
            copies["w2", half] = pltpu.make_async_copy(w2_hbm.at[rows, :], w2_ref.at[rows, :], sems.at[N_CHIPS + half])

    @pl.when(first)
    def _():
        for part in order:
            copies[part].start()

        def ready(*parts):
            for part in parts:
                copies[part].wait()

        tile(ready)

    @pl.when(jnp.logical_not(first))
    def _():
        tile(lambda *parts: None)


def _ffn_fwd(x, g, w13s, w2, name, deps=()):
    t = x.shape[0]
    tm = FFN_ROWS
    deps = tuple(deps)

    def body(x_ref, g_ref, w13_hbm, w2_hbm, xo_ref, h_ref, gu_ref, a_ref, w13_ref, w2_ref, sems):
        def tile(ready):
            xv = x_ref[...]
            hb = (xv * _rms_stats(xv) * g_ref[...]).astype(BF16)
            h_ref[...] = hb
            acc = jnp.zeros((tm, D_MODEL), F32)
            for half in range(2):
                lo = half * FF_SHARD
                ready(("w13", half), ("w13", 2 + half))
                gate = _dot(hb, w13_ref[half])
                up = _dot(hb, w13_ref[2 + half])
                gu_ref[:, lo:lo + FF_SHARD] = gate.astype(BF16)
                gu_ref[:, D_FF + lo:D_FF + lo + FF_SHARD] = up.astype(BF16)
                a = (gate * _sigmoid(gate) * up).astype(BF16)
                a_ref[:, lo:lo + FF_SHARD] = a
                ready(("w2", half))
                acc = acc + _dot(a, w2_ref[lo:lo + FF_SHARD, :])
            xo_ref[...] = xv + 0.5 * acc

        _with_ffn_weights(w13_hbm, w2_hbm, w13_ref, w2_ref, sems,
                          [("w13", 0), ("w13", 2), ("w2", 0), ("w13", 1), ("w13", 3), ("w2", 1)], tile)

    return pl.pallas_call(
        _skip(len(deps), body), name=name, grid=(t // tm,),
        in_specs=[_ANY] * len(deps) + [_row_spec(tm, D_MODEL), _full_spec((1, D_MODEL)), _ANY, _ANY],
        out_specs=[_row_spec(tm, D_MODEL), _row_spec(tm, D_MODEL), _row_spec(tm, 2 * D_FF), _row_spec(tm, D_FF)],
        out_shape=[jax.ShapeDtypeStruct((t, D_MODEL), F32), jax.ShapeDtypeStruct((t, D_MODEL), BF16),
                   jax.ShapeDtypeStruct((t, 2 * D_FF), BF16), jax.ShapeDtypeStruct((t, D_FF), BF16)],
        scratch_shapes=[pltpu.VMEM(w13s.shape, BF16), pltpu.VMEM(w2.shape, BF16),
                        pltpu.SemaphoreType.DMA((FFN_WEIGHT_PARTS,))],
        compiler_params=_params(dimension_semantics=("arbitrary",)),
    )(*deps, x, g, w13s, w2)


def _ffn_up(x, g, w13s, name, deps=()):
    t = x.shape[0]
    tm = FFN_ROWS
    deps = tuple(deps)

    def body(x_ref, g_ref, w13_hbm, h_ref, gu_ref, a_ref, w13_ref, sems):
        def tile(ready):
            xv = x_ref[...]
            hb = (xv * _rms_stats(xv) * g_ref[...]).astype(BF16)
            h_ref[...] = hb
            for half in range(2):
                lo = half * FF_SHARD
                ready(("w13", half), ("w13", 2 + half))
                gate = _dot(hb, w13_ref[half])
                up = _dot(hb, w13_ref[2 + half])
                gu_ref[:, lo:lo + FF_SHARD] = gate.astype(BF16)
                gu_ref[:, D_FF + lo:D_FF + lo + FF_SHARD] = up.astype(BF16)
                a_ref[:, lo:lo + FF_SHARD] = (gate * _sigmoid(gate) * up).astype(BF16)

        _with_ffn_weights(w13_hbm, None, w13_ref, None, sems, [("w13", 0), ("w13", 2), ("w13", 1), ("w13", 3)], tile)

    return pl.pallas_call(
        _skip(len(deps), body), name=name, grid=(t // tm,),
        in_specs=[_ANY] * len(deps) + [_row_spec(tm, D_MODEL), _full_spec((1, D_MODEL)), _ANY],
        out_specs=[_row_spec(tm, D_MODEL), _row_spec(tm, 2 * D_FF), _row_spec(tm, D_FF)],
        out_shape=[jax.ShapeDtypeStruct((t, D_MODEL), BF16), jax.ShapeDtypeStruct((t, 2 * D_FF), BF16),
                   jax.ShapeDtypeStruct((t, D_FF), BF16)],
        scratch_shapes=[pltpu.VMEM(w13s.shape, BF16), pltpu.SemaphoreType.DMA((FFN_WEIGHT_PARTS,))],
        compiler_params=_params(dimension_semantics=("arbitrary",)),
    )(*deps, x, g, w13s)


def _ffn_down(x, a, w2, name):
    t = x.shape[0]
    tm = FFN_ROWS

    def body(x_ref, a_ref, w2_hbm, xo_ref, w2_ref, sems):
        def tile(ready):
            ready(("w2", 0))
            acc = _dot(a_ref[:, 0:FF_SHARD], w2_ref[0:FF_SHARD, :])
            ready(("w2", 1))
            acc = acc + _dot(a_ref[:, FF_SHARD:], w2_ref[FF_SHARD:, :])
            xo_ref[...] = x_ref[...] + 0.5 * acc

        _with_ffn_weights(None, w2_hbm, None, w2_ref, sems, [("w2", 0), ("w2", 1)], tile)

    return pl.pallas_call(
        body, name=name, grid=(t // tm,),
        in_specs=[_row_spec(tm, D_MODEL), _row_spec(tm, D_FF), _ANY],
        out_specs=_row_spec(tm, D_MODEL), out_shape=jax.ShapeDtypeStruct((t, D_MODEL), F32),
        scratch_shapes=[pltpu.VMEM(w2.shape, BF16), pltpu.SemaphoreType.DMA((FFN_WEIGHT_PARTS,))],
        compiler_params=_params(dimension_semantics=("arbitrary",)),
    )(x, a, w2)


def _ffn_bwd(dy, x, gu, g, w13s, w2, name, deps=()):
    t = x.shape[0]
    tm = FFN_ROWS
    deps = tuple(deps)

    def body(dy_ref, x_ref, gu_ref, g_ref, w13_hbm, w2_hbm, dx_ref, dgu_ref, dg_ref, dyh_ref, dxb_ref,
             w13_ref, w2_ref, sems):
        @pl.when(pl.program_id(0) == 0)
        def _():
            dg_ref[...] = jnp.zeros_like(dg_ref)

        def tile(ready):
            dyv = dy_ref[...]
            dyh = (0.5 * dyv).astype(BF16)
            dyh_ref[...] = dyh
            dh = jnp.zeros((tm, D_MODEL), F32)
            for half in range(2):
                lo = half * FF_SHARD
                ready(("w2", half))
                da = _dot_nt(dyh, w2_ref[lo:lo + FF_SHARD, :])
                gate = gu_ref[:, lo:lo + FF_SHARD].astype(F32)
                up = gu_ref[:, D_FF + lo:D_FF + lo + FF_SHARD].astype(F32)
                sg = _sigmoid(gate)
                act = gate * sg
                dgate = (da * up * _silu_grad(gate, sg)).astype(BF16)
                dup = (da * act).astype(BF16)
                dgu_ref[:, lo:lo + FF_SHARD] = dgate
                dgu_ref[:, D_FF + lo:D_FF + lo + FF_SHARD] = dup
                ready(("w13", half), ("w13", 2 + half))
                dh = dh + _dot_nt(dgate, w13_ref[half]) + _dot_nt(dup, w13_ref[2 + half])
            xv = x_ref[...]
            dxn, dg_rows = _rms_bwd(xv, _rms_stats(xv), g_ref[...], dh)
            dx = dyv + dxn
            dx_ref[...] = dx
            dxb_ref[...] = dx.astype(BF16)
            dg_ref[...] += jnp.sum(dg_rows, axis=0, keepdims=True)

        _with_ffn_weights(w13_hbm, w2_hbm, w13_ref, w2_ref, sems,
                          [("w2", 0), ("w13", 0), ("w13", 2), ("w2", 1), ("w13", 1), ("w13", 3)], tile)

    return pl.pallas_call(
        _skip(len(deps), body), name=name, grid=(t // tm,),
        in_specs=[_ANY] * len(deps) + [_row_spec(tm, D_MODEL), _row_spec(tm, D_MODEL), _row_spec(tm, 2 * D_FF),
                                       _full_spec((1, D_MODEL)), _ANY, _ANY],
        out_specs=[_row_spec(tm, D_MODEL), _row_spec(tm, 2 * D_FF),
                   _full_spec((1, D_MODEL)), _row_spec(tm, D_MODEL), _row_spec(tm, D_MODEL)],
        out_shape=[jax.ShapeDtypeStruct((t, D_MODEL), F32), jax.ShapeDtypeStruct((t, 2 * D_FF), BF16),
                   jax.ShapeDtypeStruct((1, D_MODEL), F32),
                   jax.ShapeDtypeStruct((t, D_MODEL), BF16), jax.ShapeDtypeStruct((t, D_MODEL), BF16)],
        scratch_shapes=[pltpu.VMEM(w13s.shape, BF16), pltpu.VMEM(w2.shape, BF16),
                        pltpu.SemaphoreType.DMA((FFN_WEIGHT_PARTS,))],
        compiler_params=_params(dimension_semantics=("arbitrary",)),
    )(*deps, dy, x, gu, g, w13s, w2)


WGRAD_ROWS = (512, 384, 256)


def _wgrad(a, b, n_blocks, name, deps=()):
    t, m = a.shape
    tm = next(rows for rows in WGRAD_ROWS if m % rows == 0)
    n = b.shape[1]
    bn = n // n_blocks
    deps = tuple(deps)
    assert a.dtype == BF16 and b.dtype == BF16

    def body(a_ref, b_ref, o_ref):
        o_ref[0] = _dot_tn(a_ref[...], b_ref[...]).astype(BF16)

    return pl.pallas_call(
        _skip(len(deps), body), name=name, grid=(n_blocks, m // tm),
        in_specs=[_ANY] * len(deps) + [pl.BlockSpec((t, tm), lambda j, i: (0, i)),
                                       pl.BlockSpec((t, bn), lambda j, i: (0, j))],
        out_specs=pl.BlockSpec((1, tm, bn), lambda j, i: (j, i, 0)),
        out_shape=jax.ShapeDtypeStruct((n_blocks, m, bn), BF16),
        compiler_params=_params(dimension_semantics=("arbitrary", "arbitrary")),
    )(*deps, a, b)


def _mix_proj(x, g, w_ag, w_qkv, w_f):
    t = x.shape[0]
    tm = TOKEN_ROWS

    def body(x_ref, g_ref, wag_ref, wqkv_ref, wf_ref, h_ref, ag_ref, qkv_ref, fl_ref):
        xv = x_ref[...]
        hb = (xv * _rms_stats(xv) * g_ref[...]).astype(BF16)
        h_ref[...] = hb
        ag_ref[...] = _dot_nt(hb, wag_ref[...])
        qkv_ref[...] = _dot_nt(hb, wqkv_ref[...]).astype(BF16)
        fl_ref[...] = _dot_nt(hb, wf_ref[...])

    return pl.pallas_call(
        body, name="mix_proj", grid=(t // tm,),
        in_specs=[_row_spec(tm, D_MODEL), _full_spec((1, D_MODEL)), _full_spec(w_ag.shape),
                  _full_spec(w_qkv.shape), _full_spec(w_f.shape)],
        out_specs=[_row_spec(tm, D_MODEL), _row_spec(tm, 2 * D_CONV), _row_spec(tm, 3 * D_ATTN),
                   _row_spec(tm, LANES)],
        out_shape=[jax.ShapeDtypeStruct((t, D_MODEL), BF16), jax.ShapeDtypeStruct((t, 2 * D_CONV), F32),
                   jax.ShapeDtypeStruct((t, 3 * D_ATTN), BF16), jax.ShapeDtypeStruct((t, LANES), F32)],
        compiler_params=_params(dimension_semantics=("arbitrary",)),
    )(x, g, w_ag, w_qkv, w_f)


def _mix_proj_bwd(dproj, dx2, x1, g, w_ag, w_qkv, w_f):
    t = x1.shape[0]
    tm = TOKEN_ROWS
    n_ag, n_qkv = 2 * D_CONV, 3 * D_ATTN

    def body(dp_ref, dx2_ref, x_ref, g_ref, wag_ref, wqkv_ref, wf_ref, dx_ref, dg_ref):
        @pl.when(pl.program_id(0) == 0)
        def _():
            dg_ref[...] = jnp.zeros_like(dg_ref)

        dh = (_dot(dp_ref[:, 0:n_ag], wag_ref[...]) + _dot(dp_ref[:, n_ag:n_ag + n_qkv], wqkv_ref[...])
              + _dot(dp_ref[:, n_ag + n_qkv:], wf_ref[...]))
        xv = x_ref[...]
        dxn, dg_rows = _rms_bwd(xv, _rms_stats(xv), g_ref[...], dh)
        dx_ref[...] = dx2_ref[...] + dxn
        dg_ref[...] += jnp.sum(dg_rows, axis=0, keepdims=True)

    return pl.pallas_call(
        body, name="mix_proj_bwd", grid=(t // tm,),
        in_specs=[_row_spec(tm, dproj.shape[1]),
                  _row_spec(tm, D_MODEL), _row_spec(tm, D_MODEL), _full_spec((1, D_MODEL)),
                  _full_spec(w_ag.shape), _full_spec(w_qkv.shape), _full_spec(w_f.shape)],
        out_specs=[_row_spec(tm, D_MODEL), _full_spec((1, D_MODEL))],
        out_shape=[jax.ShapeDtypeStruct((t, D_MODEL), F32), jax.ShapeDtypeStruct((1, D_MODEL), F32)],
        compiler_params=_params(dimension_semantics=("arbitrary",)),
    )(dproj, dx2, x1, g, w_ag, w_qkv, w_f)


def _split3(x):
    hi = x.astype(BF16)
    r1 = x - hi.astype(F32)
    mid = r1.astype(BF16)
    lo = (r1 - mid.astype(F32)).astype(BF16)
    return hi, mid, lo


def _gates_fwd(flt, fb):
    t = flt.shape[1]

    def body(f_ref, b_ref, d_ref):
        z = f_ref[...] + b_ref[...]
        logf = jnp.minimum(z, 0.0) - jnp.log(1.0 + jnp.exp(-jnp.abs(z)))
        row = lax.broadcasted_iota(jnp.int32, (LANES, LANES), 0)
        col = lax.broadcasted_iota(jnp.int32, (LANES, LANES), 1)
        upper = (row <= col).astype(BF16)
        carry = jnp.zeros((HEAD_ROWS, 1), F32)
        for blk in range(t // LANES):
            hi, mid, lo = _split3(logf[:, blk * LANES:(blk + 1) * LANES])
            cs = _dot(hi, upper) + _dot(mid, upper) + _dot(lo, upper)
            d_ref[:, blk * LANES:(blk + 1) * LANES] = cs + carry
            carry = carry + cs[:, LANES - 1:LANES]

    return pl.pallas_call(
        body, name="gates_fwd", out_shape=jax.ShapeDtypeStruct((HEAD_ROWS, t), F32),
        compiler_params=_params(),
    )(flt, fb)


def _gates_bwd(dd, flt, fb):
    t = flt.shape[1]

    def body(dd_ref, f_ref, b_ref, df_ref, db_ref):
        z = f_ref[...] + b_ref[...]
        row = lax.broadcasted_iota(jnp.int32, (LANES, LANES), 0)
        col = lax.broadcasted_iota(jnp.int32, (LANES, LANES), 1)
        lower = (row >= col).astype(BF16)
        carry = jnp.zeros((HEAD_ROWS, 1), F32)
        db = jnp.zeros((HEAD_ROWS, 1), F32)
        for blk in reversed(range(t // LANES)):
            sl = slice(blk * LANES, (blk + 1) * LANES)
            hi, mid, lo = _split3(dd_ref[:, sl])
            cs = _dot(hi, lower) + _dot(mid, lower) + _dot(lo, lower)
            dz = (cs + carry) * _sigmoid(-z[:, sl])
            df_ref[:, sl] = dz
            db = db + jnp.sum(dz, axis=1, keepdims=True)
            carry = carry + cs[:, 0:1]
        db_ref[...] = db

    return pl.pallas_call(
        body, name="gates_bwd",
        out_shape=[jax.ShapeDtypeStruct((HEAD_ROWS, t), F32), jax.ShapeDtypeStruct((HEAD_ROWS, 1), F32)],
        compiler_params=_params(),
    )(dd, flt, fb)


CONV_CHUNK = 128
CONV_TAIL = 16
CONV_WINDOW = CONV_CHUNK + CONV_PAD + 8
CONV_ROWS_EXTRA = CONV_PAD + CONV_TAIL
SUBLANES = 8


def _conv_rows(ag_ref, u_ref, t):
    u_ref[0:CONV_PAD, :] = jnp.zeros((CONV_PAD, D_CONV), F32)
    u_ref[CONV_PAD + t:CONV_ROWS_EXTRA + t, :] = jnp.zeros((CONV_TAIL, D_CONV), F32)

    def fill(i, c):
        r0 = pl.multiple_of(i * CONV_CHUNK, CONV_CHUNK)
        a = ag_ref[pl.ds(r0, CONV_CHUNK), 0:D_CONV]
        gt = ag_ref[pl.ds(r0, CONV_CHUNK), D_CONV:2 * D_CONV]
        u_ref[pl.ds(CONV_PAD + r0, CONV_CHUNK), :] = a * _sigmoid(gt)
        return c

    lax.fori_loop(0, t // CONV_CHUNK, fill, 0)


def _for_shifted(ref, r0, offsets, fn):
    window = ref[pl.ds(r0, CONV_WINDOW), :]
    for rem in range(SUBLANES):
        mine = [o for o in offsets if o % SUBLANES == rem]
        if not mine:
            continue
        turned = window if rem == 0 else pltpu.roll(window, CONV_WINDOW - rem, 0)
        for o in mine:
            fn(o, turned[o - rem:o - rem + CONV_CHUNK])


def _conv_taps(u_ref, r0, w_ref, cb):
    acc = [jnp.zeros((CONV_CHUNK, D_CONV), F32)]

    def tap(o, rows):
        j = o - (CONV_PAD - CONV_WIDTH + 1)
        acc[0] = acc[0] + w_ref[j:j + 1, :] * rows

    _for_shifted(u_ref, r0, [j + CONV_PAD - CONV_WIDTH + 1 for j in range(CONV_WIDTH)], tap)
    return acc[0] + cb


def _conv_point(y, lg, lb):
    mu = jnp.mean(y, axis=-1, keepdims=True)
    yc = y - mu
    rstd = lax.rsqrt(jnp.mean(yc * yc, axis=-1, keepdims=True) + EPS)
    yhat = yc * rstd
    z = yhat * lg + lb
    sg = _sigmoid(z)
    s = z * sg
    rr = _rms_stats(s)
    return yhat, rstd, z, sg, s, rr


def _conv_fwd(ag, conv_w, conv_b, ln_g, ln_b, norm_g):
    t = ag.shape[0]

    def body(ag_ref, w_ref, cb_ref, lg_ref, lb_ref, ng_ref, o_ref, y_ref, u_ref):
        _conv_rows(ag_ref, u_ref, t)
        cb, lg, lb, ng = cb_ref[...], lg_ref[...], lb_ref[...], ng_ref[...]

        def chunk(i, c):
            r0 = pl.multiple_of(i * CONV_CHUNK, CONV_CHUNK)
            y = _conv_taps(u_ref, r0, w_ref, cb)
            y_ref[pl.ds(r0, CONV_CHUNK), :] = y
            _, _, _, _, s, rr = _conv_point(y, lg, lb)
            o_ref[pl.ds(r0, CONV_CHUNK), :] = (s * rr * ng).astype(BF16)
            return c

        lax.fori_loop(0, t // CONV_CHUNK, chunk, 0)

    return pl.pallas_call(
        body, name="conv_fwd",
        out_shape=[jax.ShapeDtypeStruct((t, D_CONV), BF16), jax.ShapeDtypeStruct((t, D_CONV), F32)],
        scratch_shapes=[pltpu.VMEM((t + CONV_ROWS_EXTRA, D_CONV), F32)],
        compiler_params=_params(),
    )(ag, conv_w, conv_b, ln_g, ln_b, norm_g)


def _conv_bwd(ag, y, dout, conv_w, ln_g, ln_b, norm_g):
    t = ag.shape[0]

    def body(ag_ref, y_ref, do_ref, w_ref, lg_ref, lb_ref, ng_ref,
             dag_ref, dw_ref, dcb_ref, dlg_ref, dlb_ref, dng_ref, u_ref, dy_ref):
        _conv_rows(ag_ref, u_ref, t)
        dy_ref[t:t + CONV_ROWS_EXTRA, :] = jnp.zeros((CONV_ROWS_EXTRA, D_CONV), F32)
        lg, lb, ng = lg_ref[...], lb_ref[...], ng_ref[...]
        dw_ref[...] = jnp.zeros_like(dw_ref)
        zero = jnp.zeros((1, D_CONV), F32)

        def chunk(i, carry):
            dcb, dlg, dlb, dng = carry
            r0 = pl.multiple_of(i * CONV_CHUNK, CONV_CHUNK)
            yhat, rstd, z, sg, s, rr = _conv_point(y_ref[pl.ds(r0, CONV_CHUNK), :], lg, lb)
            do = do_ref[pl.ds(r0, CONV_CHUNK), :]
            ds, dng_rows = _rms_bwd(s, rr, ng, do)
            dz = ds * _silu_grad(z, sg)
            dyhat = dz * lg
            dy = rstd * (dyhat - jnp.mean(dyhat, axis=-1, keepdims=True)
                         - yhat * jnp.mean(dyhat * yhat, axis=-1, keepdims=True))
            dy_ref[pl.ds(r0, CONV_CHUNK), :] = dy
            def tap(o, rows):
                j = o - (CONV_PAD - CONV_WIDTH + 1)
                dw_ref[j:j + 1, :] += jnp.sum(dy * rows, axis=0, keepdims=True)

            _for_shifted(u_ref, r0, [j + CONV_PAD - CONV_WIDTH + 1 for j in range(CONV_WIDTH)], tap)
            return (dcb + jnp.sum(dy, axis=0, keepdims=True), dlg + jnp.sum(dz * yhat, axis=0, keepdims=True),
                    dlb + jnp.sum(dz, axis=0, keepdims=True), dng + jnp.sum(dng_rows, axis=0, keepdims=True))

        dcb, dlg, dlb, dng = lax.fori_loop(0, t // CONV_CHUNK, chunk, (zero, zero, zero, zero))
        dcb_ref[...] = dcb
        dlg_ref[...] = dlg
        dlb_ref[...] = dlb
        dng_ref[...] = dng

        def chunk2(i, c):
            r0 = pl.multiple_of(i * CONV_CHUNK, CONV_CHUNK)
            acc = [jnp.zeros((CONV_CHUNK, D_CONV), F32)]

            def tap(o, rows):
                j = CONV_WIDTH - 1 - o
                acc[0] = acc[0] + w_ref[j:j + 1, :] * rows

            _for_shifted(dy_ref, r0, list(range(CONV_WIDTH)), tap)
            du = acc[0]
            a = ag_ref[pl.ds(r0, CONV_CHUNK), 0:D_CONV]
            gt = ag_ref[pl.ds(r0, CONV_CHUNK), D_CONV:2 * D_CONV]
            sg = _sigmoid(gt)
            dag_ref[pl.ds(r0, CONV_CHUNK), 0:D_CONV] = (du * sg).astype(BF16)
            dag_ref[pl.ds(r0, CONV_CHUNK), D_CONV:2 * D_CONV] = (du * a * sg * (1.0 - sg)).astype(BF16)
            return c

        lax.fori_loop(0, t // CONV_CHUNK, chunk2, 0)

    vec = jax.ShapeDtypeStruct((1, D_CONV), F32)
    return pl.pallas_call(
        body, name="conv_bwd",
        out_shape=[jax.ShapeDtypeStruct((t, 2 * D_CONV), BF16), jax.ShapeDtypeStruct((CONV_PAD, D_CONV), F32),
                   vec, vec, vec, vec],
        scratch_shapes=[pltpu.VMEM((t + CONV_ROWS_EXTRA, D_CONV), F32), pltpu.VMEM((t + CONV_ROWS_EXTRA, D_CONV), F32)],
        compiler_params=_params(),
    )(ag, y, dout, conv_w, ln_g, ln_b, norm_g)


Q_ROWS = 256
ATTN_SCALE = HEAD_DIM ** -0.5
ATTN_AHEAD = 1


def _attn_specs(t):
    blk = lambda off: pl.BlockSpec((t, LANES), lambda p: (0, off + p))
    pairs = N_HEADS // 2
    return [blk(0), blk(pairs), blk(2 * pairs), pl.BlockSpec((2, 1, t), lambda p: (p, 0, 0))]


def _one_head(q2, mask):
    return jnp.where(mask, q2, jnp.zeros_like(q2)) * ATTN_SCALE


def _attn_scores(qs, k2, drow, r0, q1):
    s = _dot_nt(qs, k2) - drow
    rowi = lax.broadcasted_iota(jnp.int32, (q1 - r0, q1 - r0), 0)
    coli = lax.broadcasted_iota(jnp.int32, (q1 - r0, q1 - r0), 1)
    diag = jnp.where(coli <= rowi, s[:, r0:q1], -jnp.inf)
    return diag if r0 == 0 else jnp.concatenate([s[:, :r0], diag], axis=1)


def _attn_fwd(qkv, drow, deps=()):
    t = qkv.shape[0]
    deps = tuple(deps)

    def body(q_ref, k_ref, v_ref, dr_ref, o_ref, lse_ref):
        head_a = lax.broadcasted_iota(jnp.int32, (1, LANES), 1) < HEAD_DIM
        items = [(qb, hh) for qb in range(t // Q_ROWS) for hh in range(2)]

        def scores(item):
            qb, hh = item
            r0, q1 = qb * Q_ROWS, (qb + 1) * Q_ROWS
            qs = _one_head(q_ref[r0:q1, :], head_a if hh == 0 else ~head_a)
            return _attn_scores(qs, k_ref[0:q1, :], dr_ref[hh, :, 0:q1], r0, q1)

        ahead = [scores(item) for item in items[:ATTN_AHEAD]]
        outs = []
        for n, (qb, hh) in enumerate(items):
            r0, q1 = qb * Q_ROWS, (qb + 1) * Q_ROWS
            s = ahead.pop(0)
            if n + ATTN_AHEAD < len(items):
                ahead.append(scores(items[n + ATTN_AHEAD]))
            mx = jnp.max(s, axis=1, keepdims=True)
            p = jnp.exp(s - mx)
            l = jnp.sum(p, axis=1, keepdims=True)
            lse_ref[hh, r0:q1, :] = mx + jnp.log(l)
            outs.append(_dot(p.astype(BF16), v_ref[0:q1, :]) * (1.0 / l))
            if hh == 1:
                o_ref[r0:q1, :] = jnp.where(head_a, outs[0], outs[1])
                outs = []

    pairs = N_HEADS // 2
    return pl.pallas_call(
        _skip(len(deps), body), name="attn_fwd", grid=(pairs,), in_specs=[_ANY] * len(deps) + _attn_specs(t),
        out_specs=[pl.BlockSpec((t, LANES), lambda p: (0, p)), pl.BlockSpec((2, t, 1), lambda p: (p, 0, 0))],
        out_shape=[jax.ShapeDtypeStruct((t, D_ATTN), F32), jax.ShapeDtypeStruct((N_HEADS, t, 1), F32)],
        compiler_params=_params(dimension_semantics=("arbitrary",)),
    )(*deps, qkv, qkv, qkv, drow)


def _attn_bwd(qkv, drow, lse, do):
    t = qkv.shape[0]

    def body(q_ref, k_ref, v_ref, dr_ref, lse_ref, do_ref,
             dq_ref, dk_ref, dv_ref, dd_ref, dk_acc, dv_acc):
        head_a = lax.broadcasted_iota(jnp.int32, (1, LANES), 1) < HEAD_DIM
        dk_acc[...] = jnp.zeros_like(dk_acc)
        dv_acc[...] = jnp.zeros_like(dv_acc)
        dd_ref[...] = jnp.zeros_like(dd_ref)
        items = [(qb, hh) for qb in range(t // Q_ROWS) for hh in range(2)]

        def products(item):
            qb, hh = item
            r0, q1 = qb * Q_ROWS, (qb + 1) * Q_ROWS
            mask = head_a if hh == 0 else ~head_a
            qs = _one_head(q_ref[r0:q1, :], mask)
            dob = jnp.where(mask, do_ref[r0:q1, :], 0.0).astype(BF16)
            s = _attn_scores(qs, k_ref[0:q1, :], dr_ref[hh, :, 0:q1], r0, q1)
            return qs, dob, s, _dot_nt(dob, v_ref[0:q1, :])

        ahead = products(items[0])
        dqs = []
        for n, (qb, hh) in enumerate(items):
            r0, q1 = qb * Q_ROWS, (qb + 1) * Q_ROWS
            qs, dob, s, dp = ahead
            if n + 1 < len(items):
                ahead = products(items[n + 1])
            p = jnp.exp(s - lse_ref[hh, r0:q1, :])
            ds = p * (dp - jnp.sum(p * dp, axis=1, keepdims=True))
            dsb = ds.astype(BF16)
            dqs.append(_dot(dsb, k_ref[0:q1, :]) * ATTN_SCALE)
            dk_acc[0:q1, :] += _dot_tn(dsb, qs)
            dv_acc[0:q1, :] += _dot_tn(p.astype(BF16), dob)
            dd_ref[hh, :, 0:q1] -= jnp.sum(ds, axis=0, keepdims=True)
            if hh == 1:
                dq_ref[r0:q1, :] = jnp.where(head_a, dqs[0], dqs[1]).astype(BF16)
                dqs = []
        dk_ref[...] = dk_acc[...].astype(BF16)
        dv_ref[...] = dv_acc[...].astype(BF16)

    pairs = N_HEADS // 2
    col = pl.BlockSpec((t, LANES), lambda p: (0, p))
    grad = jax.ShapeDtypeStruct((t, D_ATTN), BF16)
    return pl.pallas_call(
        body, name="attn_bwd", grid=(pairs,),
        in_specs=_attn_specs(t) + [pl.BlockSpec((2, t, 1), lambda p: (p, 0, 0)), col],
        out_specs=[col, col, col, pl.BlockSpec((2, 1, t), lambda p: (p, 0, 0))],
        out_shape=[grad, grad, grad, jax.ShapeDtypeStruct((N_HEADS, 1, t), F32)],
        scratch_shapes=[pltpu.VMEM((t, LANES), F32), pltpu.VMEM((t, LANES), F32)],
        compiler_params=_params(dimension_semantics=("arbitrary",)),
    )(qkv, qkv, qkv, drow, lse, do)


def _out_proj(ycn, o, g_attn, w_out, x1, deps=()):
    t = x1.shape[0]
    tm = TOKEN_ROWS
    deps = tuple(deps)

    def body(yc_ref, o_ref, g_ref, w_ref, x_ref, xo_ref, ya_ref):
        ov = o_ref[...]
        ya = (ov * _rms_stats(ov) * g_ref[...]).astype(BF16)
        ya_ref[...] = ya
        xo_ref[...] = x_ref[...] + _dot(yc_ref[...], w_ref[0:D_CONV, :]) + _dot(ya, w_ref[D_CONV:, :])

    return pl.pallas_call(
        _skip(len(deps), body), name="out_proj", grid=(t // tm,),
        in_specs=[_ANY] * len(deps) + [_row_spec(tm, D_CONV), _row_spec(tm, D_ATTN), _full_spec((1, D_ATTN)),
                                       _full_spec(w_out.shape), _row_spec(tm, D_MODEL)],
        out_specs=[_row_spec(tm, D_MODEL), _row_spec(tm, D_ATTN)],
        out_shape=[jax.ShapeDtypeStruct((t, D_MODEL), F32), jax.ShapeDtypeStruct((t, D_ATTN), BF16)],
        compiler_params=_params(dimension_semantics=("arbitrary",)),
    )(*deps, ycn, o, g_attn, w_out, x1)


def _out_proj_bwd(dx2, o, g_attn, w_out, deps=()):
    t = dx2.shape[0]
    tm = TOKEN_ROWS
    deps = tuple(deps)

    def body(dx_ref, o_ref, g_ref, w_ref, dyc_ref, do_ref, dg_ref):
        @pl.when(pl.program_id(0) == 0)
        def _():
            dg_ref[...] = jnp.zeros_like(dg_ref)

        dxb = dx_ref[...]
        dyc_ref[...] = _dot_nt(dxb, w_ref[0:D_CONV, :])
        dya = _dot_nt(dxb, w_ref[D_CONV:, :])
        ov = o_ref[...]
        do, dg_rows = _rms_bwd(ov, _rms_stats(ov), g_ref[...], dya)
        do_ref[...] = do
        dg_ref[...] += jnp.sum(dg_rows, axis=0, keepdims=True)

    return pl.pallas_call(
        _skip(len(deps), body), name="out_proj_bwd", grid=(t // tm,),
        in_specs=[_ANY] * len(deps) + [_row_spec(tm, D_MODEL), _row_spec(tm, D_ATTN), _full_spec((1, D_ATTN)),
                                       _full_spec(w_out.shape)],
        out_specs=[_row_spec(tm, D_CONV), _row_spec(tm, D_ATTN), _full_spec((1, D_ATTN))],
        out_shape=[jax.ShapeDtypeStruct((t, D_CONV), F32), jax.ShapeDtypeStruct((t, D_ATTN), F32),
                   jax.ShapeDtypeStruct((1, D_ATTN), F32)],
        compiler_params=_params(dimension_semantics=("arbitrary",)),
    )(*deps, dx2, o, g_attn, w_out)


def _loss_bwd(x3, target, g):
    t = x3.shape[0]
    tm = TOKEN_ROWS

    def body(x_ref, t_ref, g_ref, loss_ref, dx_ref, dg_ref):
        @pl.when(pl.program_id(0) == 0)
        def _():
            loss_ref[...] = jnp.zeros_like(loss_ref)
            dg_ref[...] = jnp.zeros_like(dg_ref)

        xv = x_ref[...]
        r = _rms_stats(xv)
        gv = g_ref[...]
        err = xv * r * gv - t_ref[...]
        row = jnp.sum(err * err, axis=1, keepdims=True) * (0.5 / D_MODEL)
        loss_ref[...] += jnp.sum(row, axis=0, keepdims=True)
        dx, dg_rows = _rms_bwd(xv, r, gv, err * (1.0 / D_MODEL))
        dx_ref[...] = dx
        dg_ref[...] += jnp.sum(dg_rows, axis=0, keepdims=True)

    return pl.pallas_call(
        body, name="loss_bwd", grid=(t // tm,),
        in_specs=[_row_spec(tm, D_MODEL), _row_spec(tm, D_MODEL), _full_spec((1, D_MODEL))],
        out_specs=[_full_spec((1, LANES)), _row_spec(tm, D_MODEL), _full_spec((1, D_MODEL))],
        out_shape=[jax.ShapeDtypeStruct((1, LANES), F32), jax.ShapeDtypeStruct((t, D_MODEL), F32),
                   jax.ShapeDtypeStruct((1, D_MODEL), F32)],
        compiler_params=_params(dimension_semantics=("arbitrary",)),
    )(x3, target, g)


def _split_w_in(w_in_t):
    w_ag = w_in_t[:2 * D_CONV]
    w_qkv = w_in_t[2 * D_CONV:2 * D_CONV + 3 * D_ATTN]
    w_f = jnp.pad(w_in_t[2 * D_CONV + 3 * D_ATTN:], ((0, LANES - N_HEADS), (0, 0)))
    return w_ag, w_qkv, w_f


def _head_rows(v):
    return jnp.pad(v, ((0, HEAD_ROWS - N_HEADS),) + ((0, 0),) * (v.ndim - 1))


def _local_step(x, target, p, get_weights, put_grads, flush_grads):
    t = x.shape[0]
    fb = _head_rows(p["forget_b"].reshape(N_HEADS, 1))

    w, deps = get_weights("ffn1_w13", None)
    h1, gu1, act1 = _ffn_up(x, p["ffn1_norm"], w["ffn1_w13"], "ffn1_up", deps)
    w2, _ = get_weights("ffn1_w2", act1)
    w.update(w2)
    x1 = _ffn_down(x, act1, w["ffn1_w2"], "ffn1_down")
    wm, _ = get_weights("mix", x1)
    w.update(wm)
    w_ag, w_qkv, w_f = _split_w_in(w["w_in"])
    conv_w = jnp.pad(w["conv_w"], ((0, CONV_PAD - CONV_WIDTH), (0, 0)))
    h2, ag, qkv, fl = _mix_proj(x1, p["mix_norm"], w_ag, w_qkv, w_f)
    flt = _head_rows(fl[:, :N_HEADS].T)
    dcum = _gates_fwd(flt, fb)[:N_HEADS]
    drow = dcum.reshape(N_HEADS, 1, t)
    ycn, y_conv = _conv_fwd(ag, conv_w, p["conv_b"], p["conv_ln_g"], p["conv_ln_b"], p["out_norm_conv"])
    o, lse = _attn_fwd(qkv, drow, [ycn])
    _, deps = get_weights("ffn2:landed", o)
    x2, yan = _out_proj(ycn, o, p["out_norm_attn"], w["w_out"], x1, deps)
    w2, _ = get_weights("ffn2", x2)
    w.update(w2)
    x3, h3, gu2, act2 = _ffn_fwd(x2, p["ffn2_norm"], w["ffn2_w13"], w["ffn2_w2"], "ffn2_fwd")
    loss, dx3, d_final = _loss_bwd(x3, target, p["final_norm"])

    g = {}
    dx2, dgu2, g["ffn2_norm"], dx3_half, dx2_bf16 = _ffn_bwd(
        dx3, x2, gu2, p["ffn2_norm"], w["ffn2_w13"], w["ffn2_w2"], "ffn2_bwd")
    dw13 = _wgrad(h3, dgu2, N_CHIPS, "ffn2_dw13")
    dw2 = _wgrad(act2, dx3_half, 1, "ffn2_dw2").reshape(D_FF, D_MODEL)
    deps = put_grads("ffn2", {"ffn2_w13": dw13, "ffn2_w2": dw2})
    dyc, do, g["out_norm_attn"] = _out_proj_bwd(dx2_bf16, o, p["out_norm_attn"], w["w_out"], deps)
    deps = flush_grads("ffn2", [dyc])
    dw_out = _wgrad(jnp.concatenate([ycn, yan], axis=1), dx2_bf16, 1, "dw_out", deps).reshape(D_MODEL, D_MODEL)
    dq, dk, dv, ddrow = _attn_bwd(qkv, drow, lse, do)
    dflt, dfb = _gates_bwd(_head_rows(ddrow.reshape(N_HEADS, t)), flt, fb)
    g["forget_b"] = dfb[:N_HEADS, 0].reshape(1, N_HEADS)
    dfl = jnp.pad(dflt[:N_HEADS].T, ((0, 0), (0, LANES - N_HEADS)))
    dag, dconv_w, g["conv_b"], g["conv_ln_g"], g["conv_ln_b"], g["out_norm_conv"] = _conv_bwd(
        ag, y_conv, dyc, conv_w, p["conv_ln_g"], p["conv_ln_b"], p["out_norm_conv"])
    g["conv_w"] = dconv_w[:CONV_WIDTH]
    dproj = jnp.concatenate([dag, dq, dk, dv, dfl.astype(BF16)], axis=1)
    dx1, g["mix_norm"] = _mix_proj_bwd(dproj, dx2, x1, p["mix_norm"], w_ag, w_qkv, w_f)
    dw_in = _wgrad(dproj, h2, 1, "dw_in").reshape(dproj.shape[1], D_MODEL)[:N_IN]
    deps = put_grads("mix", {"w_in": dw_in, "w_out": dw_out})
    dx0, dgu1, g["ffn1_norm"], dx1_half, _ = _ffn_bwd(
        dx1, x, gu1, p["ffn1_norm"], w["ffn1_w13"], w["ffn1_w2"], "ffn1_bwd", deps)
    g["final_norm"] = d_final
    g["loss"] = loss[:, :1]
    deps = flush_grads("mix", put_grads("small", g))
    dw2 = _wgrad(act1, dx1_half, 1, "ffn1_dw2", deps).reshape(D_FF, D_MODEL)
    deps = flush_grads("ffn1_w2", put_grads("ffn1_w2", {"ffn1_w2": dw2}))
    dw13 = _wgrad(h1, dgu1, N_CHIPS, "ffn1_dw13", deps)
    put_grads("ffn1_w13", {"ffn1_w13": dw13})
    return dx0


MESH = pl.DeviceIdType.MESH


def _place():
    x, y, c = lax.axis_index("x"), lax.axis_index("y"), lax.axis_index("c")
    chips = [(1 - x, y), (x, 1 - y), (1 - x, 1 - y)]
    return x, y, c, chips


def _hbm_out(shape, dtype):
    return jax.ShapeDtypeStruct(shape, dtype)


def _comm_call(body, name, ins, out_shapes, n_remote, in_place=False):
    return pl.pallas_call(
        body, name=name, in_specs=[_ANY] * len(ins), out_specs=[_ANY] * len(out_shapes), out_shape=out_shapes,
        scratch_shapes=[pltpu.SemaphoreType.DMA((n_remote,)), pltpu.SemaphoreType.DMA((n_remote,))],
        input_output_aliases={i: i for i in range(len(ins))} if in_place else {},
    )(*ins)


def _remote(src, dst, sems, n, to):
    send_sems, recv_sems = sems
    return pltpu.make_async_remote_copy(src_ref=src, dst_ref=dst, send_sem=send_sems.at[n], recv_sem=recv_sems.at[n],
                                        device_id=to, device_id_type=MESH)


HALF_ROWS_MULTIPLE = 32


def _halved_by_rows(rows):
    return rows % HALF_ROWS_MULTIPLE == 0


def _half_shape(rows, cols):
    return (rows // 2, cols) if _halved_by_rows(rows) else (rows, cols // 2)


def _half_index(rows, core):
    return (core, 0) if _halved_by_rows(rows) else (0, core)


def _half_of(ref, rows, cols, core, *lead):
    if _halved_by_rows(rows):
        return ref.at[(*lead, pl.ds(core * (rows // 2), rows // 2), slice(None))]
    return ref.at[(*lead, slice(None), pl.ds(core * (cols // 2), cols // 2))]


def _into_slot(shard, chip, dtype, name, deps=()):
    rows, cols = shard.shape
    half = _half_shape(rows, cols)
    by_rows = _halved_by_rows(rows)
    deps = tuple(deps)

    def body(k_ref, *refs):
        s_ref, o_ref = refs[len(deps):]
        o_ref[0] = s_ref[...].astype(dtype)

    return pl.pallas_call(
        body, name=name,
        grid_spec=pltpu.PrefetchScalarGridSpec(
            num_scalar_prefetch=1, grid=(2,),
            in_specs=[_ANY] * len(deps) + [pl.BlockSpec(half, lambda i, k_ref: (i, 0) if by_rows else (0, i))],
            out_specs=pl.BlockSpec((1,) + half, lambda i, k_ref: (k_ref[0], i, 0) if by_rows else (k_ref[0], 0, i))),
        out_shape=jax.ShapeDtypeStruct((N_CHIPS, rows, cols), dtype),
        compiler_params=_params(dimension_semantics=("arbitrary",)),
    )(chip, *deps, shard)


def _gather_shards(slots, name, ici=True, passed=()):
    n = len(slots)
    slots = list(slots) + list(passed)
    total = len(slots)

    def body(*refs):
        outs = refs[total:total + n]
        sems = refs[2 * total:2 * total + 2]
        x, y, c, chips = _place()
        me = 2 * x + y
        sibling = (x, y, 1 - c)

        def half(i, chip_index, core):
            return _half_of(outs[i], *slots[i].shape[1:], core, chip_index)

        sends = []
        if ici:
            for i in range(n):
                for j, chip in enumerate(chips):
                    cp = _remote(half(i, me, c), half(i, me, c), sems, 6 * i + j, (*chip, c))
                    cp.start()
                    sends.append(cp)
        for i in range(n):
            for j, chip in enumerate(chips):
                src_chip = 2 * chip[0] + chip[1]
                landed = half(i, src_chip, c)
                if ici:
                    _remote(landed, landed, sems, 6 * i + j, (*chip, c)).wait_recv()
                cp = _remote(landed, landed, sems, 6 * i + 3 + j, sibling)
                cp.start()
                sends.append(cp)
        for i in range(n):
            for j, chip in enumerate(chips):
                src_chip = 2 * chip[0] + chip[1]
                landed = half(i, src_chip, 1 - c)
                _remote(landed, landed, sems, 6 * i + 3 + j, sibling).wait_recv()
        for cp in sends:
            cp.wait_send()

    outs = [_hbm_out(s.shape, s.dtype) for s in slots]
    return _comm_call(body, name, slots, outs, 6 * n, in_place=True)


_HBM = pl.BlockSpec(memory_space=pltpu.HBM)
_SEM = pl.BlockSpec(memory_space=pltpu.SEMAPHORE)
_DATAFLOW = pltpu.SideEffectType.DATAFLOW_SIDE_EFFECTING


def _split_copy_start(name, bufs, n_copies, plan):
    n = len(bufs)

    def body(*refs):
        for send, _ in plan(refs[:n], (refs[n], refs[n + 1])):
            send.start()
        token = refs[-1]
        token[...] = jnp.zeros_like(token)

    out = pl.pallas_call(
        body, name=name,
        out_shape=(pltpu.SemaphoreType.DMA((n_copies,)), pltpu.SemaphoreType.DMA((n_copies,)),
                   *[pltpu.HBM(b.shape, b.dtype) for b in bufs], jax.ShapeDtypeStruct((8, LANES), F32)),
        in_specs=[_HBM] * n, out_specs=(_SEM, _SEM, *[_HBM] * n, pl.BlockSpec(memory_space=pltpu.VMEM)),
        input_output_aliases={i: 2 + i for i in range(n)},
        compiler_params=pltpu.CompilerParams(has_side_effects=_DATAFLOW),
    )(*[pltpu.with_memory_space_constraint(b, pltpu.HBM) for b in bufs])
    return out[0], out[1], list(out[2:2 + n]), out[-1]


def _split_copy_wait(name, started, plan, after, passed=()):
    send_sems, recv_sems, bufs, _ = started
    n = len(bufs)
    after = tuple(after)
    bufs = list(bufs) + list(passed)
    total = len(bufs)

    def body(*refs):
        for send, recv in plan(refs[:n], (refs[total], refs[total + 1])):
            send.wait_send()
            recv.wait_recv()

    out = pl.pallas_call(
        body, name=name, out_shape=tuple(pltpu.HBM(b.shape, b.dtype) for b in bufs),
        in_specs=[_HBM] * total + [_SEM, _SEM] + [_ANY] * len(after), out_specs=tuple([_HBM] * total),
        input_output_aliases={i: i for i in range(total)},
        compiler_params=pltpu.CompilerParams(has_side_effects=_DATAFLOW),
    )(*bufs, send_sems, recv_sems, *after)
    return list(out)


def _ici_gather_plan(slots):
    def plan(refs, sems):
        x, y, c, chips = _place()
        me = 2 * x + y
        copies = []
        for i, ref in enumerate(refs):
            for j, chip in enumerate(chips):
                mine = _half_of(ref, *slots[i].shape[1:], c, me)
                theirs = _half_of(ref, *slots[i].shape[1:], c, 2 * chip[0] + chip[1])
                to = (*chip, c)
                copies.append((_remote(mine, mine, sems, 3 * i + j, to), _remote(theirs, theirs, sems, 3 * i + j, to)))
        return copies

    return plan


def _ici_scatter_plan(n):
    def plan(refs, sems):
        x, y, c, chips = _place()
        copies = []
        for i in range(n):
            for j, chip in enumerate(chips):
                cp = _remote(refs[i].at[2 * chip[0] + chip[1]], refs[n + i].at[j], sems, 3 * i + j, (*chip, c))
                copies.append((cp, cp))
        return copies

    return plan


def _d2d_forward_plan(slots):
    def plan(refs, sems):
        x, y, c, chips = _place()
        sibling = (x, y, 1 - c)
        copies = []
        for i, ref in enumerate(refs):
            for j, chip in enumerate(chips):
                src_chip = 2 * chip[0] + chip[1]
                mine = _half_of(ref, *slots[i].shape[1:], c, src_chip)
                theirs = _half_of(ref, *slots[i].shape[1:], 1 - c, src_chip)
                copies.append((_remote(mine, mine, sems, 3 * i + j, sibling),
                               _remote(theirs, theirs, sems, 3 * i + j, sibling)))
        return copies

    return plan


def _pair_exchange_plan(grads):
    n = len(grads)

    def plan(refs, sems):
        x, y, c, _ = _place()
        copies = []
        for i in range(n):
            theirs = _half_of(refs[i], *grads[i].shape[1:], 1 - c, slice(None))
            cp = _remote(theirs, refs[n + i], sems, i, (x, y, 1 - c))
            copies.append((cp, cp))
        return copies

    return plan


def _pair_share_plan(shapes):
    def plan(refs, sems):
        x, y, c, _ = _place()
        sibling = (x, y, 1 - c)
        copies = []
        for i, ref in enumerate(refs):
            mine, theirs = _half_of(ref, *shapes[i], c), _half_of(ref, *shapes[i], 1 - c)
            copies.append((_remote(mine, mine, sems, i, sibling), _remote(theirs, theirs, sems, i, sibling)))
        return copies

    return plan


def _pair_share(halves, name):
    n = len(halves)
    plan = _pair_share_plan([h.shape for h in halves])

    def body(*refs):
        copies = plan(refs[n:2 * n], refs[2 * n:2 * n + 2])
        for send, _ in copies:
            send.start()
        for send, recv in copies:
            send.wait_send()
            recv.wait_recv()

    outs = [_hbm_out(h.shape, h.dtype) for h in halves]
    return _comm_call(body, name, halves, outs, n, in_place=True)


N_DEVICES = 8
FLIPS = [(fx, fy, fc) for fx in range(2) for fy in range(2) for fc in range(2)][1:]


def _small_slots(v, me):
    rows = v.shape[0]

    def body(k_ref, v_ref, o_ref):
        o_ref[0] = v_ref[...]

    return pl.pallas_call(
        body, name="small_slot",
        grid_spec=pltpu.PrefetchScalarGridSpec(
            num_scalar_prefetch=1, grid=(1,),
            in_specs=[pl.BlockSpec((rows, LANES), lambda i, k_ref: (0, 0))],
            out_specs=pl.BlockSpec((1, rows, LANES), lambda i, k_ref: (k_ref[0], 0, 0))),
        out_shape=jax.ShapeDtypeStruct((N_DEVICES, rows, LANES), F32),
        compiler_params=_params(dimension_semantics=("arbitrary",)),
    )(me, v)


def _small_plan():
    def plan(refs, sems):
        x, y, c, _ = _place()
        slots = refs[0]
        me = 4 * x + 2 * y + c
        copies = []
        for n, (fx, fy, fc) in enumerate(FLIPS):
            to = (x ^ fx, y ^ fy, c ^ fc)
            src = 4 * to[0] + 2 * to[1] + to[2]
            copies.append((_remote(slots.at[me], slots.at[me], sems, n, to), _remote(slots.at[src], slots.at[src], sems, n, to)))
        return copies

    return plan


def _small_sum(slots):
    def body(s_ref, o_ref):
        acc = s_ref[0]
        for s in range(1, N_DEVICES):
            acc = acc + s_ref[s]
        o_ref[...] = acc

    return pl.pallas_call(body, name="small_sum", out_shape=jax.ShapeDtypeStruct(slots.shape[1:], F32),
                          compiler_params=_params())(slots)


def _pair_add(gs, sibs, core, name):
    n = len(gs)
    halves = [_half_shape(*g.shape[1:]) for g in gs]

    def body(c_ref, *refs):
        for g_ref, s_ref, o_ref in zip(refs[:n], refs[n:2 * n], refs[2 * n:]):
            o_ref[0] = (g_ref[0].astype(F32) + s_ref[0].astype(F32)).astype(BF16)

    def mine(g, half):
        return pl.BlockSpec((1,) + half, lambda s, c_ref: (s, *_half_index(g.shape[1], c_ref[0])))

    whole = [pl.BlockSpec((1,) + half, lambda s, c_ref: (s, 0, 0)) for half in halves]
    return pl.pallas_call(
        body, name=name,
        grid_spec=pltpu.PrefetchScalarGridSpec(
            num_scalar_prefetch=1, grid=(N_CHIPS,),
            in_specs=[mine(g, half) for g, half in zip(gs, halves)] + whole, out_specs=whole),
        out_shape=[jax.ShapeDtypeStruct((N_CHIPS,) + half, BF16) for half in halves],
        compiler_params=_params(dimension_semantics=("arbitrary",)),
    )(core, *gs, *sibs)


def _chip_add(parts, recvs, chip_core, shapes, name):
    n = len(parts)
    halves = [_half_shape(*shape) for shape in shapes]

    def body(kc_ref, *refs):
        for p_ref, r_ref, o_ref in zip(refs[:n], refs[n:2 * n], refs[2 * n:]):
            acc = p_ref[0].astype(F32)
            for j in range(N_CHIPS - 1):
                acc = acc + r_ref[j].astype(F32)
            o_ref[...] = acc

    def out_spec(shape, half):
        return pl.BlockSpec(half, lambda s, kc_ref: _half_index(shape[0], kc_ref[1]))

    return pl.pallas_call(
        body, name=name,
        grid_spec=pltpu.PrefetchScalarGridSpec(
            num_scalar_prefetch=1, grid=(1,),
            in_specs=[pl.BlockSpec((1,) + half, lambda s, kc_ref: (kc_ref[0], 0, 0)) for half in halves]
            + [pl.BlockSpec((N_CHIPS - 1,) + half, lambda s, kc_ref: (0, 0, 0)) for half in halves],
            out_specs=[out_spec(shape, half) for shape, half in zip(shapes, halves)]),
        out_shape=[jax.ShapeDtypeStruct(tuple(shape), F32) for shape in shapes],
        compiler_params=_params(dimension_semantics=("arbitrary",)),
    )(chip_core, *parts, *recvs)


def _adamw_math(w, g, m, v):
    m = ADAM_B1 * m + (1.0 - ADAM_B1) * g
    v = ADAM_B2 * v + (1.0 - ADAM_B2) * (g * g)
    m_hat = m / (1.0 - ADAM_B1 ** ADAM_STEP)
    v_hat = v / (1.0 - ADAM_B2 ** ADAM_STEP)
    delta = -ADAM_LR * (m_hat / (jnp.sqrt(v_hat) + ADAM_EPS) + ADAM_WD * w)
    return delta, m, v


ADAM_PARTS = 4


def _adamw_matrix(w, g, m, v, name):
    rows, cols = w.shape
    by_rows = rows % (8 * ADAM_PARTS) == 0
    block = (rows // ADAM_PARTS, cols) if by_rows else (rows, cols // ADAM_PARTS)

    def body(w_ref, g_ref, m_ref, v_ref, go_ref, d_ref, mo_ref, vo_ref):
        gv = g_ref[...]
        go_ref[...] = gv
        d_ref[...], mo_ref[...], vo_ref[...] = _adamw_math(w_ref[...], gv, m_ref[...], v_ref[...])

    spec = pl.BlockSpec(block, lambda i: (i, 0) if by_rows else (0, i))
    shape = jax.ShapeDtypeStruct((rows, cols), F32)
    return pl.pallas_call(
        body, name=name, grid=(ADAM_PARTS,), in_specs=[spec] * 4, out_specs=[spec] * 4, out_shape=[shape] * 4,
        compiler_params=_params(dimension_semantics=("arbitrary",)),
    )(w, g, m, v)


def _adamw_small(ws, gs, ms, vs):
    n = len(ws)

    def body(*refs):
        for i in range(n):
            w_ref, g_ref, m_ref, v_ref = (refs[k * n + i] for k in range(4))
            d_ref, mo_ref, vo_ref = (refs[(4 + k) * n + i] for k in range(3))
            d_ref[...], mo_ref[...], vo_ref[...] = _adamw_math(w_ref[...], g_ref[...], m_ref[...], v_ref[...])

    shapes = [jax.ShapeDtypeStruct(w.shape, F32) for w in ws]
    out = pl.pallas_call(body, name="adamw_small", out_shape=shapes * 3, compiler_params=_params())(*ws, *gs, *ms, *vs)
    return out[:n], out[n:2 * n], out[2 * n:]


MATRICES = ["ffn1_w13", "ffn1_w2", "w_in", "w_out", "ffn2_w13", "ffn2_w2"]
VECTORS = ["ffn1_norm", "mix_norm", "conv_b", "conv_ln_g", "conv_ln_b", "forget_b", "out_norm_conv",
           "out_norm_attn", "ffn2_norm", "final_norm"]
WEIGHTS = ["ffn1_norm", "ffn1_w13", "ffn1_w2", "mix_norm", "w_in", "conv_w", "conv_b", "conv_ln_g", "conv_ln_b",
           "forget_b", "out_norm_conv", "out_norm_attn", "w_out", "ffn2_norm", "ffn2_w13", "ffn2_w2", "final_norm"]


def _pack_small(g, names):
    rows, layout = [], []
    for n in names:
        flat = g[n].reshape(-1)
        pad = (-flat.shape[0]) % LANES
        rows.append(jnp.pad(flat, (0, pad)).reshape(-1, LANES))
        layout.append((n, g[n].shape, flat.shape[0], rows[-1].shape[0]))
    packed = jnp.concatenate(rows, axis=0)
    pad_rows = (-packed.shape[0]) % 8
    return jnp.pad(packed, ((0, pad_rows), (0, 0))), layout


def _unpack_small(packed, layout):
    out, r = {}, 0
    for n, shape, size, nrows in layout:
        out[n] = packed[r:r + nrows].reshape(-1)[:size].reshape(shape)
        r += nrows
    return out


def kernel(x, ffn1_norm, ffn1_w13, ffn1_w2, mix_norm, w_in, conv_w, conv_b, conv_ln_g, conv_ln_b, forget_b, out_norm_conv, out_norm_attn, w_out, ffn2_norm, ffn2_w13, ffn2_w2, final_norm, loss_target, m_ffn1_norm, m_ffn1_w13, m_ffn1_w2, m_mix_norm, m_w_in, m_conv_w, m_conv_b, m_conv_ln_g, m_conv_ln_b, m_forget_b, m_out_norm_conv, m_out_norm_attn, m_w_out, m_ffn2_norm, m_ffn2_w13, m_ffn2_w2, m_final_norm, v_ffn1_norm, v_ffn1_w13, v_ffn1_w2, v_mix_norm, v_w_in, v_conv_w, v_conv_b, v_conv_ln_g, v_conv_ln_b, v_forget_b, v_out_norm_conv, v_out_norm_attn, v_w_out, v_ffn2_norm, v_ffn2_w13, v_ffn2_w2, v_final_norm):
    args = dict(locals())
    weights = {n: args[n] for n in WEIGHTS}
    core = lax.axis_index("c").astype(jnp.int32).reshape(1)
    chip = (2 * lax.axis_index("x") + lax.axis_index("y")).astype(jnp.int32)
    chip1 = chip.reshape(1)
    chip_core = jnp.concatenate([chip1, core])

    def held(n, a):
        return a[0].T if n == "w_in" else a[0]

    def given(n, a):
        return (a.T if n == "w_in" else a)[None]

    def slot(n, deps=()):
        if n == "conv_w":
            rows = jnp.pad(conv_w[0], ((0, CONV_PAD - CONV_WIDTH), (0, 0)))
            return _into_slot(rows, chip1, F32, "slot_conv_w", deps)
        return _into_slot(held(n, weights[n]), chip1, BF16, "slot_" + n, deps)

    fetched = {"ffn1_w13": ["ffn1_w13"], "ffn1_w2": ["ffn1_w2"], "mix": ["w_in", "w_out", "conv_w"],
               "ffn2": ["ffn2_w13", "ffn2_w2"]}
    fetch = {}

    def as_weights(group, bufs):
        out = {}
        for n, b in zip(fetched[group], bufs):
            if n.endswith("w13"):
                out[n] = b
            elif n != "conv_w":
                out[n] = b.reshape(N_CHIPS * b.shape[1], b.shape[2])
            else:
                out[n] = b[:, :CONV_WIDTH].transpose(1, 0, 2).reshape(CONV_WIDTH, D_CONV)
        return out

    def get_weights(group, after):
        if group == "ffn1_w13":
            first = [slot("ffn1_w13")]
            plan = _ici_gather_plan(first)
            started = _split_copy_start("gather_ffn1_w13_start", first, 3, plan)
            second = [slot("ffn1_w2", [started[3]])]
            plan2 = _ici_gather_plan(second)
            fetch["ffn1_w2"] = plan2, _split_copy_start("gather_ffn1_w2_start", second, 3, plan2)
            later_names = fetched["mix"] + fetched["ffn2"]
            later = [slot(n, [fetch["ffn1_w2"][1][3]]) for n in later_names]
            landed = _split_copy_wait("gather_ffn1_w13_wait", started, plan, [], passed=later)
            bufs = _gather_shards(landed[:1], "forward_ffn1_w13", ici=False)
            behind = dict(zip(later_names, landed[1:]))
            for later in ("mix", "ffn2"):
                bufs_later = [behind[n] for n in fetched[later]]
                plan = _ici_gather_plan(bufs_later)
                fetch[later] = plan, _split_copy_start("gather_%s_start" % later, bufs_later, 3 * len(bufs_later), plan)
            return as_weights(group, bufs), [fetch["mix"][1][3], fetch["ffn2"][1][3]]
        plan, started = fetch[group.split(":")[0]]
        if group == "ffn2:landed":
            landed = _split_copy_wait("gather_ffn2_wait", started, plan, [after])
            plan = _d2d_forward_plan(landed)
            fetch["ffn2"] = plan, _split_copy_start("forward_ffn2_start", landed, 3 * len(landed), plan)
            return {}, [fetch["ffn2"][1][3]]
        if group == "ffn2":
            return as_weights(group, _split_copy_wait("forward_ffn2_wait", started, plan, [after])), []
        landed = _split_copy_wait("gather_%s_wait" % group, started, plan, [after])
        return as_weights(group, _gather_shards(landed, "forward_" + group, ici=False)), []

    def shard_major(n, g):
        return g if n.endswith("w13") else g.reshape(N_CHIPS, g.shape[0] // N_CHIPS, g.shape[1])

    exchange, scatter = {}, {}
    small_names = VECTORS + ["conv_w"]
    small = {}

    def put_grads(group, grads):
        if group == "small":
            packed, layout = _pack_small(grads, small_names + ["loss"])
            me = (4 * lax.axis_index("x") + 2 * lax.axis_index("y") + lax.axis_index("c")).astype(jnp.int32).reshape(1)
            plan = _small_plan()
            exchange[group] = layout, plan, _split_copy_start("small_start", [_small_slots(packed, me)], len(FLIPS), plan)
            return [exchange[group][2][3]]
        names = list(grads)
        local = [shard_major(n, grads[n]) for n in names]
        landing = [lax.empty((N_CHIPS,) + _half_shape(*a.shape[1:]), BF16) for a in local]
        plan = _pair_exchange_plan(local)
        exchange[group] = names, plan, _split_copy_start("exchange_%s_start" % group, local + landing, len(local), plan)
        return [exchange[group][2][3]]

    def flush_grads(group, after):
        names, plan, started = exchange[group]
        done = _split_copy_wait("exchange_%s_wait" % group, started, plan, after)
        local, sib = done[:len(names)], done[len(names):]
        parts = list(_pair_add(local, sib, core, "pair_add_" + group))
        landing = [lax.empty((N_CHIPS - 1,) + q.shape[1:], BF16) for q in parts]
        plan = _ici_scatter_plan(len(parts))
        shapes = [a.shape[1:] for a in local]
        scatter[group] = names, plan, _split_copy_start("scatter_%s_start" % group, parts + landing, 3 * len(parts), plan), shapes
        return [scatter[group][2][3]]

    p = {n: weights[n] for n in VECTORS}
    p["final_norm"] = final_norm.reshape(1, D_MODEL)
    dx = _local_step(x[0], loss_target[0], p, get_weights, put_grads, flush_grads)
    layout, plan, started = exchange["small"]
    slots, = _split_copy_wait("small_wait", started, plan, [exchange["ffn1_w13"][2][3]])
    small.update(_unpack_small(_small_sum(slots), layout))
    loss = small["loss"].reshape(())

    grad = {n: small[n] for n in VECTORS}
    grad["final_norm"] = small["final_norm"].reshape(D_MODEL)
    grad["conv_w"] = lax.dynamic_slice_in_dim(small["conv_w"], chip * (D_CONV // N_CHIPS), D_CONV // N_CHIPS, axis=1)[None]

    delta, new_m, new_v = {}, {}, {}

    def reduce_chips(group, after):
        names, plan, started, shapes = scatter[group]
        done = _split_copy_wait("scatter_%s_wait" % group, started, plan, after)
        parts, landed = done[:len(names)], done[len(names):]
        return list(_chip_add(parts, landed, chip_core, shapes, "chip_add_" + group))

    def update(group, full):
        ends = []
        for n, reduced in zip(scatter[group][0], full):
            go, d, mo, vo = _adamw_matrix(held(n, weights[n]), reduced, held(n, args["m_" + n]), held(n, args["v_" + n]),
                                          "adamw_" + n)
            grad[n], delta[n], new_m[n], new_v[n] = given(n, go), given(n, d), given(n, mo), given(n, vo)
            ends.append(vo)
        return ends

    def share_start(group, halves):
        plan = _pair_share_plan(scatter[group][3])
        return plan, _split_copy_start("share_%s_start" % group, halves, len(halves), plan)

    halves_ffn2 = reduce_chips("ffn2", [exchange["ffn1_w13"][2][3]])
    plan_ffn2, share_ffn2 = share_start("ffn2", halves_ffn2)
    last_scatter = flush_grads("ffn1_w13", [share_ffn2[3]])
    halves_mix = reduce_chips("mix", last_scatter)
    plan_mix, share_mix = share_start("mix", halves_mix)
    done_ffn2 = update("ffn2", _split_copy_wait("share_ffn2_wait", share_ffn2, plan_ffn2, [share_mix[3]]))
    done_mix = update("mix", _split_copy_wait("share_mix_wait", share_mix, plan_mix, done_ffn2))
    as2d = lambda a: a.reshape(-1, a.shape[-1])
    ds, mos, vos = _adamw_small([as2d(weights[n]) for n in small_names], [as2d(grad[n]) for n in small_names],
                                [as2d(args["m_" + n]) for n in small_names], [as2d(args["v_" + n]) for n in small_names])
    for n, d, mo, vo in zip(small_names, ds, mos, vos):
        shape = weights[n].shape
        delta[n], new_m[n], new_v[n] = d.reshape(shape), mo.reshape(shape), vo.reshape(shape)
    behind = done_ffn2 + done_mix + [vos[0]]
    halves_w2 = reduce_chips("ffn1_w2", behind)
    halves_w13 = reduce_chips("ffn1_w13", behind)
    full_w2, full_w13 = _pair_share(halves_w2 + halves_w13, "pair_share_ffn1")
    update("ffn1_w2", [full_w2])
    update("ffn1_w13", [full_w13])

    return (loss, dx[None], *[grad[n] for n in WEIGHTS], *[delta[n] for n in WEIGHTS],
            *[new_m[n] for n in WEIGHTS], *[new_v[n] for n in WEIGHTS])
```

```python
import jax
import jax.numpy as jnp
from jax import lax
from jax.experimental import pallas as pl
from jax.experimental.pallas import tpu as pltpu

F32 = jnp.float32
BF16 = jnp.bfloat16

D_MODEL = 1024
D_FF = 2816
FF_SHARD = D_FF // 2
D_CONV = 512
D_ATTN = 512
N_HEADS = 8
HEAD_DIM = 64
CONV_WIDTH = 31
CONV_PAD = 32
N_IN = 2 * D_CONV + 3 * D_ATTN + N_HEADS
EPS = 1e-6
N_CHIPS = 4
LANES = 128
TOKEN_ROWS = 512
HEAD_ROWS = 16

ADAM_LR = 0.001
ADAM_B1 = 0.9
ADAM_B2 = 0.999
ADAM_EPS = 1e-08
ADAM_WD = 0.01
ADAM_STEP = 10

VMEM_LIMIT = 56 * 1024 * 1024

_NT = (((1,), (1,)), ((), ()))
_TN = (((0,), (0,)), ((), ()))


def _dot(a, b):
    return jnp.dot(a, b, preferred_element_type=F32)


def _dot_nt(a, b):
    return lax.dot_general(a, b, _NT, preferred_element_type=F32)


def _dot_tn(a, b):
    return lax.dot_general(a, b, _TN, preferred_element_type=F32)


def _params(**kw):
    return pltpu.CompilerParams(vmem_limit_bytes=VMEM_LIMIT, **kw)


def _sigmoid(x):
    return 1.0 / (1.0 + jnp.exp(-x))


def _rms_stats(x):
    return lax.rsqrt(jnp.mean(x * x, axis=-1, keepdims=True) + EPS)


def _rms_bwd(x, r, g, dh):
    t = dh * g
    dx = r * t - x * (r * r * r) * jnp.mean(t * x, axis=-1, keepdims=True)
    return dx, dh * x * r


def _silu_grad(z, sg):
    return sg * (1.0 + z * (1.0 - sg))


def _row_spec(tm, n):
    return pl.BlockSpec((tm, n), lambda i: (i, 0))


def _full_spec(shape):
    nd = len(shape)
    return pl.BlockSpec(shape, lambda i: (0,) * nd)


_ANY = pl.BlockSpec(memory_space=pl.ANY)


def _skip(n, body):
    return lambda *refs: body(*refs[n:])


FFN_ROWS = 256
FFN_WEIGHT_PARTS = N_CHIPS + 2


def _with_ffn_weights(w13_hbm, w2_hbm, w13_ref, w2_ref, sems, order, tile):
    first = pl.program_id(0) == 0
    copies = {}
    if w13_hbm is not None:
        for k in range(N_CHIPS):
            copies["w13", k] = pltpu.make_async_copy(w13_hbm.at[k], w13_ref.at[k], sems.at[k])
    if w2_hbm is not None:
        for half in range(2):
            rows = pl.ds(half * FF_SHARD, FF_SHARD)
            copies["w2", half] = pltpu.make_async_copy(w2_hbm.at[rows, :], w2_ref.at[rows, :], sems.at[N_CHIPS + half])

    @pl.when(first)
    def _():
        for part in order:
            copies[part].start()

        def ready(*parts):
            for part in parts:
                copies[part].wait()

        tile(ready)

    @pl.when(jnp.logical_not(first))
    def _():
        tile(lambda *parts: None)


def _ffn_fwd(x, g, w13s, w2, name, deps=()):
    t = x.shape[0]
    tm = FFN_ROWS
    deps = tuple(deps)

    def body(x_ref, g_ref, w13_hbm, w2_hbm, xo_ref, h_ref, gu_ref, a_ref, w13_ref, w2_ref, sems):
        def tile(ready):
            xv = x_ref[...]
            hb = (xv * _rms_stats(xv) * g_ref[...]).astype(BF16)
            h_ref[...] = hb
            acc = jnp.zeros((tm, D_MODEL), F32)
            for half in range(2):
                lo = half * FF_SHARD
                ready(("w13", half), ("w13", 2 + half))
                gate = _dot(hb, w13_ref[half])
                up = _dot(hb, w13_ref[2 + half])
                gu_ref[:, lo:lo + FF_SHARD] = gate.astype(BF16)
                gu_ref[:, D_FF + lo:D_FF + lo + FF_SHARD] = up.astype(BF16)
                a = (gate * _sigmoid(gate) * up).astype(BF16)
                a_ref[:, lo:lo + FF_SHARD] = a
                ready(("w2", half))
                acc = acc + _dot(a, w2_ref[lo:lo + FF_SHARD, :])
            xo_ref[...] = xv + 0.5 * acc

        _with_ffn_weights(w13_hbm, w2_hbm, w13_ref, w2_ref, sems,
                          [("w13", 0), ("w13", 2), ("w2", 0), ("w13", 1), ("w13", 3), ("w2", 1)], tile)

    return pl.pallas_call(
        _skip(len(deps), body), name=name, grid=(t // tm,),
        in_specs=[_ANY] * len(deps) + [_row_spec(tm, D_MODEL), _full_spec((1, D_MODEL)), _ANY, _ANY],
        out_specs=[_row_spec(tm, D_MODEL), _row_spec(tm, D_MODEL), _row_spec(tm, 2 * D_FF), _row_spec(tm, D_FF)],
        out_shape=[jax.ShapeDtypeStruct((t, D_MODEL), F32), jax.ShapeDtypeStruct((t, D_MODEL), BF16),
                   jax.ShapeDtypeStruct((t, 2 * D_FF), BF16), jax.ShapeDtypeStruct((t, D_FF), BF16)],
        scratch_shapes=[pltpu.VMEM(w13s.shape, BF16), pltpu.VMEM(w2.shape, BF16),
                        pltpu.SemaphoreType.DMA((FFN_WEIGHT_PARTS,))],
        compiler_params=_params(dimension_semantics=("arbitrary",)),
    )(*deps, x, g, w13s, w2)


def _ffn_up(x, g, w13s, name, deps=()):
    t = x.shape[0]
    tm = FFN_ROWS
    deps = tuple(deps)

    def body(x_ref, g_ref, w13_hbm, h_ref, gu_ref, a_ref, w13_ref, sems):
        def tile(ready):
            xv = x_ref[...]
            hb = (xv * _rms_stats(xv) * g_ref[...]).astype(BF16)
            h_ref[...] = hb
            for half in range(2):
                lo = half * FF_SHARD
                ready(("w13", half), ("w13", 2 + half))
                gate = _dot(hb, w13_ref[half])
                up = _dot(hb, w13_ref[2 + half])
                gu_ref[:, lo:lo + FF_SHARD] = gate.astype(BF16)
                gu_ref[:, D_FF + lo:D_FF + lo + FF_SHARD] = up.astype(BF16)
                a_ref[:, lo:lo + FF_SHARD] = (gate * _sigmoid(gate) * up).astype(BF16)

        _with_ffn_weights(w13_hbm, None, w13_ref, None, sems, [("w13", 0), ("w13", 2), ("w13", 1), ("w13", 3)], tile)

    return pl.pallas_call(
        _skip(len(deps), body), name=name, grid=(t // tm,),
        in_specs=[_ANY] * len(deps) + [_row_spec(tm, D_MODEL), _full_spec((1, D_MODEL)), _ANY],
        out_specs=[_row_spec(tm, D_MODEL), _row_spec(tm, 2 * D_FF), _row_spec(tm, D_FF)],
        out_shape=[jax.ShapeDtypeStruct((t, D_MODEL), BF16), jax.ShapeDtypeStruct((t, 2 * D_FF), BF16),
                   jax.ShapeDtypeStruct((t, D_FF), BF16)],
        scratch_shapes=[pltpu.VMEM(w13s.shape, BF16), pltpu.SemaphoreType.DMA((FFN_WEIGHT_PARTS,))],
        compiler_params=_params(dimension_semantics=("arbitrary",)),
    )(*deps, x, g, w13s)


def _ffn_down(x, a, w2, name):
    t = x.shape[0]
    tm = FFN_ROWS

    def body(x_ref, a_ref, w2_hbm, xo_ref, w2_ref, sems):
        def tile(ready):
            ready(("w2", 0))
            acc = _dot(a_ref[:, 0:FF_SHARD], w2_ref[0:FF_SHARD, :])
            ready(("w2", 1))
            acc = acc + _dot(a_ref[:, FF_SHARD:], w2_ref[FF_SHARD:, :])
            xo_ref[...] = x_ref[...] + 0.5 * acc

        _with_ffn_weights(None, w2_hbm, None, w2_ref, sems, [("w2", 0), ("w2", 1)], tile)

    return pl.pallas_call(
        body, name=name, grid=(t // tm,),
        in_specs=[_row_spec(tm, D_MODEL), _row_spec(tm, D_FF), _ANY],
        out_specs=_row_spec(tm, D_MODEL), out_shape=jax.ShapeDtypeStruct((t, D_MODEL), F32),
        scratch_shapes=[pltpu.VMEM(w2.shape, BF16), pltpu.SemaphoreType.DMA((FFN_WEIGHT_PARTS,))],
        compiler_params=_params(dimension_semantics=("arbitrary",)),
    )(x, a, w2)


def _ffn_bwd(dy, x, gu, g, w13s, w2, name, deps=()):
    t = x.shape[0]
    tm = FFN_ROWS
    deps = tuple(deps)

    def body(dy_ref, x_ref, gu_ref, g_ref, w13_hbm, w2_hbm, dx_ref, dgu_ref, dg_ref, dyh_ref, dxb_ref,
             w13_ref, w2_ref, sems):
        @pl.when(pl.program_id(0) == 0)
        def _():
            dg_ref[...] = jnp.zeros_like(dg_ref)

        def tile(ready):
            dyv = dy_ref[...]
            dyh = (0.5 * dyv).astype(BF16)
            dyh_ref[...] = dyh
            dh = jnp.zeros((tm, D_MODEL), F32)
            for half in range(2):
                lo = half * FF_SHARD
                ready(("w2", half))
                da = _dot_nt(dyh, w2_ref[lo:lo + FF_SHARD, :])
                gate = gu_ref[:, lo:lo + FF_SHARD].astype(F32)
                up = gu_ref[:, D_FF + lo:D_FF + lo + FF_SHARD].astype(F32)
                sg = _sigmoid(gate)
                act = gate * sg
                dgate = (da * up * _silu_grad(gate, sg)).astype(BF16)
                dup = (da * act).astype(BF16)
                dgu_ref[:, lo:lo + FF_SHARD] = dgate
                dgu_ref[:, D_FF + lo:D_FF + lo + FF_SHARD] = dup
                ready(("w13", half), ("w13", 2 + half))
                dh = dh + _dot_nt(dgate, w13_ref[half]) + _dot_nt(dup, w13_ref[2 + half])
            xv = x_ref[...]
            dxn, dg_rows = _rms_bwd(xv, _rms_stats(xv), g_ref[...], dh)
            dx = dyv + dxn
            dx_ref[...] = dx
            dxb_ref[...] = dx.astype(BF16)
            dg_ref[...] += jnp.sum(dg_rows, axis=0, keepdims=True)

        _with_ffn_weights(w13_hbm, w2_hbm, w13_ref, w2_ref, sems,
                          [("w2", 0), ("w13", 0), ("w13", 2), ("w2", 1), ("w13", 1), ("w13", 3)], tile)

    return pl.pallas_call(
        _skip(len(deps), body), name=name, grid=(t // tm,),
        in_specs=[_ANY] * len(deps) + [_row_spec(tm, D_MODEL), _row_spec(tm, D_MODEL), _row_spec(tm, 2 * D_FF),
                                       _full_spec((1, D_MODEL)), _ANY, _ANY],
        out_specs=[_row_spec(tm, D_MODEL), _row_spec(tm, 2 * D_FF),
                   _full_spec((1, D_MODEL)), _row_spec(tm, D_MODEL), _row_spec(tm, D_MODEL)],
        out_shape=[jax.ShapeDtypeStruct((t, D_MODEL), F32), jax.ShapeDtypeStruct((t, 2 * D_FF), BF16),
                   jax.ShapeDtypeStruct((1, D_MODEL), F32),
                   jax.ShapeDtypeStruct((t, D_MODEL), BF16), jax.ShapeDtypeStruct((t, D_MODEL), BF16)],
        scratch_shapes=[pltpu.VMEM(w13s.shape, BF16), pltpu.VMEM(w2.shape, BF16),
                        pltpu.SemaphoreType.DMA((FFN_WEIGHT_PARTS,))],
        compiler_params=_params(dimension_semantics=("arbitrary",)),
    )(*deps, dy, x, gu, g, w13s, w2)


WGRAD_ROWS = (1024, 512, 384, 256)


def _wgrad(a, b, n_blocks, name, deps=()):
    t, m = a.shape
    tm = next(rows for rows in WGRAD_ROWS if m % rows == 0)
    n = b.shape[1]
    bn = n // n_blocks
    deps = tuple(deps)
    assert a.dtype == BF16 and b.dtype == BF16

    def body(a_ref, b_ref, o_ref):
        o_ref[0] = _dot_tn(a_ref[...], b_ref[...]).astype(BF16)

    return pl.pallas_call(
        _skip(len(deps), body), name=name, grid=(n_blocks, m // tm),
        in_specs=[_ANY] * len(deps) + [pl.BlockSpec((t, tm), lambda j, i: (0, i)),
                                       pl.BlockSpec((t, bn), lambda j, i: (0, j))],
        out_specs=pl.BlockSpec((1, tm, bn), lambda j, i: (j, i, 0)),
        out_shape=jax.ShapeDtypeStruct((n_blocks, m, bn), BF16),
        compiler_params=_params(dimension_semantics=("arbitrary", "arbitrary")),
    )(*deps, a, b)


def _mix_proj(x, g, w_ag, w_qkv, w_f):
    t = x.shape[0]
    tm = TOKEN_ROWS

    def body(x_ref, g_ref, wag_ref, wqkv_ref, wf_ref, h_ref, ag_ref, qkv_ref, fl_ref):
        xv = x_ref[...]
        hb = (xv * _rms_stats(xv) * g_ref[...]).astype(BF16)
        h_ref[...] = hb
        ag_ref[...] = _dot_nt(hb, wag_ref[...])
        qkv_ref[...] = _dot_nt(hb, wqkv_ref[...]).astype(BF16)
        fl_ref[...] = _dot_nt(hb, wf_ref[...])

    return pl.pallas_call(
        body, name="mix_proj", grid=(t // tm,),
        in_specs=[_row_spec(tm, D_MODEL), _full_spec((1, D_MODEL)), _full_spec(w_ag.shape),
                  _full_spec(w_qkv.shape), _full_spec(w_f.shape)],
        out_specs=[_row_spec(tm, D_MODEL), _row_spec(tm, 2 * D_CONV), _row_spec(tm, 3 * D_ATTN),
                   _row_spec(tm, LANES)],
        out_shape=[jax.ShapeDtypeStruct((t, D_MODEL), BF16), jax.ShapeDtypeStruct((t, 2 * D_CONV), F32),
                   jax.ShapeDtypeStruct((t, 3 * D_ATTN), BF16), jax.ShapeDtypeStruct((t, LANES), F32)],
        compiler_params=_params(dimension_semantics=("arbitrary",)),
    )(x, g, w_ag, w_qkv, w_f)


def _mix_proj_bwd(dproj, dx2, x1, g, w_ag, w_qkv, w_f):
    t = x1.shape[0]
    tm = TOKEN_ROWS
    n_ag, n_qkv = 2 * D_CONV, 3 * D_ATTN

    def body(dp_ref, dx2_ref, x_ref, g_ref, wag_ref, wqkv_ref, wf_ref, dx_ref, dg_ref):
        @pl.when(pl.program_id(0) == 0)
        def _():
            dg_ref[...] = jnp.zeros_like(dg_ref)

        dh = (_dot(dp_ref[:, 0:n_ag], wag_ref[...]) + _dot(dp_ref[:, n_ag:n_ag + n_qkv], wqkv_ref[...])
              + _dot(dp_ref[:, n_ag + n_qkv:], wf_ref[...]))
        xv = x_ref[...]
        dxn, dg_rows = _rms_bwd(xv, _rms_stats(xv), g_ref[...], dh)
        dx_ref[...] = dx2_ref[...] + dxn
        dg_ref[...] += jnp.sum(dg_rows, axis=0, keepdims=True)

    return pl.pallas_call(
        body, name="mix_proj_bwd", grid=(t // tm,),
        in_specs=[_row_spec(tm, dproj.shape[1]),
                  _row_spec(tm, D_MODEL), _row_spec(tm, D_MODEL), _full_spec((1, D_MODEL)),
                  _full_spec(w_ag.shape), _full_spec(w_qkv.shape), _full_spec(w_f.shape)],
        out_specs=[_row_spec(tm, D_MODEL), _full_spec((1, D_MODEL))],
        out_shape=[jax.ShapeDtypeStruct((t, D_MODEL), F32), jax.ShapeDtypeStruct((1, D_MODEL), F32)],
        compiler_params=_params(dimension_semantics=("arbitrary",)),
    )(dproj, dx2, x1, g, w_ag, w_qkv, w_f)


def _split3(x):
    hi = x.astype(BF16)
    r1 = x - hi.astype(F32)
    mid = r1.astype(BF16)
    lo = (r1 - mid.astype(F32)).astype(BF16)
    return hi, mid, lo


def _gates_fwd(flt, fb):
    t = flt.shape[1]

    def body(f_ref, b_ref, d_ref):
        z = f_ref[...] + b_ref[...]
        logf = jnp.minimum(z, 0.0) - jnp.log(1.0 + jnp.exp(-jnp.abs(z)))
        row = lax.broadcasted_iota(jnp.int32, (LANES, LANES), 0)
        col = lax.broadcasted_iota(jnp.int32, (LANES, LANES), 1)
        upper = (row <= col).astype(BF16)
        carry = jnp.zeros((HEAD_ROWS, 1), F32)
        for blk in range(t // LANES):
            hi, mid, lo = _split3(logf[:, blk * LANES:(blk + 1) * LANES])
            cs = _dot(hi, upper) + _dot(mid, upper) + _dot(lo, upper)
            d_ref[:, blk * LANES:(blk + 1) * LANES] = cs + carry
            carry = carry + cs[:, LANES - 1:LANES]

    return pl.pallas_call(
        body, name="gates_fwd", out_shape=jax.ShapeDtypeStruct((HEAD_ROWS, t), F32),
        compiler_params=_params(),
    )(flt, fb)


def _gates_bwd(dd, flt, fb):
    t = flt.shape[1]

    def body(dd_ref, f_ref, b_ref, df_ref, db_ref):
        z = f_ref[...] + b_ref[...]
        row = lax.broadcasted_iota(jnp.int32, (LANES, LANES), 0)
        col = lax.broadcasted_iota(jnp.int32, (LANES, LANES), 1)
        lower = (row >= col).astype(BF16)
        carry = jnp.zeros((HEAD_ROWS, 1), F32)
        db = jnp.zeros((HEAD_ROWS, 1), F32)
        for blk in reversed(range(t // LANES)):
            sl = slice(blk * LANES, (blk + 1) * LANES)
            hi, mid, lo = _split3(dd_ref[:, sl])
            cs = _dot(hi, lower) + _dot(mid, lower) + _dot(lo, lower)
            dz = (cs + carry) * _sigmoid(-z[:, sl])
            df_ref[:, sl] = dz
            db = db + jnp.sum(dz, axis=1, keepdims=True)
            carry = carry + cs[:, 0:1]
        db_ref[...] = db

    return pl.pallas_call(
        body, name="gates_bwd",
        out_shape=[jax.ShapeDtypeStruct((HEAD_ROWS, t), F32), jax.ShapeDtypeStruct((HEAD_ROWS, 1), F32)],
        compiler_params=_params(),
    )(dd, flt, fb)


CONV_CHUNK = 128
CONV_TAIL = 16
CONV_WINDOW = CONV_CHUNK + CONV_PAD + 8
CONV_ROWS_EXTRA = CONV_PAD + CONV_TAIL
SUBLANES = 8


def _conv_rows(ag_ref, u_ref, t):
    u_ref[0:CONV_PAD, :] = jnp.zeros((CONV_PAD, D_CONV), F32)
    u_ref[CONV_PAD + t:CONV_ROWS_EXTRA + t, :] = jnp.zeros((CONV_TAIL, D_CONV), F32)

    def fill(i, c):
        r0 = pl.multiple_of(i * CONV_CHUNK, CONV_CHUNK)
        a = ag_ref[pl.ds(r0, CONV_CHUNK), 0:D_CONV]
        gt = ag_ref[pl.ds(r0, CONV_CHUNK), D_CONV:2 * D_CONV]
        u_ref[pl.ds(CONV_PAD + r0, CONV_CHUNK), :] = a * _sigmoid(gt)
        return c

    lax.fori_loop(0, t // CONV_CHUNK, fill, 0)


def _for_shifted(ref, r0, offsets, fn):
    window = ref[pl.ds(r0, CONV_WINDOW), :]
    for rem in range(SUBLANES):
        mine = [o for o in offsets if o % SUBLANES == rem]
        if not mine:
            continue
        turned = window if rem == 0 else pltpu.roll(window, CONV_WINDOW - rem, 0)
        for o in mine:
            fn(o, turned[o - rem:o - rem + CONV_CHUNK])


def _conv_taps(u_ref, r0, w_ref, cb):
    acc = [jnp.zeros((CONV_CHUNK, D_CONV), F32)]

    def tap(o, rows):
        j = o - (CONV_PAD - CONV_WIDTH + 1)
        acc[0] = acc[0] + w_ref[j:j + 1, :] * rows

    _for_shifted(u_ref, r0, [j + CONV_PAD - CONV_WIDTH + 1 for j in range(CONV_WIDTH)], tap)
    return acc[0] + cb


def _conv_point(y, lg, lb):
    mu = jnp.mean(y, axis=-1, keepdims=True)
    yc = y - mu
    rstd = lax.rsqrt(jnp.mean(yc * yc, axis=-1, keepdims=True) + EPS)
    yhat = yc * rstd
    z = yhat * lg + lb
    sg = _sigmoid(z)
    s = z * sg
    rr = _rms_stats(s)
    return yhat, rstd, z, sg, s, rr


def _conv_fwd(ag, conv_w, conv_b, ln_g, ln_b, norm_g):
    t = ag.shape[0]

    def body(ag_ref, w_ref, cb_ref, lg_ref, lb_ref, ng_ref, o_ref, y_ref, u_ref):
        _conv_rows(ag_ref, u_ref, t)
        cb, lg, lb, ng = cb_ref[...], lg_ref[...], lb_ref[...], ng_ref[...]

        def chunk(i, c):
            r0 = pl.multiple_of(i * CONV_CHUNK, CONV_CHUNK)
            y = _conv_taps(u_ref, r0, w_ref, cb)
            y_ref[pl.ds(r0, CONV_CHUNK), :] = y
            _, _, _, _, s, rr = _conv_point(y, lg, lb)
            o_ref[pl.ds(r0, CONV_CHUNK), :] = (s * rr * ng).astype(BF16)
            return c

        lax.fori_loop(0, t // CONV_CHUNK, chunk, 0)

    return pl.pallas_call(
        body, name="conv_fwd",
        out_shape=[jax.ShapeDtypeStruct((t, D_CONV), BF16), jax.ShapeDtypeStruct((t, D_CONV), F32)],
        scratch_shapes=[pltpu.VMEM((t + CONV_ROWS_EXTRA, D_CONV), F32)],
        compiler_params=_params(),
    )(ag, conv_w, conv_b, ln_g, ln_b, norm_g)


def _conv_bwd(ag, y, dout, conv_w, ln_g, ln_b, norm_g):
    t = ag.shape[0]

    def body(ag_ref, y_ref, do_ref, w_ref, lg_ref, lb_ref, ng_ref,
             dag_ref, dw_ref, dcb_ref, dlg_ref, dlb_ref, dng_ref, u_ref, dy_ref):
        _conv_rows(ag_ref, u_ref, t)
        dy_ref[t:t + CONV_ROWS_EXTRA, :] = jnp.zeros((CONV_ROWS_EXTRA, D_CONV), F32)
        lg, lb, ng = lg_ref[...], lb_ref[...], ng_ref[...]
        dw_ref[...] = jnp.zeros_like(dw_ref)
        zero = jnp.zeros((1, D_CONV), F32)

        def chunk(i, carry):
            dcb, dlg, dlb, dng = carry
            r0 = pl.multiple_of(i * CONV_CHUNK, CONV_CHUNK)
            yhat, rstd, z, sg, s, rr = _conv_point(y_ref[pl.ds(r0, CONV_CHUNK), :], lg, lb)
            do = do_ref[pl.ds(r0, CONV_CHUNK), :]
            ds, dng_rows = _rms_bwd(s, rr, ng, do)
            dz = ds * _silu_grad(z, sg)
            dyhat = dz * lg
            dy = rstd * (dyhat - jnp.mean(dyhat, axis=-1, keepdims=True)
                         - yhat * jnp.mean(dyhat * yhat, axis=-1, keepdims=True))
            dy_ref[pl.ds(r0, CONV_CHUNK), :] = dy
            def tap(o, rows):
                j = o - (CONV_PAD - CONV_WIDTH + 1)
                dw_ref[j:j + 1, :] += jnp.sum(dy * rows, axis=0, keepdims=True)

            _for_shifted(u_ref, r0, [j + CONV_PAD - CONV_WIDTH + 1 for j in range(CONV_WIDTH)], tap)
            return (dcb + jnp.sum(dy, axis=0, keepdims=True), dlg + jnp.sum(dz * yhat, axis=0, keepdims=True),
                    dlb + jnp.sum(dz, axis=0, keepdims=True), dng + jnp.sum(dng_rows, axis=0, keepdims=True))

        dcb, dlg, dlb, dng = lax.fori_loop(0, t // CONV_CHUNK, chunk, (zero, zero, zero, zero))
        dcb_ref[...] = dcb
        dlg_ref[...] = dlg
        dlb_ref[...] = dlb
        dng_ref[...] = dng

        def chunk2(i, c):
            r0 = pl.multiple_of(i * CONV_CHUNK, CONV_CHUNK)
            acc = [jnp.zeros((CONV_CHUNK, D_CONV), F32)]

            def tap(o, rows):
                j = CONV_WIDTH - 1 - o
                acc[0] = acc[0] + w_ref[j:j + 1, :] * rows

            _for_shifted(dy_ref, r0, list(range(CONV_WIDTH)), tap)
            du = acc[0]
            a = ag_ref[pl.ds(r0, CONV_CHUNK), 0:D_CONV]
            gt = ag_ref[pl.ds(r0, CONV_CHUNK), D_CONV:2 * D_CONV]
            sg = _sigmoid(gt)
            dag_ref[pl.ds(r0, CONV_CHUNK), 0:D_CONV] = (du * sg).astype(BF16)
            dag_ref[pl.ds(r0, CONV_CHUNK), D_CONV:2 * D_CONV] = (du * a * sg * (1.0 - sg)).astype(BF16)
            return c

        lax.fori_loop(0, t // CONV_CHUNK, chunk2, 0)

    vec = jax.ShapeDtypeStruct((1, D_CONV), F32)
    return pl.pallas_call(
        body, name="conv_bwd",
        out_shape=[jax.ShapeDtypeStruct((t, 2 * D_CONV), BF16), jax.ShapeDtypeStruct((CONV_PAD, D_CONV), F32),
                   vec, vec, vec, vec],
        scratch_shapes=[pltpu.VMEM((t + CONV_ROWS_EXTRA, D_CONV), F32), pltpu.VMEM((t + CONV_ROWS_EXTRA, D_CONV), F32)],
        compiler_params=_params(),
    )(ag, y, dout, conv_w, ln_g, ln_b, norm_g)


Q_ROWS = 256
ATTN_SCALE = HEAD_DIM ** -0.5
ATTN_AHEAD = 1


def _attn_specs(t):
    blk = lambda off: pl.BlockSpec((t, LANES), lambda p: (0, off + p))
    pairs = N_HEADS // 2
    return [blk(0), blk(pairs), blk(2 * pairs), pl.BlockSpec((2, 1, t), lambda p: (p, 0, 0))]


def _one_head(q2, mask):
    return jnp.where(mask, q2, jnp.zeros_like(q2)) * ATTN_SCALE


def _attn_scores(qs, k2, drow, r0, q1):
    s = _dot_nt(qs, k2) - drow
    rowi = lax.broadcasted_iota(jnp.int32, (q1 - r0, q1 - r0), 0)
    coli = lax.broadcasted_iota(jnp.int32, (q1 - r0, q1 - r0), 1)
    diag = jnp.where(coli <= rowi, s[:, r0:q1], -jnp.inf)
    return diag if r0 == 0 else jnp.concatenate([s[:, :r0], diag], axis=1)


def _attn_fwd(qkv, drow, deps=()):
    t = qkv.shape[0]
    deps = tuple(deps)

    def body(q_ref, k_ref, v_ref, dr_ref, o_ref, lse_ref):
        head_a = lax.broadcasted_iota(jnp.int32, (1, LANES), 1) < HEAD_DIM
        items = [(qb, hh) for qb in range(t // Q_ROWS) for hh in range(2)]

        def scores(item):
            qb, hh = item
            r0, q1 = qb * Q_ROWS, (qb + 1) * Q_ROWS
            qs = _one_head(q_ref[r0:q1, :], head_a if hh == 0 else ~head_a)
            return _attn_scores(qs, k_ref[0:q1, :], dr_ref[hh, :, 0:q1], r0, q1)

        ahead = [scores(item) for item in items[:ATTN_AHEAD]]
        outs = []
        for n, (qb, hh) in enumerate(items):
            r0, q1 = qb * Q_ROWS, (qb + 1) * Q_ROWS
            s = ahead.pop(0)
            if n + ATTN_AHEAD < len(items):
                ahead.append(scores(items[n + ATTN_AHEAD]))
            mx = jnp.max(s, axis=1, keepdims=True)
            p = jnp.exp(s - mx)
            l = jnp.sum(p, axis=1, keepdims=True)
            lse_ref[hh, r0:q1, :] = mx + jnp.log(l)
            outs.append(_dot(p.astype(BF16), v_ref[0:q1, :]) * (1.0 / l))
            if hh == 1:
                o_ref[r0:q1, :] = jnp.where(head_a, outs[0], outs[1])
                outs = []

    pairs = N_HEADS // 2
    return pl.pallas_call(
        _skip(len(deps), body), name="attn_fwd", grid=(pairs,), in_specs=[_ANY] * len(deps) + _attn_specs(t),
        out_specs=[pl.BlockSpec((t, LANES), lambda p: (0, p)), pl.BlockSpec((2, t, 1), lambda p: (p, 0, 0))],
        out_shape=[jax.ShapeDtypeStruct((t, D_ATTN), F32), jax.ShapeDtypeStruct((N_HEADS, t, 1), F32)],
        compiler_params=_params(dimension_semantics=("arbitrary",)),
    )(*deps, qkv, qkv, qkv, drow)


def _attn_bwd(qkv, drow, lse, do):
    t = qkv.shape[0]

    def body(q_ref, k_ref, v_ref, dr_ref, lse_ref, do_ref,
             dq_ref, dk_ref, dv_ref, dd_ref, dk_acc, dv_acc):
        head_a = lax.broadcasted_iota(jnp.int32, (1, LANES), 1) < HEAD_DIM
        dk_acc[...] = jnp.zeros_like(dk_acc)
        dv_acc[...] = jnp.zeros_like(dv_acc)
        dd_ref[...] = jnp.zeros_like(dd_ref)
        items = [(qb, hh) for qb in range(t // Q_ROWS) for hh in range(2)]

        def products(item):
            qb, hh = item
            r0, q1 = qb * Q_ROWS, (qb + 1) * Q_ROWS
            mask = head_a if hh == 0 else ~head_a
            qs = _one_head(q_ref[r0:q1, :], mask)
            dob = jnp.where(mask, do_ref[r0:q1, :], 0.0).astype(BF16)
            s = _attn_scores(qs, k_ref[0:q1, :], dr_ref[hh, :, 0:q1], r0, q1)
            return qs, dob, s, _dot_nt(dob, v_ref[0:q1, :])

        ahead = products(items[0])
        dqs = []
        for n, (qb, hh) in enumerate(items):
            r0, q1 = qb * Q_ROWS, (qb + 1) * Q_ROWS
            qs, dob, s, dp = ahead
            if n + 1 < len(items):
                ahead = products(items[n + 1])
            p = jnp.exp(s - lse_ref[hh, r0:q1, :])
            ds = p * (dp - jnp.sum(p * dp, axis=1, keepdims=True))
            dsb = ds.astype(BF16)
            dqs.append(_dot(dsb, k_ref[0:q1, :]) * ATTN_SCALE)
            dk_acc[0:q1, :] += _dot_tn(dsb, qs)
            dv_acc[0:q1, :] += _dot_tn(p.astype(BF16), dob)
            dd_ref[hh, :, 0:q1] -= jnp.sum(ds, axis=0, keepdims=True)
            if hh == 1:
                dq_ref[r0:q1, :] = jnp.where(head_a, dqs[0], dqs[1]).astype(BF16)
                dqs = []
        dk_ref[...] = dk_acc[...].astype(BF16)
        dv_ref[...] = dv_acc[...].astype(BF16)

    pairs = N_HEADS // 2
    col = pl.BlockSpec((t, LANES), lambda p: (0, p))
    grad = jax.ShapeDtypeStruct((t, D_ATTN), BF16)
    return pl.pallas_call(
        body, name="attn_bwd", grid=(pairs,),
        in_specs=_attn_specs(t) + [pl.BlockSpec((2, t, 1), lambda p: (p, 0, 0)), col],
        out_specs=[col, col, col, pl.BlockSpec((2, 1, t), lambda p: (p, 0, 0))],
        out_shape=[grad, grad, grad, jax.ShapeDtypeStruct((N_HEADS, 1, t), F32)],
        scratch_shapes=[pltpu.VMEM((t, LANES), F32), pltpu.VMEM((t, LANES), F32)],
        compiler_params=_params(dimension_semantics=("arbitrary",)),
    )(qkv, qkv, qkv, drow, lse, do)


def _out_proj(ycn, o, g_attn, w_out, x1, deps=()):
    t = x1.shape[0]
    tm = TOKEN_ROWS
    deps = tuple(deps)

    def body(yc_ref, o_ref, g_ref, w_ref, x_ref, xo_ref, ya_ref):
        ov = o_ref[...]
        ya = (ov * _rms_stats(ov) * g_ref[...]).astype(BF16)
        ya_ref[...] = ya
        xo_ref[...] = x_ref[...] + _dot(yc_ref[...], w_ref[0:D_CONV, :]) + _dot(ya, w_ref[D_CONV:, :])

    return pl.pallas_call(
        _skip(len(deps), body), name="out_proj", grid=(t // tm,),
        in_specs=[_ANY] * len(deps) + [_row_spec(tm, D_CONV), _row_spec(tm, D_ATTN), _full_spec((1, D_ATTN)),
                                       _full_spec(w_out.shape), _row_spec(tm, D_MODEL)],
        out_specs=[_row_spec(tm, D_MODEL), _row_spec(tm, D_ATTN)],
        out_shape=[jax.ShapeDtypeStruct((t, D_MODEL), F32), jax.ShapeDtypeStruct((t, D_ATTN), BF16)],
        compiler_params=_params(dimension_semantics=("arbitrary",)),
    )(*deps, ycn, o, g_attn, w_out, x1)


def _out_proj_bwd(dx2, o, g_attn, w_out, deps=()):
    t = dx2.shape[0]
    tm = TOKEN_ROWS
    deps = tuple(deps)

    def body(dx_ref, o_ref, g_ref, w_ref, dyc_ref, do_ref, dg_ref):
        @pl.when(pl.program_id(0) == 0)
        def _():
            dg_ref[...] = jnp.zeros_like(dg_ref)

        dxb = dx_ref[...]
        dyc_ref[...] = _dot_nt(dxb, w_ref[0:D_CONV, :])
        dya = _dot_nt(dxb, w_ref[D_CONV:, :])
        ov = o_ref[...]
        do, dg_rows = _rms_bwd(ov, _rms_stats(ov), g_ref[...], dya)
        do_ref[...] = do
        dg_ref[...] += jnp.sum(dg_rows, axis=0, keepdims=True)

    return pl.pallas_call(
        _skip(len(deps), body), name="out_proj_bwd", grid=(t // tm,),
        in_specs=[_ANY] * len(deps) + [_row_spec(tm, D_MODEL), _row_spec(tm, D_ATTN), _full_spec((1, D_ATTN)),
                                       _full_spec(w_out.shape)],
        out_specs=[_row_spec(tm, D_CONV), _row_spec(tm, D_ATTN), _full_spec((1, D_ATTN))],
        out_shape=[jax.ShapeDtypeStruct((t, D_CONV), F32), jax.ShapeDtypeStruct((t, D_ATTN), F32),
                   jax.ShapeDtypeStruct((1, D_ATTN), F32)],
        compiler_params=_params(dimension_semantics=("arbitrary",)),
    )(*deps, dx2, o, g_attn, w_out)


def _loss_bwd(x3, target, g):
    t = x3.shape[0]
    tm = TOKEN_ROWS

    def body(x_ref, t_ref, g_ref, loss_ref, dx_ref, dg_ref):
        @pl.when(pl.program_id(0) == 0)
        def _():
            loss_ref[...] = jnp.zeros_like(loss_ref)
            dg_ref[...] = jnp.zeros_like(dg_ref)

        xv = x_ref[...]
        r = _rms_stats(xv)
        gv = g_ref[...]
        err = xv * r * gv - t_ref[...]
        row = jnp.sum(err * err, axis=1, keepdims=True) * (0.5 / D_MODEL)
        loss_ref[...] += jnp.sum(row, axis=0, keepdims=True)
        dx, dg_rows = _rms_bwd(xv, r, gv, err * (1.0 / D_MODEL))
        dx_ref[...] = dx
        dg_ref[...] += jnp.sum(dg_rows, axis=0, keepdims=True)

    return pl.pallas_call(
        body, name="loss_bwd", grid=(t // tm,),
        in_specs=[_row_spec(tm, D_MODEL), _row_spec(tm, D_MODEL), _full_spec((1, D_MODEL))],
        out_specs=[_full_spec((1, LANES)), _row_spec(tm, D_MODEL), _full_spec((1, D_MODEL))],
        out_shape=[jax.ShapeDtypeStruct((1, LANES), F32), jax.ShapeDtypeStruct((t, D_MODEL), F32),
                   jax.ShapeDtypeStruct((1, D_MODEL), F32)],
        compiler_params=_params(dimension_semantics=("arbitrary",)),
    )(x3, target, g)


def _split_w_in(w_in_t):
    w_ag = w_in_t[:2 * D_CONV]
    w_qkv = w_in_t[2 * D_CONV:2 * D_CONV + 3 * D_ATTN]
    w_f = jnp.pad(w_in_t[2 * D_CONV + 3 * D_ATTN:], ((0, LANES - N_HEADS), (0, 0)))
    return w_ag, w_qkv, w_f


def _head_rows(v):
    return jnp.pad(v, ((0, HEAD_ROWS - N_HEADS),) + ((0, 0),) * (v.ndim - 1))


def _local_step(x, target, p, get_weights, put_grads, flush_grads):
    t = x.shape[0]
    fb = _head_rows(p["forget_b"].reshape(N_HEADS, 1))

    w, deps = get_weights("ffn1_w13", None)
    h1, gu1, act1 = _ffn_up(x, p["ffn1_norm"], w["ffn1_w13"], "ffn1_up", deps)
    w2, _ = get_weights("ffn1_w2", act1)
    w.update(w2)
    x1 = _ffn_down(x, act1, w["ffn1_w2"], "ffn1_down")
    wm, _ = get_weights("mix", x1)
    w.update(wm)
    w_ag, w_qkv, w_f = _split_w_in(w["w_in"])
    conv_w = jnp.pad(w["conv_w"], ((0, CONV_PAD - CONV_WIDTH), (0, 0)))
    h2, ag, qkv, fl = _mix_proj(x1, p["mix_norm"], w_ag, w_qkv, w_f)
    flt = _head_rows(fl[:, :N_HEADS].T)
    dcum = _gates_fwd(flt, fb)[:N_HEADS]
    drow = dcum.reshape(N_HEADS, 1, t)
    ycn, y_conv = _conv_fwd(ag, conv_w, p["conv_b"], p["conv_ln_g"], p["conv_ln_b"], p["out_norm_conv"])
    o, lse = _attn_fwd(qkv, drow, [ycn])
    _, deps = get_weights("ffn2:landed", o)
    x2, yan = _out_proj(ycn, o, p["out_norm_attn"], w["w_out"], x1, deps)
    w2, _ = get_weights("ffn2", x2)
    w.update(w2)
    x3, h3, gu2, act2 = _ffn_fwd(x2, p["ffn2_norm"], w["ffn2_w13"], w["ffn2_w2"], "ffn2_fwd")
    loss, dx3, d_final = _loss_bwd(x3, target, p["final_norm"])

    g = {}
    dx2, dgu2, g["ffn2_norm"], dx3_half, dx2_bf16 = _ffn_bwd(
        dx3, x2, gu2, p["ffn2_norm"], w["ffn2_w13"], w["ffn2_w2"], "ffn2_bwd")
    dw13 = _wgrad(h3, dgu2, N_CHIPS, "ffn2_dw13")
    dw2 = _wgrad(act2, dx3_half, 1, "ffn2_dw2").reshape(D_FF, D_MODEL)
    deps = put_grads("ffn2", {"ffn2_w13": dw13, "ffn2_w2": dw2})
    dyc, do, g["out_norm_attn"] = _out_proj_bwd(dx2_bf16, o, p["out_norm_attn"], w["w_out"], deps)
    deps = flush_grads("ffn2", [dyc])
    dw_out = _wgrad(jnp.concatenate([ycn, yan], axis=1), dx2_bf16, 1, "dw_out", deps).reshape(D_MODEL, D_MODEL)
    dq, dk, dv, ddrow = _attn_bwd(qkv, drow, lse, do)
    dflt, dfb = _gates_bwd(_head_rows(ddrow.reshape(N_HEADS, t)), flt, fb)
    g["forget_b"] = dfb[:N_HEADS, 0].reshape(1, N_HEADS)
    dfl = jnp.pad(dflt[:N_HEADS].T, ((0, 0), (0, LANES - N_HEADS)))
    dag, dconv_w, g["conv_b"], g["conv_ln_g"], g["conv_ln_b"], g["out_norm_conv"] = _conv_bwd(
        ag, y_conv, dyc, conv_w, p["conv_ln_g"], p["conv_ln_b"], p["out_norm_conv"])
    g["conv_w"] = dconv_w[:CONV_WIDTH]
    dproj = jnp.concatenate([dag, dq, dk, dv, dfl.astype(BF16)], axis=1)
    dx1, g["mix_norm"] = _mix_proj_bwd(dproj, dx2, x1, p["mix_norm"], w_ag, w_qkv, w_f)
    dw_in = _wgrad(dproj, h2, 1, "dw_in").reshape(dproj.shape[1], D_MODEL)[:N_IN]
    deps = put_grads("mix", {"w_in": dw_in, "w_out": dw_out})
    dx0, dgu1, g["ffn1_norm"], dx1_half, _ = _ffn_bwd(
        dx1, x, gu1, p["ffn1_norm"], w["ffn1_w13"], w["ffn1_w2"], "ffn1_bwd", deps)
    g["final_norm"] = d_final
    g["loss"] = loss[:, :1]
    deps = flush_grads("mix", put_grads("small", g))
    dw2 = _wgrad(act1, dx1_half, 1, "ffn1_dw2", deps).reshape(D_FF, D_MODEL)
    deps = flush_grads("ffn1_w2", put_grads("ffn1_w2", {"ffn1_w2": dw2}))
    dw13 = _wgrad(h1, dgu1, N_CHIPS, "ffn1_dw13", deps)
    put_grads("ffn1_w13", {"ffn1_w13": dw13})
    return dx0


MESH = pl.DeviceIdType.MESH


def _place():
    x, y, c = lax.axis_index("x"), lax.axis_index("y"), lax.axis_index("c")
    chips = [(1 - x, y), (x, 1 - y), (1 - x, 1 - y)]
    return x, y, c, chips


def _hbm_out(shape, dtype):
    return jax.ShapeDtypeStruct(shape, dtype)


def _comm_call(body, name, ins, out_shapes, n_remote, in_place=False):
    return pl.pallas_call(
        body, name=name, in_specs=[_ANY] * len(ins), out_specs=[_ANY] * len(out_shapes), out_shape=out_shapes,
        scratch_shapes=[pltpu.SemaphoreType.DMA((n_remote,)), pltpu.SemaphoreType.DMA((n_remote,))],
        input_output_aliases={i: i for i in range(len(ins))} if in_place else {},
    )(*ins)


def _remote(src, dst, sems, n, to):
    send_sems, recv_sems = sems
    return pltpu.make_async_remote_copy(src_ref=src, dst_ref=dst, send_sem=send_sems.at[n], recv_sem=recv_sems.at[n],
                                        device_id=to, device_id_type=MESH)


HALF_ROWS_MULTIPLE = 32


def _halved_by_rows(rows):
    return rows % HALF_ROWS_MULTIPLE == 0


def _half_shape(rows, cols):
    return (rows // 2, cols) if _halved_by_rows(rows) else (rows, cols // 2)


def _half_index(rows, core):
    return (core, 0) if _halved_by_rows(rows) else (0, core)


def _half_of(ref, rows, cols, core, *lead):
    if _halved_by_rows(rows):
        return ref.at[(*lead, pl.ds(core * (rows // 2), rows // 2), slice(None))]
    return ref.at[(*lead, slice(None), pl.ds(core * (cols // 2), cols // 2))]


def _into_slot(shard, chip, dtype, name, deps=()):
    rows, cols = shard.shape
    half = _half_shape(rows, cols)
    by_rows = _halved_by_rows(rows)
    deps = tuple(deps)

    def body(k_ref, *refs):
        s_ref, o_ref = refs[len(deps):]
        o_ref[0] = s_ref[...].astype(dtype)

    return pl.pallas_call(
        body, name=name,
        grid_spec=pltpu.PrefetchScalarGridSpec(
            num_scalar_prefetch=1, grid=(2,),
            in_specs=[_ANY] * len(deps) + [pl.BlockSpec(half, lambda i, k_ref: (i, 0) if by_rows else (0, i))],
            out_specs=pl.BlockSpec((1,) + half, lambda i, k_ref: (k_ref[0], i, 0) if by_rows else (k_ref[0], 0, i))),
        out_shape=jax.ShapeDtypeStruct((N_CHIPS, rows, cols), dtype),
        compiler_params=_params(dimension_semantics=("arbitrary",)),
    )(chip, *deps, shard)


def _run_copies(name, bufs, n_copies, plan):
    n = len(bufs)

    def body(*refs):
        copies = plan(refs[n:2 * n], refs[2 * n:2 * n + 2])
        for send, _ in copies:
            send.start()
        for send, recv in copies:
            send.wait_send()
            recv.wait_recv()

    return _comm_call(body, name, bufs, [_hbm_out(b.shape, b.dtype) for b in bufs], n_copies, in_place=True)


def _forward_halves(slots, name):
    return _run_copies(name, slots, 3 * len(slots), _d2d_forward_plan(slots))


_HBM = pl.BlockSpec(memory_space=pltpu.HBM)
_SEM = pl.BlockSpec(memory_space=pltpu.SEMAPHORE)
_DATAFLOW = pltpu.SideEffectType.DATAFLOW_SIDE_EFFECTING


def _split_copy_start(name, bufs, n_copies, plan):
    n = len(bufs)

    def body(*refs):
        for send, _ in plan(refs[:n], (refs[n], refs[n + 1])):
            send.start()
        token = refs[-1]
        token[...] = jnp.zeros_like(token)

    out = pl.pallas_call(
        body, name=name,
        out_shape=(pltpu.SemaphoreType.DMA((n_copies,)), pltpu.SemaphoreType.DMA((n_copies,)),
                   *[pltpu.HBM(b.shape, b.dtype) for b in bufs], jax.ShapeDtypeStruct((8, LANES), F32)),
        in_specs=[_HBM] * n, out_specs=(_SEM, _SEM, *[_HBM] * n, pl.BlockSpec(memory_space=pltpu.VMEM)),
        input_output_aliases={i: 2 + i for i in range(n)},
        compiler_params=pltpu.CompilerParams(has_side_effects=_DATAFLOW),
    )(*[pltpu.with_memory_space_constraint(b, pltpu.HBM) for b in bufs])
    return out[0], out[1], list(out[2:2 + n]), out[-1]


def _split_copy_wait(name, started, plan, after, passed=()):
    send_sems, recv_sems, bufs, _ = started
    n = len(bufs)
    after = tuple(after)
    bufs = list(bufs) + list(passed)
    total = len(bufs)

    def body(*refs):
        for send, recv in plan(refs[:n], (refs[total], refs[total + 1])):
            send.wait_send()
            recv.wait_recv()

    out = pl.pallas_call(
        body, name=name, out_shape=tuple(pltpu.HBM(b.shape, b.dtype) for b in bufs),
        in_specs=[_HBM] * total + [_SEM, _SEM] + [_ANY] * len(after), out_specs=tuple([_HBM] * total),
        input_output_aliases={i: i for i in range(total)},
        compiler_params=pltpu.CompilerParams(has_side_effects=_DATAFLOW),
    )(*bufs, send_sems, recv_sems, *after)
    return list(out)


def _ici_gather_plan(slots):
    def plan(refs, sems):
        x, y, c, chips = _place()
        me = 2 * x + y
        copies = []
        for i, ref in enumerate(refs):
            for j, chip in enumerate(chips):
                mine = _half_of(ref, *slots[i].shape[1:], c, me)
                theirs = _half_of(ref, *slots[i].shape[1:], c, 2 * chip[0] + chip[1])
                to = (*chip, c)
                copies.append((_remote(mine, mine, sems, 3 * i + j, to), _remote(theirs, theirs, sems, 3 * i + j, to)))
        return copies

    return plan


def _ici_scatter_plan(n):
    def plan(refs, sems):
        x, y, c, chips = _place()
        copies = []
        for i in range(n):
            for j, chip in enumerate(chips):
                cp = _remote(refs[i].at[2 * chip[0] + chip[1]], refs[n + i].at[j], sems, 3 * i + j, (*chip, c))
                copies.append((cp, cp))
        return copies

    return plan


def _d2d_forward_plan(slots):
    def plan(refs, sems):
        x, y, c, chips = _place()
        sibling = (x, y, 1 - c)
        copies = []
        for i, ref in enumerate(refs):
            for j, chip in enumerate(chips):
                src_chip = 2 * chip[0] + chip[1]
                mine = _half_of(ref, *slots[i].shape[1:], c, src_chip)
                theirs = _half_of(ref, *slots[i].shape[1:], 1 - c, src_chip)
                copies.append((_remote(mine, mine, sems, 3 * i + j, sibling),
                               _remote(theirs, theirs, sems, 3 * i + j, sibling)))
        return copies

    return plan


def _pair_exchange_plan(grads):
    n = len(grads)

    def plan(refs, sems):
        x, y, c, _ = _place()
        copies = []
        for i in range(n):
            theirs = _half_of(refs[i], *grads[i].shape[1:], 1 - c, slice(None))
            cp = _remote(theirs, refs[n + i], sems, i, (x, y, 1 - c))
            copies.append((cp, cp))
        return copies

    return plan


def _pair_share_plan(shapes):
    def plan(refs, sems):
        x, y, c, _ = _place()
        sibling = (x, y, 1 - c)
        copies = []
        for i, ref in enumerate(refs):
            mine, theirs = _half_of(ref, *shapes[i], c), _half_of(ref, *shapes[i], 1 - c)
            copies.append((_remote(mine, mine, sems, i, sibling), _remote(theirs, theirs, sems, i, sibling)))
        return copies

    return plan


def _pair_share(halves, name):
    return _run_copies(name, halves, len(halves), _pair_share_plan([h.shape for h in halves]))


N_DEVICES = 8
FLIPS = [(fx, fy, fc) for fx in range(2) for fy in range(2) for fc in range(2)][1:]


def _small_slots(v, me):
    rows = v.shape[0]

    def body(k_ref, v_ref, o_ref):
        o_ref[0] = v_ref[...]

    return pl.pallas_call(
        body, name="small_slot",
        grid_spec=pltpu.PrefetchScalarGridSpec(
            num_scalar_prefetch=1, grid=(1,),
            in_specs=[pl.BlockSpec((rows, LANES), lambda i, k_ref: (0, 0))],
            out_specs=pl.BlockSpec((1, rows, LANES), lambda i, k_ref: (k_ref[0], 0, 0))),
        out_shape=jax.ShapeDtypeStruct((N_DEVICES, rows, LANES), F32),
        compiler_params=_params(dimension_semantics=("arbitrary",)),
    )(me, v)


def _small_plan():
    def plan(refs, sems):
        x, y, c, _ = _place()
        slots = refs[0]
        me = 4 * x + 2 * y + c
        copies = []
        for n, (fx, fy, fc) in enumerate(FLIPS):
            to = (x ^ fx, y ^ fy, c ^ fc)
            src = 4 * to[0] + 2 * to[1] + to[2]
            copies.append((_remote(slots.at[me], slots.at[me], sems, n, to), _remote(slots.at[src], slots.at[src], sems, n, to)))
        return copies

    return plan


def _small_sum(slots):
    def body(s_ref, o_ref):
        acc = s_ref[0]
        for s in range(1, N_DEVICES):
            acc = acc + s_ref[s]
        o_ref[...] = acc

    return pl.pallas_call(body, name="small_sum", out_shape=jax.ShapeDtypeStruct(slots.shape[1:], F32),
                          compiler_params=_params())(slots)


def _pair_add(gs, sibs, core, name):
    n = len(gs)
    halves = [_half_shape(*g.shape[1:]) for g in gs]

    def body(c_ref, *refs):
        for g_ref, s_ref, o_ref in zip(refs[:n], refs[n:2 * n], refs[2 * n:]):
            o_ref[0] = (g_ref[0].astype(F32) + s_ref[0].astype(F32)).astype(BF16)

    def mine(g, half):
        return pl.BlockSpec((1,) + half, lambda s, c_ref: (s, *_half_index(g.shape[1], c_ref[0])))

    whole = [pl.BlockSpec((1,) + half, lambda s, c_ref: (s, 0, 0)) for half in halves]
    return pl.pallas_call(
        body, name=name,
        grid_spec=pltpu.PrefetchScalarGridSpec(
            num_scalar_prefetch=1, grid=(N_CHIPS,),
            in_specs=[mine(g, half) for g, half in zip(gs, halves)] + whole, out_specs=whole),
        out_shape=[jax.ShapeDtypeStruct((N_CHIPS,) + half, BF16) for half in halves],
        compiler_params=_params(dimension_semantics=("arbitrary",)),
    )(core, *gs, *sibs)


def _chip_add(parts, recvs, chip_core, shapes, name):
    n = len(parts)
    halves = [_half_shape(*shape) for shape in shapes]

    def body(kc_ref, *refs):
        for p_ref, r_ref, o_ref in zip(refs[:n], refs[n:2 * n], refs[2 * n:]):
            acc = p_ref[0].astype(F32)
            for j in range(N_CHIPS - 1):
                acc = acc + r_ref[j].astype(F32)
            o_ref[...] = acc

    def out_spec(shape, half):
        return pl.BlockSpec(half, lambda s, kc_ref: _half_index(shape[0], kc_ref[1]))

    return pl.pallas_call(
        body, name=name,
        grid_spec=pltpu.PrefetchScalarGridSpec(
            num_scalar_prefetch=1, grid=(1,),
            in_specs=[pl.BlockSpec((1,) + half, lambda s, kc_ref: (kc_ref[0], 0, 0)) for half in halves]
            + [pl.BlockSpec((N_CHIPS - 1,) + half, lambda s, kc_ref: (0, 0, 0)) for half in halves],
            out_specs=[out_spec(shape, half) for shape, half in zip(shapes, halves)]),
        out_shape=[jax.ShapeDtypeStruct(tuple(shape), F32) for shape in shapes],
        compiler_params=_params(dimension_semantics=("arbitrary",)),
    )(chip_core, *parts, *recvs)


def _adamw_math(w, g, m, v):
    m = ADAM_B1 * m + (1.0 - ADAM_B1) * g
    v = ADAM_B2 * v + (1.0 - ADAM_B2) * (g * g)
    m_hat = m / (1.0 - ADAM_B1 ** ADAM_STEP)
    v_hat = v / (1.0 - ADAM_B2 ** ADAM_STEP)
    delta = -ADAM_LR * (m_hat / (jnp.sqrt(v_hat) + ADAM_EPS) + ADAM_WD * w)
    return delta, m, v


ADAM_PARTS = 4


def _adamw_matrix(w, g, m, v, name):
    rows, cols = w.shape
    by_rows = rows % (8 * ADAM_PARTS) == 0
    block = (rows // ADAM_PARTS, cols) if by_rows else (rows, cols // ADAM_PARTS)

    def body(w_ref, g_ref, m_ref, v_ref, go_ref, d_ref, mo_ref, vo_ref):
        gv = g_ref[...]
        go_ref[...] = gv
        d_ref[...], mo_ref[...], vo_ref[...] = _adamw_math(w_ref[...], gv, m_ref[...], v_ref[...])

    spec = pl.BlockSpec(block, lambda i: (i, 0) if by_rows else (0, i))
    shape = jax.ShapeDtypeStruct((rows, cols), F32)
    return pl.pallas_call(
        body, name=name, grid=(ADAM_PARTS,), in_specs=[spec] * 4, out_specs=[spec] * 4, out_shape=[shape] * 4,
        compiler_params=_params(dimension_semantics=("arbitrary",)),
    )(w, g, m, v)


def _adamw_small(ws, gs, ms, vs):
    n = len(ws)

    def body(*refs):
        for i in range(n):
            w_ref, g_ref, m_ref, v_ref = (refs[k * n + i] for k in range(4))
            d_ref, mo_ref, vo_ref = (refs[(4 + k) * n + i] for k in range(3))
            d_ref[...], mo_ref[...], vo_ref[...] = _adamw_math(w_ref[...], g_ref[...], m_ref[...], v_ref[...])

    shapes = [jax.ShapeDtypeStruct(w.shape, F32) for w in ws]
    out = pl.pallas_call(body, name="adamw_small", out_shape=shapes * 3, compiler_params=_params())(*ws, *gs, *ms, *vs)
    return out[:n], out[n:2 * n], out[2 * n:]


MATRICES = ["ffn1_w13", "ffn1_w2", "w_in", "w_out", "ffn2_w13", "ffn2_w2"]
VECTORS = ["ffn1_norm", "mix_norm", "conv_b", "conv_ln_g", "conv_ln_b", "forget_b", "out_norm_conv",
           "out_norm_attn", "ffn2_norm", "final_norm"]
WEIGHTS = ["ffn1_norm", "ffn1_w13", "ffn1_w2", "mix_norm", "w_in", "conv_w", "conv_b", "conv_ln_g", "conv_ln_b",
           "forget_b", "out_norm_conv", "out_norm_attn", "w_out", "ffn2_norm", "ffn2_w13", "ffn2_w2", "final_norm"]


def _pack_small(g, names):
    rows, layout = [], []
    for n in names:
        flat = g[n].reshape(-1)
        pad = (-flat.shape[0]) % LANES
        rows.append(jnp.pad(flat, (0, pad)).reshape(-1, LANES))
        layout.append((n, g[n].shape, flat.shape[0], rows[-1].shape[0]))
    packed = jnp.concatenate(rows, axis=0)
    pad_rows = (-packed.shape[0]) % 8
    return jnp.pad(packed, ((0, pad_rows), (0, 0))), layout


def _unpack_small(packed, layout):
    out, r = {}, 0
    for n, shape, size, nrows in layout:
        out[n] = packed[r:r + nrows].reshape(-1)[:size].reshape(shape)
        r += nrows
    return out


def kernel(x, ffn1_norm, ffn1_w13, ffn1_w2, mix_norm, w_in, conv_w, conv_b, conv_ln_g, conv_ln_b, forget_b, out_norm_conv, out_norm_attn, w_out, ffn2_norm, ffn2_w13, ffn2_w2, final_norm, loss_target, m_ffn1_norm, m_ffn1_w13, m_ffn1_w2, m_mix_norm, m_w_in, m_conv_w, m_conv_b, m_conv_ln_g, m_conv_ln_b, m_forget_b, m_out_norm_conv, m_out_norm_attn, m_w_out, m_ffn2_norm, m_ffn2_w13, m_ffn2_w2, m_final_norm, v_ffn1_norm, v_ffn1_w13, v_ffn1_w2, v_mix_norm, v_w_in, v_conv_w, v_conv_b, v_conv_ln_g, v_conv_ln_b, v_forget_b, v_out_norm_conv, v_out_norm_attn, v_w_out, v_ffn2_norm, v_ffn2_w13, v_ffn2_w2, v_final_norm):
    args = dict(locals())
    weights = {n: args[n] for n in WEIGHTS}
    core = lax.axis_index("c").astype(jnp.int32).reshape(1)
    chip = (2 * lax.axis_index("x") + lax.axis_index("y")).astype(jnp.int32)
    chip1 = chip.reshape(1)
    chip_core = jnp.concatenate([chip1, core])

    def held(n, a):
        return a[0].T if n == "w_in" else a[0]

    def given(n, a):
        return (a.T if n == "w_in" else a)[None]

    def slot(n, deps=()):
        if n == "conv_w":
            rows = jnp.pad(conv_w[0], ((0, CONV_PAD - CONV_WIDTH), (0, 0)))
            return _into_slot(rows, chip1, F32, "slot_conv_w", deps)
        return _into_slot(held(n, weights[n]), chip1, BF16, "slot_" + n, deps)

    fetched = {"ffn1_w13": ["ffn1_w13"], "ffn1_w2": ["ffn1_w2"], "mix": ["w_in", "w_out", "conv_w"],
               "ffn2": ["ffn2_w13", "ffn2_w2"]}
    fetch = {}

    def as_weights(group, bufs):
        out = {}
        for n, b in zip(fetched[group], bufs):
            if n.endswith("w13"):
                out[n] = b
            elif n != "conv_w":
                out[n] = b.reshape(N_CHIPS * b.shape[1], b.shape[2])
            else:
                out[n] = b[:, :CONV_WIDTH].transpose(1, 0, 2).reshape(CONV_WIDTH, D_CONV)
        return out

    def get_weights(group, after):
        if group == "ffn1_w13":
            first = [slot("ffn1_w13")]
            plan = _ici_gather_plan(first)
            started = _split_copy_start("gather_ffn1_w13_start", first, 3, plan)
            second = [slot("ffn1_w2", [started[3]])]
            plan2 = _ici_gather_plan(second)
            fetch["ffn1_w2"] = plan2, _split_copy_start("gather_ffn1_w2_start", second, 3, plan2)
            later_names = fetched["mix"] + fetched["ffn2"]
            later = [slot(n, [fetch["ffn1_w2"][1][3]]) for n in later_names]
            landed = _split_copy_wait("gather_ffn1_w13_wait", started, plan, [], passed=later)
            bufs = _forward_halves(landed[:1], "forward_ffn1_w13")
            behind = dict(zip(later_names, landed[1:]))
            for later in ("mix", "ffn2"):
                bufs_later = [behind[n] for n in fetched[later]]
                plan = _ici_gather_plan(bufs_later)
                fetch[later] = plan, _split_copy_start("gather_%s_start" % later, bufs_later, 3 * len(bufs_later), plan)
            return as_weights(group, bufs), [fetch["mix"][1][3], fetch["ffn2"][1][3]]
        plan, started = fetch[group.split(":")[0]]
        if group == "ffn2:landed":
            landed = _split_copy_wait("gather_ffn2_wait", started, plan, [after])
            plan = _d2d_forward_plan(landed)
            fetch["ffn2"] = plan, _split_copy_start("forward_ffn2_start", landed, 3 * len(landed), plan)
            return {}, [fetch["ffn2"][1][3]]
        if group == "ffn2":
            return as_weights(group, _split_copy_wait("forward_ffn2_wait", started, plan, [after])), []
        landed = _split_copy_wait("gather_%s_wait" % group, started, plan, [after])
        return as_weights(group, _forward_halves(landed, "forward_" + group)), []

    def shard_major(n, g):
        return g if n.endswith("w13") else g.reshape(N_CHIPS, g.shape[0] // N_CHIPS, g.shape[1])

    exchange, scatter = {}, {}
    small_names = VECTORS + ["conv_w"]
    small = {}

    def put_grads(group, grads):
        if group == "small":
            packed, layout = _pack_small(grads, small_names + ["loss"])
            me = (4 * lax.axis_index("x") + 2 * lax.axis_index("y") + lax.axis_index("c")).astype(jnp.int32).reshape(1)
            plan = _small_plan()
            exchange[group] = layout, plan, _split_copy_start("small_start", [_small_slots(packed, me)], len(FLIPS), plan)
            return [exchange[group][2][3]]
        names = list(grads)
        local = [shard_major(n, grads[n]) for n in names]
        landing = [lax.empty((N_CHIPS,) + _half_shape(*a.shape[1:]), BF16) for a in local]
        plan = _pair_exchange_plan(local)
        exchange[group] = names, plan, _split_copy_start("exchange_%s_start" % group, local + landing, len(local), plan)
        return [exchange[group][2][3]]

    def flush_grads(group, after):
        names, plan, started = exchange[group]
        done = _split_copy_wait("exchange_%s_wait" % group, started, plan, after)
        local, sib = done[:len(names)], done[len(names):]
        parts = list(_pair_add(local, sib, core, "pair_add_" + group))
        landing = [lax.empty((N_CHIPS - 1,) + q.shape[1:], BF16) for q in parts]
        plan = _ici_scatter_plan(len(parts))
        shapes = [a.shape[1:] for a in local]
        scatter[group] = names, plan, _split_copy_start("scatter_%s_start" % group, parts + landing, 3 * len(parts), plan), shapes
        return [scatter[group][2][3]]

    p = {n: weights[n] for n in VECTORS}
    p["final_norm"] = final_norm.reshape(1, D_MODEL)
    dx = _local_step(x[0], loss_target[0], p, get_weights, put_grads, flush_grads)
    layout, plan, started = exchange["small"]
    slots, = _split_copy_wait("small_wait", started, plan, [exchange["ffn1_w13"][2][3]])
    small.update(_unpack_small(_small_sum(slots), layout))
    loss = small["loss"].reshape(())

    grad = {n: small[n] for n in VECTORS}
    grad["final_norm"] = small["final_norm"].reshape(D_MODEL)
    grad["conv_w"] = lax.dynamic_slice_in_dim(small["conv_w"], chip * (D_CONV // N_CHIPS), D_CONV // N_CHIPS, axis=1)[None]

    delta, new_m, new_v = {}, {}, {}

    def reduce_chips(group, after):
        names, plan, started, shapes = scatter[group]
        done = _split_copy_wait("scatter_%s_wait" % group, started, plan, after)
        parts, landed = done[:len(names)], done[len(names):]
        return list(_chip_add(parts, landed, chip_core, shapes, "chip_add_" + group))

    def update(group, full):
        ends = []
        for n, reduced in zip(scatter[group][0], full):
            go, d, mo, vo = _adamw_matrix(held(n, weights[n]), reduced, held(n, args["m_" + n]), held(n, args["v_" + n]),
                                          "adamw_" + n)
            grad[n], delta[n], new_m[n], new_v[n] = given(n, go), given(n, d), given(n, mo), given(n, vo)
            ends.append(vo)
        return ends

    def share_start(group, halves):
        plan = _pair_share_plan(scatter[group][3])
        return plan, _split_copy_start("share_%s_start" % group, halves, len(halves), plan)

    halves_ffn2 = reduce_chips("ffn2", [exchange["ffn1_w13"][2][3]])
    plan_ffn2, share_ffn2 = share_start("ffn2", halves_ffn2)
    last_scatter = flush_grads("ffn1_w13", [share_ffn2[3]])
    halves_mix = reduce_chips("mix", last_scatter)
    plan_mix, share_mix = share_start("mix", halves_mix)
    done_ffn2 = update("ffn2", _split_copy_wait("share_ffn2_wait", share_ffn2, plan_ffn2, [share_mix[3]]))
    done_mix = update("mix", _split_copy_wait("share_mix_wait", share_mix, plan_mix, done_ffn2))
    as2d = lambda a: a.reshape(-1, a.shape[-1])
    ds, mos, vos = _adamw_small([as2d(weights[n]) for n in small_names], [as2d(grad[n]) for n in small_names],
                                [as2d(args["m_" + n]) for n in small_names], [as2d(args["v_" + n]) for n in small_names])
    for n, d, mo, vo in zip(small_names, ds, mos, vos):
        shape = weights[n].shape
        delta[n], new_m[n], new_v[n] = d.reshape(shape), mo.reshape(shape), vo.reshape(shape)
    behind = done_ffn2 + done_mix + [vos[0]]
    halves_w2 = reduce_chips("ffn1_w2", behind)
    halves_w13 = reduce_chips("ffn1_w13", behind)
    full_w2, full_w13 = _pair_share(halves_w2 + halves_w13, "pair_share_ffn1")
    update("ffn1_w2", [full_w2])
    update("ffn1_w13", [full_w13])

    return (loss, dx[None], *[grad[n] for n in WEIGHTS], *[delta[n] for n in WEIGHTS],
            *[new_m[n] for n in WEIGHTS], *[new_v[n] for n in WEIGHTS])
```

```python
import jax
import jax.numpy as jnp
from jax import lax
from jax.experimental import pallas as pl
from jax.experimental.pallas import tpu as pltpu

F32 = jnp.float32
BF16 = jnp.bfloat16

D_MODEL = 1024
D_FF = 2816
FF_SHARD = D_FF // 2
D_CONV = 512
D_ATTN = 512
N_HEADS = 8
HEAD_DIM = 64
CONV_WIDTH = 31
CONV_PAD = 32
N_IN = 2 * D_CONV + 3 * D_ATTN + N_HEADS
EPS = 1e-6
N_CHIPS = 4
LANES = 128
TOKEN_ROWS = 512
OUT_PROJ_ROWS = 256
HEAD_ROWS = 16

ADAM_LR = 0.001
ADAM_B1 = 0.9
ADAM_B2 = 0.999
ADAM_EPS = 1e-08
ADAM_WD = 0.01
ADAM_STEP = 10

VMEM_LIMIT = 56 * 1024 * 1024

_NT = (((1,), (1,)), ((), ()))
_TN = (((0,), (0,)), ((), ()))


def _dot(a, b):
    return jnp.dot(a, b, preferred_element_type=F32)


def _dot_nt(a, b):
    return lax.dot_general(a, b, _NT, preferred_element_type=F32)


def _dot_tn(a, b):
    return lax.dot_general(a, b, _TN, preferred_element_type=F32)


def _params(**kw):
    return pltpu.CompilerParams(vmem_limit_bytes=VMEM_LIMIT, **kw)


def _sigmoid(x):
    return 1.0 / (1.0 + jnp.exp(-x))


def _rms_stats(x):
    return lax.rsqrt(jnp.mean(x * x, axis=-1, keepdims=True) + EPS)


def _rms_bwd(x, r, g, dh):
    t = dh * g
    dx = r * t - x * (r * r * r) * jnp.mean(t * x, axis=-1, keepdims=True)
    return dx, dh * x * r


def _silu_grad(z, sg):
    return sg * (1.0 + z * (1.0 - sg))


def _row_spec(tm, n):
    return pl.BlockSpec((tm, n), lambda i: (i, 0))


def _full_spec(shape):
    nd = len(shape)
    return pl.BlockSpec(shape, lambda i: (0,) * nd)


_ANY = pl.BlockSpec(memory_space=pl.ANY)


def _skip(n, body):
    return lambda *refs: body(*refs[n:])


FFN_ROWS = 256
FFN_WEIGHT_PARTS = N_CHIPS + 2


def _with_ffn_weights(w13_hbm, w2_hbm, w13_ref, w2_ref, sems, order, tile):
    first = pl.program_id(0) == 0
    copies = {}
    if w13_hbm is not None:
        for k in range(N_CHIPS):
            copies["w13", k] = pltpu.make_async_copy(w13_hbm.at[k], w13_ref.at[k], sems.at[k])
    if w2_hbm is not None:
        for half in range(2):
            rows = pl.ds(half * FF_SHARD, FF_SHARD)
            copies["w2", half] = pltpu.make_async_copy(w2_hbm.at[rows, :], w2_ref.at[rows, :], sems.at[N_CHIPS + half])

    @pl.when(first)
    def _():
        for part in order:
            copies[part].start()

        def ready(*parts):
            for part in parts:
                copies[part].wait()

        tile(ready)

    @pl.when(jnp.logical_not(first))
    def _():
        tile(lambda *parts: None)


def _ffn_fwd(x, g, w13s, w2, name, deps=()):
    t = x.shape[0]
    tm = FFN_ROWS
    deps = tuple(deps)

    def body(x_ref, g_ref, w13_hbm, w2_hbm, xo_ref, h_ref, gu_ref, a_ref, w13_ref, w2_ref, sems):
        def tile(ready):
            xv = x_ref[...]
            hb = (xv * _rms_stats(xv) * g_ref[...]).astype(BF16)
            h_ref[...] = hb
            acc = jnp.zeros((tm, D_MODEL), F32)
            for half in range(2):
                lo = half * FF_SHARD
                ready(("w13", half), ("w13", 2 + half))
                gate = _dot(hb, w13_ref[half])
                up = _dot(hb, w13_ref[2 + half])
                gu_ref[:, lo:lo + FF_SHARD] = gate.astype(BF16)
                gu_ref[:, D_FF + lo:D_FF + lo + FF_SHARD] = up.astype(BF16)
                a = (gate * _sigmoid(gate) * up).astype(BF16)
                a_ref[:, lo:lo + FF_SHARD] = a
                ready(("w2", half))
                acc = acc + _dot(a, w2_ref[lo:lo + FF_SHARD, :])
            xo_ref[...] = xv + 0.5 * acc

        _with_ffn_weights(w13_hbm, w2_hbm, w13_ref, w2_ref, sems,
                          [("w13", 0), ("w13", 2), ("w2", 0), ("w13", 1), ("w13", 3), ("w2", 1)], tile)

    return pl.pallas_call(
        _skip(len(deps), body), name=name, grid=(t // tm,),
        in_specs=[_ANY] * len(deps) + [_row_spec(tm, D_MODEL), _full_spec((1, D_MODEL)), _ANY, _ANY],
        out_specs=[_row_spec(tm, D_MODEL), _row_spec(tm, D_MODEL), _row_spec(tm, 2 * D_FF), _row_spec(tm, D_FF)],
        out_shape=[jax.ShapeDtypeStruct((t, D_MODEL), F32), jax.ShapeDtypeStruct((t, D_MODEL), BF16),
                   jax.ShapeDtypeStruct((t, 2 * D_FF), BF16), jax.ShapeDtypeStruct((t, D_FF), BF16)],
        scratch_shapes=[pltpu.VMEM(w13s.shape, BF16), pltpu.VMEM(w2.shape, BF16),
                        pltpu.SemaphoreType.DMA((FFN_WEIGHT_PARTS,))],
        compiler_params=_params(dimension_semantics=("arbitrary",)),
    )(*deps, x, g, w13s, w2)


def _ffn_up(x, g, w13s, name, deps=()):
    t = x.shape[0]
    tm = FFN_ROWS
    deps = tuple(deps)

    def body(x_ref, g_ref, w13_hbm, h_ref, gu_ref, a_ref, w13_ref, sems):
        def tile(ready):
            xv = x_ref[...]
            hb = (xv * _rms_stats(xv) * g_ref[...]).astype(BF16)
            h_ref[...] = hb
            for half in range(2):
                lo = half * FF_SHARD
                ready(("w13", half), ("w13", 2 + half))
                gate = _dot(hb, w13_ref[half])
                up = _dot(hb, w13_ref[2 + half])
                gu_ref[:, lo:lo + FF_SHARD] = gate.astype(BF16)
                gu_ref[:, D_FF + lo:D_FF + lo + FF_SHARD] = up.astype(BF16)
                a_ref[:, lo:lo + FF_SHARD] = (gate * _sigmoid(gate) * up).astype(BF16)

        _with_ffn_weights(w13_hbm, None, w13_ref, None, sems, [("w13", 0), ("w13", 2), ("w13", 1), ("w13", 3)], tile)

    return pl.pallas_call(
        _skip(len(deps), body), name=name, grid=(t // tm,),
        in_specs=[_ANY] * len(deps) + [_row_spec(tm, D_MODEL), _full_spec((1, D_MODEL)), _ANY],
        out_specs=[_row_spec(tm, D_MODEL), _row_spec(tm, 2 * D_FF), _row_spec(tm, D_FF)],
        out_shape=[jax.ShapeDtypeStruct((t, D_MODEL), BF16), jax.ShapeDtypeStruct((t, 2 * D_FF), BF16),
                   jax.ShapeDtypeStruct((t, D_FF), BF16)],
        scratch_shapes=[pltpu.VMEM(w13s.shape, BF16), pltpu.SemaphoreType.DMA((FFN_WEIGHT_PARTS,))],
        compiler_params=_params(dimension_semantics=("arbitrary",)),
    )(*deps, x, g, w13s)


def _ffn_down(x, a, w2, name):
    t = x.shape[0]
    tm = FFN_ROWS

    def body(x_ref, a_ref, w2_hbm, xo_ref, w2_ref, sems):
        def tile(ready):
            ready(("w2", 0))
            acc = _dot(a_ref[:, 0:FF_SHARD], w2_ref[0:FF_SHARD, :])
            ready(("w2", 1))
            acc = acc + _dot(a_ref[:, FF_SHARD:], w2_ref[FF_SHARD:, :])
            xo_ref[...] = x_ref[...] + 0.5 * acc

        _with_ffn_weights(None, w2_hbm, None, w2_ref, sems, [("w2", 0), ("w2", 1)], tile)

    return pl.pallas_call(
        body, name=name, grid=(t // tm,),
        in_specs=[_row_spec(tm, D_MODEL), _row_spec(tm, D_FF), _ANY],
        out_specs=_row_spec(tm, D_MODEL), out_shape=jax.ShapeDtypeStruct((t, D_MODEL), F32),
        scratch_shapes=[pltpu.VMEM(w2.shape, BF16), pltpu.SemaphoreType.DMA((FFN_WEIGHT_PARTS,))],
        compiler_params=_params(dimension_semantics=("arbitrary",)),
    )(x, a, w2)


def _ffn_bwd(dy, x, gu, g, w13s, w2, name, deps=()):
    t = x.shape[0]
    tm = FFN_ROWS
    deps = tuple(deps)

    def body(dy_ref, x_ref, gu_ref, g_ref, w13_hbm, w2_hbm, dx_ref, dgu_ref, dg_ref, dyh_ref, dxb_ref,
             w13_ref, w2_ref, sems):
        @pl.when(pl.program_id(0) == 0)
        def _():
            dg_ref[...] = jnp.zeros_like(dg_ref)

        def tile(ready):
            dyv = dy_ref[...]
            dyh = (0.5 * dyv).astype(BF16)
            dyh_ref[...] = dyh
            dh = jnp.zeros((tm, D_MODEL), F32)
            for half in range(2):
                lo = half * FF_SHARD
                ready(("w2", half))
                da = _dot_nt(dyh, w2_ref[lo:lo + FF_SHARD, :])
                gate = gu_ref[:, lo:lo + FF_SHARD].astype(F32)
                up = gu_ref[:, D_FF + lo:D_FF + lo + FF_SHARD].astype(F32)
                sg = _sigmoid(gate)
                act = gate * sg
                dgate = (da * up * _silu_grad(gate, sg)).astype(BF16)
                dup = (da * act).astype(BF16)
                dgu_ref[:, lo:lo + FF_SHARD] = dgate
                dgu_ref[:, D_FF + lo:D_FF + lo + FF_SHARD] = dup
                ready(("w13", half), ("w13", 2 + half))
                dh = dh + _dot_nt(dgate, w13_ref[half]) + _dot_nt(dup, w13_ref[2 + half])
            xv = x_ref[...]
            dxn, dg_rows = _rms_bwd(xv, _rms_stats(xv), g_ref[...], dh)
            dx = dyv + dxn
            dx_ref[...] = dx
            dxb_ref[...] = dx.astype(BF16)
            dg_ref[...] += jnp.sum(dg_rows, axis=0, keepdims=True)

        _with_ffn_weights(w13_hbm, w2_hbm, w13_ref, w2_ref, sems,
                          [("w2", 0), ("w13", 0), ("w13", 2), ("w2", 1), ("w13", 1), ("w13", 3)], tile)

    return pl.pallas_call(
        _skip(len(deps), body), name=name, grid=(t // tm,),
        in_specs=[_ANY] * len(deps) + [_row_spec(tm, D_MODEL), _row_spec(tm, D_MODEL), _row_spec(tm, 2 * D_FF),
                                       _full_spec((1, D_MODEL)), _ANY, _ANY],
        out_specs=[_row_spec(tm, D_MODEL), _row_spec(tm, 2 * D_FF),
                   _full_spec((1, D_MODEL)), _row_spec(tm, D_MODEL), _row_spec(tm, D_MODEL)],
        out_shape=[jax.ShapeDtypeStruct((t, D_MODEL), F32), jax.ShapeDtypeStruct((t, 2 * D_FF), BF16),
                   jax.ShapeDtypeStruct((1, D_MODEL), F32),
                   jax.ShapeDtypeStruct((t, D_MODEL), BF16), jax.ShapeDtypeStruct((t, D_MODEL), BF16)],
        scratch_shapes=[pltpu.VMEM(w13s.shape, BF16), pltpu.VMEM(w2.shape, BF16),
                        pltpu.SemaphoreType.DMA((FFN_WEIGHT_PARTS,))],
        compiler_params=_params(dimension_semantics=("arbitrary",)),
    )(*deps, dy, x, gu, g, w13s, w2)


WGRAD_ROWS = (1024, 512, 384, 256)


def _wgrad(a, b, n_blocks, name, deps=()):
    t, m = a.shape
    tm = next(rows for rows in WGRAD_ROWS if m % rows == 0)
    n = b.shape[1]
    bn = n // n_blocks
    deps = tuple(deps)
    assert a.dtype == BF16 and b.dtype == BF16

    def body(a_ref, b_ref, o_ref):
        o_ref[0] = _dot_tn(a_ref[...], b_ref[...]).astype(BF16)

    return pl.pallas_call(
        _skip(len(deps), body), name=name, grid=(n_blocks, m // tm),
        in_specs=[_ANY] * len(deps) + [pl.BlockSpec((t, tm), lambda j, i: (0, i)),
                                       pl.BlockSpec((t, bn), lambda j, i: (0, j))],
        out_specs=pl.BlockSpec((1, tm, bn), lambda j, i: (j, i, 0)),
        out_shape=jax.ShapeDtypeStruct((n_blocks, m, bn), BF16),
        compiler_params=_params(dimension_semantics=("arbitrary", "arbitrary")),
    )(*deps, a, b)


def _mix_proj(x, g, w_ag, w_qkv, w_f):
    t = x.shape[0]
    tm = TOKEN_ROWS

    def body(x_ref, g_ref, wag_ref, wqkv_ref, wf_ref, h_ref, ag_ref, qkv_ref, fl_ref):
        xv = x_ref[...]
        hb = (xv * _rms_stats(xv) * g_ref[...]).astype(BF16)
        h_ref[...] = hb
        ag_ref[...] = _dot_nt(hb, wag_ref[...])
        qkv_ref[...] = _dot_nt(hb, wqkv_ref[...]).astype(BF16)
        fl_ref[...] = _dot_nt(hb, wf_ref[...])

    return pl.pallas_call(
        body, name="mix_proj", grid=(t // tm,),
        in_specs=[_row_spec(tm, D_MODEL), _full_spec((1, D_MODEL)), _full_spec(w_ag.shape),
                  _full_spec(w_qkv.shape), _full_spec(w_f.shape)],
        out_specs=[_row_spec(tm, D_MODEL), _row_spec(tm, 2 * D_CONV), _row_spec(tm, 3 * D_ATTN),
                   _row_spec(tm, LANES)],
        out_shape=[jax.ShapeDtypeStruct((t, D_MODEL), BF16), jax.ShapeDtypeStruct((t, 2 * D_CONV), F32),
                   jax.ShapeDtypeStruct((t, 3 * D_ATTN), BF16), jax.ShapeDtypeStruct((t, LANES), F32)],
        compiler_params=_params(dimension_semantics=("arbitrary",)),
    )(x, g, w_ag, w_qkv, w_f)


def _mix_proj_bwd(dproj, dx2, x1, g, w_ag, w_qkv, w_f):
    t = x1.shape[0]
    tm = TOKEN_ROWS
    n_ag, n_qkv = 2 * D_CONV, 3 * D_ATTN

    def body(dp_ref, dx2_ref, x_ref, g_ref, wag_ref, wqkv_ref, wf_ref, dx_ref, dg_ref):
        @pl.when(pl.program_id(0) == 0)
        def _():
            dg_ref[...] = jnp.zeros_like(dg_ref)

        dh = (_dot(dp_ref[:, 0:n_ag], wag_ref[...]) + _dot(dp_ref[:, n_ag:n_ag + n_qkv], wqkv_ref[...])
              + _dot(dp_ref[:, n_ag + n_qkv:], wf_ref[...]))
        xv = x_ref[...]
        dxn, dg_rows = _rms_bwd(xv, _rms_stats(xv), g_ref[...], dh)
        dx_ref[...] = dx2_ref[...] + dxn
        dg_ref[...] += jnp.sum(dg_rows, axis=0, keepdims=True)

    return pl.pallas_call(
        body, name="mix_proj_bwd", grid=(t // tm,),
        in_specs=[_row_spec(tm, dproj.shape[1]),
                  _row_spec(tm, D_MODEL), _row_spec(tm, D_MODEL), _full_spec((1, D_MODEL)),
                  _full_spec(w_ag.shape), _full_spec(w_qkv.shape), _full_spec(w_f.shape)],
        out_specs=[_row_spec(tm, D_MODEL), _full_spec((1, D_MODEL))],
        out_shape=[jax.ShapeDtypeStruct((t, D_MODEL), F32), jax.ShapeDtypeStruct((1, D_MODEL), F32)],
        compiler_params=_params(dimension_semantics=("arbitrary",)),
    )(dproj, dx2, x1, g, w_ag, w_qkv, w_f)


def _split3(x):
    hi = x.astype(BF16)
    r1 = x - hi.astype(F32)
    mid = r1.astype(BF16)
    lo = (r1 - mid.astype(F32)).astype(BF16)
    return hi, mid, lo


def _gates_fwd(flt, fb):
    t = flt.shape[1]

    def body(f_ref, b_ref, d_ref):
        z = f_ref[...] + b_ref[...]
        logf = jnp.minimum(z, 0.0) - jnp.log(1.0 + jnp.exp(-jnp.abs(z)))
        row = lax.broadcasted_iota(jnp.int32, (LANES, LANES), 0)
        col = lax.broadcasted_iota(jnp.int32, (LANES, LANES), 1)
        upper = (row <= col).astype(BF16)
        carry = jnp.zeros((HEAD_ROWS, 1), F32)
        for blk in range(t // LANES):
            hi, mid, lo = _split3(logf[:, blk * LANES:(blk + 1) * LANES])
            cs = _dot(hi, upper) + _dot(mid, upper) + _dot(lo, upper)
            d_ref[:, blk * LANES:(blk + 1) * LANES] = cs + carry
            carry = carry + cs[:, LANES - 1:LANES]

    return pl.pallas_call(
        body, name="gates_fwd", out_shape=jax.ShapeDtypeStruct((HEAD_ROWS, t), F32),
        compiler_params=_params(),
    )(flt, fb)


def _gates_bwd(dd, flt, fb):
    t = flt.shape[1]

    def body(dd_ref, f_ref, b_ref, df_ref, db_ref):
        z = f_ref[...] + b_ref[...]
        row = lax.broadcasted_iota(jnp.int32, (LANES, LANES), 0)
        col = lax.broadcasted_iota(jnp.int32, (LANES, LANES), 1)
        lower = (row >= col).astype(BF16)
        carry = jnp.zeros((HEAD_ROWS, 1), F32)
        db = jnp.zeros((HEAD_ROWS, 1), F32)
        for blk in reversed(range(t // LANES)):
            sl = slice(blk * LANES, (blk + 1) * LANES)
            hi, mid, lo = _split3(dd_ref[:, sl])
            cs = _dot(hi, lower) + _dot(mid, lower) + _dot(lo, lower)
            dz = (cs + carry) * _sigmoid(-z[:, sl])
            df_ref[:, sl] = dz
            db = db + jnp.sum(dz, axis=1, keepdims=True)
            carry = carry + cs[:, 0:1]
        db_ref[...] = db

    return pl.pallas_call(
        body, name="gates_bwd",
        out_shape=[jax.ShapeDtypeStruct((HEAD_ROWS, t), F32), jax.ShapeDtypeStruct((HEAD_ROWS, 1), F32)],
        compiler_params=_params(),
    )(dd, flt, fb)


CONV_CHUNK = 128
CONV_TAIL = 16
CONV_WINDOW = CONV_CHUNK + CONV_PAD + 8
CONV_ROWS_EXTRA = CONV_PAD + CONV_TAIL
SUBLANES = 8


def _conv_rows(ag_ref, u_ref, t):
    u_ref[0:CONV_PAD, :] = jnp.zeros((CONV_PAD, D_CONV), F32)
    u_ref[CONV_PAD + t:CONV_ROWS_EXTRA + t, :] = jnp.zeros((CONV_TAIL, D_CONV), F32)

    def fill(i, c):
        r0 = pl.multiple_of(i * CONV_CHUNK, CONV_CHUNK)
        a = ag_ref[pl.ds(r0, CONV_CHUNK), 0:D_CONV]
        gt = ag_ref[pl.ds(r0, CONV_CHUNK), D_CONV:2 * D_CONV]
        u_ref[pl.ds(CONV_PAD + r0, CONV_CHUNK), :] = a * _sigmoid(gt)
        return c

    lax.fori_loop(0, t // CONV_CHUNK, fill, 0)


def _for_shifted(ref, r0, offsets, fn):
    window = ref[pl.ds(r0, CONV_WINDOW), :]
    for rem in range(SUBLANES):
        mine = [o for o in offsets if o % SUBLANES == rem]
        if not mine:
            continue
        turned = window if rem == 0 else pltpu.roll(window, CONV_WINDOW - rem, 0)
        for o in mine:
            fn(o, turned[o - rem:o - rem + CONV_CHUNK])


def _conv_taps(u_ref, r0, w_ref, cb):
    acc = [jnp.zeros((CONV_CHUNK, D_CONV), F32)]

    def tap(o, rows):
        j = o - (CONV_PAD - CONV_WIDTH + 1)
        acc[0] = acc[0] + w_ref[j:j + 1, :] * rows

    _for_shifted(u_ref, r0, [j + CONV_PAD - CONV_WIDTH + 1 for j in range(CONV_WIDTH)], tap)
    return acc[0] + cb


def _conv_point(y, lg, lb):
    mu = jnp.mean(y, axis=-1, keepdims=True)
    yc = y - mu
    rstd = lax.rsqrt(jnp.mean(yc * yc, axis=-1, keepdims=True) + EPS)
    yhat = yc * rstd
    z = yhat * lg + lb
    sg = _sigmoid(z)
    s = z * sg
    rr = _rms_stats(s)
    return yhat, rstd, z, sg, s, rr


def _conv_fwd(ag, conv_w, conv_b, ln_g, ln_b, norm_g):
    t = ag.shape[0]

    def body(ag_ref, w_ref, cb_ref, lg_ref, lb_ref, ng_ref, o_ref, y_ref, u_ref):
        _conv_rows(ag_ref, u_ref, t)
        cb, lg, lb, ng = cb_ref[...], lg_ref[...], lb_ref[...], ng_ref[...]

        def chunk(i, c):
            r0 = pl.multiple_of(i * CONV_CHUNK, CONV_CHUNK)
            y = _conv_taps(u_ref, r0, w_ref, cb)
            y_ref[pl.ds(r0, CONV_CHUNK), :] = y
            _, _, _, _, s, rr = _conv_point(y, lg, lb)
            o_ref[pl.ds(r0, CONV_CHUNK), :] = (s * rr * ng).astype(BF16)
            return c

        lax.fori_loop(0, t // CONV_CHUNK, chunk, 0)

    return pl.pallas_call(
        body, name="conv_fwd",
        out_shape=[jax.ShapeDtypeStruct((t, D_CONV), BF16), jax.ShapeDtypeStruct((t, D_CONV), F32)],
        scratch_shapes=[pltpu.VMEM((t + CONV_ROWS_EXTRA, D_CONV), F32)],
        compiler_params=_params(),
    )(ag, conv_w, conv_b, ln_g, ln_b, norm_g)


def _conv_bwd(ag, y, dout, conv_w, ln_g, ln_b, norm_g):
    t = ag.shape[0]

    def body(ag_ref, y_ref, do_ref, w_ref, lg_ref, lb_ref, ng_ref,
             dag_ref, dw_ref, dcb_ref, dlg_ref, dlb_ref, dng_ref, u_ref, dy_ref):
        _conv_rows(ag_ref, u_ref, t)
        dy_ref[t:t + CONV_ROWS_EXTRA, :] = jnp.zeros((CONV_ROWS_EXTRA, D_CONV), F32)
        lg, lb, ng = lg_ref[...], lb_ref[...], ng_ref[...]
        dw_ref[...] = jnp.zeros_like(dw_ref)
        zero = jnp.zeros((1, D_CONV), F32)

        def chunk(i, carry):
            dcb, dlg, dlb, dng = carry
            r0 = pl.multiple_of(i * CONV_CHUNK, CONV_CHUNK)
            yhat, rstd, z, sg, s, rr = _conv_point(y_ref[pl.ds(r0, CONV_CHUNK), :], lg, lb)
            do = do_ref[pl.ds(r0, CONV_CHUNK), :]
            ds, dng_rows = _rms_bwd(s, rr, ng, do)
            dz = ds * _silu_grad(z, sg)
            dyhat = dz * lg
            dy = rstd * (dyhat - jnp.mean(dyhat, axis=-1, keepdims=True)
                         - yhat * jnp.mean(dyhat * yhat, axis=-1, keepdims=True))
            dy_ref[pl.ds(r0, CONV_CHUNK), :] = dy
            def tap(o, rows):
                j = o - (CONV_PAD - CONV_WIDTH + 1)
                dw_ref[j:j + 1, :] += jnp.sum(dy * rows, axis=0, keepdims=True)

            _for_shifted(u_ref, r0, [j + CONV_PAD - CONV_WIDTH + 1 for j in range(CONV_WIDTH)], tap)
            return (dcb + jnp.sum(dy, axis=0, keepdims=True), dlg + jnp.sum(dz * yhat, axis=0, keepdims=True),
                    dlb + jnp.sum(dz, axis=0, keepdims=True), dng + jnp.sum(dng_rows, axis=0, keepdims=True))

        dcb, dlg, dlb, dng = lax.fori_loop(0, t // CONV_CHUNK, chunk, (zero, zero, zero, zero))
        dcb_ref[...] = dcb
        dlg_ref[...] = dlg
        dlb_ref[...] = dlb
        dng_ref[...] = dng

        def chunk2(i, c):
            r0 = pl.multiple_of(i * CONV_CHUNK, CONV_CHUNK)
            acc = [jnp.zeros((CONV_CHUNK, D_CONV), F32)]

            def tap(o, rows):
                j = CONV_WIDTH - 1 - o
                acc[0] = acc[0] + w_ref[j:j + 1, :] * rows

            _for_shifted(dy_ref, r0, list(range(CONV_WIDTH)), tap)
            du = acc[0]
            a = ag_ref[pl.ds(r0, CONV_CHUNK), 0:D_CONV]
            gt = ag_ref[pl.ds(r0, CONV_CHUNK), D_CONV:2 * D_CONV]
            sg = _sigmoid(gt)
            dag_ref[pl.ds(r0, CONV_CHUNK), 0:D_CONV] = (du * sg).astype(BF16)
            dag_ref[pl.ds(r0, CONV_CHUNK), D_CONV:2 * D_CONV] = (du * a * sg * (1.0 - sg)).astype(BF16)
            return c

        lax.fori_loop(0, t // CONV_CHUNK, chunk2, 0)

    vec = jax.ShapeDtypeStruct((1, D_CONV), F32)
    return pl.pallas_call(
        body, name="conv_bwd",
        out_shape=[jax.ShapeDtypeStruct((t, 2 * D_CONV), BF16), jax.ShapeDtypeStruct((CONV_PAD, D_CONV), F32),
                   vec, vec, vec, vec],
        scratch_shapes=[pltpu.VMEM((t + CONV_ROWS_EXTRA, D_CONV), F32), pltpu.VMEM((t + CONV_ROWS_EXTRA, D_CONV), F32)],
        compiler_params=_params(),
    )(ag, y, dout, conv_w, ln_g, ln_b, norm_g)


Q_ROWS = 256
ATTN_SCALE = HEAD_DIM ** -0.5
ATTN_AHEAD = 1


def _attn_specs(t):
    blk = lambda off: pl.BlockSpec((t, LANES), lambda p: (0, off + p))
    pairs = N_HEADS // 2
    return [blk(0), blk(pairs), blk(2 * pairs), pl.BlockSpec((2, 1, t), lambda p: (p, 0, 0))]


def _one_head(q2, mask):
    return jnp.where(mask, q2, jnp.zeros_like(q2)) * ATTN_SCALE


def _attn_scores(qs, k2, drow, r0, q1):
    s = _dot_nt(qs, k2) - drow
    rowi = lax.broadcasted_iota(jnp.int32, (q1 - r0, q1 - r0), 0)
    coli = lax.broadcasted_iota(jnp.int32, (q1 - r0, q1 - r0), 1)
    diag = jnp.where(coli <= rowi, s[:, r0:q1], -jnp.inf)
    return diag if r0 == 0 else jnp.concatenate([s[:, :r0], diag], axis=1)


def _attn_fwd(qkv, drow, deps=()):
    t = qkv.shape[0]
    deps = tuple(deps)

    def body(q_ref, k_ref, v_ref, dr_ref, o_ref, lse_ref):
        head_a = lax.broadcasted_iota(jnp.int32, (1, LANES), 1) < HEAD_DIM
        items = [(qb, hh) for qb in range(t // Q_ROWS) for hh in range(2)]

        def scores(item):
            qb, hh = item
            r0, q1 = qb * Q_ROWS, (qb + 1) * Q_ROWS
            qs = _one_head(q_ref[r0:q1, :], head_a if hh == 0 else ~head_a)
            return _attn_scores(qs, k_ref[0:q1, :], dr_ref[hh, :, 0:q1], r0, q1)

        ahead = [scores(item) for item in items[:ATTN_AHEAD]]
        outs = []
        for n, (qb, hh) in enumerate(items):
            r0, q1 = qb * Q_ROWS, (qb + 1) * Q_ROWS
            s = ahead.pop(0)
            if n + ATTN_AHEAD < len(items):
                ahead.append(scores(items[n + ATTN_AHEAD]))
            mx = jnp.max(s, axis=1, keepdims=True)
            p = jnp.exp(s - mx)
            l = jnp.sum(p, axis=1, keepdims=True)
            lse_ref[hh, r0:q1, :] = mx + jnp.log(l)
            outs.append(_dot(p.astype(BF16), v_ref[0:q1, :]) * (1.0 / l))
            if hh == 1:
                o_ref[r0:q1, :] = jnp.where(head_a, outs[0], outs[1])
                outs = []

    pairs = N_HEADS // 2
    return pl.pallas_call(
        _skip(len(deps), body), name="attn_fwd", grid=(pairs,), in_specs=[_ANY] * len(deps) + _attn_specs(t),
        out_specs=[pl.BlockSpec((t, LANES), lambda p: (0, p)), pl.BlockSpec((2, t, 1), lambda p: (p, 0, 0))],
        out_shape=[jax.ShapeDtypeStruct((t, D_ATTN), F32), jax.ShapeDtypeStruct((N_HEADS, t, 1), F32)],
        compiler_params=_params(dimension_semantics=("arbitrary",)),
    )(*deps, qkv, qkv, qkv, drow)


def _attn_bwd(qkv, drow, lse, do):
    t = qkv.shape[0]

    def body(q_ref, k_ref, v_ref, dr_ref, lse_ref, do_ref,
             dq_ref, dk_ref, dv_ref, dd_ref, dk_acc, dv_acc):
        head_a = lax.broadcasted_iota(jnp.int32, (1, LANES), 1) < HEAD_DIM
        dk_acc[...] = jnp.zeros_like(dk_acc)
        dv_acc[...] = jnp.zeros_like(dv_acc)
        dd_ref[...] = jnp.zeros_like(dd_ref)
        items = [(qb, hh) for qb in range(t // Q_ROWS) for hh in range(2)]

        def products(item):
            qb, hh = item
            r0, q1 = qb * Q_ROWS, (qb + 1) * Q_ROWS
            mask = head_a if hh == 0 else ~head_a
            qs = _one_head(q_ref[r0:q1, :], mask)
            dob = jnp.where(mask, do_ref[r0:q1, :], 0.0).astype(BF16)
            s = _attn_scores(qs, k_ref[0:q1, :], dr_ref[hh, :, 0:q1], r0, q1)
            return qs, dob, s, _dot_nt(dob, v_ref[0:q1, :])

        ahead = products(items[0])
        dqs = []
        for n, (qb, hh) in enumerate(items):
            r0, q1 = qb * Q_ROWS, (qb + 1) * Q_ROWS
            qs, dob, s, dp = ahead
            if n + 1 < len(items):
                ahead = products(items[n + 1])
            p = jnp.exp(s - lse_ref[hh, r0:q1, :])
            ds = p * (dp - jnp.sum(p * dp, axis=1, keepdims=True))
            dsb = ds.astype(BF16)
            dqs.append(_dot(dsb, k_ref[0:q1, :]) * ATTN_SCALE)
            dk_acc[0:q1, :] += _dot_tn(dsb, qs)
            dv_acc[0:q1, :] += _dot_tn(p.astype(BF16), dob)
            dd_ref[hh, :, 0:q1] -= jnp.sum(ds, axis=0, keepdims=True)
            if hh == 1:
                dq_ref[r0:q1, :] = jnp.where(head_a, dqs[0], dqs[1]).astype(BF16)
                dqs = []
        dk_ref[...] = dk_acc[...].astype(BF16)
        dv_ref[...] = dv_acc[...].astype(BF16)

    pairs = N_HEADS // 2
    col = pl.BlockSpec((t, LANES), lambda p: (0, p))
    grad = jax.ShapeDtypeStruct((t, D_ATTN), BF16)
    return pl.pallas_call(
        body, name="attn_bwd", grid=(pairs,),
        in_specs=_attn_specs(t) + [pl.BlockSpec((2, t, 1), lambda p: (p, 0, 0)), col],
        out_specs=[col, col, col, pl.BlockSpec((2, 1, t), lambda p: (p, 0, 0))],
        out_shape=[grad, grad, grad, jax.ShapeDtypeStruct((N_HEADS, 1, t), F32)],
        scratch_shapes=[pltpu.VMEM((t, LANES), F32), pltpu.VMEM((t, LANES), F32)],
        compiler_params=_params(dimension_semantics=("arbitrary",)),
    )(qkv, qkv, qkv, drow, lse, do)


def _out_proj(ycn, o, g_attn, w_out, x1, deps=()):
    t = x1.shape[0]
    tm = OUT_PROJ_ROWS
    deps = tuple(deps)

    def body(yc_ref, o_ref, g_ref, w_ref, x_ref, xo_ref, ya_ref):
        ov = o_ref[...]
        ya = (ov * _rms_stats(ov) * g_ref[...]).astype(BF16)
        ya_ref[...] = ya
        xo_ref[...] = x_ref[...] + _dot(yc_ref[...], w_ref[0:D_CONV, :]) + _dot(ya, w_ref[D_CONV:, :])

    return pl.pallas_call(
        _skip(len(deps), body), name="out_proj", grid=(t // tm,),
        in_specs=[_ANY] * len(deps) + [_row_spec(tm, D_CONV), _row_spec(tm, D_ATTN), _full_spec((1, D_ATTN)),
                                       _full_spec(w_out.shape), _row_spec(tm, D_MODEL)],
        out_specs=[_row_spec(tm, D_MODEL), _row_spec(tm, D_ATTN)],
        out_shape=[jax.ShapeDtypeStruct((t, D_MODEL), F32), jax.ShapeDtypeStruct((t, D_ATTN), BF16)],
        compiler_params=_params(dimension_semantics=("arbitrary",)),
    )(*deps, ycn, o, g_attn, w_out, x1)


def _out_proj_bwd(dx2, o, g_attn, w_out, deps=()):
    t = dx2.shape[0]
    tm = TOKEN_ROWS
    deps = tuple(deps)

    def body(dx_ref, o_ref, g_ref, w_ref, dyc_ref, do_ref, dg_ref):
        @pl.when(pl.program_id(0) == 0)
        def _():
            dg_ref[...] = jnp.zeros_like(dg_ref)

        dxb = dx_ref[...]
        dyc_ref[...] = _dot_nt(dxb, w_ref[0:D_CONV, :])
        dya = _dot_nt(dxb, w_ref[D_CONV:, :])
        ov = o_ref[...]
        do, dg_rows = _rms_bwd(ov, _rms_stats(ov), g_ref[...], dya)
        do_ref[...] = do
        dg_ref[...] += jnp.sum(dg_rows, axis=0, keepdims=True)

    return pl.pallas_call(
        _skip(len(deps), body), name="out_proj_bwd", grid=(t // tm,),
        in_specs=[_ANY] * len(deps) + [_row_spec(tm, D_MODEL), _row_spec(tm, D_ATTN), _full_spec((1, D_ATTN)),
                                       _full_spec(w_out.shape)],
        out_specs=[_row_spec(tm, D_CONV), _row_spec(tm, D_ATTN), _full_spec((1, D_ATTN))],
        out_shape=[jax.ShapeDtypeStruct((t, D_CONV), F32), jax.ShapeDtypeStruct((t, D_ATTN), F32),
                   jax.ShapeDtypeStruct((1, D_ATTN), F32)],
        compiler_params=_params(dimension_semantics=("arbitrary",)),
    )(*deps, dx2, o, g_attn, w_out)


def _loss_bwd(x3, target, g):
    t = x3.shape[0]
    tm = TOKEN_ROWS

    def body(x_ref, t_ref, g_ref, loss_ref, dx_ref, dg_ref):
        @pl.when(pl.program_id(0) == 0)
        def _():
            loss_ref[...] = jnp.zeros_like(loss_ref)
            dg_ref[...] = jnp.zeros_like(dg_ref)

        xv = x_ref[...]
        r = _rms_stats(xv)
        gv = g_ref[...]
        err = xv * r * gv - t_ref[...]
        row = jnp.sum(err * err, axis=1, keepdims=True) * (0.5 / D_MODEL)
        loss_ref[...] += jnp.sum(row, axis=0, keepdims=True)
        dx, dg_rows = _rms_bwd(xv, r, gv, err * (1.0 / D_MODEL))
        dx_ref[...] = dx
        dg_ref[...] += jnp.sum(dg_rows, axis=0, keepdims=True)

    return pl.pallas_call(
        body, name="loss_bwd", grid=(t // tm,),
        in_specs=[_row_spec(tm, D_MODEL), _row_spec(tm, D_MODEL), _full_spec((1, D_MODEL))],
        out_specs=[_full_spec((1, LANES)), _row_spec(tm, D_MODEL), _full_spec((1, D_MODEL))],
        out_shape=[jax.ShapeDtypeStruct((1, LANES), F32), jax.ShapeDtypeStruct((t, D_MODEL), F32),
                   jax.ShapeDtypeStruct((1, D_MODEL), F32)],
        compiler_params=_params(dimension_semantics=("arbitrary",)),
    )(x3, target, g)


def _split_w_in(w_in_t):
    w_ag = w_in_t[:2 * D_CONV]
    w_qkv = w_in_t[2 * D_CONV:2 * D_CONV + 3 * D_ATTN]
    w_f = jnp.pad(w_in_t[2 * D_CONV + 3 * D_ATTN:], ((0, LANES - N_HEADS), (0, 0)))
    return w_ag, w_qkv, w_f


def _head_rows(v):
    return jnp.pad(v, ((0, HEAD_ROWS - N_HEADS),) + ((0, 0),) * (v.ndim - 1))


def _local_step(x, target, p, get_weights, put_grads, flush_grads):
    t = x.shape[0]
    fb = _head_rows(p["forget_b"].reshape(N_HEADS, 1))

    w, deps = get_weights("ffn1_w13", None)
    h1, gu1, act1 = _ffn_up(x, p["ffn1_norm"], w["ffn1_w13"], "ffn1_up", deps)
    w2, _ = get_weights("ffn1_w2", act1)
    w.update(w2)
    x1 = _ffn_down(x, act1, w["ffn1_w2"], "ffn1_down")
    wm, _ = get_weights("mix", x1)
    w.update(wm)
    w_ag, w_qkv, w_f = _split_w_in(w["w_in"])
    conv_w = jnp.pad(w["conv_w"], ((0, CONV_PAD - CONV_WIDTH), (0, 0)))
    h2, ag, qkv, fl = _mix_proj(x1, p["mix_norm"], w_ag, w_qkv, w_f)
    flt = _head_rows(fl[:, :N_HEADS].T)
    dcum = _gates_fwd(flt, fb)[:N_HEADS]
    drow = dcum.reshape(N_HEADS, 1, t)
    ycn, y_conv = _conv_fwd(ag, conv_w, p["conv_b"], p["conv_ln_g"], p["conv_ln_b"], p["out_norm_conv"])
    o, lse = _attn_fwd(qkv, drow, [ycn])
    _, deps = get_weights("ffn2:landed", o)
    x2, yan = _out_proj(ycn, o, p["out_norm_attn"], w["w_out"], x1, deps)
    w2, _ = get_weights("ffn2", x2)
    w.update(w2)
    x3, h3, gu2, act2 = _ffn_fwd(x2, p["ffn2_norm"], w["ffn2_w13"], w["ffn2_w2"], "ffn2_fwd")
    loss, dx3, d_final = _loss_bwd(x3, target, p["final_norm"])

    g = {}
    dx2, dgu2, g["ffn2_norm"], dx3_half, dx2_bf16 = _ffn_bwd(
        dx3, x2, gu2, p["ffn2_norm"], w["ffn2_w13"], w["ffn2_w2"], "ffn2_bwd")
    dw13 = _wgrad(h3, dgu2, N_CHIPS, "ffn2_dw13")
    dw2 = _wgrad(act2, dx3_half, 1, "ffn2_dw2").reshape(D_FF, D_MODEL)
    deps = put_grads("ffn2", {"ffn2_w13": dw13, "ffn2_w2": dw2})
    dyc, do, g["out_norm_attn"] = _out_proj_bwd(dx2_bf16, o, p["out_norm_attn"], w["w_out"], deps)
    deps = flush_grads("ffn2", [dyc])
    dw_out = _wgrad(jnp.concatenate([ycn, yan], axis=1), dx2_bf16, 1, "dw_out", deps).reshape(D_MODEL, D_MODEL)
    dq, dk, dv, ddrow = _attn_bwd(qkv, drow, lse, do)
    dflt, dfb = _gates_bwd(_head_rows(ddrow.reshape(N_HEADS, t)), flt, fb)
    g["forget_b"] = dfb[:N_HEADS, 0].reshape(1, N_HEADS)
    dfl = jnp.pad(dflt[:N_HEADS].T, ((0, 0), (0, LANES - N_HEADS)))
    dag, dconv_w, g["conv_b"], g["conv_ln_g"], g["conv_ln_b"], g["out_norm_conv"] = _conv_bwd(
        ag, y_conv, dyc, conv_w, p["conv_ln_g"], p["conv_ln_b"], p["out_norm_conv"])
    g["conv_w"] = dconv_w[:CONV_WIDTH]
    dproj = jnp.concatenate([dag, dq, dk, dv, dfl.astype(BF16)], axis=1)
    dx1, g["mix_norm"] = _mix_proj_bwd(dproj, dx2, x1, p["mix_norm"], w_ag, w_qkv, w_f)
    dw_in = _wgrad(dproj, h2, 1, "dw_in").reshape(dproj.shape[1], D_MODEL)[:N_IN]
    deps = put_grads("mix", {"w_in": dw_in, "w_out": dw_out})
    dx0, dgu1, g["ffn1_norm"], dx1_half, _ = _ffn_bwd(
        dx1, x, gu1, p["ffn1_norm"], w["ffn1_w13"], w["ffn1_w2"], "ffn1_bwd", deps)
    g["final_norm"] = d_final
    g["loss"] = loss[:, :1]
    deps = flush_grads("mix", put_grads("small", g))
    dw2 = _wgrad(act1, dx1_half, 1, "ffn1_dw2", deps).reshape(D_FF, D_MODEL)
    deps = flush_grads("ffn1_w2", put_grads("ffn1_w2", {"ffn1_w2": dw2}))
    dw13 = _wgrad(h1, dgu1, N_CHIPS, "ffn1_dw13", deps)
    put_grads("ffn1_w13", {"ffn1_w13": dw13})
    return dx0


MESH = pl.DeviceIdType.MESH


def _place():
    x, y, c = lax.axis_index("x"), lax.axis_index("y"), lax.axis_index("c")
    chips = [(1 - x, y), (x, 1 - y), (1 - x, 1 - y)]
    return x, y, c, chips


def _hbm_out(shape, dtype):
    return jax.ShapeDtypeStruct(shape, dtype)


def _comm_call(body, name, ins, out_shapes, n_remote, in_place=False):
    return pl.pallas_call(
        body, name=name, in_specs=[_ANY] * len(ins), out_specs=[_ANY] * len(out_shapes), out_shape=out_shapes,
        scratch_shapes=[pltpu.SemaphoreType.DMA((n_remote,)), pltpu.SemaphoreType.DMA((n_remote,))],
        input_output_aliases={i: i for i in range(len(ins))} if in_place else {},
    )(*ins)


def _remote(src, dst, sems, n, to):
    send_sems, recv_sems = sems
    return pltpu.make_async_remote_copy(src_ref=src, dst_ref=dst, send_sem=send_sems.at[n], recv_sem=recv_sems.at[n],
                                        device_id=to, device_id_type=MESH)


HALF_ROWS_MULTIPLE = 32


def _halved_by_rows(rows):
    return rows % HALF_ROWS_MULTIPLE == 0


def _half_shape(rows, cols):
    return (rows // 2, cols) if _halved_by_rows(rows) else (rows, cols // 2)


def _half_index(rows, core):
    return (core, 0) if _halved_by_rows(rows) else (0, core)


def _half_of(ref, rows, cols, core, *lead):
    if _halved_by_rows(rows):
        return ref.at[(*lead, pl.ds(core * (rows // 2), rows // 2), slice(None))]
    return ref.at[(*lead, slice(None), pl.ds(core * (cols // 2), cols // 2))]


def _into_slot(shard, chip, dtype, name, deps=()):
    rows, cols = shard.shape
    half = _half_shape(rows, cols)
    by_rows = _halved_by_rows(rows)
    deps = tuple(deps)

    def body(k_ref, *refs):
        s_ref, o_ref = refs[len(deps):]
        o_ref[0] = s_ref[...].astype(dtype)

    return pl.pallas_call(
        body, name=name,
        grid_spec=pltpu.PrefetchScalarGridSpec(
            num_scalar_prefetch=1, grid=(2,),
            in_specs=[_ANY] * len(deps) + [pl.BlockSpec(half, lambda i, k_ref: (i, 0) if by_rows else (0, i))],
            out_specs=pl.BlockSpec((1,) + half, lambda i, k_ref: (k_ref[0], i, 0) if by_rows else (k_ref[0], 0, i))),
        out_shape=jax.ShapeDtypeStruct((N_CHIPS, rows, cols), dtype),
        compiler_params=_params(dimension_semantics=("arbitrary",)),
    )(chip, *deps, shard)


def _run_copies(name, bufs, n_copies, plan):
    n = len(bufs)

    def body(*refs):
        copies = plan(refs[n:2 * n], refs[2 * n:2 * n + 2])
        for send, _ in copies:
            send.start()
        for send, recv in copies:
            send.wait_send()
            recv.wait_recv()

    return _comm_call(body, name, bufs, [_hbm_out(b.shape, b.dtype) for b in bufs], n_copies, in_place=True)


def _forward_halves(slots, name):
    return _run_copies(name, slots, 3 * len(slots), _d2d_forward_plan(slots))


_HBM = pl.BlockSpec(memory_space=pltpu.HBM)
_SEM = pl.BlockSpec(memory_space=pltpu.SEMAPHORE)
_DATAFLOW = pltpu.SideEffectType.DATAFLOW_SIDE_EFFECTING


def _split_copy_start(name, bufs, n_copies, plan):
    n = len(bufs)

    def body(*refs):
        for send, _ in plan(refs[:n], (refs[n], refs[n + 1])):
            send.start()
        token = refs[-1]
        token[...] = jnp.zeros_like(token)

    out = pl.pallas_call(
        body, name=name,
        out_shape=(pltpu.SemaphoreType.DMA((n_copies,)), pltpu.SemaphoreType.DMA((n_copies,)),
                   *[pltpu.HBM(b.shape, b.dtype) for b in bufs], jax.ShapeDtypeStruct((8, LANES), F32)),
        in_specs=[_HBM] * n, out_specs=(_SEM, _SEM, *[_HBM] * n, pl.BlockSpec(memory_space=pltpu.VMEM)),
        input_output_aliases={i: 2 + i for i in range(n)},
        compiler_params=pltpu.CompilerParams(has_side_effects=_DATAFLOW),
    )(*[pltpu.with_memory_space_constraint(b, pltpu.HBM) for b in bufs])
    return out[0], out[1], list(out[2:2 + n]), out[-1]


def _split_copy_wait(name, started, plan, after, passed=()):
    send_sems, recv_sems, bufs, _ = started
    n = len(bufs)
    after = tuple(after)
    bufs = list(bufs) + list(passed)
    total = len(bufs)

    def body(*refs):
        for send, recv in plan(refs[:n], (refs[total], refs[total + 1])):
            send.wait_send()
            recv.wait_recv()

    out = pl.pallas_call(
        body, name=name, out_shape=tuple(pltpu.HBM(b.shape, b.dtype) for b in bufs),
        in_specs=[_HBM] * total + [_SEM, _SEM] + [_ANY] * len(after), out_specs=tuple([_HBM] * total),
        input_output_aliases={i: i for i in range(total)},
        compiler_params=pltpu.CompilerParams(has_side_effects=_DATAFLOW),
    )(*bufs, send_sems, recv_sems, *after)
    return list(out)


def _ici_gather_plan(slots):
    def plan(refs, sems):
        x, y, c, chips = _place()
        me = 2 * x + y
        copies = []
        for i, ref in enumerate(refs):
            for j, chip in enumerate(chips):
                mine = _half_of(ref, *slots[i].shape[1:], c, me)
                theirs = _half_of(ref, *slots[i].shape[1:], c, 2 * chip[0] + chip[1])
                to = (*chip, c)
                copies.append((_remote(mine, mine, sems, 3 * i + j, to), _remote(theirs, theirs, sems, 3 * i + j, to)))
        return copies

    return plan


def _ici_scatter_plan(n):
    def plan(refs, sems):
        x, y, c, chips = _place()
        copies = []
        for i in range(n):
            for j, chip in enumerate(chips):
                cp = _remote(refs[i].at[2 * chip[0] + chip[1]], refs[n + i].at[j], sems, 3 * i + j, (*chip, c))
                copies.append((cp, cp))
        return copies

    return plan


def _d2d_forward_plan(slots):
    def plan(refs, sems):
        x, y, c, chips = _place()
        sibling = (x, y, 1 - c)
        copies = []
        for i, ref in enumerate(refs):
            for j, chip in enumerate(chips):
                src_chip = 2 * chip[0] + chip[1]
                mine = _half_of(ref, *slots[i].shape[1:], c, src_chip)
                theirs = _half_of(ref, *slots[i].shape[1:], 1 - c, src_chip)
                copies.append((_remote(mine, mine, sems, 3 * i + j, sibling),
                               _remote(theirs, theirs, sems, 3 * i + j, sibling)))
        return copies

    return plan


def _pair_exchange_plan(grads):
    n = len(grads)

    def plan(refs, sems):
        x, y, c, _ = _place()
        copies = []
        for i in range(n):
            theirs = _half_of(refs[i], *grads[i].shape[1:], 1 - c, slice(None))
            cp = _remote(theirs, refs[n + i], sems, i, (x, y, 1 - c))
            copies.append((cp, cp))
        return copies

    return plan


def _pair_share_plan(shapes):
    def plan(refs, sems):
        x, y, c, _ = _place()
        sibling = (x, y, 1 - c)
        copies = []
        for i, ref in enumerate(refs):
            mine, theirs = _half_of(ref, *shapes[i], c), _half_of(ref, *shapes[i], 1 - c)
            copies.append((_remote(mine, mine, sems, i, sibling), _remote(theirs, theirs, sems, i, sibling)))
        return copies

    return plan


def _pair_share(halves, name):
    return _run_copies(name, halves, len(halves), _pair_share_plan([h.shape for h in halves]))


N_DEVICES = 8
FLIPS = [(fx, fy, fc) for fx in range(2) for fy in range(2) for fc in range(2)][1:]


def _small_slots(v, me):
    rows = v.shape[0]

    def body(k_ref, v_ref, o_ref):
        o_ref[0] = v_ref[...]

    return pl.pallas_call(
        body, name="small_slot",
        grid_spec=pltpu.PrefetchScalarGridSpec(
            num_scalar_prefetch=1, grid=(1,),
            in_specs=[pl.BlockSpec((rows, LANES), lambda i, k_ref: (0, 0))],
            out_specs=pl.BlockSpec((1, rows, LANES), lambda i, k_ref: (k_ref[0], 0, 0))),
        out_shape=jax.ShapeDtypeStruct((N_DEVICES, rows, LANES), F32),
        compiler_params=_params(dimension_semantics=("arbitrary",)),
    )(me, v)


def _small_plan():
    def plan(refs, sems):
        x, y, c, _ = _place()
        slots = refs[0]
        me = 4 * x + 2 * y + c
        copies = []
        for n, (fx, fy, fc) in enumerate(FLIPS):
            to = (x ^ fx, y ^ fy, c ^ fc)
            src = 4 * to[0] + 2 * to[1] + to[2]
            copies.append((_remote(slots.at[me], slots.at[me], sems, n, to), _remote(slots.at[src], slots.at[src], sems, n, to)))
        return copies

    return plan


def _small_sum(slots):
    def body(s_ref, o_ref):
        acc = s_ref[0]
        for s in range(1, N_DEVICES):
            acc = acc + s_ref[s]
        o_ref[...] = acc

    return pl.pallas_call(body, name="small_sum", out_shape=jax.ShapeDtypeStruct(slots.shape[1:], F32),
                          compiler_params=_params())(slots)


def _pair_add(gs, sibs, core, name):
    n = len(gs)
    halves = [_half_shape(*g.shape[1:]) for g in gs]

    def body(c_ref, *refs):
        for g_ref, s_ref, o_ref in zip(refs[:n], refs[n:2 * n], refs[2 * n:]):
            o_ref[0] = (g_ref[0].astype(F32) + s_ref[0].astype(F32)).astype(BF16)

    def mine(g, half):
        return pl.BlockSpec((1,) + half, lambda s, c_ref: (s, *_half_index(g.shape[1], c_ref[0])))

    whole = [pl.BlockSpec((1,) + half, lambda s, c_ref: (s, 0, 0)) for half in halves]
    return pl.pallas_call(
        body, name=name,
        grid_spec=pltpu.PrefetchScalarGridSpec(
            num_scalar_prefetch=1, grid=(N_CHIPS,),
            in_specs=[mine(g, half) for g, half in zip(gs, halves)] + whole, out_specs=whole),
        out_shape=[jax.ShapeDtypeStruct((N_CHIPS,) + half, BF16) for half in halves],
        compiler_params=_params(dimension_semantics=("arbitrary",)),
    )(core, *gs, *sibs)


def _chip_add(parts, recvs, chip_core, shapes, name):
    n = len(parts)
    halves = [_half_shape(*shape) for shape in shapes]

    def body(kc_ref, *refs):
        for p_ref, r_ref, o_ref in zip(refs[:n], refs[n:2 * n], refs[2 * n:]):
            acc = p_ref[0].astype(F32)
            for j in range(N_CHIPS - 1):
                acc = acc + r_ref[j].astype(F32)
            o_ref[...] = acc

    def out_spec(shape, half):
        return pl.BlockSpec(half, lambda s, kc_ref: _half_index(shape[0], kc_ref[1]))

    return pl.pallas_call(
        body, name=name,
        grid_spec=pltpu.PrefetchScalarGridSpec(
            num_scalar_prefetch=1, grid=(1,),
            in_specs=[pl.BlockSpec((1,) + half, lambda s, kc_ref: (kc_ref[0], 0, 0)) for half in halves]
            + [pl.BlockSpec((N_CHIPS - 1,) + half, lambda s, kc_ref: (0, 0, 0)) for half in halves],
            out_specs=[out_spec(shape, half) for shape, half in zip(shapes, halves)]),
        out_shape=[jax.ShapeDtypeStruct(tuple(shape), F32) for shape in shapes],
        compiler_params=_params(dimension_semantics=("arbitrary",)),
    )(chip_core, *parts, *recvs)


def _adamw_math(w, g, m, v):
    m = ADAM_B1 * m + (1.0 - ADAM_B1) * g
    v = ADAM_B2 * v + (1.0 - ADAM_B2) * (g * g)
    m_hat = m / (1.0 - ADAM_B1 ** ADAM_STEP)
    v_hat = v / (1.0 - ADAM_B2 ** ADAM_STEP)
    delta = -ADAM_LR * (m_hat / (jnp.sqrt(v_hat) + ADAM_EPS) + ADAM_WD * w)
    return delta, m, v


ADAM_PARTS = 8


def _adamw_matrix(w, g, m, v, name):
    rows, cols = w.shape
    by_rows = rows % (8 * ADAM_PARTS) == 0
    block = (rows // ADAM_PARTS, cols) if by_rows else (rows, cols // ADAM_PARTS)

    def body(w_ref, g_ref, m_ref, v_ref, go_ref, d_ref, mo_ref, vo_ref):
        gv = g_ref[...]
        go_ref[...] = gv
        d_ref[...], mo_ref[...], vo_ref[...] = _adamw_math(w_ref[...], gv, m_ref[...], v_ref[...])

    spec = pl.BlockSpec(block, lambda i: (i, 0) if by_rows else (0, i))
    shape = jax.ShapeDtypeStruct((rows, cols), F32)
    return pl.pallas_call(
        body, name=name, grid=(ADAM_PARTS,), in_specs=[spec] * 4, out_specs=[spec] * 4, out_shape=[shape] * 4,
        compiler_params=_params(dimension_semantics=("arbitrary",)),
    )(w, g, m, v)


def _adamw_small(ws, gs, ms, vs):
    n = len(ws)

    def body(*refs):
        for i in range(n):
            w_ref, g_ref, m_ref, v_ref = (refs[k * n + i] for k in range(4))
            d_ref, mo_ref, vo_ref = (refs[(4 + k) * n + i] for k in range(3))
            d_ref[...], mo_ref[...], vo_ref[...] = _adamw_math(w_ref[...], g_ref[...], m_ref[...], v_ref[...])

    shapes = [jax.ShapeDtypeStruct(w.shape, F32) for w in ws]
    out = pl.pallas_call(body, name="adamw_small", out_shape=shapes * 3, compiler_params=_params())(*ws, *gs, *ms, *vs)
    return out[:n], out[n:2 * n], out[2 * n:]


VECTORS = ["ffn1_norm", "mix_norm", "conv_b", "conv_ln_g", "conv_ln_b", "forget_b", "out_norm_conv",
           "out_norm_attn", "ffn2_norm", "final_norm"]
WEIGHTS = ["ffn1_norm", "ffn1_w13", "ffn1_w2", "mix_norm", "w_in", "conv_w", "conv_b", "conv_ln_g", "conv_ln_b",
           "forget_b", "out_norm_conv", "out_norm_attn", "w_out", "ffn2_norm", "ffn2_w13", "ffn2_w2", "final_norm"]


def _pack_small(g, names):
    rows, layout = [], []
    for n in names:
        flat = g[n].reshape(-1)
        pad = (-flat.shape[0]) % LANES
        rows.append(jnp.pad(flat, (0, pad)).reshape(-1, LANES))
        layout.append((n, g[n].shape, flat.shape[0], rows[-1].shape[0]))
    packed = jnp.concatenate(rows, axis=0)
    pad_rows = (-packed.shape[0]) % 8
    return jnp.pad(packed, ((0, pad_rows), (0, 0))), layout


def _unpack_small(packed, layout):
    out, r = {}, 0
    for n, shape, size, nrows in layout:
        out[n] = packed[r:r + nrows].reshape(-1)[:size].reshape(shape)
        r += nrows
    return out


def kernel(x, ffn1_norm, ffn1_w13, ffn1_w2, mix_norm, w_in, conv_w, conv_b, conv_ln_g, conv_ln_b, forget_b, out_norm_conv, out_norm_attn, w_out, ffn2_norm, ffn2_w13, ffn2_w2, final_norm, loss_target, m_ffn1_norm, m_ffn1_w13, m_ffn1_w2, m_mix_norm, m_w_in, m_conv_w, m_conv_b, m_conv_ln_g, m_conv_ln_b, m_forget_b, m_out_norm_conv, m_out_norm_attn, m_w_out, m_ffn2_norm, m_ffn2_w13, m_ffn2_w2, m_final_norm, v_ffn1_norm, v_ffn1_w13, v_ffn1_w2, v_mix_norm, v_w_in, v_conv_w, v_conv_b, v_conv_ln_g, v_conv_ln_b, v_forget_b, v_out_norm_conv, v_out_norm_attn, v_w_out, v_ffn2_norm, v_ffn2_w13, v_ffn2_w2, v_final_norm):
    args = dict(locals())
    weights = {n: args[n] for n in WEIGHTS}
    core = lax.axis_index("c").astype(jnp.int32).reshape(1)
    chip = (2 * lax.axis_index("x") + lax.axis_index("y")).astype(jnp.int32)
    chip1 = chip.reshape(1)
    chip_core = jnp.concatenate([chip1, core])

    def held(n, a):
        return a[0].T if n == "w_in" else a[0]

    def given(n, a):
        return (a.T if n == "w_in" else a)[None]

    def slot(n, deps=()):
        if n == "conv_w":
            rows = jnp.pad(conv_w[0], ((0, CONV_PAD - CONV_WIDTH), (0, 0)))
            return _into_slot(rows, chip1, F32, "slot_conv_w", deps)
        return _into_slot(held(n, weights[n]), chip1, BF16, "slot_" + n, deps)

    fetched = {"ffn1_w13": ["ffn1_w13"], "ffn1_w2": ["ffn1_w2"], "mix": ["w_in", "w_out", "conv_w"],
               "ffn2": ["ffn2_w13", "ffn2_w2"]}
    fetch = {}

    def as_weights(group, bufs):
        out = {}
        for n, b in zip(fetched[group], bufs):
            if n.endswith("w13"):
                out[n] = b
            elif n != "conv_w":
                out[n] = b.reshape(N_CHIPS * b.shape[1], b.shape[2])
            else:
                out[n] = b[:, :CONV_WIDTH].transpose(1, 0, 2).reshape(CONV_WIDTH, D_CONV)
        return out

    def get_weights(group, after):
        if group == "ffn1_w13":
            first = [slot("ffn1_w13")]
            plan = _ici_gather_plan(first)
            started = _split_copy_start("gather_ffn1_w13_start", first, 3, plan)
            second = [slot("ffn1_w2", [started[3]])]
            plan2 = _ici_gather_plan(second)
            fetch["ffn1_w2"] = plan2, _split_copy_start("gather_ffn1_w2_start", second, 3, plan2)
            later_names = fetched["mix"] + fetched["ffn2"]
            later = [slot(n, [fetch["ffn1_w2"][1][3]]) for n in later_names]
            landed = _split_copy_wait("gather_ffn1_w13_wait", started, plan, [], passed=later)
            bufs = _forward_halves(landed[:1], "forward_ffn1_w13")
            behind = dict(zip(later_names, landed[1:]))
            for later in ("mix", "ffn2"):
                bufs_later = [behind[n] for n in fetched[later]]
                plan = _ici_gather_plan(bufs_later)
                fetch[later] = plan, _split_copy_start("gather_%s_start" % later, bufs_later, 3 * len(bufs_later), plan)
            return as_weights(group, bufs), [fetch["mix"][1][3], fetch["ffn2"][1][3]]
        plan, started = fetch[group.split(":")[0]]
        if group == "ffn2:landed":
            landed = _split_copy_wait("gather_ffn2_wait", started, plan, [after])
            plan = _d2d_forward_plan(landed)
            fetch["ffn2"] = plan, _split_copy_start("forward_ffn2_start", landed, 3 * len(landed), plan)
            return {}, [fetch["ffn2"][1][3]]
        if group == "ffn2":
            return as_weights(group, _split_copy_wait("forward_ffn2_wait", started, plan, [after])), []
        landed = _split_copy_wait("gather_%s_wait" % group, started, plan, [after])
        return as_weights(group, _forward_halves(landed, "forward_" + group)), []

    def shard_major(n, g):
        return g if n.endswith("w13") else g.reshape(N_CHIPS, g.shape[0] // N_CHIPS, g.shape[1])

    exchange, scatter = {}, {}
    small_names = VECTORS + ["conv_w"]
    small = {}

    def put_grads(group, grads):
        if group == "small":
            packed, layout = _pack_small(grads, small_names + ["loss"])
            me = (4 * lax.axis_index("x") + 2 * lax.axis_index("y") + lax.axis_index("c")).astype(jnp.int32).reshape(1)
            plan = _small_plan()
            exchange[group] = layout, plan, _split_copy_start("small_start", [_small_slots(packed, me)], len(FLIPS), plan)
            return [exchange[group][2][3]]
        names = list(grads)
        local = [shard_major(n, grads[n]) for n in names]
        landing = [lax.empty((N_CHIPS,) + _half_shape(*a.shape[1:]), BF16) for a in local]
        plan = _pair_exchange_plan(local)
        exchange[group] = names, plan, _split_copy_start("exchange_%s_start" % group, local + landing, len(local), plan)
        return [exchange[group][2][3]]

    def flush_grads(group, after):
        names, plan, started = exchange[group]
        done = _split_copy_wait("exchange_%s_wait" % group, started, plan, after)
        local, sib = done[:len(names)], done[len(names):]
        parts = list(_pair_add(local, sib, core, "pair_add_" + group))
        landing = [lax.empty((N_CHIPS - 1,) + q.shape[1:], BF16) for q in parts]
        plan = _ici_scatter_plan(len(parts))
        shapes = [a.shape[1:] for a in local]
        scatter[group] = names, plan, _split_copy_start("scatter_%s_start" % group, parts + landing, 3 * len(parts), plan), shapes
        return [scatter[group][2][3]]

    p = {n: weights[n] for n in VECTORS}
    p["final_norm"] = final_norm.reshape(1, D_MODEL)
    dx = _local_step(x[0], loss_target[0], p, get_weights, put_grads, flush_grads)
    layout, plan, started = exchange["small"]
    slots, = _split_copy_wait("small_wait", started, plan, [exchange["ffn1_w13"][2][3]])
    small.update(_unpack_small(_small_sum(slots), layout))
    loss = small["loss"].reshape(())

    grad = {n: small[n] for n in VECTORS}
    grad["final_norm"] = small["final_norm"].reshape(D_MODEL)
    grad["conv_w"] = lax.dynamic_slice_in_dim(small["conv_w"], chip * (D_CONV // N_CHIPS), D_CONV // N_CHIPS, axis=1)[None]

    delta, new_m, new_v = {}, {}, {}

    def reduce_chips(group, after):
        names, plan, started, shapes = scatter[group]
        done = _split_copy_wait("scatter_%s_wait" % group, started, plan, after)
        parts, landed = done[:len(names)], done[len(names):]
        return list(_chip_add(parts, landed, chip_core, shapes, "chip_add_" + group))

    def update(group, full):
        ends = []
        for n, reduced in zip(scatter[group][0], full):
            go, d, mo, vo = _adamw_matrix(held(n, weights[n]), reduced, held(n, args["m_" + n]), held(n, args["v_" + n]),
                                          "adamw_" + n)
            grad[n], delta[n], new_m[n], new_v[n] = given(n, go), given(n, d), given(n, mo), given(n, vo)
            ends.append(vo)
        return ends

    def share_start(group, halves):
        plan = _pair_share_plan(scatter[group][3])
        return plan, _split_copy_start("share_%s_start" % group, halves, len(halves), plan)

    halves_ffn2 = reduce_chips("ffn2", [exchange["ffn1_w13"][2][3]])
    plan_ffn2, share_ffn2 = share_start("ffn2", halves_ffn2)
    last_scatter = flush_grads("ffn1_w13", [share_ffn2[3]])
    halves_mix = reduce_chips("mix", last_scatter)
    plan_mix, share_mix = share_start("mix", halves_mix)
    done_ffn2 = update("ffn2", _split_copy_wait("share_ffn2_wait", share_ffn2, plan_ffn2, [share_mix[3]]))
    done_mix = update("mix", _split_copy_wait("share_mix_wait", share_mix, plan_mix, done_ffn2))
    as2d = lambda a: a.reshape(-1, a.shape[-1])
    ds, mos, vos = _adamw_small([as2d(weights[n]) for n in small_names], [as2d(grad[n]) for n in small_names],
                                [as2d(args["m_" + n]) for n in small_names], [as2d(args["v_" + n]) for n in small_names])
    for n, d, mo, vo in zip(small_names, ds, mos, vos):
        shape = weights[n].shape
        delta[n], new_m[n], new_v[n] = d.reshape(shape), mo.reshape(shape), vo.reshape(shape)
    behind = done_ffn2 + done_mix + [vos[0]]
    halves_w2 = reduce_chips("ffn1_w2", behind)
    halves_w13 = reduce_chips("ffn1_w13", behind)
    full_w2, full_w13 = _pair_share(halves_w2 + halves_w13, "pair_share_ffn1")
    update("ffn1_w2", [full_w2])
    update("ffn1_w13", [full_w13])

    return (loss, dx[None], *[grad[n] for n in WEIGHTS], *[delta[n] for n in WEIGHTS],
            *[new_m[n] for n in WEIGHTS], *[new_v[n] for n in WEIGHTS])
```

```python
import jax
import jax.numpy as jnp
from jax import lax
from jax.experimental import pallas as pl
from jax.experimental.pallas import tpu as pltpu

F32 = jnp.float32
BF16 = jnp.bfloat16

D_MODEL = 1024
D_FF = 2816
FF_SHARD = D_FF // 2
D_CONV = 512
D_ATTN = 512
N_HEADS = 8
HEAD_DIM = 64
CONV_WIDTH = 31
CONV_PAD = 32
N_IN = 2 * D_CONV + 3 * D_ATTN + N_HEADS
EPS = 1e-6
N_CHIPS = 4
LANES = 128
TOKEN_ROWS = 512
HEAD_ROWS = 16

ADAM_LR = 0.001
ADAM_B1 = 0.9
ADAM_B2 = 0.999
ADAM_EPS = 1e-08
ADAM_WD = 0.01
ADAM_STEP = 10

VMEM_LIMIT = 56 * 1024 * 1024

_NT = (((1,), (1,)), ((), ()))
_TN = (((0,), (0,)), ((), ()))


def _dot(a, b):
    return jnp.dot(a, b, preferred_element_type=F32)


def _dot_nt(a, b):
    return lax.dot_general(a, b, _NT, preferred_element_type=F32)


def _dot_tn(a, b):
    return lax.dot_general(a, b, _TN, preferred_element_type=F32)


def _params(**kw):
    return pltpu.CompilerParams(vmem_limit_bytes=VMEM_LIMIT, **kw)


def _sigmoid(x):
    return 1.0 / (1.0 + jnp.exp(-x))


def _rms_stats(x):
    return lax.rsqrt(jnp.mean(x * x, axis=-1, keepdims=True) + EPS)


def _rms_bwd(x, r, g, dh):
    t = dh * g
    dx = r * t - x * (r * r * r) * jnp.mean(t * x, axis=-1, keepdims=True)
    return dx, dh * x * r


def _silu_grad(z, sg):
    return sg * (1.0 + z * (1.0 - sg))


def _row_spec(tm, n):
    return pl.BlockSpec((tm, n), lambda i: (i, 0))


def _full_spec(shape):
    nd = len(shape)
    return pl.BlockSpec(shape, lambda i: (0,) * nd)


_ANY = pl.BlockSpec(memory_space=pl.ANY)


def _skip(n, body):
    return lambda *refs: body(*refs[n:])


FFN_ROWS = 256
FFN_WEIGHT_PARTS = N_CHIPS + 2


def _with_ffn_weights(w13_hbm, w2_hbm, w13_ref, w2_ref, sems, order, tile):
    first = pl.program_id(0) == 0
    copies = {}
    if w13_hbm is not None:
        for k in range(N_CHIPS):
            copies["w13", k] = pltpu.make_async_copy(w13_hbm.at[k], w13_ref.at[k], sems.at[k])
    if w2_hbm is not None:
        for half in range(2):
            rows = pl.ds(half * FF_SHARD, FF_SHARD)
            copies["w2", half] = pltpu.make_async_copy(w2_hbm.at[rows, :], w2_ref.at[rows, :], sems.at[N_CHIPS + half])

    @pl.when(first)
    def _():
        for part in order:
            copies[part].start()

        def ready(*parts):
            for part in parts:
                copies[part].wait()

        tile(ready)

    @pl.when(jnp.logical_not(first))
    def _():
        tile(lambda *parts: None)


def _ffn_fwd(x, g, w13s, w2, name, deps=()):
    t = x.shape[0]
    tm = FFN_ROWS
    deps = tuple(deps)

    def body(x_ref, g_ref, w13_hbm, w2_hbm, xo_ref, h_ref, gu_ref, a_ref, w13_ref, w2_ref, sems):
        def tile(ready):
            xv = x_ref[...]
            hb = (xv * _rms_stats(xv) * g_ref[...]).astype(BF16)
            h_ref[...] = hb
            acc = jnp.zeros((tm, D_MODEL), F32)
            for half in range(2):
                lo = half * FF_SHARD
                ready(("w13", half), ("w13", 2 + half))
                gate = _dot(hb, w13_ref[half])
                up = _dot(hb, w13_ref[2 + half])
                gu_ref[:, lo:lo + FF_SHARD] = gate.astype(BF16)
                gu_ref[:, D_FF + lo:D_FF + lo + FF_SHARD] = up.astype(BF16)
                a = (gate * _sigmoid(gate) * up).astype(BF16)
                a_ref[:, lo:lo + FF_SHARD] = a
                ready(("w2", half))
                acc = acc + _dot(a, w2_ref[lo:lo + FF_SHARD, :])
            xo_ref[...] = xv + 0.5 * acc

        _with_ffn_weights(w13_hbm, w2_hbm, w13_ref, w2_ref, sems,
                          [("w13", 0), ("w13", 2), ("w2", 0), ("w13", 1), ("w13", 3), ("w2", 1)], tile)

    return pl.pallas_call(
        _skip(len(deps), body), name=name, grid=(t // tm,),
        in_specs=[_ANY] * len(deps) + [_row_spec(tm, D_MODEL), _full_spec((1, D_MODEL)), _ANY, _ANY],
        out_specs=[_row_spec(tm, D_MODEL), _row_spec(tm, D_MODEL), _row_spec(tm, 2 * D_FF), _row_spec(tm, D_FF)],
        out_shape=[jax.ShapeDtypeStruct((t, D_MODEL), F32), jax.ShapeDtypeStruct((t, D_MODEL), BF16),
                   jax.ShapeDtypeStruct((t, 2 * D_FF), BF16), jax.ShapeDtypeStruct((t, D_FF), BF16)],
        scratch_shapes=[pltpu.VMEM(w13s.shape, BF16), pltpu.VMEM(w2.shape, BF16),
                        pltpu.SemaphoreType.DMA((FFN_WEIGHT_PARTS,))],
        compiler_params=_params(dimension_semantics=("arbitrary",)),
    )(*deps, x, g, w13s, w2)


def _ffn_up(x, g, w13s, name, deps=()):
    t = x.shape[0]
    tm = FFN_ROWS
    deps = tuple(deps)

    def body(x_ref, g_ref, w13_hbm, h_ref, gu_ref, a_ref, w13_ref, sems):
        def tile(ready):
            xv = x_ref[...]
            hb = (xv * _rms_stats(xv) * g_ref[...]).astype(BF16)
            h_ref[...] = hb
            for half in range(2):
                lo = half * FF_SHARD
                ready(("w13", half), ("w13", 2 + half))
                gate = _dot(hb, w13_ref[half])
                up = _dot(hb, w13_ref[2 + half])
                gu_ref[:, lo:lo + FF_SHARD] = gate.astype(BF16)
                gu_ref[:, D_FF + lo:D_FF + lo + FF_SHARD] = up.astype(BF16)
                a_ref[:, lo:lo + FF_SHARD] = (gate * _sigmoid(gate) * up).astype(BF16)

        _with_ffn_weights(w13_hbm, None, w13_ref, None, sems, [("w13", 0), ("w13", 2), ("w13", 1), ("w13", 3)], tile)

    return pl.pallas_call(
        _skip(len(deps), body), name=name, grid=(t // tm,),
        in_specs=[_ANY] * len(deps) + [_row_spec(tm, D_MODEL), _full_spec((1, D_MODEL)), _ANY],
        out_specs=[_row_spec(tm, D_MODEL), _row_spec(tm, 2 * D_FF), _row_spec(tm, D_FF)],
        out_shape=[jax.ShapeDtypeStruct((t, D_MODEL), BF16), jax.ShapeDtypeStruct((t, 2 * D_FF), BF16),
                   jax.ShapeDtypeStruct((t, D_FF), BF16)],
        scratch_shapes=[pltpu.VMEM(w13s.shape, BF16), pltpu.SemaphoreType.DMA((FFN_WEIGHT_PARTS,))],
        compiler_params=_params(dimension_semantics=("arbitrary",)),
    )(*deps, x, g, w13s)


def _ffn_down(x, a, w2, name):
    t = x.shape[0]
    tm = FFN_ROWS

    def body(x_ref, a_ref, w2_hbm, xo_ref, w2_ref, sems):
        def tile(ready):
            ready(("w2", 0))
            acc = _dot(a_ref[:, 0:FF_SHARD], w2_ref[0:FF_SHARD, :])
            ready(("w2", 1))
            acc = acc + _dot(a_ref[:, FF_SHARD:], w2_ref[FF_SHARD:, :])
            xo_ref[...] = x_ref[...] + 0.5 * acc

        _with_ffn_weights(None, w2_hbm, None, w2_ref, sems, [("w2", 0), ("w2", 1)], tile)

    return pl.pallas_call(
        body, name=name, grid=(t // tm,),
        in_specs=[_row_spec(tm, D_MODEL), _row_spec(tm, D_FF), _ANY],
        out_specs=_row_spec(tm, D_MODEL), out_shape=jax.ShapeDtypeStruct((t, D_MODEL), F32),
        scratch_shapes=[pltpu.VMEM(w2.shape, BF16), pltpu.SemaphoreType.DMA((FFN_WEIGHT_PARTS,))],
        compiler_params=_params(dimension_semantics=("arbitrary",)),
    )(x, a, w2)


def _ffn_bwd(dy, x, gu, g, w13s, w2, name, deps=()):
    t = x.shape[0]
    tm = FFN_ROWS
    deps = tuple(deps)

    def body(dy_ref, x_ref, gu_ref, g_ref, w13_hbm, w2_hbm, dx_ref, dgu_ref, dg_ref, dyh_ref, dxb_ref,
             w13_ref, w2_ref, sems):
        @pl.when(pl.program_id(0) == 0)
        def _():
            dg_ref[...] = jnp.zeros_like(dg_ref)

        def tile(ready):
            dyv = dy_ref[...]
            dyh = (0.5 * dyv).astype(BF16)
            dyh_ref[...] = dyh
            dh = jnp.zeros((tm, D_MODEL), F32)
            for half in range(2):
                lo = half * FF_SHARD
                ready(("w2", half))
                da = _dot_nt(dyh, w2_ref[lo:lo + FF_SHARD, :])
                gate = gu_ref[:, lo:lo + FF_SHARD].astype(F32)
                up = gu_ref[:, D_FF + lo:D_FF + lo + FF_SHARD].astype(F32)
                sg = _sigmoid(gate)
                act = gate * sg
                dgate = (da * up * _silu_grad(gate, sg)).astype(BF16)
                dup = (da * act).astype(BF16)
                dgu_ref[:, lo:lo + FF_SHARD] = dgate
                dgu_ref[:, D_FF + lo:D_FF + lo + FF_SHARD] = dup
                ready(("w13", half), ("w13", 2 + half))
                dh = dh + _dot_nt(dgate, w13_ref[half]) + _dot_nt(dup, w13_ref[2 + half])
            xv = x_ref[...]
            dxn, dg_rows = _rms_bwd(xv, _rms_stats(xv), g_ref[...], dh)
            dx = dyv + dxn
            dx_ref[...] = dx
            dxb_ref[...] = dx.astype(BF16)
            dg_ref[...] += jnp.sum(dg_rows, axis=0, keepdims=True)

        _with_ffn_weights(w13_hbm, w2_hbm, w13_ref, w2_ref, sems,
                          [("w2", 0), ("w13", 0), ("w13", 2), ("w2", 1), ("w13", 1), ("w13", 3)], tile)

    return pl.pallas_call(
        _skip(len(deps), body), name=name, grid=(t // tm,),
        in_specs=[_ANY] * len(deps) + [_row_spec(tm, D_MODEL), _row_spec(tm, D_MODEL), _row_spec(tm, 2 * D_FF),
                                       _full_spec((1, D_MODEL)), _ANY, _ANY],
        out_specs=[_row_spec(tm, D_MODEL), _row_spec(tm, 2 * D_FF),
                   _full_spec((1, D_MODEL)), _row_spec(tm, D_MODEL), _row_spec(tm, D_MODEL)],
        out_shape=[jax.ShapeDtypeStruct((t, D_MODEL), F32), jax.ShapeDtypeStruct((t, 2 * D_FF), BF16),
                   jax.ShapeDtypeStruct((1, D_MODEL), F32),
                   jax.ShapeDtypeStruct((t, D_MODEL), BF16), jax.ShapeDtypeStruct((t, D_MODEL), BF16)],
        scratch_shapes=[pltpu.VMEM(w13s.shape, BF16), pltpu.VMEM(w2.shape, BF16),
                        pltpu.SemaphoreType.DMA((FFN_WEIGHT_PARTS,))],
        compiler_params=_params(dimension_semantics=("arbitrary",)),
    )(*deps, dy, x, gu, g, w13s, w2)


WGRAD_ROWS = (1024, 512, 384, 256)


def _wgrad(a, b, n_blocks, name, deps=()):
    t, m = a.shape
    tm = next(rows for rows in WGRAD_ROWS if m % rows == 0)
    n = b.shape[1]
    bn = n // n_blocks
    deps = tuple(deps)
    assert a.dtype == BF16 and b.dtype == BF16

    def body(a_ref, b_ref, o_ref):
        o_ref[0] = _dot_tn(a_ref[...], b_ref[...]).astype(BF16)

    return pl.pallas_call(
        _skip(len(deps), body), name=name, grid=(n_blocks, m // tm),
        in_specs=[_ANY] * len(deps) + [pl.BlockSpec((t, tm), lambda j, i: (0, i)),
                                       pl.BlockSpec((t, bn), lambda j, i: (0, j))],
        out_specs=pl.BlockSpec((1, tm, bn), lambda j, i: (j, i, 0)),
        out_shape=jax.ShapeDtypeStruct((n_blocks, m, bn), BF16),
        compiler_params=_params(dimension_semantics=("arbitrary", "arbitrary")),
    )(*deps, a, b)


def _mix_proj(x, g, w_ag, w_qkv, w_f):
    t = x.shape[0]
    tm = TOKEN_ROWS

    def body(x_ref, g_ref, wag_ref, wqkv_ref, wf_ref, h_ref, ag_ref, qkv_ref, fl_ref):
        xv = x_ref[...]
        hb = (xv * _rms_stats(xv) * g_ref[...]).astype(BF16)
        h_ref[...] = hb
        ag_ref[...] = _dot_nt(hb, wag_ref[...])
        qkv_ref[...] = _dot_nt(hb, wqkv_ref[...]).astype(BF16)
        fl_ref[...] = _dot_nt(hb, wf_ref[...])

    return pl.pallas_call(
        body, name="mix_proj", grid=(t // tm,),
        in_specs=[_row_spec(tm, D_MODEL), _full_spec((1, D_MODEL)), _full_spec(w_ag.shape),
                  _full_spec(w_qkv.shape), _full_spec(w_f.shape)],
        out_specs=[_row_spec(tm, D_MODEL), _row_spec(tm, 2 * D_CONV), _row_spec(tm, 3 * D_ATTN),
                   _row_spec(tm, LANES)],
        out_shape=[jax.ShapeDtypeStruct((t, D_MODEL), BF16), jax.ShapeDtypeStruct((t, 2 * D_CONV), F32),
                   jax.ShapeDtypeStruct((t, 3 * D_ATTN), BF16), jax.ShapeDtypeStruct((t, LANES), F32)],
        compiler_params=_params(dimension_semantics=("arbitrary",)),
    )(x, g, w_ag, w_qkv, w_f)


def _mix_proj_bwd(dproj, dx2, x1, g, w_ag, w_qkv, w_f):
    t = x1.shape[0]
    tm = TOKEN_ROWS
    n_ag, n_qkv = 2 * D_CONV, 3 * D_ATTN

    def body(dp_ref, dx2_ref, x_ref, g_ref, wag_ref, wqkv_ref, wf_ref, dx_ref, dg_ref):
        @pl.when(pl.program_id(0) == 0)
        def _():
            dg_ref[...] = jnp.zeros_like(dg_ref)

        dh = (_dot(dp_ref[:, 0:n_ag], wag_ref[...]) + _dot(dp_ref[:, n_ag:n_ag + n_qkv], wqkv_ref[...])
              + _dot(dp_ref[:, n_ag + n_qkv:], wf_ref[...]))
        xv = x_ref[...]
        dxn, dg_rows = _rms_bwd(xv, _rms_stats(xv), g_ref[...], dh)
        dx_ref[...] = dx2_ref[...] + dxn
        dg_ref[...] += jnp.sum(dg_rows, axis=0, keepdims=True)

    return pl.pallas_call(
        body, name="mix_proj_bwd", grid=(t // tm,),
        in_specs=[_row_spec(tm, dproj.shape[1]),
                  _row_spec(tm, D_MODEL), _row_spec(tm, D_MODEL), _full_spec((1, D_MODEL)),
                  _full_spec(w_ag.shape), _full_spec(w_qkv.shape), _full_spec(w_f.shape)],
        out_specs=[_row_spec(tm, D_MODEL), _full_spec((1, D_MODEL))],
        out_shape=[jax.ShapeDtypeStruct((t, D_MODEL), F32), jax.ShapeDtypeStruct((1, D_MODEL), F32)],
        compiler_params=_params(dimension_semantics=("arbitrary",)),
    )(dproj, dx2, x1, g, w_ag, w_qkv, w_f)


def _split3(x):
    hi = x.astype(BF16)
    r1 = x - hi.astype(F32)
    mid = r1.astype(BF16)
    lo = (r1 - mid.astype(F32)).astype(BF16)
    return hi, mid, lo


def _gates_fwd(flt, fb):
    t = flt.shape[1]

    def body(f_ref, b_ref, d_ref):
        z = f_ref[...] + b_ref[...]
        logf = jnp.minimum(z, 0.0) - jnp.log(1.0 + jnp.exp(-jnp.abs(z)))
        row = lax.broadcasted_iota(jnp.int32, (LANES, LANES), 0)
        col = lax.broadcasted_iota(jnp.int32, (LANES, LANES), 1)
        upper = (row <= col).astype(BF16)
        carry = jnp.zeros((HEAD_ROWS, 1), F32)
        for blk in range(t // LANES):
            hi, mid, lo = _split3(logf[:, blk * LANES:(blk + 1) * LANES])
            cs = _dot(hi, upper) + _dot(mid, upper) + _dot(lo, upper)
            d_ref[:, blk * LANES:(blk + 1) * LANES] = cs + carry
            carry = carry + cs[:, LANES - 1:LANES]

    return pl.pallas_call(
        body, name="gates_fwd", out_shape=jax.ShapeDtypeStruct((HEAD_ROWS, t), F32),
        compiler_params=_params(),
    )(flt, fb)


def _gates_bwd(dd, flt, fb):
    t = flt.shape[1]

    def body(dd_ref, f_ref, b_ref, df_ref, db_ref):
        z = f_ref[...] + b_ref[...]
        row = lax.broadcasted_iota(jnp.int32, (LANES, LANES), 0)
        col = lax.broadcasted_iota(jnp.int32, (LANES, LANES), 1)
        lower = (row >= col).astype(BF16)
        carry = jnp.zeros((HEAD_ROWS, 1), F32)
        db = jnp.zeros((HEAD_ROWS, 1), F32)
        for blk in reversed(range(t // LANES)):
            sl = slice(blk * LANES, (blk + 1) * LANES)
            hi, mid, lo = _split3(dd_ref[:, sl])
            cs = _dot(hi, lower) + _dot(mid, lower) + _dot(lo, lower)
            dz = (cs + carry) * _sigmoid(-z[:, sl])
            df_ref[:, sl] = dz
            db = db + jnp.sum(dz, axis=1, keepdims=True)
            carry = carry + cs[:, 0:1]
        db_ref[...] = db

    return pl.pallas_call(
        body, name="gates_bwd",
        out_shape=[jax.ShapeDtypeStruct((HEAD_ROWS, t), F32), jax.ShapeDtypeStruct((HEAD_ROWS, 1), F32)],
        compiler_params=_params(),
    )(dd, flt, fb)


CONV_CHUNK = 128
CONV_TAIL = 16
CONV_WINDOW = CONV_CHUNK + CONV_PAD + 8
CONV_ROWS_EXTRA = CONV_PAD + CONV_TAIL
SUBLANES = 8


def _conv_rows(ag_ref, u_ref, t):
    u_ref[0:CONV_PAD, :] = jnp.zeros((CONV_PAD, D_CONV), F32)
    u_ref[CONV_PAD + t:CONV_ROWS_EXTRA + t, :] = jnp.zeros((CONV_TAIL, D_CONV), F32)

    def fill(i, c):
        r0 = pl.multiple_of(i * CONV_CHUNK, CONV_CHUNK)
        a = ag_ref[pl.ds(r0, CONV_CHUNK), 0:D_CONV]
        gt = ag_ref[pl.ds(r0, CONV_CHUNK), D_CONV:2 * D_CONV]
        u_ref[pl.ds(CONV_PAD + r0, CONV_CHUNK), :] = a * _sigmoid(gt)
        return c

    lax.fori_loop(0, t // CONV_CHUNK, fill, 0)


def _for_shifted(ref, r0, offsets, fn):
    window = ref[pl.ds(r0, CONV_WINDOW), :]
    for rem in range(SUBLANES):
        mine = [o for o in offsets if o % SUBLANES == rem]
        if not mine:
            continue
        turned = window if rem == 0 else pltpu.roll(window, CONV_WINDOW - rem, 0)
        for o in mine:
            fn(o, turned[o - rem:o - rem + CONV_CHUNK])


def _conv_taps(u_ref, r0, w_ref, cb):
    acc = [jnp.zeros((CONV_CHUNK, D_CONV), F32)]

    def tap(o, rows):
        j = o - (CONV_PAD - CONV_WIDTH + 1)
        acc[0] = acc[0] + w_ref[j:j + 1, :] * rows

    _for_shifted(u_ref, r0, [j + CONV_PAD - CONV_WIDTH + 1 for j in range(CONV_WIDTH)], tap)
    return acc[0] + cb


def _conv_point(y, lg, lb):
    mu = jnp.mean(y, axis=-1, keepdims=True)
    yc = y - mu
    rstd = lax.rsqrt(jnp.mean(yc * yc, axis=-1, keepdims=True) + EPS)
    yhat = yc * rstd
    z = yhat * lg + lb
    sg = _sigmoid(z)
    s = z * sg
    rr = _rms_stats(s)
    return yhat, rstd, z, sg, s, rr


def _conv_fwd(ag, conv_w, conv_b, ln_g, ln_b, norm_g):
    t = ag.shape[0]

    def body(ag_ref, w_ref, cb_ref, lg_ref, lb_ref, ng_ref, o_ref, y_ref, u_ref):
        _conv_rows(ag_ref, u_ref, t)
        cb, lg, lb, ng = cb_ref[...], lg_ref[...], lb_ref[...], ng_ref[...]

        def chunk(i, c):
            r0 = pl.multiple_of(i * CONV_CHUNK, CONV_CHUNK)
            y = _conv_taps(u_ref, r0, w_ref, cb)
            y_ref[pl.ds(r0, CONV_CHUNK), :] = y
            _, _, _, _, s, rr = _conv_point(y, lg, lb)
            o_ref[pl.ds(r0, CONV_CHUNK), :] = (s * rr * ng).astype(BF16)
            return c

        lax.fori_loop(0, t // CONV_CHUNK, chunk, 0)

    return pl.pallas_call(
        body, name="conv_fwd",
        out_shape=[jax.ShapeDtypeStruct((t, D_CONV), BF16), jax.ShapeDtypeStruct((t, D_CONV), F32)],
        scratch_shapes=[pltpu.VMEM((t + CONV_ROWS_EXTRA, D_CONV), F32)],
        compiler_params=_params(),
    )(ag, conv_w, conv_b, ln_g, ln_b, norm_g)


def _conv_bwd(ag, y, dout, conv_w, ln_g, ln_b, norm_g):
    t = ag.shape[0]

    def body(ag_ref, y_ref, do_ref, w_ref, lg_ref, lb_ref, ng_ref,
             dag_ref, dw_ref, dcb_ref, dlg_ref, dlb_ref, dng_ref, u_ref, dy_ref):
        _conv_rows(ag_ref, u_ref, t)
        dy_ref[t:t + CONV_ROWS_EXTRA, :] = jnp.zeros((CONV_ROWS_EXTRA, D_CONV), F32)
        lg, lb, ng = lg_ref[...], lb_ref[...], ng_ref[...]
        dw_ref[...] = jnp.zeros_like(dw_ref)
        zero = jnp.zeros((1, D_CONV), F32)

        def chunk(i, carry):
            dcb, dlg, dlb, dng = carry
            r0 = pl.multiple_of(i * CONV_CHUNK, CONV_CHUNK)
            yhat, rstd, z, sg, s, rr = _conv_point(y_ref[pl.ds(r0, CONV_CHUNK), :], lg, lb)
            do = do_ref[pl.ds(r0, CONV_CHUNK), :]
            ds, dng_rows = _rms_bwd(s, rr, ng, do)
            dz = ds * _silu_grad(z, sg)
            dyhat = dz * lg
            dy = rstd * (dyhat - jnp.mean(dyhat, axis=-1, keepdims=True)
                         - yhat * jnp.mean(dyhat * yhat, axis=-1, keepdims=True))
            dy_ref[pl.ds(r0, CONV_CHUNK), :] = dy
            def tap(o, rows):
                j = o - (CONV_PAD - CONV_WIDTH + 1)
                dw_ref[j:j + 1, :] += jnp.sum(dy * rows, axis=0, keepdims=True)

            _for_shifted(u_ref, r0, [j + CONV_PAD - CONV_WIDTH + 1 for j in range(CONV_WIDTH)], tap)
            return (dcb + jnp.sum(dy, axis=0, keepdims=True), dlg + jnp.sum(dz * yhat, axis=0, keepdims=True),
                    dlb + jnp.sum(dz, axis=0, keepdims=True), dng + jnp.sum(dng_rows, axis=0, keepdims=True))

        dcb, dlg, dlb, dng = lax.fori_loop(0, t // CONV_CHUNK, chunk, (zero, zero, zero, zero))
        dcb_ref[...] = dcb
        dlg_ref[...] = dlg
        dlb_ref[...] = dlb
        dng_ref[...] = dng

        def chunk2(i, c):
            r0 = pl.multiple_of(i * CONV_CHUNK, CONV_CHUNK)
            acc = [jnp.zeros((CONV_CHUNK, D_CONV), F32)]

            def tap(o, rows):
                j = CONV_WIDTH - 1 - o
                acc[0] = acc[0] + w_ref[j:j + 1, :] * rows

            _for_shifted(dy_ref, r0, list(range(CONV_WIDTH)), tap)
            du = acc[0]
            a = ag_ref[pl.ds(r0, CONV_CHUNK), 0:D_CONV]
            gt = ag_ref[pl.ds(r0, CONV_CHUNK), D_CONV:2 * D_CONV]
            sg = _sigmoid(gt)
            dag_ref[pl.ds(r0, CONV_CHUNK), 0:D_CONV] = (du * sg).astype(BF16)
            dag_ref[pl.ds(r0, CONV_CHUNK), D_CONV:2 * D_CONV] = (du * a * sg * (1.0 - sg)).astype(BF16)
            return c

        lax.fori_loop(0, t // CONV_CHUNK, chunk2, 0)

    vec = jax.ShapeDtypeStruct((1, D_CONV), F32)
    return pl.pallas_call(
        body, name="conv_bwd",
        out_shape=[jax.ShapeDtypeStruct((t, 2 * D_CONV), BF16), jax.ShapeDtypeStruct((CONV_PAD, D_CONV), F32),
                   vec, vec, vec, vec],
        scratch_shapes=[pltpu.VMEM((t + CONV_ROWS_EXTRA, D_CONV), F32), pltpu.VMEM((t + CONV_ROWS_EXTRA, D_CONV), F32)],
        compiler_params=_params(),
    )(ag, y, dout, conv_w, ln_g, ln_b, norm_g)


Q_ROWS = 256
ATTN_SCALE = HEAD_DIM ** -0.5
ATTN_AHEAD = 1


def _attn_specs(t):
    blk = lambda off: pl.BlockSpec((t, LANES), lambda p: (0, off + p))
    pairs = N_HEADS // 2
    return [blk(0), blk(pairs), blk(2 * pairs), pl.BlockSpec((2, 1, t), lambda p: (p, 0, 0))]


def _one_head(q2, mask):
    return jnp.where(mask, q2, jnp.zeros_like(q2)) * ATTN_SCALE


def _attn_scores(qs, k2, drow, r0, q1):
    s = _dot_nt(qs, k2) - drow
    rowi = lax.broadcasted_iota(jnp.int32, (q1 - r0, q1 - r0), 0)
    coli = lax.broadcasted_iota(jnp.int32, (q1 - r0, q1 - r0), 1)
    diag = jnp.where(coli <= rowi, s[:, r0:q1], -jnp.inf)
    return diag if r0 == 0 else jnp.concatenate([s[:, :r0], diag], axis=1)


def _attn_fwd(qkv, drow, deps=()):
    t = qkv.shape[0]
    deps = tuple(deps)

    def body(q_ref, k_ref, v_ref, dr_ref, o_ref, lse_ref):
        head_a = lax.broadcasted_iota(jnp.int32, (1, LANES), 1) < HEAD_DIM
        items = [(qb, hh) for qb in range(t // Q_ROWS) for hh in range(2)]

        def scores(item):
            qb, hh = item
            r0, q1 = qb * Q_ROWS, (qb + 1) * Q_ROWS
            qs = _one_head(q_ref[r0:q1, :], head_a if hh == 0 else ~head_a)
            return _attn_scores(qs, k_ref[0:q1, :], dr_ref[hh, :, 0:q1], r0, q1)

        ahead = [scores(item) for item in items[:ATTN_AHEAD]]
        outs = []
        for n, (qb, hh) in enumerate(items):
            r0, q1 = qb * Q_ROWS, (qb + 1) * Q_ROWS
            s = ahead.pop(0)
            if n + ATTN_AHEAD < len(items):
                ahead.append(scores(items[n + ATTN_AHEAD]))
            mx = jnp.max(s, axis=1, keepdims=True)
            p = jnp.exp(s - mx)
            l = jnp.sum(p, axis=1, keepdims=True)
            lse_ref[hh, r0:q1, :] = mx + jnp.log(l)
            outs.append(_dot(p.astype(BF16), v_ref[0:q1, :]) * (1.0 / l))
            if hh == 1:
                o_ref[r0:q1, :] = jnp.where(head_a, outs[0], outs[1])
                outs = []

    pairs = N_HEADS // 2
    return pl.pallas_call(
        _skip(len(deps), body), name="attn_fwd", grid=(pairs,), in_specs=[_ANY] * len(deps) + _attn_specs(t),
        out_specs=[pl.BlockSpec((t, LANES), lambda p: (0, p)), pl.BlockSpec((2, t, 1), lambda p: (p, 0, 0))],
        out_shape=[jax.ShapeDtypeStruct((t, D_ATTN), F32), jax.ShapeDtypeStruct((N_HEADS, t, 1), F32)],
        compiler_params=_params(dimension_semantics=("arbitrary",)),
    )(*deps, qkv, qkv, qkv, drow)


def _attn_bwd(qkv, drow, lse, do):
    t = qkv.shape[0]

    def body(q_ref, k_ref, v_ref, dr_ref, lse_ref, do_ref,
             dq_ref, dk_ref, dv_ref, dd_ref, dk_acc, dv_acc):
        head_a = lax.broadcasted_iota(jnp.int32, (1, LANES), 1) < HEAD_DIM
        dk_acc[...] = jnp.zeros_like(dk_acc)
        dv_acc[...] = jnp.zeros_like(dv_acc)
        dd_ref[...] = jnp.zeros_like(dd_ref)
        items = [(qb, hh) for qb in range(t // Q_ROWS) for hh in range(2)]

        def products(item):
            qb, hh = item
            r0, q1 = qb * Q_ROWS, (qb + 1) * Q_ROWS
            mask = head_a if hh == 0 else ~head_a
            qs = _one_head(q_ref[r0:q1, :], mask)
            dob = jnp.where(mask, do_ref[r0:q1, :], 0.0).astype(BF16)
            s = _attn_scores(qs, k_ref[0:q1, :], dr_ref[hh, :, 0:q1], r0, q1)
            return qs, dob, s, _dot_nt(dob, v_ref[0:q1, :])

        ahead = products(items[0])
        dqs = []
        for n, (qb, hh) in enumerate(items):
            r0, q1 = qb * Q_ROWS, (qb + 1) * Q_ROWS
            qs, dob, s, dp = ahead
            if n + 1 < len(items):
                ahead = products(items[n + 1])
            p = jnp.exp(s - lse_ref[hh, r0:q1, :])
            ds = p * (dp - jnp.sum(p * dp, axis=1, keepdims=True))
            dsb = ds.astype(BF16)
            dqs.append(_dot(dsb, k_ref[0:q1, :]) * ATTN_SCALE)
            dk_acc[0:q1, :] += _dot_tn(dsb, qs)
            dv_acc[0:q1, :] += _dot_tn(p.astype(BF16), dob)
            dd_ref[hh, :, 0:q1] -= jnp.sum(ds, axis=0, keepdims=True)
            if hh == 1:
                dq_ref[r0:q1, :] = jnp.where(head_a, dqs[0], dqs[1]).astype(BF16)
                dqs = []
        dk_ref[...] = dk_acc[...].astype(BF16)
        dv_ref[...] = dv_acc[...].astype(BF16)

    pairs = N_HEADS // 2
    col = pl.BlockSpec((t, LANES), lambda p: (0, p))
    grad = jax.ShapeDtypeStruct((t, D_ATTN), BF16)
    return pl.pallas_call(
        body, name="attn_bwd", grid=(pairs,),
        in_specs=_attn_specs(t) + [pl.BlockSpec((2, t, 1), lambda p: (p, 0, 0)), col],
        out_specs=[col, col, col, pl.BlockSpec((2, 1, t), lambda p: (p, 0, 0))],
        out_shape=[grad, grad, grad, jax.ShapeDtypeStruct((N_HEADS, 1, t), F32)],
        scratch_shapes=[pltpu.VMEM((t, LANES), F32), pltpu.VMEM((t, LANES), F32)],
        compiler_params=_params(dimension_semantics=("arbitrary",)),
    )(qkv, qkv, qkv, drow, lse, do)


def _out_proj(ycn, o, g_attn, w_out, x1, deps=()):
    t = x1.shape[0]
    tm = TOKEN_ROWS
    deps = tuple(deps)

    def body(yc_ref, o_ref, g_ref, w_ref, x_ref, xo_ref, ya_ref):
        ov = o_ref[...]
        ya = (ov * _rms_stats(ov) * g_ref[...]).astype(BF16)
        ya_ref[...] = ya
        xo_ref[...] = x_ref[...] + _dot(yc_ref[...], w_ref[0:D_CONV, :]) + _dot(ya, w_ref[D_CONV:, :])

    return pl.pallas_call(
        _skip(len(deps), body), name="out_proj", grid=(t // tm,),
        in_specs=[_ANY] * len(deps) + [_row_spec(tm, D_CONV), _row_spec(tm, D_ATTN), _full_spec((1, D_ATTN)),
                                       _full_spec(w_out.shape), _row_spec(tm, D_MODEL)],
        out_specs=[_row_spec(tm, D_MODEL), _row_spec(tm, D_ATTN)],
        out_shape=[jax.ShapeDtypeStruct((t, D_MODEL), F32), jax.ShapeDtypeStruct((t, D_ATTN), BF16)],
        compiler_params=_params(dimension_semantics=("arbitrary",)),
    )(*deps, ycn, o, g_attn, w_out, x1)


def _out_proj_bwd(dx2, o, g_attn, w_out, deps=()):
    t = dx2.shape[0]
    tm = TOKEN_ROWS
    deps = tuple(deps)

    def body(dx_ref, o_ref, g_ref, w_ref, dyc_ref, do_ref, dg_ref):
        @pl.when(pl.program_id(0) == 0)
        def _():
            dg_ref[...] = jnp.zeros_like(dg_ref)

        dxb = dx_ref[...]
        dyc_ref[...] = _dot_nt(dxb, w_ref[0:D_CONV, :])
        dya = _dot_nt(dxb, w_ref[D_CONV:, :])
        ov = o_ref[...]
        do, dg_rows = _rms_bwd(ov, _rms_stats(ov), g_ref[...], dya)
        do_ref[...] = do
        dg_ref[...] += jnp.sum(dg_rows, axis=0, keepdims=True)

    return pl.pallas_call(
        _skip(len(deps), body), name="out_proj_bwd", grid=(t // tm,),
        in_specs=[_ANY] * len(deps) + [_row_spec(tm, D_MODEL), _row_spec(tm, D_ATTN), _full_spec((1, D_ATTN)),
                                       _full_spec(w_out.shape)],
        out_specs=[_row_spec(tm, D_CONV), _row_spec(tm, D_ATTN), _full_spec((1, D_ATTN))],
        out_shape=[jax.ShapeDtypeStruct((t, D_CONV), F32), jax.ShapeDtypeStruct((t, D_ATTN), F32),
                   jax.ShapeDtypeStruct((1, D_ATTN), F32)],
        compiler_params=_params(dimension_semantics=("arbitrary",)),
    )(*deps, dx2, o, g_attn, w_out)


def _loss_bwd(x3, target, g):
    t = x3.shape[0]
    tm = TOKEN_ROWS

    def body(x_ref, t_ref, g_ref, loss_ref, dx_ref, dg_ref):
        @pl.when(pl.program_id(0) == 0)
        def _():
            loss_ref[...] = jnp.zeros_like(loss_ref)
            dg_ref[...] = jnp.zeros_like(dg_ref)

        xv = x_ref[...]
        r = _rms_stats(xv)
        gv = g_ref[...]
        err = xv * r * gv - t_ref[...]
        row = jnp.sum(err * err, axis=1, keepdims=True) * (0.5 / D_MODEL)
        loss_ref[...] += jnp.sum(row, axis=0, keepdims=True)
        dx, dg_rows = _rms_bwd(xv, r, gv, err * (1.0 / D_MODEL))
        dx_ref[...] = dx
        dg_ref[...] += jnp.sum(dg_rows, axis=0, keepdims=True)

    return pl.pallas_call(
        body, name="loss_bwd", grid=(t // tm,),
        in_specs=[_row_spec(tm, D_MODEL), _row_spec(tm, D_MODEL), _full_spec((1, D_MODEL))],
        out_specs=[_full_spec((1, LANES)), _row_spec(tm, D_MODEL), _full_spec((1, D_MODEL))],
        out_shape=[jax.ShapeDtypeStruct((1, LANES), F32), jax.ShapeDtypeStruct((t, D_MODEL), F32),
                   jax.ShapeDtypeStruct((1, D_MODEL), F32)],
        compiler_params=_params(dimension_semantics=("arbitrary",)),
    )(x3, target, g)


def _split_w_in(w_in_t):
    w_ag = w_in_t[:2 * D_CONV]
    w_qkv = w_in_t[2 * D_CONV:2 * D_CONV + 3 * D_ATTN]
    w_f = jnp.pad(w_in_t[2 * D_CONV + 3 * D_ATTN:], ((0, LANES - N_HEADS), (0, 0)))
    return w_ag, w_qkv, w_f


def _head_rows(v):
    return jnp.pad(v, ((0, HEAD_ROWS - N_HEADS),) + ((0, 0),) * (v.ndim - 1))


def _local_step(x, target, p, get_weights, put_grads, flush_grads):
    t = x.shape[0]
    fb = _head_rows(p["forget_b"].reshape(N_HEADS, 1))

    w, deps = get_weights("ffn1_w13", None)
    h1, gu1, act1 = _ffn_up(x, p["ffn1_norm"], w["ffn1_w13"], "ffn1_up", deps)
    w2, _ = get_weights("ffn1_w2", act1)
    w.update(w2)
    x1 = _ffn_down(x, act1, w["ffn1_w2"], "ffn1_down")
    wm, _ = get_weights("mix", x1)
    w.update(wm)
    w_ag, w_qkv, w_f = _split_w_in(w["w_in"])
    conv_w = jnp.pad(w["conv_w"], ((0, CONV_PAD - CONV_WIDTH), (0, 0)))
    h2, ag, qkv, fl = _mix_proj(x1, p["mix_norm"], w_ag, w_qkv, w_f)
    flt = _head_rows(fl[:, :N_HEADS].T)
    dcum = _gates_fwd(flt, fb)[:N_HEADS]
    drow = dcum.reshape(N_HEADS, 1, t)
    ycn, y_conv = _conv_fwd(ag, conv_w, p["conv_b"], p["conv_ln_g"], p["conv_ln_b"], p["out_norm_conv"])
    o, lse = _attn_fwd(qkv, drow, [ycn])
    _, deps = get_weights("ffn2:landed", o)
    x2, yan = _out_proj(ycn, o, p["out_norm_attn"], w["w_out"], x1, deps)
    w2, _ = get_weights("ffn2", x2)
    w.update(w2)
    x3, h3, gu2, act2 = _ffn_fwd(x2, p["ffn2_norm"], w["ffn2_w13"], w["ffn2_w2"], "ffn2_fwd")
    loss, dx3, d_final = _loss_bwd(x3, target, p["final_norm"])

    g = {}
    dx2, dgu2, g["ffn2_norm"], dx3_half, dx2_bf16 = _ffn_bwd(
        dx3, x2, gu2, p["ffn2_norm"], w["ffn2_w13"], w["ffn2_w2"], "ffn2_bwd")
    dw13 = _wgrad(h3, dgu2, N_CHIPS, "ffn2_dw13")
    dw2 = _wgrad(act2, dx3_half, 1, "ffn2_dw2").reshape(D_FF, D_MODEL)
    deps = put_grads("ffn2", {"ffn2_w13": dw13, "ffn2_w2": dw2})
    dyc, do, g["out_norm_attn"] = _out_proj_bwd(dx2_bf16, o, p["out_norm_attn"], w["w_out"], deps)
    deps = flush_grads("ffn2", [dyc])
    dw_out = _wgrad(jnp.concatenate([ycn, yan], axis=1), dx2_bf16, 1, "dw_out", deps).reshape(D_MODEL, D_MODEL)
    dq, dk, dv, ddrow = _attn_bwd(qkv, drow, lse, do)
    dflt, dfb = _gates_bwd(_head_rows(ddrow.reshape(N_HEADS, t)), flt, fb)
    g["forget_b"] = dfb[:N_HEADS, 0].reshape(1, N_HEADS)
    dfl = jnp.pad(dflt[:N_HEADS].T, ((0, 0), (0, LANES - N_HEADS)))
    dag, dconv_w, g["conv_b"], g["conv_ln_g"], g["conv_ln_b"], g["out_norm_conv"] = _conv_bwd(
        ag, y_conv, dyc, conv_w, p["conv_ln_g"], p["conv_ln_b"], p["out_norm_conv"])
    g["conv_w"] = dconv_w[:CONV_WIDTH]
    dproj = jnp.concatenate([dag, dq, dk, dv, dfl.astype(BF16)], axis=1)
    dx1, g["mix_norm"] = _mix_proj_bwd(dproj, dx2, x1, p["mix_norm"], w_ag, w_qkv, w_f)
    dw_in = _wgrad(dproj, h2, 1, "dw_in").reshape(dproj.shape[1], D_MODEL)[:N_IN]
    deps = put_grads("mix", {"w_in": dw_in, "w_out": dw_out})
    dx0, dgu1, g["ffn1_norm"], dx1_half, _ = _ffn_bwd(
        dx1, x, gu1, p["ffn1_norm"], w["ffn1_w13"], w["ffn1_w2"], "ffn1_bwd", deps)
    g["final_norm"] = d_final
    g["loss"] = loss[:, :1]
    deps = flush_grads("mix", put_grads("small", g))
    dw2 = _wgrad(act1, dx1_half, 1, "ffn1_dw2", deps).reshape(D_FF, D_MODEL)
    deps = flush_grads("ffn1_w2", put_grads("ffn1_w2", {"ffn1_w2": dw2}))
    dw13 = _wgrad(h1, dgu1, N_CHIPS, "ffn1_dw13", deps)
    put_grads("ffn1_w13", {"ffn1_w13": dw13})
    return dx0


MESH = pl.DeviceIdType.MESH


def _place():
    x, y, c = lax.axis_index("x"), lax.axis_index("y"), lax.axis_index("c")
    chips = [(1 - x, y), (x, 1 - y), (1 - x, 1 - y)]
    return x, y, c, chips


def _hbm_out(shape, dtype):
    return jax.ShapeDtypeStruct(shape, dtype)


def _comm_call(body, name, ins, out_shapes, n_remote, in_place=False):
    return pl.pallas_call(
        body, name=name, in_specs=[_ANY] * len(ins), out_specs=[_ANY] * len(out_shapes), out_shape=out_shapes,
        scratch_shapes=[pltpu.SemaphoreType.DMA((n_remote,)), pltpu.SemaphoreType.DMA((n_remote,))],
        input_output_aliases={i: i for i in range(len(ins))} if in_place else {},
    )(*ins)


def _remote(src, dst, sems, n, to):
    send_sems, recv_sems = sems
    return pltpu.make_async_remote_copy(src_ref=src, dst_ref=dst, send_sem=send_sems.at[n], recv_sem=recv_sems.at[n],
                                        device_id=to, device_id_type=MESH)


HALF_ROWS_MULTIPLE = 32


def _halved_by_rows(rows):
    return rows % HALF_ROWS_MULTIPLE == 0


def _half_shape(rows, cols):
    return (rows // 2, cols) if _halved_by_rows(rows) else (rows, cols // 2)


def _half_index(rows, core):
    return (core, 0) if _halved_by_rows(rows) else (0, core)


def _half_of(ref, rows, cols, core, *lead):
    if _halved_by_rows(rows):
        return ref.at[(*lead, pl.ds(core * (rows // 2), rows // 2), slice(None))]
    return ref.at[(*lead, slice(None), pl.ds(core * (cols // 2), cols // 2))]


def _into_slot(shard, chip, dtype, name, deps=()):
    rows, cols = shard.shape
    half = _half_shape(rows, cols)
    by_rows = _halved_by_rows(rows)
    deps = tuple(deps)

    def body(k_ref, *refs):
        s_ref, o_ref = refs[len(deps):]
        o_ref[0] = s_ref[...].astype(dtype)

    return pl.pallas_call(
        body, name=name,
        grid_spec=pltpu.PrefetchScalarGridSpec(
            num_scalar_prefetch=1, grid=(2,),
            in_specs=[_ANY] * len(deps) + [pl.BlockSpec(half, lambda i, k_ref: (i, 0) if by_rows else (0, i))],
            out_specs=pl.BlockSpec((1,) + half, lambda i, k_ref: (k_ref[0], i, 0) if by_rows else (k_ref[0], 0, i))),
        out_shape=jax.ShapeDtypeStruct((N_CHIPS, rows, cols), dtype),
        compiler_params=_params(dimension_semantics=("arbitrary",)),
    )(chip, *deps, shard)


def _run_copies(name, bufs, n_copies, plan):
    n = len(bufs)

    def body(*refs):
        copies = plan(refs[n:2 * n], refs[2 * n:2 * n + 2])
        for send, _ in copies:
            send.start()
        for send, recv in copies:
            send.wait_send()
            recv.wait_recv()

    return _comm_call(body, name, bufs, [_hbm_out(b.shape, b.dtype) for b in bufs], n_copies, in_place=True)


def _forward_halves(slots, name):
    return _run_copies(name, slots, 3 * len(slots), _d2d_forward_plan(slots))


_HBM = pl.BlockSpec(memory_space=pltpu.HBM)
_SEM = pl.BlockSpec(memory_space=pltpu.SEMAPHORE)
_DATAFLOW = pltpu.SideEffectType.DATAFLOW_SIDE_EFFECTING


def _split_copy_start(name, bufs, n_copies, plan):
    n = len(bufs)

    def body(*refs):
        for send, _ in plan(refs[:n], (refs[n], refs[n + 1])):
            send.start()
        token = refs[-1]
        token[...] = jnp.zeros_like(token)

    out = pl.pallas_call(
        body, name=name,
        out_shape=(pltpu.SemaphoreType.DMA((n_copies,)), pltpu.SemaphoreType.DMA((n_copies,)),
                   *[pltpu.HBM(b.shape, b.dtype) for b in bufs], jax.ShapeDtypeStruct((8, LANES), F32)),
        in_specs=[_HBM] * n, out_specs=(_SEM, _SEM, *[_HBM] * n, pl.BlockSpec(memory_space=pltpu.VMEM)),
        input_output_aliases={i: 2 + i for i in range(n)},
        compiler_params=pltpu.CompilerParams(has_side_effects=_DATAFLOW),
    )(*[pltpu.with_memory_space_constraint(b, pltpu.HBM) for b in bufs])
    return out[0], out[1], list(out[2:2 + n]), out[-1]


def _split_copy_wait(name, started, plan, after, passed=()):
    send_sems, recv_sems, bufs, _ = started
    n = len(bufs)
    after = tuple(after)
    bufs = list(bufs) + list(passed)
    total = len(bufs)

    def body(*refs):
        for send, recv in plan(refs[:n], (refs[total], refs[total + 1])):
            send.wait_send()
            recv.wait_recv()

    out = pl.pallas_call(
        body, name=name, out_shape=tuple(pltpu.HBM(b.shape, b.dtype) for b in bufs),
        in_specs=[_HBM] * total + [_SEM, _SEM] + [_ANY] * len(after), out_specs=tuple([_HBM] * total),
        input_output_aliases={i: i for i in range(total)},
        compiler_params=pltpu.CompilerParams(has_side_effects=_DATAFLOW),
    )(*bufs, send_sems, recv_sems, *after)
    return list(out)


def _ici_gather_plan(slots):
    def plan(refs, sems):
        x, y, c, chips = _place()
        me = 2 * x + y
        copies = []
        for i, ref in enumerate(refs):
            for j, chip in enumerate(chips):
                mine = _half_of(ref, *slots[i].shape[1:], c, me)
                theirs = _half_of(ref, *slots[i].shape[1:], c, 2 * chip[0] + chip[1])
                to = (*chip, c)
                copies.append((_remote(mine, mine, sems, 3 * i + j, to), _remote(theirs, theirs, sems, 3 * i + j, to)))
        return copies

    return plan


def _ici_scatter_plan(n):
    def plan(refs, sems):
        x, y, c, chips = _place()
        copies = []
        for i in range(n):
            for j, chip in enumerate(chips):
                cp = _remote(refs[i].at[2 * chip[0] + chip[1]], refs[n + i].at[j], sems, 3 * i + j, (*chip, c))
                copies.append((cp, cp))
        return copies

    return plan


def _d2d_forward_plan(slots):
    def plan(refs, sems):
        x, y, c, chips = _place()
        sibling = (x, y, 1 - c)
        copies = []
        for i, ref in enumerate(refs):
            for j, chip in enumerate(chips):
                src_chip = 2 * chip[0] + chip[1]
                mine = _half_of(ref, *slots[i].shape[1:], c, src_chip)
                theirs = _half_of(ref, *slots[i].shape[1:], 1 - c, src_chip)
                copies.append((_remote(mine, mine, sems, 3 * i + j, sibling),
                               _remote(theirs, theirs, sems, 3 * i + j, sibling)))
        return copies

    return plan


def _pair_exchange_plan(grads):
    n = len(grads)

    def plan(refs, sems):
        x, y, c, _ = _place()
        copies = []
        for i in range(n):
            theirs = _half_of(refs[i], *grads[i].shape[1:], 1 - c, slice(None))
            cp = _remote(theirs, refs[n + i], sems, i, (x, y, 1 - c))
            copies.append((cp, cp))
        return copies

    return plan


def _pair_share_plan(shapes):
    def plan(refs, sems):
        x, y, c, _ = _place()
        sibling = (x, y, 1 - c)
        copies = []
        for i, ref in enumerate(refs):
            mine, theirs = _half_of(ref, *shapes[i], c), _half_of(ref, *shapes[i], 1 - c)
            copies.append((_remote(mine, mine, sems, i, sibling), _remote(theirs, theirs, sems, i, sibling)))
        return copies

    return plan


def _pair_share(halves, name):
    return _run_copies(name, halves, len(halves), _pair_share_plan([h.shape for h in halves]))


N_DEVICES = 8
FLIPS = [(fx, fy, fc) for fx in range(2) for fy in range(2) for fc in range(2)][1:]


def _small_slots(v, me):
    rows = v.shape[0]

    def body(k_ref, v_ref, o_ref):
        o_ref[0] = v_ref[...]

    return pl.pallas_call(
        body, name="small_slot",
        grid_spec=pltpu.PrefetchScalarGridSpec(
            num_scalar_prefetch=1, grid=(1,),
            in_specs=[pl.BlockSpec((rows, LANES), lambda i, k_ref: (0, 0))],
            out_specs=pl.BlockSpec((1, rows, LANES), lambda i, k_ref: (k_ref[0], 0, 0))),
        out_shape=jax.ShapeDtypeStruct((N_DEVICES, rows, LANES), F32),
        compiler_params=_params(dimension_semantics=("arbitrary",)),
    )(me, v)


def _small_plan():
    def plan(refs, sems):
        x, y, c, _ = _place()
        slots = refs[0]
        me = 4 * x + 2 * y + c
        copies = []
        for n, (fx, fy, fc) in enumerate(FLIPS):
            to = (x ^ fx, y ^ fy, c ^ fc)
            src = 4 * to[0] + 2 * to[1] + to[2]
            copies.append((_remote(slots.at[me], slots.at[me], sems, n, to), _remote(slots.at[src], slots.at[src], sems, n, to)))
        return copies

    return plan


def _small_sum(slots):
    def body(s_ref, o_ref):
        acc = s_ref[0]
        for s in range(1, N_DEVICES):
            acc = acc + s_ref[s]
        o_ref[...] = acc

    return pl.pallas_call(body, name="small_sum", out_shape=jax.ShapeDtypeStruct(slots.shape[1:], F32),
                          compiler_params=_params())(slots)


def _pair_add(gs, sibs, core, name):
    n = len(gs)
    halves = [_half_shape(*g.shape[1:]) for g in gs]

    def body(c_ref, *refs):
        for g_ref, s_ref, o_ref in zip(refs[:n], refs[n:2 * n], refs[2 * n:]):
            o_ref[0] = (g_ref[0].astype(F32) + s_ref[0].astype(F32)).astype(BF16)

    def mine(g, half):
        return pl.BlockSpec((1,) + half, lambda s, c_ref: (s, *_half_index(g.shape[1], c_ref[0])))

    whole = [pl.BlockSpec((1,) + half, lambda s, c_ref: (s, 0, 0)) for half in halves]
    return pl.pallas_call(
        body, name=name,
        grid_spec=pltpu.PrefetchScalarGridSpec(
            num_scalar_prefetch=1, grid=(N_CHIPS,),
            in_specs=[mine(g, half) for g, half in zip(gs, halves)] + whole, out_specs=whole),
        out_shape=[jax.ShapeDtypeStruct((N_CHIPS,) + half, BF16) for half in halves],
        compiler_params=_params(dimension_semantics=("arbitrary",)),
    )(core, *gs, *sibs)


def _chip_add(parts, recvs, chip_core, shapes, name):
    n = len(parts)
    halves = [_half_shape(*shape) for shape in shapes]

    def body(kc_ref, *refs):
        for p_ref, r_ref, o_ref in zip(refs[:n], refs[n:2 * n], refs[2 * n:]):
            acc = p_ref[0].astype(F32)
            for j in range(N_CHIPS - 1):
                acc = acc + r_ref[j].astype(F32)
            o_ref[...] = acc

    def out_spec(shape, half):
        return pl.BlockSpec(half, lambda s, kc_ref: _half_index(shape[0], kc_ref[1]))

    return pl.pallas_call(
        body, name=name,
        grid_spec=pltpu.PrefetchScalarGridSpec(
            num_scalar_prefetch=1, grid=(1,),
            in_specs=[pl.BlockSpec((1,) + half, lambda s, kc_ref: (kc_ref[0], 0, 0)) for half in halves]
            + [pl.BlockSpec((N_CHIPS - 1,) + half, lambda s, kc_ref: (0, 0, 0)) for half in halves],
            out_specs=[out_spec(shape, half) for shape, half in zip(shapes, halves)]),
        out_shape=[jax.ShapeDtypeStruct(tuple(shape), F32) for shape in shapes],
        compiler_params=_params(dimension_semantics=("arbitrary",)),
    )(chip_core, *parts, *recvs)


def _adamw_math(w, g, m, v):
    m = ADAM_B1 * m + (1.0 - ADAM_B1) * g
    v = ADAM_B2 * v + (1.0 - ADAM_B2) * (g * g)
    m_hat = m / (1.0 - ADAM_B1 ** ADAM_STEP)
    v_hat = v / (1.0 - ADAM_B2 ** ADAM_STEP)
    delta = -ADAM_LR * (m_hat / (jnp.sqrt(v_hat) + ADAM_EPS) + ADAM_WD * w)
    return delta, m, v


ADAM_PARTS = 2


def _adamw_matrix(w, g, m, v, name):
    rows, cols = w.shape
    by_rows = rows % (8 * ADAM_PARTS) == 0
    block = (rows // ADAM_PARTS, cols) if by_rows else (rows, cols // ADAM_PARTS)

    def body(w_ref, g_ref, m_ref, v_ref, go_ref, d_ref, mo_ref, vo_ref):
        gv = g_ref[...]
        go_ref[...] = gv
        d_ref[...], mo_ref[...], vo_ref[...] = _adamw_math(w_ref[...], gv, m_ref[...], v_ref[...])

    spec = pl.BlockSpec(block, lambda i: (i, 0) if by_rows else (0, i))
    shape = jax.ShapeDtypeStruct((rows, cols), F32)
    return pl.pallas_call(
        body, name=name, grid=(ADAM_PARTS,), in_specs=[spec] * 4, out_specs=[spec] * 4, out_shape=[shape] * 4,
        compiler_params=_params(dimension_semantics=("arbitrary",)),
    )(w, g, m, v)


def _adamw_small(ws, gs, ms, vs):
    n = len(ws)

    def body(*refs):
        for i in range(n):
            w_ref, g_ref, m_ref, v_ref = (refs[k * n + i] for k in range(4))
            d_ref, mo_ref, vo_ref = (refs[(4 + k) * n + i] for k in range(3))
            d_ref[...], mo_ref[...], vo_ref[...] = _adamw_math(w_ref[...], g_ref[...], m_ref[...], v_ref[...])

    shapes = [jax.ShapeDtypeStruct(w.shape, F32) for w in ws]
    out = pl.pallas_call(body, name="adamw_small", out_shape=shapes * 3, compiler_params=_params())(*ws, *gs, *ms, *vs)
    return out[:n], out[n:2 * n], out[2 * n:]


MATRICES = ["ffn1_w13", "ffn1_w2", "w_in", "w_out", "ffn2_w13", "ffn2_w2"]
VECTORS = ["ffn1_norm", "mix_norm", "conv_b", "conv_ln_g", "conv_ln_b", "forget_b", "out_norm_conv",
           "out_norm_attn", "ffn2_norm", "final_norm"]
WEIGHTS = ["ffn1_norm", "ffn1_w13", "ffn1_w2", "mix_norm", "w_in", "conv_w", "conv_b", "conv_ln_g", "conv_ln_b",
           "forget_b", "out_norm_conv", "out_norm_attn", "w_out", "ffn2_norm", "ffn2_w13", "ffn2_w2", "final_norm"]


def _pack_small(g, names):
    rows, layout = [], []
    for n in names:
        flat = g[n].reshape(-1)
        pad = (-flat.shape[0]) % LANES
        rows.append(jnp.pad(flat, (0, pad)).reshape(-1, LANES))
        layout.append((n, g[n].shape, flat.shape[0], rows[-1].shape[0]))
    packed = jnp.concatenate(rows, axis=0)
    pad_rows = (-packed.shape[0]) % 8
    return jnp.pad(packed, ((0, pad_rows), (0, 0))), layout


def _unpack_small(packed, layout):
    out, r = {}, 0
    for n, shape, size, nrows in layout:
        out[n] = packed[r:r + nrows].reshape(-1)[:size].reshape(shape)
        r += nrows
    return out


def kernel(x, ffn1_norm, ffn1_w13, ffn1_w2, mix_norm, w_in, conv_w, conv_b, conv_ln_g, conv_ln_b, forget_b, out_norm_conv, out_norm_attn, w_out, ffn2_norm, ffn2_w13, ffn2_w2, final_norm, loss_target, m_ffn1_norm, m_ffn1_w13, m_ffn1_w2, m_mix_norm, m_w_in, m_conv_w, m_conv_b, m_conv_ln_g, m_conv_ln_b, m_forget_b, m_out_norm_conv, m_out_norm_attn, m_w_out, m_ffn2_norm, m_ffn2_w13, m_ffn2_w2, m_final_norm, v_ffn1_norm, v_ffn1_w13, v_ffn1_w2, v_mix_norm, v_w_in, v_conv_w, v_conv_b, v_conv_ln_g, v_conv_ln_b, v_forget_b, v_out_norm_conv, v_out_norm_attn, v_w_out, v_ffn2_norm, v_ffn2_w13, v_ffn2_w2, v_final_norm):
    args = dict(locals())
    weights = {n: args[n] for n in WEIGHTS}
    core = lax.axis_index("c").astype(jnp.int32).reshape(1)
    chip = (2 * lax.axis_index("x") + lax.axis_index("y")).astype(jnp.int32)
    chip1 = chip.reshape(1)
    chip_core = jnp.concatenate([chip1, core])

    def held(n, a):
        return a[0].T if n == "w_in" else a[0]

    def given(n, a):
        return (a.T if n == "w_in" else a)[None]

    def slot(n, deps=()):
        if n == "conv_w":
            rows = jnp.pad(conv_w[0], ((0, CONV_PAD - CONV_WIDTH), (0, 0)))
            return _into_slot(rows, chip1, F32, "slot_conv_w", deps)
        return _into_slot(held(n, weights[n]), chip1, BF16, "slot_" + n, deps)

    fetched = {"ffn1_w13": ["ffn1_w13"], "ffn1_w2": ["ffn1_w2"], "mix": ["w_in", "w_out", "conv_w"],
               "ffn2": ["ffn2_w13", "ffn2_w2"]}
    fetch = {}

    def as_weights(group, bufs):
        out = {}
        for n, b in zip(fetched[group], bufs):
            if n.endswith("w13"):
                out[n] = b
            elif n != "conv_w":
                out[n] = b.reshape(N_CHIPS * b.shape[1], b.shape[2])
            else:
                out[n] = b[:, :CONV_WIDTH].transpose(1, 0, 2).reshape(CONV_WIDTH, D_CONV)
        return out

    def get_weights(group, after):
        if group == "ffn1_w13":
            first = [slot("ffn1_w13")]
            plan = _ici_gather_plan(first)
            started = _split_copy_start("gather_ffn1_w13_start", first, 3, plan)
            second = [slot("ffn1_w2", [started[3]])]
            plan2 = _ici_gather_plan(second)
            fetch["ffn1_w2"] = plan2, _split_copy_start("gather_ffn1_w2_start", second, 3, plan2)
            later_names = fetched["mix"] + fetched["ffn2"]
            later = [slot(n, [fetch["ffn1_w2"][1][3]]) for n in later_names]
            landed = _split_copy_wait("gather_ffn1_w13_wait", started, plan, [], passed=later)
            bufs = _forward_halves(landed[:1], "forward_ffn1_w13")
            behind = dict(zip(later_names, landed[1:]))
            for later in ("mix", "ffn2"):
                bufs_later = [behind[n] for n in fetched[later]]
                plan = _ici_gather_plan(bufs_later)
                fetch[later] = plan, _split_copy_start("gather_%s_start" % later, bufs_later, 3 * len(bufs_later), plan)
            return as_weights(group, bufs), [fetch["mix"][1][3], fetch["ffn2"][1][3]]
        plan, started = fetch[group.split(":")[0]]
        if group == "ffn2:landed":
            landed = _split_copy_wait("gather_ffn2_wait", started, plan, [after])
            plan = _d2d_forward_plan(landed)
            fetch["ffn2"] = plan, _split_copy_start("forward_ffn2_start", landed, 3 * len(landed), plan)
            return {}, [fetch["ffn2"][1][3]]
        if group == "ffn2":
            return as_weights(group, _split_copy_wait("forward_ffn2_wait", started, plan, [after])), []
        landed = _split_copy_wait("gather_%s_wait" % group, started, plan, [after])
        return as_weights(group, _forward_halves(landed, "forward_" + group)), []

    def shard_major(n, g):
        return g if n.endswith("w13") else g.reshape(N_CHIPS, g.shape[0] // N_CHIPS, g.shape[1])

    exchange, scatter = {}, {}
    small_names = VECTORS + ["conv_w"]
    small = {}

    def put_grads(group, grads):
        if group == "small":
            packed, layout = _pack_small(grads, small_names + ["loss"])
            me = (4 * lax.axis_index("x") + 2 * lax.axis_index("y") + lax.axis_index("c")).astype(jnp.int32).reshape(1)
            plan = _small_plan()
            exchange[group] = layout, plan, _split_copy_start("small_start", [_small_slots(packed, me)], len(FLIPS), plan)
            return [exchange[group][2][3]]
        names = list(grads)
        local = [shard_major(n, grads[n]) for n in names]
        landing = [lax.empty((N_CHIPS,) + _half_shape(*a.shape[1:]), BF16) for a in local]
        plan = _pair_exchange_plan(local)
        exchange[group] = names, plan, _split_copy_start("exchange_%s_start" % group, local + landing, len(local), plan)
        return [exchange[group][2][3]]

    def flush_grads(group, after):
        names, plan, started = exchange[group]
        done = _split_copy_wait("exchange_%s_wait" % group, started, plan, after)
        local, sib = done[:len(names)], done[len(names):]
        parts = list(_pair_add(local, sib, core, "pair_add_" + group))
        landing = [lax.empty((N_CHIPS - 1,) + q.shape[1:], BF16) for q in parts]
        plan = _ici_scatter_plan(len(parts))
        shapes = [a.shape[1:] for a in local]
        scatter[group] = names, plan, _split_copy_start("scatter_%s_start" % group, parts + landing, 3 * len(parts), plan), shapes
        return [scatter[group][2][3]]

    p = {n: weights[n] for n in VECTORS}
    p["final_norm"] = final_norm.reshape(1, D_MODEL)
    dx = _local_step(x[0], loss_target[0], p, get_weights, put_grads, flush_grads)
    layout, plan, started = exchange["small"]
    slots, = _split_copy_wait("small_wait", started, plan, [exchange["ffn1_w13"][2][3]])
    small.update(_unpack_small(_small_sum(slots), layout))
    loss = small["loss"].reshape(())

    grad = {n: small[n] for n in VECTORS}
    grad["final_norm"] = small["final_norm"].reshape(D_MODEL)
    grad["conv_w"] = lax.dynamic_slice_in_dim(small["conv_w"], chip * (D_CONV // N_CHIPS), D_CONV // N_CHIPS, axis=1)[None]

    delta, new_m, new_v = {}, {}, {}

    def reduce_chips(group, after):
        names, plan, started, shapes = scatter[group]
        done = _split_copy_wait("scatter_%s_wait" % group, started, plan, after)
        parts, landed = done[:len(names)], done[len(names):]
        return list(_chip_add(parts, landed, chip_core, shapes, "chip_add_" + group))

    def update(group, full):
        ends = []
        for n, reduced in zip(scatter[group][0], full):
            go, d, mo, vo = _adamw_matrix(held(n, weights[n]), reduced, held(n, args["m_" + n]), held(n, args["v_" + n]),
                                          "adamw_" + n)
            grad[n], delta[n], new_m[n], new_v[n] = given(n, go), given(n, d), given(n, mo), given(n, vo)
            ends.append(vo)
        return ends

    def share_start(group, halves):
        plan = _pair_share_plan(scatter[group][3])
        return plan, _split_copy_start("share_%s_start" % group, halves, len(halves), plan)

    halves_ffn2 = reduce_chips("ffn2", [exchange["ffn1_w13"][2][3]])
    plan_ffn2, share_ffn2 = share_start("ffn2", halves_ffn2)
    last_scatter = flush_grads("ffn1_w13", [share_ffn2[3]])
    halves_mix = reduce_chips("mix", last_scatter)
    plan_mix, share_mix = share_start("mix", halves_mix)
    done_ffn2 = update("ffn2", _split_copy_wait("share_ffn2_wait", share_ffn2, plan_ffn2, [share_mix[3]]))
    done_mix = update("mix", _split_copy_wait("share_mix_wait", share_mix, plan_mix, done_ffn2))
    as2d = lambda a: a.reshape(-1, a.shape[-1])
    ds, mos, vos = _adamw_small([as2d(weights[n]) for n in small_names], [as2d(grad[n]) for n in small_names],
                                [as2d(args["m_" + n]) for n in small_names], [as2d(args["v_" + n]) for n in small_names])
    for n, d, mo, vo in zip(small_names, ds, mos, vos):
        shape = weights[n].shape
        delta[n], new_m[n], new_v[n] = d.reshape(shape), mo.reshape(shape), vo.reshape(shape)
    behind = done_ffn2 + done_mix + [vos[0]]
    halves_w2 = reduce_chips("ffn1_w2", behind)
    halves_w13 = reduce_chips("ffn1_w13", behind)
    full_w2, full_w13 = _pair_share(halves_w2 + halves_w13, "pair_share_ffn1")
    update("ffn1_w2", [full_w2])
    update("ffn1_w13", [full_w13])

    return (loss, dx[None], *[grad[n] for n in WEIGHTS], *[delta[n] for n in WEIGHTS],
            *[new_m[n] for n in WEIGHTS], *[new_v[n] for n in WEIGHTS])
```

```python
import jax
import jax.numpy as jnp
from jax import lax
from jax.experimental import pallas as pl
from jax.experimental.pallas import tpu as pltpu

F32 = jnp.float32
BF16 = jnp.bfloat16

D_MODEL = 1024
D_FF = 2816
FF_SHARD = D_FF // 2
D_CONV = 512
D_ATTN = 512
N_HEADS = 8
HEAD_DIM = 64
CONV_WIDTH = 31
CONV_PAD = 32
N_IN = 2 * D_CONV + 3 * D_ATTN + N_HEADS
EPS = 1e-6
N_CHIPS = 4
LANES = 128
TOKEN_ROWS = 512
HEAD_ROWS = 16

ADAM_LR = 0.001
ADAM_B1 = 0.9
ADAM_B2 = 0.999
ADAM_EPS = 1e-08
ADAM_WD = 0.01
ADAM_STEP = 10

VMEM_LIMIT = 56 * 1024 * 1024

_NT = (((1,), (1,)), ((), ()))
_TN = (((0,), (0,)), ((), ()))


def _dot(a, b):
    return jnp.dot(a, b, preferred_element_type=F32)


def _dot_nt(a, b):
    return lax.dot_general(a, b, _NT, preferred_element_type=F32)


def _dot_tn(a, b):
    return lax.dot_general(a, b, _TN, preferred_element_type=F32)


def _params(**kw):
    return pltpu.CompilerParams(vmem_limit_bytes=VMEM_LIMIT, **kw)


def _sigmoid(x):
    return 1.0 / (1.0 + jnp.exp(-x))


def _rms_stats(x):
    return lax.rsqrt(jnp.mean(x * x, axis=-1, keepdims=True) + EPS)


def _rms_bwd(x, r, g, dh):
    t = dh * g
    dx = r * t - x * (r * r * r) * jnp.mean(t * x, axis=-1, keepdims=True)
    return dx, dh * x * r


def _silu_grad(z, sg):
    return sg * (1.0 + z * (1.0 - sg))


def _row_spec(tm, n):
    return pl.BlockSpec((tm, n), lambda i: (i, 0))


def _full_spec(shape):
    nd = len(shape)
    return pl.BlockSpec(shape, lambda i: (0,) * nd)


_ANY = pl.BlockSpec(memory_space=pl.ANY)


def _skip(n, body):
    return lambda *refs: body(*refs[n:])


FFN_ROWS = 256
FFN_WEIGHT_PARTS = N_CHIPS + 2


def _with_ffn_weights(w13_hbm, w2_hbm, w13_ref, w2_ref, sems, order, tile):
    first = pl.program_id(0) == 0
    copies = {}
    if w13_hbm is not None:
        for k in range(N_CHIPS):
            copies["w13", k] = pltpu.make_async_copy(w13_hbm.at[k], w13_ref.at[k], sems.at[k])
    if w2_hbm is not None:
        for half in range(2):
            rows = pl.ds(half * FF_SHARD, FF_SHARD)
            copies["w2", half] = pltpu.make_async_copy(w2_hbm.at[rows, :], w2_ref.at[rows, :], sems.at[N_CHIPS + half])

    @pl.when(first)
    def _():
        for part in order:
            copies[part].start()

        def ready(*parts):
            for part in parts:
                copies[part].wait()

        tile(ready)

    @pl.when(jnp.logical_not(first))
    def _():
        tile(lambda *parts: None)


def _ffn_fwd(x, g, w13s, w2, name, deps=()):
    t = x.shape[0]
    tm = FFN_ROWS
    deps = tuple(deps)

    def body(x_ref, g_ref, w13_hbm, w2_hbm, xo_ref, h_ref, gu_ref, a_ref, w13_ref, w2_ref, sems):
        def tile(ready):
            xv = x_ref[...]
            hb = (xv * _rms_stats(xv) * g_ref[...]).astype(BF16)
            h_ref[...] = hb
            acc = jnp.zeros((tm, D_MODEL), F32)
            for half in range(2):
                lo = half * FF_SHARD
                ready(("w13", half), ("w13", 2 + half))
                gate = _dot(hb, w13_ref[half])
                up = _dot(hb, w13_ref[2 + half])
                gu_ref[:, lo:lo + FF_SHARD] = gate.astype(BF16)
                gu_ref[:, D_FF + lo:D_FF + lo + FF_SHARD] = up.astype(BF16)
                a = (gate * _sigmoid(gate) * up).astype(BF16)
                a_ref[:, lo:lo + FF_SHARD] = a
                ready(("w2", half))
                acc = acc + _dot(a, w2_ref[lo:lo + FF_SHARD, :])
            xo_ref[...] = xv + 0.5 * acc

        _with_ffn_weights(w13_hbm, w2_hbm, w13_ref, w2_ref, sems,
                          [("w13", 0), ("w13", 2), ("w2", 0), ("w13", 1), ("w13", 3), ("w2", 1)], tile)

    return pl.pallas_call(
        _skip(len(deps), body), name=name, grid=(t // tm,),
        in_specs=[_ANY] * len(deps) + [_row_spec(tm, D_MODEL), _full_spec((1, D_MODEL)), _ANY, _ANY],
        out_specs=[_row_spec(tm, D_MODEL), _row_spec(tm, D_MODEL), _row_spec(tm, 2 * D_FF), _row_spec(tm, D_FF)],
        out_shape=[jax.ShapeDtypeStruct((t, D_MODEL), F32), jax.ShapeDtypeStruct((t, D_MODEL), BF16),
                   jax.ShapeDtypeStruct((t, 2 * D_FF), BF16), jax.ShapeDtypeStruct((t, D_FF), BF16)],
        scratch_shapes=[pltpu.VMEM(w13s.shape, BF16), pltpu.VMEM(w2.shape, BF16),
                        pltpu.SemaphoreType.DMA((FFN_WEIGHT_PARTS,))],
        compiler_params=_params(dimension_semantics=("arbitrary",)),
    )(*deps, x, g, w13s, w2)


def _ffn_up(x, g, w13s, name, deps=()):
    t = x.shape[0]
    tm = FFN_ROWS
    deps = tuple(deps)

    def body(x_ref, g_ref, w13_hbm, h_ref, gu_ref, a_ref, w13_ref, sems):
        def tile(ready):
            xv = x_ref[...]
            hb = (xv * _rms_stats(xv) * g_ref[...]).astype(BF16)
            h_ref[...] = hb
            for half in range(2):
                lo = half * FF_SHARD
                ready(("w13", half), ("w13", 2 + half))
                gate = _dot(hb, w13_ref[half])
                up = _dot(hb, w13_ref[2 + half])
                gu_ref[:, lo:lo + FF_SHARD] = gate.astype(BF16)
                gu_ref[:, D_FF + lo:D_FF + lo + FF_SHARD] = up.astype(BF16)
                a_ref[:, lo:lo + FF_SHARD] = (gate * _sigmoid(gate) * up).astype(BF16)

        _with_ffn_weights(w13_hbm, None, w13_ref, None, sems, [("w13", 0), ("w13", 2), ("w13", 1), ("w13", 3)], tile)

    return pl.pallas_call(
        _skip(len(deps), body), name=name, grid=(t // tm,),
        in_specs=[_ANY] * len(deps) + [_row_spec(tm, D_MODEL), _full_spec((1, D_MODEL)), _ANY],
        out_specs=[_row_spec(tm, D_MODEL), _row_spec(tm, 2 * D_FF), _row_spec(tm, D_FF)],
        out_shape=[jax.ShapeDtypeStruct((t, D_MODEL), BF16), jax.ShapeDtypeStruct((t, 2 * D_FF), BF16),
                   jax.ShapeDtypeStruct((t, D_FF), BF16)],
        scratch_shapes=[pltpu.VMEM(w13s.shape, BF16), pltpu.SemaphoreType.DMA((FFN_WEIGHT_PARTS,))],
        compiler_params=_params(dimension_semantics=("arbitrary",)),
    )(*deps, x, g, w13s)


def _ffn_down(x, a, w2, name):
    t = x.shape[0]
    tm = FFN_ROWS

    def body(x_ref, a_ref, w2_hbm, xo_ref, w2_ref, sems):
        def tile(ready):
            ready(("w2", 0))
            acc = _dot(a_ref[:, 0:FF_SHARD], w2_ref[0:FF_SHARD, :])
            ready(("w2", 1))
            acc = acc + _dot(a_ref[:, FF_SHARD:], w2_ref[FF_SHARD:, :])
            xo_ref[...] = x_ref[...] + 0.5 * acc

        _with_ffn_weights(None, w2_hbm, None, w2_ref, sems, [("w2", 0), ("w2", 1)], tile)

    return pl.pallas_call(
        body, name=name, grid=(t // tm,),
        in_specs=[_row_spec(tm, D_MODEL), _row_spec(tm, D_FF), _ANY],
        out_specs=_row_spec(tm, D_MODEL), out_shape=jax.ShapeDtypeStruct((t, D_MODEL), F32),
        scratch_shapes=[pltpu.VMEM(w2.shape, BF16), pltpu.SemaphoreType.DMA((FFN_WEIGHT_PARTS,))],
        compiler_params=_params(dimension_semantics=("arbitrary",)),
    )(x, a, w2)


def _ffn_bwd(dy, x, gu, g, w13s, w2, name, deps=()):
    t = x.shape[0]
    tm = FFN_ROWS
    deps = tuple(deps)

    def body(dy_ref, x_ref, gu_ref, g_ref, w13_hbm, w2_hbm, dx_ref, dgu_ref, dg_ref, dyh_ref, dxb_ref,
             w13_ref, w2_ref, sems):
        @pl.when(pl.program_id(0) == 0)
        def _():
            dg_ref[...] = jnp.zeros_like(dg_ref)

        def tile(ready):
            dyv = dy_ref[...]
            dyh = (0.5 * dyv).astype(BF16)
            dyh_ref[...] = dyh
            dh = jnp.zeros((tm, D_MODEL), F32)
            for half in range(2):
                lo = half * FF_SHARD
                ready(("w2", half))
                da = _dot_nt(dyh, w2_ref[lo:lo + FF_SHARD, :])
                gate = gu_ref[:, lo:lo + FF_SHARD].astype(F32)
                up = gu_ref[:, D_FF + lo:D_FF + lo + FF_SHARD].astype(F32)
                sg = _sigmoid(gate)
                act = gate * sg
                dgate = (da * up * _silu_grad(gate, sg)).astype(BF16)
                dup = (da * act).astype(BF16)
                dgu_ref[:, lo:lo + FF_SHARD] = dgate
                dgu_ref[:, D_FF + lo:D_FF + lo + FF_SHARD] = dup
                ready(("w13", half), ("w13", 2 + half))
                dh = dh + _dot_nt(dgate, w13_ref[half]) + _dot_nt(dup, w13_ref[2 + half])
            xv = x_ref[...]
            dxn, dg_rows = _rms_bwd(xv, _rms_stats(xv), g_ref[...], dh)
            dx = dyv + dxn
            dx_ref[...] = dx
            dxb_ref[...] = dx.astype(BF16)
            dg_ref[...] += jnp.sum(dg_rows, axis=0, keepdims=True)

        _with_ffn_weights(w13_hbm, w2_hbm, w13_ref, w2_ref, sems,
                          [("w2", 0), ("w13", 0), ("w13", 2), ("w2", 1), ("w13", 1), ("w13", 3)], tile)

    return pl.pallas_call(
        _skip(len(deps), body), name=name, grid=(t // tm,),
        in_specs=[_ANY] * len(deps) + [_row_spec(tm, D_MODEL), _row_spec(tm, D_MODEL), _row_spec(tm, 2 * D_FF),
                                       _full_spec((1, D_MODEL)), _ANY, _ANY],
        out_specs=[_row_spec(tm, D_MODEL), _row_spec(tm, 2 * D_FF),
                   _full_spec((1, D_MODEL)), _row_spec(tm, D_MODEL), _row_spec(tm, D_MODEL)],
        out_shape=[jax.ShapeDtypeStruct((t, D_MODEL), F32), jax.ShapeDtypeStruct((t, 2 * D_FF), BF16),
                   jax.ShapeDtypeStruct((1, D_MODEL), F32),
                   jax.ShapeDtypeStruct((t, D_MODEL), BF16), jax.ShapeDtypeStruct((t, D_MODEL), BF16)],
        scratch_shapes=[pltpu.VMEM(w13s.shape, BF16), pltpu.VMEM(w2.shape, BF16),
                        pltpu.SemaphoreType.DMA((FFN_WEIGHT_PARTS,))],
        compiler_params=_params(dimension_semantics=("arbitrary",)),
    )(*deps, dy, x, gu, g, w13s, w2)


WGRAD_ROWS = (1408, 1024, 512, 384, 256)


def _wgrad(a, b, n_blocks, name, deps=()):
    t, m = a.shape
    tm = next(rows for rows in WGRAD_ROWS if m % rows == 0)
    n = b.shape[1]
    bn = n // n_blocks
    deps = tuple(deps)
    assert a.dtype == BF16 and b.dtype == BF16

    def body(a_ref, b_ref, o_ref):
        o_ref[0] = _dot_tn(a_ref[...], b_ref[...]).astype(BF16)

    return pl.pallas_call(
        _skip(len(deps), body), name=name, grid=(n_blocks, m // tm),
        in_specs=[_ANY] * len(deps) + [pl.BlockSpec((t, tm), lambda j, i: (0, i)),
                                       pl.BlockSpec((t, bn), lambda j, i: (0, j))],
        out_specs=pl.BlockSpec((1, tm, bn), lambda j, i: (j, i, 0)),
        out_shape=jax.ShapeDtypeStruct((n_blocks, m, bn), BF16),
        compiler_params=_params(dimension_semantics=("arbitrary", "arbitrary")),
    )(*deps, a, b)


def _mix_proj(x, g, w_ag, w_qkv, w_f):
    t = x.shape[0]
    tm = TOKEN_ROWS

    def body(x_ref, g_ref, wag_ref, wqkv_ref, wf_ref, h_ref, ag_ref, qkv_ref, fl_ref):
        xv = x_ref[...]
        hb = (xv * _rms_stats(xv) * g_ref[...]).astype(BF16)
        h_ref[...] = hb
        ag_ref[...] = _dot_nt(hb, wag_ref[...])
        qkv_ref[...] = _dot_nt(hb, wqkv_ref[...]).astype(BF16)
        fl_ref[...] = _dot_nt(hb, wf_ref[...])

    return pl.pallas_call(
        body, name="mix_proj", grid=(t // tm,),
        in_specs=[_row_spec(tm, D_MODEL), _full_spec((1, D_MODEL)), _full_spec(w_ag.shape),
                  _full_spec(w_qkv.shape), _full_spec(w_f.shape)],
        out_specs=[_row_spec(tm, D_MODEL), _row_spec(tm, 2 * D_CONV), _row_spec(tm, 3 * D_ATTN),
                   _row_spec(tm, LANES)],
        out_shape=[jax.ShapeDtypeStruct((t, D_MODEL), BF16), jax.ShapeDtypeStruct((t, 2 * D_CONV), F32),
                   jax.ShapeDtypeStruct((t, 3 * D_ATTN), BF16), jax.ShapeDtypeStruct((t, LANES), F32)],
        compiler_params=_params(dimension_semantics=("arbitrary",)),
    )(x, g, w_ag, w_qkv, w_f)


def _mix_proj_bwd(dproj, dx2, x1, g, w_ag, w_qkv, w_f):
    t = x1.shape[0]
    tm = TOKEN_ROWS
    n_ag, n_qkv = 2 * D_CONV, 3 * D_ATTN

    def body(dp_ref, dx2_ref, x_ref, g_ref, wag_ref, wqkv_ref, wf_ref, dx_ref, dg_ref):
        @pl.when(pl.program_id(0) == 0)
        def _():
            dg_ref[...] = jnp.zeros_like(dg_ref)

        dh = (_dot(dp_ref[:, 0:n_ag], wag_ref[...]) + _dot(dp_ref[:, n_ag:n_ag + n_qkv], wqkv_ref[...])
              + _dot(dp_ref[:, n_ag + n_qkv:], wf_ref[...]))
        xv = x_ref[...]
        dxn, dg_rows = _rms_bwd(xv, _rms_stats(xv), g_ref[...], dh)
        dx_ref[...] = dx2_ref[...] + dxn
        dg_ref[...] += jnp.sum(dg_rows, axis=0, keepdims=True)

    return pl.pallas_call(
        body, name="mix_proj_bwd", grid=(t // tm,),
        in_specs=[_row_spec(tm, dproj.shape[1]),
                  _row_spec(tm, D_MODEL), _row_spec(tm, D_MODEL), _full_spec((1, D_MODEL)),
                  _full_spec(w_ag.shape), _full_spec(w_qkv.shape), _full_spec(w_f.shape)],
        out_specs=[_row_spec(tm, D_MODEL), _full_spec((1, D_MODEL))],
        out_shape=[jax.ShapeDtypeStruct((t, D_MODEL), F32), jax.ShapeDtypeStruct((1, D_MODEL), F32)],
        compiler_params=_params(dimension_semantics=("arbitrary",)),
    )(dproj, dx2, x1, g, w_ag, w_qkv, w_f)


def _split3(x):
    hi = x.astype(BF16)
    r1 = x - hi.astype(F32)
    mid = r1.astype(BF16)
    lo = (r1 - mid.astype(F32)).astype(BF16)
    return hi, mid, lo


def _gates_fwd(flt, fb):
    t = flt.shape[1]

    def body(f_ref, b_ref, d_ref):
        z = f_ref[...] + b_ref[...]
        logf = jnp.minimum(z, 0.0) - jnp.log(1.0 + jnp.exp(-jnp.abs(z)))
        row = lax.broadcasted_iota(jnp.int32, (LANES, LANES), 0)
        col = lax.broadcasted_iota(jnp.int32, (LANES, LANES), 1)
        upper = (row <= col).astype(BF16)
        carry = jnp.zeros((HEAD_ROWS, 1), F32)
        for blk in range(t // LANES):
            hi, mid, lo = _split3(logf[:, blk * LANES:(blk + 1) * LANES])
            cs = _dot(hi, upper) + _dot(mid, upper) + _dot(lo, upper)
            d_ref[:, blk * LANES:(blk + 1) * LANES] = cs + carry
            carry = carry + cs[:, LANES - 1:LANES]

    return pl.pallas_call(
        body, name="gates_fwd", out_shape=jax.ShapeDtypeStruct((HEAD_ROWS, t), F32),
        compiler_params=_params(),
    )(flt, fb)


def _gates_bwd(dd, flt, fb):
    t = flt.shape[1]

    def body(dd_ref, f_ref, b_ref, df_ref, db_ref):
        z = f_ref[...] + b_ref[...]
        row = lax.broadcasted_iota(jnp.int32, (LANES, LANES), 0)
        col = lax.broadcasted_iota(jnp.int32, (LANES, LANES), 1)
        lower = (row >= col).astype(BF16)
        carry = jnp.zeros((HEAD_ROWS, 1), F32)
        db = jnp.zeros((HEAD_ROWS, 1), F32)
        for blk in reversed(range(t // LANES)):
            sl = slice(blk * LANES, (blk + 1) * LANES)
            hi, mid, lo = _split3(dd_ref[:, sl])
            cs = _dot(hi, lower) + _dot(mid, lower) + _dot(lo, lower)
            dz = (cs + carry) * _sigmoid(-z[:, sl])
            df_ref[:, sl] = dz
            db = db + jnp.sum(dz, axis=1, keepdims=True)
            carry = carry + cs[:, 0:1]
        db_ref[...] = db

    return pl.pallas_call(
        body, name="gates_bwd",
        out_shape=[jax.ShapeDtypeStruct((HEAD_ROWS, t), F32), jax.ShapeDtypeStruct((HEAD_ROWS, 1), F32)],
        compiler_params=_params(),
    )(dd, flt, fb)


CONV_CHUNK = 128
CONV_TAIL = 16
CONV_WINDOW = CONV_CHUNK + CONV_PAD + 8
CONV_ROWS_EXTRA = CONV_PAD + CONV_TAIL
SUBLANES = 8


def _conv_rows(ag_ref, u_ref, t):
    u_ref[0:CONV_PAD, :] = jnp.zeros((CONV_PAD, D_CONV), F32)
    u_ref[CONV_PAD + t:CONV_ROWS_EXTRA + t, :] = jnp.zeros((CONV_TAIL, D_CONV), F32)

    def fill(i, c):
        r0 = pl.multiple_of(i * CONV_CHUNK, CONV_CHUNK)
        a = ag_ref[pl.ds(r0, CONV_CHUNK), 0:D_CONV]
        gt = ag_ref[pl.ds(r0, CONV_CHUNK), D_CONV:2 * D_CONV]
        u_ref[pl.ds(CONV_PAD + r0, CONV_CHUNK), :] = a * _sigmoid(gt)
        return c

    lax.fori_loop(0, t // CONV_CHUNK, fill, 0)


def _for_shifted(ref, r0, offsets, fn):
    window = ref[pl.ds(r0, CONV_WINDOW), :]
    for rem in range(SUBLANES):
        mine = [o for o in offsets if o % SUBLANES == rem]
        if not mine:
            continue
        turned = window if rem == 0 else pltpu.roll(window, CONV_WINDOW - rem, 0)
        for o in mine:
            fn(o, turned[o - rem:o - rem + CONV_CHUNK])


def _conv_taps(u_ref, r0, w_ref, cb):
    acc = [jnp.zeros((CONV_CHUNK, D_CONV), F32)]

    def tap(o, rows):
        j = o - (CONV_PAD - CONV_WIDTH + 1)
        acc[0] = acc[0] + w_ref[j:j + 1, :] * rows

    _for_shifted(u_ref, r0, [j + CONV_PAD - CONV_WIDTH + 1 for j in range(CONV_WIDTH)], tap)
    return acc[0] + cb


def _conv_point(y, lg, lb):
    mu = jnp.mean(y, axis=-1, keepdims=True)
    yc = y - mu
    rstd = lax.rsqrt(jnp.mean(yc * yc, axis=-1, keepdims=True) + EPS)
    yhat = yc * rstd
    z = yhat * lg + lb
    sg = _sigmoid(z)
    s = z * sg
    rr = _rms_stats(s)
    return yhat, rstd, z, sg, s, rr


def _conv_fwd(ag, conv_w, conv_b, ln_g, ln_b, norm_g):
    t = ag.shape[0]

    def body(ag_ref, w_ref, cb_ref, lg_ref, lb_ref, ng_ref, o_ref, y_ref, u_ref):
        _conv_rows(ag_ref, u_ref, t)
        cb, lg, lb, ng = cb_ref[...], lg_ref[...], lb_ref[...], ng_ref[...]

        def chunk(i, c):
            r0 = pl.multiple_of(i * CONV_CHUNK, CONV_CHUNK)
            y = _conv_taps(u_ref, r0, w_ref, cb)
            y_ref[pl.ds(r0, CONV_CHUNK), :] = y
            _, _, _, _, s, rr = _conv_point(y, lg, lb)
            o_ref[pl.ds(r0, CONV_CHUNK), :] = (s * rr * ng).astype(BF16)
            return c

        lax.fori_loop(0, t // CONV_CHUNK, chunk, 0)

    return pl.pallas_call(
        body, name="conv_fwd",
        out_shape=[jax.ShapeDtypeStruct((t, D_CONV), BF16), jax.ShapeDtypeStruct((t, D_CONV), F32)],
        scratch_shapes=[pltpu.VMEM((t + CONV_ROWS_EXTRA, D_CONV), F32)],
        compiler_params=_params(),
    )(ag, conv_w, conv_b, ln_g, ln_b, norm_g)


def _conv_bwd(ag, y, dout, conv_w, ln_g, ln_b, norm_g):
    t = ag.shape[0]

    def body(ag_ref, y_ref, do_ref, w_ref, lg_ref, lb_ref, ng_ref,
             dag_ref, dw_ref, dcb_ref, dlg_ref, dlb_ref, dng_ref, u_ref, dy_ref):
        _conv_rows(ag_ref, u_ref, t)
        dy_ref[t:t + CONV_ROWS_EXTRA, :] = jnp.zeros((CONV_ROWS_EXTRA, D_CONV), F32)
        lg, lb, ng = lg_ref[...], lb_ref[...], ng_ref[...]
        dw_ref[...] = jnp.zeros_like(dw_ref)
        zero = jnp.zeros((1, D_CONV), F32)

        def chunk(i, carry):
            dcb, dlg, dlb, dng = carry
            r0 = pl.multiple_of(i * CONV_CHUNK, CONV_CHUNK)
            yhat, rstd, z, sg, s, rr = _conv_point(y_ref[pl.ds(r0, CONV_CHUNK), :], lg, lb)
            do = do_ref[pl.ds(r0, CONV_CHUNK), :]
            ds, dng_rows = _rms_bwd(s, rr, ng, do)
            dz = ds * _silu_grad(z, sg)
            dyhat = dz * lg
            dy = rstd * (dyhat - jnp.mean(dyhat, axis=-1, keepdims=True)
                         - yhat * jnp.mean(dyhat * yhat, axis=-1, keepdims=True))
            dy_ref[pl.ds(r0, CONV_CHUNK), :] = dy
            def tap(o, rows):
                j = o - (CONV_PAD - CONV_WIDTH + 1)
                dw_ref[j:j + 1, :] += jnp.sum(dy * rows, axis=0, keepdims=True)

            _for_shifted(u_ref, r0, [j + CONV_PAD - CONV_WIDTH + 1 for j in range(CONV_WIDTH)], tap)
            return (dcb + jnp.sum(dy, axis=0, keepdims=True), dlg + jnp.sum(dz * yhat, axis=0, keepdims=True),
                    dlb + jnp.sum(dz, axis=0, keepdims=True), dng + jnp.sum(dng_rows, axis=0, keepdims=True))

        dcb, dlg, dlb, dng = lax.fori_loop(0, t // CONV_CHUNK, chunk, (zero, zero, zero, zero))
        dcb_ref[...] = dcb
        dlg_ref[...] = dlg
        dlb_ref[...] = dlb
        dng_ref[...] = dng

        def chunk2(i, c):
            r0 = pl.multiple_of(i * CONV_CHUNK, CONV_CHUNK)
            acc = [jnp.zeros((CONV_CHUNK, D_CONV), F32)]

            def tap(o, rows):
                j = CONV_WIDTH - 1 - o
                acc[0] = acc[0] + w_ref[j:j + 1, :] * rows

            _for_shifted(dy_ref, r0, list(range(CONV_WIDTH)), tap)
            du = acc[0]
            a = ag_ref[pl.ds(r0, CONV_CHUNK), 0:D_CONV]
            gt = ag_ref[pl.ds(r0, CONV_CHUNK), D_CONV:2 * D_CONV]
            sg = _sigmoid(gt)
            dag_ref[pl.ds(r0, CONV_CHUNK), 0:D_CONV] = (du * sg).astype(BF16)
            dag_ref[pl.ds(r0, CONV_CHUNK), D_CONV:2 * D_CONV] = (du * a * sg * (1.0 - sg)).astype(BF16)
            return c

        lax.fori_loop(0, t // CONV_CHUNK, chunk2, 0)

    vec = jax.ShapeDtypeStruct((1, D_CONV), F32)
    return pl.pallas_call(
        body, name="conv_bwd",
        out_shape=[jax.ShapeDtypeStruct((t, 2 * D_CONV), BF16), jax.ShapeDtypeStruct((CONV_PAD, D_CONV), F32),
                   vec, vec, vec, vec],
        scratch_shapes=[pltpu.VMEM((t + CONV_ROWS_EXTRA, D_CONV), F32), pltpu.VMEM((t + CONV_ROWS_EXTRA, D_CONV), F32)],
        compiler_params=_params(),
    )(ag, y, dout, conv_w, ln_g, ln_b, norm_g)


Q_ROWS = 256
ATTN_SCALE = HEAD_DIM ** -0.5
ATTN_AHEAD = 1


def _attn_specs(t):
    blk = lambda off: pl.BlockSpec((t, LANES), lambda p: (0, off + p))
    pairs = N_HEADS // 2
    return [blk(0), blk(pairs), blk(2 * pairs), pl.BlockSpec((2, 1, t), lambda p: (p, 0, 0))]


def _one_head(q2, mask):
    return jnp.where(mask, q2, jnp.zeros_like(q2)) * ATTN_SCALE


def _attn_scores(qs, k2, drow, r0, q1):
    s = _dot_nt(qs, k2) - drow
    rowi = lax.broadcasted_iota(jnp.int32, (q1 - r0, q1 - r0), 0)
    coli = lax.broadcasted_iota(jnp.int32, (q1 - r0, q1 - r0), 1)
    diag = jnp.where(coli <= rowi, s[:, r0:q1], -jnp.inf)
    return diag if r0 == 0 else jnp.concatenate([s[:, :r0], diag], axis=1)


def _attn_fwd(qkv, drow, deps=()):
    t = qkv.shape[0]
    deps = tuple(deps)

    def body(q_ref, k_ref, v_ref, dr_ref, o_ref, lse_ref):
        head_a = lax.broadcasted_iota(jnp.int32, (1, LANES), 1) < HEAD_DIM
        items = [(qb, hh) for qb in range(t // Q_ROWS) for hh in range(2)]

        def scores(item):
            qb, hh = item
            r0, q1 = qb * Q_ROWS, (qb + 1) * Q_ROWS
            qs = _one_head(q_ref[r0:q1, :], head_a if hh == 0 else ~head_a)
            return _attn_scores(qs, k_ref[0:q1, :], dr_ref[hh, :, 0:q1], r0, q1)

        ahead = [scores(item) for item in items[:ATTN_AHEAD]]
        outs = []
        for n, (qb, hh) in enumerate(items):
            r0, q1 = qb * Q_ROWS, (qb + 1) * Q_ROWS
            s = ahead.pop(0)
            if n + ATTN_AHEAD < len(items):
                ahead.append(scores(items[n + ATTN_AHEAD]))
            mx = jnp.max(s, axis=1, keepdims=True)
            p = jnp.exp(s - mx)
            l = jnp.sum(p, axis=1, keepdims=True)
            lse_ref[hh, r0:q1, :] = mx + jnp.log(l)
            outs.append(_dot(p.astype(BF16), v_ref[0:q1, :]) * (1.0 / l))
            if hh == 1:
                o_ref[r0:q1, :] = jnp.where(head_a, outs[0], outs[1])
                outs = []

    pairs = N_HEADS // 2
    return pl.pallas_call(
        _skip(len(deps), body), name="attn_fwd", grid=(pairs,), in_specs=[_ANY] * len(deps) + _attn_specs(t),
        out_specs=[pl.BlockSpec((t, LANES), lambda p: (0, p)), pl.BlockSpec((2, t, 1), lambda p: (p, 0, 0))],
        out_shape=[jax.ShapeDtypeStruct((t, D_ATTN), F32), jax.ShapeDtypeStruct((N_HEADS, t, 1), F32)],
        compiler_params=_params(dimension_semantics=("arbitrary",)),
    )(*deps, qkv, qkv, qkv, drow)


def _attn_bwd(qkv, drow, lse, do):
    t = qkv.shape[0]

    def body(q_ref, k_ref, v_ref, dr_ref, lse_ref, do_ref,
             dq_ref, dk_ref, dv_ref, dd_ref, dk_acc, dv_acc):
        head_a = lax.broadcasted_iota(jnp.int32, (1, LANES), 1) < HEAD_DIM
        dk_acc[...] = jnp.zeros_like(dk_acc)
        dv_acc[...] = jnp.zeros_like(dv_acc)
        dd_ref[...] = jnp.zeros_like(dd_ref)
        items = [(qb, hh) for qb in range(t // Q_ROWS) for hh in range(2)]

        def products(item):
            qb, hh = item
            r0, q1 = qb * Q_ROWS, (qb + 1) * Q_ROWS
            mask = head_a if hh == 0 else ~head_a
            qs = _one_head(q_ref[r0:q1, :], mask)
            dob = jnp.where(mask, do_ref[r0:q1, :], 0.0).astype(BF16)
            s = _attn_scores(qs, k_ref[0:q1, :], dr_ref[hh, :, 0:q1], r0, q1)
            return qs, dob, s, _dot_nt(dob, v_ref[0:q1, :])

        ahead = products(items[0])
        dqs = []
        for n, (qb, hh) in enumerate(items):
            r0, q1 = qb * Q_ROWS, (qb + 1) * Q_ROWS
            qs, dob, s, dp = ahead
            if n + 1 < len(items):
                ahead = products(items[n + 1])
            p = jnp.exp(s - lse_ref[hh, r0:q1, :])
            ds = p * (dp - jnp.sum(p * dp, axis=1, keepdims=True))
            dsb = ds.astype(BF16)
            dqs.append(_dot(dsb, k_ref[0:q1, :]) * ATTN_SCALE)
            dk_acc[0:q1, :] += _dot_tn(dsb, qs)
            dv_acc[0:q1, :] += _dot_tn(p.astype(BF16), dob)
            dd_ref[hh, :, 0:q1] -= jnp.sum(ds, axis=0, keepdims=True)
            if hh == 1:
                dq_ref[r0:q1, :] = jnp.where(head_a, dqs[0], dqs[1]).astype(BF16)
                dqs = []
        dk_ref[...] = dk_acc[...].astype(BF16)
        dv_ref[...] = dv_acc[...].astype(BF16)

    pairs = N_HEADS // 2
    col = pl.BlockSpec((t, LANES), lambda p: (0, p))
    grad = jax.ShapeDtypeStruct((t, D_ATTN), BF16)
    return pl.pallas_call(
        body, name="attn_bwd", grid=(pairs,),
        in_specs=_attn_specs(t) + [pl.BlockSpec((2, t, 1), lambda p: (p, 0, 0)), col],
        out_specs=[col, col, col, pl.BlockSpec((2, 1, t), lambda p: (p, 0, 0))],
        out_shape=[grad, grad, grad, jax.ShapeDtypeStruct((N_HEADS, 1, t), F32)],
        scratch_shapes=[pltpu.VMEM((t, LANES), F32), pltpu.VMEM((t, LANES), F32)],
        compiler_params=_params(dimension_semantics=("arbitrary",)),
    )(qkv, qkv, qkv, drow, lse, do)


def _out_proj(ycn, o, g_attn, w_out, x1, deps=()):
    t = x1.shape[0]
    tm = TOKEN_ROWS
    deps = tuple(deps)

    def body(yc_ref, o_ref, g_ref, w_ref, x_ref, xo_ref, ya_ref):
        ov = o_ref[...]
        ya = (ov * _rms_stats(ov) * g_ref[...]).astype(BF16)
        ya_ref[...] = ya
        xo_ref[...] = x_ref[...] + _dot(yc_ref[...], w_ref[0:D_CONV, :]) + _dot(ya, w_ref[D_CONV:, :])

    return pl.pallas_call(
        _skip(len(deps), body), name="out_proj", grid=(t // tm,),
        in_specs=[_ANY] * len(deps) + [_row_spec(tm, D_CONV), _row_spec(tm, D_ATTN), _full_spec((1, D_ATTN)),
                                       _full_spec(w_out.shape), _row_spec(tm, D_MODEL)],
        out_specs=[_row_spec(tm, D_MODEL), _row_spec(tm, D_ATTN)],
        out_shape=[jax.ShapeDtypeStruct((t, D_MODEL), F32), jax.ShapeDtypeStruct((t, D_ATTN), BF16)],
        compiler_params=_params(dimension_semantics=("arbitrary",)),
    )(*deps, ycn, o, g_attn, w_out, x1)


def _out_proj_bwd(dx2, o, g_attn, w_out, deps=()):
    t = dx2.shape[0]
    tm = TOKEN_ROWS
    deps = tuple(deps)

    def body(dx_ref, o_ref, g_ref, w_ref, dyc_ref, do_ref, dg_ref):
        @pl.when(pl.program_id(0) == 0)
        def _():
            dg_ref[...] = jnp.zeros_like(dg_ref)

        dxb = dx_ref[...]
        dyc_ref[...] = _dot_nt(dxb, w_ref[0:D_CONV, :])
        dya = _dot_nt(dxb, w_ref[D_CONV:, :])
        ov = o_ref[...]
        do, dg_rows = _rms_bwd(ov, _rms_stats(ov), g_ref[...], dya)
        do_ref[...] = do
        dg_ref[...] += jnp.sum(dg_rows, axis=0, keepdims=True)

    return pl.pallas_call(
        _skip(len(deps), body), name="out_proj_bwd", grid=(t // tm,),
        in_specs=[_ANY] * len(deps) + [_row_spec(tm, D_MODEL), _row_spec(tm, D_ATTN), _full_spec((1, D_ATTN)),
                                       _full_spec(w_out.shape)],
        out_specs=[_row_spec(tm, D_CONV), _row_spec(tm, D_ATTN), _full_spec((1, D_ATTN))],
        out_shape=[jax.ShapeDtypeStruct((t, D_CONV), F32), jax.ShapeDtypeStruct((t, D_ATTN), F32),
                   jax.ShapeDtypeStruct((1, D_ATTN), F32)],
        compiler_params=_params(dimension_semantics=("arbitrary",)),
    )(*deps, dx2, o, g_attn, w_out)


def _loss_bwd(x3, target, g):
    t = x3.shape[0]
    tm = TOKEN_ROWS

    def body(x_ref, t_ref, g_ref, loss_ref, dx_ref, dg_ref):
        @pl.when(pl.program_id(0) == 0)
        def _():
            loss_ref[...] = jnp.zeros_like(loss_ref)
            dg_ref[...] = jnp.zeros_like(dg_ref)

        xv = x_ref[...]
        r = _rms_stats(xv)
        gv = g_ref[...]
        err = xv * r * gv - t_ref[...]
        row = jnp.sum(err * err, axis=1, keepdims=True) * (0.5 / D_MODEL)
        loss_ref[...] += jnp.sum(row, axis=0, keepdims=True)
        dx, dg_rows = _rms_bwd(xv, r, gv, err * (1.0 / D_MODEL))
        dx_ref[...] = dx
        dg_ref[...] += jnp.sum(dg_rows, axis=0, keepdims=True)

    return pl.pallas_call(
        body, name="loss_bwd", grid=(t // tm,),
        in_specs=[_row_spec(tm, D_MODEL), _row_spec(tm, D_MODEL), _full_spec((1, D_MODEL))],
        out_specs=[_full_spec((1, LANES)), _row_spec(tm, D_MODEL), _full_spec((1, D_MODEL))],
        out_shape=[jax.ShapeDtypeStruct((1, LANES), F32), jax.ShapeDtypeStruct((t, D_MODEL), F32),
                   jax.ShapeDtypeStruct((1, D_MODEL), F32)],
        compiler_params=_params(dimension_semantics=("arbitrary",)),
    )(x3, target, g)


def _split_w_in(w_in_t):
    w_ag = w_in_t[:2 * D_CONV]
    w_qkv = w_in_t[2 * D_CONV:2 * D_CONV + 3 * D_ATTN]
    w_f = jnp.pad(w_in_t[2 * D_CONV + 3 * D_ATTN:], ((0, LANES - N_HEADS), (0, 0)))
    return w_ag, w_qkv, w_f


def _head_rows(v):
    return jnp.pad(v, ((0, HEAD_ROWS - N_HEADS),) + ((0, 0),) * (v.ndim - 1))


def _local_step(x, target, p, get_weights, put_grads, flush_grads):
    t = x.shape[0]
    fb = _head_rows(p["forget_b"].reshape(N_HEADS, 1))

    w, deps = get_weights("ffn1_w13", None)
    h1, gu1, act1 = _ffn_up(x, p["ffn1_norm"], w["ffn1_w13"], "ffn1_up", deps)
    w2, _ = get_weights("ffn1_w2", act1)
    w.update(w2)
    x1 = _ffn_down(x, act1, w["ffn1_w2"], "ffn1_down")
    wm, _ = get_weights("mix", x1)
    w.update(wm)
    w_ag, w_qkv, w_f = _split_w_in(w["w_in"])
    conv_w = jnp.pad(w["conv_w"], ((0, CONV_PAD - CONV_WIDTH), (0, 0)))
    h2, ag, qkv, fl = _mix_proj(x1, p["mix_norm"], w_ag, w_qkv, w_f)
    flt = _head_rows(fl[:, :N_HEADS].T)
    dcum = _gates_fwd(flt, fb)[:N_HEADS]
    drow = dcum.reshape(N_HEADS, 1, t)
    ycn, y_conv = _conv_fwd(ag, conv_w, p["conv_b"], p["conv_ln_g"], p["conv_ln_b"], p["out_norm_conv"])
    o, lse = _attn_fwd(qkv, drow, [ycn])
    _, deps = get_weights("ffn2:landed", o)
    x2, yan = _out_proj(ycn, o, p["out_norm_attn"], w["w_out"], x1, deps)
    w2, _ = get_weights("ffn2", x2)
    w.update(w2)
    x3, h3, gu2, act2 = _ffn_fwd(x2, p["ffn2_norm"], w["ffn2_w13"], w["ffn2_w2"], "ffn2_fwd")
    loss, dx3, d_final = _loss_bwd(x3, target, p["final_norm"])

    g = {}
    dx2, dgu2, g["ffn2_norm"], dx3_half, dx2_bf16 = _ffn_bwd(
        dx3, x2, gu2, p["ffn2_norm"], w["ffn2_w13"], w["ffn2_w2"], "ffn2_bwd")
    dw13 = _wgrad(h3, dgu2, N_CHIPS, "ffn2_dw13")
    dw2 = _wgrad(act2, dx3_half, 1, "ffn2_dw2").reshape(D_FF, D_MODEL)
    deps = put_grads("ffn2", {"ffn2_w13": dw13, "ffn2_w2": dw2})
    dyc, do, g["out_norm_attn"] = _out_proj_bwd(dx2_bf16, o, p["out_norm_attn"], w["w_out"], deps)
    deps = flush_grads("ffn2", [dyc])
    dw_out = _wgrad(jnp.concatenate([ycn, yan], axis=1), dx2_bf16, 1, "dw_out", deps).reshape(D_MODEL, D_MODEL)
    dq, dk, dv, ddrow = _attn_bwd(qkv, drow, lse, do)
    dflt, dfb = _gates_bwd(_head_rows(ddrow.reshape(N_HEADS, t)), flt, fb)
    g["forget_b"] = dfb[:N_HEADS, 0].reshape(1, N_HEADS)
    dfl = jnp.pad(dflt[:N_HEADS].T, ((0, 0), (0, LANES - N_HEADS)))
    dag, dconv_w, g["conv_b"], g["conv_ln_g"], g["conv_ln_b"], g["out_norm_conv"] = _conv_bwd(
        ag, y_conv, dyc, conv_w, p["conv_ln_g"], p["conv_ln_b"], p["out_norm_conv"])
    g["conv_w"] = dconv_w[:CONV_WIDTH]
    dproj = jnp.concatenate([dag, dq, dk, dv, dfl.astype(BF16)], axis=1)
    dx1, g["mix_norm"] = _mix_proj_bwd(dproj, dx2, x1, p["mix_norm"], w_ag, w_qkv, w_f)
    dw_in = _wgrad(dproj, h2, 1, "dw_in").reshape(dproj.shape[1], D_MODEL)[:N_IN]
    deps = put_grads("mix", {"w_in": dw_in, "w_out": dw_out})
    dx0, dgu1, g["ffn1_norm"], dx1_half, _ = _ffn_bwd(
        dx1, x, gu1, p["ffn1_norm"], w["ffn1_w13"], w["ffn1_w2"], "ffn1_bwd", deps)
    g["final_norm"] = d_final
    g["loss"] = loss[:, :1]
    deps = flush_grads("mix", put_grads("small", g))
    dw2 = _wgrad(act1, dx1_half, 1, "ffn1_dw2", deps).reshape(D_FF, D_MODEL)
    deps = flush_grads("ffn1_w2", put_grads("ffn1_w2", {"ffn1_w2": dw2}))
    dw13 = _wgrad(h1, dgu1, N_CHIPS, "ffn1_dw13", deps)
    put_grads("ffn1_w13", {"ffn1_w13": dw13})
    return dx0


MESH = pl.DeviceIdType.MESH


def _place():
    x, y, c = lax.axis_index("x"), lax.axis_index("y"), lax.axis_index("c")
    chips = [(1 - x, y), (x, 1 - y), (1 - x, 1 - y)]
    return x, y, c, chips


def _hbm_out(shape, dtype):
    return jax.ShapeDtypeStruct(shape, dtype)


def _comm_call(body, name, ins, out_shapes, n_remote, in_place=False):
    return pl.pallas_call(
        body, name=name, in_specs=[_ANY] * len(ins), out_specs=[_ANY] * len(out_shapes), out_shape=out_shapes,
        scratch_shapes=[pltpu.SemaphoreType.DMA((n_remote,)), pltpu.SemaphoreType.DMA((n_remote,))],
        input_output_aliases={i: i for i in range(len(ins))} if in_place else {},
    )(*ins)


def _remote(src, dst, sems, n, to):
    send_sems, recv_sems = sems
    return pltpu.make_async_remote_copy(src_ref=src, dst_ref=dst, send_sem=send_sems.at[n], recv_sem=recv_sems.at[n],
                                        device_id=to, device_id_type=MESH)


HALF_ROWS_MULTIPLE = 32


def _halved_by_rows(rows):
    return rows % HALF_ROWS_MULTIPLE == 0


def _half_shape(rows, cols):
    return (rows // 2, cols) if _halved_by_rows(rows) else (rows, cols // 2)


def _half_index(rows, core):
    return (core, 0) if _halved_by_rows(rows) else (0, core)


def _half_of(ref, rows, cols, core, *lead):
    if _halved_by_rows(rows):
        return ref.at[(*lead, pl.ds(core * (rows // 2), rows // 2), slice(None))]
    return ref.at[(*lead, slice(None), pl.ds(core * (cols // 2), cols // 2))]


def _into_slot(shard, chip, dtype, name, deps=()):
    rows, cols = shard.shape
    half = _half_shape(rows, cols)
    by_rows = _halved_by_rows(rows)
    deps = tuple(deps)

    def body(k_ref, *refs):
        s_ref, o_ref = refs[len(deps):]
        o_ref[0] = s_ref[...].astype(dtype)

    return pl.pallas_call(
        body, name=name,
        grid_spec=pltpu.PrefetchScalarGridSpec(
            num_scalar_prefetch=1, grid=(2,),
            in_specs=[_ANY] * len(deps) + [pl.BlockSpec(half, lambda i, k_ref: (i, 0) if by_rows else (0, i))],
            out_specs=pl.BlockSpec((1,) + half, lambda i, k_ref: (k_ref[0], i, 0) if by_rows else (k_ref[0], 0, i))),
        out_shape=jax.ShapeDtypeStruct((N_CHIPS, rows, cols), dtype),
        compiler_params=_params(dimension_semantics=("arbitrary",)),
    )(chip, *deps, shard)


def _run_copies(name, bufs, n_copies, plan):
    n = len(bufs)

    def body(*refs):
        copies = plan(refs[n:2 * n], refs[2 * n:2 * n + 2])
        for send, _ in copies:
            send.start()
        for send, recv in copies:
            send.wait_send()
            recv.wait_recv()

    return _comm_call(body, name, bufs, [_hbm_out(b.shape, b.dtype) for b in bufs], n_copies, in_place=True)


def _forward_halves(slots, name):
    return _run_copies(name, slots, 3 * len(slots), _d2d_forward_plan(slots))


_HBM = pl.BlockSpec(memory_space=pltpu.HBM)
_SEM = pl.BlockSpec(memory_space=pltpu.SEMAPHORE)
_DATAFLOW = pltpu.SideEffectType.DATAFLOW_SIDE_EFFECTING


def _split_copy_start(name, bufs, n_copies, plan):
    n = len(bufs)

    def body(*refs):
        for send, _ in plan(refs[:n], (refs[n], refs[n + 1])):
            send.start()
        token = refs[-1]
        token[...] = jnp.zeros_like(token)

    out = pl.pallas_call(
        body, name=name,
        out_shape=(pltpu.SemaphoreType.DMA((n_copies,)), pltpu.SemaphoreType.DMA((n_copies,)),
                   *[pltpu.HBM(b.shape, b.dtype) for b in bufs], jax.ShapeDtypeStruct((8, LANES), F32)),
        in_specs=[_HBM] * n, out_specs=(_SEM, _SEM, *[_HBM] * n, pl.BlockSpec(memory_space=pltpu.VMEM)),
        input_output_aliases={i: 2 + i for i in range(n)},
        compiler_params=pltpu.CompilerParams(has_side_effects=_DATAFLOW),
    )(*[pltpu.with_memory_space_constraint(b, pltpu.HBM) for b in bufs])
    return out[0], out[1], list(out[2:2 + n]), out[-1]


def _split_copy_wait(name, started, plan, after, passed=()):
    send_sems, recv_sems, bufs, _ = started
    n = len(bufs)
    after = tuple(after)
    bufs = list(bufs) + list(passed)
    total = len(bufs)

    def body(*refs):
        for send, recv in plan(refs[:n], (refs[total], refs[total + 1])):
            send.wait_send()
            recv.wait_recv()

    out = pl.pallas_call(
        body, name=name, out_shape=tuple(pltpu.HBM(b.shape, b.dtype) for b in bufs),
        in_specs=[_HBM] * total + [_SEM, _SEM] + [_ANY] * len(after), out_specs=tuple([_HBM] * total),
        input_output_aliases={i: i for i in range(total)},
        compiler_params=pltpu.CompilerParams(has_side_effects=_DATAFLOW),
    )(*bufs, send_sems, recv_sems, *after)
    return list(out)


def _ici_gather_plan(slots):
    def plan(refs, sems):
        x, y, c, chips = _place()
        me = 2 * x + y
        copies = []
        for i, ref in enumerate(refs):
            for j, chip in enumerate(chips):
                mine = _half_of(ref, *slots[i].shape[1:], c, me)
                theirs = _half_of(ref, *slots[i].shape[1:], c, 2 * chip[0] + chip[1])
                to = (*chip, c)
                copies.append((_remote(mine, mine, sems, 3 * i + j, to), _remote(theirs, theirs, sems, 3 * i + j, to)))
        return copies

    return plan


def _ici_scatter_plan(n):
    def plan(refs, sems):
        x, y, c, chips = _place()
        copies = []
        for i in range(n):
            for j, chip in enumerate(chips):
                cp = _remote(refs[i].at[2 * chip[0] + chip[1]], refs[n + i].at[j], sems, 3 * i + j, (*chip, c))
                copies.append((cp, cp))
        return copies

    return plan


def _d2d_forward_plan(slots):
    def plan(refs, sems):
        x, y, c, chips = _place()
        sibling = (x, y, 1 - c)
        copies = []
        for i, ref in enumerate(refs):
            for j, chip in enumerate(chips):
                src_chip = 2 * chip[0] + chip[1]
                mine = _half_of(ref, *slots[i].shape[1:], c, src_chip)
                theirs = _half_of(ref, *slots[i].shape[1:], 1 - c, src_chip)
                copies.append((_remote(mine, mine, sems, 3 * i + j, sibling),
                               _remote(theirs, theirs, sems, 3 * i + j, sibling)))
        return copies

    return plan


def _pair_exchange_plan(grads):
    n = len(grads)

    def plan(refs, sems):
        x, y, c, _ = _place()
        copies = []
        for i in range(n):
            theirs = _half_of(refs[i], *grads[i].shape[1:], 1 - c, slice(None))
            cp = _remote(theirs, refs[n + i], sems, i, (x, y, 1 - c))
            copies.append((cp, cp))
        return copies

    return plan


def _pair_share_plan(shapes):
    def plan(refs, sems):
        x, y, c, _ = _place()
        sibling = (x, y, 1 - c)
        copies = []
        for i, ref in enumerate(refs):
            mine, theirs = _half_of(ref, *shapes[i], c), _half_of(ref, *shapes[i], 1 - c)
            copies.append((_remote(mine, mine, sems, i, sibling), _remote(theirs, theirs, sems, i, sibling)))
        return copies

    return plan


def _pair_share(halves, name):
    return _run_copies(name, halves, len(halves), _pair_share_plan([h.shape for h in halves]))


N_DEVICES = 8
FLIPS = [(fx, fy, fc) for fx in range(2) for fy in range(2) for fc in range(2)][1:]


def _small_slots(v, me):
    rows = v.shape[0]

    def body(k_ref, v_ref, o_ref):
        o_ref[0] = v_ref[...]

    return pl.pallas_call(
        body, name="small_slot",
        grid_spec=pltpu.PrefetchScalarGridSpec(
            num_scalar_prefetch=1, grid=(1,),
            in_specs=[pl.BlockSpec((rows, LANES), lambda i, k_ref: (0, 0))],
            out_specs=pl.BlockSpec((1, rows, LANES), lambda i, k_ref: (k_ref[0], 0, 0))),
        out_shape=jax.ShapeDtypeStruct((N_DEVICES, rows, LANES), F32),
        compiler_params=_params(dimension_semantics=("arbitrary",)),
    )(me, v)


def _small_plan():
    def plan(refs, sems):
        x, y, c, _ = _place()
        slots = refs[0]
        me = 4 * x + 2 * y + c
        copies = []
        for n, (fx, fy, fc) in enumerate(FLIPS):
            to = (x ^ fx, y ^ fy, c ^ fc)
            src = 4 * to[0] + 2 * to[1] + to[2]
            copies.append((_remote(slots.at[me], slots.at[me], sems, n, to), _remote(slots.at[src], slots.at[src], sems, n, to)))
        return copies

    return plan


def _small_sum(slots):
    def body(s_ref, o_ref):
        acc = s_ref[0]
        for s in range(1, N_DEVICES):
            acc = acc + s_ref[s]
        o_ref[...] = acc

    return pl.pallas_call(body, name="small_sum", out_shape=jax.ShapeDtypeStruct(slots.shape[1:], F32),
                          compiler_params=_params())(slots)


def _pair_add(gs, sibs, core, name):
    n = len(gs)
    halves = [_half_shape(*g.shape[1:]) for g in gs]

    def body(c_ref, *refs):
        for g_ref, s_ref, o_ref in zip(refs[:n], refs[n:2 * n], refs[2 * n:]):
            o_ref[0] = (g_ref[0].astype(F32) + s_ref[0].astype(F32)).astype(BF16)

    def mine(g, half):
        return pl.BlockSpec((1,) + half, lambda s, c_ref: (s, *_half_index(g.shape[1], c_ref[0])))

    whole = [pl.BlockSpec((1,) + half, lambda s, c_ref: (s, 0, 0)) for half in halves]
    return pl.pallas_call(
        body, name=name,
        grid_spec=pltpu.PrefetchScalarGridSpec(
            num_scalar_prefetch=1, grid=(N_CHIPS,),
            in_specs=[mine(g, half) for g, half in zip(gs, halves)] + whole, out_specs=whole),
        out_shape=[jax.ShapeDtypeStruct((N_CHIPS,) + half, BF16) for half in halves],
        compiler_params=_params(dimension_semantics=("arbitrary",)),
    )(core, *gs, *sibs)


def _chip_add(parts, recvs, chip_core, shapes, name):
    n = len(parts)
    halves = [_half_shape(*shape) for shape in shapes]

    def body(kc_ref, *refs):
        for p_ref, r_ref, o_ref in zip(refs[:n], refs[n:2 * n], refs[2 * n:]):
            acc = p_ref[0].astype(F32)
            for j in range(N_CHIPS - 1):
                acc = acc + r_ref[j].astype(F32)
            o_ref[...] = acc

    def out_spec(shape, half):
        return pl.BlockSpec(half, lambda s, kc_ref: _half_index(shape[0], kc_ref[1]))

    return pl.pallas_call(
        body, name=name,
        grid_spec=pltpu.PrefetchScalarGridSpec(
            num_scalar_prefetch=1, grid=(1,),
            in_specs=[pl.BlockSpec((1,) + half, lambda s, kc_ref: (kc_ref[0], 0, 0)) for half in halves]
            + [pl.BlockSpec((N_CHIPS - 1,) + half, lambda s, kc_ref: (0, 0, 0)) for half in halves],
            out_specs=[out_spec(shape, half) for shape, half in zip(shapes, halves)]),
        out_shape=[jax.ShapeDtypeStruct(tuple(shape), F32) for shape in shapes],
        compiler_params=_params(dimension_semantics=("arbitrary",)),
    )(chip_core, *parts, *recvs)


def _adamw_math(w, g, m, v):
    m = ADAM_B1 * m + (1.0 - ADAM_B1) * g
    v = ADAM_B2 * v + (1.0 - ADAM_B2) * (g * g)
    m_hat = m / (1.0 - ADAM_B1 ** ADAM_STEP)
    v_hat = v / (1.0 - ADAM_B2 ** ADAM_STEP)
    delta = -ADAM_LR * (m_hat / (jnp.sqrt(v_hat) + ADAM_EPS) + ADAM_WD * w)
    return delta, m, v


ADAM_PARTS = 2


def _adamw_matrix(w, g, m, v, name):
    rows, cols = w.shape
    by_rows = rows % (8 * ADAM_PARTS) == 0
    block = (rows // ADAM_PARTS, cols) if by_rows else (rows, cols // ADAM_PARTS)

    def body(w_ref, g_ref, m_ref, v_ref, go_ref, d_ref, mo_ref, vo_ref):
        gv = g_ref[...]
        go_ref[...] = gv
        d_ref[...], mo_ref[...], vo_ref[...] = _adamw_math(w_ref[...], gv, m_ref[...], v_ref[...])

    spec = pl.BlockSpec(block, lambda i: (i, 0) if by_rows else (0, i))
    shape = jax.ShapeDtypeStruct((rows, cols), F32)
    return pl.pallas_call(
        body, name=name, grid=(ADAM_PARTS,), in_specs=[spec] * 4, out_specs=[spec] * 4, out_shape=[shape] * 4,
        compiler_params=_params(dimension_semantics=("arbitrary",)),
    )(w, g, m, v)


def _adamw_small(ws, gs, ms, vs):
    n = len(ws)

    def body(*refs):
        for i in range(n):
            w_ref, g_ref, m_ref, v_ref = (refs[k * n + i] for k in range(4))
            d_ref, mo_ref, vo_ref = (refs[(4 + k) * n + i] for k in range(3))
            d_ref[...], mo_ref[...], vo_ref[...] = _adamw_math(w_ref[...], g_ref[...], m_ref[...], v_ref[...])

    shapes = [jax.ShapeDtypeStruct(w.shape, F32) for w in ws]
    out = pl.pallas_call(body, name="adamw_small", out_shape=shapes * 3, compiler_params=_params())(*ws, *gs, *ms, *vs)
    return out[:n], out[n:2 * n], out[2 * n:]


MATRICES = ["ffn1_w13", "ffn1_w2", "w_in", "w_out", "ffn2_w13", "ffn2_w2"]
VECTORS = ["ffn1_norm", "mix_norm", "conv_b", "conv_ln_g", "conv_ln_b", "forget_b", "out_norm_conv",
           "out_norm_attn", "ffn2_norm", "final_norm"]
WEIGHTS = ["ffn1_norm", "ffn1_w13", "ffn1_w2", "mix_norm", "w_in", "conv_w", "conv_b", "conv_ln_g", "conv_ln_b",
           "forget_b", "out_norm_conv", "out_norm_attn", "w_out", "ffn2_norm", "ffn2_w13", "ffn2_w2", "final_norm"]


def _pack_small(g, names):
    rows, layout = [], []
    for n in names:
        flat = g[n].reshape(-1)
        pad = (-flat.shape[0]) % LANES
        rows.append(jnp.pad(flat, (0, pad)).reshape(-1, LANES))
        layout.append((n, g[n].shape, flat.shape[0], rows[-1].shape[0]))
    packed = jnp.concatenate(rows, axis=0)
    pad_rows = (-packed.shape[0]) % 8
    return jnp.pad(packed, ((0, pad_rows), (0, 0))), layout


def _unpack_small(packed, layout):
    out, r = {}, 0
    for n, shape, size, nrows in layout:
        out[n] = packed[r:r + nrows].reshape(-1)[:size].reshape(shape)
        r += nrows
    return out


def kernel(x, ffn1_norm, ffn1_w13, ffn1_w2, mix_norm, w_in, conv_w, conv_b, conv_ln_g, conv_ln_b, forget_b, out_norm_conv, out_norm_attn, w_out, ffn2_norm, ffn2_w13, ffn2_w2, final_norm, loss_target, m_ffn1_norm, m_ffn1_w13, m_ffn1_w2, m_mix_norm, m_w_in, m_conv_w, m_conv_b, m_conv_ln_g, m_conv_ln_b, m_forget_b, m_out_norm_conv, m_out_norm_attn, m_w_out, m_ffn2_norm, m_ffn2_w13, m_ffn2_w2, m_final_norm, v_ffn1_norm, v_ffn1_w13, v_ffn1_w2, v_mix_norm, v_w_in, v_conv_w, v_conv_b, v_conv_ln_g, v_conv_ln_b, v_forget_b, v_out_norm_conv, v_out_norm_attn, v_w_out, v_ffn2_norm, v_ffn2_w13, v_ffn2_w2, v_final_norm):
    args = dict(locals())
    weights = {n: args[n] for n in WEIGHTS}
    core = lax.axis_index("c").astype(jnp.int32).reshape(1)
    chip = (2 * lax.axis_index("x") + lax.axis_index("y")).astype(jnp.int32)
    chip1 = chip.reshape(1)
    chip_core = jnp.concatenate([chip1, core])

    def held(n, a):
        return a[0].T if n == "w_in" else a[0]

    def given(n, a):
        return (a.T if n == "w_in" else a)[None]

    def slot(n, deps=()):
        if n == "conv_w":
            rows = jnp.pad(conv_w[0], ((0, CONV_PAD - CONV_WIDTH), (0, 0)))
            return _into_slot(rows, chip1, F32, "slot_conv_w", deps)
        return _into_slot(held(n, weights[n]), chip1, BF16, "slot_" + n, deps)

    fetched = {"ffn1_w13": ["ffn1_w13"], "ffn1_w2": ["ffn1_w2"], "mix": ["w_in", "w_out", "conv_w"],
               "ffn2": ["ffn2_w13", "ffn2_w2"]}
    fetch = {}

    def as_weights(group, bufs):
        out = {}
        for n, b in zip(fetched[group], bufs):
            if n.endswith("w13"):
                out[n] = b
            elif n != "conv_w":
                out[n] = b.reshape(N_CHIPS * b.shape[1], b.shape[2])
            else:
                out[n] = b[:, :CONV_WIDTH].transpose(1, 0, 2).reshape(CONV_WIDTH, D_CONV)
        return out

    def get_weights(group, after):
        if group == "ffn1_w13":
            first = [slot("ffn1_w13")]
            plan = _ici_gather_plan(first)
            started = _split_copy_start("gather_ffn1_w13_start", first, 3, plan)
            second = [slot("ffn1_w2", [started[3]])]
            plan2 = _ici_gather_plan(second)
            fetch["ffn1_w2"] = plan2, _split_copy_start("gather_ffn1_w2_start", second, 3, plan2)
            later_names = fetched["mix"] + fetched["ffn2"]
            later = [slot(n, [fetch["ffn1_w2"][1][3]]) for n in later_names]
            landed = _split_copy_wait("gather_ffn1_w13_wait", started, plan, [], passed=later)
            bufs = _forward_halves(landed[:1], "forward_ffn1_w13")
            behind = dict(zip(later_names, landed[1:]))
            for later in ("mix", "ffn2"):
                bufs_later = [behind[n] for n in fetched[later]]
                plan = _ici_gather_plan(bufs_later)
                fetch[later] = plan, _split_copy_start("gather_%s_start" % later, bufs_later, 3 * len(bufs_later), plan)
            return as_weights(group, bufs), [fetch["mix"][1][3], fetch["ffn2"][1][3]]
        plan, started = fetch[group.split(":")[0]]
        if group == "ffn2:landed":
            landed = _split_copy_wait("gather_ffn2_wait", started, plan, [after])
            plan = _d2d_forward_plan(landed)
            fetch["ffn2"] = plan, _split_copy_start("forward_ffn2_start", landed, 3 * len(landed), plan)
            return {}, [fetch["ffn2"][1][3]]
        if group == "ffn2":
            return as_weights(group, _split_copy_wait("forward_ffn2_wait", started, plan, [after])), []
        landed = _split_copy_wait("gather_%s_wait" % group, started, plan, [after])
        return as_weights(group, _forward_halves(landed, "forward_" + group)), []

    def shard_major(n, g):
        return g if n.endswith("w13") else g.reshape(N_CHIPS, g.shape[0] // N_CHIPS, g.shape[1])

    exchange, scatter = {}, {}
    small_names = VECTORS + ["conv_w"]
    small = {}

    def put_grads(group, grads):
        if group == "small":
            packed, layout = _pack_small(grads, small_names + ["loss"])
            me = (4 * lax.axis_index("x") + 2 * lax.axis_index("y") + lax.axis_index("c")).astype(jnp.int32).reshape(1)
            plan = _small_plan()
            exchange[group] = layout, plan, _split_copy_start("small_start", [_small_slots(packed, me)], len(FLIPS), plan)
            return [exchange[group][2][3]]
        names = list(grads)
        local = [shard_major(n, grads[n]) for n in names]
        landing = [lax.empty((N_CHIPS,) + _half_shape(*a.shape[1:]), BF16) for a in local]
        plan = _pair_exchange_plan(local)
        exchange[group] = names, plan, _split_copy_start("exchange_%s_start" % group, local + landing, len(local), plan)
        return [exchange[group][2][3]]

    def flush_grads(group, after):
        names, plan, started = exchange[group]
        done = _split_copy_wait("exchange_%s_wait" % group, started, plan, after)
        local, sib = done[:len(names)], done[len(names):]
        parts = list(_pair_add(local, sib, core, "pair_add_" + group))
        landing = [lax.empty((N_CHIPS - 1,) + q.shape[1:], BF16) for q in parts]
        plan = _ici_scatter_plan(len(parts))
        shapes = [a.shape[1:] for a in local]
        scatter[group] = names, plan, _split_copy_start("scatter_%s_start" % group, parts + landing, 3 * len(parts), plan), shapes
        return [scatter[group][2][3]]

    p = {n: weights[n] for n in VECTORS}
    p["final_norm"] = final_norm.reshape(1, D_MODEL)
    dx = _local_step(x[0], loss_target[0], p, get_weights, put_grads, flush_grads)
    layout, plan, started = exchange["small"]
    slots, = _split_copy_wait("small_wait", started, plan, [exchange["ffn1_w13"][2][3]])
    small.update(_unpack_small(_small_sum(slots), layout))
    loss = small["loss"].reshape(())

    grad = {n: small[n] for n in VECTORS}
    grad["final_norm"] = small["final_norm"].reshape(D_MODEL)
    grad["conv_w"] = lax.dynamic_slice_in_dim(small["conv_w"], chip * (D_CONV // N_CHIPS), D_CONV // N_CHIPS, axis=1)[None]

    delta, new_m, new_v = {}, {}, {}

    def reduce_chips(group, after):
        names, plan, started, shapes = scatter[group]
        done = _split_copy_wait("scatter_%s_wait" % group, started, plan, after)
        parts, landed = done[:len(names)], done[len(names):]
        return list(_chip_add(parts, landed, chip_core, shapes, "chip_add_" + group))

    def update(group, full):
        ends = []
        for n, reduced in zip(scatter[group][0], full):
            go, d, mo, vo = _adamw_matrix(held(n, weights[n]), reduced, held(n, args["m_" + n]), held(n, args["v_" + n]),
                                          "adamw_" + n)
            grad[n], delta[n], new_m[n], new_v[n] = given(n, go), given(n, d), given(n, mo), given(n, vo)
            ends.append(vo)
        return ends

    def share_start(group, halves):
        plan = _pair_share_plan(scatter[group][3])
        return plan, _split_copy_start("share_%s_start" % group, halves, len(halves), plan)

    halves_ffn2 = reduce_chips("ffn2", [exchange["ffn1_w13"][2][3]])
    plan_ffn2, share_ffn2 = share_start("ffn2", halves_ffn2)
    last_scatter = flush_grads("ffn1_w13", [share_ffn2[3]])
    halves_mix = reduce_chips("mix", last_scatter)
    plan_mix, share_mix = share_start("mix", halves_mix)
    done_ffn2 = update("ffn2", _split_copy_wait("share_ffn2_wait", share_ffn2, plan_ffn2, [share_mix[3]]))
    done_mix = update("mix", _split_copy_wait("share_mix_wait", share_mix, plan_mix, done_ffn2))
    as2d = lambda a: a.reshape(-1, a.shape[-1])
    ds, mos, vos = _adamw_small([as2d(weights[n]) for n in small_names], [as2d(grad[n]) for n in small_names],
                                [as2d(args["m_" + n]) for n in small_names], [as2d(args["v_" + n]) for n in small_names])
    for n, d, mo, vo in zip(small_names, ds, mos, vos):
        shape = weights[n].shape
        delta[n], new_m[n], new_v[n] = d.reshape(shape), mo.reshape(shape), vo.reshape(shape)
    behind = done_ffn2 + done_mix + [vos[0]]
    halves_w2 = reduce_chips("ffn1_w2", behind)
    halves_w13 = reduce_chips("ffn1_w13", behind)
    full_w2, full_w13 = _pair_share(halves_w2 + halves_w13, "pair_share_ffn1")
    update("ffn1_w2", [full_w2])
    update("ffn1_w13", [full_w13])

    return (loss, dx[None], *[grad[n] for n in WEIGHTS], *[delta[n] for n in WEIGHTS],
            *[new_m[n] for n in WEIGHTS], *[new_v[n] for n in WEIGHTS])
```

```python
import jax
import jax.numpy as jnp
from jax import lax
from jax.experimental import pallas as pl
from jax.experimental.pallas import tpu as pltpu

F32 = jnp.float32
BF16 = jnp.bfloat16

D_MODEL = 1024
D_FF = 2816
FF_SHARD = D_FF // 2
D_CONV = 512
D_ATTN = 512
N_HEADS = 8
HEAD_DIM = 64
CONV_WIDTH = 31
CONV_PAD = 32
N_IN = 2 * D_CONV + 3 * D_ATTN + N_HEADS
EPS = 1e-6
N_CHIPS = 4
LANES = 128
TOKEN_ROWS = 512
HEAD_ROWS = 16

ADAM_LR = 0.001
ADAM_B1 = 0.9
ADAM_B2 = 0.999
ADAM_EPS = 1e-08
ADAM_WD = 0.01
ADAM_STEP = 10

VMEM_LIMIT = 56 * 1024 * 1024

_NT = (((1,), (1,)), ((), ()))
_TN = (((0,), (0,)), ((), ()))


def _dot(a, b):
    return jnp.dot(a, b, preferred_element_type=F32)


def _dot_nt(a, b):
    return lax.dot_general(a, b, _NT, preferred_element_type=F32)


def _dot_tn(a, b):
    return lax.dot_general(a, b, _TN, preferred_element_type=F32)


def _params(**kw):
    return pltpu.CompilerParams(vmem_limit_bytes=VMEM_LIMIT, **kw)


def _sigmoid(x):
    return 1.0 / (1.0 + jnp.exp(-x))


def _rms_stats(x):
    return lax.rsqrt(jnp.mean(x * x, axis=-1, keepdims=True) + EPS)


def _rms_bwd(x, r, g, dh):
    t = dh * g
    dx = r * t - x * (r * r * r) * jnp.mean(t * x, axis=-1, keepdims=True)
    return dx, dh * x * r


def _silu_grad(z, sg):
    return sg * (1.0 + z * (1.0 - sg))


def _row_spec(tm, n):
    return pl.BlockSpec((tm, n), lambda i: (i, 0))


def _full_spec(shape):
    nd = len(shape)
    return pl.BlockSpec(shape, lambda i: (0,) * nd)


_ANY = pl.BlockSpec(memory_space=pl.ANY)


def _skip(n, body):
    return lambda *refs: body(*refs[n:])


FFN_ROWS = 256
FFN_WEIGHT_PARTS = N_CHIPS + 2


def _with_ffn_weights(w13_hbm, w2_hbm, w13_ref, w2_ref, sems, order, tile):
    first = pl.program_id(0) == 0
    copies = {}
    if w13_hbm is not None:
        for k in range(N_CHIPS):
            copies["w13", k] = pltpu.make_async_copy(w13_hbm.at[k], w13_ref.at[k], sems.at[k])
    if w2_hbm is not None:
        for half in range(2):
            rows = pl.ds(half * FF_SHARD, FF_SHARD)
            copies["w2", half] = pltpu.make_async_copy(w2_hbm.at[rows, :], w2_ref.at[rows, :], sems.at[N_CHIPS + half])

    @pl.when(first)
    def _():
        for part in order:
            copies[part].start()

        def ready(*parts):
            for part in parts:
                copies[part].wait()

        tile(ready)

    @pl.when(jnp.logical_not(first))
    def _():
        tile(lambda *parts: None)


def _ffn_fwd(x, g, w13s, w2, name, deps=()):
    t = x.shape[0]
    tm = FFN_ROWS
    deps = tuple(deps)

    def body(x_ref, g_ref, w13_hbm, w2_hbm, xo_ref, h_ref, gu_ref, a_ref, w13_ref, w2_ref, sems):
        def tile(ready):
            xv = x_ref[...]
            hb = (xv * _rms_stats(xv) * g_ref[...]).astype(BF16)
            h_ref[...] = hb
            acc = jnp.zeros((tm, D_MODEL), F32)
            for half in range(2):
                lo = half * FF_SHARD
                ready(("w13", half), ("w13", 2 + half))
                gate = _dot(hb, w13_ref[half])
                up = _dot(hb, w13_ref[2 + half])
                gu_ref[:, lo:lo + FF_SHARD] = gate.astype(BF16)
                gu_ref[:, D_FF + lo:D_FF + lo + FF_SHARD] = up.astype(BF16)
                a = (gate * _sigmoid(gate) * up).astype(BF16)
                a_ref[:, lo:lo + FF_SHARD] = a
                ready(("w2", half))
                acc = acc + _dot(a, w2_ref[lo:lo + FF_SHARD, :])
            xo_ref[...] = xv + 0.5 * acc

        _with_ffn_weights(w13_hbm, w2_hbm, w13_ref, w2_ref, sems,
                          [("w13", 0), ("w13", 2), ("w2", 0), ("w13", 1), ("w13", 3), ("w2", 1)], tile)

    return pl.pallas_call(
        _skip(len(deps), body), name=name, grid=(t // tm,),
        in_specs=[_ANY] * len(deps) + [_row_spec(tm, D_MODEL), _full_spec((1, D_MODEL)), _ANY, _ANY],
        out_specs=[_row_spec(tm, D_MODEL), _row_spec(tm, D_MODEL), _row_spec(tm, 2 * D_FF), _row_spec(tm, D_FF)],
        out_shape=[jax.ShapeDtypeStruct((t, D_MODEL), F32), jax.ShapeDtypeStruct((t, D_MODEL), BF16),
                   jax.ShapeDtypeStruct((t, 2 * D_FF), BF16), jax.ShapeDtypeStruct((t, D_FF), BF16)],
        scratch_shapes=[pltpu.VMEM(w13s.shape, BF16), pltpu.VMEM(w2.shape, BF16),
                        pltpu.SemaphoreType.DMA((FFN_WEIGHT_PARTS,))],
        compiler_params=_params(dimension_semantics=("arbitrary",)),
    )(*deps, x, g, w13s, w2)


def _ffn_up(x, g, w13s, name, deps=()):
    t = x.shape[0]
    tm = FFN_ROWS
    deps = tuple(deps)

    def body(x_ref, g_ref, w13_hbm, h_ref, gu_ref, a_ref, w13_ref, sems):
        def tile(ready):
            xv = x_ref[...]
            hb = (xv * _rms_stats(xv) * g_ref[...]).astype(BF16)
            h_ref[...] = hb
            for half in range(2):
                lo = half * FF_SHARD
                ready(("w13", half), ("w13", 2 + half))
                gate = _dot(hb, w13_ref[half])
                up = _dot(hb, w13_ref[2 + half])
                gu_ref[:, lo:lo + FF_SHARD] = gate.astype(BF16)
                gu_ref[:, D_FF + lo:D_FF + lo + FF_SHARD] = up.astype(BF16)
                a_ref[:, lo:lo + FF_SHARD] = (gate * _sigmoid(gate) * up).astype(BF16)

        _with_ffn_weights(w13_hbm, None, w13_ref, None, sems, [("w13", 0), ("w13", 2), ("w13", 1), ("w13", 3)], tile)

    return pl.pallas_call(
        _skip(len(deps), body), name=name, grid=(t // tm,),
        in_specs=[_ANY] * len(deps) + [_row_spec(tm, D_MODEL), _full_spec((1, D_MODEL)), _ANY],
        out_specs=[_row_spec(tm, D_MODEL), _row_spec(tm, 2 * D_FF), _row_spec(tm, D_FF)],
        out_shape=[jax.ShapeDtypeStruct((t, D_MODEL), BF16), jax.ShapeDtypeStruct((t, 2 * D_FF), BF16),
                   jax.ShapeDtypeStruct((t, D_FF), BF16)],
        scratch_shapes=[pltpu.VMEM(w13s.shape, BF16), pltpu.SemaphoreType.DMA((FFN_WEIGHT_PARTS,))],
        compiler_params=_params(dimension_semantics=("arbitrary",)),
    )(*deps, x, g, w13s)


def _ffn_down(x, a, w2, name):
    t = x.shape[0]
    tm = FFN_ROWS

    def body(x_ref, a_ref, w2_hbm, xo_ref, w2_ref, sems):
        def tile(ready):
            ready(("w2", 0))
            acc = _dot(a_ref[:, 0:FF_SHARD], w2_ref[0:FF_SHARD, :])
            ready(("w2", 1))
            acc = acc + _dot(a_ref[:, FF_SHARD:], w2_ref[FF_SHARD:, :])
            xo_ref[...] = x_ref[...] + 0.5 * acc

        _with_ffn_weights(None, w2_hbm, None, w2_ref, sems, [("w2", 0), ("w2", 1)], tile)

    return pl.pallas_call(
        body, name=name, grid=(t // tm,),
        in_specs=[_row_spec(tm, D_MODEL), _row_spec(tm, D_FF), _ANY],
        out_specs=_row_spec(tm, D_MODEL), out_shape=jax.ShapeDtypeStruct((t, D_MODEL), F32),
        scratch_shapes=[pltpu.VMEM(w2.shape, BF16), pltpu.SemaphoreType.DMA((FFN_WEIGHT_PARTS,))],
        compiler_params=_params(dimension_semantics=("arbitrary",)),
    )(x, a, w2)


def _ffn_bwd(dy, x, gu, g, w13s, w2, name, deps=()):
    t = x.shape[0]
    tm = FFN_ROWS
    deps = tuple(deps)

    def body(dy_ref, x_ref, gu_ref, g_ref, w13_hbm, w2_hbm, dx_ref, dgu_ref, dg_ref, dyh_ref, dxb_ref,
             w13_ref, w2_ref, sems):
        @pl.when(pl.program_id(0) == 0)
        def _():
            dg_ref[...] = jnp.zeros_like(dg_ref)

        def tile(ready):
            dyv = dy_ref[...]
            dyh = (0.5 * dyv).astype(BF16)
            dyh_ref[...] = dyh
            dh = jnp.zeros((tm, D_MODEL), F32)
            for half in range(2):
                lo = half * FF_SHARD
                ready(("w2", half))
                da = _dot_nt(dyh, w2_ref[lo:lo + FF_SHARD, :])
                gate = gu_ref[:, lo:lo + FF_SHARD].astype(F32)
                up = gu_ref[:, D_FF + lo:D_FF + lo + FF_SHARD].astype(F32)
                sg = _sigmoid(gate)
                act = gate * sg
                dgate = (da * up * _silu_grad(gate, sg)).astype(BF16)
                dup = (da * act).astype(BF16)
                dgu_ref[:, lo:lo + FF_SHARD] = dgate
                dgu_ref[:, D_FF + lo:D_FF + lo + FF_SHARD] = dup
                ready(("w13", half), ("w13", 2 + half))
                dh = dh + _dot_nt(dgate, w13_ref[half]) + _dot_nt(dup, w13_ref[2 + half])
            xv = x_ref[...]
            dxn, dg_rows = _rms_bwd(xv, _rms_stats(xv), g_ref[...], dh)
            dx = dyv + dxn
            dx_ref[...] = dx
            dxb_ref[...] = dx.astype(BF16)
            dg_ref[...] += jnp.sum(dg_rows, axis=0, keepdims=True)

        _with_ffn_weights(w13_hbm, w2_hbm, w13_ref, w2_ref, sems,
                          [("w2", 0), ("w13", 0), ("w13", 2), ("w2", 1), ("w13", 1), ("w13", 3)], tile)

    return pl.pallas_call(
        _skip(len(deps), body), name=name, grid=(t // tm,),
        in_specs=[_ANY] * len(deps) + [_row_spec(tm, D_MODEL), _row_spec(tm, D_MODEL), _row_spec(tm, 2 * D_FF),
                                       _full_spec((1, D_MODEL)), _ANY, _ANY],
        out_specs=[_row_spec(tm, D_MODEL), _row_spec(tm, 2 * D_FF),
                   _full_spec((1, D_MODEL)), _row_spec(tm, D_MODEL), _row_spec(tm, D_MODEL)],
        out_shape=[jax.ShapeDtypeStruct((t, D_MODEL), F32), jax.ShapeDtypeStruct((t, 2 * D_FF), BF16),
                   jax.ShapeDtypeStruct((1, D_MODEL), F32),
                   jax.ShapeDtypeStruct((t, D_MODEL), BF16), jax.ShapeDtypeStruct((t, D_MODEL), BF16)],
        scratch_shapes=[pltpu.VMEM(w13s.shape, BF16), pltpu.VMEM(w2.shape, BF16),
                        pltpu.SemaphoreType.DMA((FFN_WEIGHT_PARTS,))],
        compiler_params=_params(dimension_semantics=("arbitrary",)),
    )(*deps, dy, x, gu, g, w13s, w2)


WGRAD_ROWS = (1408, 1024, 896, 512, 256)


def _wgrad(a, b, n_blocks, name, deps=()):
    t, m = a.shape
    tm = next(rows for rows in WGRAD_ROWS if m % rows == 0)
    n = b.shape[1]
    bn = n // n_blocks
    deps = tuple(deps)
    assert a.dtype == BF16 and b.dtype == BF16

    def body(a_ref, b_ref, o_ref):
        o_ref[0] = _dot_tn(a_ref[...], b_ref[...]).astype(BF16)

    return pl.pallas_call(
        _skip(len(deps), body), name=name, grid=(n_blocks, m // tm),
        in_specs=[_ANY] * len(deps) + [pl.BlockSpec((t, tm), lambda j, i: (0, i)),
                                       pl.BlockSpec((t, bn), lambda j, i: (0, j))],
        out_specs=pl.BlockSpec((1, tm, bn), lambda j, i: (j, i, 0)),
        out_shape=jax.ShapeDtypeStruct((n_blocks, m, bn), BF16),
        compiler_params=_params(dimension_semantics=("arbitrary", "arbitrary")),
    )(*deps, a, b)


def _mix_proj(x, g, w_ag, w_qkv, w_f):
    t = x.shape[0]
    tm = TOKEN_ROWS

    def body(x_ref, g_ref, wag_ref, wqkv_ref, wf_ref, h_ref, ag_ref, qkv_ref, fl_ref):
        xv = x_ref[...]
        hb = (xv * _rms_stats(xv) * g_ref[...]).astype(BF16)
        h_ref[...] = hb
        ag_ref[...] = _dot_nt(hb, wag_ref[...])
        qkv_ref[...] = _dot_nt(hb, wqkv_ref[...]).astype(BF16)
        fl_ref[...] = _dot_nt(hb, wf_ref[...])

    return pl.pallas_call(
        body, name="mix_proj", grid=(t // tm,),
        in_specs=[_row_spec(tm, D_MODEL), _full_spec((1, D_MODEL)), _full_spec(w_ag.shape),
                  _full_spec(w_qkv.shape), _full_spec(w_f.shape)],
        out_specs=[_row_spec(tm, D_MODEL), _row_spec(tm, 2 * D_CONV), _row_spec(tm, 3 * D_ATTN),
                   _row_spec(tm, LANES)],
        out_shape=[jax.ShapeDtypeStruct((t, D_MODEL), BF16), jax.ShapeDtypeStruct((t, 2 * D_CONV), F32),
                   jax.ShapeDtypeStruct((t, 3 * D_ATTN), BF16), jax.ShapeDtypeStruct((t, LANES), F32)],
        compiler_params=_params(dimension_semantics=("arbitrary",)),
    )(x, g, w_ag, w_qkv, w_f)


def _mix_proj_bwd(dproj, dx2, x1, g, w_ag, w_qkv, w_f):
    t = x1.shape[0]
    tm = TOKEN_ROWS
    n_ag, n_qkv = 2 * D_CONV, 3 * D_ATTN

    def body(dp_ref, dx2_ref, x_ref, g_ref, wag_ref, wqkv_ref, wf_ref, dx_ref, dg_ref):
        @pl.when(pl.program_id(0) == 0)
        def _():
            dg_ref[...] = jnp.zeros_like(dg_ref)

        dh = (_dot(dp_ref[:, 0:n_ag], wag_ref[...]) + _dot(dp_ref[:, n_ag:n_ag + n_qkv], wqkv_ref[...])
              + _dot(dp_ref[:, n_ag + n_qkv:], wf_ref[...]))
        xv = x_ref[...]
        dxn, dg_rows = _rms_bwd(xv, _rms_stats(xv), g_ref[...], dh)
        dx_ref[...] = dx2_ref[...] + dxn
        dg_ref[...] += jnp.sum(dg_rows, axis=0, keepdims=True)

    return pl.pallas_call(
        body, name="mix_proj_bwd", grid=(t // tm,),
        in_specs=[_row_spec(tm, dproj.shape[1]),
                  _row_spec(tm, D_MODEL), _row_spec(tm, D_MODEL), _full_spec((1, D_MODEL)),
                  _full_spec(w_ag.shape), _full_spec(w_qkv.shape), _full_spec(w_f.shape)],
        out_specs=[_row_spec(tm, D_MODEL), _full_spec((1, D_MODEL))],
        out_shape=[jax.ShapeDtypeStruct((t, D_MODEL), F32), jax.ShapeDtypeStruct((1, D_MODEL), F32)],
        compiler_params=_params(dimension_semantics=("arbitrary",)),
    )(dproj, dx2, x1, g, w_ag, w_qkv, w_f)


def _split3(x):
    hi = x.astype(BF16)
    r1 = x - hi.astype(F32)
    mid = r1.astype(BF16)
    lo = (r1 - mid.astype(F32)).astype(BF16)
    return hi, mid, lo


def _gates_fwd(flt, fb):
    t = flt.shape[1]

    def body(f_ref, b_ref, d_ref):
        z = f_ref[...] + b_ref[...]
        logf = jnp.minimum(z, 0.0) - jnp.log(1.0 + jnp.exp(-jnp.abs(z)))
        row = lax.broadcasted_iota(jnp.int32, (LANES, LANES), 0)
        col = lax.broadcasted_iota(jnp.int32, (LANES, LANES), 1)
        upper = (row <= col).astype(BF16)
        carry = jnp.zeros((HEAD_ROWS, 1), F32)
        for blk in range(t // LANES):
            hi, mid, lo = _split3(logf[:, blk * LANES:(blk + 1) * LANES])
            cs = _dot(hi, upper) + _dot(mid, upper) + _dot(lo, upper)
            d_ref[:, blk * LANES:(blk + 1) * LANES] = cs + carry
            carry = carry + cs[:, LANES - 1:LANES]

    return pl.pallas_call(
        body, name="gates_fwd", out_shape=jax.ShapeDtypeStruct((HEAD_ROWS, t), F32),
        compiler_params=_params(),
    )(flt, fb)


def _gates_bwd(dd, flt, fb):
    t = flt.shape[1]

    def body(dd_ref, f_ref, b_ref, df_ref, db_ref):
        z = f_ref[...] + b_ref[...]
        row = lax.broadcasted_iota(jnp.int32, (LANES, LANES), 0)
        col = lax.broadcasted_iota(jnp.int32, (LANES, LANES), 1)
        lower = (row >= col).astype(BF16)
        carry = jnp.zeros((HEAD_ROWS, 1), F32)
        db = jnp.zeros((HEAD_ROWS, 1), F32)
        for blk in reversed(range(t // LANES)):
            sl = slice(blk * LANES, (blk + 1) * LANES)
            hi, mid, lo = _split3(dd_ref[:, sl])
            cs = _dot(hi, lower) + _dot(mid, lower) + _dot(lo, lower)
            dz = (cs + carry) * _sigmoid(-z[:, sl])
            df_ref[:, sl] = dz
            db = db + jnp.sum(dz, axis=1, keepdims=True)
            carry = carry + cs[:, 0:1]
        db_ref[...] = db

    return pl.pallas_call(
        body, name="gates_bwd",
        out_shape=[jax.ShapeDtypeStruct((HEAD_ROWS, t), F32), jax.ShapeDtypeStruct((HEAD_ROWS, 1), F32)],
        compiler_params=_params(),
    )(dd, flt, fb)


CONV_CHUNK = 128
CONV_TAIL = 16
CONV_WINDOW = CONV_CHUNK + CONV_PAD + 8
CONV_ROWS_EXTRA = CONV_PAD + CONV_TAIL
SUBLANES = 8


def _conv_rows(ag_ref, u_ref, t):
    u_ref[0:CONV_PAD, :] = jnp.zeros((CONV_PAD, D_CONV), F32)
    u_ref[CONV_PAD + t:CONV_ROWS_EXTRA + t, :] = jnp.zeros((CONV_TAIL, D_CONV), F32)

    def fill(i, c):
        r0 = pl.multiple_of(i * CONV_CHUNK, CONV_CHUNK)
        a = ag_ref[pl.ds(r0, CONV_CHUNK), 0:D_CONV]
        gt = ag_ref[pl.ds(r0, CONV_CHUNK), D_CONV:2 * D_CONV]
        u_ref[pl.ds(CONV_PAD + r0, CONV_CHUNK), :] = a * _sigmoid(gt)
        return c

    lax.fori_loop(0, t // CONV_CHUNK, fill, 0)


def _for_shifted(ref, r0, offsets, fn):
    window = ref[pl.ds(r0, CONV_WINDOW), :]
    for rem in range(SUBLANES):
        mine = [o for o in offsets if o % SUBLANES == rem]
        if not mine:
            continue
        turned = window if rem == 0 else pltpu.roll(window, CONV_WINDOW - rem, 0)
        for o in mine:
            fn(o, turned[o - rem:o - rem + CONV_CHUNK])


def _conv_taps(u_ref, r0, w_ref, cb):
    acc = [jnp.zeros((CONV_CHUNK, D_CONV), F32)]

    def tap(o, rows):
        j = o - (CONV_PAD - CONV_WIDTH + 1)
        acc[0] = acc[0] + w_ref[j:j + 1, :] * rows

    _for_shifted(u_ref, r0, [j + CONV_PAD - CONV_WIDTH + 1 for j in range(CONV_WIDTH)], tap)
    return acc[0] + cb


def _conv_point(y, lg, lb):
    mu = jnp.mean(y, axis=-1, keepdims=True)
    yc = y - mu
    rstd = lax.rsqrt(jnp.mean(yc * yc, axis=-1, keepdims=True) + EPS)
    yhat = yc * rstd
    z = yhat * lg + lb
    sg = _sigmoid(z)
    s = z * sg
    rr = _rms_stats(s)
    return yhat, rstd, z, sg, s, rr


def _conv_fwd(ag, conv_w, conv_b, ln_g, ln_b, norm_g):
    t = ag.shape[0]

    def body(ag_ref, w_ref, cb_ref, lg_ref, lb_ref, ng_ref, o_ref, y_ref, u_ref):
        _conv_rows(ag_ref, u_ref, t)
        cb, lg, lb, ng = cb_ref[...], lg_ref[...], lb_ref[...], ng_ref[...]

        def chunk(i, c):
            r0 = pl.multiple_of(i * CONV_CHUNK, CONV_CHUNK)
            y = _conv_taps(u_ref, r0, w_ref, cb)
            y_ref[pl.ds(r0, CONV_CHUNK), :] = y
            _, _, _, _, s, rr = _conv_point(y, lg, lb)
            o_ref[pl.ds(r0, CONV_CHUNK), :] = (s * rr * ng).astype(BF16)
            return c

        lax.fori_loop(0, t // CONV_CHUNK, chunk, 0)

    return pl.pallas_call(
        body, name="conv_fwd",
        out_shape=[jax.ShapeDtypeStruct((t, D_CONV), BF16), jax.ShapeDtypeStruct((t, D_CONV), F32)],
        scratch_shapes=[pltpu.VMEM((t + CONV_ROWS_EXTRA, D_CONV), F32)],
        compiler_params=_params(),
    )(ag, conv_w, conv_b, ln_g, ln_b, norm_g)


def _conv_bwd(ag, y, dout, conv_w, ln_g, ln_b, norm_g):
    t = ag.shape[0]

    def body(ag_ref, y_ref, do_ref, w_ref, lg_ref, lb_ref, ng_ref,
             dag_ref, dw_ref, dcb_ref, dlg_ref, dlb_ref, dng_ref, u_ref, dy_ref):
        _conv_rows(ag_ref, u_ref, t)
        dy_ref[t:t + CONV_ROWS_EXTRA, :] = jnp.zeros((CONV_ROWS_EXTRA, D_CONV), F32)
        lg, lb, ng = lg_ref[...], lb_ref[...], ng_ref[...]
        dw_ref[...] = jnp.zeros_like(dw_ref)
        zero = jnp.zeros((1, D_CONV), F32)

        def chunk(i, carry):
            dcb, dlg, dlb, dng = carry
            r0 = pl.multiple_of(i * CONV_CHUNK, CONV_CHUNK)
            yhat, rstd, z, sg, s, rr = _conv_point(y_ref[pl.ds(r0, CONV_CHUNK), :], lg, lb)
            do = do_ref[pl.ds(r0, CONV_CHUNK), :]
            ds, dng_rows = _rms_bwd(s, rr, ng, do)
            dz = ds * _silu_grad(z, sg)
            dyhat = dz * lg
            dy = rstd * (dyhat - jnp.mean(dyhat, axis=-1, keepdims=True)
                         - yhat * jnp.mean(dyhat * yhat, axis=-1, keepdims=True))
            dy_ref[pl.ds(r0, CONV_CHUNK), :] = dy
            def tap(o, rows):
                j = o - (CONV_PAD - CONV_WIDTH + 1)
                dw_ref[j:j + 1, :] += jnp.sum(dy * rows, axis=0, keepdims=True)

            _for_shifted(u_ref, r0, [j + CONV_PAD - CONV_WIDTH + 1 for j in range(CONV_WIDTH)], tap)
            return (dcb + jnp.sum(dy, axis=0, keepdims=True), dlg + jnp.sum(dz * yhat, axis=0, keepdims=True),
                    dlb + jnp.sum(dz, axis=0, keepdims=True), dng + jnp.sum(dng_rows, axis=0, keepdims=True))

        dcb, dlg, dlb, dng = lax.fori_loop(0, t // CONV_CHUNK, chunk, (zero, zero, zero, zero))
        dcb_ref[...] = dcb
        dlg_ref[...] = dlg
        dlb_ref[...] = dlb
        dng_ref[...] = dng

        def chunk2(i, c):
            r0 = pl.multiple_of(i * CONV_CHUNK, CONV_CHUNK)
            acc = [jnp.zeros((CONV_CHUNK, D_CONV), F32)]

            def tap(o, rows):
                j = CONV_WIDTH - 1 - o
                acc[0] = acc[0] + w_ref[j:j + 1, :] * rows

            _for_shifted(dy_ref, r0, list(range(CONV_WIDTH)), tap)
            du = acc[0]
            a = ag_ref[pl.ds(r0, CONV_CHUNK), 0:D_CONV]
            gt = ag_ref[pl.ds(r0, CONV_CHUNK), D_CONV:2 * D_CONV]
            sg = _sigmoid(gt)
            dag_ref[pl.ds(r0, CONV_CHUNK), 0:D_CONV] = (du * sg).astype(BF16)
            dag_ref[pl.ds(r0, CONV_CHUNK), D_CONV:2 * D_CONV] = (du * a * sg * (1.0 - sg)).astype(BF16)
            return c

        lax.fori_loop(0, t // CONV_CHUNK, chunk2, 0)

    vec = jax.ShapeDtypeStruct((1, D_CONV), F32)
    return pl.pallas_call(
        body, name="conv_bwd",
        out_shape=[jax.ShapeDtypeStruct((t, 2 * D_CONV), BF16), jax.ShapeDtypeStruct((CONV_PAD, D_CONV), F32),
                   vec, vec, vec, vec],
        scratch_shapes=[pltpu.VMEM((t + CONV_ROWS_EXTRA, D_CONV), F32), pltpu.VMEM((t + CONV_ROWS_EXTRA, D_CONV), F32)],
        compiler_params=_params(),
    )(ag, y, dout, conv_w, ln_g, ln_b, norm_g)


Q_ROWS = 256
ATTN_SCALE = HEAD_DIM ** -0.5
ATTN_AHEAD = 1


def _attn_specs(t):
    blk = lambda off: pl.BlockSpec((t, LANES), lambda p: (0, off + p))
    pairs = N_HEADS // 2
    return [blk(0), blk(pairs), blk(2 * pairs), pl.BlockSpec((2, 1, t), lambda p: (p, 0, 0))]


def _one_head(q2, mask):
    return jnp.where(mask, q2, jnp.zeros_like(q2)) * ATTN_SCALE


def _attn_scores(qs, k2, drow, r0, q1):
    s = _dot_nt(qs, k2) - drow
    rowi = lax.broadcasted_iota(jnp.int32, (q1 - r0, q1 - r0), 0)
    coli = lax.broadcasted_iota(jnp.int32, (q1 - r0, q1 - r0), 1)
    diag = jnp.where(coli <= rowi, s[:, r0:q1], -jnp.inf)
    return diag if r0 == 0 else jnp.concatenate([s[:, :r0], diag], axis=1)


def _attn_fwd(qkv, drow, deps=()):
    t = qkv.shape[0]
    deps = tuple(deps)

    def body(q_ref, k_ref, v_ref, dr_ref, o_ref, lse_ref):
        head_a = lax.broadcasted_iota(jnp.int32, (1, LANES), 1) < HEAD_DIM
        items = [(qb, hh) for qb in range(t // Q_ROWS) for hh in range(2)]

        def scores(item):
            qb, hh = item
            r0, q1 = qb * Q_ROWS, (qb + 1) * Q_ROWS
            qs = _one_head(q_ref[r0:q1, :], head_a if hh == 0 else ~head_a)
            return _attn_scores(qs, k_ref[0:q1, :], dr_ref[hh, :, 0:q1], r0, q1)

        ahead = [scores(item) for item in items[:ATTN_AHEAD]]
        outs = []
        for n, (qb, hh) in enumerate(items):
            r0, q1 = qb * Q_ROWS, (qb + 1) * Q_ROWS
            s = ahead.pop(0)
            if n + ATTN_AHEAD < len(items):
                ahead.append(scores(items[n + ATTN_AHEAD]))
            mx = jnp.max(s, axis=1, keepdims=True)
            p = jnp.exp(s - mx)
            l = jnp.sum(p, axis=1, keepdims=True)
            lse_ref[hh, r0:q1, :] = mx + jnp.log(l)
            outs.append(_dot(p.astype(BF16), v_ref[0:q1, :]) * (1.0 / l))
            if hh == 1:
                o_ref[r0:q1, :] = jnp.where(head_a, outs[0], outs[1])
                outs = []

    pairs = N_HEADS // 2
    return pl.pallas_call(
        _skip(len(deps), body), name="attn_fwd", grid=(pairs,), in_specs=[_ANY] * len(deps) + _attn_specs(t),
        out_specs=[pl.BlockSpec((t, LANES), lambda p: (0, p)), pl.BlockSpec((2, t, 1), lambda p: (p, 0, 0))],
        out_shape=[jax.ShapeDtypeStruct((t, D_ATTN), F32), jax.ShapeDtypeStruct((N_HEADS, t, 1), F32)],
        compiler_params=_params(dimension_semantics=("arbitrary",)),
    )(*deps, qkv, qkv, qkv, drow)


def _attn_bwd(qkv, drow, lse, do):
    t = qkv.shape[0]

    def body(q_ref, k_ref, v_ref, dr_ref, lse_ref, do_ref,
             dq_ref, dk_ref, dv_ref, dd_ref, dk_acc, dv_acc):
        head_a = lax.broadcasted_iota(jnp.int32, (1, LANES), 1) < HEAD_DIM
        dk_acc[...] = jnp.zeros_like(dk_acc)
        dv_acc[...] = jnp.zeros_like(dv_acc)
        dd_ref[...] = jnp.zeros_like(dd_ref)
        items = [(qb, hh) for qb in range(t // Q_ROWS) for hh in range(2)]

        def products(item):
            qb, hh = item
            r0, q1 = qb * Q_ROWS, (qb + 1) * Q_ROWS
            mask = head_a if hh == 0 else ~head_a
            qs = _one_head(q_ref[r0:q1, :], mask)
            dob = jnp.where(mask, do_ref[r0:q1, :], 0.0).astype(BF16)
            s = _attn_scores(qs, k_ref[0:q1, :], dr_ref[hh, :, 0:q1], r0, q1)
            return qs, dob, s, _dot_nt(dob, v_ref[0:q1, :])

        ahead = products(items[0])
        dqs = []
        for n, (qb, hh) in enumerate(items):
            r0, q1 = qb * Q_ROWS, (qb + 1) * Q_ROWS
            qs, dob, s, dp = ahead
            if n + 1 < len(items):
                ahead = products(items[n + 1])
            p = jnp.exp(s - lse_ref[hh, r0:q1, :])
            ds = p * (dp - jnp.sum(p * dp, axis=1, keepdims=True))
            dsb = ds.astype(BF16)
            dqs.append(_dot(dsb, k_ref[0:q1, :]) * ATTN_SCALE)
            dk_acc[0:q1, :] += _dot_tn(dsb, qs)
            dv_acc[0:q1, :] += _dot_tn(p.astype(BF16), dob)
            dd_ref[hh, :, 0:q1] -= jnp.sum(ds, axis=0, keepdims=True)
            if hh == 1:
                dq_ref[r0:q1, :] = jnp.where(head_a, dqs[0], dqs[1]).astype(BF16)
                dqs = []
        dk_ref[...] = dk_acc[...].astype(BF16)
        dv_ref[...] = dv_acc[...].astype(BF16)

    pairs = N_HEADS // 2
    col = pl.BlockSpec((t, LANES), lambda p: (0, p))
    grad = jax.ShapeDtypeStruct((t, D_ATTN), BF16)
    return pl.pallas_call(
        body, name="attn_bwd", grid=(pairs,),
        in_specs=_attn_specs(t) + [pl.BlockSpec((2, t, 1), lambda p: (p, 0, 0)), col],
        out_specs=[col, col, col, pl.BlockSpec((2, 1, t), lambda p: (p, 0, 0))],
        out_shape=[grad, grad, grad, jax.ShapeDtypeStruct((N_HEADS, 1, t), F32)],
        scratch_shapes=[pltpu.VMEM((t, LANES), F32), pltpu.VMEM((t, LANES), F32)],
        compiler_params=_params(dimension_semantics=("arbitrary",)),
    )(qkv, qkv, qkv, drow, lse, do)


def _out_proj(ycn, o, g_attn, w_out, x1, deps=()):
    t = x1.shape[0]
    tm = TOKEN_ROWS
    deps = tuple(deps)

    def body(yc_ref, o_ref, g_ref, w_ref, x_ref, xo_ref, ya_ref):
        ov = o_ref[...]
        ya = (ov * _rms_stats(ov) * g_ref[...]).astype(BF16)
        ya_ref[...] = ya
        xo_ref[...] = x_ref[...] + _dot(yc_ref[...], w_ref[0:D_CONV, :]) + _dot(ya, w_ref[D_CONV:, :])

    return pl.pallas_call(
        _skip(len(deps), body), name="out_proj", grid=(t // tm,),
        in_specs=[_ANY] * len(deps) + [_row_spec(tm, D_CONV), _row_spec(tm, D_ATTN), _full_spec((1, D_ATTN)),
                                       _full_spec(w_out.shape), _row_spec(tm, D_MODEL)],
        out_specs=[_row_spec(tm, D_MODEL), _row_spec(tm, D_ATTN)],
        out_shape=[jax.ShapeDtypeStruct((t, D_MODEL), F32), jax.ShapeDtypeStruct((t, D_ATTN), BF16)],
        compiler_params=_params(dimension_semantics=("arbitrary",)),
    )(*deps, ycn, o, g_attn, w_out, x1)


def _out_proj_bwd(dx2, o, g_attn, w_out, deps=()):
    t = dx2.shape[0]
    tm = TOKEN_ROWS
    deps = tuple(deps)

    def body(dx_ref, o_ref, g_ref, w_ref, dyc_ref, do_ref, dg_ref):
        @pl.when(pl.program_id(0) == 0)
        def _():
            dg_ref[...] = jnp.zeros_like(dg_ref)

        dxb = dx_ref[...]
        dyc_ref[...] = _dot_nt(dxb, w_ref[0:D_CONV, :])
        dya = _dot_nt(dxb, w_ref[D_CONV:, :])
        ov = o_ref[...]
        do, dg_rows = _rms_bwd(ov, _rms_stats(ov), g_ref[...], dya)
        do_ref[...] = do
        dg_ref[...] += jnp.sum(dg_rows, axis=0, keepdims=True)

    return pl.pallas_call(
        _skip(len(deps), body), name="out_proj_bwd", grid=(t // tm,),
        in_specs=[_ANY] * len(deps) + [_row_spec(tm, D_MODEL), _row_spec(tm, D_ATTN), _full_spec((1, D_ATTN)),
                                       _full_spec(w_out.shape)],
        out_specs=[_row_spec(tm, D_CONV), _row_spec(tm, D_ATTN), _full_spec((1, D_ATTN))],
        out_shape=[jax.ShapeDtypeStruct((t, D_CONV), F32), jax.ShapeDtypeStruct((t, D_ATTN), F32),
                   jax.ShapeDtypeStruct((1, D_ATTN), F32)],
        compiler_params=_params(dimension_semantics=("arbitrary",)),
    )(*deps, dx2, o, g_attn, w_out)


def _loss_bwd(x3, target, g):
    t = x3.shape[0]
    tm = TOKEN_ROWS

    def body(x_ref, t_ref, g_ref, loss_ref, dx_ref, dg_ref):
        @pl.when(pl.program_id(0) == 0)
        def _():
            loss_ref[...] = jnp.zeros_like(loss_ref)
            dg_ref[...] = jnp.zeros_like(dg_ref)

        xv = x_ref[...]
        r = _rms_stats(xv)
        gv = g_ref[...]
        err = xv * r * gv - t_ref[...]
        row = jnp.sum(err * err, axis=1, keepdims=True) * (0.5 / D_MODEL)
        loss_ref[...] += jnp.sum(row, axis=0, keepdims=True)
        dx, dg_rows = _rms_bwd(xv, r, gv, err * (1.0 / D_MODEL))
        dx_ref[...] = dx
        dg_ref[...] += jnp.sum(dg_rows, axis=0, keepdims=True)

    return pl.pallas_call(
        body, name="loss_bwd", grid=(t // tm,),
        in_specs=[_row_spec(tm, D_MODEL), _row_spec(tm, D_MODEL), _full_spec((1, D_MODEL))],
        out_specs=[_full_spec((1, LANES)), _row_spec(tm, D_MODEL), _full_spec((1, D_MODEL))],
        out_shape=[jax.ShapeDtypeStruct((1, LANES), F32), jax.ShapeDtypeStruct((t, D_MODEL), F32),
                   jax.ShapeDtypeStruct((1, D_MODEL), F32)],
        compiler_params=_params(dimension_semantics=("arbitrary",)),
    )(x3, target, g)


def _split_w_in(w_in_t):
    w_ag = w_in_t[:2 * D_CONV]
    w_qkv = w_in_t[2 * D_CONV:2 * D_CONV + 3 * D_ATTN]
    w_f = jnp.pad(w_in_t[2 * D_CONV + 3 * D_ATTN:], ((0, LANES - N_HEADS), (0, 0)))
    return w_ag, w_qkv, w_f


def _head_rows(v):
    return jnp.pad(v, ((0, HEAD_ROWS - N_HEADS),) + ((0, 0),) * (v.ndim - 1))


def _local_step(x, target, p, get_weights, put_grads, flush_grads):
    t = x.shape[0]
    fb = _head_rows(p["forget_b"].reshape(N_HEADS, 1))

    w, deps = get_weights("ffn1_w13", None)
    h1, gu1, act1 = _ffn_up(x, p["ffn1_norm"], w["ffn1_w13"], "ffn1_up", deps)
    w2, _ = get_weights("ffn1_w2", act1)
    w.update(w2)
    x1 = _ffn_down(x, act1, w["ffn1_w2"], "ffn1_down")
    wm, _ = get_weights("mix", x1)
    w.update(wm)
    w_ag, w_qkv, w_f = _split_w_in(w["w_in"])
    conv_w = jnp.pad(w["conv_w"], ((0, CONV_PAD - CONV_WIDTH), (0, 0)))
    h2, ag, qkv, fl = _mix_proj(x1, p["mix_norm"], w_ag, w_qkv, w_f)
    flt = _head_rows(fl[:, :N_HEADS].T)
    dcum = _gates_fwd(flt, fb)[:N_HEADS]
    drow = dcum.reshape(N_HEADS, 1, t)
    ycn, y_conv = _conv_fwd(ag, conv_w, p["conv_b"], p["conv_ln_g"], p["conv_ln_b"], p["out_norm_conv"])
    o, lse = _attn_fwd(qkv, drow, [ycn])
    _, deps = get_weights("ffn2:landed", o)
    x2, yan = _out_proj(ycn, o, p["out_norm_attn"], w["w_out"], x1, deps)
    w2, _ = get_weights("ffn2", x2)
    w.update(w2)
    x3, h3, gu2, act2 = _ffn_fwd(x2, p["ffn2_norm"], w["ffn2_w13"], w["ffn2_w2"], "ffn2_fwd")
    loss, dx3, d_final = _loss_bwd(x3, target, p["final_norm"])

    g = {}
    dx2, dgu2, g["ffn2_norm"], dx3_half, dx2_bf16 = _ffn_bwd(
        dx3, x2, gu2, p["ffn2_norm"], w["ffn2_w13"], w["ffn2_w2"], "ffn2_bwd")
    dw13 = _wgrad(h3, dgu2, N_CHIPS, "ffn2_dw13")
    dw2 = _wgrad(act2, dx3_half, 1, "ffn2_dw2").reshape(D_FF, D_MODEL)
    deps = put_grads("ffn2", {"ffn2_w13": dw13, "ffn2_w2": dw2})
    dyc, do, g["out_norm_attn"] = _out_proj_bwd(dx2_bf16, o, p["out_norm_attn"], w["w_out"], deps)
    deps = flush_grads("ffn2", [dyc])
    dw_out = _wgrad(jnp.concatenate([ycn, yan], axis=1), dx2_bf16, 1, "dw_out", deps).reshape(D_MODEL, D_MODEL)
    dq, dk, dv, ddrow = _attn_bwd(qkv, drow, lse, do)
    dflt, dfb = _gates_bwd(_head_rows(ddrow.reshape(N_HEADS, t)), flt, fb)
    g["forget_b"] = dfb[:N_HEADS, 0].reshape(1, N_HEADS)
    dfl = jnp.pad(dflt[:N_HEADS].T, ((0, 0), (0, LANES - N_HEADS)))
    dag, dconv_w, g["conv_b"], g["conv_ln_g"], g["conv_ln_b"], g["out_norm_conv"] = _conv_bwd(
        ag, y_conv, dyc, conv_w, p["conv_ln_g"], p["conv_ln_b"], p["out_norm_conv"])
    g["conv_w"] = dconv_w[:CONV_WIDTH]
    dproj = jnp.concatenate([dag, dq, dk, dv, dfl.astype(BF16)], axis=1)
    dx1, g["mix_norm"] = _mix_proj_bwd(dproj, dx2, x1, p["mix_norm"], w_ag, w_qkv, w_f)
    dw_in = _wgrad(dproj, h2, 1, "dw_in").reshape(dproj.shape[1], D_MODEL)[:N_IN]
    deps = put_grads("mix", {"w_in": dw_in, "w_out": dw_out})
    dx0, dgu1, g["ffn1_norm"], dx1_half, _ = _ffn_bwd(
        dx1, x, gu1, p["ffn1_norm"], w["ffn1_w13"], w["ffn1_w2"], "ffn1_bwd", deps)
    g["final_norm"] = d_final
    g["loss"] = loss[:, :1]
    deps = flush_grads("mix", put_grads("small", g))
    dw2 = _wgrad(act1, dx1_half, 1, "ffn1_dw2", deps).reshape(D_FF, D_MODEL)
    deps = flush_grads("ffn1_w2", put_grads("ffn1_w2", {"ffn1_w2": dw2}))
    dw13 = _wgrad(h1, dgu1, N_CHIPS, "ffn1_dw13", deps)
    put_grads("ffn1_w13", {"ffn1_w13": dw13})
    return dx0


MESH = pl.DeviceIdType.MESH


def _place():
    x, y, c = lax.axis_index("x"), lax.axis_index("y"), lax.axis_index("c")
    chips = [(1 - x, y), (x, 1 - y), (1 - x, 1 - y)]
    return x, y, c, chips


def _hbm_out(shape, dtype):
    return jax.ShapeDtypeStruct(shape, dtype)


def _comm_call(body, name, ins, out_shapes, n_remote, in_place=False):
    return pl.pallas_call(
        body, name=name, in_specs=[_ANY] * len(ins), out_specs=[_ANY] * len(out_shapes), out_shape=out_shapes,
        scratch_shapes=[pltpu.SemaphoreType.DMA((n_remote,)), pltpu.SemaphoreType.DMA((n_remote,))],
        input_output_aliases={i: i for i in range(len(ins))} if in_place else {},
    )(*ins)


def _remote(src, dst, sems, n, to):
    send_sems, recv_sems = sems
    return pltpu.make_async_remote_copy(src_ref=src, dst_ref=dst, send_sem=send_sems.at[n], recv_sem=recv_sems.at[n],
                                        device_id=to, device_id_type=MESH)


HALF_ROWS_MULTIPLE = 32


def _halved_by_rows(rows):
    return rows % HALF_ROWS_MULTIPLE == 0


def _half_shape(rows, cols):
    return (rows // 2, cols) if _halved_by_rows(rows) else (rows, cols // 2)


def _half_index(rows, core):
    return (core, 0) if _halved_by_rows(rows) else (0, core)


def _half_of(ref, rows, cols, core, *lead):
    if _halved_by_rows(rows):
        return ref.at[(*lead, pl.ds(core * (rows // 2), rows // 2), slice(None))]
    return ref.at[(*lead, slice(None), pl.ds(core * (cols // 2), cols // 2))]


def _into_slot(shard, chip, dtype, name, deps=()):
    rows, cols = shard.shape
    half = _half_shape(rows, cols)
    by_rows = _halved_by_rows(rows)
    deps = tuple(deps)

    def body(k_ref, *refs):
        s_ref, o_ref = refs[len(deps):]
        o_ref[0] = s_ref[...].astype(dtype)

    return pl.pallas_call(
        body, name=name,
        grid_spec=pltpu.PrefetchScalarGridSpec(
            num_scalar_prefetch=1, grid=(2,),
            in_specs=[_ANY] * len(deps) + [pl.BlockSpec(half, lambda i, k_ref: (i, 0) if by_rows else (0, i))],
            out_specs=pl.BlockSpec((1,) + half, lambda i, k_ref: (k_ref[0], i, 0) if by_rows else (k_ref[0], 0, i))),
        out_shape=jax.ShapeDtypeStruct((N_CHIPS, rows, cols), dtype),
        compiler_params=_params(dimension_semantics=("arbitrary",)),
    )(chip, *deps, shard)


def _run_copies(name, bufs, n_copies, plan):
    n = len(bufs)

    def body(*refs):
        copies = plan(refs[n:2 * n], refs[2 * n:2 * n + 2])
        for send, _ in copies:
            send.start()
        for send, recv in copies:
            send.wait_send()
            recv.wait_recv()

    return _comm_call(body, name, bufs, [_hbm_out(b.shape, b.dtype) for b in bufs], n_copies, in_place=True)


def _forward_halves(slots, name):
    return _run_copies(name, slots, 3 * len(slots), _d2d_forward_plan(slots))


_HBM = pl.BlockSpec(memory_space=pltpu.HBM)
_SEM = pl.BlockSpec(memory_space=pltpu.SEMAPHORE)
_DATAFLOW = pltpu.SideEffectType.DATAFLOW_SIDE_EFFECTING


def _split_copy_start(name, bufs, n_copies, plan):
    n = len(bufs)

    def body(*refs):
        for send, _ in plan(refs[:n], (refs[n], refs[n + 1])):
            send.start()
        token = refs[-1]
        token[...] = jnp.zeros_like(token)

    out = pl.pallas_call(
        body, name=name,
        out_shape=(pltpu.SemaphoreType.DMA((n_copies,)), pltpu.SemaphoreType.DMA((n_copies,)),
                   *[pltpu.HBM(b.shape, b.dtype) for b in bufs], jax.ShapeDtypeStruct((8, LANES), F32)),
        in_specs=[_HBM] * n, out_specs=(_SEM, _SEM, *[_HBM] * n, pl.BlockSpec(memory_space=pltpu.VMEM)),
        input_output_aliases={i: 2 + i for i in range(n)},
        compiler_params=pltpu.CompilerParams(has_side_effects=_DATAFLOW),
    )(*[pltpu.with_memory_space_constraint(b, pltpu.HBM) for b in bufs])
    return out[0], out[1], list(out[2:2 + n]), out[-1]


def _split_copy_wait(name, started, plan, after, passed=()):
    send_sems, recv_sems, bufs, _ = started
    n = len(bufs)
    after = tuple(after)
    bufs = list(bufs) + list(passed)
    total = len(bufs)

    def body(*refs):
        for send, recv in plan(refs[:n], (refs[total], refs[total + 1])):
            send.wait_send()
            recv.wait_recv()

    out = pl.pallas_call(
        body, name=name, out_shape=tuple(pltpu.HBM(b.shape, b.dtype) for b in bufs),
        in_specs=[_HBM] * total + [_SEM, _SEM] + [_ANY] * len(after), out_specs=tuple([_HBM] * total),
        input_output_aliases={i: i for i in range(total)},
        compiler_params=pltpu.CompilerParams(has_side_effects=_DATAFLOW),
    )(*bufs, send_sems, recv_sems, *after)
    return list(out)


def _ici_gather_plan(slots):
    def plan(refs, sems):
        x, y, c, chips = _place()
        me = 2 * x + y
        copies = []
        for i, ref in enumerate(refs):
            for j, chip in enumerate(chips):
                mine = _half_of(ref, *slots[i].shape[1:], c, me)
                theirs = _half_of(ref, *slots[i].shape[1:], c, 2 * chip[0] + chip[1])
                to = (*chip, c)
                copies.append((_remote(mine, mine, sems, 3 * i + j, to), _remote(theirs, theirs, sems, 3 * i + j, to)))
        return copies

    return plan


def _ici_scatter_plan(n):
    def plan(refs, sems):
        x, y, c, chips = _place()
        copies = []
        for i in range(n):
            for j, chip in enumerate(chips):
                cp = _remote(refs[i].at[2 * chip[0] + chip[1]], refs[n + i].at[j], sems, 3 * i + j, (*chip, c))
                copies.append((cp, cp))
        return copies

    return plan


def _d2d_forward_plan(slots):
    def plan(refs, sems):
        x, y, c, chips = _place()
        sibling = (x, y, 1 - c)
        copies = []
        for i, ref in enumerate(refs):
            for j, chip in enumerate(chips):
                src_chip = 2 * chip[0] + chip[1]
                mine = _half_of(ref, *slots[i].shape[1:], c, src_chip)
                theirs = _half_of(ref, *slots[i].shape[1:], 1 - c, src_chip)
                copies.append((_remote(mine, mine, sems, 3 * i + j, sibling),
                               _remote(theirs, theirs, sems, 3 * i + j, sibling)))
        return copies

    return plan


def _pair_exchange_plan(grads):
    n = len(grads)

    def plan(refs, sems):
        x, y, c, _ = _place()
        copies = []
        for i in range(n):
            theirs = _half_of(refs[i], *grads[i].shape[1:], 1 - c, slice(None))
            cp = _remote(theirs, refs[n + i], sems, i, (x, y, 1 - c))
            copies.append((cp, cp))
        return copies

    return plan


def _pair_share_plan(shapes):
    def plan(refs, sems):
        x, y, c, _ = _place()
        sibling = (x, y, 1 - c)
        copies = []
        for i, ref in enumerate(refs):
            mine, theirs = _half_of(ref, *shapes[i], c), _half_of(ref, *shapes[i], 1 - c)
            copies.append((_remote(mine, mine, sems, i, sibling), _remote(theirs, theirs, sems, i, sibling)))
        return copies

    return plan


def _pair_share(halves, name):
    return _run_copies(name, halves, len(halves), _pair_share_plan([h.shape for h in halves]))


N_DEVICES = 8
FLIPS = [(fx, fy, fc) for fx in range(2) for fy in range(2) for fc in range(2)][1:]


def _small_slots(v, me):
    rows = v.shape[0]

    def body(k_ref, v_ref, o_ref):
        o_ref[0] = v_ref[...]

    return pl.pallas_call(
        body, name="small_slot",
        grid_spec=pltpu.PrefetchScalarGridSpec(
            num_scalar_prefetch=1, grid=(1,),
            in_specs=[pl.BlockSpec((rows, LANES), lambda i, k_ref: (0, 0))],
            out_specs=pl.BlockSpec((1, rows, LANES), lambda i, k_ref: (k_ref[0], 0, 0))),
        out_shape=jax.ShapeDtypeStruct((N_DEVICES, rows, LANES), F32),
        compiler_params=_params(dimension_semantics=("arbitrary",)),
    )(me, v)


def _small_plan():
    def plan(refs, sems):
        x, y, c, _ = _place()
        slots = refs[0]
        me = 4 * x + 2 * y + c
        copies = []
        for n, (fx, fy, fc) in enumerate(FLIPS):
            to = (x ^ fx, y ^ fy, c ^ fc)
            src = 4 * to[0] + 2 * to[1] + to[2]
            copies.append((_remote(slots.at[me], slots.at[me], sems, n, to), _remote(slots.at[src], slots.at[src], sems, n, to)))
        return copies

    return plan


def _small_sum(slots):
    def body(s_ref, o_ref):
        acc = s_ref[0]
        for s in range(1, N_DEVICES):
            acc = acc + s_ref[s]
        o_ref[...] = acc

    return pl.pallas_call(body, name="small_sum", out_shape=jax.ShapeDtypeStruct(slots.shape[1:], F32),
                          compiler_params=_params())(slots)


def _pair_add(gs, sibs, core, name):
    n = len(gs)
    halves = [_half_shape(*g.shape[1:]) for g in gs]

    def body(c_ref, *refs):
        for g_ref, s_ref, o_ref in zip(refs[:n], refs[n:2 * n], refs[2 * n:]):
            o_ref[0] = (g_ref[0].astype(F32) + s_ref[0].astype(F32)).astype(BF16)

    def mine(g, half):
        return pl.BlockSpec((1,) + half, lambda s, c_ref: (s, *_half_index(g.shape[1], c_ref[0])))

    whole = [pl.BlockSpec((1,) + half, lambda s, c_ref: (s, 0, 0)) for half in halves]
    return pl.pallas_call(
        body, name=name,
        grid_spec=pltpu.PrefetchScalarGridSpec(
            num_scalar_prefetch=1, grid=(N_CHIPS,),
            in_specs=[mine(g, half) for g, half in zip(gs, halves)] + whole, out_specs=whole),
        out_shape=[jax.ShapeDtypeStruct((N_CHIPS,) + half, BF16) for half in halves],
        compiler_params=_params(dimension_semantics=("arbitrary",)),
    )(core, *gs, *sibs)


CHIP_ADD_STEPS = 2


def _chip_add(parts, recvs, chip_core, shapes, name):
    n = len(parts)
    by_rows = [_halved_by_rows(shape[0]) for shape in shapes]
    pieces = [(h[0] // CHIP_ADD_STEPS, h[1]) if rows else (h[0], h[1] // CHIP_ADD_STEPS)
              for h, rows in zip((_half_shape(*shape) for shape in shapes), by_rows)]

    def body(kc_ref, *refs):
        for p_ref, r_ref, o_ref in zip(refs[:n], refs[n:2 * n], refs[2 * n:]):
            acc = p_ref[0].astype(F32)
            for j in range(N_CHIPS - 1):
                acc = acc + r_ref[j].astype(F32)
            o_ref[...] = acc

    def at(rows, lead, piece_of):
        return lambda s, kc_ref: (*lead(kc_ref), piece_of(s, kc_ref), 0) if rows else (*lead(kc_ref), 0, piece_of(s, kc_ref))

    mine = [pl.BlockSpec((1,) + p, at(rows, lambda kc_ref: (kc_ref[0],), lambda s, kc_ref: s)) for p, rows in zip(pieces, by_rows)]
    theirs = [pl.BlockSpec((N_CHIPS - 1,) + p, at(rows, lambda kc_ref: (0,), lambda s, kc_ref: s)) for p, rows in zip(pieces, by_rows)]
    out = [pl.BlockSpec(p, at(rows, lambda kc_ref: (), lambda s, kc_ref: kc_ref[1] * CHIP_ADD_STEPS + s))
           for p, rows in zip(pieces, by_rows)]
    return pl.pallas_call(
        body, name=name,
        grid_spec=pltpu.PrefetchScalarGridSpec(
            num_scalar_prefetch=1, grid=(CHIP_ADD_STEPS,), in_specs=mine + theirs, out_specs=out),
        out_shape=[jax.ShapeDtypeStruct(tuple(shape), F32) for shape in shapes],
        compiler_params=_params(dimension_semantics=("arbitrary",)),
    )(chip_core, *parts, *recvs)


def _adamw_math(w, g, m, v):
    m = ADAM_B1 * m + (1.0 - ADAM_B1) * g
    v = ADAM_B2 * v + (1.0 - ADAM_B2) * (g * g)
    m_hat = m / (1.0 - ADAM_B1 ** ADAM_STEP)
    v_hat = v / (1.0 - ADAM_B2 ** ADAM_STEP)
    delta = -ADAM_LR * (m_hat / (jnp.sqrt(v_hat) + ADAM_EPS) + ADAM_WD * w)
    return delta, m, v


ADAM_PARTS = 2


def _adamw_matrix(w, g, m, v, name):
    rows, cols = w.shape
    by_rows = rows % (8 * ADAM_PARTS) == 0
    block = (rows // ADAM_PARTS, cols) if by_rows else (rows, cols // ADAM_PARTS)

    def body(w_ref, g_ref, m_ref, v_ref, go_ref, d_ref, mo_ref, vo_ref):
        gv = g_ref[...]
        go_ref[...] = gv
        d_ref[...], mo_ref[...], vo_ref[...] = _adamw_math(w_ref[...], gv, m_ref[...], v_ref[...])

    spec = pl.BlockSpec(block, lambda i: (i, 0) if by_rows else (0, i))
    shape = jax.ShapeDtypeStruct((rows, cols), F32)
    return pl.pallas_call(
        body, name=name, grid=(ADAM_PARTS,), in_specs=[spec] * 4, out_specs=[spec] * 4, out_shape=[shape] * 4,
        compiler_params=_params(dimension_semantics=("arbitrary",)),
    )(w, g, m, v)


def _adamw_small(ws, gs, ms, vs):
    n = len(ws)

    def body(*refs):
        for i in range(n):
            w_ref, g_ref, m_ref, v_ref = (refs[k * n + i] for k in range(4))
            d_ref, mo_ref, vo_ref = (refs[(4 + k) * n + i] for k in range(3))
            d_ref[...], mo_ref[...], vo_ref[...] = _adamw_math(w_ref[...], g_ref[...], m_ref[...], v_ref[...])

    shapes = [jax.ShapeDtypeStruct(w.shape, F32) for w in ws]
    out = pl.pallas_call(body, name="adamw_small", out_shape=shapes * 3, compiler_params=_params())(*ws, *gs, *ms, *vs)
    return out[:n], out[n:2 * n], out[2 * n:]


MATRICES = ["ffn1_w13", "ffn1_w2", "w_in", "w_out", "ffn2_w13", "ffn2_w2"]
VECTORS = ["ffn1_norm", "mix_norm", "conv_b", "conv_ln_g", "conv_ln_b", "forget_b", "out_norm_conv",
           "out_norm_attn", "ffn2_norm", "final_norm"]
WEIGHTS = ["ffn1_norm", "ffn1_w13", "ffn1_w2", "mix_norm", "w_in", "conv_w", "conv_b", "conv_ln_g", "conv_ln_b",
           "forget_b", "out_norm_conv", "out_norm_attn", "w_out", "ffn2_norm", "ffn2_w13", "ffn2_w2", "final_norm"]


def _pack_small(g, names):
    rows, layout = [], []
    for n in names:
        flat = g[n].reshape(-1)
        pad = (-flat.shape[0]) % LANES
        rows.append(jnp.pad(flat, (0, pad)).reshape(-1, LANES))
        layout.append((n, g[n].shape, flat.shape[0], rows[-1].shape[0]))
    packed = jnp.concatenate(rows, axis=0)
    pad_rows = (-packed.shape[0]) % 8
    return jnp.pad(packed, ((0, pad_rows), (0, 0))), layout


def _unpack_small(packed, layout):
    out, r = {}, 0
    for n, shape, size, nrows in layout:
        out[n] = packed[r:r + nrows].reshape(-1)[:size].reshape(shape)
        r += nrows
    return out


def kernel(x, ffn1_norm, ffn1_w13, ffn1_w2, mix_norm, w_in, conv_w, conv_b, conv_ln_g, conv_ln_b, forget_b, out_norm_conv, out_norm_attn, w_out, ffn2_norm, ffn2_w13, ffn2_w2, final_norm, loss_target, m_ffn1_norm, m_ffn1_w13, m_ffn1_w2, m_mix_norm, m_w_in, m_conv_w, m_conv_b, m_conv_ln_g, m_conv_ln_b, m_forget_b, m_out_norm_conv, m_out_norm_attn, m_w_out, m_ffn2_norm, m_ffn2_w13, m_ffn2_w2, m_final_norm, v_ffn1_norm, v_ffn1_w13, v_ffn1_w2, v_mix_norm, v_w_in, v_conv_w, v_conv_b, v_conv_ln_g, v_conv_ln_b, v_forget_b, v_out_norm_conv, v_out_norm_attn, v_w_out, v_ffn2_norm, v_ffn2_w13, v_ffn2_w2, v_final_norm):
    args = dict(locals())
    weights = {n: args[n] for n in WEIGHTS}
    core = lax.axis_index("c").astype(jnp.int32).reshape(1)
    chip = (2 * lax.axis_index("x") + lax.axis_index("y")).astype(jnp.int32)
    chip1 = chip.reshape(1)
    chip_core = jnp.concatenate([chip1, core])

    def held(n, a):
        return a[0].T if n == "w_in" else a[0]

    def given(n, a):
        return (a.T if n == "w_in" else a)[None]

    def slot(n, deps=()):
        if n == "conv_w":
            rows = jnp.pad(conv_w[0], ((0, CONV_PAD - CONV_WIDTH), (0, 0)))
            return _into_slot(rows, chip1, F32, "slot_conv_w", deps)
        return _into_slot(held(n, weights[n]), chip1, BF16, "slot_" + n, deps)

    fetched = {"ffn1_w13": ["ffn1_w13"], "ffn1_w2": ["ffn1_w2"], "mix": ["w_in", "w_out", "conv_w"],
               "ffn2": ["ffn2_w13", "ffn2_w2"]}
    fetch = {}

    def as_weights(group, bufs):
        out = {}
        for n, b in zip(fetched[group], bufs):
            if n.endswith("w13"):
                out[n] = b
            elif n != "conv_w":
                out[n] = b.reshape(N_CHIPS * b.shape[1], b.shape[2])
            else:
                out[n] = b[:, :CONV_WIDTH].transpose(1, 0, 2).reshape(CONV_WIDTH, D_CONV)
        return out

    def get_weights(group, after):
        if group == "ffn1_w13":
            first = [slot("ffn1_w13")]
            plan = _ici_gather_plan(first)
            started = _split_copy_start("gather_ffn1_w13_start", first, 3, plan)
            second = [slot("ffn1_w2", [started[3]])]
            plan2 = _ici_gather_plan(second)
            fetch["ffn1_w2"] = plan2, _split_copy_start("gather_ffn1_w2_start", second, 3, plan2)
            later_names = fetched["mix"] + fetched["ffn2"]
            later = [slot(n, [fetch["ffn1_w2"][1][3]]) for n in later_names]
            landed = _split_copy_wait("gather_ffn1_w13_wait", started, plan, [], passed=later)
            bufs = _forward_halves(landed[:1], "forward_ffn1_w13")
            behind = dict(zip(later_names, landed[1:]))
            for later in ("mix", "ffn2"):
                bufs_later = [behind[n] for n in fetched[later]]
                plan = _ici_gather_plan(bufs_later)
                fetch[later] = plan, _split_copy_start("gather_%s_start" % later, bufs_later, 3 * len(bufs_later), plan)
            return as_weights(group, bufs), [fetch["mix"][1][3], fetch["ffn2"][1][3]]
        plan, started = fetch[group.split(":")[0]]
        if group == "ffn2:landed":
            landed = _split_copy_wait("gather_ffn2_wait", started, plan, [after])
            plan = _d2d_forward_plan(landed)
            fetch["ffn2"] = plan, _split_copy_start("forward_ffn2_start", landed, 3 * len(landed), plan)
            return {}, [fetch["ffn2"][1][3]]
        if group == "ffn2":
            return as_weights(group, _split_copy_wait("forward_ffn2_wait", started, plan, [after])), []
        landed = _split_copy_wait("gather_%s_wait" % group, started, plan, [after])
        return as_weights(group, _forward_halves(landed, "forward_" + group)), []

    def shard_major(n, g):
        return g if n.endswith("w13") else g.reshape(N_CHIPS, g.shape[0] // N_CHIPS, g.shape[1])

    exchange, scatter = {}, {}
    small_names = VECTORS + ["conv_w"]
    small = {}

    def put_grads(group, grads):
        if group == "small":
            packed, layout = _pack_small(grads, small_names + ["loss"])
            me = (4 * lax.axis_index("x") + 2 * lax.axis_index("y") + lax.axis_index("c")).astype(jnp.int32).reshape(1)
            plan = _small_plan()
            exchange[group] = layout, plan, _split_copy_start("small_start", [_small_slots(packed, me)], len(FLIPS), plan)
            return [exchange[group][2][3]]
        names = list(grads)
        local = [shard_major(n, grads[n]) for n in names]
        landing = [lax.empty((N_CHIPS,) + _half_shape(*a.shape[1:]), BF16) for a in local]
        plan = _pair_exchange_plan(local)
        exchange[group] = names, plan, _split_copy_start("exchange_%s_start" % group, local + landing, len(local), plan)
        return [exchange[group][2][3]]

    def flush_grads(group, after):
        names, plan, started = exchange[group]
        done = _split_copy_wait("exchange_%s_wait" % group, started, plan, after)
        local, sib = done[:len(names)], done[len(names):]
        parts = list(_pair_add(local, sib, core, "pair_add_" + group))
        landing = [lax.empty((N_CHIPS - 1,) + q.shape[1:], BF16) for q in parts]
        plan = _ici_scatter_plan(len(parts))
        shapes = [a.shape[1:] for a in local]
        scatter[group] = names, plan, _split_copy_start("scatter_%s_start" % group, parts + landing, 3 * len(parts), plan), shapes
        return [scatter[group][2][3]]

    p = {n: weights[n] for n in VECTORS}
    p["final_norm"] = final_norm.reshape(1, D_MODEL)
    dx = _local_step(x[0], loss_target[0], p, get_weights, put_grads, flush_grads)
    layout, plan, started = exchange["small"]
    slots, = _split_copy_wait("small_wait", started, plan, [exchange["ffn1_w13"][2][3]])
    small.update(_unpack_small(_small_sum(slots), layout))
    loss = small["loss"].reshape(())

    grad = {n: small[n] for n in VECTORS}
    grad["final_norm"] = small["final_norm"].reshape(D_MODEL)
    grad["conv_w"] = lax.dynamic_slice_in_dim(small["conv_w"], chip * (D_CONV // N_CHIPS), D_CONV // N_CHIPS, axis=1)[None]

    delta, new_m, new_v = {}, {}, {}

    def reduce_chips(group, after):
        names, plan, started, shapes = scatter[group]
        done = _split_copy_wait("scatter_%s_wait" % group, started, plan, after)
        parts, landed = done[:len(names)], done[len(names):]
        return list(_chip_add(parts, landed, chip_core, shapes, "chip_add_" + group))

    def update(group, full):
        ends = []
        for n, reduced in zip(scatter[group][0], full):
            go, d, mo, vo = _adamw_matrix(held(n, weights[n]), reduced, held(n, args["m_" + n]), held(n, args["v_" + n]),
                                          "adamw_" + n)
            grad[n], delta[n], new_m[n], new_v[n] = given(n, go), given(n, d), given(n, mo), given(n, vo)
            ends.append(vo)
        return ends

    def share_start(group, halves):
        plan = _pair_share_plan(scatter[group][3])
        return plan, _split_copy_start("share_%s_start" % group, halves, len(halves), plan)

    halves_ffn2 = reduce_chips("ffn2", [exchange["ffn1_w13"][2][3]])
    plan_ffn2, share_ffn2 = share_start("ffn2", halves_ffn2)
    last_scatter = flush_grads("ffn1_w13", [share_ffn2[3]])
    halves_mix = reduce_chips("mix", last_scatter)
    plan_mix, share_mix = share_start("mix", halves_mix)
    done_ffn2 = update("ffn2", _split_copy_wait("share_ffn2_wait", share_ffn2, plan_ffn2, [share_mix[3]]))
    done_mix = update("mix", _split_copy_wait("share_mix_wait", share_mix, plan_mix, done_ffn2))
    as2d = lambda a: a.reshape(-1, a.shape[-1])
    ds, mos, vos = _adamw_small([as2d(weights[n]) for n in small_names], [as2d(grad[n]) for n in small_names],
                                [as2d(args["m_" + n]) for n in small_names], [as2d(args["v_" + n]) for n in small_names])
    for n, d, mo, vo in zip(small_names, ds, mos, vos):
        shape = weights[n].shape
        delta[n], new_m[n], new_v[n] = d.reshape(shape), mo.reshape(shape), vo.reshape(shape)
    behind = done_ffn2 + done_mix + [vos[0]]
    halves_w2 = reduce_chips("ffn1_w2", behind)
    halves_w13 = reduce_chips("ffn1_w13", behind)
    full_w2, full_w13 = _pair_share(halves_w2 + halves_w13, "pair_share_ffn1")
    update("ffn1_w2", [full_w2])
    update("ffn1_w13", [full_w13])

    return (loss, dx[None], *[grad[n] for n in WEIGHTS], *[delta[n] for n in WEIGHTS],
            *[new_m[n] for n in WEIGHTS], *[new_v[n] for n in WEIGHTS])
```

```python
import jax
import jax.numpy as jnp
from jax import lax
from jax.experimental import pallas as pl
from jax.experimental.pallas import tpu as pltpu

F32 = jnp.float32
BF16 = jnp.bfloat16

D_MODEL = 1024
D_FF = 2816
FF_SHARD = D_FF // 2
D_CONV = 512
D_ATTN = 512
N_HEADS = 8
HEAD_DIM = 64
CONV_WIDTH = 31
CONV_PAD = 32
N_IN = 2 * D_CONV + 3 * D_ATTN + N_HEADS
EPS = 1e-6
N_CHIPS = 4
LANES = 128
TOKEN_ROWS = 512
HEAD_ROWS = 16

ADAM_LR = 0.001
ADAM_B1 = 0.9
ADAM_B2 = 0.999
ADAM_EPS = 1e-08
ADAM_WD = 0.01
ADAM_STEP = 10

VMEM_LIMIT = 56 * 1024 * 1024

_NT = (((1,), (1,)), ((), ()))
_TN = (((0,), (0,)), ((), ()))


def _dot(a, b):
    return jnp.dot(a, b, preferred_element_type=F32)


def _dot_nt(a, b):
    return lax.dot_general(a, b, _NT, preferred_element_type=F32)


def _dot_tn(a, b):
    return lax.dot_general(a, b, _TN, preferred_element_type=F32)


def _params(**kw):
    return pltpu.CompilerParams(vmem_limit_bytes=VMEM_LIMIT, **kw)


def _sigmoid(x):
    return 1.0 / (1.0 + jnp.exp(-x))


def _rms_stats(x):
    return lax.rsqrt(jnp.mean(x * x, axis=-1, keepdims=True) + EPS)


def _rms_bwd(x, r, g, dh):
    t = dh * g
    dx = r * t - x * (r * r * r) * jnp.mean(t * x, axis=-1, keepdims=True)
    return dx, dh * x * r


def _silu_grad(z, sg):
    return sg * (1.0 + z * (1.0 - sg))


def _row_spec(tm, n):
    return pl.BlockSpec((tm, n), lambda i: (i, 0))


def _full_spec(shape):
    nd = len(shape)
    return pl.BlockSpec(shape, lambda i: (0,) * nd)


_ANY = pl.BlockSpec(memory_space=pl.ANY)


def _skip(n, body):
    return lambda *refs: body(*refs[n:])


FFN_ROWS = 256
FFN_WEIGHT_PARTS = N_CHIPS + 2


def _with_ffn_weights(w13_hbm, w2_hbm, w13_ref, w2_ref, sems, order, tile):
    first = pl.program_id(0) == 0
    copies = {}
    if w13_hbm is not None:
        for k in range(N_CHIPS):
            copies["w13", k] = pltpu.make_async_copy(w13_hbm.at[k], w13_ref.at[k], sems.at[k])
    if w2_hbm is not None:
        for half in range(2):
            rows = pl.ds(half * FF_SHARD, FF_SHARD)
            copies["w2", half] = pltpu.make_async_copy(w2_hbm.at[rows, :], w2_ref.at[rows, :], sems.at[N_CHIPS + half])

    @pl.when(first)
    def _():
        for part in order:
            copies[part].start()

        def ready(*parts):
            for part in parts:
                copies[part].wait()

        tile(ready)

    @pl.when(jnp.logical_not(first))
    def _():
        tile(lambda *parts: None)


def _ffn_fwd(x, g, w13s, w2, name, deps=()):
    t = x.shape[0]
    tm = FFN_ROWS
    deps = tuple(deps)

    def body(x_ref, g_ref, w13_hbm, w2_hbm, xo_ref, h_ref, gu_ref, a_ref, w13_ref, w2_ref, sems):
        def tile(ready):
            xv = x_ref[...]
            hb = (xv * _rms_stats(xv) * g_ref[...]).astype(BF16)
            h_ref[...] = hb
            acc = jnp.zeros((tm, D_MODEL), F32)
            for half in range(2):
                lo = half * FF_SHARD
                ready(("w13", half), ("w13", 2 + half))
                gate = _dot(hb, w13_ref[half])
                up = _dot(hb, w13_ref[2 + half])
                gu_ref[:, lo:lo + FF_SHARD] = gate.astype(BF16)
                gu_ref[:, D_FF + lo:D_FF + lo + FF_SHARD] = up.astype(BF16)
                a = (gate * _sigmoid(gate) * up).astype(BF16)
                a_ref[:, lo:lo + FF_SHARD] = a
                ready(("w2", half))
                acc = acc + _dot(a, w2_ref[lo:lo + FF_SHARD, :])
            xo_ref[...] = xv + 0.5 * acc

        _with_ffn_weights(w13_hbm, w2_hbm, w13_ref, w2_ref, sems,
                          [("w13", 0), ("w13", 2), ("w2", 0), ("w13", 1), ("w13", 3), ("w2", 1)], tile)

    return pl.pallas_call(
        _skip(len(deps), body), name=name, grid=(t // tm,),
        in_specs=[_ANY] * len(deps) + [_row_spec(tm, D_MODEL), _full_spec((1, D_MODEL)), _ANY, _ANY],
        out_specs=[_row_spec(tm, D_MODEL), _row_spec(tm, D_MODEL), _row_spec(tm, 2 * D_FF), _row_spec(tm, D_FF)],
        out_shape=[jax.ShapeDtypeStruct((t, D_MODEL), F32), jax.ShapeDtypeStruct((t, D_MODEL), BF16),
                   jax.ShapeDtypeStruct((t, 2 * D_FF), BF16), jax.ShapeDtypeStruct((t, D_FF), BF16)],
        scratch_shapes=[pltpu.VMEM(w13s.shape, BF16), pltpu.VMEM(w2.shape, BF16),
                        pltpu.SemaphoreType.DMA((FFN_WEIGHT_PARTS,))],
        compiler_params=_params(dimension_semantics=("arbitrary",)),
    )(*deps, x, g, w13s, w2)


def _ffn_up(x, g, w13s, name, deps=()):
    t = x.shape[0]
    tm = FFN_ROWS
    deps = tuple(deps)

    def body(x_ref, g_ref, w13_hbm, h_ref, gu_ref, a_ref, w13_ref, sems):
        def tile(ready):
            xv = x_ref[...]
            hb = (xv * _rms_stats(xv) * g_ref[...]).astype(BF16)
            h_ref[...] = hb
            for half in range(2):
                lo = half * FF_SHARD
                ready(("w13", half), ("w13", 2 + half))
                gate = _dot(hb, w13_ref[half])
                up = _dot(hb, w13_ref[2 + half])
                gu_ref[:, lo:lo + FF_SHARD] = gate.astype(BF16)
                gu_ref[:, D_FF + lo:D_FF + lo + FF_SHARD] = up.astype(BF16)
                a_ref[:, lo:lo + FF_SHARD] = (gate * _sigmoid(gate) * up).astype(BF16)

        _with_ffn_weights(w13_hbm, None, w13_ref, None, sems, [("w13", 0), ("w13", 2), ("w13", 1), ("w13", 3)], tile)

    return pl.pallas_call(
        _skip(len(deps), body), name=name, grid=(t // tm,),
        in_specs=[_ANY] * len(deps) + [_row_spec(tm, D_MODEL), _full_spec((1, D_MODEL)), _ANY],
        out_specs=[_row_spec(tm, D_MODEL), _row_spec(tm, 2 * D_FF), _row_spec(tm, D_FF)],
        out_shape=[jax.ShapeDtypeStruct((t, D_MODEL), BF16), jax.ShapeDtypeStruct((t, 2 * D_FF), BF16),
                   jax.ShapeDtypeStruct((t, D_FF), BF16)],
        scratch_shapes=[pltpu.VMEM(w13s.shape, BF16), pltpu.SemaphoreType.DMA((FFN_WEIGHT_PARTS,))],
        compiler_params=_params(dimension_semantics=("arbitrary",)),
    )(*deps, x, g, w13s)


def _ffn_down(x, a, w2, name):
    t = x.shape[0]
    tm = FFN_ROWS

    def body(x_ref, a_ref, w2_hbm, xo_ref, w2_ref, sems):
        def tile(ready):
            ready(("w2", 0))
            acc = _dot(a_ref[:, 0:FF_SHARD], w2_ref[0:FF_SHARD, :])
            ready(("w2", 1))
            acc = acc + _dot(a_ref[:, FF_SHARD:], w2_ref[FF_SHARD:, :])
            xo_ref[...] = x_ref[...] + 0.5 * acc

        _with_ffn_weights(None, w2_hbm, None, w2_ref, sems, [("w2", 0), ("w2", 1)], tile)

    return pl.pallas_call(
        body, name=name, grid=(t // tm,),
        in_specs=[_row_spec(tm, D_MODEL), _row_spec(tm, D_FF), _ANY],
        out_specs=_row_spec(tm, D_MODEL), out_shape=jax.ShapeDtypeStruct((t, D_MODEL), F32),
        scratch_shapes=[pltpu.VMEM(w2.shape, BF16), pltpu.SemaphoreType.DMA((FFN_WEIGHT_PARTS,))],
        compiler_params=_params(dimension_semantics=("arbitrary",)),
    )(x, a, w2)


def _ffn_bwd(dy, x, gu, g, w13s, w2, name, deps=()):
    t = x.shape[0]
    tm = FFN_ROWS
    deps = tuple(deps)

    def body(dy_ref, x_ref, gu_ref, g_ref, w13_hbm, w2_hbm, dx_ref, dgu_ref, dg_ref, dyh_ref, dxb_ref,
             w13_ref, w2_ref, sems):
        @pl.when(pl.program_id(0) == 0)
        def _():
            dg_ref[...] = jnp.zeros_like(dg_ref)

        def tile(ready):
            dyv = dy_ref[...]
            dyh = (0.5 * dyv).astype(BF16)
            dyh_ref[...] = dyh
            dh = jnp.zeros((tm, D_MODEL), F32)
            for half in range(2):
                lo = half * FF_SHARD
                ready(("w2", half))
                da = _dot_nt(dyh, w2_ref[lo:lo + FF_SHARD, :])
                gate = gu_ref[:, lo:lo + FF_SHARD].astype(F32)
                up = gu_ref[:, D_FF + lo:D_FF + lo + FF_SHARD].astype(F32)
                sg = _sigmoid(gate)
                act = gate * sg
                dgate = (da * up * _silu_grad(gate, sg)).astype(BF16)
                dup = (da * act).astype(BF16)
                dgu_ref[:, lo:lo + FF_SHARD] = dgate
                dgu_ref[:, D_FF + lo:D_FF + lo + FF_SHARD] = dup
                ready(("w13", half), ("w13", 2 + half))
                dh = dh + _dot_nt(dgate, w13_ref[half]) + _dot_nt(dup, w13_ref[2 + half])
            xv = x_ref[...]
            dxn, dg_rows = _rms_bwd(xv, _rms_stats(xv), g_ref[...], dh)
            dx = dyv + dxn
            dx_ref[...] = dx
            dxb_ref[...] = dx.astype(BF16)
            dg_ref[...] += jnp.sum(dg_rows, axis=0, keepdims=True)

        _with_ffn_weights(w13_hbm, w2_hbm, w13_ref, w2_ref, sems,
                          [("w2", 0), ("w13", 0), ("w13", 2), ("w2", 1), ("w13", 1), ("w13", 3)], tile)

    return pl.pallas_call(
        _skip(len(deps), body), name=name, grid=(t // tm,),
        in_specs=[_ANY] * len(deps) + [_row_spec(tm, D_MODEL), _row_spec(tm, D_MODEL), _row_spec(tm, 2 * D_FF),
                                       _full_spec((1, D_MODEL)), _ANY, _ANY],
        out_specs=[_row_spec(tm, D_MODEL), _row_spec(tm, 2 * D_FF),
                   _full_spec((1, D_MODEL)), _row_spec(tm, D_MODEL), _row_spec(tm, D_MODEL)],
        out_shape=[jax.ShapeDtypeStruct((t, D_MODEL), F32), jax.ShapeDtypeStruct((t, 2 * D_FF), BF16),
                   jax.ShapeDtypeStruct((1, D_MODEL), F32),
                   jax.ShapeDtypeStruct((t, D_MODEL), BF16), jax.ShapeDtypeStruct((t, D_MODEL), BF16)],
        scratch_shapes=[pltpu.VMEM(w13s.shape, BF16), pltpu.VMEM(w2.shape, BF16),
                        pltpu.SemaphoreType.DMA((FFN_WEIGHT_PARTS,))],
        compiler_params=_params(dimension_semantics=("arbitrary",)),
    )(*deps, dy, x, gu, g, w13s, w2)


WGRAD_ROWS = (1408, 1024, 896, 512, 256)


def _wgrad(a, b, n_blocks, name, deps=()):
    t, m = a.shape
    tm = next(rows for rows in WGRAD_ROWS if m % rows == 0)
    n = b.shape[1]
    bn = n // n_blocks
    deps = tuple(deps)
    assert a.dtype == BF16 and b.dtype == BF16

    def body(a_ref, b_ref, o_ref):
        o_ref[0] = _dot_tn(a_ref[...], b_ref[...]).astype(BF16)

    return pl.pallas_call(
        _skip(len(deps), body), name=name, grid=(n_blocks, m // tm),
        in_specs=[_ANY] * len(deps) + [pl.BlockSpec((t, tm), lambda j, i: (0, i)),
                                       pl.BlockSpec((t, bn), lambda j, i: (0, j))],
        out_specs=pl.BlockSpec((1, tm, bn), lambda j, i: (j, i, 0)),
        out_shape=jax.ShapeDtypeStruct((n_blocks, m, bn), BF16),
        compiler_params=_params(dimension_semantics=("arbitrary", "arbitrary")),
    )(*deps, a, b)


def _mix_proj(x, g, w_ag, w_qkv, w_f):
    t = x.shape[0]
    tm = TOKEN_ROWS

    def body(x_ref, g_ref, wag_ref, wqkv_ref, wf_ref, h_ref, ag_ref, qkv_ref, fl_ref):
        xv = x_ref[...]
        hb = (xv * _rms_stats(xv) * g_ref[...]).astype(BF16)
        h_ref[...] = hb
        ag_ref[...] = _dot_nt(hb, wag_ref[...])
        qkv_ref[...] = _dot_nt(hb, wqkv_ref[...]).astype(BF16)
        fl_ref[...] = _dot_nt(hb, wf_ref[...])

    return pl.pallas_call(
        body, name="mix_proj", grid=(t // tm,),
        in_specs=[_row_spec(tm, D_MODEL), _full_spec((1, D_MODEL)), _full_spec(w_ag.shape),
                  _full_spec(w_qkv.shape), _full_spec(w_f.shape)],
        out_specs=[_row_spec(tm, D_MODEL), _row_spec(tm, 2 * D_CONV), _row_spec(tm, 3 * D_ATTN),
                   _row_spec(tm, LANES)],
        out_shape=[jax.ShapeDtypeStruct((t, D_MODEL), BF16), jax.ShapeDtypeStruct((t, 2 * D_CONV), F32),
                   jax.ShapeDtypeStruct((t, 3 * D_ATTN), BF16), jax.ShapeDtypeStruct((t, LANES), F32)],
        compiler_params=_params(dimension_semantics=("arbitrary",)),
    )(x, g, w_ag, w_qkv, w_f)


def _mix_proj_bwd(dproj, dx2, x1, g, w_ag, w_qkv, w_f):
    t = x1.shape[0]
    tm = TOKEN_ROWS
    n_ag, n_qkv = 2 * D_CONV, 3 * D_ATTN

    def body(dp_ref, dx2_ref, x_ref, g_ref, wag_ref, wqkv_ref, wf_ref, dx_ref, dg_ref):
        @pl.when(pl.program_id(0) == 0)
        def _():
            dg_ref[...] = jnp.zeros_like(dg_ref)

        dh = (_dot(dp_ref[:, 0:n_ag], wag_ref[...]) + _dot(dp_ref[:, n_ag:n_ag + n_qkv], wqkv_ref[...])
              + _dot(dp_ref[:, n_ag + n_qkv:], wf_ref[...]))
        xv = x_ref[...]
        dxn, dg_rows = _rms_bwd(xv, _rms_stats(xv), g_ref[...], dh)
        dx_ref[...] = dx2_ref[...] + dxn
        dg_ref[...] += jnp.sum(dg_rows, axis=0, keepdims=True)

    return pl.pallas_call(
        body, name="mix_proj_bwd", grid=(t // tm,),
        in_specs=[_row_spec(tm, dproj.shape[1]),
                  _row_spec(tm, D_MODEL), _row_spec(tm, D_MODEL), _full_spec((1, D_MODEL)),
                  _full_spec(w_ag.shape), _full_spec(w_qkv.shape), _full_spec(w_f.shape)],
        out_specs=[_row_spec(tm, D_MODEL), _full_spec((1, D_MODEL))],
        out_shape=[jax.ShapeDtypeStruct((t, D_MODEL), F32), jax.ShapeDtypeStruct((1, D_MODEL), F32)],
        compiler_params=_params(dimension_semantics=("arbitrary",)),
    )(dproj, dx2, x1, g, w_ag, w_qkv, w_f)


def _split3(x):
    hi = x.astype(BF16)
    r1 = x - hi.astype(F32)
    mid = r1.astype(BF16)
    lo = (r1 - mid.astype(F32)).astype(BF16)
    return hi, mid, lo


def _gates_fwd(flt, fb):
    t = flt.shape[1]

    def body(f_ref, b_ref, d_ref):
        z = f_ref[...] + b_ref[...]
        logf = jnp.minimum(z, 0.0) - jnp.log(1.0 + jnp.exp(-jnp.abs(z)))
        row = lax.broadcasted_iota(jnp.int32, (LANES, LANES), 0)
        col = lax.broadcasted_iota(jnp.int32, (LANES, LANES), 1)
        upper = (row <= col).astype(BF16)
        carry = jnp.zeros((HEAD_ROWS, 1), F32)
        for blk in range(t // LANES):
            hi, mid, lo = _split3(logf[:, blk * LANES:(blk + 1) * LANES])
            cs = _dot(hi, upper) + _dot(mid, upper) + _dot(lo, upper)
            d_ref[:, blk * LANES:(blk + 1) * LANES] = cs + carry
            carry = carry + cs[:, LANES - 1:LANES]

    return pl.pallas_call(
        body, name="gates_fwd", out_shape=jax.ShapeDtypeStruct((HEAD_ROWS, t), F32),
        compiler_params=_params(),
    )(flt, fb)


def _gates_bwd(dd, flt, fb):
    t = flt.shape[1]

    def body(dd_ref, f_ref, b_ref, df_ref, db_ref):
        z = f_ref[...] + b_ref[...]
        row = lax.broadcasted_iota(jnp.int32, (LANES, LANES), 0)
        col = lax.broadcasted_iota(jnp.int32, (LANES, LANES), 1)
        lower = (row >= col).astype(BF16)
        carry = jnp.zeros((HEAD_ROWS, 1), F32)
        db = jnp.zeros((HEAD_ROWS, 1), F32)
        for blk in reversed(range(t // LANES)):
            sl = slice(blk * LANES, (blk + 1) * LANES)
            hi, mid, lo = _split3(dd_ref[:, sl])
            cs = _dot(hi, lower) + _dot(mid, lower) + _dot(lo, lower)
            dz = (cs + carry) * _sigmoid(-z[:, sl])
            df_ref[:, sl] = dz
            db = db + jnp.sum(dz, axis=1, keepdims=True)
            carry = carry + cs[:, 0:1]
        db_ref[...] = db

    return pl.pallas_call(
        body, name="gates_bwd",
        out_shape=[jax.ShapeDtypeStruct((HEAD_ROWS, t), F32), jax.ShapeDtypeStruct((HEAD_ROWS, 1), F32)],
        compiler_params=_params(),
    )(dd, flt, fb)


CONV_CHUNK = 128
CONV_TAIL = 16
CONV_WINDOW = CONV_CHUNK + CONV_PAD + 8
CONV_ROWS_EXTRA = CONV_PAD + CONV_TAIL
SUBLANES = 8


def _conv_rows(ag_ref, u_ref, t):
    u_ref[0:CONV_PAD, :] = jnp.zeros((CONV_PAD, D_CONV), F32)
    u_ref[CONV_PAD + t:CONV_ROWS_EXTRA + t, :] = jnp.zeros((CONV_TAIL, D_CONV), F32)

    def fill(i, c):
        r0 = pl.multiple_of(i * CONV_CHUNK, CONV_CHUNK)
        a = ag_ref[pl.ds(r0, CONV_CHUNK), 0:D_CONV]
        gt = ag_ref[pl.ds(r0, CONV_CHUNK), D_CONV:2 * D_CONV]
        u_ref[pl.ds(CONV_PAD + r0, CONV_CHUNK), :] = a * _sigmoid(gt)
        return c

    lax.fori_loop(0, t // CONV_CHUNK, fill, 0)


def _for_shifted(ref, r0, offsets, fn):
    window = ref[pl.ds(r0, CONV_WINDOW), :]
    for rem in range(SUBLANES):
        mine = [o for o in offsets if o % SUBLANES == rem]
        if not mine:
            continue
        turned = window if rem == 0 else pltpu.roll(window, CONV_WINDOW - rem, 0)
        for o in mine:
            fn(o, turned[o - rem:o - rem + CONV_CHUNK])


def _conv_taps(u_ref, r0, w_ref, cb):
    acc = [jnp.zeros((CONV_CHUNK, D_CONV), F32)]

    def tap(o, rows):
        j = o - (CONV_PAD - CONV_WIDTH + 1)
        acc[0] = acc[0] + w_ref[j:j + 1, :] * rows

    _for_shifted(u_ref, r0, [j + CONV_PAD - CONV_WIDTH + 1 for j in range(CONV_WIDTH)], tap)
    return acc[0] + cb


def _conv_point(y, lg, lb):
    mu = jnp.mean(y, axis=-1, keepdims=True)
    yc = y - mu
    rstd = lax.rsqrt(jnp.mean(yc * yc, axis=-1, keepdims=True) + EPS)
    yhat = yc * rstd
    z = yhat * lg + lb
    sg = _sigmoid(z)
    s = z * sg
    rr = _rms_stats(s)
    return yhat, rstd, z, sg, s, rr


def _conv_fwd(ag, conv_w, conv_b, ln_g, ln_b, norm_g):
    t = ag.shape[0]

    def body(ag_ref, w_ref, cb_ref, lg_ref, lb_ref, ng_ref, o_ref, y_ref, u_ref):
        _conv_rows(ag_ref, u_ref, t)
        cb, lg, lb, ng = cb_ref[...], lg_ref[...], lb_ref[...], ng_ref[...]

        def chunk(i, c):
            r0 = pl.multiple_of(i * CONV_CHUNK, CONV_CHUNK)
            y = _conv_taps(u_ref, r0, w_ref, cb)
            y_ref[pl.ds(r0, CONV_CHUNK), :] = y
            _, _, _, _, s, rr = _conv_point(y, lg, lb)
            o_ref[pl.ds(r0, CONV_CHUNK), :] = (s * rr * ng).astype(BF16)
            return c

        lax.fori_loop(0, t // CONV_CHUNK, chunk, 0)

    return pl.pallas_call(
        body, name="conv_fwd",
        out_shape=[jax.ShapeDtypeStruct((t, D_CONV), BF16), jax.ShapeDtypeStruct((t, D_CONV), F32)],
        scratch_shapes=[pltpu.VMEM((t + CONV_ROWS_EXTRA, D_CONV), F32)],
        compiler_params=_params(),
    )(ag, conv_w, conv_b, ln_g, ln_b, norm_g)


def _conv_bwd(ag, y, dout, conv_w, ln_g, ln_b, norm_g):
    t = ag.shape[0]

    def body(ag_ref, y_ref, do_ref, w_ref, lg_ref, lb_ref, ng_ref,
             dag_ref, dw_ref, dcb_ref, dlg_ref, dlb_ref, dng_ref, u_ref, dy_ref):
        _conv_rows(ag_ref, u_ref, t)
        dy_ref[t:t + CONV_ROWS_EXTRA, :] = jnp.zeros((CONV_ROWS_EXTRA, D_CONV), F32)
        lg, lb, ng = lg_ref[...], lb_ref[...], ng_ref[...]
        dw_ref[...] = jnp.zeros_like(dw_ref)
        zero = jnp.zeros((1, D_CONV), F32)

        def chunk(i, carry):
            dcb, dlg, dlb, dng = carry
            r0 = pl.multiple_of(i * CONV_CHUNK, CONV_CHUNK)
            yhat, rstd, z, sg, s, rr = _conv_point(y_ref[pl.ds(r0, CONV_CHUNK), :], lg, lb)
            do = do_ref[pl.ds(r0, CONV_CHUNK), :]
            ds, dng_rows = _rms_bwd(s, rr, ng, do)
            dz = ds * _silu_grad(z, sg)
            dyhat = dz * lg
            dy = rstd * (dyhat - jnp.mean(dyhat, axis=-1, keepdims=True)
                         - yhat * jnp.mean(dyhat * yhat, axis=-1, keepdims=True))
            dy_ref[pl.ds(r0, CONV_CHUNK), :] = dy
            def tap(o, rows):
                j = o - (CONV_PAD - CONV_WIDTH + 1)
                dw_ref[j:j + 1, :] += jnp.sum(dy * rows, axis=0, keepdims=True)

            _for_shifted(u_ref, r0, [j + CONV_PAD - CONV_WIDTH + 1 for j in range(CONV_WIDTH)], tap)
            return (dcb + jnp.sum(dy, axis=0, keepdims=True), dlg + jnp.sum(dz * yhat, axis=0, keepdims=True),
                    dlb + jnp.sum(dz, axis=0, keepdims=True), dng + jnp.sum(dng_rows, axis=0, keepdims=True))

        dcb, dlg, dlb, dng = lax.fori_loop(0, t // CONV_CHUNK, chunk, (zero, zero, zero, zero))
        dcb_ref[...] = dcb
        dlg_ref[...] = dlg
        dlb_ref[...] = dlb
        dng_ref[...] = dng

        def chunk2(i, c):
            r0 = pl.multiple_of(i * CONV_CHUNK, CONV_CHUNK)
            acc = [jnp.zeros((CONV_CHUNK, D_CONV), F32)]

            def tap(o, rows):
                j = CONV_WIDTH - 1 - o
                acc[0] = acc[0] + w_ref[j:j + 1, :] * rows

            _for_shifted(dy_ref, r0, list(range(CONV_WIDTH)), tap)
            du = acc[0]
            a = ag_ref[pl.ds(r0, CONV_CHUNK), 0:D_CONV]
            gt = ag_ref[pl.ds(r0, CONV_CHUNK), D_CONV:2 * D_CONV]
            sg = _sigmoid(gt)
            dag_ref[pl.ds(r0, CONV_CHUNK), 0:D_CONV] = (du * sg).astype(BF16)
            dag_ref[pl.ds(r0, CONV_CHUNK), D_CONV:2 * D_CONV] = (du * a * sg * (1.0 - sg)).astype(BF16)
            return c

        lax.fori_loop(0, t // CONV_CHUNK, chunk2, 0)

    vec = jax.ShapeDtypeStruct((1, D_CONV), F32)
    return pl.pallas_call(
        body, name="conv_bwd",
        out_shape=[jax.ShapeDtypeStruct((t, 2 * D_CONV), BF16), jax.ShapeDtypeStruct((CONV_PAD, D_CONV), F32),
                   vec, vec, vec, vec],
        scratch_shapes=[pltpu.VMEM((t + CONV_ROWS_EXTRA, D_CONV), F32), pltpu.VMEM((t + CONV_ROWS_EXTRA, D_CONV), F32)],
        compiler_params=_params(),
    )(ag, y, dout, conv_w, ln_g, ln_b, norm_g)


Q_ROWS = 256
ATTN_SCALE = HEAD_DIM ** -0.5
ATTN_AHEAD = 1


def _attn_specs(t):
    blk = lambda off: pl.BlockSpec((t, LANES), lambda p: (0, off + p))
    pairs = N_HEADS // 2
    return [blk(0), blk(pairs), blk(2 * pairs), pl.BlockSpec((2, 1, t), lambda p: (p, 0, 0))]


def _one_head(q2, mask):
    return jnp.where(mask, q2, jnp.zeros_like(q2)) * ATTN_SCALE


def _attn_scores(qs, k2, drow, r0, q1):
    s = _dot_nt(qs, k2) - drow
    rowi = lax.broadcasted_iota(jnp.int32, (q1 - r0, q1 - r0), 0)
    coli = lax.broadcasted_iota(jnp.int32, (q1 - r0, q1 - r0), 1)
    diag = jnp.where(coli <= rowi, s[:, r0:q1], -jnp.inf)
    return diag if r0 == 0 else jnp.concatenate([s[:, :r0], diag], axis=1)


def _attn_fwd(qkv, drow, deps=()):
    t = qkv.shape[0]
    deps = tuple(deps)

    def body(q_ref, k_ref, v_ref, dr_ref, o_ref, lse_ref):
        head_a = lax.broadcasted_iota(jnp.int32, (1, LANES), 1) < HEAD_DIM
        items = [(qb, hh) for qb in range(t // Q_ROWS) for hh in range(2)]

        def scores(item):
            qb, hh = item
            r0, q1 = qb * Q_ROWS, (qb + 1) * Q_ROWS
            qs = _one_head(q_ref[r0:q1, :], head_a if hh == 0 else ~head_a)
            return _attn_scores(qs, k_ref[0:q1, :], dr_ref[hh, :, 0:q1], r0, q1)

        ahead = [scores(item) for item in items[:ATTN_AHEAD]]
        outs = []
        for n, (qb, hh) in enumerate(items):
            r0, q1 = qb * Q_ROWS, (qb + 1) * Q_ROWS
            s = ahead.pop(0)
            if n + ATTN_AHEAD < len(items):
                ahead.append(scores(items[n + ATTN_AHEAD]))
            mx = jnp.max(s, axis=1, keepdims=True)
            p = jnp.exp(s - mx)
            l = jnp.sum(p, axis=1, keepdims=True)
            lse_ref[hh, r0:q1, :] = mx + jnp.log(l)
            outs.append(_dot(p.astype(BF16), v_ref[0:q1, :]) * (1.0 / l))
            if hh == 1:
                o_ref[r0:q1, :] = jnp.where(head_a, outs[0], outs[1])
                outs = []

    pairs = N_HEADS // 2
    return pl.pallas_call(
        _skip(len(deps), body), name="attn_fwd", grid=(pairs,), in_specs=[_ANY] * len(deps) + _attn_specs(t),
        out_specs=[pl.BlockSpec((t, LANES), lambda p: (0, p)), pl.BlockSpec((2, t, 1), lambda p: (p, 0, 0))],
        out_shape=[jax.ShapeDtypeStruct((t, D_ATTN), F32), jax.ShapeDtypeStruct((N_HEADS, t, 1), F32)],
        compiler_params=_params(dimension_semantics=("arbitrary",)),
    )(*deps, qkv, qkv, qkv, drow)


def _attn_bwd(qkv, drow, lse, do):
    t = qkv.shape[0]

    def body(q_ref, k_ref, v_ref, dr_ref, lse_ref, do_ref,
             dq_ref, dk_ref, dv_ref, dd_ref, dk_acc, dv_acc):
        head_a = lax.broadcasted_iota(jnp.int32, (1, LANES), 1) < HEAD_DIM
        dk_acc[...] = jnp.zeros_like(dk_acc)
        dv_acc[...] = jnp.zeros_like(dv_acc)
        dd_ref[...] = jnp.zeros_like(dd_ref)
        items = [(qb, hh) for qb in range(t // Q_ROWS) for hh in range(2)]

        def products(item):
            qb, hh = item
            r0, q1 = qb * Q_ROWS, (qb + 1) * Q_ROWS
            mask = head_a if hh == 0 else ~head_a
            qs = _one_head(q_ref[r0:q1, :], mask)
            dob = jnp.where(mask, do_ref[r0:q1, :], 0.0).astype(BF16)
            s = _attn_scores(qs, k_ref[0:q1, :], dr_ref[hh, :, 0:q1], r0, q1)
            return qs, dob, s, _dot_nt(dob, v_ref[0:q1, :])

        ahead = products(items[0])
        dqs = []
        for n, (qb, hh) in enumerate(items):
            r0, q1 = qb * Q_ROWS, (qb + 1) * Q_ROWS
            qs, dob, s, dp = ahead
            if n + 1 < len(items):
                ahead = products(items[n + 1])
            p = jnp.exp(s - lse_ref[hh, r0:q1, :])
            ds = p * (dp - jnp.sum(p * dp, axis=1, keepdims=True))
            dsb = ds.astype(BF16)
            dqs.append(_dot(dsb, k_ref[0:q1, :]) * ATTN_SCALE)
            dk_acc[0:q1, :] += _dot_tn(dsb, qs)
            dv_acc[0:q1, :] += _dot_tn(p.astype(BF16), dob)
            dd_ref[hh, :, 0:q1] -= jnp.sum(ds, axis=0, keepdims=True)
            if hh == 1:
                dq_ref[r0:q1, :] = jnp.where(head_a, dqs[0], dqs[1]).astype(BF16)
                dqs = []
        dk_ref[...] = dk_acc[...].astype(BF16)
        dv_ref[...] = dv_acc[...].astype(BF16)

    pairs = N_HEADS // 2
    col = pl.BlockSpec((t, LANES), lambda p: (0, p))
    grad = jax.ShapeDtypeStruct((t, D_ATTN), BF16)
    return pl.pallas_call(
        body, name="attn_bwd", grid=(pairs,),
        in_specs=_attn_specs(t) + [pl.BlockSpec((2, t, 1), lambda p: (p, 0, 0)), col],
        out_specs=[col, col, col, pl.BlockSpec((2, 1, t), lambda p: (p, 0, 0))],
        out_shape=[grad, grad, grad, jax.ShapeDtypeStruct((N_HEADS, 1, t), F32)],
        scratch_shapes=[pltpu.VMEM((t, LANES), F32), pltpu.VMEM((t, LANES), F32)],
        compiler_params=_params(dimension_semantics=("arbitrary",)),
    )(qkv, qkv, qkv, drow, lse, do)


def _out_proj(ycn, o, g_attn, w_out, x1, deps=()):
    t = x1.shape[0]
    tm = TOKEN_ROWS
    deps = tuple(deps)

    def body(yc_ref, o_ref, g_ref, w_ref, x_ref, xo_ref, ya_ref):
        ov = o_ref[...]
        ya = (ov * _rms_stats(ov) * g_ref[...]).astype(BF16)
        ya_ref[...] = ya
        xo_ref[...] = x_ref[...] + _dot(yc_ref[...], w_ref[0:D_CONV, :]) + _dot(ya, w_ref[D_CONV:, :])

    return pl.pallas_call(
        _skip(len(deps), body), name="out_proj", grid=(t // tm,),
        in_specs=[_ANY] * len(deps) + [_row_spec(tm, D_CONV), _row_spec(tm, D_ATTN), _full_spec((1, D_ATTN)),
                                       _full_spec(w_out.shape), _row_spec(tm, D_MODEL)],
        out_specs=[_row_spec(tm, D_MODEL), _row_spec(tm, D_ATTN)],
        out_shape=[jax.ShapeDtypeStruct((t, D_MODEL), F32), jax.ShapeDtypeStruct((t, D_ATTN), BF16)],
        compiler_params=_params(dimension_semantics=("arbitrary",)),
    )(*deps, ycn, o, g_attn, w_out, x1)


def _out_proj_bwd(dx2, o, g_attn, w_out, deps=()):
    t = dx2.shape[0]
    tm = TOKEN_ROWS
    deps = tuple(deps)

    def body(dx_ref, o_ref, g_ref, w_ref, dyc_ref, do_ref, dg_ref):
        @pl.when(pl.program_id(0) == 0)
        def _():
            dg_ref[...] = jnp.zeros_like(dg_ref)

        dxb = dx_ref[...]
        dyc_ref[...] = _dot_nt(dxb, w_ref[0:D_CONV, :])
        dya = _dot_nt(dxb, w_ref[D_CONV:, :])
        ov = o_ref[...]
        do, dg_rows = _rms_bwd(ov, _rms_stats(ov), g_ref[...], dya)
        do_ref[...] = do
        dg_ref[...] += jnp.sum(dg_rows, axis=0, keepdims=True)

    return pl.pallas_call(
        _skip(len(deps), body), name="out_proj_bwd", grid=(t // tm,),
        in_specs=[_ANY] * len(deps) + [_row_spec(tm, D_MODEL), _row_spec(tm, D_ATTN), _full_spec((1, D_ATTN)),
                                       _full_spec(w_out.shape)],
        out_specs=[_row_spec(tm, D_CONV), _row_spec(tm, D_ATTN), _full_spec((1, D_ATTN))],
        out_shape=[jax.ShapeDtypeStruct((t, D_CONV), F32), jax.ShapeDtypeStruct((t, D_ATTN), F32),
                   jax.ShapeDtypeStruct((1, D_ATTN), F32)],
        compiler_params=_params(dimension_semantics=("arbitrary",)),
    )(*deps, dx2, o, g_attn, w_out)


def _loss_bwd(x3, target, g):
    t = x3.shape[0]
    tm = TOKEN_ROWS

    def body(x_ref, t_ref, g_ref, loss_ref, dx_ref, dg_ref):
        @pl.when(pl.program_id(0) == 0)
        def _():
            loss_ref[...] = jnp.zeros_like(loss_ref)
            dg_ref[...] = jnp.zeros_like(dg_ref)

        xv = x_ref[...]
        r = _rms_stats(xv)
        gv = g_ref[...]
        err = xv * r * gv - t_ref[...]
        row = jnp.sum(err * err, axis=1, keepdims=True) * (0.5 / D_MODEL)
        loss_ref[...] += jnp.sum(row, axis=0, keepdims=True)
        dx, dg_rows = _rms_bwd(xv, r, gv, err * (1.0 / D_MODEL))
        dx_ref[...] = dx
        dg_ref[...] += jnp.sum(dg_rows, axis=0, keepdims=True)

    return pl.pallas_call(
        body, name="loss_bwd", grid=(t // tm,),
        in_specs=[_row_spec(tm, D_MODEL), _row_spec(tm, D_MODEL), _full_spec((1, D_MODEL))],
        out_specs=[_full_spec((1, LANES)), _row_spec(tm, D_MODEL), _full_spec((1, D_MODEL))],
        out_shape=[jax.ShapeDtypeStruct((1, LANES), F32), jax.ShapeDtypeStruct((t, D_MODEL), F32),
                   jax.ShapeDtypeStruct((1, D_MODEL), F32)],
        compiler_params=_params(dimension_semantics=("arbitrary",)),
    )(x3, target, g)


def _split_w_in(w_in_t):
    w_ag = w_in_t[:2 * D_CONV]
    w_qkv = w_in_t[2 * D_CONV:2 * D_CONV + 3 * D_ATTN]
    w_f = jnp.pad(w_in_t[2 * D_CONV + 3 * D_ATTN:], ((0, LANES - N_HEADS), (0, 0)))
    return w_ag, w_qkv, w_f


def _head_rows(v):
    return jnp.pad(v, ((0, HEAD_ROWS - N_HEADS),) + ((0, 0),) * (v.ndim - 1))


def _local_step(x, target, p, get_weights, put_grads, flush_grads):
    t = x.shape[0]
    fb = _head_rows(p["forget_b"].reshape(N_HEADS, 1))

    w, deps = get_weights("ffn1_w13", None)
    h1, gu1, act1 = _ffn_up(x, p["ffn1_norm"], w["ffn1_w13"], "ffn1_up", deps)
    w2, _ = get_weights("ffn1_w2", act1)
    w.update(w2)
    x1 = _ffn_down(x, act1, w["ffn1_w2"], "ffn1_down")
    wm, _ = get_weights("mix", x1)
    w.update(wm)
    w_ag, w_qkv, w_f = _split_w_in(w["w_in"])
    conv_w = jnp.pad(w["conv_w"], ((0, CONV_PAD - CONV_WIDTH), (0, 0)))
    h2, ag, qkv, fl = _mix_proj(x1, p["mix_norm"], w_ag, w_qkv, w_f)
    flt = _head_rows(fl[:, :N_HEADS].T)
    dcum = _gates_fwd(flt, fb)[:N_HEADS]
    drow = dcum.reshape(N_HEADS, 1, t)
    ycn, y_conv = _conv_fwd(ag, conv_w, p["conv_b"], p["conv_ln_g"], p["conv_ln_b"], p["out_norm_conv"])
    o, lse = _attn_fwd(qkv, drow, [ycn])
    _, deps = get_weights("ffn2:landed", o)
    x2, yan = _out_proj(ycn, o, p["out_norm_attn"], w["w_out"], x1, deps)
    w2, _ = get_weights("ffn2", x2)
    w.update(w2)
    x3, h3, gu2, act2 = _ffn_fwd(x2, p["ffn2_norm"], w["ffn2_w13"], w["ffn2_w2"], "ffn2_fwd")
    loss, dx3, d_final = _loss_bwd(x3, target, p["final_norm"])

    g = {}
    dx2, dgu2, g["ffn2_norm"], dx3_half, dx2_bf16 = _ffn_bwd(
        dx3, x2, gu2, p["ffn2_norm"], w["ffn2_w13"], w["ffn2_w2"], "ffn2_bwd")
    dw13 = _wgrad(h3, dgu2, N_CHIPS, "ffn2_dw13")
    dw2 = _wgrad(act2, dx3_half, 1, "ffn2_dw2").reshape(D_FF, D_MODEL)
    deps = put_grads("ffn2", {"ffn2_w13": dw13, "ffn2_w2": dw2})
    dyc, do, g["out_norm_attn"] = _out_proj_bwd(dx2_bf16, o, p["out_norm_attn"], w["w_out"], deps)
    deps = flush_grads("ffn2", [dyc])
    dw_out = _wgrad(jnp.concatenate([ycn, yan], axis=1), dx2_bf16, 1, "dw_out", deps).reshape(D_MODEL, D_MODEL)
    dq, dk, dv, ddrow = _attn_bwd(qkv, drow, lse, do)
    dflt, dfb = _gates_bwd(_head_rows(ddrow.reshape(N_HEADS, t)), flt, fb)
    g["forget_b"] = dfb[:N_HEADS, 0].reshape(1, N_HEADS)
    dfl = jnp.pad(dflt[:N_HEADS].T, ((0, 0), (0, LANES - N_HEADS)))
    dag, dconv_w, g["conv_b"], g["conv_ln_g"], g["conv_ln_b"], g["out_norm_conv"] = _conv_bwd(
        ag, y_conv, dyc, conv_w, p["conv_ln_g"], p["conv_ln_b"], p["out_norm_conv"])
    g["conv_w"] = dconv_w[:CONV_WIDTH]
    dproj = jnp.concatenate([dag, dq, dk, dv, dfl.astype(BF16)], axis=1)
    dx1, g["mix_norm"] = _mix_proj_bwd(dproj, dx2, x1, p["mix_norm"], w_ag, w_qkv, w_f)
    dw_in = _wgrad(dproj, h2, 1, "dw_in").reshape(dproj.shape[1], D_MODEL)[:N_IN]
    deps = put_grads("mix", {"w_in": dw_in, "w_out": dw_out})
    dx0, dgu1, g["ffn1_norm"], dx1_half, _ = _ffn_bwd(
        dx1, x, gu1, p["ffn1_norm"], w["ffn1_w13"], w["ffn1_w2"], "ffn1_bwd", deps)
    g["final_norm"] = d_final
    g["loss"] = loss[:, :1]
    deps = flush_grads("mix", put_grads("small", g))
    dw2 = _wgrad(act1, dx1_half, 1, "ffn1_dw2", deps).reshape(D_FF, D_MODEL)
    deps = flush_grads("ffn1_w2", put_grads("ffn1_w2", {"ffn1_w2": dw2}))
    dw13 = _wgrad(h1, dgu1, N_CHIPS, "ffn1_dw13", deps)
    put_grads("ffn1_w13", {"ffn1_w13": dw13})
    return dx0


MESH = pl.DeviceIdType.MESH


def _place():
    x, y, c = lax.axis_index("x"), lax.axis_index("y"), lax.axis_index("c")
    chips = [(1 - x, y), (x, 1 - y), (1 - x, 1 - y)]
    return x, y, c, chips


def _hbm_out(shape, dtype):
    return jax.ShapeDtypeStruct(shape, dtype)


def _comm_call(body, name, ins, out_shapes, n_remote, in_place=False):
    return pl.pallas_call(
        body, name=name, in_specs=[_ANY] * len(ins), out_specs=[_ANY] * len(out_shapes), out_shape=out_shapes,
        scratch_shapes=[pltpu.SemaphoreType.DMA((n_remote,)), pltpu.SemaphoreType.DMA((n_remote,))],
        input_output_aliases={i: i for i in range(len(ins))} if in_place else {},
    )(*ins)


def _remote(src, dst, sems, n, to):
    send_sems, recv_sems = sems
    return pltpu.make_async_remote_copy(src_ref=src, dst_ref=dst, send_sem=send_sems.at[n], recv_sem=recv_sems.at[n],
                                        device_id=to, device_id_type=MESH)


HALF_ROWS_MULTIPLE = 32


def _halved_by_rows(rows):
    return rows % HALF_ROWS_MULTIPLE == 0


def _half_shape(rows, cols):
    return (rows // 2, cols) if _halved_by_rows(rows) else (rows, cols // 2)


def _half_index(rows, core):
    return (core, 0) if _halved_by_rows(rows) else (0, core)


def _half_of(ref, rows, cols, core, *lead):
    if _halved_by_rows(rows):
        return ref.at[(*lead, pl.ds(core * (rows // 2), rows // 2), slice(None))]
    return ref.at[(*lead, slice(None), pl.ds(core * (cols // 2), cols // 2))]


def _into_slot(shard, chip, dtype, name, deps=()):
    rows, cols = shard.shape
    half = _half_shape(rows, cols)
    by_rows = _halved_by_rows(rows)
    deps = tuple(deps)

    def body(k_ref, *refs):
        s_ref, o_ref = refs[len(deps):]
        o_ref[0] = s_ref[...].astype(dtype)

    return pl.pallas_call(
        body, name=name,
        grid_spec=pltpu.PrefetchScalarGridSpec(
            num_scalar_prefetch=1, grid=(2,),
            in_specs=[_ANY] * len(deps) + [pl.BlockSpec(half, lambda i, k_ref: (i, 0) if by_rows else (0, i))],
            out_specs=pl.BlockSpec((1,) + half, lambda i, k_ref: (k_ref[0], i, 0) if by_rows else (k_ref[0], 0, i))),
        out_shape=jax.ShapeDtypeStruct((N_CHIPS, rows, cols), dtype),
        compiler_params=_params(dimension_semantics=("arbitrary",)),
    )(chip, *deps, shard)


def _run_copies(name, bufs, n_copies, plan):
    n = len(bufs)

    def body(*refs):
        copies = plan(refs[n:2 * n], refs[2 * n:2 * n + 2])
        for send, _ in copies:
            send.start()
        for send, recv in copies:
            send.wait_send()
            recv.wait_recv()

    return _comm_call(body, name, bufs, [_hbm_out(b.shape, b.dtype) for b in bufs], n_copies, in_place=True)


def _forward_halves(slots, name):
    return _run_copies(name, slots, 3 * len(slots), _d2d_forward_plan(slots))


_HBM = pl.BlockSpec(memory_space=pltpu.HBM)
_SEM = pl.BlockSpec(memory_space=pltpu.SEMAPHORE)
_DATAFLOW = pltpu.SideEffectType.DATAFLOW_SIDE_EFFECTING


def _split_copy_start(name, bufs, n_copies, plan):
    n = len(bufs)

    def body(*refs):
        for send, _ in plan(refs[:n], (refs[n], refs[n + 1])):
            send.start()
        token = refs[-1]
        token[...] = jnp.zeros_like(token)

    out = pl.pallas_call(
        body, name=name,
        out_shape=(pltpu.SemaphoreType.DMA((n_copies,)), pltpu.SemaphoreType.DMA((n_copies,)),
                   *[pltpu.HBM(b.shape, b.dtype) for b in bufs], jax.ShapeDtypeStruct((8, LANES), F32)),
        in_specs=[_HBM] * n, out_specs=(_SEM, _SEM, *[_HBM] * n, pl.BlockSpec(memory_space=pltpu.VMEM)),
        input_output_aliases={i: 2 + i for i in range(n)},
        compiler_params=pltpu.CompilerParams(has_side_effects=_DATAFLOW),
    )(*[pltpu.with_memory_space_constraint(b, pltpu.HBM) for b in bufs])
    return out[0], out[1], list(out[2:2 + n]), out[-1]


def _split_copy_wait(name, started, plan, after, passed=()):
    send_sems, recv_sems, bufs, _ = started
    n = len(bufs)
    after = tuple(after)
    bufs = list(bufs) + list(passed)
    total = len(bufs)

    def body(*refs):
        for send, recv in plan(refs[:n], (refs[total], refs[total + 1])):
            send.wait_send()
            recv.wait_recv()

    out = pl.pallas_call(
        body, name=name, out_shape=tuple(pltpu.HBM(b.shape, b.dtype) for b in bufs),
        in_specs=[_HBM] * total + [_SEM, _SEM] + [_ANY] * len(after), out_specs=tuple([_HBM] * total),
        input_output_aliases={i: i for i in range(total)},
        compiler_params=pltpu.CompilerParams(has_side_effects=_DATAFLOW),
    )(*bufs, send_sems, recv_sems, *after)
    return list(out)


def _ici_gather_plan(slots):
    def plan(refs, sems):
        x, y, c, chips = _place()
        me = 2 * x + y
        copies = []
        for i, ref in enumerate(refs):
            for j, chip in enumerate(chips):
                mine = _half_of(ref, *slots[i].shape[1:], c, me)
                theirs = _half_of(ref, *slots[i].shape[1:], c, 2 * chip[0] + chip[1])
                to = (*chip, c)
                copies.append((_remote(mine, mine, sems, 3 * i + j, to), _remote(theirs, theirs, sems, 3 * i + j, to)))
        return copies

    return plan


def _ici_scatter_plan(n):
    def plan(refs, sems):
        x, y, c, chips = _place()
        copies = []
        for i in range(n):
            for j, chip in enumerate(chips):
                cp = _remote(refs[i].at[2 * chip[0] + chip[1]], refs[n + i].at[j], sems, 3 * i + j, (*chip, c))
                copies.append((cp, cp))
        return copies

    return plan


def _d2d_forward_plan(slots):
    def plan(refs, sems):
        x, y, c, chips = _place()
        sibling = (x, y, 1 - c)
        copies = []
        for i, ref in enumerate(refs):
            for j, chip in enumerate(chips):
                src_chip = 2 * chip[0] + chip[1]
                mine = _half_of(ref, *slots[i].shape[1:], c, src_chip)
                theirs = _half_of(ref, *slots[i].shape[1:], 1 - c, src_chip)
                copies.append((_remote(mine, mine, sems, 3 * i + j, sibling),
                               _remote(theirs, theirs, sems, 3 * i + j, sibling)))
        return copies

    return plan


def _pair_exchange_plan(grads):
    n = len(grads)

    def plan(refs, sems):
        x, y, c, _ = _place()
        copies = []
        for i in range(n):
            theirs = _half_of(refs[i], *grads[i].shape[1:], 1 - c, slice(None))
            cp = _remote(theirs, refs[n + i], sems, i, (x, y, 1 - c))
            copies.append((cp, cp))
        return copies

    return plan


def _pair_share_plan(shapes):
    def plan(refs, sems):
        x, y, c, _ = _place()
        sibling = (x, y, 1 - c)
        copies = []
        for i, ref in enumerate(refs):
            mine, theirs = _half_of(ref, *shapes[i], c), _half_of(ref, *shapes[i], 1 - c)
            copies.append((_remote(mine, mine, sems, i, sibling), _remote(theirs, theirs, sems, i, sibling)))
        return copies

    return plan


def _pair_share(halves, name):
    return _run_copies(name, halves, len(halves), _pair_share_plan([h.shape for h in halves]))


N_DEVICES = 8
FLIPS = [(fx, fy, fc) for fx in range(2) for fy in range(2) for fc in range(2)][1:]


def _small_slots(v, me):
    rows = v.shape[0]

    def body(k_ref, v_ref, o_ref):
        o_ref[0] = v_ref[...]

    return pl.pallas_call(
        body, name="small_slot",
        grid_spec=pltpu.PrefetchScalarGridSpec(
            num_scalar_prefetch=1, grid=(1,),
            in_specs=[pl.BlockSpec((rows, LANES), lambda i, k_ref: (0, 0))],
            out_specs=pl.BlockSpec((1, rows, LANES), lambda i, k_ref: (k_ref[0], 0, 0))),
        out_shape=jax.ShapeDtypeStruct((N_DEVICES, rows, LANES), F32),
        compiler_params=_params(dimension_semantics=("arbitrary",)),
    )(me, v)


def _small_plan():
    def plan(refs, sems):
        x, y, c, _ = _place()
        slots = refs[0]
        me = 4 * x + 2 * y + c
        copies = []
        for n, (fx, fy, fc) in enumerate(FLIPS):
            to = (x ^ fx, y ^ fy, c ^ fc)
            src = 4 * to[0] + 2 * to[1] + to[2]
            copies.append((_remote(slots.at[me], slots.at[me], sems, n, to), _remote(slots.at[src], slots.at[src], sems, n, to)))
        return copies

    return plan


def _small_sum(slots):
    def body(s_ref, o_ref):
        acc = s_ref[0]
        for s in range(1, N_DEVICES):
            acc = acc + s_ref[s]
        o_ref[...] = acc

    return pl.pallas_call(body, name="small_sum", out_shape=jax.ShapeDtypeStruct(slots.shape[1:], F32),
                          compiler_params=_params())(slots)


def _pair_add(gs, sibs, core, name):
    n = len(gs)
    halves = [_half_shape(*g.shape[1:]) for g in gs]

    def body(c_ref, *refs):
        for g_ref, s_ref, o_ref in zip(refs[:n], refs[n:2 * n], refs[2 * n:]):
            o_ref[0] = (g_ref[0].astype(F32) + s_ref[0].astype(F32)).astype(BF16)

    def mine(g, half):
        return pl.BlockSpec((1,) + half, lambda s, c_ref: (s, *_half_index(g.shape[1], c_ref[0])))

    whole = [pl.BlockSpec((1,) + half, lambda s, c_ref: (s, 0, 0)) for half in halves]
    return pl.pallas_call(
        body, name=name,
        grid_spec=pltpu.PrefetchScalarGridSpec(
            num_scalar_prefetch=1, grid=(N_CHIPS,),
            in_specs=[mine(g, half) for g, half in zip(gs, halves)] + whole, out_specs=whole),
        out_shape=[jax.ShapeDtypeStruct((N_CHIPS,) + half, BF16) for half in halves],
        compiler_params=_params(dimension_semantics=("arbitrary",)),
    )(core, *gs, *sibs)


CHIP_ADD_STEPS = 2


def _chip_add(parts, recvs, chip_core, shapes, name):
    n = len(parts)
    by_rows = [_halved_by_rows(shape[0]) for shape in shapes]
    pieces = [(h[0] // CHIP_ADD_STEPS, h[1]) if rows else (h[0], h[1] // CHIP_ADD_STEPS)
              for h, rows in zip((_half_shape(*shape) for shape in shapes), by_rows)]

    def body(kc_ref, *refs):
        for p_ref, r_ref, o_ref in zip(refs[:n], refs[n:2 * n], refs[2 * n:]):
            acc = p_ref[0].astype(F32)
            for j in range(N_CHIPS - 1):
                acc = acc + r_ref[j].astype(F32)
            o_ref[...] = acc

    def at(rows, lead, piece_of):
        return lambda s, kc_ref: (*lead(kc_ref), piece_of(s, kc_ref), 0) if rows else (*lead(kc_ref), 0, piece_of(s, kc_ref))

    mine = [pl.BlockSpec((1,) + p, at(rows, lambda kc_ref: (kc_ref[0],), lambda s, kc_ref: s)) for p, rows in zip(pieces, by_rows)]
    theirs = [pl.BlockSpec((N_CHIPS - 1,) + p, at(rows, lambda kc_ref: (0,), lambda s, kc_ref: s)) for p, rows in zip(pieces, by_rows)]
    out = [pl.BlockSpec(p, at(rows, lambda kc_ref: (), lambda s, kc_ref: kc_ref[1] * CHIP_ADD_STEPS + s))
           for p, rows in zip(pieces, by_rows)]
    return pl.pallas_call(
        body, name=name,
        grid_spec=pltpu.PrefetchScalarGridSpec(
            num_scalar_prefetch=1, grid=(CHIP_ADD_STEPS,), in_specs=mine + theirs, out_specs=out),
        out_shape=[jax.ShapeDtypeStruct(tuple(shape), F32) for shape in shapes],
        compiler_params=_params(dimension_semantics=("arbitrary",)),
    )(chip_core, *parts, *recvs)


def _adamw_math(w, g, m, v):
    m = ADAM_B1 * m + (1.0 - ADAM_B1) * g
    v = ADAM_B2 * v + (1.0 - ADAM_B2) * (g * g)
    m_hat = m / (1.0 - ADAM_B1 ** ADAM_STEP)
    v_hat = v / (1.0 - ADAM_B2 ** ADAM_STEP)
    delta = -ADAM_LR * (m_hat / (jnp.sqrt(v_hat) + ADAM_EPS) + ADAM_WD * w)
    return delta, m, v


ADAM_PARTS = 2


def _adamw_matrix(w, g, m, v, name):
    rows, cols = w.shape
    by_rows = rows % (8 * ADAM_PARTS) == 0
    block = (rows // ADAM_PARTS, cols) if by_rows else (rows, cols // ADAM_PARTS)

    def body(w_ref, g_ref, m_ref, v_ref, go_ref, d_ref, mo_ref, vo_ref):
        gv = g_ref[...]
        go_ref[...] = gv
        d_ref[...], mo_ref[...], vo_ref[...] = _adamw_math(w_ref[...], gv, m_ref[...], v_ref[...])

    spec = pl.BlockSpec(block, lambda i: (i, 0) if by_rows else (0, i))
    shape = jax.ShapeDtypeStruct((rows, cols), F32)
    return pl.pallas_call(
        body, name=name, grid=(ADAM_PARTS,), in_specs=[spec] * 4, out_specs=[spec] * 4, out_shape=[shape] * 4,
        compiler_params=_params(dimension_semantics=("arbitrary",)),
    )(w, g, m, v)


def _adamw_small(ws, gs, ms, vs):
    n = len(ws)

    def body(*refs):
        for i in range(n):
            w_ref, g_ref, m_ref, v_ref = (refs[k * n + i] for k in range(4))
            d_ref, mo_ref, vo_ref = (refs[(4 + k) * n + i] for k in range(3))
            d_ref[...], mo_ref[...], vo_ref[...] = _adamw_math(w_ref[...], g_ref[...], m_ref[...], v_ref[...])

    shapes = [jax.ShapeDtypeStruct(w.shape, F32) for w in ws]
    out = pl.pallas_call(body, name="adamw_small", out_shape=shapes * 3, compiler_params=_params())(*ws, *gs, *ms, *vs)
    return out[:n], out[n:2 * n], out[2 * n:]


MATRICES = ["ffn1_w13", "ffn1_w2", "w_in", "w_out", "ffn2_w13", "ffn2_w2"]
VECTORS = ["ffn1_norm", "mix_norm", "conv_b", "conv_ln_g", "conv_ln_b", "forget_b", "out_norm_conv",
           "out_norm_attn", "ffn2_norm", "final_norm"]
WEIGHTS = ["ffn1_norm", "ffn1_w13", "ffn1_w2", "mix_norm", "w_in", "conv_w", "conv_b", "conv_ln_g", "conv_ln_b",
           "forget_b", "out_norm_conv", "out_norm_attn", "w_out", "ffn2_norm", "ffn2_w13", "ffn2_w2", "final_norm"]


def _pack_small(g, names):
    rows, layout = [], []
    for n in names:
        flat = g[n].reshape(-1)
        pad = (-flat.shape[0]) % LANES
        rows.append(jnp.pad(flat, (0, pad)).reshape(-1, LANES))
        layout.append((n, g[n].shape, flat.shape[0], rows[-1].shape[0]))
    packed = jnp.concatenate(rows, axis=0)
    pad_rows = (-packed.shape[0]) % 8
    return jnp.pad(packed, ((0, pad_rows), (0, 0))), layout


def _unpack_small(packed, layout):
    out, r = {}, 0
    for n, shape, size, nrows in layout:
        out[n] = packed[r:r + nrows].reshape(-1)[:size].reshape(shape)
        r += nrows
    return out


def kernel(x, ffn1_norm, ffn1_w13, ffn1_w2, mix_norm, w_in, conv_w, conv_b, conv_ln_g, conv_ln_b, forget_b, out_norm_conv, out_norm_attn, w_out, ffn2_norm, ffn2_w13, ffn2_w2, final_norm, loss_target, m_ffn1_norm, m_ffn1_w13, m_ffn1_w2, m_mix_norm, m_w_in, m_conv_w, m_conv_b, m_conv_ln_g, m_conv_ln_b, m_forget_b, m_out_norm_conv, m_out_norm_attn, m_w_out, m_ffn2_norm, m_ffn2_w13, m_ffn2_w2, m_final_norm, v_ffn1_norm, v_ffn1_w13, v_ffn1_w2, v_mix_norm, v_w_in, v_conv_w, v_conv_b, v_conv_ln_g, v_conv_ln_b, v_forget_b, v_out_norm_conv, v_out_norm_attn, v_w_out, v_ffn2_norm, v_ffn2_w13, v_ffn2_w2, v_final_norm):
    args = dict(locals())
    weights = {n: args[n] for n in WEIGHTS}
    core = lax.axis_index("c").astype(jnp.int32).reshape(1)
    chip = (2 * lax.axis_index("x") + lax.axis_index("y")).astype(jnp.int32)
    chip1 = chip.reshape(1)
    chip_core = jnp.concatenate([chip1, core])

    def held(n, a):
        return a[0].T if n == "w_in" else a[0]

    def given(n, a):
        return (a.T if n == "w_in" else a)[None]

    def slot(n, deps=()):
        if n == "conv_w":
            rows = jnp.pad(conv_w[0], ((0, CONV_PAD - CONV_WIDTH), (0, 0)))
            return _into_slot(rows, chip1, F32, "slot_conv_w", deps)
        return _into_slot(held(n, weights[n]), chip1, BF16, "slot_" + n, deps)

    fetched = {"ffn1_w13": ["ffn1_w13"], "ffn1_w2": ["ffn1_w2", "w_out"], "mix": ["w_in", "conv_w"],
               "ffn2": ["ffn2_w13", "ffn2_w2"]}
    fetch = {}

    def as_weights(group, bufs):
        out = {}
        for n, b in zip(fetched[group], bufs):
            if n.endswith("w13"):
                out[n] = b
            elif n != "conv_w":
                out[n] = b.reshape(N_CHIPS * b.shape[1], b.shape[2])
            else:
                out[n] = b[:, :CONV_WIDTH].transpose(1, 0, 2).reshape(CONV_WIDTH, D_CONV)
        return out

    def get_weights(group, after):
        if group == "ffn1_w13":
            first = [slot("ffn1_w13")]
            plan = _ici_gather_plan(first)
            started = _split_copy_start("gather_ffn1_w13_start", first, 3, plan)
            second = [slot(n, [started[3]]) for n in fetched["ffn1_w2"]]
            plan2 = _ici_gather_plan(second)
            fetch["ffn1_w2"] = plan2, _split_copy_start("gather_ffn1_w2_start", second, 3 * len(second), plan2)
            later_names = fetched["mix"] + fetched["ffn2"]
            later = [slot(n, [fetch["ffn1_w2"][1][3]]) for n in later_names]
            landed = _split_copy_wait("gather_ffn1_w13_wait", started, plan, [], passed=later)
            bufs = _forward_halves(landed[:1], "forward_ffn1_w13")
            behind = dict(zip(later_names, landed[1:]))
            for later in ("mix", "ffn2"):
                bufs_later = [behind[n] for n in fetched[later]]
                plan = _ici_gather_plan(bufs_later)
                fetch[later] = plan, _split_copy_start("gather_%s_start" % later, bufs_later, 3 * len(bufs_later), plan)
            return as_weights(group, bufs), [fetch["mix"][1][3], fetch["ffn2"][1][3]]
        plan, started = fetch[group.split(":")[0]]
        if group == "ffn2:landed":
            landed = _split_copy_wait("gather_ffn2_wait", started, plan, [after])
            plan = _d2d_forward_plan(landed)
            fetch["ffn2"] = plan, _split_copy_start("forward_ffn2_start", landed, 3 * len(landed), plan)
            return {}, [fetch["ffn2"][1][3]]
        if group == "ffn2":
            return as_weights(group, _split_copy_wait("forward_ffn2_wait", started, plan, [after])), []
        landed = _split_copy_wait("gather_%s_wait" % group, started, plan, [after])
        return as_weights(group, _forward_halves(landed, "forward_" + group)), []

    def shard_major(n, g):
        return g if n.endswith("w13") else g.reshape(N_CHIPS, g.shape[0] // N_CHIPS, g.shape[1])

    exchange, scatter = {}, {}
    small_names = VECTORS + ["conv_w"]
    small = {}

    def put_grads(group, grads):
        if group == "small":
            packed, layout = _pack_small(grads, small_names + ["loss"])
            me = (4 * lax.axis_index("x") + 2 * lax.axis_index("y") + lax.axis_index("c")).astype(jnp.int32).reshape(1)
            plan = _small_plan()
            exchange[group] = layout, plan, _split_copy_start("small_start", [_small_slots(packed, me)], len(FLIPS), plan)
            return [exchange[group][2][3]]
        names = list(grads)
        local = [shard_major(n, grads[n]) for n in names]
        landing = [lax.empty((N_CHIPS,) + _half_shape(*a.shape[1:]), BF16) for a in local]
        plan = _pair_exchange_plan(local)
        exchange[group] = names, plan, _split_copy_start("exchange_%s_start" % group, local + landing, len(local), plan)
        return [exchange[group][2][3]]

    def flush_grads(group, after):
        names, plan, started = exchange[group]
        done = _split_copy_wait("exchange_%s_wait" % group, started, plan, after)
        local, sib = done[:len(names)], done[len(names):]
        parts = list(_pair_add(local, sib, core, "pair_add_" + group))
        landing = [lax.empty((N_CHIPS - 1,) + q.shape[1:], BF16) for q in parts]
        plan = _ici_scatter_plan(len(parts))
        shapes = [a.shape[1:] for a in local]
        scatter[group] = names, plan, _split_copy_start("scatter_%s_start" % group, parts + landing, 3 * len(parts), plan), shapes
        return [scatter[group][2][3]]

    p = {n: weights[n] for n in VECTORS}
    p["final_norm"] = final_norm.reshape(1, D_MODEL)
    dx = _local_step(x[0], loss_target[0], p, get_weights, put_grads, flush_grads)
    layout, plan, started = exchange["small"]
    slots, = _split_copy_wait("small_wait", started, plan, [exchange["ffn1_w13"][2][3]])
    small.update(_unpack_small(_small_sum(slots), layout))
    loss = small["loss"].reshape(())

    grad = {n: small[n] for n in VECTORS}
    grad["final_norm"] = small["final_norm"].reshape(D_MODEL)
    grad["conv_w"] = lax.dynamic_slice_in_dim(small["conv_w"], chip * (D_CONV // N_CHIPS), D_CONV // N_CHIPS, axis=1)[None]

    delta, new_m, new_v = {}, {}, {}

    def reduce_chips(group, after):
        names, plan, started, shapes = scatter[group]
        done = _split_copy_wait("scatter_%s_wait" % group, started, plan, after)
        parts, landed = done[:len(names)], done[len(names):]
        return list(_chip_add(parts, landed, chip_core, shapes, "chip_add_" + group))

    def update(group, full):
        ends = []
        for n, reduced in zip(scatter[group][0], full):
            go, d, mo, vo = _adamw_matrix(held(n, weights[n]), reduced, held(n, args["m_" + n]), held(n, args["v_" + n]),
                                          "adamw_" + n)
            grad[n], delta[n], new_m[n], new_v[n] = given(n, go), given(n, d), given(n, mo), given(n, vo)
            ends.append(vo)
        return ends

    def share_start(group, halves):
        plan = _pair_share_plan(scatter[group][3])
        return plan, _split_copy_start("share_%s_start" % group, halves, len(halves), plan)

    halves_ffn2 = reduce_chips("ffn2", [exchange["ffn1_w13"][2][3]])
    plan_ffn2, share_ffn2 = share_start("ffn2", halves_ffn2)
    last_scatter = flush_grads("ffn1_w13", [share_ffn2[3]])
    halves_mix = reduce_chips("mix", last_scatter)
    plan_mix, share_mix = share_start("mix", halves_mix)
    done_ffn2 = update("ffn2", _split_copy_wait("share_ffn2_wait", share_ffn2, plan_ffn2, [share_mix[3]]))
    done_mix = update("mix", _split_copy_wait("share_mix_wait", share_mix, plan_mix, done_ffn2))
    as2d = lambda a: a.reshape(-1, a.shape[-1])
    ds, mos, vos = _adamw_small([as2d(weights[n]) for n in small_names], [as2d(grad[n]) for n in small_names],
                                [as2d(args["m_" + n]) for n in small_names], [as2d(args["v_" + n]) for n in small_names])
    for n, d, mo, vo in zip(small_names, ds, mos, vos):
        shape = weights[n].shape
        delta[n], new_m[n], new_v[n] = d.reshape(shape), mo.reshape(shape), vo.reshape(shape)
    behind = done_ffn2 + done_mix + [vos[0]]
    halves_w2 = reduce_chips("ffn1_w2", behind)
    halves_w13 = reduce_chips("ffn1_w13", behind)
    full_w2, full_w13 = _pair_share(halves_w2 + halves_w13, "pair_share_ffn1")
    update("ffn1_w2", [full_w2])
    update("ffn1_w13", [full_w13])

    return (loss, dx[None], *[grad[n] for n in WEIGHTS], *[delta[n] for n in WEIGHTS],
            *[new_m[n] for n in WEIGHTS], *[new_v[n] for n in WEIGHTS])
```

```python
import jax
import jax.numpy as jnp
from jax import lax
from jax.experimental import pallas as pl
from jax.experimental.pallas import tpu as pltpu

F32 = jnp.float32
BF16 = jnp.bfloat16

D_MODEL = 1024
D_FF = 2816
FF_SHARD = D_FF // 2
D_CONV = 512
D_ATTN = 512
N_HEADS = 8
HEAD_DIM = 64
CONV_WIDTH = 31
CONV_PAD = 32
N_IN = 2 * D_CONV + 3 * D_ATTN + N_HEADS
EPS = 1e-6
N_CHIPS = 4
LANES = 128
TOKEN_ROWS = 512
HEAD_ROWS = 16

ADAM_LR = 0.001
ADAM_B1 = 0.9
ADAM_B2 = 0.999
ADAM_EPS = 1e-08
ADAM_WD = 0.01
ADAM_STEP = 10

VMEM_LIMIT = 56 * 1024 * 1024

_NT = (((1,), (1,)), ((), ()))
_TN = (((0,), (0,)), ((), ()))


def _dot(a, b):
    return jnp.dot(a, b, preferred_element_type=F32)


def _dot_nt(a, b):
    return lax.dot_general(a, b, _NT, preferred_element_type=F32)


def _dot_tn(a, b):
    return lax.dot_general(a, b, _TN, preferred_element_type=F32)


def _params(**kw):
    return pltpu.CompilerParams(vmem_limit_bytes=VMEM_LIMIT, **kw)


def _sigmoid(x):
    return 1.0 / (1.0 + jnp.exp(-x))


def _rms_stats(x):
    return lax.rsqrt(jnp.mean(x * x, axis=-1, keepdims=True) + EPS)


def _rms_bwd(x, r, g, dh):
    t = dh * g
    dx = r * t - x * (r * r * r) * jnp.mean(t * x, axis=-1, keepdims=True)
    return dx, dh * x * r


def _silu_grad(z, sg):
    return sg * (1.0 + z * (1.0 - sg))


def _row_spec(tm, n):
    return pl.BlockSpec((tm, n), lambda i: (i, 0))


def _full_spec(shape):
    nd = len(shape)
    return pl.BlockSpec(shape, lambda i: (0,) * nd)


_ANY = pl.BlockSpec(memory_space=pl.ANY)


def _skip(n, body):
    return lambda *refs: body(*refs[n:])


FFN_ROWS = 256
FFN_WEIGHT_PARTS = N_CHIPS + 2


def _with_ffn_weights(w13_hbm, w2_hbm, w13_ref, w2_ref, sems, order, tile):
    first = pl.program_id(0) == 0
    copies = {}
    if w13_hbm is not None:
        for k in range(N_CHIPS):
            copies["w13", k] = pltpu.make_async_copy(w13_hbm.at[k], w13_ref.at[k], sems.at[k])
    if w2_hbm is not None:
        for half in range(2):
            rows = pl.ds(half * FF_SHARD, FF_SHARD)
            copies["w2", half] = pltpu.make_async_copy(w2_hbm.at[rows, :], w2_ref.at[rows, :], sems.at[N_CHIPS + half])

    @pl.when(first)
    def _():
        for part in order:
            copies[part].start()

        def ready(*parts):
            for part in parts:
                copies[part].wait()

        tile(ready)

    @pl.when(jnp.logical_not(first))
    def _():
        tile(lambda *parts: None)


def _ffn_fwd(x, g, w13s, w2, name, deps=()):
    t = x.shape[0]
    tm = FFN_ROWS
    deps = tuple(deps)

    def body(x_ref, g_ref, w13_hbm, w2_hbm, xo_ref, h_ref, gu_ref, a_ref, w13_ref, w2_ref, sems):
        def tile(ready):
            xv = x_ref[...]
            hb = (xv * _rms_stats(xv) * g_ref[...]).astype(BF16)
            h_ref[...] = hb
            acc = jnp.zeros((tm, D_MODEL), F32)
            for half in range(2):
                lo = half * FF_SHARD
                ready(("w13", half), ("w13", 2 + half))
                gate = _dot(hb, w13_ref[half])
                up = _dot(hb, w13_ref[2 + half])
                gu_ref[:, lo:lo + FF_SHARD] = gate.astype(BF16)
                gu_ref[:, D_FF + lo:D_FF + lo + FF_SHARD] = up.astype(BF16)
                a = (gate * _sigmoid(gate) * up).astype(BF16)
                a_ref[:, lo:lo + FF_SHARD] = a
                ready(("w2", half))
                acc = acc + _dot(a, w2_ref[lo:lo + FF_SHARD, :])
            xo_ref[...] = xv + 0.5 * acc

        _with_ffn_weights(w13_hbm, w2_hbm, w13_ref, w2_ref, sems,
                          [("w13", 0), ("w13", 2), ("w2", 0), ("w13", 1), ("w13", 3), ("w2", 1)], tile)

    return pl.pallas_call(
        _skip(len(deps), body), name=name, grid=(t // tm,),
        in_specs=[_ANY] * len(deps) + [_row_spec(tm, D_MODEL), _full_spec((1, D_MODEL)), _ANY, _ANY],
        out_specs=[_row_spec(tm, D_MODEL), _row_spec(tm, D_MODEL), _row_spec(tm, 2 * D_FF), _row_spec(tm, D_FF)],
        out_shape=[jax.ShapeDtypeStruct((t, D_MODEL), F32), jax.ShapeDtypeStruct((t, D_MODEL), BF16),
                   jax.ShapeDtypeStruct((t, 2 * D_FF), BF16), jax.ShapeDtypeStruct((t, D_FF), BF16)],
        scratch_shapes=[pltpu.VMEM(w13s.shape, BF16), pltpu.VMEM(w2.shape, BF16),
                        pltpu.SemaphoreType.DMA((FFN_WEIGHT_PARTS,))],
        compiler_params=_params(dimension_semantics=("arbitrary",)),
    )(*deps, x, g, w13s, w2)


def _ffn_up(x, g, w13s, name, deps=()):
    t = x.shape[0]
    tm = FFN_ROWS
    deps = tuple(deps)

    def body(x_ref, g_ref, w13_hbm, h_ref, gu_ref, a_ref, w13_ref, sems):
        def tile(ready):
            xv = x_ref[...]
            hb = (xv * _rms_stats(xv) * g_ref[...]).astype(BF16)
            h_ref[...] = hb
            for half in range(2):
                lo = half * FF_SHARD
                ready(("w13", half), ("w13", 2 + half))
                gate = _dot(hb, w13_ref[half])
                up = _dot(hb, w13_ref[2 + half])
                gu_ref[:, lo:lo + FF_SHARD] = gate.astype(BF16)
                gu_ref[:, D_FF + lo:D_FF + lo + FF_SHARD] = up.astype(BF16)
                a_ref[:, lo:lo + FF_SHARD] = (gate * _sigmoid(gate) * up).astype(BF16)

        _with_ffn_weights(w13_hbm, None, w13_ref, None, sems, [("w13", 0), ("w13", 2), ("w13", 1), ("w13", 3)], tile)

    return pl.pallas_call(
        _skip(len(deps), body), name=name, grid=(t // tm,),
        in_specs=[_ANY] * len(deps) + [_row_spec(tm, D_MODEL), _full_spec((1, D_MODEL)), _ANY],
        out_specs=[_row_spec(tm, D_MODEL), _row_spec(tm, 2 * D_FF), _row_spec(tm, D_FF)],
        out_shape=[jax.ShapeDtypeStruct((t, D_MODEL), BF16), jax.ShapeDtypeStruct((t, 2 * D_FF), BF16),
                   jax.ShapeDtypeStruct((t, D_FF), BF16)],
        scratch_shapes=[pltpu.VMEM(w13s.shape, BF16), pltpu.SemaphoreType.DMA((FFN_WEIGHT_PARTS,))],
        compiler_params=_params(dimension_semantics=("arbitrary",)),
    )(*deps, x, g, w13s)


def _ffn_down(x, a, w2, name):
    t = x.shape[0]
    tm = FFN_ROWS

    def body(x_ref, a_ref, w2_hbm, xo_ref, w2_ref, sems):
        def tile(ready):
            ready(("w2", 0))
            acc = _dot(a_ref[:, 0:FF_SHARD], w2_ref[0:FF_SHARD, :])
            ready(("w2", 1))
            acc = acc + _dot(a_ref[:, FF_SHARD:], w2_ref[FF_SHARD:, :])
            xo_ref[...] = x_ref[...] + 0.5 * acc

        _with_ffn_weights(None, w2_hbm, None, w2_ref, sems, [("w2", 0), ("w2", 1)], tile)

    return pl.pallas_call(
        body, name=name, grid=(t // tm,),
        in_specs=[_row_spec(tm, D_MODEL), _row_spec(tm, D_FF), _ANY],
        out_specs=_row_spec(tm, D_MODEL), out_shape=jax.ShapeDtypeStruct((t, D_MODEL), F32),
        scratch_shapes=[pltpu.VMEM(w2.shape, BF16), pltpu.SemaphoreType.DMA((FFN_WEIGHT_PARTS,))],
        compiler_params=_params(dimension_semantics=("arbitrary",)),
    )(x, a, w2)


def _ffn_bwd(dy, x, gu, g, w13s, w2, name, deps=()):
    t = x.shape[0]
    tm = FFN_ROWS
    deps = tuple(deps)

    def body(dy_ref, x_ref, gu_ref, g_ref, w13_hbm, w2_hbm, dx_ref, dgu_ref, dg_ref, dyh_ref, dxb_ref,
             w13_ref, w2_ref, sems):
        @pl.when(pl.program_id(0) == 0)
        def _():
            dg_ref[...] = jnp.zeros_like(dg_ref)

        def tile(ready):
            dyv = dy_ref[...]
            dyh = (0.5 * dyv).astype(BF16)
            dyh_ref[...] = dyh
            dh = jnp.zeros((tm, D_MODEL), F32)
            for half in range(2):
                lo = half * FF_SHARD
                ready(("w2", half))
                da = _dot_nt(dyh, w2_ref[lo:lo + FF_SHARD, :])
                gate = gu_ref[:, lo:lo + FF_SHARD].astype(F32)
                up = gu_ref[:, D_FF + lo:D_FF + lo + FF_SHARD].astype(F32)
                sg = _sigmoid(gate)
                act = gate * sg
                dgate = (da * up * _silu_grad(gate, sg)).astype(BF16)
                dup = (da * act).astype(BF16)
                dgu_ref[:, lo:lo + FF_SHARD] = dgate
                dgu_ref[:, D_FF + lo:D_FF + lo + FF_SHARD] = dup
                ready(("w13", half), ("w13", 2 + half))
                dh = dh + _dot_nt(dgate, w13_ref[half]) + _dot_nt(dup, w13_ref[2 + half])
            xv = x_ref[...]
            dxn, dg_rows = _rms_bwd(xv, _rms_stats(xv), g_ref[...], dh)
            dx = dyv + dxn
            dx_ref[...] = dx
            dxb_ref[...] = dx.astype(BF16)
            dg_ref[...] += jnp.sum(dg_rows, axis=0, keepdims=True)

        _with_ffn_weights(w13_hbm, w2_hbm, w13_ref, w2_ref, sems,
                          [("w2", 0), ("w13", 0), ("w13", 2), ("w2", 1), ("w13", 1), ("w13", 3)], tile)

    return pl.pallas_call(
        _skip(len(deps), body), name=name, grid=(t // tm,),
        in_specs=[_ANY] * len(deps) + [_row_spec(tm, D_MODEL), _row_spec(tm, D_MODEL), _row_spec(tm, 2 * D_FF),
                                       _full_spec((1, D_MODEL)), _ANY, _ANY],
        out_specs=[_row_spec(tm, D_MODEL), _row_spec(tm, 2 * D_FF),
                   _full_spec((1, D_MODEL)), _row_spec(tm, D_MODEL), _row_spec(tm, D_MODEL)],
        out_shape=[jax.ShapeDtypeStruct((t, D_MODEL), F32), jax.ShapeDtypeStruct((t, 2 * D_FF), BF16),
                   jax.ShapeDtypeStruct((1, D_MODEL), F32),
                   jax.ShapeDtypeStruct((t, D_MODEL), BF16), jax.ShapeDtypeStruct((t, D_MODEL), BF16)],
        scratch_shapes=[pltpu.VMEM(w13s.shape, BF16), pltpu.VMEM(w2.shape, BF16),
                        pltpu.SemaphoreType.DMA((FFN_WEIGHT_PARTS,))],
        compiler_params=_params(dimension_semantics=("arbitrary",)),
    )(*deps, dy, x, gu, g, w13s, w2)


WGRAD_ROWS = (1408, 1024, 512, 384, 256)


def _wgrad(a, b, n_blocks, name, deps=()):
    t, m = a.shape
    tm = next(rows for rows in WGRAD_ROWS if m % rows == 0)
    n = b.shape[1]
    bn = n // n_blocks
    deps = tuple(deps)
    assert a.dtype == BF16 and b.dtype == BF16

    def body(a_ref, b_ref, o_ref):
        o_ref[0] = _dot_tn(a_ref[...], b_ref[...]).astype(BF16)

    return pl.pallas_call(
        _skip(len(deps), body), name=name, grid=(n_blocks, m // tm),
        in_specs=[_ANY] * len(deps) + [pl.BlockSpec((t, tm), lambda j, i: (0, i)),
                                       pl.BlockSpec((t, bn), lambda j, i: (0, j))],
        out_specs=pl.BlockSpec((1, tm, bn), lambda j, i: (j, i, 0)),
        out_shape=jax.ShapeDtypeStruct((n_blocks, m, bn), BF16),
        compiler_params=_params(dimension_semantics=("arbitrary", "arbitrary")),
    )(*deps, a, b)


def _mix_proj(x, g, w_ag, w_qkv, w_f):
    t = x.shape[0]
    tm = TOKEN_ROWS

    def body(x_ref, g_ref, wag_ref, wqkv_ref, wf_ref, h_ref, ag_ref, qkv_ref, fl_ref):
        xv = x_ref[...]
        hb = (xv * _rms_stats(xv) * g_ref[...]).astype(BF16)
        h_ref[...] = hb
        ag_ref[...] = _dot_nt(hb, wag_ref[...])
        qkv_ref[...] = _dot_nt(hb, wqkv_ref[...]).astype(BF16)
        fl_ref[...] = _dot_nt(hb, wf_ref[...])

    return pl.pallas_call(
        body, name="mix_proj", grid=(t // tm,),
        in_specs=[_row_spec(tm, D_MODEL), _full_spec((1, D_MODEL)), _full_spec(w_ag.shape),
                  _full_spec(w_qkv.shape), _full_spec(w_f.shape)],
        out_specs=[_row_spec(tm, D_MODEL), _row_spec(tm, 2 * D_CONV), _row_spec(tm, 3 * D_ATTN),
                   _row_spec(tm, LANES)],
        out_shape=[jax.ShapeDtypeStruct((t, D_MODEL), BF16), jax.ShapeDtypeStruct((t, 2 * D_CONV), F32),
                   jax.ShapeDtypeStruct((t, 3 * D_ATTN), BF16), jax.ShapeDtypeStruct((t, LANES), F32)],
        compiler_params=_params(dimension_semantics=("arbitrary",)),
    )(x, g, w_ag, w_qkv, w_f)


def _mix_proj_bwd(dproj, dx2, x1, g, w_ag, w_qkv, w_f):
    t = x1.shape[0]
    tm = TOKEN_ROWS
    n_ag, n_qkv = 2 * D_CONV, 3 * D_ATTN

    def body(dp_ref, dx2_ref, x_ref, g_ref, wag_ref, wqkv_ref, wf_ref, dx_ref, dg_ref):
        @pl.when(pl.program_id(0) == 0)
        def _():
            dg_ref[...] = jnp.zeros_like(dg_ref)

        dh = (_dot(dp_ref[:, 0:n_ag], wag_ref[...]) + _dot(dp_ref[:, n_ag:n_ag + n_qkv], wqkv_ref[...])
              + _dot(dp_ref[:, n_ag + n_qkv:], wf_ref[...]))
        xv = x_ref[...]
        dxn, dg_rows = _rms_bwd(xv, _rms_stats(xv), g_ref[...], dh)
        dx_ref[...] = dx2_ref[...] + dxn
        dg_ref[...] += jnp.sum(dg_rows, axis=0, keepdims=True)

    return pl.pallas_call(
        body, name="mix_proj_bwd", grid=(t // tm,),
        in_specs=[_row_spec(tm, dproj.shape[1]),
                  _row_spec(tm, D_MODEL), _row_spec(tm, D_MODEL), _full_spec((1, D_MODEL)),
                  _full_spec(w_ag.shape), _full_spec(w_qkv.shape), _full_spec(w_f.shape)],
        out_specs=[_row_spec(tm, D_MODEL), _full_spec((1, D_MODEL))],
        out_shape=[jax.ShapeDtypeStruct((t, D_MODEL), F32), jax.ShapeDtypeStruct((1, D_MODEL), F32)],
        compiler_params=_params(dimension_semantics=("arbitrary",)),
    )(dproj, dx2, x1, g, w_ag, w_qkv, w_f)


def _split3(x):
    hi = x.astype(BF16)
    r1 = x - hi.astype(F32)
    mid = r1.astype(BF16)
    lo = (r1 - mid.astype(F32)).astype(BF16)
    return hi, mid, lo


def _gates_fwd(flt, fb):
    t = flt.shape[1]

    def body(f_ref, b_ref, d_ref):
        z = f_ref[...] + b_ref[...]
        logf = jnp.minimum(z, 0.0) - jnp.log(1.0 + jnp.exp(-jnp.abs(z)))
        row = lax.broadcasted_iota(jnp.int32, (LANES, LANES), 0)
        col = lax.broadcasted_iota(jnp.int32, (LANES, LANES), 1)
        upper = (row <= col).astype(BF16)
        carry = jnp.zeros((HEAD_ROWS, 1), F32)
        for blk in range(t // LANES):
            hi, mid, lo = _split3(logf[:, blk * LANES:(blk + 1) * LANES])
            cs = _dot(hi, upper) + _dot(mid, upper) + _dot(lo, upper)
            d_ref[:, blk * LANES:(blk + 1) * LANES] = cs + carry
            carry = carry + cs[:, LANES - 1:LANES]

    return pl.pallas_call(
        body, name="gates_fwd", out_shape=jax.ShapeDtypeStruct((HEAD_ROWS, t), F32),
        compiler_params=_params(),
    )(flt, fb)


def _gates_bwd(dd, flt, fb):
    t = flt.shape[1]

    def body(dd_ref, f_ref, b_ref, df_ref, db_ref):
        z = f_ref[...] + b_ref[...]
        row = lax.broadcasted_iota(jnp.int32, (LANES, LANES), 0)
        col = lax.broadcasted_iota(jnp.int32, (LANES, LANES), 1)
        lower = (row >= col).astype(BF16)
        carry = jnp.zeros((HEAD_ROWS, 1), F32)
        db = jnp.zeros((HEAD_ROWS, 1), F32)
        for blk in reversed(range(t // LANES)):
            sl = slice(blk * LANES, (blk + 1) * LANES)
            hi, mid, lo = _split3(dd_ref[:, sl])
            cs = _dot(hi, lower) + _dot(mid, lower) + _dot(lo, lower)
            dz = (cs + carry) * _sigmoid(-z[:, sl])
            df_ref[:, sl] = dz
            db = db + jnp.sum(dz, axis=1, keepdims=True)
            carry = carry + cs[:, 0:1]
        db_ref[...] = db

    return pl.pallas_call(
        body, name="gates_bwd",
        out_shape=[jax.ShapeDtypeStruct((HEAD_ROWS, t), F32), jax.ShapeDtypeStruct((HEAD_ROWS, 1), F32)],
        compiler_params=_params(),
    )(dd, flt, fb)


CONV_CHUNK = 128
CONV_TAIL = 16
CONV_WINDOW = CONV_CHUNK + CONV_PAD + 8
CONV_ROWS_EXTRA = CONV_PAD + CONV_TAIL
SUBLANES = 8


def _conv_rows(ag_ref, u_ref, t):
    u_ref[0:CONV_PAD, :] = jnp.zeros((CONV_PAD, D_CONV), F32)
    u_ref[CONV_PAD + t:CONV_ROWS_EXTRA + t, :] = jnp.zeros((CONV_TAIL, D_CONV), F32)

    def fill(i, c):
        r0 = pl.multiple_of(i * CONV_CHUNK, CONV_CHUNK)
        a = ag_ref[pl.ds(r0, CONV_CHUNK), 0:D_CONV]
        gt = ag_ref[pl.ds(r0, CONV_CHUNK), D_CONV:2 * D_CONV]
        u_ref[pl.ds(CONV_PAD + r0, CONV_CHUNK), :] = a * _sigmoid(gt)
        return c

    lax.fori_loop(0, t // CONV_CHUNK, fill, 0)


def _for_shifted(ref, r0, offsets, fn):
    window = ref[pl.ds(r0, CONV_WINDOW), :]
    for rem in range(SUBLANES):
        mine = [o for o in offsets if o % SUBLANES == rem]
        if not mine:
            continue
        turned = window if rem == 0 else pltpu.roll(window, CONV_WINDOW - rem, 0)
        for o in mine:
            fn(o, turned[o - rem:o - rem + CONV_CHUNK])


def _conv_taps(u_ref, r0, w_ref, cb):
    acc = [jnp.zeros((CONV_CHUNK, D_CONV), F32)]

    def tap(o, rows):
        j = o - (CONV_PAD - CONV_WIDTH + 1)
        acc[0] = acc[0] + w_ref[j:j + 1, :] * rows

    _for_shifted(u_ref, r0, [j + CONV_PAD - CONV_WIDTH + 1 for j in range(CONV_WIDTH)], tap)
    return acc[0] + cb


def _conv_point(y, lg, lb):
    mu = jnp.mean(y, axis=-1, keepdims=True)
    yc = y - mu
    rstd = lax.rsqrt(jnp.mean(yc * yc, axis=-1, keepdims=True) + EPS)
    yhat = yc * rstd
    z = yhat * lg + lb
    sg = _sigmoid(z)
    s = z * sg
    rr = _rms_stats(s)
    return yhat, rstd, z, sg, s, rr


def _conv_fwd(ag, conv_w, conv_b, ln_g, ln_b, norm_g):
    t = ag.shape[0]

    def body(ag_ref, w_ref, cb_ref, lg_ref, lb_ref, ng_ref, o_ref, y_ref, u_ref):
        _conv_rows(ag_ref, u_ref, t)
        cb, lg, lb, ng = cb_ref[...], lg_ref[...], lb_ref[...], ng_ref[...]

        def chunk(i, c):
            r0 = pl.multiple_of(i * CONV_CHUNK, CONV_CHUNK)
            y = _conv_taps(u_ref, r0, w_ref, cb)
            y_ref[pl.ds(r0, CONV_CHUNK), :] = y
            _, _, _, _, s, rr = _conv_point(y, lg, lb)
            o_ref[pl.ds(r0, CONV_CHUNK), :] = (s * rr * ng).astype(BF16)
            return c

        lax.fori_loop(0, t // CONV_CHUNK, chunk, 0)

    return pl.pallas_call(
        body, name="conv_fwd",
        out_shape=[jax.ShapeDtypeStruct((t, D_CONV), BF16), jax.ShapeDtypeStruct((t, D_CONV), F32)],
        scratch_shapes=[pltpu.VMEM((t + CONV_ROWS_EXTRA, D_CONV), F32)],
        compiler_params=_params(),
    )(ag, conv_w, conv_b, ln_g, ln_b, norm_g)


def _conv_bwd(ag, y, dout, conv_w, ln_g, ln_b, norm_g):
    t = ag.shape[0]

    def body(ag_ref, y_ref, do_ref, w_ref, lg_ref, lb_ref, ng_ref,
             dag_ref, dw_ref, dcb_ref, dlg_ref, dlb_ref, dng_ref, u_ref, dy_ref):
        _conv_rows(ag_ref, u_ref, t)
        dy_ref[t:t + CONV_ROWS_EXTRA, :] = jnp.zeros((CONV_ROWS_EXTRA, D_CONV), F32)
        lg, lb, ng = lg_ref[...], lb_ref[...], ng_ref[...]
        dw_ref[...] = jnp.zeros_like(dw_ref)
        zero = jnp.zeros((1, D_CONV), F32)

        def chunk(i, carry):
            dcb, dlg, dlb, dng = carry
            r0 = pl.multiple_of(i * CONV_CHUNK, CONV_CHUNK)
            yhat, rstd, z, sg, s, rr = _conv_point(y_ref[pl.ds(r0, CONV_CHUNK), :], lg, lb)
            do = do_ref[pl.ds(r0, CONV_CHUNK), :]
            ds, dng_rows = _rms_bwd(s, rr, ng, do)
            dz = ds * _silu_grad(z, sg)
            dyhat = dz * lg
            dy = rstd * (dyhat - jnp.mean(dyhat, axis=-1, keepdims=True)
                         - yhat * jnp.mean(dyhat * yhat, axis=-1, keepdims=True))
            dy_ref[pl.ds(r0, CONV_CHUNK), :] = dy
            def tap(o, rows):
                j = o - (CONV_PAD - CONV_WIDTH + 1)
                dw_ref[j:j + 1, :] += jnp.sum(dy * rows, axis=0, keepdims=True)

            _for_shifted(u_ref, r0, [j + CONV_PAD - CONV_WIDTH + 1 for j in range(CONV_WIDTH)], tap)
            return (dcb + jnp.sum(dy, axis=0, keepdims=True), dlg + jnp.sum(dz * yhat, axis=0, keepdims=True),
                    dlb + jnp.sum(dz, axis=0, keepdims=True), dng + jnp.sum(dng_rows, axis=0, keepdims=True))

        dcb, dlg, dlb, dng = lax.fori_loop(0, t // CONV_CHUNK, chunk, (zero, zero, zero, zero))
        dcb_ref[...] = dcb
        dlg_ref[...] = dlg
        dlb_ref[...] = dlb
        dng_ref[...] = dng

        def chunk2(i, c):
            r0 = pl.multiple_of(i * CONV_CHUNK, CONV_CHUNK)
            acc = [jnp.zeros((CONV_CHUNK, D_CONV), F32)]

            def tap(o, rows):
                j = CONV_WIDTH - 1 - o
                acc[0] = acc[0] + w_ref[j:j + 1, :] * rows

            _for_shifted(dy_ref, r0, list(range(CONV_WIDTH)), tap)
            du = acc[0]
            a = ag_ref[pl.ds(r0, CONV_CHUNK), 0:D_CONV]
            gt = ag_ref[pl.ds(r0, CONV_CHUNK), D_CONV:2 * D_CONV]
            sg = _sigmoid(gt)
            dag_ref[pl.ds(r0, CONV_CHUNK), 0:D_CONV] = (du * sg).astype(BF16)
            dag_ref[pl.ds(r0, CONV_CHUNK), D_CONV:2 * D_CONV] = (du * a * sg * (1.0 - sg)).astype(BF16)
            return c

        lax.fori_loop(0, t // CONV_CHUNK, chunk2, 0)

    vec = jax.ShapeDtypeStruct((1, D_CONV), F32)
    return pl.pallas_call(
        body, name="conv_bwd",
        out_shape=[jax.ShapeDtypeStruct((t, 2 * D_CONV), BF16), jax.ShapeDtypeStruct((CONV_PAD, D_CONV), F32),
                   vec, vec, vec, vec],
        scratch_shapes=[pltpu.VMEM((t + CONV_ROWS_EXTRA, D_CONV), F32), pltpu.VMEM((t + CONV_ROWS_EXTRA, D_CONV), F32)],
        compiler_params=_params(),
    )(ag, y, dout, conv_w, ln_g, ln_b, norm_g)


Q_ROWS = 256
ATTN_SCALE = HEAD_DIM ** -0.5
ATTN_AHEAD = 1


def _attn_specs(t):
    blk = lambda off: pl.BlockSpec((t, LANES), lambda p: (0, off + p))
    pairs = N_HEADS // 2
    return [blk(0), blk(pairs), blk(2 * pairs), pl.BlockSpec((2, 1, t), lambda p: (p, 0, 0))]


def _one_head(q2, mask):
    return jnp.where(mask, q2, jnp.zeros_like(q2)) * ATTN_SCALE


def _attn_scores(qs, k2, drow, r0, q1):
    s = _dot_nt(qs, k2) - drow
    rowi = lax.broadcasted_iota(jnp.int32, (q1 - r0, q1 - r0), 0)
    coli = lax.broadcasted_iota(jnp.int32, (q1 - r0, q1 - r0), 1)
    diag = jnp.where(coli <= rowi, s[:, r0:q1], -jnp.inf)
    return diag if r0 == 0 else jnp.concatenate([s[:, :r0], diag], axis=1)


def _attn_fwd(qkv, drow, deps=()):
    t = qkv.shape[0]
    deps = tuple(deps)

    def body(q_ref, k_ref, v_ref, dr_ref, o_ref, lse_ref):
        head_a = lax.broadcasted_iota(jnp.int32, (1, LANES), 1) < HEAD_DIM
        items = [(qb, hh) for qb in range(t // Q_ROWS) for hh in range(2)]

        def scores(item):
            qb, hh = item
            r0, q1 = qb * Q_ROWS, (qb + 1) * Q_ROWS
            qs = _one_head(q_ref[r0:q1, :], head_a if hh == 0 else ~head_a)
            return _attn_scores(qs, k_ref[0:q1, :], dr_ref[hh, :, 0:q1], r0, q1)

        ahead = [scores(item) for item in items[:ATTN_AHEAD]]
        outs = []
        for n, (qb, hh) in enumerate(items):
            r0, q1 = qb * Q_ROWS, (qb + 1) * Q_ROWS
            s = ahead.pop(0)
            if n + ATTN_AHEAD < len(items):
                ahead.append(scores(items[n + ATTN_AHEAD]))
            mx = jnp.max(s, axis=1, keepdims=True)
            p = jnp.exp(s - mx)
            l = jnp.sum(p, axis=1, keepdims=True)
            lse_ref[hh, r0:q1, :] = mx + jnp.log(l)
            outs.append(_dot(p.astype(BF16), v_ref[0:q1, :]) * (1.0 / l))
            if hh == 1:
                o_ref[r0:q1, :] = jnp.where(head_a, outs[0], outs[1])
                outs = []

    pairs = N_HEADS // 2
    return pl.pallas_call(
        _skip(len(deps), body), name="attn_fwd", grid=(pairs,), in_specs=[_ANY] * len(deps) + _attn_specs(t),
        out_specs=[pl.BlockSpec((t, LANES), lambda p: (0, p)), pl.BlockSpec((2, t, 1), lambda p: (p, 0, 0))],
        out_shape=[jax.ShapeDtypeStruct((t, D_ATTN), F32), jax.ShapeDtypeStruct((N_HEADS, t, 1), F32)],
        compiler_params=_params(dimension_semantics=("arbitrary",)),
    )(*deps, qkv, qkv, qkv, drow)


def _attn_bwd(qkv, drow, lse, do):
    t = qkv.shape[0]

    def body(q_ref, k_ref, v_ref, dr_ref, lse_ref, do_ref,
             dq_ref, dk_ref, dv_ref, dd_ref, dk_acc, dv_acc):
        head_a = lax.broadcasted_iota(jnp.int32, (1, LANES), 1) < HEAD_DIM
        dk_acc[...] = jnp.zeros_like(dk_acc)
        dv_acc[...] = jnp.zeros_like(dv_acc)
        dd_ref[...] = jnp.zeros_like(dd_ref)
        items = [(qb, hh) for qb in range(t // Q_ROWS) for hh in range(2)]

        def products(item):
            qb, hh = item
            r0, q1 = qb * Q_ROWS, (qb + 1) * Q_ROWS
            mask = head_a if hh == 0 else ~head_a
            qs = _one_head(q_ref[r0:q1, :], mask)
            dob = jnp.where(mask, do_ref[r0:q1, :], 0.0).astype(BF16)
            s = _attn_scores(qs, k_ref[0:q1, :], dr_ref[hh, :, 0:q1], r0, q1)
            return qs, dob, s, _dot_nt(dob, v_ref[0:q1, :])

        ahead = products(items[0])
        dqs = []
        for n, (qb, hh) in enumerate(items):
            r0, q1 = qb * Q_ROWS, (qb + 1) * Q_ROWS
            qs, dob, s, dp = ahead
            if n + 1 < len(items):
                ahead = products(items[n + 1])
            p = jnp.exp(s - lse_ref[hh, r0:q1, :])
            ds = p * (dp - jnp.sum(p * dp, axis=1, keepdims=True))
            dsb = ds.astype(BF16)
            dqs.append(_dot(dsb, k_ref[0:q1, :]) * ATTN_SCALE)
            dk_acc[0:q1, :] += _dot_tn(dsb, qs)
            dv_acc[0:q1, :] += _dot_tn(p.astype(BF16), dob)
            dd_ref[hh, :, 0:q1] -= jnp.sum(ds, axis=0, keepdims=True)
            if hh == 1:
                dq_ref[r0:q1, :] = jnp.where(head_a, dqs[0], dqs[1]).astype(BF16)
                dqs = []
        dk_ref[...] = dk_acc[...].astype(BF16)
        dv_ref[...] = dv_acc[...].astype(BF16)

    pairs = N_HEADS // 2
    col = pl.BlockSpec((t, LANES), lambda p: (0, p))
    grad = jax.ShapeDtypeStruct((t, D_ATTN), BF16)
    return pl.pallas_call(
        body, name="attn_bwd", grid=(pairs,),
        in_specs=_attn_specs(t) + [pl.BlockSpec((2, t, 1), lambda p: (p, 0, 0)), col],
        out_specs=[col, col, col, pl.BlockSpec((2, 1, t), lambda p: (p, 0, 0))],
        out_shape=[grad, grad, grad, jax.ShapeDtypeStruct((N_HEADS, 1, t), F32)],
        scratch_shapes=[pltpu.VMEM((t, LANES), F32), pltpu.VMEM((t, LANES), F32)],
        compiler_params=_params(dimension_semantics=("arbitrary",)),
    )(qkv, qkv, qkv, drow, lse, do)


def _out_proj(ycn, o, g_attn, w_out, x1, deps=()):
    t = x1.shape[0]
    tm = TOKEN_ROWS
    deps = tuple(deps)

    def body(yc_ref, o_ref, g_ref, w_ref, x_ref, xo_ref, ya_ref):
        ov = o_ref[...]
        ya = (ov * _rms_stats(ov) * g_ref[...]).astype(BF16)
        ya_ref[...] = ya
        xo_ref[...] = x_ref[...] + _dot(yc_ref[...], w_ref[0:D_CONV, :]) + _dot(ya, w_ref[D_CONV:, :])

    return pl.pallas_call(
        _skip(len(deps), body), name="out_proj", grid=(t // tm,),
        in_specs=[_ANY] * len(deps) + [_row_spec(tm, D_CONV), _row_spec(tm, D_ATTN), _full_spec((1, D_ATTN)),
                                       _full_spec(w_out.shape), _row_spec(tm, D_MODEL)],
        out_specs=[_row_spec(tm, D_MODEL), _row_spec(tm, D_ATTN)],
        out_shape=[jax.ShapeDtypeStruct((t, D_MODEL), F32), jax.ShapeDtypeStruct((t, D_ATTN), BF16)],
        compiler_params=_params(dimension_semantics=("arbitrary",)),
    )(*deps, ycn, o, g_attn, w_out, x1)


def _out_proj_bwd(dx2, o, g_attn, w_out, deps=()):
    t = dx2.shape[0]
    tm = TOKEN_ROWS
    deps = tuple(deps)

    def body(dx_ref, o_ref, g_ref, w_ref, dyc_ref, do_ref, dg_ref):
        @pl.when(pl.program_id(0) == 0)
        def _():
            dg_ref[...] = jnp.zeros_like(dg_ref)

        dxb = dx_ref[...]
        dyc_ref[...] = _dot_nt(dxb, w_ref[0:D_CONV, :])
        dya = _dot_nt(dxb, w_ref[D_CONV:, :])
        ov = o_ref[...]
        do, dg_rows = _rms_bwd(ov, _rms_stats(ov), g_ref[...], dya)
        do_ref[...] = do
        dg_ref[...] += jnp.sum(dg_rows, axis=0, keepdims=True)

    return pl.pallas_call(
        _skip(len(deps), body), name="out_proj_bwd", grid=(t // tm,),
        in_specs=[_ANY] * len(deps) + [_row_spec(tm, D_MODEL), _row_spec(tm, D_ATTN), _full_spec((1, D_ATTN)),
                                       _full_spec(w_out.shape)],
        out_specs=[_row_spec(tm, D_CONV), _row_spec(tm, D_ATTN), _full_spec((1, D_ATTN))],
        out_shape=[jax.ShapeDtypeStruct((t, D_CONV), F32), jax.ShapeDtypeStruct((t, D_ATTN), F32),
                   jax.ShapeDtypeStruct((1, D_ATTN), F32)],
        compiler_params=_params(dimension_semantics=("arbitrary",)),
    )(*deps, dx2, o, g_attn, w_out)


def _loss_bwd(x3, target, g):
    t = x3.shape[0]
    tm = TOKEN_ROWS

    def body(x_ref, t_ref, g_ref, loss_ref, dx_ref, dg_ref):
        @pl.when(pl.program_id(0) == 0)
        def _():
            loss_ref[...] = jnp.zeros_like(loss_ref)
            dg_ref[...] = jnp.zeros_like(dg_ref)

        xv = x_ref[...]
        r = _rms_stats(xv)
        gv = g_ref[...]
        err = xv * r * gv - t_ref[...]
        row = jnp.sum(err * err, axis=1, keepdims=True) * (0.5 / D_MODEL)
        loss_ref[...] += jnp.sum(row, axis=0, keepdims=True)
        dx, dg_rows = _rms_bwd(xv, r, gv, err * (1.0 / D_MODEL))
        dx_ref[...] = dx
        dg_ref[...] += jnp.sum(dg_rows, axis=0, keepdims=True)

    return pl.pallas_call(
        body, name="loss_bwd", grid=(t // tm,),
        in_specs=[_row_spec(tm, D_MODEL), _row_spec(tm, D_MODEL), _full_spec((1, D_MODEL))],
        out_specs=[_full_spec((1, LANES)), _row_spec(tm, D_MODEL), _full_spec((1, D_MODEL))],
        out_shape=[jax.ShapeDtypeStruct((1, LANES), F32), jax.ShapeDtypeStruct((t, D_MODEL), F32),
                   jax.ShapeDtypeStruct((1, D_MODEL), F32)],
        compiler_params=_params(dimension_semantics=("arbitrary",)),
    )(x3, target, g)


def _split_w_in(w_in_t):
    w_ag = w_in_t[:2 * D_CONV]
    w_qkv = w_in_t[2 * D_CONV:2 * D_CONV + 3 * D_ATTN]
    w_f = jnp.pad(w_in_t[2 * D_CONV + 3 * D_ATTN:], ((0, LANES - N_HEADS), (0, 0)))
    return w_ag, w_qkv, w_f


def _head_rows(v):
    return jnp.pad(v, ((0, HEAD_ROWS - N_HEADS),) + ((0, 0),) * (v.ndim - 1))


def _local_step(x, target, p, get_weights, put_grads, flush_grads):
    t = x.shape[0]
    fb = _head_rows(p["forget_b"].reshape(N_HEADS, 1))

    w, deps = get_weights("ffn1_w13", None)
    h1, gu1, act1 = _ffn_up(x, p["ffn1_norm"], w["ffn1_w13"], "ffn1_up", deps)
    w2, _ = get_weights("ffn1_w2", act1)
    w.update(w2)
    x1 = _ffn_down(x, act1, w["ffn1_w2"], "ffn1_down")
    wm, _ = get_weights("mix", x1)
    w.update(wm)
    w_ag, w_qkv, w_f = _split_w_in(w["w_in"])
    conv_w = jnp.pad(w["conv_w"], ((0, CONV_PAD - CONV_WIDTH), (0, 0)))
    h2, ag, qkv, fl = _mix_proj(x1, p["mix_norm"], w_ag, w_qkv, w_f)
    flt = _head_rows(fl[:, :N_HEADS].T)
    dcum = _gates_fwd(flt, fb)[:N_HEADS]
    drow = dcum.reshape(N_HEADS, 1, t)
    ycn, y_conv = _conv_fwd(ag, conv_w, p["conv_b"], p["conv_ln_g"], p["conv_ln_b"], p["out_norm_conv"])
    o, lse = _attn_fwd(qkv, drow, [ycn])
    _, deps = get_weights("ffn2:landed", o)
    x2, yan = _out_proj(ycn, o, p["out_norm_attn"], w["w_out"], x1, deps)
    w2, _ = get_weights("ffn2", x2)
    w.update(w2)
    x3, h3, gu2, act2 = _ffn_fwd(x2, p["ffn2_norm"], w["ffn2_w13"], w["ffn2_w2"], "ffn2_fwd")
    loss, dx3, d_final = _loss_bwd(x3, target, p["final_norm"])

    g = {}
    dx2, dgu2, g["ffn2_norm"], dx3_half, dx2_bf16 = _ffn_bwd(
        dx3, x2, gu2, p["ffn2_norm"], w["ffn2_w13"], w["ffn2_w2"], "ffn2_bwd")
    dw13 = _wgrad(h3, dgu2, N_CHIPS, "ffn2_dw13")
    dw2 = _wgrad(act2, dx3_half, 1, "ffn2_dw2").reshape(D_FF, D_MODEL)
    deps = put_grads("ffn2", {"ffn2_w13": dw13, "ffn2_w2": dw2})
    dyc, do, g["out_norm_attn"] = _out_proj_bwd(dx2_bf16, o, p["out_norm_attn"], w["w_out"], deps)
    deps = flush_grads("ffn2", [dyc])
    dw_out = _wgrad(jnp.concatenate([ycn, yan], axis=1), dx2_bf16, 1, "dw_out", deps).reshape(D_MODEL, D_MODEL)
    dq, dk, dv, ddrow = _attn_bwd(qkv, drow, lse, do)
    dflt, dfb = _gates_bwd(_head_rows(ddrow.reshape(N_HEADS, t)), flt, fb)
    g["forget_b"] = dfb[:N_HEADS, 0].reshape(1, N_HEADS)
    dfl = jnp.pad(dflt[:N_HEADS].T, ((0, 0), (0, LANES - N_HEADS)))
    dag, dconv_w, g["conv_b"], g["conv_ln_g"], g["conv_ln_b"], g["out_norm_conv"] = _conv_bwd(
        ag, y_conv, dyc, conv_w, p["conv_ln_g"], p["conv_ln_b"], p["out_norm_conv"])
    g["conv_w"] = dconv_w[:CONV_WIDTH]
    dproj = jnp.concatenate([dag, dq, dk, dv, dfl.astype(BF16)], axis=1)
    dx1, g["mix_norm"] = _mix_proj_bwd(dproj, dx2, x1, p["mix_norm"], w_ag, w_qkv, w_f)
    dw_in = _wgrad(dproj, h2, 1, "dw_in").reshape(dproj.shape[1], D_MODEL)[:N_IN]
    deps = put_grads("mix", {"w_in": dw_in, "w_out": dw_out})
    dx0, dgu1, g["ffn1_norm"], dx1_half, _ = _ffn_bwd(
        dx1, x, gu1, p["ffn1_norm"], w["ffn1_w13"], w["ffn1_w2"], "ffn1_bwd", deps)
    g["final_norm"] = d_final
    g["loss"] = loss[:, :1]
    deps = flush_grads("mix", put_grads("small", g))
    dw2 = _wgrad(act1, dx1_half, 1, "ffn1_dw2", deps).reshape(D_FF, D_MODEL)
    deps = flush_grads("ffn1_w2", put_grads("ffn1_w2", {"ffn1_w2": dw2}))
    dw13 = _wgrad(h1, dgu1, N_CHIPS, "ffn1_dw13", deps)
    put_grads("ffn1_w13", {"ffn1_w13": dw13})
    return dx0


MESH = pl.DeviceIdType.MESH


def _place():
    x, y, c = lax.axis_index("x"), lax.axis_index("y"), lax.axis_index("c")
    chips = [(1 - x, y), (x, 1 - y), (1 - x, 1 - y)]
    return x, y, c, chips


def _hbm_out(shape, dtype):
    return jax.ShapeDtypeStruct(shape, dtype)


def _comm_call(body, name, ins, out_shapes, n_remote, in_place=False):
    return pl.pallas_call(
        body, name=name, in_specs=[_ANY] * len(ins), out_specs=[_ANY] * len(out_shapes), out_shape=out_shapes,
        scratch_shapes=[pltpu.SemaphoreType.DMA((n_remote,)), pltpu.SemaphoreType.DMA((n_remote,))],
        input_output_aliases={i: i for i in range(len(ins))} if in_place else {},
    )(*ins)


def _remote(src, dst, sems, n, to):
    send_sems, recv_sems = sems
    return pltpu.make_async_remote_copy(src_ref=src, dst_ref=dst, send_sem=send_sems.at[n], recv_sem=recv_sems.at[n],
                                        device_id=to, device_id_type=MESH)


HALF_ROWS_MULTIPLE = 32


def _halved_by_rows(rows):
    return rows % HALF_ROWS_MULTIPLE == 0


def _half_shape(rows, cols):
    return (rows // 2, cols) if _halved_by_rows(rows) else (rows, cols // 2)


def _half_index(rows, core):
    return (core, 0) if _halved_by_rows(rows) else (0, core)


def _half_of(ref, rows, cols, core, *lead):
    if _halved_by_rows(rows):
        return ref.at[(*lead, pl.ds(core * (rows // 2), rows // 2), slice(None))]
    return ref.at[(*lead, slice(None), pl.ds(core * (cols // 2), cols // 2))]


def _into_slot(shard, chip, dtype, name, deps=()):
    rows, cols = shard.shape
    half = _half_shape(rows, cols)
    by_rows = _halved_by_rows(rows)
    deps = tuple(deps)

    def body(k_ref, *refs):
        s_ref, o_ref = refs[len(deps):]
        o_ref[0] = s_ref[...].astype(dtype)

    return pl.pallas_call(
        body, name=name,
        grid_spec=pltpu.PrefetchScalarGridSpec(
            num_scalar_prefetch=1, grid=(2,),
            in_specs=[_ANY] * len(deps) + [pl.BlockSpec(half, lambda i, k_ref: (i, 0) if by_rows else (0, i))],
            out_specs=pl.BlockSpec((1,) + half, lambda i, k_ref: (k_ref[0], i, 0) if by_rows else (k_ref[0], 0, i))),
        out_shape=jax.ShapeDtypeStruct((N_CHIPS, rows, cols), dtype),
        compiler_params=_params(dimension_semantics=("arbitrary",)),
    )(chip, *deps, shard)


def _run_copies(name, bufs, n_copies, plan):
    n = len(bufs)

    def body(*refs):
        copies = plan(refs[n:2 * n], refs[2 * n:2 * n + 2])
        for send, _ in copies:
            send.start()
        for send, recv in copies:
            send.wait_send()
            recv.wait_recv()

    return _comm_call(body, name, bufs, [_hbm_out(b.shape, b.dtype) for b in bufs], n_copies, in_place=True)


def _forward_halves(slots, name):
    return _run_copies(name, slots, 3 * len(slots), _d2d_forward_plan(slots))


_HBM = pl.BlockSpec(memory_space=pltpu.HBM)
_SEM = pl.BlockSpec(memory_space=pltpu.SEMAPHORE)
_DATAFLOW = pltpu.SideEffectType.DATAFLOW_SIDE_EFFECTING


def _split_copy_start(name, bufs, n_copies, plan):
    n = len(bufs)

    def body(*refs):
        for send, _ in plan(refs[:n], (refs[n], refs[n + 1])):
            send.start()
        token = refs[-1]
        token[...] = jnp.zeros_like(token)

    out = pl.pallas_call(
        body, name=name,
        out_shape=(pltpu.SemaphoreType.DMA((n_copies,)), pltpu.SemaphoreType.DMA((n_copies,)),
                   *[pltpu.HBM(b.shape, b.dtype) for b in bufs], jax.ShapeDtypeStruct((8, LANES), F32)),
        in_specs=[_HBM] * n, out_specs=(_SEM, _SEM, *[_HBM] * n, pl.BlockSpec(memory_space=pltpu.VMEM)),
        input_output_aliases={i: 2 + i for i in range(n)},
        compiler_params=pltpu.CompilerParams(has_side_effects=_DATAFLOW),
    )(*[pltpu.with_memory_space_constraint(b, pltpu.HBM) for b in bufs])
    return out[0], out[1], list(out[2:2 + n]), out[-1]


def _split_copy_wait(name, started, plan, after, passed=()):
    send_sems, recv_sems, bufs, _ = started
    n = len(bufs)
    after = tuple(after)
    bufs = list(bufs) + list(passed)
    total = len(bufs)

    def body(*refs):
        for send, recv in plan(refs[:n], (refs[total], refs[total + 1])):
            send.wait_send()
            recv.wait_recv()

    out = pl.pallas_call(
        body, name=name, out_shape=tuple(pltpu.HBM(b.shape, b.dtype) for b in bufs),
        in_specs=[_HBM] * total + [_SEM, _SEM] + [_ANY] * len(after), out_specs=tuple([_HBM] * total),
        input_output_aliases={i: i for i in range(total)},
        compiler_params=pltpu.CompilerParams(has_side_effects=_DATAFLOW),
    )(*bufs, send_sems, recv_sems, *after)
    return list(out)


def _ici_gather_plan(slots):
    def plan(refs, sems):
        x, y, c, chips = _place()
        me = 2 * x + y
        copies = []
        for i, ref in enumerate(refs):
            for j, chip in enumerate(chips):
                mine = _half_of(ref, *slots[i].shape[1:], c, me)
                theirs = _half_of(ref, *slots[i].shape[1:], c, 2 * chip[0] + chip[1])
                to = (*chip, c)
                copies.append((_remote(mine, mine, sems, 3 * i + j, to), _remote(theirs, theirs, sems, 3 * i + j, to)))
        return copies

    return plan


def _ici_scatter_plan(n):
    def plan(refs, sems):
        x, y, c, chips = _place()
        copies = []
        for i in range(n):
            for j, chip in enumerate(chips):
                cp = _remote(refs[i].at[2 * chip[0] + chip[1]], refs[n + i].at[j], sems, 3 * i + j, (*chip, c))
                copies.append((cp, cp))
        return copies

    return plan


def _d2d_forward_plan(slots):
    def plan(refs, sems):
        x, y, c, chips = _place()
        sibling = (x, y, 1 - c)
        copies = []
        for i, ref in enumerate(refs):
            for j, chip in enumerate(chips):
                src_chip = 2 * chip[0] + chip[1]
                mine = _half_of(ref, *slots[i].shape[1:], c, src_chip)
                theirs = _half_of(ref, *slots[i].shape[1:], 1 - c, src_chip)
                copies.append((_remote(mine, mine, sems, 3 * i + j, sibling),
                               _remote(theirs, theirs, sems, 3 * i + j, sibling)))
        return copies

    return plan


def _pair_exchange_plan(grads):
    n = len(grads)

    def plan(refs, sems):
        x, y, c, _ = _place()
        copies = []
        for i in range(n):
            theirs = _half_of(refs[i], *grads[i].shape[1:], 1 - c, slice(None))
            cp = _remote(theirs, refs[n + i], sems, i, (x, y, 1 - c))
            copies.append((cp, cp))
        return copies

    return plan


def _pair_share_plan(shapes):
    def plan(refs, sems):
        x, y, c, _ = _place()
        sibling = (x, y, 1 - c)
        copies = []
        for i, ref in enumerate(refs):
            mine, theirs = _half_of(ref, *shapes[i], c), _half_of(ref, *shapes[i], 1 - c)
            copies.append((_remote(mine, mine, sems, i, sibling), _remote(theirs, theirs, sems, i, sibling)))
        return copies

    return plan


def _pair_share(halves, name):
    return _run_copies(name, halves, len(halves), _pair_share_plan([h.shape for h in halves]))


N_DEVICES = 8
FLIPS = [(fx, fy, fc) for fx in range(2) for fy in range(2) for fc in range(2)][1:]


def _small_slots(v, me):
    rows = v.shape[0]

    def body(k_ref, v_ref, o_ref):
        o_ref[0] = v_ref[...]

    return pl.pallas_call(
        body, name="small_slot",
        grid_spec=pltpu.PrefetchScalarGridSpec(
            num_scalar_prefetch=1, grid=(1,),
            in_specs=[pl.BlockSpec((rows, LANES), lambda i, k_ref: (0, 0))],
            out_specs=pl.BlockSpec((1, rows, LANES), lambda i, k_ref: (k_ref[0], 0, 0))),
        out_shape=jax.ShapeDtypeStruct((N_DEVICES, rows, LANES), F32),
        compiler_params=_params(dimension_semantics=("arbitrary",)),
    )(me, v)


def _small_plan():
    def plan(refs, sems):
        x, y, c, _ = _place()
        slots = refs[0]
        me = 4 * x + 2 * y + c
        copies = []
        for n, (fx, fy, fc) in enumerate(FLIPS):
            to = (x ^ fx, y ^ fy, c ^ fc)
            src = 4 * to[0] + 2 * to[1] + to[2]
            copies.append((_remote(slots.at[me], slots.at[me], sems, n, to), _remote(slots.at[src], slots.at[src], sems, n, to)))
        return copies

    return plan


def _small_sum(slots):
    def body(s_ref, o_ref):
        acc = s_ref[0]
        for s in range(1, N_DEVICES):
            acc = acc + s_ref[s]
        o_ref[...] = acc

    return pl.pallas_call(body, name="small_sum", out_shape=jax.ShapeDtypeStruct(slots.shape[1:], F32),
                          compiler_params=_params())(slots)


def _pair_add(gs, sibs, core, name):
    n = len(gs)
    halves = [_half_shape(*g.shape[1:]) for g in gs]

    def body(c_ref, *refs):
        for g_ref, s_ref, o_ref in zip(refs[:n], refs[n:2 * n], refs[2 * n:]):
            o_ref[0] = (g_ref[0].astype(F32) + s_ref[0].astype(F32)).astype(BF16)

    def mine(g, half):
        return pl.BlockSpec((1,) + half, lambda s, c_ref: (s, *_half_index(g.shape[1], c_ref[0])))

    whole = [pl.BlockSpec((1,) + half, lambda s, c_ref: (s, 0, 0)) for half in halves]
    return pl.pallas_call(
        body, name=name,
        grid_spec=pltpu.PrefetchScalarGridSpec(
            num_scalar_prefetch=1, grid=(N_CHIPS,),
            in_specs=[mine(g, half) for g, half in zip(gs, halves)] + whole, out_specs=whole),
        out_shape=[jax.ShapeDtypeStruct((N_CHIPS,) + half, BF16) for half in halves],
        compiler_params=_params(dimension_semantics=("arbitrary",)),
    )(core, *gs, *sibs)


CHIP_ADD_STEPS = 2


def _chip_add(parts, recvs, chip_core, shapes, name):
    n = len(parts)
    by_rows = [_halved_by_rows(shape[0]) for shape in shapes]
    pieces = [(h[0] // CHIP_ADD_STEPS, h[1]) if rows else (h[0], h[1] // CHIP_ADD_STEPS)
              for h, rows in zip((_half_shape(*shape) for shape in shapes), by_rows)]

    def body(kc_ref, *refs):
        for p_ref, r_ref, o_ref in zip(refs[:n], refs[n:2 * n], refs[2 * n:]):
            acc = p_ref[0].astype(F32)
            for j in range(N_CHIPS - 1):
                acc = acc + r_ref[j].astype(F32)
            o_ref[...] = acc

    def at(rows, lead, piece_of):
        return lambda s, kc_ref: (*lead(kc_ref), piece_of(s, kc_ref), 0) if rows else (*lead(kc_ref), 0, piece_of(s, kc_ref))

    mine = [pl.BlockSpec((1,) + p, at(rows, lambda kc_ref: (kc_ref[0],), lambda s, kc_ref: s)) for p, rows in zip(pieces, by_rows)]
    theirs = [pl.BlockSpec((N_CHIPS - 1,) + p, at(rows, lambda kc_ref: (0,), lambda s, kc_ref: s)) for p, rows in zip(pieces, by_rows)]
    out = [pl.BlockSpec(p, at(rows, lambda kc_ref: (), lambda s, kc_ref: kc_ref[1] * CHIP_ADD_STEPS + s))
           for p, rows in zip(pieces, by_rows)]
    return pl.pallas_call(
        body, name=name,
        grid_spec=pltpu.PrefetchScalarGridSpec(
            num_scalar_prefetch=1, grid=(CHIP_ADD_STEPS,), in_specs=mine + theirs, out_specs=out),
        out_shape=[jax.ShapeDtypeStruct(tuple(shape), F32) for shape in shapes],
        compiler_params=_params(dimension_semantics=("arbitrary",)),
    )(chip_core, *parts, *recvs)


def _adamw_math(w, g, m, v):
    m = ADAM_B1 * m + (1.0 - ADAM_B1) * g
    v = ADAM_B2 * v + (1.0 - ADAM_B2) * (g * g)
    m_hat = m / (1.0 - ADAM_B1 ** ADAM_STEP)
    v_hat = v / (1.0 - ADAM_B2 ** ADAM_STEP)
    delta = -ADAM_LR * (m_hat / (jnp.sqrt(v_hat) + ADAM_EPS) + ADAM_WD * w)
    return delta, m, v


ADAM_PARTS = 2


def _adamw_matrix(w, g, m, v, name):
    rows, cols = w.shape
    by_rows = rows % (8 * ADAM_PARTS) == 0
    block = (rows // ADAM_PARTS, cols) if by_rows else (rows, cols // ADAM_PARTS)

    def body(w_ref, g_ref, m_ref, v_ref, go_ref, d_ref, mo_ref, vo_ref):
        gv = g_ref[...]
        go_ref[...] = gv
        d_ref[...], mo_ref[...], vo_ref[...] = _adamw_math(w_ref[...], gv, m_ref[...], v_ref[...])

    spec = pl.BlockSpec(block, lambda i: (i, 0) if by_rows else (0, i))
    shape = jax.ShapeDtypeStruct((rows, cols), F32)
    return pl.pallas_call(
        body, name=name, grid=(ADAM_PARTS,), in_specs=[spec] * 4, out_specs=[spec] * 4, out_shape=[shape] * 4,
        compiler_params=_params(dimension_semantics=("arbitrary",)),
    )(w, g, m, v)


def _adamw_small(ws, gs, ms, vs):
    n = len(ws)

    def body(*refs):
        for i in range(n):
            w_ref, g_ref, m_ref, v_ref = (refs[k * n + i] for k in range(4))
            d_ref, mo_ref, vo_ref = (refs[(4 + k) * n + i] for k in range(3))
            d_ref[...], mo_ref[...], vo_ref[...] = _adamw_math(w_ref[...], g_ref[...], m_ref[...], v_ref[...])

    shapes = [jax.ShapeDtypeStruct(w.shape, F32) for w in ws]
    out = pl.pallas_call(body, name="adamw_small", out_shape=shapes * 3, compiler_params=_params())(*ws, *gs, *ms, *vs)
    return out[:n], out[n:2 * n], out[2 * n:]


MATRICES = ["ffn1_w13", "ffn1_w2", "w_in", "w_out", "ffn2_w13", "ffn2_w2"]
VECTORS = ["ffn1_norm", "mix_norm", "conv_b", "conv_ln_g", "conv_ln_b", "forget_b", "out_norm_conv",
           "out_norm_attn", "ffn2_norm", "final_norm"]
WEIGHTS = ["ffn1_norm", "ffn1_w13", "ffn1_w2", "mix_norm", "w_in", "conv_w", "conv_b", "conv_ln_g", "conv_ln_b",
           "forget_b", "out_norm_conv", "out_norm_attn", "w_out", "ffn2_norm", "ffn2_w13", "ffn2_w2", "final_norm"]


def _pack_small(g, names):
    rows, layout = [], []
    for n in names:
        flat = g[n].reshape(-1)
        pad = (-flat.shape[0]) % LANES
        rows.append(jnp.pad(flat, (0, pad)).reshape(-1, LANES))
        layout.append((n, g[n].shape, flat.shape[0], rows[-1].shape[0]))
    packed = jnp.concatenate(rows, axis=0)
    pad_rows = (-packed.shape[0]) % 8
    return jnp.pad(packed, ((0, pad_rows), (0, 0))), layout


def _unpack_small(packed, layout):
    out, r = {}, 0
    for n, shape, size, nrows in layout:
        out[n] = packed[r:r + nrows].reshape(-1)[:size].reshape(shape)
        r += nrows
    return out


def kernel(x, ffn1_norm, ffn1_w13, ffn1_w2, mix_norm, w_in, conv_w, conv_b, conv_ln_g, conv_ln_b, forget_b, out_norm_conv, out_norm_attn, w_out, ffn2_norm, ffn2_w13, ffn2_w2, final_norm, loss_target, m_ffn1_norm, m_ffn1_w13, m_ffn1_w2, m_mix_norm, m_w_in, m_conv_w, m_conv_b, m_conv_ln_g, m_conv_ln_b, m_forget_b, m_out_norm_conv, m_out_norm_attn, m_w_out, m_ffn2_norm, m_ffn2_w13, m_ffn2_w2, m_final_norm, v_ffn1_norm, v_ffn1_w13, v_ffn1_w2, v_mix_norm, v_w_in, v_conv_w, v_conv_b, v_conv_ln_g, v_conv_ln_b, v_forget_b, v_out_norm_conv, v_out_norm_attn, v_w_out, v_ffn2_norm, v_ffn2_w13, v_ffn2_w2, v_final_norm):
    args = dict(locals())
    weights = {n: args[n] for n in WEIGHTS}
    core = lax.axis_index("c").astype(jnp.int32).reshape(1)
    chip = (2 * lax.axis_index("x") + lax.axis_index("y")).astype(jnp.int32)
    chip1 = chip.reshape(1)
    chip_core = jnp.concatenate([chip1, core])

    def held(n, a):
        return a[0].T if n == "w_in" else a[0]

    def given(n, a):
        return (a.T if n == "w_in" else a)[None]

    def slot(n, deps=()):
        if n == "conv_w":
            rows = jnp.pad(conv_w[0], ((0, CONV_PAD - CONV_WIDTH), (0, 0)))
            return _into_slot(rows, chip1, F32, "slot_conv_w", deps)
        return _into_slot(held(n, weights[n]), chip1, BF16, "slot_" + n, deps)

    fetched = {"ffn1_w13": ["ffn1_w13"], "ffn1_w2": ["ffn1_w2"], "mix": ["w_in", "w_out", "conv_w"],
               "ffn2": ["ffn2_w13", "ffn2_w2"]}
    fetch = {}

    def as_weights(group, bufs):
        out = {}
        for n, b in zip(fetched[group], bufs):
            if n.endswith("w13"):
                out[n] = b
            elif n != "conv_w":
                out[n] = b.reshape(N_CHIPS * b.shape[1], b.shape[2])
            else:
                out[n] = b[:, :CONV_WIDTH].transpose(1, 0, 2).reshape(CONV_WIDTH, D_CONV)
        return out

    def get_weights(group, after):
        if group == "ffn1_w13":
            first = [slot("ffn1_w13")]
            plan = _ici_gather_plan(first)
            started = _split_copy_start("gather_ffn1_w13_start", first, 3, plan)
            second = [slot("ffn1_w2", [started[3]])]
            plan2 = _ici_gather_plan(second)
            fetch["ffn1_w2"] = plan2, _split_copy_start("gather_ffn1_w2_start", second, 3, plan2)
            later_names = fetched["mix"] + fetched["ffn2"]
            later = [slot(n, [fetch["ffn1_w2"][1][3]]) for n in later_names]
            landed = _split_copy_wait("gather_ffn1_w13_wait", started, plan, [], passed=later)
            bufs = _forward_halves(landed[:1], "forward_ffn1_w13")
            behind = dict(zip(later_names, landed[1:]))
            for later in ("mix", "ffn2"):
                bufs_later = [behind[n] for n in fetched[later]]
                plan = _ici_gather_plan(bufs_later)
                fetch[later] = plan, _split_copy_start("gather_%s_start" % later, bufs_later, 3 * len(bufs_later), plan)
            return as_weights(group, bufs), [fetch["mix"][1][3], fetch["ffn2"][1][3]]
        plan, started = fetch[group.split(":")[0]]
        if group == "ffn2:landed":
            landed = _split_copy_wait("gather_ffn2_wait", started, plan, [after])
            plan = _d2d_forward_plan(landed)
            fetch["ffn2"] = plan, _split_copy_start("forward_ffn2_start", landed, 3 * len(landed), plan)
            return {}, [fetch["ffn2"][1][3]]
        if group == "ffn2":
            return as_weights(group, _split_copy_wait("forward_ffn2_wait", started, plan, [after])), []
        landed = _split_copy_wait("gather_%s_wait" % group, started, plan, [after])
        return as_weights(group, _forward_halves(landed, "forward_" + group)), []

    def shard_major(n, g):
        return g if n.endswith("w13") else g.reshape(N_CHIPS, g.shape[0] // N_CHIPS, g.shape[1])

    exchange, scatter = {}, {}
    small_names = VECTORS + ["conv_w"]
    small = {}

    def put_grads(group, grads):
        if group == "small":
            packed, layout = _pack_small(grads, small_names + ["loss"])
            me = (4 * lax.axis_index("x") + 2 * lax.axis_index("y") + lax.axis_index("c")).astype(jnp.int32).reshape(1)
            plan = _small_plan()
            exchange[group] = layout, plan, _split_copy_start("small_start", [_small_slots(packed, me)], len(FLIPS), plan)
            return [exchange[group][2][3]]
        names = list(grads)
        local = [shard_major(n, grads[n]) for n in names]
        landing = [lax.empty((N_CHIPS,) + _half_shape(*a.shape[1:]), BF16) for a in local]
        plan = _pair_exchange_plan(local)
        exchange[group] = names, plan, _split_copy_start("exchange_%s_start" % group, local + landing, len(local), plan)
        return [exchange[group][2][3]]

    def reduce_chips(group, after):
        names, plan, started, shapes = scatter[group]
        done = _split_copy_wait("scatter_%s_wait" % group, started, plan, after)
        parts, landed = done[:len(names)], done[len(names):]
        return list(_chip_add(parts, landed, chip_core, shapes, "chip_add_" + group))

    def share_start(group, halves):
        plan = _pair_share_plan(scatter[group][3])
        return plan, _split_copy_start("share_%s_start" % group, halves, len(halves), plan)

    early = {}

    def flush_grads(group, after):
        if group == "ffn1_w2":
            halves = reduce_chips("ffn2", after)
            early["ffn2"] = share_start("ffn2", halves)
            after = [early["ffn2"][1][3]]
        names, plan, started = exchange[group]
        done = _split_copy_wait("exchange_%s_wait" % group, started, plan, after)
        local, sib = done[:len(names)], done[len(names):]
        parts = list(_pair_add(local, sib, core, "pair_add_" + group))
        landing = [lax.empty((N_CHIPS - 1,) + q.shape[1:], BF16) for q in parts]
        plan = _ici_scatter_plan(len(parts))
        shapes = [a.shape[1:] for a in local]
        scatter[group] = names, plan, _split_copy_start("scatter_%s_start" % group, parts + landing, 3 * len(parts), plan), shapes
        return [scatter[group][2][3]]

    p = {n: weights[n] for n in VECTORS}
    p["final_norm"] = final_norm.reshape(1, D_MODEL)
    dx = _local_step(x[0], loss_target[0], p, get_weights, put_grads, flush_grads)
    layout, plan, started = exchange["small"]
    slots, = _split_copy_wait("small_wait", started, plan, [exchange["ffn1_w13"][2][3]])
    small.update(_unpack_small(_small_sum(slots), layout))
    loss = small["loss"].reshape(())

    grad = {n: small[n] for n in VECTORS}
    grad["final_norm"] = small["final_norm"].reshape(D_MODEL)
    grad["conv_w"] = lax.dynamic_slice_in_dim(small["conv_w"], chip * (D_CONV // N_CHIPS), D_CONV // N_CHIPS, axis=1)[None]

    delta, new_m, new_v = {}, {}, {}

    def update(group, full):
        ends = []
        for n, reduced in zip(scatter[group][0], full):
            go, d, mo, vo = _adamw_matrix(held(n, weights[n]), reduced, held(n, args["m_" + n]), held(n, args["v_" + n]),
                                          "adamw_" + n)
            grad[n], delta[n], new_m[n], new_v[n] = given(n, go), given(n, d), given(n, mo), given(n, vo)
            ends.append(vo)
        return ends

    plan_ffn2, share_ffn2 = early["ffn2"]
    halves_mix = reduce_chips("mix", [exchange["ffn1_w13"][2][3]])
    plan_mix, share_mix = share_start("mix", halves_mix)
    last_scatter = flush_grads("ffn1_w13", [share_mix[3]])
    done_ffn2 = update("ffn2", _split_copy_wait("share_ffn2_wait", share_ffn2, plan_ffn2, last_scatter))
    done_mix = update("mix", _split_copy_wait("share_mix_wait", share_mix, plan_mix, done_ffn2))
    as2d = lambda a: a.reshape(-1, a.shape[-1])
    ds, mos, vos = _adamw_small([as2d(weights[n]) for n in small_names], [as2d(grad[n]) for n in small_names],
                                [as2d(args["m_" + n]) for n in small_names], [as2d(args["v_" + n]) for n in small_names])
    for n, d, mo, vo in zip(small_names, ds, mos, vos):
        shape = weights[n].shape
        delta[n], new_m[n], new_v[n] = d.reshape(shape), mo.reshape(shape), vo.reshape(shape)
    behind = done_ffn2 + done_mix + [vos[0]]
    halves_w2 = reduce_chips("ffn1_w2", behind)
    halves_w13 = reduce_chips("ffn1_w13", behind)
    full_w2, full_w13 = _pair_share(halves_w2 + halves_w13, "pair_share_ffn1")
    update("ffn1_w2", [full_w2])
    update("ffn1_w13", [full_w13])

    return (loss, dx[None], *[grad[n] for n in WEIGHTS], *[delta[n] for n in WEIGHTS],
            *[new_m[n] for n in WEIGHTS], *[new_v[n] for n in WEIGHTS])
```

```python
import jax
import jax.numpy as jnp
from jax import lax
from jax.experimental import pallas as pl
from jax.experimental.pallas import tpu as pltpu

F32 = jnp.float32
BF16 = jnp.bfloat16

D_MODEL = 1024
D_FF = 2816
FF_SHARD = D_FF // 2
D_CONV = 512
D_ATTN = 512
N_HEADS = 8
HEAD_DIM = 64
CONV_WIDTH = 31
CONV_PAD = 32
N_IN = 2 * D_CONV + 3 * D_ATTN + N_HEADS
EPS = 1e-6
N_CHIPS = 4
LANES = 128
TOKEN_ROWS = 512
HEAD_ROWS = 16

ADAM_LR = 0.001
ADAM_B1 = 0.9
ADAM_B2 = 0.999
ADAM_EPS = 1e-08
ADAM_WD = 0.01
ADAM_STEP = 10

VMEM_LIMIT = 56 * 1024 * 1024

_NT = (((1,), (1,)), ((), ()))
_TN = (((0,), (0,)), ((), ()))


def _dot(a, b):
    return jnp.dot(a, b, preferred_element_type=F32)


def _dot_nt(a, b):
    return lax.dot_general(a, b, _NT, preferred_element_type=F32)


def _dot_tn(a, b):
    return lax.dot_general(a, b, _TN, preferred_element_type=F32)


def _params(**kw):
    return pltpu.CompilerParams(vmem_limit_bytes=VMEM_LIMIT, **kw)


def _sigmoid(x):
    return 1.0 / (1.0 + jnp.exp(-x))


def _rms_stats(x):
    return lax.rsqrt(jnp.mean(x * x, axis=-1, keepdims=True) + EPS)


def _rms_bwd(x, r, g, dh):
    t = dh * g
    dx = r * t - x * (r * r * r) * jnp.mean(t * x, axis=-1, keepdims=True)
    return dx, dh * x * r


def _silu_grad(z, sg):
    return sg * (1.0 + z * (1.0 - sg))


def _row_spec(tm, n):
    return pl.BlockSpec((tm, n), lambda i: (i, 0))


def _full_spec(shape):
    nd = len(shape)
    return pl.BlockSpec(shape, lambda i: (0,) * nd)


_ANY = pl.BlockSpec(memory_space=pl.ANY)


def _skip(n, body):
    return lambda *refs: body(*refs[n:])


FFN_ROWS = 256
FFN_WEIGHT_PARTS = N_CHIPS + 2


def _with_ffn_weights(w13_hbm, w2_hbm, w13_ref, w2_ref, sems, order, tile):
    first = pl.program_id(0) == 0
    copies = {}
    if w13_hbm is not None:
        for k in range(N_CHIPS):
            copies["w13", k] = pltpu.make_async_copy(w13_hbm.at[k], w13_ref.at[k], sems.at[k])
    if w2_hbm is not None:
        for half in range(2):
            rows = pl.ds(half * FF_SHARD, FF_SHARD)
            copies["w2", half] = pltpu.make_async_copy(w2_hbm.at[rows, :], w2_ref.at[rows, :], sems.at[N_CHIPS + half])

    @pl.when(first)
    def _():
        for part in order:
            copies[part].start()

        def ready(*parts):
            for part in parts:
                copies[part].wait()

        tile(ready)

    @pl.when(jnp.logical_not(first))
    def _():
        tile(lambda *parts: None)


def _ffn_fwd(x, g, w13s, w2, name, deps=(), head=None):
    t = x.shape[0]
    tm = FFN_ROWS
    deps = tuple(deps)
    n_head = 0 if head is None else 2

    def body(x_ref, g_ref, *refs):
        head_refs, (w13_hbm, w2_hbm, xo_ref, h_ref, gu_ref, a_ref) = refs[:n_head], refs[n_head:n_head + 6]
        tail_refs, (w13_ref, w2_ref, sems) = refs[n_head + 6:n_head + 6 + n_head], refs[-3:]
        if head is not None:
            @pl.when(pl.program_id(0) == 0)
            def _():
                for ref in tail_refs:
                    ref[...] = jnp.zeros_like(ref)

        def finish(xo):
            if head is None:
                xo_ref[...] = xo
                return
            (t_ref, fg_ref), (loss_ref, dfg_ref) = head_refs, tail_refs
            r = _rms_stats(xo)
            fg = fg_ref[...]
            err = xo * r * fg - t_ref[...]
            row = jnp.sum(err * err, axis=1, keepdims=True) * (0.5 / D_MODEL)
            loss_ref[...] += jnp.sum(row, axis=0, keepdims=True)
            dx, dg_rows = _rms_bwd(xo, r, fg, err * (1.0 / D_MODEL))
            xo_ref[...] = dx
            dfg_ref[...] += jnp.sum(dg_rows, axis=0, keepdims=True)

        def tile(ready):
            xv = x_ref[...]
            hb = (xv * _rms_stats(xv) * g_ref[...]).astype(BF16)
            h_ref[...] = hb
            acc = jnp.zeros((tm, D_MODEL), F32)
            for half in range(2):
                lo = half * FF_SHARD
                ready(("w13", half), ("w13", 2 + half))
                gate = _dot(hb, w13_ref[half])
                up = _dot(hb, w13_ref[2 + half])
                gu_ref[:, lo:lo + FF_SHARD] = gate.astype(BF16)
                gu_ref[:, D_FF + lo:D_FF + lo + FF_SHARD] = up.astype(BF16)
                a = (gate * _sigmoid(gate) * up).astype(BF16)
                a_ref[:, lo:lo + FF_SHARD] = a
                ready(("w2", half))
                acc = acc + _dot(a, w2_ref[lo:lo + FF_SHARD, :])
            finish(xv + 0.5 * acc)

        _with_ffn_weights(w13_hbm, w2_hbm, w13_ref, w2_ref, sems,
                          [("w13", 0), ("w13", 2), ("w2", 0), ("w13", 1), ("w13", 3), ("w2", 1)], tile)

    head_in = [] if head is None else [_row_spec(tm, D_MODEL), _full_spec((1, D_MODEL))]
    head_out = [] if head is None else [_full_spec((1, LANES)), _full_spec((1, D_MODEL))]
    head_shapes = [] if head is None else [jax.ShapeDtypeStruct((1, LANES), F32), jax.ShapeDtypeStruct((1, D_MODEL), F32)]
    return pl.pallas_call(
        _skip(len(deps), body), name=name, grid=(t // tm,),
        in_specs=[_ANY] * len(deps) + [_row_spec(tm, D_MODEL), _full_spec((1, D_MODEL))] + head_in + [_ANY, _ANY],
        out_specs=[_row_spec(tm, D_MODEL), _row_spec(tm, D_MODEL), _row_spec(tm, 2 * D_FF), _row_spec(tm, D_FF)] + head_out,
        out_shape=[jax.ShapeDtypeStruct((t, D_MODEL), F32), jax.ShapeDtypeStruct((t, D_MODEL), BF16),
                   jax.ShapeDtypeStruct((t, 2 * D_FF), BF16), jax.ShapeDtypeStruct((t, D_FF), BF16)] + head_shapes,
        scratch_shapes=[pltpu.VMEM(w13s.shape, BF16), pltpu.VMEM(w2.shape, BF16),
                        pltpu.SemaphoreType.DMA((FFN_WEIGHT_PARTS,))],
        compiler_params=_params(dimension_semantics=("arbitrary",)),
    )(*deps, x, g, *(head or ()), w13s, w2)


def _ffn_up(x, g, w13s, name, deps=()):
    t = x.shape[0]
    tm = FFN_ROWS
    deps = tuple(deps)

    def body(x_ref, g_ref, w13_hbm, h_ref, gu_ref, a_ref, w13_ref, sems):
        def tile(ready):
            xv = x_ref[...]
            hb = (xv * _rms_stats(xv) * g_ref[...]).astype(BF16)
            h_ref[...] = hb
            for half in range(2):
                lo = half * FF_SHARD
                ready(("w13", half), ("w13", 2 + half))
                gate = _dot(hb, w13_ref[half])
                up = _dot(hb, w13_ref[2 + half])
                gu_ref[:, lo:lo + FF_SHARD] = gate.astype(BF16)
                gu_ref[:, D_FF + lo:D_FF + lo + FF_SHARD] = up.astype(BF16)
                a_ref[:, lo:lo + FF_SHARD] = (gate * _sigmoid(gate) * up).astype(BF16)

        _with_ffn_weights(w13_hbm, None, w13_ref, None, sems, [("w13", 0), ("w13", 2), ("w13", 1), ("w13", 3)], tile)

    return pl.pallas_call(
        _skip(len(deps), body), name=name, grid=(t // tm,),
        in_specs=[_ANY] * len(deps) + [_row_spec(tm, D_MODEL), _full_spec((1, D_MODEL)), _ANY],
        out_specs=[_row_spec(tm, D_MODEL), _row_spec(tm, 2 * D_FF), _row_spec(tm, D_FF)],
        out_shape=[jax.ShapeDtypeStruct((t, D_MODEL), BF16), jax.ShapeDtypeStruct((t, 2 * D_FF), BF16),
                   jax.ShapeDtypeStruct((t, D_FF), BF16)],
        scratch_shapes=[pltpu.VMEM(w13s.shape, BF16), pltpu.SemaphoreType.DMA((FFN_WEIGHT_PARTS,))],
        compiler_params=_params(dimension_semantics=("arbitrary",)),
    )(*deps, x, g, w13s)


def _ffn_down(x, a, w2, name):
    t = x.shape[0]
    tm = FFN_ROWS

    def body(x_ref, a_ref, w2_hbm, xo_ref, w2_ref, sems):
        def tile(ready):
            ready(("w2", 0))
            acc = _dot(a_ref[:, 0:FF_SHARD], w2_ref[0:FF_SHARD, :])
            ready(("w2", 1))
            acc = acc + _dot(a_ref[:, FF_SHARD:], w2_ref[FF_SHARD:, :])
            xo_ref[...] = x_ref[...] + 0.5 * acc

        _with_ffn_weights(None, w2_hbm, None, w2_ref, sems, [("w2", 0), ("w2", 1)], tile)

    return pl.pallas_call(
        body, name=name, grid=(t // tm,),
        in_specs=[_row_spec(tm, D_MODEL), _row_spec(tm, D_FF), _ANY],
        out_specs=_row_spec(tm, D_MODEL), out_shape=jax.ShapeDtypeStruct((t, D_MODEL), F32),
        scratch_shapes=[pltpu.VMEM(w2.shape, BF16), pltpu.SemaphoreType.DMA((FFN_WEIGHT_PARTS,))],
        compiler_params=_params(dimension_semantics=("arbitrary",)),
    )(x, a, w2)


def _ffn_bwd(dy, x, gu, g, w13s, w2, name, deps=()):
    t = x.shape[0]
    tm = FFN_ROWS
    deps = tuple(deps)

    def body(dy_ref, x_ref, gu_ref, g_ref, w13_hbm, w2_hbm, dx_ref, dgu_ref, dg_ref, dyh_ref, dxb_ref,
             w13_ref, w2_ref, sems):
        @pl.when(pl.program_id(0) == 0)
        def _():
            dg_ref[...] = jnp.zeros_like(dg_ref)

        def tile(ready):
            dyv = dy_ref[...]
            dyh = (0.5 * dyv).astype(BF16)
            dyh_ref[...] = dyh
            dh = jnp.zeros((tm, D_MODEL), F32)
            for half in range(2):
                lo = half * FF_SHARD
                ready(("w2", half))
                da = _dot_nt(dyh, w2_ref[lo:lo + FF_SHARD, :])
                gate = gu_ref[:, lo:lo + FF_SHARD].astype(F32)
                up = gu_ref[:, D_FF + lo:D_FF + lo + FF_SHARD].astype(F32)
                sg = _sigmoid(gate)
                act = gate * sg
                dgate = (da * up * _silu_grad(gate, sg)).astype(BF16)
                dup = (da * act).astype(BF16)
                dgu_ref[:, lo:lo + FF_SHARD] = dgate
                dgu_ref[:, D_FF + lo:D_FF + lo + FF_SHARD] = dup
                ready(("w13", half), ("w13", 2 + half))
                dh = dh + _dot_nt(dgate, w13_ref[half]) + _dot_nt(dup, w13_ref[2 + half])
            xv = x_ref[...]
            dxn, dg_rows = _rms_bwd(xv, _rms_stats(xv), g_ref[...], dh)
            dx = dyv + dxn
            dx_ref[...] = dx
            dxb_ref[...] = dx.astype(BF16)
            dg_ref[...] += jnp.sum(dg_rows, axis=0, keepdims=True)

        _with_ffn_weights(w13_hbm, w2_hbm, w13_ref, w2_ref, sems,
                          [("w2", 0), ("w13", 0), ("w13", 2), ("w2", 1), ("w13", 1), ("w13", 3)], tile)

    return pl.pallas_call(
        _skip(len(deps), body), name=name, grid=(t // tm,),
        in_specs=[_ANY] * len(deps) + [_row_spec(tm, D_MODEL), _row_spec(tm, D_MODEL), _row_spec(tm, 2 * D_FF),
                                       _full_spec((1, D_MODEL)), _ANY, _ANY],
        out_specs=[_row_spec(tm, D_MODEL), _row_spec(tm, 2 * D_FF),
                   _full_spec((1, D_MODEL)), _row_spec(tm, D_MODEL), _row_spec(tm, D_MODEL)],
        out_shape=[jax.ShapeDtypeStruct((t, D_MODEL), F32), jax.ShapeDtypeStruct((t, 2 * D_FF), BF16),
                   jax.ShapeDtypeStruct((1, D_MODEL), F32),
                   jax.ShapeDtypeStruct((t, D_MODEL), BF16), jax.ShapeDtypeStruct((t, D_MODEL), BF16)],
        scratch_shapes=[pltpu.VMEM(w13s.shape, BF16), pltpu.VMEM(w2.shape, BF16),
                        pltpu.SemaphoreType.DMA((FFN_WEIGHT_PARTS,))],
        compiler_params=_params(dimension_semantics=("arbitrary",)),
    )(*deps, dy, x, gu, g, w13s, w2)


WGRAD_ROWS = (1408, 1024, 512, 384, 256)


def _wgrad(a, b, n_blocks, name, deps=()):
    t, m = a.shape
    tm = next(rows for rows in WGRAD_ROWS if m % rows == 0)
    n = b.shape[1]
    bn = n // n_blocks
    deps = tuple(deps)
    assert a.dtype == BF16 and b.dtype == BF16

    def body(a_ref, b_ref, o_ref):
        o_ref[0] = _dot_tn(a_ref[...], b_ref[...]).astype(BF16)

    return pl.pallas_call(
        _skip(len(deps), body), name=name, grid=(n_blocks, m // tm),
        in_specs=[_ANY] * len(deps) + [pl.BlockSpec((t, tm), lambda j, i: (0, i)),
                                       pl.BlockSpec((t, bn), lambda j, i: (0, j))],
        out_specs=pl.BlockSpec((1, tm, bn), lambda j, i: (j, i, 0)),
        out_shape=jax.ShapeDtypeStruct((n_blocks, m, bn), BF16),
        compiler_params=_params(dimension_semantics=("arbitrary", "arbitrary")),
    )(*deps, a, b)


def _mix_proj(x, g, w_ag, w_qkv, w_f):
    t = x.shape[0]
    tm = TOKEN_ROWS

    def body(x_ref, g_ref, wag_ref, wqkv_ref, wf_ref, h_ref, ag_ref, qkv_ref, fl_ref):
        xv = x_ref[...]
        hb = (xv * _rms_stats(xv) * g_ref[...]).astype(BF16)
        h_ref[...] = hb
        ag_ref[...] = _dot_nt(hb, wag_ref[...])
        qkv_ref[...] = _dot_nt(hb, wqkv_ref[...]).astype(BF16)
        fl_ref[...] = _dot_nt(hb, wf_ref[...])

    return pl.pallas_call(
        body, name="mix_proj", grid=(t // tm,),
        in_specs=[_row_spec(tm, D_MODEL), _full_spec((1, D_MODEL)), _full_spec(w_ag.shape),
                  _full_spec(w_qkv.shape), _full_spec(w_f.shape)],
        out_specs=[_row_spec(tm, D_MODEL), _row_spec(tm, 2 * D_CONV), _row_spec(tm, 3 * D_ATTN),
                   _row_spec(tm, LANES)],
        out_shape=[jax.ShapeDtypeStruct((t, D_MODEL), BF16), jax.ShapeDtypeStruct((t, 2 * D_CONV), F32),
                   jax.ShapeDtypeStruct((t, 3 * D_ATTN), BF16), jax.ShapeDtypeStruct((t, LANES), F32)],
        compiler_params=_params(dimension_semantics=("arbitrary",)),
    )(x, g, w_ag, w_qkv, w_f)


def _mix_proj_bwd(dproj, dx2, x1, g, w_ag, w_qkv, w_f):
    t = x1.shape[0]
    tm = TOKEN_ROWS
    n_ag, n_qkv = 2 * D_CONV, 3 * D_ATTN

    def body(dp_ref, dx2_ref, x_ref, g_ref, wag_ref, wqkv_ref, wf_ref, dx_ref, dg_ref):
        @pl.when(pl.program_id(0) == 0)
        def _():
            dg_ref[...] = jnp.zeros_like(dg_ref)

        dh = (_dot(dp_ref[:, 0:n_ag], wag_ref[...]) + _dot(dp_ref[:, n_ag:n_ag + n_qkv], wqkv_ref[...])
              + _dot(dp_ref[:, n_ag + n_qkv:], wf_ref[...]))
        xv = x_ref[...]
        dxn, dg_rows = _rms_bwd(xv, _rms_stats(xv), g_ref[...], dh)
        dx_ref[...] = dx2_ref[...] + dxn
        dg_ref[...] += jnp.sum(dg_rows, axis=0, keepdims=True)

    return pl.pallas_call(
        body, name="mix_proj_bwd", grid=(t // tm,),
        in_specs=[_row_spec(tm, dproj.shape[1]),
                  _row_spec(tm, D_MODEL), _row_spec(tm, D_MODEL), _full_spec((1, D_MODEL)),
                  _full_spec(w_ag.shape), _full_spec(w_qkv.shape), _full_spec(w_f.shape)],
        out_specs=[_row_spec(tm, D_MODEL), _full_spec((1, D_MODEL))],
        out_shape=[jax.ShapeDtypeStruct((t, D_MODEL), F32), jax.ShapeDtypeStruct((1, D_MODEL), F32)],
        compiler_params=_params(dimension_semantics=("arbitrary",)),
    )(dproj, dx2, x1, g, w_ag, w_qkv, w_f)


def _split3(x):
    hi = x.astype(BF16)
    r1 = x - hi.astype(F32)
    mid = r1.astype(BF16)
    lo = (r1 - mid.astype(F32)).astype(BF16)
    return hi, mid, lo


def _gates_fwd(flt, fb):
    t = flt.shape[1]

    def body(f_ref, b_ref, d_ref):
        z = f_ref[...] + b_ref[...]
        logf = jnp.minimum(z, 0.0) - jnp.log(1.0 + jnp.exp(-jnp.abs(z)))
        row = lax.broadcasted_iota(jnp.int32, (LANES, LANES), 0)
        col = lax.broadcasted_iota(jnp.int32, (LANES, LANES), 1)
        upper = (row <= col).astype(BF16)
        carry = jnp.zeros((HEAD_ROWS, 1), F32)
        for blk in range(t // LANES):
            hi, mid, lo = _split3(logf[:, blk * LANES:(blk + 1) * LANES])
            cs = _dot(hi, upper) + _dot(mid, upper) + _dot(lo, upper)
            d_ref[:, blk * LANES:(blk + 1) * LANES] = cs + carry
            carry = carry + cs[:, LANES - 1:LANES]

    return pl.pallas_call(
        body, name="gates_fwd", out_shape=jax.ShapeDtypeStruct((HEAD_ROWS, t), F32),
        compiler_params=_params(),
    )(flt, fb)


def _gates_bwd(dd, flt, fb):
    t = flt.shape[1]

    def body(dd_ref, f_ref, b_ref, df_ref, db_ref):
        z = f_ref[...] + b_ref[...]
        row = lax.broadcasted_iota(jnp.int32, (LANES, LANES), 0)
        col = lax.broadcasted_iota(jnp.int32, (LANES, LANES), 1)
        lower = (row >= col).astype(BF16)
        carry = jnp.zeros((HEAD_ROWS, 1), F32)
        db = jnp.zeros((HEAD_ROWS, 1), F32)
        for blk in reversed(range(t // LANES)):
            sl = slice(blk * LANES, (blk + 1) * LANES)
            hi, mid, lo = _split3(dd_ref[:, sl])
            cs = _dot(hi, lower) + _dot(mid, lower) + _dot(lo, lower)
            dz = (cs + carry) * _sigmoid(-z[:, sl])
            df_ref[:, sl] = dz
            db = db + jnp.sum(dz, axis=1, keepdims=True)
            carry = carry + cs[:, 0:1]
        db_ref[...] = db

    return pl.pallas_call(
        body, name="gates_bwd",
        out_shape=[jax.ShapeDtypeStruct((HEAD_ROWS, t), F32), jax.ShapeDtypeStruct((HEAD_ROWS, 1), F32)],
        compiler_params=_params(),
    )(dd, flt, fb)


CONV_CHUNK = 128
CONV_TAIL = 16
CONV_WINDOW = CONV_CHUNK + CONV_PAD + 8
CONV_ROWS_EXTRA = CONV_PAD + CONV_TAIL
SUBLANES = 8


def _conv_rows(ag_ref, u_ref, t):
    u_ref[0:CONV_PAD, :] = jnp.zeros((CONV_PAD, D_CONV), F32)
    u_ref[CONV_PAD + t:CONV_ROWS_EXTRA + t, :] = jnp.zeros((CONV_TAIL, D_CONV), F32)

    def fill(i, c):
        r0 = pl.multiple_of(i * CONV_CHUNK, CONV_CHUNK)
        a = ag_ref[pl.ds(r0, CONV_CHUNK), 0:D_CONV]
        gt = ag_ref[pl.ds(r0, CONV_CHUNK), D_CONV:2 * D_CONV]
        u_ref[pl.ds(CONV_PAD + r0, CONV_CHUNK), :] = a * _sigmoid(gt)
        return c

    lax.fori_loop(0, t // CONV_CHUNK, fill, 0)


def _for_shifted(ref, r0, offsets, fn):
    window = ref[pl.ds(r0, CONV_WINDOW), :]
    for rem in range(SUBLANES):
        mine = [o for o in offsets if o % SUBLANES == rem]
        if not mine:
            continue
        turned = window if rem == 0 else pltpu.roll(window, CONV_WINDOW - rem, 0)
        for o in mine:
            fn(o, turned[o - rem:o - rem + CONV_CHUNK])


def _conv_taps(u_ref, r0, w_ref, cb):
    acc = [jnp.zeros((CONV_CHUNK, D_CONV), F32)]

    def tap(o, rows):
        j = o - (CONV_PAD - CONV_WIDTH + 1)
        acc[0] = acc[0] + w_ref[j:j + 1, :] * rows

    _for_shifted(u_ref, r0, [j + CONV_PAD - CONV_WIDTH + 1 for j in range(CONV_WIDTH)], tap)
    return acc[0] + cb


def _conv_point(y, lg, lb):
    mu = jnp.mean(y, axis=-1, keepdims=True)
    yc = y - mu
    rstd = lax.rsqrt(jnp.mean(yc * yc, axis=-1, keepdims=True) + EPS)
    yhat = yc * rstd
    z = yhat * lg + lb
    sg = _sigmoid(z)
    s = z * sg
    rr = _rms_stats(s)
    return yhat, rstd, z, sg, s, rr


def _conv_fwd(ag, conv_w, conv_b, ln_g, ln_b, norm_g):
    t = ag.shape[0]

    def body(ag_ref, w_ref, cb_ref, lg_ref, lb_ref, ng_ref, o_ref, y_ref, u_ref):
        _conv_rows(ag_ref, u_ref, t)
        cb, lg, lb, ng = cb_ref[...], lg_ref[...], lb_ref[...], ng_ref[...]

        def chunk(i, c):
            r0 = pl.multiple_of(i * CONV_CHUNK, CONV_CHUNK)
            y = _conv_taps(u_ref, r0, w_ref, cb)
            y_ref[pl.ds(r0, CONV_CHUNK), :] = y
            _, _, _, _, s, rr = _conv_point(y, lg, lb)
            o_ref[pl.ds(r0, CONV_CHUNK), :] = (s * rr * ng).astype(BF16)
            return c

        lax.fori_loop(0, t // CONV_CHUNK, chunk, 0)

    return pl.pallas_call(
        body, name="conv_fwd",
        out_shape=[jax.ShapeDtypeStruct((t, D_CONV), BF16), jax.ShapeDtypeStruct((t, D_CONV), F32)],
        scratch_shapes=[pltpu.VMEM((t + CONV_ROWS_EXTRA, D_CONV), F32)],
        compiler_params=_params(),
    )(ag, conv_w, conv_b, ln_g, ln_b, norm_g)


def _conv_bwd(ag, y, dout, conv_w, ln_g, ln_b, norm_g):
    t = ag.shape[0]

    def body(ag_ref, y_ref, do_ref, w_ref, lg_ref, lb_ref, ng_ref,
             dag_ref, dw_ref, dcb_ref, dlg_ref, dlb_ref, dng_ref, u_ref, dy_ref):
        _conv_rows(ag_ref, u_ref, t)
        dy_ref[t:t + CONV_ROWS_EXTRA, :] = jnp.zeros((CONV_ROWS_EXTRA, D_CONV), F32)
        lg, lb, ng = lg_ref[...], lb_ref[...], ng_ref[...]
        dw_ref[...] = jnp.zeros_like(dw_ref)
        zero = jnp.zeros((1, D_CONV), F32)

        def chunk(i, carry):
            dcb, dlg, dlb, dng = carry
            r0 = pl.multiple_of(i * CONV_CHUNK, CONV_CHUNK)
            yhat, rstd, z, sg, s, rr = _conv_point(y_ref[pl.ds(r0, CONV_CHUNK), :], lg, lb)
            do = do_ref[pl.ds(r0, CONV_CHUNK), :]
            ds, dng_rows = _rms_bwd(s, rr, ng, do)
            dz = ds * _silu_grad(z, sg)
            dyhat = dz * lg
            dy = rstd * (dyhat - jnp.mean(dyhat, axis=-1, keepdims=True)
                         - yhat * jnp.mean(dyhat * yhat, axis=-1, keepdims=True))
            dy_ref[pl.ds(r0, CONV_CHUNK), :] = dy
            def tap(o, rows):
                j = o - (CONV_PAD - CONV_WIDTH + 1)
                dw_ref[j:j + 1, :] += jnp.sum(dy * rows, axis=0, keepdims=True)

            _for_shifted(u_ref, r0, [j + CONV_PAD - CONV_WIDTH + 1 for j in range(CONV_WIDTH)], tap)
            return (dcb + jnp.sum(dy, axis=0, keepdims=True), dlg + jnp.sum(dz * yhat, axis=0, keepdims=True),
                    dlb + jnp.sum(dz, axis=0, keepdims=True), dng + jnp.sum(dng_rows, axis=0, keepdims=True))

        dcb, dlg, dlb, dng = lax.fori_loop(0, t // CONV_CHUNK, chunk, (zero, zero, zero, zero))
        dcb_ref[...] = dcb
        dlg_ref[...] = dlg
        dlb_ref[...] = dlb
        dng_ref[...] = dng

        def chunk2(i, c):
            r0 = pl.multiple_of(i * CONV_CHUNK, CONV_CHUNK)
            acc = [jnp.zeros((CONV_CHUNK, D_CONV), F32)]

            def tap(o, rows):
                j = CONV_WIDTH - 1 - o
                acc[0] = acc[0] + w_ref[j:j + 1, :] * rows

            _for_shifted(dy_ref, r0, list(range(CONV_WIDTH)), tap)
            du = acc[0]
            a = ag_ref[pl.ds(r0, CONV_CHUNK), 0:D_CONV]
            gt = ag_ref[pl.ds(r0, CONV_CHUNK), D_CONV:2 * D_CONV]
            sg = _sigmoid(gt)
            dag_ref[pl.ds(r0, CONV_CHUNK), 0:D_CONV] = (du * sg).astype(BF16)
            dag_ref[pl.ds(r0, CONV_CHUNK), D_CONV:2 * D_CONV] = (du * a * sg * (1.0 - sg)).astype(BF16)
            return c

        lax.fori_loop(0, t // CONV_CHUNK, chunk2, 0)

    vec = jax.ShapeDtypeStruct((1, D_CONV), F32)
    return pl.pallas_call(
        body, name="conv_bwd",
        out_shape=[jax.ShapeDtypeStruct((t, 2 * D_CONV), BF16), jax.ShapeDtypeStruct((CONV_PAD, D_CONV), F32),
                   vec, vec, vec, vec],
        scratch_shapes=[pltpu.VMEM((t + CONV_ROWS_EXTRA, D_CONV), F32), pltpu.VMEM((t + CONV_ROWS_EXTRA, D_CONV), F32)],
        compiler_params=_params(),
    )(ag, y, dout, conv_w, ln_g, ln_b, norm_g)


Q_ROWS = 256
ATTN_SCALE = HEAD_DIM ** -0.5
ATTN_AHEAD = 1


def _attn_specs(t):
    blk = lambda off: pl.BlockSpec((t, LANES), lambda p: (0, off + p))
    pairs = N_HEADS // 2
    return [blk(0), blk(pairs), blk(2 * pairs), pl.BlockSpec((2, 1, t), lambda p: (p, 0, 0))]


def _one_head(q2, mask):
    return jnp.where(mask, q2, jnp.zeros_like(q2)) * ATTN_SCALE


def _attn_scores(qs, k2, drow, r0, q1):
    s = _dot_nt(qs, k2) - drow
    rowi = lax.broadcasted_iota(jnp.int32, (q1 - r0, q1 - r0), 0)
    coli = lax.broadcasted_iota(jnp.int32, (q1 - r0, q1 - r0), 1)
    diag = jnp.where(coli <= rowi, s[:, r0:q1], -jnp.inf)
    return diag if r0 == 0 else jnp.concatenate([s[:, :r0], diag], axis=1)


def _attn_fwd(qkv, drow, deps=()):
    t = qkv.shape[0]
    deps = tuple(deps)

    def body(q_ref, k_ref, v_ref, dr_ref, o_ref, lse_ref):
        head_a = lax.broadcasted_iota(jnp.int32, (1, LANES), 1) < HEAD_DIM
        items = [(qb, hh) for qb in range(t // Q_ROWS) for hh in range(2)]

        def scores(item):
            qb, hh = item
            r0, q1 = qb * Q_ROWS, (qb + 1) * Q_ROWS
            qs = _one_head(q_ref[r0:q1, :], head_a if hh == 0 else ~head_a)
            return _attn_scores(qs, k_ref[0:q1, :], dr_ref[hh, :, 0:q1], r0, q1)

        ahead = [scores(item) for item in items[:ATTN_AHEAD]]
        outs = []
        for n, (qb, hh) in enumerate(items):
            r0, q1 = qb * Q_ROWS, (qb + 1) * Q_ROWS
            s = ahead.pop(0)
            if n + ATTN_AHEAD < len(items):
                ahead.append(scores(items[n + ATTN_AHEAD]))
            mx = jnp.max(s, axis=1, keepdims=True)
            p = jnp.exp(s - mx)
            l = jnp.sum(p, axis=1, keepdims=True)
            lse_ref[hh, r0:q1, :] = mx + jnp.log(l)
            outs.append(_dot(p.astype(BF16), v_ref[0:q1, :]) * (1.0 / l))
            if hh == 1:
                o_ref[r0:q1, :] = jnp.where(head_a, outs[0], outs[1])
                outs = []

    pairs = N_HEADS // 2
    return pl.pallas_call(
        _skip(len(deps), body), name="attn_fwd", grid=(pairs,), in_specs=[_ANY] * len(deps) + _attn_specs(t),
        out_specs=[pl.BlockSpec((t, LANES), lambda p: (0, p)), pl.BlockSpec((2, t, 1), lambda p: (p, 0, 0))],
        out_shape=[jax.ShapeDtypeStruct((t, D_ATTN), F32), jax.ShapeDtypeStruct((N_HEADS, t, 1), F32)],
        compiler_params=_params(dimension_semantics=("arbitrary",)),
    )(*deps, qkv, qkv, qkv, drow)


def _attn_bwd(qkv, drow, lse, do):
    t = qkv.shape[0]

    def body(q_ref, k_ref, v_ref, dr_ref, lse_ref, do_ref,
             dq_ref, dk_ref, dv_ref, dd_ref, dk_acc, dv_acc):
        head_a = lax.broadcasted_iota(jnp.int32, (1, LANES), 1) < HEAD_DIM
        dk_acc[...] = jnp.zeros_like(dk_acc)
        dv_acc[...] = jnp.zeros_like(dv_acc)
        dd_ref[...] = jnp.zeros_like(dd_ref)
        items = [(qb, hh) for qb in range(t // Q_ROWS) for hh in range(2)]

        def products(item):
            qb, hh = item
            r0, q1 = qb * Q_ROWS, (qb + 1) * Q_ROWS
            mask = head_a if hh == 0 else ~head_a
            qs = _one_head(q_ref[r0:q1, :], mask)
            dob = jnp.where(mask, do_ref[r0:q1, :], 0.0).astype(BF16)
            s = _attn_scores(qs, k_ref[0:q1, :], dr_ref[hh, :, 0:q1], r0, q1)
            return qs, dob, s, _dot_nt(dob, v_ref[0:q1, :])

        ahead = products(items[0])
        dqs = []
        for n, (qb, hh) in enumerate(items):
            r0, q1 = qb * Q_ROWS, (qb + 1) * Q_ROWS
            qs, dob, s, dp = ahead
            if n + 1 < len(items):
                ahead = products(items[n + 1])
            p = jnp.exp(s - lse_ref[hh, r0:q1, :])
            ds = p * (dp - jnp.sum(p * dp, axis=1, keepdims=True))
            dsb = ds.astype(BF16)
            dqs.append(_dot(dsb, k_ref[0:q1, :]) * ATTN_SCALE)
            dk_acc[0:q1, :] += _dot_tn(dsb, qs)
            dv_acc[0:q1, :] += _dot_tn(p.astype(BF16), dob)
            dd_ref[hh, :, 0:q1] -= jnp.sum(ds, axis=0, keepdims=True)
            if hh == 1:
                dq_ref[r0:q1, :] = jnp.where(head_a, dqs[0], dqs[1]).astype(BF16)
                dqs = []
        dk_ref[...] = dk_acc[...].astype(BF16)
        dv_ref[...] = dv_acc[...].astype(BF16)

    pairs = N_HEADS // 2
    col = pl.BlockSpec((t, LANES), lambda p: (0, p))
    grad = jax.ShapeDtypeStruct((t, D_ATTN), BF16)
    return pl.pallas_call(
        body, name="attn_bwd", grid=(pairs,),
        in_specs=_attn_specs(t) + [pl.BlockSpec((2, t, 1), lambda p: (p, 0, 0)), col],
        out_specs=[col, col, col, pl.BlockSpec((2, 1, t), lambda p: (p, 0, 0))],
        out_shape=[grad, grad, grad, jax.ShapeDtypeStruct((N_HEADS, 1, t), F32)],
        scratch_shapes=[pltpu.VMEM((t, LANES), F32), pltpu.VMEM((t, LANES), F32)],
        compiler_params=_params(dimension_semantics=("arbitrary",)),
    )(qkv, qkv, qkv, drow, lse, do)


def _out_proj(ycn, o, g_attn, w_out, x1, deps=()):
    t = x1.shape[0]
    tm = TOKEN_ROWS
    deps = tuple(deps)

    def body(yc_ref, o_ref, g_ref, w_ref, x_ref, xo_ref, ya_ref):
        ov = o_ref[...]
        ya = (ov * _rms_stats(ov) * g_ref[...]).astype(BF16)
        ya_ref[...] = ya
        xo_ref[...] = x_ref[...] + _dot(yc_ref[...], w_ref[0:D_CONV, :]) + _dot(ya, w_ref[D_CONV:, :])

    return pl.pallas_call(
        _skip(len(deps), body), name="out_proj", grid=(t // tm,),
        in_specs=[_ANY] * len(deps) + [_row_spec(tm, D_CONV), _row_spec(tm, D_ATTN), _full_spec((1, D_ATTN)),
                                       _full_spec(w_out.shape), _row_spec(tm, D_MODEL)],
        out_specs=[_row_spec(tm, D_MODEL), _row_spec(tm, D_ATTN)],
        out_shape=[jax.ShapeDtypeStruct((t, D_MODEL), F32), jax.ShapeDtypeStruct((t, D_ATTN), BF16)],
        compiler_params=_params(dimension_semantics=("arbitrary",)),
    )(*deps, ycn, o, g_attn, w_out, x1)


def _out_proj_bwd(dx2, o, g_attn, w_out, deps=()):
    t = dx2.shape[0]
    tm = TOKEN_ROWS
    deps = tuple(deps)

    def body(dx_ref, o_ref, g_ref, w_ref, dyc_ref, do_ref, dg_ref):
        @pl.when(pl.program_id(0) == 0)
        def _():
            dg_ref[...] = jnp.zeros_like(dg_ref)

        dxb = dx_ref[...]
        dyc_ref[...] = _dot_nt(dxb, w_ref[0:D_CONV, :])
        dya = _dot_nt(dxb, w_ref[D_CONV:, :])
        ov = o_ref[...]
        do, dg_rows = _rms_bwd(ov, _rms_stats(ov), g_ref[...], dya)
        do_ref[...] = do
        dg_ref[...] += jnp.sum(dg_rows, axis=0, keepdims=True)

    return pl.pallas_call(
        _skip(len(deps), body), name="out_proj_bwd", grid=(t // tm,),
        in_specs=[_ANY] * len(deps) + [_row_spec(tm, D_MODEL), _row_spec(tm, D_ATTN), _full_spec((1, D_ATTN)),
                                       _full_spec(w_out.shape)],
        out_specs=[_row_spec(tm, D_CONV), _row_spec(tm, D_ATTN), _full_spec((1, D_ATTN))],
        out_shape=[jax.ShapeDtypeStruct((t, D_CONV), F32), jax.ShapeDtypeStruct((t, D_ATTN), F32),
                   jax.ShapeDtypeStruct((1, D_ATTN), F32)],
        compiler_params=_params(dimension_semantics=("arbitrary",)),
    )(*deps, dx2, o, g_attn, w_out)


def _split_w_in(w_in_t):
    w_ag = w_in_t[:2 * D_CONV]
    w_qkv = w_in_t[2 * D_CONV:2 * D_CONV + 3 * D_ATTN]
    w_f = jnp.pad(w_in_t[2 * D_CONV + 3 * D_ATTN:], ((0, LANES - N_HEADS), (0, 0)))
    return w_ag, w_qkv, w_f


def _head_rows(v):
    return jnp.pad(v, ((0, HEAD_ROWS - N_HEADS),) + ((0, 0),) * (v.ndim - 1))


def _local_step(x, target, p, get_weights, put_grads, flush_grads):
    t = x.shape[0]
    fb = _head_rows(p["forget_b"].reshape(N_HEADS, 1))

    w, deps = get_weights("ffn1_w13", None)
    h1, gu1, act1 = _ffn_up(x, p["ffn1_norm"], w["ffn1_w13"], "ffn1_up", deps)
    w2, _ = get_weights("ffn1_w2", act1)
    w.update(w2)
    x1 = _ffn_down(x, act1, w["ffn1_w2"], "ffn1_down")
    wm, _ = get_weights("mix", x1)
    w.update(wm)
    w_ag, w_qkv, w_f = _split_w_in(w["w_in"])
    conv_w = jnp.pad(w["conv_w"], ((0, CONV_PAD - CONV_WIDTH), (0, 0)))
    h2, ag, qkv, fl = _mix_proj(x1, p["mix_norm"], w_ag, w_qkv, w_f)
    flt = _head_rows(fl[:, :N_HEADS].T)
    dcum = _gates_fwd(flt, fb)[:N_HEADS]
    drow = dcum.reshape(N_HEADS, 1, t)
    ycn, y_conv = _conv_fwd(ag, conv_w, p["conv_b"], p["conv_ln_g"], p["conv_ln_b"], p["out_norm_conv"])
    o, lse = _attn_fwd(qkv, drow, [ycn])
    _, deps = get_weights("ffn2:landed", o)
    x2, yan = _out_proj(ycn, o, p["out_norm_attn"], w["w_out"], x1, deps)
    w2, _ = get_weights("ffn2", x2)
    w.update(w2)
    dx3, h3, gu2, act2, loss, d_final = _ffn_fwd(x2, p["ffn2_norm"], w["ffn2_w13"], w["ffn2_w2"], "ffn2_fwd",
                                                 head=(target, p["final_norm"]))

    g = {}
    dx2, dgu2, g["ffn2_norm"], dx3_half, dx2_bf16 = _ffn_bwd(
        dx3, x2, gu2, p["ffn2_norm"], w["ffn2_w13"], w["ffn2_w2"], "ffn2_bwd")
    dw13 = _wgrad(h3, dgu2, N_CHIPS, "ffn2_dw13")
    dw2 = _wgrad(act2, dx3_half, 1, "ffn2_dw2").reshape(D_FF, D_MODEL)
    deps = put_grads("ffn2", {"ffn2_w13": dw13, "ffn2_w2": dw2})
    dyc, do, g["out_norm_attn"] = _out_proj_bwd(dx2_bf16, o, p["out_norm_attn"], w["w_out"], deps)
    deps = flush_grads("ffn2", [dyc])
    dw_out = _wgrad(jnp.concatenate([ycn, yan], axis=1), dx2_bf16, 1, "dw_out", deps).reshape(D_MODEL, D_MODEL)
    dq, dk, dv, ddrow = _attn_bwd(qkv, drow, lse, do)
    dflt, dfb = _gates_bwd(_head_rows(ddrow.reshape(N_HEADS, t)), flt, fb)
    g["forget_b"] = dfb[:N_HEADS, 0].reshape(1, N_HEADS)
    dfl = jnp.pad(dflt[:N_HEADS].T, ((0, 0), (0, LANES - N_HEADS)))
    dag, dconv_w, g["conv_b"], g["conv_ln_g"], g["conv_ln_b"], g["out_norm_conv"] = _conv_bwd(
        ag, y_conv, dyc, conv_w, p["conv_ln_g"], p["conv_ln_b"], p["out_norm_conv"])
    g["conv_w"] = dconv_w[:CONV_WIDTH]
    dproj = jnp.concatenate([dag, dq, dk, dv, dfl.astype(BF16)], axis=1)
    dx1, g["mix_norm"] = _mix_proj_bwd(dproj, dx2, x1, p["mix_norm"], w_ag, w_qkv, w_f)
    dw_in = _wgrad(dproj, h2, 1, "dw_in").reshape(dproj.shape[1], D_MODEL)[:N_IN]
    deps = put_grads("mix", {"w_in": dw_in, "w_out": dw_out})
    dx0, dgu1, g["ffn1_norm"], dx1_half, _ = _ffn_bwd(
        dx1, x, gu1, p["ffn1_norm"], w["ffn1_w13"], w["ffn1_w2"], "ffn1_bwd", deps)
    g["final_norm"] = d_final
    g["loss"] = loss[:, :1]
    deps = flush_grads("mix", put_grads("small", g))
    dw2 = _wgrad(act1, dx1_half, 1, "ffn1_dw2", deps).reshape(D_FF, D_MODEL)
    deps = flush_grads("ffn1_w2", put_grads("ffn1_w2", {"ffn1_w2": dw2}))
    dw13 = _wgrad(h1, dgu1, N_CHIPS, "ffn1_dw13", deps)
    put_grads("ffn1_w13", {"ffn1_w13": dw13})
    return dx0


MESH = pl.DeviceIdType.MESH


def _place():
    x, y, c = lax.axis_index("x"), lax.axis_index("y"), lax.axis_index("c")
    chips = [(1 - x, y), (x, 1 - y), (1 - x, 1 - y)]
    return x, y, c, chips


def _hbm_out(shape, dtype):
    return jax.ShapeDtypeStruct(shape, dtype)


def _comm_call(body, name, ins, out_shapes, n_remote, in_place=False):
    return pl.pallas_call(
        body, name=name, in_specs=[_ANY] * len(ins), out_specs=[_ANY] * len(out_shapes), out_shape=out_shapes,
        scratch_shapes=[pltpu.SemaphoreType.DMA((n_remote,)), pltpu.SemaphoreType.DMA((n_remote,))],
        input_output_aliases={i: i for i in range(len(ins))} if in_place else {},
    )(*ins)


def _remote(src, dst, sems, n, to):
    send_sems, recv_sems = sems
    return pltpu.make_async_remote_copy(src_ref=src, dst_ref=dst, send_sem=send_sems.at[n], recv_sem=recv_sems.at[n],
                                        device_id=to, device_id_type=MESH)


HALF_ROWS_MULTIPLE = 32


def _halved_by_rows(rows):
    return rows % HALF_ROWS_MULTIPLE == 0


def _half_shape(rows, cols):
    return (rows // 2, cols) if _halved_by_rows(rows) else (rows, cols // 2)


def _half_index(rows, core):
    return (core, 0) if _halved_by_rows(rows) else (0, core)


def _half_of(ref, rows, cols, core, *lead):
    if _halved_by_rows(rows):
        return ref.at[(*lead, pl.ds(core * (rows // 2), rows // 2), slice(None))]
    return ref.at[(*lead, slice(None), pl.ds(core * (cols // 2), cols // 2))]


def _into_slot(shard, chip, dtype, name, deps=()):
    rows, cols = shard.shape
    half = _half_shape(rows, cols)
    by_rows = _halved_by_rows(rows)
    deps = tuple(deps)

    def body(k_ref, *refs):
        s_ref, o_ref = refs[len(deps):]
        o_ref[0] = s_ref[...].astype(dtype)

    return pl.pallas_call(
        body, name=name,
        grid_spec=pltpu.PrefetchScalarGridSpec(
            num_scalar_prefetch=1, grid=(2,),
            in_specs=[_ANY] * len(deps) + [pl.BlockSpec(half, lambda i, k_ref: (i, 0) if by_rows else (0, i))],
            out_specs=pl.BlockSpec((1,) + half, lambda i, k_ref: (k_ref[0], i, 0) if by_rows else (k_ref[0], 0, i))),
        out_shape=jax.ShapeDtypeStruct((N_CHIPS, rows, cols), dtype),
        compiler_params=_params(dimension_semantics=("arbitrary",)),
    )(chip, *deps, shard)


def _run_copies(name, bufs, n_copies, plan):
    n = len(bufs)

    def body(*refs):
        copies = plan(refs[n:2 * n], refs[2 * n:2 * n + 2])
        for send, _ in copies:
            send.start()
        for send, recv in copies:
            send.wait_send()
            recv.wait_recv()

    return _comm_call(body, name, bufs, [_hbm_out(b.shape, b.dtype) for b in bufs], n_copies, in_place=True)


def _forward_halves(slots, name):
    return _run_copies(name, slots, 3 * len(slots), _d2d_forward_plan(slots))


_HBM = pl.BlockSpec(memory_space=pltpu.HBM)
_SEM = pl.BlockSpec(memory_space=pltpu.SEMAPHORE)
_DATAFLOW = pltpu.SideEffectType.DATAFLOW_SIDE_EFFECTING


def _split_copy_start(name, bufs, n_copies, plan):
    n = len(bufs)

    def body(*refs):
        for send, _ in plan(refs[:n], (refs[n], refs[n + 1])):
            send.start()
        token = refs[-1]
        token[...] = jnp.zeros_like(token)

    out = pl.pallas_call(
        body, name=name,
        out_shape=(pltpu.SemaphoreType.DMA((n_copies,)), pltpu.SemaphoreType.DMA((n_copies,)),
                   *[pltpu.HBM(b.shape, b.dtype) for b in bufs], jax.ShapeDtypeStruct((8, LANES), F32)),
        in_specs=[_HBM] * n, out_specs=(_SEM, _SEM, *[_HBM] * n, pl.BlockSpec(memory_space=pltpu.VMEM)),
        input_output_aliases={i: 2 + i for i in range(n)},
        compiler_params=pltpu.CompilerParams(has_side_effects=_DATAFLOW),
    )(*[pltpu.with_memory_space_constraint(b, pltpu.HBM) for b in bufs])
    return out[0], out[1], list(out[2:2 + n]), out[-1]


def _split_copy_wait(name, started, plan, after, passed=()):
    send_sems, recv_sems, bufs, _ = started
    n = len(bufs)
    after = tuple(after)
    bufs = list(bufs) + list(passed)
    total = len(bufs)

    def body(*refs):
        for send, recv in plan(refs[:n], (refs[total], refs[total + 1])):
            send.wait_send()
            recv.wait_recv()

    out = pl.pallas_call(
        body, name=name, out_shape=tuple(pltpu.HBM(b.shape, b.dtype) for b in bufs),
        in_specs=[_HBM] * total + [_SEM, _SEM] + [_ANY] * len(after), out_specs=tuple([_HBM] * total),
        input_output_aliases={i: i for i in range(total)},
        compiler_params=pltpu.CompilerParams(has_side_effects=_DATAFLOW),
    )(*bufs, send_sems, recv_sems, *after)
    return list(out)


def _ici_gather_plan(slots):
    def plan(refs, sems):
        x, y, c, chips = _place()
        me = 2 * x + y
        copies = []
        for i, ref in enumerate(refs):
            for j, chip in enumerate(chips):
                mine = _half_of(ref, *slots[i].shape[1:], c, me)
                theirs = _half_of(ref, *slots[i].shape[1:], c, 2 * chip[0] + chip[1])
                to = (*chip, c)
                copies.append((_remote(mine, mine, sems, 3 * i + j, to), _remote(theirs, theirs, sems, 3 * i + j, to)))
        return copies

    return plan


def _ici_scatter_plan(n):
    def plan(refs, sems):
        x, y, c, chips = _place()
        copies = []
        for i in range(n):
            for j, chip in enumerate(chips):
                cp = _remote(refs[i].at[2 * chip[0] + chip[1]], refs[n + i].at[j], sems, 3 * i + j, (*chip, c))
                copies.append((cp, cp))
        return copies

    return plan


def _d2d_forward_plan(slots):
    def plan(refs, sems):
        x, y, c, chips = _place()
        sibling = (x, y, 1 - c)
        copies = []
        for i, ref in enumerate(refs):
            for j, chip in enumerate(chips):
                src_chip = 2 * chip[0] + chip[1]
                mine = _half_of(ref, *slots[i].shape[1:], c, src_chip)
                theirs = _half_of(ref, *slots[i].shape[1:], 1 - c, src_chip)
                copies.append((_remote(mine, mine, sems, 3 * i + j, sibling),
                               _remote(theirs, theirs, sems, 3 * i + j, sibling)))
        return copies

    return plan


def _pair_exchange_plan(grads):
    n = len(grads)

    def plan(refs, sems):
        x, y, c, _ = _place()
        copies = []
        for i in range(n):
            theirs = _half_of(refs[i], *grads[i].shape[1:], 1 - c, slice(None))
            cp = _remote(theirs, refs[n + i], sems, i, (x, y, 1 - c))
            copies.append((cp, cp))
        return copies

    return plan


def _pair_share_plan(shapes):
    def plan(refs, sems):
        x, y, c, _ = _place()
        sibling = (x, y, 1 - c)
        copies = []
        for i, ref in enumerate(refs):
            mine, theirs = _half_of(ref, *shapes[i], c), _half_of(ref, *shapes[i], 1 - c)
            copies.append((_remote(mine, mine, sems, i, sibling), _remote(theirs, theirs, sems, i, sibling)))
        return copies

    return plan


def _pair_share(halves, name):
    return _run_copies(name, halves, len(halves), _pair_share_plan([h.shape for h in halves]))


N_DEVICES = 8
FLIPS = [(fx, fy, fc) for fx in range(2) for fy in range(2) for fc in range(2)][1:]


def _small_slots(v, me):
    rows = v.shape[0]

    def body(k_ref, v_ref, o_ref):
        o_ref[0] = v_ref[...]

    return pl.pallas_call(
        body, name="small_slot",
        grid_spec=pltpu.PrefetchScalarGridSpec(
            num_scalar_prefetch=1, grid=(1,),
            in_specs=[pl.BlockSpec((rows, LANES), lambda i, k_ref: (0, 0))],
            out_specs=pl.BlockSpec((1, rows, LANES), lambda i, k_ref: (k_ref[0], 0, 0))),
        out_shape=jax.ShapeDtypeStruct((N_DEVICES, rows, LANES), F32),
        compiler_params=_params(dimension_semantics=("arbitrary",)),
    )(me, v)


def _small_plan():
    def plan(refs, sems):
        x, y, c, _ = _place()
        slots = refs[0]
        me = 4 * x + 2 * y + c
        copies = []
        for n, (fx, fy, fc) in enumerate(FLIPS):
            to = (x ^ fx, y ^ fy, c ^ fc)
            src = 4 * to[0] + 2 * to[1] + to[2]
            copies.append((_remote(slots.at[me], slots.at[me], sems, n, to), _remote(slots.at[src], slots.at[src], sems, n, to)))
        return copies

    return plan


def _small_sum(slots):
    def body(s_ref, o_ref):
        acc = s_ref[0]
        for s in range(1, N_DEVICES):
            acc = acc + s_ref[s]
        o_ref[...] = acc

    return pl.pallas_call(body, name="small_sum", out_shape=jax.ShapeDtypeStruct(slots.shape[1:], F32),
                          compiler_params=_params())(slots)


def _pair_add(gs, sibs, core, name):
    n = len(gs)
    halves = [_half_shape(*g.shape[1:]) for g in gs]

    def body(c_ref, *refs):
        for g_ref, s_ref, o_ref in zip(refs[:n], refs[n:2 * n], refs[2 * n:]):
            o_ref[0] = (g_ref[0].astype(F32) + s_ref[0].astype(F32)).astype(BF16)

    def mine(g, half):
        return pl.BlockSpec((1,) + half, lambda s, c_ref: (s, *_half_index(g.shape[1], c_ref[0])))

    whole = [pl.BlockSpec((1,) + half, lambda s, c_ref: (s, 0, 0)) for half in halves]
    return pl.pallas_call(
        body, name=name,
        grid_spec=pltpu.PrefetchScalarGridSpec(
            num_scalar_prefetch=1, grid=(N_CHIPS,),
            in_specs=[mine(g, half) for g, half in zip(gs, halves)] + whole, out_specs=whole),
        out_shape=[jax.ShapeDtypeStruct((N_CHIPS,) + half, BF16) for half in halves],
        compiler_params=_params(dimension_semantics=("arbitrary",)),
    )(core, *gs, *sibs)


def _chip_add(parts, recvs, chip_core, shapes, name):
    n = len(parts)
    halves = [_half_shape(*shape) for shape in shapes]

    def body(kc_ref, *refs):
        for p_ref, r_ref, o_ref in zip(refs[:n], refs[n:2 * n], refs[2 * n:]):
            acc = p_ref[0].astype(F32)
            for j in range(N_CHIPS - 1):
                acc = acc + r_ref[j].astype(F32)
            o_ref[...] = acc

    def out_spec(shape, half):
        return pl.BlockSpec(half, lambda s, kc_ref: _half_index(shape[0], kc_ref[1]))

    return pl.pallas_call(
        body, name=name,
        grid_spec=pltpu.PrefetchScalarGridSpec(
            num_scalar_prefetch=1, grid=(1,),
            in_specs=[pl.BlockSpec((1,) + half, lambda s, kc_ref: (kc_ref[0], 0, 0)) for half in halves]
            + [pl.BlockSpec((N_CHIPS - 1,) + half, lambda s, kc_ref: (0, 0, 0)) for half in halves],
            out_specs=[out_spec(shape, half) for shape, half in zip(shapes, halves)]),
        out_shape=[jax.ShapeDtypeStruct(tuple(shape), F32) for shape in shapes],
        compiler_params=_params(dimension_semantics=("arbitrary",)),
    )(chip_core, *parts, *recvs)


def _adamw_math(w, g, m, v):
    m = ADAM_B1 * m + (1.0 - ADAM_B1) * g
    v = ADAM_B2 * v + (1.0 - ADAM_B2) * (g * g)
    m_hat = m / (1.0 - ADAM_B1 ** ADAM_STEP)
    v_hat = v / (1.0 - ADAM_B2 ** ADAM_STEP)
    delta = -ADAM_LR * (m_hat / (jnp.sqrt(v_hat) + ADAM_EPS) + ADAM_WD * w)
    return delta, m, v


ADAM_PARTS = 2


def _adamw_matrix(w, g, m, v, name):
    rows, cols = w.shape
    by_rows = rows % (8 * ADAM_PARTS) == 0
    block = (rows // ADAM_PARTS, cols) if by_rows else (rows, cols // ADAM_PARTS)

    def body(w_ref, g_ref, m_ref, v_ref, go_ref, d_ref, mo_ref, vo_ref):
        gv = g_ref[...]
        go_ref[...] = gv
        d_ref[...], mo_ref[...], vo_ref[...] = _adamw_math(w_ref[...], gv, m_ref[...], v_ref[...])

    spec = pl.BlockSpec(block, lambda i: (i, 0) if by_rows else (0, i))
    shape = jax.ShapeDtypeStruct((rows, cols), F32)
    return pl.pallas_call(
        body, name=name, grid=(ADAM_PARTS,), in_specs=[spec] * 4, out_specs=[spec] * 4, out_shape=[shape] * 4,
        compiler_params=_params(dimension_semantics=("arbitrary",)),
    )(w, g, m, v)


def _adamw_small(ws, gs, ms, vs):
    n = len(ws)

    def body(*refs):
        for i in range(n):
            w_ref, g_ref, m_ref, v_ref = (refs[k * n + i] for k in range(4))
            d_ref, mo_ref, vo_ref = (refs[(4 + k) * n + i] for k in range(3))
            d_ref[...], mo_ref[...], vo_ref[...] = _adamw_math(w_ref[...], g_ref[...], m_ref[...], v_ref[...])

    shapes = [jax.ShapeDtypeStruct(w.shape, F32) for w in ws]
    out = pl.pallas_call(body, name="adamw_small", out_shape=shapes * 3, compiler_params=_params())(*ws, *gs, *ms, *vs)
    return out[:n], out[n:2 * n], out[2 * n:]


MATRICES = ["ffn1_w13", "ffn1_w2", "w_in", "w_out", "ffn2_w13", "ffn2_w2"]
VECTORS = ["ffn1_norm", "mix_norm", "conv_b", "conv_ln_g", "conv_ln_b", "forget_b", "out_norm_conv",
           "out_norm_attn", "ffn2_norm", "final_norm"]
WEIGHTS = ["ffn1_norm", "ffn1_w13", "ffn1_w2", "mix_norm", "w_in", "conv_w", "conv_b", "conv_ln_g", "conv_ln_b",
           "forget_b", "out_norm_conv", "out_norm_attn", "w_out", "ffn2_norm", "ffn2_w13", "ffn2_w2", "final_norm"]


def _pack_small(g, names):
    rows, layout = [], []
    for n in names:
        flat = g[n].reshape(-1)
        pad = (-flat.shape[0]) % LANES
        rows.append(jnp.pad(flat, (0, pad)).reshape(-1, LANES))
        layout.append((n, g[n].shape, flat.shape[0], rows[-1].shape[0]))
    packed = jnp.concatenate(rows, axis=0)
    pad_rows = (-packed.shape[0]) % 8
    return jnp.pad(packed, ((0, pad_rows), (0, 0))), layout


def _unpack_small(packed, layout):
    out, r = {}, 0
    for n, shape, size, nrows in layout:
        out[n] = packed[r:r + nrows].reshape(-1)[:size].reshape(shape)
        r += nrows
    return out


def kernel(x, ffn1_norm, ffn1_w13, ffn1_w2, mix_norm, w_in, conv_w, conv_b, conv_ln_g, conv_ln_b, forget_b, out_norm_conv, out_norm_attn, w_out, ffn2_norm, ffn2_w13, ffn2_w2, final_norm, loss_target, m_ffn1_norm, m_ffn1_w13, m_ffn1_w2, m_mix_norm, m_w_in, m_conv_w, m_conv_b, m_conv_ln_g, m_conv_ln_b, m_forget_b, m_out_norm_conv, m_out_norm_attn, m_w_out, m_ffn2_norm, m_ffn2_w13, m_ffn2_w2, m_final_norm, v_ffn1_norm, v_ffn1_w13, v_ffn1_w2, v_mix_norm, v_w_in, v_conv_w, v_conv_b, v_conv_ln_g, v_conv_ln_b, v_forget_b, v_out_norm_conv, v_out_norm_attn, v_w_out, v_ffn2_norm, v_ffn2_w13, v_ffn2_w2, v_final_norm):
    args = dict(locals())
    weights = {n: args[n] for n in WEIGHTS}
    core = lax.axis_index("c").astype(jnp.int32).reshape(1)
    chip = (2 * lax.axis_index("x") + lax.axis_index("y")).astype(jnp.int32)
    chip1 = chip.reshape(1)
    chip_core = jnp.concatenate([chip1, core])

    def held(n, a):
        return a[0].T if n == "w_in" else a[0]

    def given(n, a):
        return (a.T if n == "w_in" else a)[None]

    def slot(n, deps=()):
        if n == "conv_w":
            rows = jnp.pad(conv_w[0], ((0, CONV_PAD - CONV_WIDTH), (0, 0)))
            return _into_slot(rows, chip1, F32, "slot_conv_w", deps)
        return _into_slot(held(n, weights[n]), chip1, BF16, "slot_" + n, deps)

    fetched = {"ffn1_w13": ["ffn1_w13"], "ffn1_w2": ["ffn1_w2"], "mix": ["w_in", "w_out", "conv_w"],
               "ffn2": ["ffn2_w13", "ffn2_w2"]}
    fetch = {}

    def as_weights(group, bufs):
        out = {}
        for n, b in zip(fetched[group], bufs):
            if n.endswith("w13"):
                out[n] = b
            elif n != "conv_w":
                out[n] = b.reshape(N_CHIPS * b.shape[1], b.shape[2])
            else:
                out[n] = b[:, :CONV_WIDTH].transpose(1, 0, 2).reshape(CONV_WIDTH, D_CONV)
        return out

    def get_weights(group, after):
        if group == "ffn1_w13":
            first = [slot("ffn1_w13")]
            plan = _ici_gather_plan(first)
            started = _split_copy_start("gather_ffn1_w13_start", first, 3, plan)
            second = [slot("ffn1_w2", [started[3]])]
            plan2 = _ici_gather_plan(second)
            fetch["ffn1_w2"] = plan2, _split_copy_start("gather_ffn1_w2_start", second, 3, plan2)
            later_names = fetched["mix"] + fetched["ffn2"]
            later = [slot(n, [fetch["ffn1_w2"][1][3]]) for n in later_names]
            landed = _split_copy_wait("gather_ffn1_w13_wait", started, plan, [], passed=later)
            bufs = _forward_halves(landed[:1], "forward_ffn1_w13")
            behind = dict(zip(later_names, landed[1:]))
            for later in ("mix", "ffn2"):
                bufs_later = [behind[n] for n in fetched[later]]
                plan = _ici_gather_plan(bufs_later)
                fetch[later] = plan, _split_copy_start("gather_%s_start" % later, bufs_later, 3 * len(bufs_later), plan)
            return as_weights(group, bufs), [fetch["mix"][1][3], fetch["ffn2"][1][3]]
        plan, started = fetch[group.split(":")[0]]
        if group == "ffn2:landed":
            landed = _split_copy_wait("gather_ffn2_wait", started, plan, [after])
            plan = _d2d_forward_plan(landed)
            fetch["ffn2"] = plan, _split_copy_start("forward_ffn2_start", landed, 3 * len(landed), plan)
            return {}, [fetch["ffn2"][1][3]]
        if group == "ffn2":
            return as_weights(group, _split_copy_wait("forward_ffn2_wait", started, plan, [after])), []
        landed = _split_copy_wait("gather_%s_wait" % group, started, plan, [after])
        return as_weights(group, _forward_halves(landed, "forward_" + group)), []

    def shard_major(n, g):
        return g if n.endswith("w13") else g.reshape(N_CHIPS, g.shape[0] // N_CHIPS, g.shape[1])

    exchange, scatter = {}, {}
    small_names = VECTORS + ["conv_w"]
    small = {}

    def put_grads(group, grads):
        if group == "small":
            packed, layout = _pack_small(grads, small_names + ["loss"])
            me = (4 * lax.axis_index("x") + 2 * lax.axis_index("y") + lax.axis_index("c")).astype(jnp.int32).reshape(1)
            plan = _small_plan()
            exchange[group] = layout, plan, _split_copy_start("small_start", [_small_slots(packed, me)], len(FLIPS), plan)
            return [exchange[group][2][3]]
        names = list(grads)
        local = [shard_major(n, grads[n]) for n in names]
        landing = [lax.empty((N_CHIPS,) + _half_shape(*a.shape[1:]), BF16) for a in local]
        plan = _pair_exchange_plan(local)
        exchange[group] = names, plan, _split_copy_start("exchange_%s_start" % group, local + landing, len(local), plan)
        return [exchange[group][2][3]]

    def flush_grads(group, after):
        names, plan, started = exchange[group]
        done = _split_copy_wait("exchange_%s_wait" % group, started, plan, after)
        local, sib = done[:len(names)], done[len(names):]
        parts = list(_pair_add(local, sib, core, "pair_add_" + group))
        landing = [lax.empty((N_CHIPS - 1,) + q.shape[1:], BF16) for q in parts]
        plan = _ici_scatter_plan(len(parts))
        shapes = [a.shape[1:] for a in local]
        scatter[group] = names, plan, _split_copy_start("scatter_%s_start" % group, parts + landing, 3 * len(parts), plan), shapes
        return [scatter[group][2][3]]

    p = {n: weights[n] for n in VECTORS}
    p["final_norm"] = final_norm.reshape(1, D_MODEL)
    dx = _local_step(x[0], loss_target[0], p, get_weights, put_grads, flush_grads)
    layout, plan, started = exchange["small"]
    slots, = _split_copy_wait("small_wait", started, plan, [exchange["ffn1_w13"][2][3]])
    small.update(_unpack_small(_small_sum(slots), layout))
    loss = small["loss"].reshape(())

    grad = {n: small[n] for n in VECTORS}
    grad["final_norm"] = small["final_norm"].reshape(D_MODEL)
    grad["conv_w"] = lax.dynamic_slice_in_dim(small["conv_w"], chip * (D_CONV // N_CHIPS), D_CONV // N_CHIPS, axis=1)[None]

    delta, new_m, new_v = {}, {}, {}

    def reduce_chips(group, after):
        names, plan, started, shapes = scatter[group]
        done = _split_copy_wait("scatter_%s_wait" % group, started, plan, after)
        parts, landed = done[:len(names)], done[len(names):]
        return list(_chip_add(parts, landed, chip_core, shapes, "chip_add_" + group))

    def update(group, full):
        ends = []
        for n, reduced in zip(scatter[group][0], full):
            go, d, mo, vo = _adamw_matrix(held(n, weights[n]), reduced, held(n, args["m_" + n]), held(n, args["v_" + n]),
                                          "adamw_" + n)
            grad[n], delta[n], new_m[n], new_v[n] = given(n, go), given(n, d), given(n, mo), given(n, vo)
            ends.append(vo)
        return ends

    def share_start(group, halves):
        plan = _pair_share_plan(scatter[group][3])
        return plan, _split_copy_start("share_%s_start" % group, halves, len(halves), plan)

    halves_ffn2 = reduce_chips("ffn2", [exchange["ffn1_w13"][2][3]])
    plan_ffn2, share_ffn2 = share_start("ffn2", halves_ffn2)
    last_scatter = flush_grads("ffn1_w13", [share_ffn2[3]])
    halves_mix = reduce_chips("mix", last_scatter)
    plan_mix, share_mix = share_start("mix", halves_mix)
    done_ffn2 = update("ffn2", _split_copy_wait("share_ffn2_wait", share_ffn2, plan_ffn2, [share_mix[3]]))
    done_mix = update("mix", _split_copy_wait("share_mix_wait", share_mix, plan_mix, done_ffn2))
    as2d = lambda a: a.reshape(-1, a.shape[-1])
    ds, mos, vos = _adamw_small([as2d(weights[n]) for n in small_names], [as2d(grad[n]) for n in small_names],
                                [as2d(args["m_" + n]) for n in small_names], [as2d(args["v_" + n]) for n in small_names])
    for n, d, mo, vo in zip(small_names, ds, mos, vos):
        shape = weights[n].shape
        delta[n], new_m[n], new_v[n] = d.reshape(shape), mo.reshape(shape), vo.reshape(shape)
    behind = done_ffn2 + done_mix + [vos[0]]
    halves_w2 = reduce_chips("ffn1_w2", behind)
    halves_w13 = reduce_chips("ffn1_w13", behind)
    full_w2, full_w13 = _pair_share(halves_w2 + halves_w13, "pair_share_ffn1")
    update("ffn1_w2", [full_w2])
    update("ffn1_w13", [full_w13])

    return (loss, dx[None], *[grad[n] for n in WEIGHTS], *[delta[n] for n in WEIGHTS],
            *[new_m[n] for n in WEIGHTS], *[new_v[n] for n in WEIGHTS])
```

```python
import jax
import jax.numpy as jnp
from jax import lax
from jax.experimental import pallas as pl
from jax.experimental.pallas import tpu as pltpu

F32 = jnp.float32
BF16 = jnp.bfloat16

D_MODEL = 1024
D_FF = 2816
FF_SHARD = D_FF // 2
D_CONV = 512
D_ATTN = 512
N_HEADS = 8
HEAD_DIM = 64
CONV_WIDTH = 31
CONV_PAD = 32
N_IN = 2 * D_CONV + 3 * D_ATTN + N_HEADS
EPS = 1e-6
N_CHIPS = 4
LANES = 128
TOKEN_ROWS = 512
HEAD_ROWS = 16

ADAM_LR = 0.001
ADAM_B1 = 0.9
ADAM_B2 = 0.999
ADAM_EPS = 1e-08
ADAM_WD = 0.01
ADAM_STEP = 10

VMEM_LIMIT = 56 * 1024 * 1024

_NT = (((1,), (1,)), ((), ()))
_TN = (((0,), (0,)), ((), ()))


def _dot(a, b):
    return jnp.dot(a, b, preferred_element_type=F32)


def _dot_nt(a, b):
    return lax.dot_general(a, b, _NT, preferred_element_type=F32)


def _dot_tn(a, b):
    return lax.dot_general(a, b, _TN, preferred_element_type=F32)


def _params(**kw):
    return pltpu.CompilerParams(vmem_limit_bytes=VMEM_LIMIT, **kw)


def _sigmoid(x):
    return 1.0 / (1.0 + jnp.exp(-x))


def _rms_stats(x):
    return lax.rsqrt(jnp.mean(x * x, axis=-1, keepdims=True) + EPS)


def _rms_bwd(x, r, g, dh):
    t = dh * g
    dx = r * t - x * (r * r * r) * jnp.mean(t * x, axis=-1, keepdims=True)
    return dx, dh * x * r


def _silu_grad(z, sg):
    return sg * (1.0 + z * (1.0 - sg))


def _row_spec(tm, n):
    return pl.BlockSpec((tm, n), lambda i: (i, 0))


def _full_spec(shape):
    nd = len(shape)
    return pl.BlockSpec(shape, lambda i: (0,) * nd)


_ANY = pl.BlockSpec(memory_space=pl.ANY)


def _skip(n, body):
    return lambda *refs: body(*refs[n:])


FFN_ROWS = 256
FFN_WEIGHT_PARTS = N_CHIPS + 2


def _with_ffn_weights(w13_hbm, w2_hbm, w13_ref, w2_ref, sems, order, tile):
    first = pl.program_id(0) == 0
    copies = {}
    if w13_hbm is not None:
        for k in range(N_CHIPS):
            copies["w13", k] = pltpu.make_async_copy(w13_hbm.at[k], w13_ref.at[k], sems.at[k])
    if w2_hbm is not None:
        for half in range(2):
            rows = pl.ds(half * FF_SHARD, FF_SHARD)
            copies["w2", half] = pltpu.make_async_copy(w2_hbm.at[rows, :], w2_ref.at[rows, :], sems.at[N_CHIPS + half])

    @pl.when(first)
    def _():
        for part in order:
            copies[part].start()

        def ready(*parts):
            for part in parts:
                copies[part].wait()

        tile(ready)

    @pl.when(jnp.logical_not(first))
    def _():
        tile(lambda *parts: None)


def _ffn_fwd(x, g, w13s, w2, name, deps=(), head=None):
    t = x.shape[0]
    tm = FFN_ROWS
    deps = tuple(deps)
    n_head = 0 if head is None else 2

    def body(x_ref, g_ref, *refs):
        head_refs, (w13_hbm, w2_hbm, xo_ref, h_ref, gu_ref, a_ref) = refs[:n_head], refs[n_head:n_head + 6]
        tail_refs, (w13_ref, w2_ref, sems) = refs[n_head + 6:n_head + 6 + n_head], refs[-3:]
        if head is not None:
            @pl.when(pl.program_id(0) == 0)
            def _():
                for ref in tail_refs:
                    ref[...] = jnp.zeros_like(ref)

        def finish(xo):
            if head is None:
                xo_ref[...] = xo
                return
            (t_ref, fg_ref), (loss_ref, dfg_ref) = head_refs, tail_refs
            r = _rms_stats(xo)
            fg = fg_ref[...]
            err = xo * r * fg - t_ref[...]
            row = jnp.sum(err * err, axis=1, keepdims=True) * (0.5 / D_MODEL)
            loss_ref[...] += jnp.sum(row, axis=0, keepdims=True)
            dx, dg_rows = _rms_bwd(xo, r, fg, err * (1.0 / D_MODEL))
            xo_ref[...] = dx
            dfg_ref[...] += jnp.sum(dg_rows, axis=0, keepdims=True)

        def tile(ready):
            xv = x_ref[...]
            hb = (xv * _rms_stats(xv) * g_ref[...]).astype(BF16)
            h_ref[...] = hb
            acc = jnp.zeros((tm, D_MODEL), F32)
            for half in range(2):
                lo = half * FF_SHARD
                ready(("w13", half), ("w13", 2 + half))
                gate = _dot(hb, w13_ref[half])
                up = _dot(hb, w13_ref[2 + half])
                gu_ref[:, lo:lo + FF_SHARD] = gate.astype(BF16)
                gu_ref[:, D_FF + lo:D_FF + lo + FF_SHARD] = up.astype(BF16)
                a = (gate * _sigmoid(gate) * up).astype(BF16)
                a_ref[:, lo:lo + FF_SHARD] = a
                ready(("w2", half))
                acc = acc + _dot(a, w2_ref[lo:lo + FF_SHARD, :])
            finish(xv + 0.5 * acc)

        _with_ffn_weights(w13_hbm, w2_hbm, w13_ref, w2_ref, sems,
                          [("w13", 0), ("w13", 2), ("w2", 0), ("w13", 1), ("w13", 3), ("w2", 1)], tile)

    head_in = [] if head is None else [_row_spec(tm, D_MODEL), _full_spec((1, D_MODEL))]
    head_out = [] if head is None else [_full_spec((1, LANES)), _full_spec((1, D_MODEL))]
    head_shapes = [] if head is None else [jax.ShapeDtypeStruct((1, LANES), F32), jax.ShapeDtypeStruct((1, D_MODEL), F32)]
    return pl.pallas_call(
        _skip(len(deps), body), name=name, grid=(t // tm,),
        in_specs=[_ANY] * len(deps) + [_row_spec(tm, D_MODEL), _full_spec((1, D_MODEL))] + head_in + [_ANY, _ANY],
        out_specs=[_row_spec(tm, D_MODEL), _row_spec(tm, D_MODEL), _row_spec(tm, 2 * D_FF), _row_spec(tm, D_FF)] + head_out,
        out_shape=[jax.ShapeDtypeStruct((t, D_MODEL), F32), jax.ShapeDtypeStruct((t, D_MODEL), BF16),
                   jax.ShapeDtypeStruct((t, 2 * D_FF), BF16), jax.ShapeDtypeStruct((t, D_FF), BF16)] + head_shapes,
        scratch_shapes=[pltpu.VMEM(w13s.shape, BF16), pltpu.VMEM(w2.shape, BF16),
                        pltpu.SemaphoreType.DMA((FFN_WEIGHT_PARTS,))],
        compiler_params=_params(dimension_semantics=("arbitrary",)),
    )(*deps, x, g, *(head or ()), w13s, w2)


def _ffn_up(x, g, w13s, name, deps=()):
    t = x.shape[0]
    tm = FFN_ROWS
    deps = tuple(deps)

    def body(x_ref, g_ref, w13_hbm, h_ref, gu_ref, a_ref, w13_ref, sems):
        def tile(ready):
            xv = x_ref[...]
            hb = (xv * _rms_stats(xv) * g_ref[...]).astype(BF16)
            h_ref[...] = hb
            for half in range(2):
                lo = half * FF_SHARD
                ready(("w13", half), ("w13", 2 + half))
                gate = _dot(hb, w13_ref[half])
                up = _dot(hb, w13_ref[2 + half])
                gu_ref[:, lo:lo + FF_SHARD] = gate.astype(BF16)
                gu_ref[:, D_FF + lo:D_FF + lo + FF_SHARD] = up.astype(BF16)
                a_ref[:, lo:lo + FF_SHARD] = (gate * _sigmoid(gate) * up).astype(BF16)

        _with_ffn_weights(w13_hbm, None, w13_ref, None, sems, [("w13", 0), ("w13", 2), ("w13", 1), ("w13", 3)], tile)

    return pl.pallas_call(
        _skip(len(deps), body), name=name, grid=(t // tm,),
        in_specs=[_ANY] * len(deps) + [_row_spec(tm, D_MODEL), _full_spec((1, D_MODEL)), _ANY],
        out_specs=[_row_spec(tm, D_MODEL), _row_spec(tm, 2 * D_FF), _row_spec(tm, D_FF)],
        out_shape=[jax.ShapeDtypeStruct((t, D_MODEL), BF16), jax.ShapeDtypeStruct((t, 2 * D_FF), BF16),
                   jax.ShapeDtypeStruct((t, D_FF), BF16)],
        scratch_shapes=[pltpu.VMEM(w13s.shape, BF16), pltpu.SemaphoreType.DMA((FFN_WEIGHT_PARTS,))],
        compiler_params=_params(dimension_semantics=("arbitrary",)),
    )(*deps, x, g, w13s)


def _ffn_down(x, a, w2, name):
    t = x.shape[0]
    tm = FFN_ROWS

    def body(x_ref, a_ref, w2_hbm, xo_ref, w2_ref, sems):
        def tile(ready):
            ready(("w2", 0))
            acc = _dot(a_ref[:, 0:FF_SHARD], w2_ref[0:FF_SHARD, :])
            ready(("w2", 1))
            acc = acc + _dot(a_ref[:, FF_SHARD:], w2_ref[FF_SHARD:, :])
            xo_ref[...] = x_ref[...] + 0.5 * acc

        _with_ffn_weights(None, w2_hbm, None, w2_ref, sems, [("w2", 0), ("w2", 1)], tile)

    return pl.pallas_call(
        body, name=name, grid=(t // tm,),
        in_specs=[_row_spec(tm, D_MODEL), _row_spec(tm, D_FF), _ANY],
        out_specs=_row_spec(tm, D_MODEL), out_shape=jax.ShapeDtypeStruct((t, D_MODEL), F32),
        scratch_shapes=[pltpu.VMEM(w2.shape, BF16), pltpu.SemaphoreType.DMA((FFN_WEIGHT_PARTS,))],
        compiler_params=_params(dimension_semantics=("arbitrary",)),
    )(x, a, w2)


def _ffn_bwd(dy, x, gu, g, w13s, w2, name, deps=()):
    t = x.shape[0]
    tm = FFN_ROWS
    deps = tuple(deps)

    def body(dy_ref, x_ref, gu_ref, g_ref, w13_hbm, w2_hbm, dx_ref, dgu_ref, dg_ref, dyh_ref, dxb_ref,
             w13_ref, w2_ref, sems):
        @pl.when(pl.program_id(0) == 0)
        def _():
            dg_ref[...] = jnp.zeros_like(dg_ref)

        def tile(ready):
            dyv = dy_ref[...]
            dyh = (0.5 * dyv).astype(BF16)
            dyh_ref[...] = dyh
            dh = jnp.zeros((tm, D_MODEL), F32)
            for half in range(2):
                lo = half * FF_SHARD
                ready(("w2", half))
                da = _dot_nt(dyh, w2_ref[lo:lo + FF_SHARD, :])
                gate = gu_ref[:, lo:lo + FF_SHARD].astype(F32)
                up = gu_ref[:, D_FF + lo:D_FF + lo + FF_SHARD].astype(F32)
                sg = _sigmoid(gate)
                act = gate * sg
                dgate = (da * up * _silu_grad(gate, sg)).astype(BF16)
                dup = (da * act).astype(BF16)
                dgu_ref[:, lo:lo + FF_SHARD] = dgate
                dgu_ref[:, D_FF + lo:D_FF + lo + FF_SHARD] = dup
                ready(("w13", half), ("w13", 2 + half))
                dh = dh + _dot_nt(dgate, w13_ref[half]) + _dot_nt(dup, w13_ref[2 + half])
            xv = x_ref[...]
            dxn, dg_rows = _rms_bwd(xv, _rms_stats(xv), g_ref[...], dh)
            dx = dyv + dxn
            dx_ref[...] = dx
            dxb_ref[...] = dx.astype(BF16)
            dg_ref[...] += jnp.sum(dg_rows, axis=0, keepdims=True)

        _with_ffn_weights(w13_hbm, w2_hbm, w13_ref, w2_ref, sems,
                          [("w2", 0), ("w13", 0), ("w13", 2), ("w2", 1), ("w13", 1), ("w13", 3)], tile)

    return pl.pallas_call(
        _skip(len(deps), body), name=name, grid=(t // tm,),
        in_specs=[_ANY] * len(deps) + [_row_spec(tm, D_MODEL), _row_spec(tm, D_MODEL), _row_spec(tm, 2 * D_FF),
                                       _full_spec((1, D_MODEL)), _ANY, _ANY],
        out_specs=[_row_spec(tm, D_MODEL), _row_spec(tm, 2 * D_FF),
                   _full_spec((1, D_MODEL)), _row_spec(tm, D_MODEL), _row_spec(tm, D_MODEL)],
        out_shape=[jax.ShapeDtypeStruct((t, D_MODEL), F32), jax.ShapeDtypeStruct((t, 2 * D_FF), BF16),
                   jax.ShapeDtypeStruct((1, D_MODEL), F32),
                   jax.ShapeDtypeStruct((t, D_MODEL), BF16), jax.ShapeDtypeStruct((t, D_MODEL), BF16)],
        scratch_shapes=[pltpu.VMEM(w13s.shape, BF16), pltpu.VMEM(w2.shape, BF16),
                        pltpu.SemaphoreType.DMA((FFN_WEIGHT_PARTS,))],
        compiler_params=_params(dimension_semantics=("arbitrary",)),
    )(*deps, dy, x, gu, g, w13s, w2)


WGRAD_ROWS = (1408, 1024, 512, 384, 256)


def _wgrad(a, b, n_blocks, name, deps=()):
    t, m = a.shape
    tm = next(rows for rows in WGRAD_ROWS if m % rows == 0)
    n = b.shape[1]
    bn = n // n_blocks
    deps = tuple(deps)
    assert a.dtype == BF16 and b.dtype == BF16

    def body(a_ref, b_ref, o_ref):
        o_ref[0] = _dot_tn(a_ref[...], b_ref[...]).astype(BF16)

    return pl.pallas_call(
        _skip(len(deps), body), name=name, grid=(n_blocks, m // tm),
        in_specs=[_ANY] * len(deps) + [pl.BlockSpec((t, tm), lambda j, i: (0, i)),
                                       pl.BlockSpec((t, bn), lambda j, i: (0, j))],
        out_specs=pl.BlockSpec((1, tm, bn), lambda j, i: (j, i, 0)),
        out_shape=jax.ShapeDtypeStruct((n_blocks, m, bn), BF16),
        compiler_params=_params(dimension_semantics=("arbitrary", "arbitrary")),
    )(*deps, a, b)


def _mix_proj(x, g, w_ag, w_qkv, w_f):
    t = x.shape[0]
    tm = TOKEN_ROWS

    def body(x_ref, g_ref, wag_ref, wqkv_ref, wf_ref, h_ref, ag_ref, qkv_ref, fl_ref):
        xv = x_ref[...]
        hb = (xv * _rms_stats(xv) * g_ref[...]).astype(BF16)
        h_ref[...] = hb
        ag_ref[...] = _dot_nt(hb, wag_ref[...])
        qkv_ref[...] = _dot_nt(hb, wqkv_ref[...]).astype(BF16)
        fl_ref[...] = _dot_nt(hb, wf_ref[...])

    return pl.pallas_call(
        body, name="mix_proj", grid=(t // tm,),
        in_specs=[_row_spec(tm, D_MODEL), _full_spec((1, D_MODEL)), _full_spec(w_ag.shape),
                  _full_spec(w_qkv.shape), _full_spec(w_f.shape)],
        out_specs=[_row_spec(tm, D_MODEL), _row_spec(tm, 2 * D_CONV), _row_spec(tm, 3 * D_ATTN),
                   _row_spec(tm, LANES)],
        out_shape=[jax.ShapeDtypeStruct((t, D_MODEL), BF16), jax.ShapeDtypeStruct((t, 2 * D_CONV), F32),
                   jax.ShapeDtypeStruct((t, 3 * D_ATTN), BF16), jax.ShapeDtypeStruct((t, LANES), F32)],
        compiler_params=_params(dimension_semantics=("arbitrary",)),
    )(x, g, w_ag, w_qkv, w_f)


def _mix_proj_bwd(dproj, dx2, x1, g, w_ag, w_qkv, w_f):
    t = x1.shape[0]
    tm = TOKEN_ROWS
    n_ag, n_qkv = 2 * D_CONV, 3 * D_ATTN

    def body(dp_ref, dx2_ref, x_ref, g_ref, wag_ref, wqkv_ref, wf_ref, dx_ref, dg_ref):
        @pl.when(pl.program_id(0) == 0)
        def _():
            dg_ref[...] = jnp.zeros_like(dg_ref)

        dh = (_dot(dp_ref[:, 0:n_ag], wag_ref[...]) + _dot(dp_ref[:, n_ag:n_ag + n_qkv], wqkv_ref[...])
              + _dot(dp_ref[:, n_ag + n_qkv:], wf_ref[...]))
        xv = x_ref[...]
        dxn, dg_rows = _rms_bwd(xv, _rms_stats(xv), g_ref[...], dh)
        dx_ref[...] = dx2_ref[...] + dxn
        dg_ref[...] += jnp.sum(dg_rows, axis=0, keepdims=True)

    return pl.pallas_call(
        body, name="mix_proj_bwd", grid=(t // tm,),
        in_specs=[_row_spec(tm, dproj.shape[1]),
                  _row_spec(tm, D_MODEL), _row_spec(tm, D_MODEL), _full_spec((1, D_MODEL)),
                  _full_spec(w_ag.shape), _full_spec(w_qkv.shape), _full_spec(w_f.shape)],
        out_specs=[_row_spec(tm, D_MODEL), _full_spec((1, D_MODEL))],
        out_shape=[jax.ShapeDtypeStruct((t, D_MODEL), F32), jax.ShapeDtypeStruct((1, D_MODEL), F32)],
        compiler_params=_params(dimension_semantics=("arbitrary",)),
    )(dproj, dx2, x1, g, w_ag, w_qkv, w_f)


def _split3(x):
    hi = x.astype(BF16)
    r1 = x - hi.astype(F32)
    mid = r1.astype(BF16)
    lo = (r1 - mid.astype(F32)).astype(BF16)
    return hi, mid, lo


def _gates_fwd(flt, fb):
    t = flt.shape[1]

    def body(f_ref, b_ref, d_ref):
        z = f_ref[...] + b_ref[...]
        logf = jnp.minimum(z, 0.0) - jnp.log(1.0 + jnp.exp(-jnp.abs(z)))
        row = lax.broadcasted_iota(jnp.int32, (LANES, LANES), 0)
        col = lax.broadcasted_iota(jnp.int32, (LANES, LANES), 1)
        upper = (row <= col).astype(BF16)
        carry = jnp.zeros((HEAD_ROWS, 1), F32)
        for blk in range(t // LANES):
            hi, mid, lo = _split3(logf[:, blk * LANES:(blk + 1) * LANES])
            cs = _dot(hi, upper) + _dot(mid, upper) + _dot(lo, upper)
            d_ref[:, blk * LANES:(blk + 1) * LANES] = cs + carry
            carry = carry + cs[:, LANES - 1:LANES]

    return pl.pallas_call(
        body, name="gates_fwd", out_shape=jax.ShapeDtypeStruct((HEAD_ROWS, t), F32),
        compiler_params=_params(),
    )(flt, fb)


def _gates_bwd(dd, flt, fb):
    t = flt.shape[1]

    def body(dd_ref, f_ref, b_ref, df_ref, db_ref):
        z = f_ref[...] + b_ref[...]
        row = lax.broadcasted_iota(jnp.int32, (LANES, LANES), 0)
        col = lax.broadcasted_iota(jnp.int32, (LANES, LANES), 1)
        lower = (row >= col).astype(BF16)
        carry = jnp.zeros((HEAD_ROWS, 1), F32)
        db = jnp.zeros((HEAD_ROWS, 1), F32)
        for blk in reversed(range(t // LANES)):
            sl = slice(blk * LANES, (blk + 1) * LANES)
            hi, mid, lo = _split3(dd_ref[:, sl])
            cs = _dot(hi, lower) + _dot(mid, lower) + _dot(lo, lower)
            dz = (cs + carry) * _sigmoid(-z[:, sl])
            df_ref[:, sl] = dz
            db = db + jnp.sum(dz, axis=1, keepdims=True)
            carry = carry + cs[:, 0:1]
        db_ref[...] = db

    return pl.pallas_call(
        body, name="gates_bwd",
        out_shape=[jax.ShapeDtypeStruct((HEAD_ROWS, t), F32), jax.ShapeDtypeStruct((HEAD_ROWS, 1), F32)],
        compiler_params=_params(),
    )(dd, flt, fb)


CONV_CHUNK = 128
CONV_TAIL = 16
CONV_WINDOW = CONV_CHUNK + CONV_PAD + 8
CONV_ROWS_EXTRA = CONV_PAD + CONV_TAIL
SUBLANES = 8


def _conv_rows(ag_ref, u_ref, t):
    u_ref[0:CONV_PAD, :] = jnp.zeros((CONV_PAD, D_CONV), F32)
    u_ref[CONV_PAD + t:CONV_ROWS_EXTRA + t, :] = jnp.zeros((CONV_TAIL, D_CONV), F32)

    def fill(i, c):
        r0 = pl.multiple_of(i * CONV_CHUNK, CONV_CHUNK)
        a = ag_ref[pl.ds(r0, CONV_CHUNK), 0:D_CONV]
        gt = ag_ref[pl.ds(r0, CONV_CHUNK), D_CONV:2 * D_CONV]
        u_ref[pl.ds(CONV_PAD + r0, CONV_CHUNK), :] = a * _sigmoid(gt)
        return c

    lax.fori_loop(0, t // CONV_CHUNK, fill, 0)


def _for_shifted(ref, r0, offsets, fn):
    window = ref[pl.ds(r0, CONV_WINDOW), :]
    for rem in range(SUBLANES):
        mine = [o for o in offsets if o % SUBLANES == rem]
        if not mine:
            continue
        turned = window if rem == 0 else pltpu.roll(window, CONV_WINDOW - rem, 0)
        for o in mine:
            fn(o, turned[o - rem:o - rem + CONV_CHUNK])


def _conv_taps(u_ref, r0, w_ref, cb):
    acc = [jnp.zeros((CONV_CHUNK, D_CONV), F32)]

    def tap(o, rows):
        j = o - (CONV_PAD - CONV_WIDTH + 1)
        acc[0] = acc[0] + w_ref[j:j + 1, :] * rows

    _for_shifted(u_ref, r0, [j + CONV_PAD - CONV_WIDTH + 1 for j in range(CONV_WIDTH)], tap)
    return acc[0] + cb


def _conv_point(y, lg, lb):
    mu = jnp.mean(y, axis=-1, keepdims=True)
    yc = y - mu
    rstd = lax.rsqrt(jnp.mean(yc * yc, axis=-1, keepdims=True) + EPS)
    yhat = yc * rstd
    z = yhat * lg + lb
    sg = _sigmoid(z)
    s = z * sg
    rr = _rms_stats(s)
    return yhat, rstd, z, sg, s, rr


def _conv_fwd(ag, conv_w, conv_b, ln_g, ln_b, norm_g):
    t = ag.shape[0]

    def body(ag_ref, w_ref, cb_ref, lg_ref, lb_ref, ng_ref, o_ref, y_ref, u_ref):
        _conv_rows(ag_ref, u_ref, t)
        cb, lg, lb, ng = cb_ref[...], lg_ref[...], lb_ref[...], ng_ref[...]

        def chunk(i, c):
            r0 = pl.multiple_of(i * CONV_CHUNK, CONV_CHUNK)
            y = _conv_taps(u_ref, r0, w_ref, cb)
            y_ref[pl.ds(r0, CONV_CHUNK), :] = y
            _, _, _, _, s, rr = _conv_point(y, lg, lb)
            o_ref[pl.ds(r0, CONV_CHUNK), :] = (s * rr * ng).astype(BF16)
            return c

        lax.fori_loop(0, t // CONV_CHUNK, chunk, 0)

    return pl.pallas_call(
        body, name="conv_fwd",
        out_shape=[jax.ShapeDtypeStruct((t, D_CONV), BF16), jax.ShapeDtypeStruct((t, D_CONV), F32)],
        scratch_shapes=[pltpu.VMEM((t + CONV_ROWS_EXTRA, D_CONV), F32)],
        compiler_params=_params(),
    )(ag, conv_w, conv_b, ln_g, ln_b, norm_g)


def _conv_bwd(ag, y, dout, conv_w, ln_g, ln_b, norm_g):
    t = ag.shape[0]

    def body(ag_ref, y_ref, do_ref, w_ref, lg_ref, lb_ref, ng_ref,
             dag_ref, dw_ref, dcb_ref, dlg_ref, dlb_ref, dng_ref, u_ref, dy_ref):
        _conv_rows(ag_ref, u_ref, t)
        dy_ref[t:t + CONV_ROWS_EXTRA, :] = jnp.zeros((CONV_ROWS_EXTRA, D_CONV), F32)
        lg, lb, ng = lg_ref[...], lb_ref[...], ng_ref[...]
        dw_ref[...] = jnp.zeros_like(dw_ref)
        zero = jnp.zeros((1, D_CONV), F32)

        def chunk(i, carry):
            dcb, dlg, dlb, dng = carry
            r0 = pl.multiple_of(i * CONV_CHUNK, CONV_CHUNK)
            yhat, rstd, z, sg, s, rr = _conv_point(y_ref[pl.ds(r0, CONV_CHUNK), :], lg, lb)
            do = do_ref[pl.ds(r0, CONV_CHUNK), :]
            ds, dng_rows = _rms_bwd(s, rr, ng, do)
            dz = ds * _silu_grad(z, sg)
            dyhat = dz * lg
            dy = rstd * (dyhat - jnp.mean(dyhat, axis=-1, keepdims=True)
                         - yhat * jnp.mean(dyhat * yhat, axis=-1, keepdims=True))
            dy_ref[pl.ds(r0, CONV_CHUNK), :] = dy
            def tap(o, rows):
                j = o - (CONV_PAD - CONV_WIDTH + 1)
                dw_ref[j:j + 1, :] += jnp.sum(dy * rows, axis=0, keepdims=True)

            _for_shifted(u_ref, r0, [j + CONV_PAD - CONV_WIDTH + 1 for j in range(CONV_WIDTH)], tap)
            return (dcb + jnp.sum(dy, axis=0, keepdims=True), dlg + jnp.sum(dz * yhat, axis=0, keepdims=True),
                    dlb + jnp.sum(dz, axis=0, keepdims=True), dng + jnp.sum(dng_rows, axis=0, keepdims=True))

        dcb, dlg, dlb, dng = lax.fori_loop(0, t // CONV_CHUNK, chunk, (zero, zero, zero, zero))
        dcb_ref[...] = dcb
        dlg_ref[...] = dlg
        dlb_ref[...] = dlb
        dng_ref[...] = dng

        def chunk2(i, c):
            r0 = pl.multiple_of(i * CONV_CHUNK, CONV_CHUNK)
            acc = [jnp.zeros((CONV_CHUNK, D_CONV), F32)]

            def tap(o, rows):
                j = CONV_WIDTH - 1 - o
                acc[0] = acc[0] + w_ref[j:j + 1, :] * rows

            _for_shifted(dy_ref, r0, list(range(CONV_WIDTH)), tap)
            du = acc[0]
            a = ag_ref[pl.ds(r0, CONV_CHUNK), 0:D_CONV]
            gt = ag_ref[pl.ds(r0, CONV_CHUNK), D_CONV:2 * D_CONV]
            sg = _sigmoid(gt)
            dag_ref[pl.ds(r0, CONV_CHUNK), 0:D_CONV] = (du * sg).astype(BF16)
            dag_ref[pl.ds(r0, CONV_CHUNK), D_CONV:2 * D_CONV] = (du * a * sg * (1.0 - sg)).astype(BF16)
            return c

        lax.fori_loop(0, t // CONV_CHUNK, chunk2, 0)

    vec = jax.ShapeDtypeStruct((1, D_CONV), F32)
    return pl.pallas_call(
        body, name="conv_bwd",
        out_shape=[jax.ShapeDtypeStruct((t, 2 * D_CONV), BF16), jax.ShapeDtypeStruct((CONV_PAD, D_CONV), F32),
                   vec, vec, vec, vec],
        scratch_shapes=[pltpu.VMEM((t + CONV_ROWS_EXTRA, D_CONV), F32), pltpu.VMEM((t + CONV_ROWS_EXTRA, D_CONV), F32)],
        compiler_params=_params(),
    )(ag, y, dout, conv_w, ln_g, ln_b, norm_g)


Q_ROWS = 256
ATTN_SCALE = HEAD_DIM ** -0.5
ATTN_AHEAD = 1


def _attn_specs(t):
    blk = lambda off: pl.BlockSpec((t, LANES), lambda p: (0, off + p))
    pairs = N_HEADS // 2
    return [blk(0), blk(pairs), blk(2 * pairs), pl.BlockSpec((2, 1, t), lambda p: (p, 0, 0))]


def _one_head(q2, mask):
    return jnp.where(mask, q2, jnp.zeros_like(q2)) * ATTN_SCALE


def _attn_scores(qs, k2, drow, r0, q1):
    s = _dot_nt(qs, k2) - drow
    rowi = lax.broadcasted_iota(jnp.int32, (q1 - r0, q1 - r0), 0)
    coli = lax.broadcasted_iota(jnp.int32, (q1 - r0, q1 - r0), 1)
    diag = jnp.where(coli <= rowi, s[:, r0:q1], -jnp.inf)
    return diag if r0 == 0 else jnp.concatenate([s[:, :r0], diag], axis=1)


def _attn_fwd(qkv, drow, deps=()):
    t = qkv.shape[0]
    deps = tuple(deps)

    def body(q_ref, k_ref, v_ref, dr_ref, o_ref, lse_ref):
        head_a = lax.broadcasted_iota(jnp.int32, (1, LANES), 1) < HEAD_DIM
        items = [(qb, hh) for qb in range(t // Q_ROWS) for hh in range(2)]

        def scores(item):
            qb, hh = item
            r0, q1 = qb * Q_ROWS, (qb + 1) * Q_ROWS
            qs = _one_head(q_ref[r0:q1, :], head_a if hh == 0 else ~head_a)
            return _attn_scores(qs, k_ref[0:q1, :], dr_ref[hh, :, 0:q1], r0, q1)

        ahead = [scores(item) for item in items[:ATTN_AHEAD]]
        outs = []
        for n, (qb, hh) in enumerate(items):
            r0, q1 = qb * Q_ROWS, (qb + 1) * Q_ROWS
            s = ahead.pop(0)
            if n + ATTN_AHEAD < len(items):
                ahead.append(scores(items[n + ATTN_AHEAD]))
            mx = jnp.max(s, axis=1, keepdims=True)
            p = jnp.exp(s - mx)
            l = jnp.sum(p, axis=1, keepdims=True)
            lse_ref[hh, r0:q1, :] = mx + jnp.log(l)
            outs.append(_dot(p.astype(BF16), v_ref[0:q1, :]) * (1.0 / l))
            if hh == 1:
                o_ref[r0:q1, :] = jnp.where(head_a, outs[0], outs[1])
                outs = []

    pairs = N_HEADS // 2
    return pl.pallas_call(
        _skip(len(deps), body), name="attn_fwd", grid=(pairs,), in_specs=[_ANY] * len(deps) + _attn_specs(t),
        out_specs=[pl.BlockSpec((t, LANES), lambda p: (0, p)), pl.BlockSpec((2, t, 1), lambda p: (p, 0, 0))],
        out_shape=[jax.ShapeDtypeStruct((t, D_ATTN), F32), jax.ShapeDtypeStruct((N_HEADS, t, 1), F32)],
        compiler_params=_params(dimension_semantics=("arbitrary",)),
    )(*deps, qkv, qkv, qkv, drow)


def _attn_bwd(qkv, drow, lse, do):
    t = qkv.shape[0]

    def body(q_ref, k_ref, v_ref, dr_ref, lse_ref, do_ref,
             dq_ref, dk_ref, dv_ref, dd_ref, dk_acc, dv_acc):
        head_a = lax.broadcasted_iota(jnp.int32, (1, LANES), 1) < HEAD_DIM
        dk_acc[...] = jnp.zeros_like(dk_acc)
        dv_acc[...] = jnp.zeros_like(dv_acc)
        dd_ref[...] = jnp.zeros_like(dd_ref)
        items = [(qb, hh) for qb in range(t // Q_ROWS) for hh in range(2)]

        def products(item):
            qb, hh = item
            r0, q1 = qb * Q_ROWS, (qb + 1) * Q_ROWS
            mask = head_a if hh == 0 else ~head_a
            qs = _one_head(q_ref[r0:q1, :], mask)
            dob = jnp.where(mask, do_ref[r0:q1, :], 0.0).astype(BF16)
            s = _attn_scores(qs, k_ref[0:q1, :], dr_ref[hh, :, 0:q1], r0, q1)
            return qs, dob, s, _dot_nt(dob, v_ref[0:q1, :])

        ahead = products(items[0])
        dqs = []
        for n, (qb, hh) in enumerate(items):
            r0, q1 = qb * Q_ROWS, (qb + 1) * Q_ROWS
            qs, dob, s, dp = ahead
            if n + 1 < len(items):
                ahead = products(items[n + 1])
            p = jnp.exp(s - lse_ref[hh, r0:q1, :])
            ds = p * (dp - jnp.sum(p * dp, axis=1, keepdims=True))
            dsb = ds.astype(BF16)
            dqs.append(_dot(dsb, k_ref[0:q1, :]) * ATTN_SCALE)
            dk_acc[0:q1, :] += _dot_tn(dsb, qs)
            dv_acc[0:q1, :] += _dot_tn(p.astype(BF16), dob)
            dd_ref[hh, :, 0:q1] -= jnp.sum(ds, axis=0, keepdims=True)
            if hh == 1:
                dq_ref[r0:q1, :] = jnp.where(head_a, dqs[0], dqs[1]).astype(BF16)
                dqs = []
        dk_ref[...] = dk_acc[...].astype(BF16)
        dv_ref[...] = dv_acc[...].astype(BF16)

    pairs = N_HEADS // 2
    col = pl.BlockSpec((t, LANES), lambda p: (0, p))
    grad = jax.ShapeDtypeStruct((t, D_ATTN), BF16)
    return pl.pallas_call(
        body, name="attn_bwd", grid=(pairs,),
        in_specs=_attn_specs(t) + [pl.BlockSpec((2, t, 1), lambda p: (p, 0, 0)), col],
        out_specs=[col, col, col, pl.BlockSpec((2, 1, t), lambda p: (p, 0, 0))],
        out_shape=[grad, grad, grad, jax.ShapeDtypeStruct((N_HEADS, 1, t), F32)],
        scratch_shapes=[pltpu.VMEM((t, LANES), F32), pltpu.VMEM((t, LANES), F32)],
        compiler_params=_params(dimension_semantics=("arbitrary",)),
    )(qkv, qkv, qkv, drow, lse, do)


def _out_proj(ycn, o, g_attn, w_out, x1, deps=()):
    t = x1.shape[0]
    tm = TOKEN_ROWS
    deps = tuple(deps)

    def body(yc_ref, o_ref, g_ref, w_ref, x_ref, xo_ref, ya_ref):
        ov = o_ref[...]
        ya = (ov * _rms_stats(ov) * g_ref[...]).astype(BF16)
        ya_ref[...] = ya
        xo_ref[...] = x_ref[...] + _dot(yc_ref[...], w_ref[0:D_CONV, :]) + _dot(ya, w_ref[D_CONV:, :])

    return pl.pallas_call(
        _skip(len(deps), body), name="out_proj", grid=(t // tm,),
        in_specs=[_ANY] * len(deps) + [_row_spec(tm, D_CONV), _row_spec(tm, D_ATTN), _full_spec((1, D_ATTN)),
                                       _full_spec(w_out.shape), _row_spec(tm, D_MODEL)],
        out_specs=[_row_spec(tm, D_MODEL), _row_spec(tm, D_ATTN)],
        out_shape=[jax.ShapeDtypeStruct((t, D_MODEL), F32), jax.ShapeDtypeStruct((t, D_ATTN), BF16)],
        compiler_params=_params(dimension_semantics=("arbitrary",)),
    )(*deps, ycn, o, g_attn, w_out, x1)


def _out_proj_bwd(dx2, o, g_attn, w_out, deps=()):
    t = dx2.shape[0]
    tm = TOKEN_ROWS
    deps = tuple(deps)

    def body(dx_ref, o_ref, g_ref, w_ref, dyc_ref, do_ref, dg_ref):
        @pl.when(pl.program_id(0) == 0)
        def _():
            dg_ref[...] = jnp.zeros_like(dg_ref)

        dxb = dx_ref[...]
        dyc_ref[...] = _dot_nt(dxb, w_ref[0:D_CONV, :])
        dya = _dot_nt(dxb, w_ref[D_CONV:, :])
        ov = o_ref[...]
        do, dg_rows = _rms_bwd(ov, _rms_stats(ov), g_ref[...], dya)
        do_ref[...] = do
        dg_ref[...] += jnp.sum(dg_rows, axis=0, keepdims=True)

    return pl.pallas_call(
        _skip(len(deps), body), name="out_proj_bwd", grid=(t // tm,),
        in_specs=[_ANY] * len(deps) + [_row_spec(tm, D_MODEL), _row_spec(tm, D_ATTN), _full_spec((1, D_ATTN)),
                                       _full_spec(w_out.shape)],
        out_specs=[_row_spec(tm, D_CONV), _row_spec(tm, D_ATTN), _full_spec((1, D_ATTN))],
        out_shape=[jax.ShapeDtypeStruct((t, D_CONV), F32), jax.ShapeDtypeStruct((t, D_ATTN), F32),
                   jax.ShapeDtypeStruct((1, D_ATTN), F32)],
        compiler_params=_params(dimension_semantics=("arbitrary",)),
    )(*deps, dx2, o, g_attn, w_out)


def _split_w_in(w_in_t):
    w_ag = w_in_t[:2 * D_CONV]
    w_qkv = w_in_t[2 * D_CONV:2 * D_CONV + 3 * D_ATTN]
    w_f = jnp.pad(w_in_t[2 * D_CONV + 3 * D_ATTN:], ((0, LANES - N_HEADS), (0, 0)))
    return w_ag, w_qkv, w_f


def _head_rows(v):
    return jnp.pad(v, ((0, HEAD_ROWS - N_HEADS),) + ((0, 0),) * (v.ndim - 1))


def _local_step(x, target, p, get_weights, put_grads, flush_grads):
    t = x.shape[0]
    fb = _head_rows(p["forget_b"].reshape(N_HEADS, 1))

    w, deps = get_weights("ffn1_w13", None)
    h1, gu1, act1 = _ffn_up(x, p["ffn1_norm"], w["ffn1_w13"], "ffn1_up", deps)
    w2, _ = get_weights("ffn1_w2", act1)
    w.update(w2)
    x1 = _ffn_down(x, act1, w["ffn1_w2"], "ffn1_down")
    wm, _ = get_weights("mix", x1)
    w.update(wm)
    w_ag, w_qkv, w_f = _split_w_in(w["w_in"])
    conv_w = jnp.pad(w["conv_w"], ((0, CONV_PAD - CONV_WIDTH), (0, 0)))
    h2, ag, qkv, fl = _mix_proj(x1, p["mix_norm"], w_ag, w_qkv, w_f)
    flt = _head_rows(fl[:, :N_HEADS].T)
    dcum = _gates_fwd(flt, fb)[:N_HEADS]
    drow = dcum.reshape(N_HEADS, 1, t)
    ycn, y_conv = _conv_fwd(ag, conv_w, p["conv_b"], p["conv_ln_g"], p["conv_ln_b"], p["out_norm_conv"])
    o, lse = _attn_fwd(qkv, drow, [ycn])
    _, deps = get_weights("ffn2:landed", o)
    x2, yan = _out_proj(ycn, o, p["out_norm_attn"], w["w_out"], x1, deps)
    w2, _ = get_weights("ffn2", x2)
    w.update(w2)
    dx3, h3, gu2, act2, loss, d_final = _ffn_fwd(x2, p["ffn2_norm"], w["ffn2_w13"], w["ffn2_w2"], "ffn2_fwd",
                                                 head=(target, p["final_norm"]))

    g = {}
    dx2, dgu2, g["ffn2_norm"], dx3_half, dx2_bf16 = _ffn_bwd(
        dx3, x2, gu2, p["ffn2_norm"], w["ffn2_w13"], w["ffn2_w2"], "ffn2_bwd")
    dw13 = _wgrad(h3, dgu2, N_CHIPS, "ffn2_dw13")
    dw2 = _wgrad(act2, dx3_half, 1, "ffn2_dw2").reshape(D_FF, D_MODEL)
    deps = put_grads("ffn2", {"ffn2_w13": dw13, "ffn2_w2": dw2})
    dyc, do, g["out_norm_attn"] = _out_proj_bwd(dx2_bf16, o, p["out_norm_attn"], w["w_out"], deps)
    deps = flush_grads("ffn2", [dyc])
    dw_out = _wgrad(jnp.concatenate([ycn, yan], axis=1), dx2_bf16, 1, "dw_out", deps).reshape(D_MODEL, D_MODEL)
    dq, dk, dv, ddrow = _attn_bwd(qkv, drow, lse, do)
    dflt, dfb = _gates_bwd(_head_rows(ddrow.reshape(N_HEADS, t)), flt, fb)
    g["forget_b"] = dfb[:N_HEADS, 0].reshape(1, N_HEADS)
    dfl = jnp.pad(dflt[:N_HEADS].T, ((0, 0), (0, LANES - N_HEADS)))
    dag, dconv_w, g["conv_b"], g["conv_ln_g"], g["conv_ln_b"], g["out_norm_conv"] = _conv_bwd(
        ag, y_conv, dyc, conv_w, p["conv_ln_g"], p["conv_ln_b"], p["out_norm_conv"])
    g["conv_w"] = dconv_w[:CONV_WIDTH]
    dproj = jnp.concatenate([dag, dq, dk, dv, dfl.astype(BF16)], axis=1)
    dx1, g["mix_norm"] = _mix_proj_bwd(dproj, dx2, x1, p["mix_norm"], w_ag, w_qkv, w_f)
    dw_in = _wgrad(dproj, h2, 1, "dw_in").reshape(dproj.shape[1], D_MODEL)[:N_IN]
    deps = put_grads("mix", {"w_in": dw_in, "w_out": dw_out})
    dx0, dgu1, g["ffn1_norm"], dx1_half, _ = _ffn_bwd(
        dx1, x, gu1, p["ffn1_norm"], w["ffn1_w13"], w["ffn1_w2"], "ffn1_bwd", deps)
    g["final_norm"] = d_final
    g["loss"] = loss[:, :1]
    deps = flush_grads("mix", put_grads("small", g))
    dw2 = _wgrad(act1, dx1_half, 1, "ffn1_dw2", deps).reshape(D_FF, D_MODEL)
    deps = flush_grads("ffn1_w2", put_grads("ffn1_w2", {"ffn1_w2": dw2}))
    dw13 = _wgrad(h1, dgu1, N_CHIPS, "ffn1_dw13", deps)
    put_grads("ffn1_w13", {"ffn1_w13": dw13})
    return dx0


MESH = pl.DeviceIdType.MESH


def _place():
    x, y, c = lax.axis_index("x"), lax.axis_index("y"), lax.axis_index("c")
    chips = [(1 - x, y), (x, 1 - y), (1 - x, 1 - y)]
    return x, y, c, chips


def _hbm_out(shape, dtype):
    return jax.ShapeDtypeStruct(shape, dtype)


def _comm_call(body, name, ins, out_shapes, n_remote, in_place=False):
    return pl.pallas_call(
        body, name=name, in_specs=[_ANY] * len(ins), out_specs=[_ANY] * len(out_shapes), out_shape=out_shapes,
        scratch_shapes=[pltpu.SemaphoreType.DMA((n_remote,)), pltpu.SemaphoreType.DMA((n_remote,))],
        input_output_aliases={i: i for i in range(len(ins))} if in_place else {},
    )(*ins)


def _remote(src, dst, sems, n, to):
    send_sems, recv_sems = sems
    return pltpu.make_async_remote_copy(src_ref=src, dst_ref=dst, send_sem=send_sems.at[n], recv_sem=recv_sems.at[n],
                                        device_id=to, device_id_type=MESH)


HALF_ROWS_MULTIPLE = 32


def _halved_by_rows(rows):
    return rows % HALF_ROWS_MULTIPLE == 0


def _half_shape(rows, cols):
    return (rows // 2, cols) if _halved_by_rows(rows) else (rows, cols // 2)


def _half_index(rows, core):
    return (core, 0) if _halved_by_rows(rows) else (0, core)


def _half_of(ref, rows, cols, core, *lead):
    if _halved_by_rows(rows):
        return ref.at[(*lead, pl.ds(core * (rows // 2), rows // 2), slice(None))]
    return ref.at[(*lead, slice(None), pl.ds(core * (cols // 2), cols // 2))]


def _into_slot(shard, chip, dtype, name, deps=()):
    rows, cols = shard.shape
    half = _half_shape(rows, cols)
    by_rows = _halved_by_rows(rows)
    deps = tuple(deps)

    def body(k_ref, *refs):
        s_ref, o_ref = refs[len(deps):]
        o_ref[0] = s_ref[...].astype(dtype)

    return pl.pallas_call(
        body, name=name,
        grid_spec=pltpu.PrefetchScalarGridSpec(
            num_scalar_prefetch=1, grid=(2,),
            in_specs=[_ANY] * len(deps) + [pl.BlockSpec(half, lambda i, k_ref: (i, 0) if by_rows else (0, i))],
            out_specs=pl.BlockSpec((1,) + half, lambda i, k_ref: (k_ref[0], i, 0) if by_rows else (k_ref[0], 0, i))),
        out_shape=jax.ShapeDtypeStruct((N_CHIPS, rows, cols), dtype),
        compiler_params=_params(dimension_semantics=("arbitrary",)),
    )(chip, *deps, shard)


def _run_copies(name, bufs, n_copies, plan):
    n = len(bufs)

    def body(*refs):
        copies = plan(refs[n:2 * n], refs[2 * n:2 * n + 2])
        for send, _ in copies:
            send.start()
        for send, recv in copies:
            send.wait_send()
            recv.wait_recv()

    return _comm_call(body, name, bufs, [_hbm_out(b.shape, b.dtype) for b in bufs], n_copies, in_place=True)


def _forward_halves(slots, name):
    return _run_copies(name, slots, 3 * len(slots), _d2d_forward_plan(slots))


_HBM = pl.BlockSpec(memory_space=pltpu.HBM)
_SEM = pl.BlockSpec(memory_space=pltpu.SEMAPHORE)
_DATAFLOW = pltpu.SideEffectType.DATAFLOW_SIDE_EFFECTING


def _split_copy_start(name, bufs, n_copies, plan):
    n = len(bufs)

    def body(*refs):
        for send, _ in plan(refs[:n], (refs[n], refs[n + 1])):
            send.start()
        token = refs[-1]
        token[...] = jnp.zeros_like(token)

    out = pl.pallas_call(
        body, name=name,
        out_shape=(pltpu.SemaphoreType.DMA((n_copies,)), pltpu.SemaphoreType.DMA((n_copies,)),
                   *[pltpu.HBM(b.shape, b.dtype) for b in bufs], jax.ShapeDtypeStruct((8, LANES), F32)),
        in_specs=[_HBM] * n, out_specs=(_SEM, _SEM, *[_HBM] * n, pl.BlockSpec(memory_space=pltpu.VMEM)),
        input_output_aliases={i: 2 + i for i in range(n)},
        compiler_params=pltpu.CompilerParams(has_side_effects=_DATAFLOW),
    )(*[pltpu.with_memory_space_constraint(b, pltpu.HBM) for b in bufs])
    return out[0], out[1], list(out[2:2 + n]), out[-1]


def _split_copy_wait(name, started, plan, after, passed=()):
    send_sems, recv_sems, bufs, _ = started
    n = len(bufs)
    after = tuple(after)
    bufs = list(bufs) + list(passed)
    total = len(bufs)

    def body(*refs):
        for send, recv in plan(refs[:n], (refs[total], refs[total + 1])):
            send.wait_send()
            recv.wait_recv()

    out = pl.pallas_call(
        body, name=name, out_shape=tuple(pltpu.HBM(b.shape, b.dtype) for b in bufs),
        in_specs=[_HBM] * total + [_SEM, _SEM] + [_ANY] * len(after), out_specs=tuple([_HBM] * total),
        input_output_aliases={i: i for i in range(total)},
        compiler_params=pltpu.CompilerParams(has_side_effects=_DATAFLOW),
    )(*bufs, send_sems, recv_sems, *after)
    return list(out)


def _ici_gather_plan(slots):
    def plan(refs, sems):
        x, y, c, chips = _place()
        me = 2 * x + y
        copies = []
        for i, ref in enumerate(refs):
            for j, chip in enumerate(chips):
                mine = _half_of(ref, *slots[i].shape[1:], c, me)
                theirs = _half_of(ref, *slots[i].shape[1:], c, 2 * chip[0] + chip[1])
                to = (*chip, c)
                copies.append((_remote(mine, mine, sems, 3 * i + j, to), _remote(theirs, theirs, sems, 3 * i + j, to)))
        return copies

    return plan


def _ici_scatter_plan(n):
    def plan(refs, sems):
        x, y, c, chips = _place()
        copies = []
        for i in range(n):
            for j, chip in enumerate(chips):
                cp = _remote(refs[i].at[2 * chip[0] + chip[1]], refs[n + i].at[j], sems, 3 * i + j, (*chip, c))
                copies.append((cp, cp))
        return copies

    return plan


def _d2d_forward_plan(slots):
    def plan(refs, sems):
        x, y, c, chips = _place()
        sibling = (x, y, 1 - c)
        copies = []
        for i, ref in enumerate(refs):
            for j, chip in enumerate(chips):
                src_chip = 2 * chip[0] + chip[1]
                mine = _half_of(ref, *slots[i].shape[1:], c, src_chip)
                theirs = _half_of(ref, *slots[i].shape[1:], 1 - c, src_chip)
                copies.append((_remote(mine, mine, sems, 3 * i + j, sibling),
                               _remote(theirs, theirs, sems, 3 * i + j, sibling)))
        return copies

    return plan


def _pair_exchange_plan(grads):
    n = len(grads)

    def plan(refs, sems):
        x, y, c, _ = _place()
        copies = []
        for i in range(n):
            theirs = _half_of(refs[i], *grads[i].shape[1:], 1 - c, slice(None))
            cp = _remote(theirs, refs[n + i], sems, i, (x, y, 1 - c))
            copies.append((cp, cp))
        return copies

    return plan


def _pair_share_plan(shapes):
    def plan(refs, sems):
        x, y, c, _ = _place()
        sibling = (x, y, 1 - c)
        copies = []
        for i, ref in enumerate(refs):
            mine, theirs = _half_of(ref, *shapes[i], c), _half_of(ref, *shapes[i], 1 - c)
            copies.append((_remote(mine, mine, sems, i, sibling), _remote(theirs, theirs, sems, i, sibling)))
        return copies

    return plan


def _pair_share(halves, name):
    return _run_copies(name, halves, len(halves), _pair_share_plan([h.shape for h in halves]))


N_DEVICES = 8
FLIPS = [(fx, fy, fc) for fx in range(2) for fy in range(2) for fc in range(2)][1:]


def _small_slots(v, me):
    rows = v.shape[0]

    def body(k_ref, v_ref, o_ref):
        o_ref[0] = v_ref[...]

    return pl.pallas_call(
        body, name="small_slot",
        grid_spec=pltpu.PrefetchScalarGridSpec(
            num_scalar_prefetch=1, grid=(1,),
            in_specs=[pl.BlockSpec((rows, LANES), lambda i, k_ref: (0, 0))],
            out_specs=pl.BlockSpec((1, rows, LANES), lambda i, k_ref: (k_ref[0], 0, 0))),
        out_shape=jax.ShapeDtypeStruct((N_DEVICES, rows, LANES), F32),
        compiler_params=_params(dimension_semantics=("arbitrary",)),
    )(me, v)


def _small_plan():
    def plan(refs, sems):
        x, y, c, _ = _place()
        slots = refs[0]
        me = 4 * x + 2 * y + c
        copies = []
        for n, (fx, fy, fc) in enumerate(FLIPS):
            to = (x ^ fx, y ^ fy, c ^ fc)
            src = 4 * to[0] + 2 * to[1] + to[2]
            copies.append((_remote(slots.at[me], slots.at[me], sems, n, to), _remote(slots.at[src], slots.at[src], sems, n, to)))
        return copies

    return plan


def _small_sum(slots):
    def body(s_ref, o_ref):
        acc = s_ref[0]
        for s in range(1, N_DEVICES):
            acc = acc + s_ref[s]
        o_ref[...] = acc

    return pl.pallas_call(body, name="small_sum", out_shape=jax.ShapeDtypeStruct(slots.shape[1:], F32),
                          compiler_params=_params())(slots)


def _pair_add(gs, sibs, core, name):
    n = len(gs)
    halves = [_half_shape(*g.shape[1:]) for g in gs]

    def body(c_ref, *refs):
        for g_ref, s_ref, o_ref in zip(refs[:n], refs[n:2 * n], refs[2 * n:]):
            o_ref[0] = (g_ref[0].astype(F32) + s_ref[0].astype(F32)).astype(BF16)

    def mine(g, half):
        return pl.BlockSpec((1,) + half, lambda s, c_ref: (s, *_half_index(g.shape[1], c_ref[0])))

    whole = [pl.BlockSpec((1,) + half, lambda s, c_ref: (s, 0, 0)) for half in halves]
    return pl.pallas_call(
        body, name=name,
        grid_spec=pltpu.PrefetchScalarGridSpec(
            num_scalar_prefetch=1, grid=(N_CHIPS,),
            in_specs=[mine(g, half) for g, half in zip(gs, halves)] + whole, out_specs=whole),
        out_shape=[jax.ShapeDtypeStruct((N_CHIPS,) + half, BF16) for half in halves],
        compiler_params=_params(dimension_semantics=("arbitrary",)),
    )(core, *gs, *sibs)


CHIP_ADD_STEPS = 2


def _chip_add(parts, recvs, chip_core, shapes, name):
    n = len(parts)
    by_rows = [_halved_by_rows(shape[0]) for shape in shapes]
    pieces = [(h[0] // CHIP_ADD_STEPS, h[1]) if rows else (h[0], h[1] // CHIP_ADD_STEPS)
              for h, rows in zip((_half_shape(*shape) for shape in shapes), by_rows)]

    def body(kc_ref, *refs):
        for p_ref, r_ref, o_ref in zip(refs[:n], refs[n:2 * n], refs[2 * n:]):
            acc = p_ref[0].astype(F32)
            for j in range(N_CHIPS - 1):
                acc = acc + r_ref[j].astype(F32)
            o_ref[...] = acc

    def at(rows, lead, piece_of):
        return lambda s, kc_ref: (*lead(kc_ref), piece_of(s, kc_ref), 0) if rows else (*lead(kc_ref), 0, piece_of(s, kc_ref))

    mine = [pl.BlockSpec((1,) + p, at(rows, lambda kc_ref: (kc_ref[0],), lambda s, kc_ref: s)) for p, rows in zip(pieces, by_rows)]
    theirs = [pl.BlockSpec((N_CHIPS - 1,) + p, at(rows, lambda kc_ref: (0,), lambda s, kc_ref: s)) for p, rows in zip(pieces, by_rows)]
    out = [pl.BlockSpec(p, at(rows, lambda kc_ref: (), lambda s, kc_ref: kc_ref[1] * CHIP_ADD_STEPS + s))
           for p, rows in zip(pieces, by_rows)]

    return pl.pallas_call(
        body, name=name,
        grid_spec=pltpu.PrefetchScalarGridSpec(
            num_scalar_prefetch=1, grid=(CHIP_ADD_STEPS,), in_specs=mine + theirs, out_specs=out),
        out_shape=[jax.ShapeDtypeStruct(tuple(shape), F32) for shape in shapes],
        compiler_params=_params(dimension_semantics=("arbitrary",)),
    )(chip_core, *parts, *recvs)


def _adamw_math(w, g, m, v):
    m = ADAM_B1 * m + (1.0 - ADAM_B1) * g
    v = ADAM_B2 * v + (1.0 - ADAM_B2) * (g * g)
    m_hat = m / (1.0 - ADAM_B1 ** ADAM_STEP)
    v_hat = v / (1.0 - ADAM_B2 ** ADAM_STEP)
    delta = -ADAM_LR * (m_hat / (jnp.sqrt(v_hat) + ADAM_EPS) + ADAM_WD * w)
    return delta, m, v


ADAM_PARTS = 2


def _adamw_matrix(w, g, m, v, name):
    rows, cols = w.shape
    by_rows = rows % (8 * ADAM_PARTS) == 0
    block = (rows // ADAM_PARTS, cols) if by_rows else (rows, cols // ADAM_PARTS)

    def body(w_ref, g_ref, m_ref, v_ref, go_ref, d_ref, mo_ref, vo_ref):
        gv = g_ref[...]
        go_ref[...] = gv
        d_ref[...], mo_ref[...], vo_ref[...] = _adamw_math(w_ref[...], gv, m_ref[...], v_ref[...])

    spec = pl.BlockSpec(block, lambda i: (i, 0) if by_rows else (0, i))
    shape = jax.ShapeDtypeStruct((rows, cols), F32)
    return pl.pallas_call(
        body, name=name, grid=(ADAM_PARTS,), in_specs=[spec] * 4, out_specs=[spec] * 4, out_shape=[shape] * 4,
        compiler_params=_params(dimension_semantics=("arbitrary",)),
    )(w, g, m, v)


def _adamw_small(ws, gs, ms, vs):
    n = len(ws)

    def body(*refs):
        for i in range(n):
            w_ref, g_ref, m_ref, v_ref = (refs[k * n + i] for k in range(4))
            d_ref, mo_ref, vo_ref = (refs[(4 + k) * n + i] for k in range(3))
            d_ref[...], mo_ref[...], vo_ref[...] = _adamw_math(w_ref[...], g_ref[...], m_ref[...], v_ref[...])

    shapes = [jax.ShapeDtypeStruct(w.shape, F32) for w in ws]
    out = pl.pallas_call(body, name="adamw_small", out_shape=shapes * 3, compiler_params=_params())(*ws, *gs, *ms, *vs)
    return out[:n], out[n:2 * n], out[2 * n:]


MATRICES = ["ffn1_w13", "ffn1_w2", "w_in", "w_out", "ffn2_w13", "ffn2_w2"]
VECTORS = ["ffn1_norm", "mix_norm", "conv_b", "conv_ln_g", "conv_ln_b", "forget_b", "out_norm_conv",
           "out_norm_attn", "ffn2_norm", "final_norm"]
WEIGHTS = ["ffn1_norm", "ffn1_w13", "ffn1_w2", "mix_norm", "w_in", "conv_w", "conv_b", "conv_ln_g", "conv_ln_b",
           "forget_b", "out_norm_conv", "out_norm_attn", "w_out", "ffn2_norm", "ffn2_w13", "ffn2_w2", "final_norm"]


def _pack_small(g, names):
    rows, layout = [], []
    for n in names:
        flat = g[n].reshape(-1)
        pad = (-flat.shape[0]) % LANES
        rows.append(jnp.pad(flat, (0, pad)).reshape(-1, LANES))
        layout.append((n, g[n].shape, flat.shape[0], rows[-1].shape[0]))
    packed = jnp.concatenate(rows, axis=0)
    pad_rows = (-packed.shape[0]) % 8
    return jnp.pad(packed, ((0, pad_rows), (0, 0))), layout


def _unpack_small(packed, layout):
    out, r = {}, 0
    for n, shape, size, nrows in layout:
        out[n] = packed[r:r + nrows].reshape(-1)[:size].reshape(shape)
        r += nrows
    return out


def kernel(x, ffn1_norm, ffn1_w13, ffn1_w2, mix_norm, w_in, conv_w, conv_b, conv_ln_g, conv_ln_b, forget_b, out_norm_conv, out_norm_attn, w_out, ffn2_norm, ffn2_w13, ffn2_w2, final_norm, loss_target, m_ffn1_norm, m_ffn1_w13, m_ffn1_w2, m_mix_norm, m_w_in, m_conv_w, m_conv_b, m_conv_ln_g, m_conv_ln_b, m_forget_b, m_out_norm_conv, m_out_norm_attn, m_w_out, m_ffn2_norm, m_ffn2_w13, m_ffn2_w2, m_final_norm, v_ffn1_norm, v_ffn1_w13, v_ffn1_w2, v_mix_norm, v_w_in, v_conv_w, v_conv_b, v_conv_ln_g, v_conv_ln_b, v_forget_b, v_out_norm_conv, v_out_norm_attn, v_w_out, v_ffn2_norm, v_ffn2_w13, v_ffn2_w2, v_final_norm):
    args = dict(locals())
    weights = {n: args[n] for n in WEIGHTS}
    core = lax.axis_index("c").astype(jnp.int32).reshape(1)
    chip = (2 * lax.axis_index("x") + lax.axis_index("y")).astype(jnp.int32)
    chip1 = chip.reshape(1)
    chip_core = jnp.concatenate([chip1, core])

    def held(n, a):
        return a[0].T if n == "w_in" else a[0]

    def given(n, a):
        return (a.T if n == "w_in" else a)[None]

    def slot(n, deps=()):
        if n == "conv_w":
            rows = jnp.pad(conv_w[0], ((0, CONV_PAD - CONV_WIDTH), (0, 0)))
            return _into_slot(rows, chip1, F32, "slot_conv_w", deps)
        return _into_slot(held(n, weights[n]), chip1, BF16, "slot_" + n, deps)

    fetched = {"ffn1_w13": ["ffn1_w13"], "ffn1_w2": ["ffn1_w2"], "mix": ["w_in", "w_out", "conv_w"],
               "ffn2": ["ffn2_w13", "ffn2_w2"]}
    fetch = {}

    def as_weights(group, bufs):
        out = {}
        for n, b in zip(fetched[group], bufs):
            if n.endswith("w13"):
                out[n] = b
            elif n != "conv_w":
                out[n] = b.reshape(N_CHIPS * b.shape[1], b.shape[2])
            else:
                out[n] = b[:, :CONV_WIDTH].transpose(1, 0, 2).reshape(CONV_WIDTH, D_CONV)
        return out

    def get_weights(group, after):
        if group == "ffn1_w13":
            first = [slot("ffn1_w13")]
            plan = _ici_gather_plan(first)
            started = _split_copy_start("gather_ffn1_w13_start", first, 3, plan)
            second = [slot("ffn1_w2", [started[3]])]
            plan2 = _ici_gather_plan(second)
            fetch["ffn1_w2"] = plan2, _split_copy_start("gather_ffn1_w2_start", second, 3, plan2)
            later_names = fetched["mix"] + fetched["ffn2"]
            later = [slot(n, [fetch["ffn1_w2"][1][3]]) for n in later_names]
            landed = _split_copy_wait("gather_ffn1_w13_wait", started, plan, [], passed=later)
            bufs = _forward_halves(landed[:1], "forward_ffn1_w13")
            behind = dict(zip(later_names, landed[1:]))
            for later in ("mix", "ffn2"):
                bufs_later = [behind[n] for n in fetched[later]]
                plan = _ici_gather_plan(bufs_later)
                fetch[later] = plan, _split_copy_start("gather_%s_start" % later, bufs_later, 3 * len(bufs_later), plan)
            return as_weights(group, bufs), [fetch["mix"][1][3], fetch["ffn2"][1][3]]
        plan, started = fetch[group.split(":")[0]]
        if group == "ffn2:landed":
            landed = _split_copy_wait("gather_ffn2_wait", started, plan, [after])
            plan = _d2d_forward_plan(landed)
            fetch["ffn2"] = plan, _split_copy_start("forward_ffn2_start", landed, 3 * len(landed), plan)
            return {}, [fetch["ffn2"][1][3]]
        if group == "ffn2":
            return as_weights(group, _split_copy_wait("forward_ffn2_wait", started, plan, [after])), []
        landed = _split_copy_wait("gather_%s_wait" % group, started, plan, [after])
        return as_weights(group, _forward_halves(landed, "forward_" + group)), []

    def shard_major(n, g):
        return g if n.endswith("w13") else g.reshape(N_CHIPS, g.shape[0] // N_CHIPS, g.shape[1])

    exchange, scatter = {}, {}
    small_names = VECTORS + ["conv_w"]
    small = {}

    def put_grads(group, grads):
        if group == "small":
            packed, layout = _pack_small(grads, small_names + ["loss"])
            me = (4 * lax.axis_index("x") + 2 * lax.axis_index("y") + lax.axis_index("c")).astype(jnp.int32).reshape(1)
            plan = _small_plan()
            exchange[group] = layout, plan, _split_copy_start("small_start", [_small_slots(packed, me)], len(FLIPS), plan)
            return [exchange[group][2][3]]
        names = list(grads)
        local = [shard_major(n, grads[n]) for n in names]
        landing = [lax.empty((N_CHIPS,) + _half_shape(*a.shape[1:]), BF16) for a in local]
        plan = _pair_exchange_plan(local)
        exchange[group] = names, plan, _split_copy_start("exchange_%s_start" % group, local + landing, len(local), plan)
        return [exchange[group][2][3]]

    def flush_grads(group, after):
        names, plan, started = exchange[group]
        done = _split_copy_wait("exchange_%s_wait" % group, started, plan, after)
        local, sib = done[:len(names)], done[len(names):]
        parts = list(_pair_add(local, sib, core, "pair_add_" + group))
        landing = [lax.empty((N_CHIPS - 1,) + q.shape[1:], BF16) for q in parts]
        plan = _ici_scatter_plan(len(parts))
        shapes = [a.shape[1:] for a in local]
        scatter[group] = names, plan, _split_copy_start("scatter_%s_start" % group, parts + landing, 3 * len(parts), plan), shapes
        return [scatter[group][2][3]]

    p = {n: weights[n] for n in VECTORS}
    p["final_norm"] = final_norm.reshape(1, D_MODEL)
    dx = _local_step(x[0], loss_target[0], p, get_weights, put_grads, flush_grads)
    layout, plan, started = exchange["small"]
    slots, = _split_copy_wait("small_wait", started, plan, [exchange["ffn1_w13"][2][3]])
    small.update(_unpack_small(_small_sum(slots), layout))
    loss = small["loss"].reshape(())

    grad = {n: small[n] for n in VECTORS}
    grad["final_norm"] = small["final_norm"].reshape(D_MODEL)
    grad["conv_w"] = lax.dynamic_slice_in_dim(small["conv_w"], chip * (D_CONV // N_CHIPS), D_CONV // N_CHIPS, axis=1)[None]

    delta, new_m, new_v = {}, {}, {}

    def reduce_chips(group, after):
        names, plan, started, shapes = scatter[group]
        done = _split_copy_wait("scatter_%s_wait" % group, started, plan, after)
        parts, landed = done[:len(names)], done[len(names):]
        return list(_chip_add(parts, landed, chip_core, shapes, "chip_add_" + group))

    def update(group, full):
        ends = []
        for n, reduced in zip(scatter[group][0], full):
            go, d, mo, vo = _adamw_matrix(held(n, weights[n]), reduced, held(n, args["m_" + n]), held(n, args["v_" + n]),
                                          "adamw_" + n)
            grad[n], delta[n], new_m[n], new_v[n] = given(n, go), given(n, d), given(n, mo), given(n, vo)
            ends.append(vo)
        return ends

    def share_start(group, halves):
        plan = _pair_share_plan(scatter[group][3])
        return plan, _split_copy_start("share_%s_start" % group, halves, len(halves), plan)

    halves_ffn2 = reduce_chips("ffn2", [exchange["ffn1_w13"][2][3]])
    plan_ffn2, share_ffn2 = share_start("ffn2", halves_ffn2)
    last_scatter = flush_grads("ffn1_w13", [share_ffn2[3]])
    halves_mix = reduce_chips("mix", last_scatter)
    plan_mix, share_mix = share_start("mix", halves_mix)
    done_ffn2 = update("ffn2", _split_copy_wait("share_ffn2_wait", share_ffn2, plan_ffn2, [share_mix[3]]))
    done_mix = update("mix", _split_copy_wait("share_mix_wait", share_mix, plan_mix, done_ffn2))
    as2d = lambda a: a.reshape(-1, a.shape[-1])
    ds, mos, vos = _adamw_small([as2d(weights[n]) for n in small_names], [as2d(grad[n]) for n in small_names],
                                [as2d(args["m_" + n]) for n in small_names], [as2d(args["v_" + n]) for n in small_names])
    for n, d, mo, vo in zip(small_names, ds, mos, vos):
        shape = weights[n].shape
        delta[n], new_m[n], new_v[n] = d.reshape(shape), mo.reshape(shape), vo.reshape(shape)
    behind = done_ffn2 + done_mix + [vos[0]]
    halves_w2 = reduce_chips("ffn1_w2", behind)
    halves_w13 = reduce_chips("ffn1_w13", behind)
    full_w2, full_w13 = _pair_share(halves_w2 + halves_w13, "pair_share_ffn1")
    update("ffn1_w2", [full_w2])
    update("ffn1_w13", [full_w13])

    return (loss, dx[None], *[grad[n] for n in WEIGHTS], *[delta[n] for n in WEIGHTS],
            *[new_m[n] for n in WEIGHTS], *[new_v[n] for n in WEIGHTS])
```
